```python
import jax, jax.numpy as jnp
from jax import lax
import numpy as np

D_MODEL = 1024
BATCH = 32
SEQ = 2048
DEPTH = 2

MLA_HEADS = 8
Q_LORA = 384
KV_LORA = 256
QK_NOPE = 64
QK_ROPE = 32
V_HEAD = 64
ROPE_THETA = 10000.0
Q_BLOCK = 128

SG_GROUPS = 8
SG_DIM = 512
SG_CHUNK = 128

RWKV_HEADS = 8
RWKV_HEAD = 64
RWKV_DIM = RWKV_HEADS * RWKV_HEAD
DECAY_LORA = 64
AAA_LORA = 64
GATE_LORA = 128
N_DIR = 2
GN_EPS = 64e-5

N_BRANCH = 3
BRANCH_DIM = 512
D_FF = -(-8 * D_MODEL // (3 * 256)) * 256
NORM_EPS = 1e-6

MLA_IN = Q_LORA + KV_LORA + QK_ROPE
SG_IN = 2 * SG_DIM
RWKV_IN = 3 * RWKV_DIM + N_DIR * DECAY_LORA + N_DIR * AAA_LORA + GATE_LORA
GATE_IN = N_BRANCH * D_MODEL
N_IN = MLA_IN + SG_IN + RWKV_IN + GATE_IN

kernel_name = "hybrid_mla_gmlp_rwkv7_encoder"


def rmsnorm(x, g):
    xf = x.astype(jnp.float32)
    y = xf * lax.rsqrt(jnp.mean(xf * xf, axis=-1, keepdims=True) + NORM_EPS)
    return (y * g.astype(jnp.float32)).astype(x.dtype)


def layernorm(x, g, b, eps=1e-5):
    xf = x.astype(jnp.float32)
    mu = jnp.mean(xf, axis=-1, keepdims=True)
    var = jnp.mean(jnp.square(xf - mu), axis=-1, keepdims=True)
    y = (xf - mu) * lax.rsqrt(var + eps)
    return (y * g.astype(jnp.float32) + b.astype(jnp.float32)).astype(x.dtype)


def rope_angles(positions):
    inv_freq = 1.0 / (ROPE_THETA ** (jnp.arange(0, QK_ROPE, 2, dtype=jnp.float32) / QK_ROPE))
    ang = positions.astype(jnp.float32)[..., None] * inv_freq
    return jnp.cos(ang), jnp.sin(ang)


def apply_rope(x, cos, sin):
    half = QK_ROPE // 2
    xf = x.astype(jnp.float32)
    x1, x2 = xf[..., :half], xf[..., half:]
    return jnp.concatenate([x1 * cos - x2 * sin, x1 * sin + x2 * cos], axis=-1).astype(x.dtype)


def mla_attention(cq, ckv, positions, q_norm_g, w_uq, kv_norm_g, w_ukv):
    B, S, _ = cq.shape
    q = (rmsnorm(cq, q_norm_g) @ w_uq).reshape(B, S, MLA_HEADS, QK_NOPE + QK_ROPE)
    q_nope, q_rope = q[..., :QK_NOPE], q[..., QK_NOPE:]
    c_kv, k_rope = ckv[..., :KV_LORA], ckv[..., KV_LORA:]
    kv = (rmsnorm(c_kv, kv_norm_g) @ w_ukv).reshape(B, S, MLA_HEADS, QK_NOPE + V_HEAD)
    k_nope, v = kv[..., :QK_NOPE], kv[..., QK_NOPE:]
    cos, sin = rope_angles(positions)
    q_rope = apply_rope(q_rope, cos[:, :, None], sin[:, :, None])
    k_rope = apply_rope(k_rope, cos, sin)
    scale = (QK_NOPE + QK_ROPE) ** -0.5
    nb = S // Q_BLOCK

    def to_blocks(t):
        return jnp.swapaxes(t.reshape((B, nb, Q_BLOCK) + t.shape[2:]), 0, 1)

    def attend(blk):
        qn, qr = blk
        s = (jnp.einsum('bqhd,bkhd->bhqk', qn, k_nope)
             + jnp.einsum('bqhd,bkd->bhqk', qr, k_rope))
        p = jax.nn.softmax(s.astype(jnp.float32) * scale, axis=-1).astype(v.dtype)
        return jnp.einsum('bhqk,bkhd->bqhd', p, v)

    o = lax.map(attend, (to_blocks(q_nope), to_blocks(q_rope)))
    return jnp.swapaxes(o, 0, 1).reshape(B, S, MLA_HEADS * V_HEAD)


def spatial_gating(z, ln_g, ln_b, w_s, b_s):
    B, S, _ = z.shape
    z = jax.nn.gelu(z)
    u, v = z[..., :SG_DIM], z[..., SG_DIM:]
    v = layernorm(v, ln_g, ln_b)
    v = v.reshape(B, S // SG_CHUNK, SG_CHUNK, SG_GROUPS, SG_DIM // SG_GROUPS)
    mixed = jnp.einsum('gts,bcsgd->bctgd', w_s, v) + b_s.T[:, :, None]
    return u * mixed.reshape(B, S, SG_DIM)


def centred_delta(z):
    prev = jnp.pad(z[:, :-1], ((0, 0), (1, 0), (0, 0)))
    nxt = jnp.pad(z[:, 1:], ((0, 0), (0, 1), (0, 0)))
    return 0.5 * (prev + nxt) - z


def rwkv7_bidir(z, mu, w0, w2, a0, a2, g2, k_k, k_a, r_k, ln_g, ln_b):
    B, S, _ = z.shape
    f32 = jnp.float32
    z = z + mu * centred_delta(z)
    cuts = np.cumsum([RWKV_DIM, RWKV_DIM, RWKV_DIM, N_DIR * DECAY_LORA, N_DIR * AAA_LORA]).tolist()
    r, k, v, wl, al, gl = jnp.split(z, cuts, axis=-1)
    wl = wl.reshape(B, S, N_DIR, DECAY_LORA)
    al = al.reshape(B, S, N_DIR, AAA_LORA)
    w = (w0 + jnp.einsum('bsnl,nlc->bsnc', jnp.tanh(wl), w2)).astype(f32)
    decay = jnp.exp(-jnp.exp(-jax.nn.softplus(-w) - 0.5))
    a = jax.nn.sigmoid((a0 + jnp.einsum('bsnl,nlc->bsnc', al, a2)).astype(f32))
    g = jax.nn.sigmoid(gl) @ g2

    def heads(t):
        return t.reshape(t.shape[:-1] + (RWKV_HEADS, RWKV_HEAD))

    kk = heads((k * k_k).astype(f32))
    kk = kk / jnp.maximum(jnp.sqrt(jnp.sum(kk * kk, axis=-1, keepdims=True)), 1e-12)
    rf, vf = heads(r.astype(f32)), heads(v.astype(f32))
    k_dir = heads(k.astype(f32)[:, :, None] * (1.0 + (a - 1.0) * k_a.astype(f32)))
    b_dir = kk[:, :, None] * heads(a)
    decay_h = heads(decay)

    def bcast(t):
        return jnp.broadcast_to(t[:, :, None], (B, S, N_DIR) + t.shape[2:])

    def time_major(t):
        t = jnp.stack([t[:, :, 0], jnp.flip(t[:, :, 1], axis=1)], axis=0)
        return jnp.transpose(t, (2, 0, 1, 3, 4))

    xs = (time_major(bcast(rf)), time_major(decay_h), time_major(k_dir),
          time_major(bcast(vf)), time_major(bcast(kk)), time_major(b_dir))

    def step(st, inp):
        r_t, w_t, k_t, v_t, kk_t, b_t = inp
        sa = jnp.einsum('dbhij,dbhj->dbhi', st, kk_t)
        st = (st * w_t[..., None, :] - sa[..., :, None] * b_t[..., None, :]
              + v_t[..., :, None] * k_t[..., None, :])
        return st, jnp.einsum('dbhij,dbhj->dbhi', st, r_t)

    s0 = jnp.zeros((N_DIR, B, RWKV_HEADS, RWKV_HEAD, RWKV_HEAD), f32)
    _, ys = lax.scan(step, s0, xs)
    y = jnp.transpose(ys[:, 0] + jnp.flip(ys[:, 1], axis=0), (1, 0, 2, 3))
    mean = jnp.mean(y, axis=-1, keepdims=True)
    var = jnp.mean(jnp.square(y - mean), axis=-1, keepdims=True)
    y = ((y - mean) * lax.rsqrt(var + GN_EPS)).reshape(B, S, RWKV_DIM)
    y = y * ln_g.astype(f32) + ln_b.astype(f32)
    bonus = jnp.sum(jnp.sum(rf[:, :, None] * k_dir * r_k.astype(f32), axis=-1, keepdims=True), axis=2)
    y = y + (bonus * vf).reshape(B, S, RWKV_DIM)
    return y.astype(z.dtype) * g


def swiglu(h, w_gate, w_up, w_down):
    return (jax.nn.silu(h @ w_gate) * (h @ w_up)) @ w_down


def _fwd_setup_inputs(seed: int = 0) -> dict:
    key = jax.random.key(seed)
    ks = iter(jax.random.split(key, 40))
    nrm = lambda shape, s: jax.random.normal(next(ks), shape, jnp.float32) * s
    gain = lambda shape: 1.0 + nrm(shape, 0.02)
    L, D = DEPTH, D_MODEL
    x = jax.random.normal(next(ks), (BATCH, SEQ, D), jnp.float32)
    positions = jnp.broadcast_to(jnp.arange(SEQ, dtype=jnp.int32), (BATCH, SEQ))
    return {
        "x": x,
        "positions": positions,
        "attn_norm_g": gain((L, D)),
        "w_in": nrm((L, D, N_IN), D ** -0.5),
        "gate_b": nrm((L, N_BRANCH, D), 0.02),
        "q_norm_g": gain((L, Q_LORA)),
        "w_uq": nrm((L, Q_LORA, MLA_HEADS * (QK_NOPE + QK_ROPE)), Q_LORA ** -0.5),
        "kv_norm_g": gain((L, KV_LORA)),
        "w_ukv": nrm((L, KV_LORA, MLA_HEADS * (QK_NOPE + V_HEAD)), KV_LORA ** -0.5),
        "sg_ln_g": gain((L, SG_DIM)),
        "sg_ln_b": nrm((L, SG_DIM), 0.02),
        "sg_w": nrm((L, SG_GROUPS, SG_CHUNK, SG_CHUNK), SG_CHUNK ** -0.5),
        "sg_b": gain((L, SG_GROUPS, SG_CHUNK)),
        "rw_mu": jax.random.uniform(next(ks), (L, RWKV_IN), jnp.float32),
        "rw_w0": jax.random.uniform(next(ks), (L, N_DIR, RWKV_DIM), jnp.float32, -4.0, 1.0),
        "rw_w2": nrm((L, N_DIR, DECAY_LORA, RWKV_DIM), 0.5 * DECAY_LORA ** -0.5),
        "rw_a0": nrm((L, N_DIR, RWKV_DIM), 0.1),
        "rw_a2": nrm((L, N_DIR, AAA_LORA, RWKV_DIM), 0.5 * AAA_LORA ** -0.5),
        "rw_g2": nrm((L, GATE_LORA, RWKV_DIM), GATE_LORA ** -0.5),
        "rw_k_k": 0.85 + nrm((L, RWKV_DIM), 0.02),
        "rw_k_a": gain((L, RWKV_DIM)),
        "rw_r_k": nrm((L, RWKV_HEADS, RWKV_HEAD), 0.1),
        "rw_ln_g": gain((L, RWKV_DIM)),
        "rw_ln_b": nrm((L, RWKV_DIM), 0.02),
        "w_branch": nrm((L, N_BRANCH, BRANCH_DIM, D), BRANCH_DIM ** -0.5),
        "w_out": nrm((L, D, D), D ** -0.5),
        "ffn_norm_g": gain((L, D)),
        "w_ffn_gate": nrm((L, D, D_FF), D ** -0.5),
        "w_ffn_up": nrm((L, D, D_FF), D ** -0.5),
        "w_ffn_down": nrm((L, D_FF, D), D_FF ** -0.5),
        "final_norm_g": gain((D,)),
    }


def _fwd_reference(x, positions, attn_norm_g, w_in, gate_b, q_norm_g, w_uq, kv_norm_g, w_ukv,
              sg_ln_g, sg_ln_b, sg_w, sg_b, rw_mu, rw_w0, rw_w2, rw_a0, rw_a2, rw_g2,
              rw_k_k, rw_k_a, rw_r_k, rw_ln_g, rw_ln_b, w_branch, w_out, ffn_norm_g,
              w_ffn_gate, w_ffn_up, w_ffn_down, final_norm_g):
    B, S, D = x.shape
    cuts = np.cumsum([Q_LORA, KV_LORA + QK_ROPE, SG_IN, RWKV_IN]).tolist()
    for l in range(DEPTH):
        h = rmsnorm(x, attn_norm_g[l])
        p = h @ w_in[l]
        p_q, p_kv, p_sg, p_rw, p_gate = jnp.split(p, cuts, axis=-1)
        y_a = mla_attention(p_q, p_kv, positions, q_norm_g[l], w_uq[l], kv_norm_g[l], w_ukv[l])
        y_b = spatial_gating(p_sg, sg_ln_g[l], sg_ln_b[l], sg_w[l], sg_b[l])
        y_c = rwkv7_bidir(p_rw, rw_mu[l], rw_w0[l], rw_w2[l], rw_a0[l], rw_a2[l], rw_g2[l],
                          rw_k_k[l], rw_k_a[l], rw_r_k[l], rw_ln_g[l], rw_ln_b[l])
        branches = jnp.stack([y_a, y_b, y_c], axis=2)
        gates = jax.nn.sigmoid(p_gate.reshape(B, S, N_BRANCH, D) + gate_b[l])
        merged = jnp.sum(gates * jnp.einsum('bsnc,ncd->bsnd', branches, w_branch[l]), axis=2)
        x = x + merged @ w_out[l]
        x = x + swiglu(rmsnorm(x, ffn_norm_g[l]), w_ffn_gate[l], w_ffn_up[l], w_ffn_down[l])
    return rmsnorm(x, final_norm_g)


import jax as _jax
import jax.numpy as _jnp

TWIN_FORMAT = 'train_step'
FWD_PARAMS = ['x', 'positions', 'attn_norm_g', 'w_in', 'gate_b', 'q_norm_g', 'w_uq', 'kv_norm_g', 'w_ukv', 'sg_ln_g', 'sg_ln_b', 'sg_w', 'sg_b', 'rw_mu', 'rw_w0', 'rw_w2', 'rw_a0', 'rw_a2', 'rw_g2', 'rw_k_k', 'rw_k_a', 'rw_r_k', 'rw_ln_g', 'rw_ln_b', 'w_branch', 'w_out', 'ffn_norm_g', 'w_ffn_gate', 'w_ffn_up', 'w_ffn_down', 'final_norm_g']
TWIN_WEIGHTS = ['attn_norm_g', 'w_in', 'gate_b', 'q_norm_g', 'w_uq', 'kv_norm_g', 'w_ukv', 'sg_ln_g', 'sg_ln_b', 'sg_w', 'sg_b', 'rw_mu', 'rw_w0', 'rw_w2', 'rw_a0', 'rw_a2', 'rw_g2', 'rw_k_k', 'rw_k_a', 'rw_r_k', 'rw_ln_g', 'rw_ln_b', 'w_branch', 'w_out', 'ffn_norm_g', 'w_ffn_gate', 'w_ffn_up', 'w_ffn_down', 'final_norm_g']
TWIN_DIFF_INPUT = 'x'
TWIN_INPUTS = ['x', 'positions', 'attn_norm_g', 'w_in', 'gate_b', 'q_norm_g', 'w_uq', 'kv_norm_g', 'w_ukv', 'sg_ln_g', 'sg_ln_b', 'sg_w', 'sg_b', 'rw_mu', 'rw_w0', 'rw_w2', 'rw_a0', 'rw_a2', 'rw_g2', 'rw_k_k', 'rw_k_a', 'rw_r_k', 'rw_ln_g', 'rw_ln_b', 'w_branch', 'w_out', 'ffn_norm_g', 'w_ffn_gate', 'w_ffn_up', 'w_ffn_down', 'final_norm_g', 'loss_target', 'm_attn_norm_g', 'm_w_in', 'm_gate_b', 'm_q_norm_g', 'm_w_uq', 'm_kv_norm_g', 'm_w_ukv', 'm_sg_ln_g', 'm_sg_ln_b', 'm_sg_w', 'm_sg_b', 'm_rw_mu', 'm_rw_w0', 'm_rw_w2', 'm_rw_a0', 'm_rw_a2', 'm_rw_g2', 'm_rw_k_k', 'm_rw_k_a', 'm_rw_r_k', 'm_rw_ln_g', 'm_rw_ln_b', 'm_w_branch', 'm_w_out', 'm_ffn_norm_g', 'm_w_ffn_gate', 'm_w_ffn_up', 'm_w_ffn_down', 'm_final_norm_g', 'v_attn_norm_g', 'v_w_in', 'v_gate_b', 'v_q_norm_g', 'v_w_uq', 'v_kv_norm_g', 'v_w_ukv', 'v_sg_ln_g', 'v_sg_ln_b', 'v_sg_w', 'v_sg_b', 'v_rw_mu', 'v_rw_w0', 'v_rw_w2', 'v_rw_a0', 'v_rw_a2', 'v_rw_g2', 'v_rw_k_k', 'v_rw_k_a', 'v_rw_r_k', 'v_rw_ln_g', 'v_rw_ln_b', 'v_w_branch', 'v_w_out', 'v_ffn_norm_g', 'v_w_ffn_gate', 'v_w_ffn_up', 'v_w_ffn_down', 'v_final_norm_g']
TWIN_OUTPUTS = ['loss', 'grad_x', 'grad_attn_norm_g', 'grad_w_in', 'grad_gate_b', 'grad_q_norm_g', 'grad_w_uq', 'grad_kv_norm_g', 'grad_w_ukv', 'grad_sg_ln_g', 'grad_sg_ln_b', 'grad_sg_w', 'grad_sg_b', 'grad_rw_mu', 'grad_rw_w0', 'grad_rw_w2', 'grad_rw_a0', 'grad_rw_a2', 'grad_rw_g2', 'grad_rw_k_k', 'grad_rw_k_a', 'grad_rw_r_k', 'grad_rw_ln_g', 'grad_rw_ln_b', 'grad_w_branch', 'grad_w_out', 'grad_ffn_norm_g', 'grad_w_ffn_gate', 'grad_w_ffn_up', 'grad_w_ffn_down', 'grad_final_norm_g', 'delta_attn_norm_g', 'delta_w_in', 'delta_gate_b', 'delta_q_norm_g', 'delta_w_uq', 'delta_kv_norm_g', 'delta_w_ukv', 'delta_sg_ln_g', 'delta_sg_ln_b', 'delta_sg_w', 'delta_sg_b', 'delta_rw_mu', 'delta_rw_w0', 'delta_rw_w2', 'delta_rw_a0', 'delta_rw_a2', 'delta_rw_g2', 'delta_rw_k_k', 'delta_rw_k_a', 'delta_rw_r_k', 'delta_rw_ln_g', 'delta_rw_ln_b', 'delta_w_branch', 'delta_w_out', 'delta_ffn_norm_g', 'delta_w_ffn_gate', 'delta_w_ffn_up', 'delta_w_ffn_down', 'delta_final_norm_g', 'new_m_attn_norm_g', 'new_m_w_in', 'new_m_gate_b', 'new_m_q_norm_g', 'new_m_w_uq', 'new_m_kv_norm_g', 'new_m_w_ukv', 'new_m_sg_ln_g', 'new_m_sg_ln_b', 'new_m_sg_w', 'new_m_sg_b', 'new_m_rw_mu', 'new_m_rw_w0', 'new_m_rw_w2', 'new_m_rw_a0', 'new_m_rw_a2', 'new_m_rw_g2', 'new_m_rw_k_k', 'new_m_rw_k_a', 'new_m_rw_r_k', 'new_m_rw_ln_g', 'new_m_rw_ln_b', 'new_m_w_branch', 'new_m_w_out', 'new_m_ffn_norm_g', 'new_m_w_ffn_gate', 'new_m_w_ffn_up', 'new_m_w_ffn_down', 'new_m_final_norm_g', 'new_v_attn_norm_g', 'new_v_w_in', 'new_v_gate_b', 'new_v_q_norm_g', 'new_v_w_uq', 'new_v_kv_norm_g', 'new_v_w_ukv', 'new_v_sg_ln_g', 'new_v_sg_ln_b', 'new_v_sg_w', 'new_v_sg_b', 'new_v_rw_mu', 'new_v_rw_w0', 'new_v_rw_w2', 'new_v_rw_a0', 'new_v_rw_a2', 'new_v_rw_g2', 'new_v_rw_k_k', 'new_v_rw_k_a', 'new_v_rw_r_k', 'new_v_rw_ln_g', 'new_v_rw_ln_b', 'new_v_w_branch', 'new_v_w_out', 'new_v_ffn_norm_g', 'new_v_w_ffn_gate', 'new_v_w_ffn_up', 'new_v_w_ffn_down', 'new_v_final_norm_g']
TWIN_LEAF_KINDS = {'loss': 'loss', 'grad_x': 'grad_x', 'grad_attn_norm_g': 'grad_w', 'grad_w_in': 'grad_w', 'grad_gate_b': 'grad_w', 'grad_q_norm_g': 'grad_w', 'grad_w_uq': 'grad_w', 'grad_kv_norm_g': 'grad_w', 'grad_w_ukv': 'grad_w', 'grad_sg_ln_g': 'grad_w', 'grad_sg_ln_b': 'grad_w', 'grad_sg_w': 'grad_w', 'grad_sg_b': 'grad_w', 'grad_rw_mu': 'grad_w', 'grad_rw_w0': 'grad_w', 'grad_rw_w2': 'grad_w', 'grad_rw_a0': 'grad_w', 'grad_rw_a2': 'grad_w', 'grad_rw_g2': 'grad_w', 'grad_rw_k_k': 'grad_w', 'grad_rw_k_a': 'grad_w', 'grad_rw_r_k': 'grad_w', 'grad_rw_ln_g': 'grad_w', 'grad_rw_ln_b': 'grad_w', 'grad_w_branch': 'grad_w', 'grad_w_out': 'grad_w', 'grad_ffn_norm_g': 'grad_w', 'grad_w_ffn_gate': 'grad_w', 'grad_w_ffn_up': 'grad_w', 'grad_w_ffn_down': 'grad_w', 'grad_final_norm_g': 'grad_w', 'delta_attn_norm_g': 'delta_w', 'delta_w_in': 'delta_w', 'delta_gate_b': 'delta_w', 'delta_q_norm_g': 'delta_w', 'delta_w_uq': 'delta_w', 'delta_kv_norm_g': 'delta_w', 'delta_w_ukv': 'delta_w', 'delta_sg_ln_g': 'delta_w', 'delta_sg_ln_b': 'delta_w', 'delta_sg_w': 'delta_w', 'delta_sg_b': 'delta_w', 'delta_rw_mu': 'delta_w', 'delta_rw_w0': 'delta_w', 'delta_rw_w2': 'delta_w', 'delta_rw_a0': 'delta_w', 'delta_rw_a2': 'delta_w', 'delta_rw_g2': 'delta_w', 'delta_rw_k_k': 'delta_w', 'delta_rw_k_a': 'delta_w', 'delta_rw_r_k': 'delta_w', 'delta_rw_ln_g': 'delta_w', 'delta_rw_ln_b': 'delta_w', 'delta_w_branch': 'delta_w', 'delta_w_out': 'delta_w', 'delta_ffn_norm_g': 'delta_w', 'delta_w_ffn_gate': 'delta_w', 'delta_w_ffn_up': 'delta_w', 'delta_w_ffn_down': 'delta_w', 'delta_final_norm_g': 'delta_w', 'new_m_attn_norm_g': 'new_m', 'new_m_w_in': 'new_m', 'new_m_gate_b': 'new_m', 'new_m_q_norm_g': 'new_m', 'new_m_w_uq': 'new_m', 'new_m_kv_norm_g': 'new_m', 'new_m_w_ukv': 'new_m', 'new_m_sg_ln_g': 'new_m', 'new_m_sg_ln_b': 'new_m', 'new_m_sg_w': 'new_m', 'new_m_sg_b': 'new_m', 'new_m_rw_mu': 'new_m', 'new_m_rw_w0': 'new_m', 'new_m_rw_w2': 'new_m', 'new_m_rw_a0': 'new_m', 'new_m_rw_a2': 'new_m', 'new_m_rw_g2': 'new_m', 'new_m_rw_k_k': 'new_m', 'new_m_rw_k_a': 'new_m', 'new_m_rw_r_k': 'new_m', 'new_m_rw_ln_g': 'new_m', 'new_m_rw_ln_b': 'new_m', 'new_m_w_branch': 'new_m', 'new_m_w_out': 'new_m', 'new_m_ffn_norm_g': 'new_m', 'new_m_w_ffn_gate': 'new_m', 'new_m_w_ffn_up': 'new_m', 'new_m_w_ffn_down': 'new_m', 'new_m_final_norm_g': 'new_m', 'new_v_attn_norm_g': 'new_v', 'new_v_w_in': 'new_v', 'new_v_gate_b': 'new_v', 'new_v_q_norm_g': 'new_v', 'new_v_w_uq': 'new_v', 'new_v_kv_norm_g': 'new_v', 'new_v_w_ukv': 'new_v', 'new_v_sg_ln_g': 'new_v', 'new_v_sg_ln_b': 'new_v', 'new_v_sg_w': 'new_v', 'new_v_sg_b': 'new_v', 'new_v_rw_mu': 'new_v', 'new_v_rw_w0': 'new_v', 'new_v_rw_w2': 'new_v', 'new_v_rw_a0': 'new_v', 'new_v_rw_a2': 'new_v', 'new_v_rw_g2': 'new_v', 'new_v_rw_k_k': 'new_v', 'new_v_rw_k_a': 'new_v', 'new_v_rw_r_k': 'new_v', 'new_v_rw_ln_g': 'new_v', 'new_v_rw_ln_b': 'new_v', 'new_v_w_branch': 'new_v', 'new_v_w_out': 'new_v', 'new_v_ffn_norm_g': 'new_v', 'new_v_w_ffn_gate': 'new_v', 'new_v_w_ffn_up': 'new_v', 'new_v_w_ffn_down': 'new_v', 'new_v_final_norm_g': 'new_v'}


def _forward(args):
    return _fwd_reference(*[args[k] for k in FWD_PARAMS])


def _output_shape():
    out = _jax.eval_shape(lambda: _forward(_fwd_setup_inputs(0)))
    return out.shape, out.dtype

N_MICROBATCH = 1
ADAM_LR = 0.001
ADAM_B1 = 0.9
ADAM_B2 = 0.999
ADAM_EPS = 1e-08
ADAM_WD = 0.01
ADAM_STEP = 10
PER_EXAMPLE_BATCH_AXIS = {'x': 0, 'positions': 0, 'loss_target': 0}
SHARED_INPUTS = []
_WEIGHT_DTYPES = {'attn_norm_g': _jnp.float32, 'w_in': _jnp.float32, 'gate_b': _jnp.float32, 'q_norm_g': _jnp.float32, 'w_uq': _jnp.float32, 'kv_norm_g': _jnp.float32, 'w_ukv': _jnp.float32, 'sg_ln_g': _jnp.float32, 'sg_ln_b': _jnp.float32, 'sg_w': _jnp.float32, 'sg_b': _jnp.float32, 'rw_mu': _jnp.float32, 'rw_w0': _jnp.float32, 'rw_w2': _jnp.float32, 'rw_a0': _jnp.float32, 'rw_a2': _jnp.float32, 'rw_g2': _jnp.float32, 'rw_k_k': _jnp.float32, 'rw_k_a': _jnp.float32, 'rw_r_k': _jnp.float32, 'rw_ln_g': _jnp.float32, 'rw_ln_b': _jnp.float32, 'w_branch': _jnp.float32, 'w_out': _jnp.float32, 'ffn_norm_g': _jnp.float32, 'w_ffn_gate': _jnp.float32, 'w_ffn_up': _jnp.float32, 'w_ffn_down': _jnp.float32, 'final_norm_g': _jnp.float32}
MOMENT_SCALE = {'attn_norm_g': 2.183137e-01, 'w_in': 8.413990e-02, 'gate_b': 3.404073e-02, 'q_norm_g': 2.496102e-02, 'w_uq': 1.717933e-02, 'kv_norm_g': 5.063897e-02, 'w_ukv': 2.228464e-02, 'sg_ln_g': 1.300699e-01, 'sg_ln_b': 1.215909e-01, 'sg_w': 8.508554e-02, 'sg_b': 8.594578e-02, 'rw_mu': 1.537372e-01, 'rw_w0': 3.599114e-02, 'rw_w2': 5.975209e-03, 'rw_a0': 3.078262e-02, 'rw_a2': 2.270485e-02, 'rw_g2': 1.031736e-01, 'rw_k_k': 6.496034e-02, 'rw_k_a': 1.170163e-01, 'rw_r_k': 2.804841e-01, 'rw_ln_g': 1.045797e-01, 'rw_ln_b': 1.089124e-01, 'w_branch': 8.197811e-02, 'w_out': 1.422250e-01, 'ffn_norm_g': 1.691135e-01, 'w_ffn_gate': 7.337669e-02, 'w_ffn_up': 7.120246e-02, 'w_ffn_down': 1.178366e-01, 'final_norm_g': 6.383551e+01}


def _to_microbatches(a, axis):
    t = _jnp.moveaxis(a, axis, 0)
    t = t.reshape((N_MICROBATCH, t.shape[0] // N_MICROBATCH) + t.shape[1:])
    return _jnp.moveaxis(t, 1, axis + 1)


def setup_inputs(seed: int = 0) -> dict:
    inp = _fwd_setup_inputs(seed)
    key = _jax.random.fold_in(_jax.random.key(seed), 7919)
    shape, _ = _output_shape()
    out = dict(inp)
    out["loss_target"] = _jax.random.normal(_jax.random.fold_in(key, 0), shape, _jnp.float32)
    for i, name in enumerate(TWIN_WEIGHTS):
        w = inp[name].astype(_jnp.float32)
        if MOMENT_SCALE is None:
            s = _jnp.sqrt(_jnp.mean(_jnp.square(w)) + 1e-30)
        else:
            s = MOMENT_SCALE[name]
        km, kv = _jax.random.split(_jax.random.fold_in(key, i + 1))
        out[name] = w
        out["m_" + name] = s * _jax.random.normal(km, w.shape, _jnp.float32)
        out["v_" + name] = (s * s) * _jax.random.uniform(kv, w.shape, _jnp.float32, 0.5, 1.5)
    if N_MICROBATCH > 1:
        for name, axis in PER_EXAMPLE_BATCH_AXIS.items():
            out[name] = _to_microbatches(out[name], axis)
    return {'x': out['x'], 'positions': out['positions'], 'attn_norm_g': out['attn_norm_g'], 'w_in': out['w_in'], 'gate_b': out['gate_b'], 'q_norm_g': out['q_norm_g'], 'w_uq': out['w_uq'], 'kv_norm_g': out['kv_norm_g'], 'w_ukv': out['w_ukv'], 'sg_ln_g': out['sg_ln_g'], 'sg_ln_b': out['sg_ln_b'], 'sg_w': out['sg_w'], 'sg_b': out['sg_b'], 'rw_mu': out['rw_mu'], 'rw_w0': out['rw_w0'], 'rw_w2': out['rw_w2'], 'rw_a0': out['rw_a0'], 'rw_a2': out['rw_a2'], 'rw_g2': out['rw_g2'], 'rw_k_k': out['rw_k_k'], 'rw_k_a': out['rw_k_a'], 'rw_r_k': out['rw_r_k'], 'rw_ln_g': out['rw_ln_g'], 'rw_ln_b': out['rw_ln_b'], 'w_branch': out['w_branch'], 'w_out': out['w_out'], 'ffn_norm_g': out['ffn_norm_g'], 'w_ffn_gate': out['w_ffn_gate'], 'w_ffn_up': out['w_ffn_up'], 'w_ffn_down': out['w_ffn_down'], 'final_norm_g': out['final_norm_g'], 'loss_target': out['loss_target'], 'm_attn_norm_g': out['m_attn_norm_g'], 'm_w_in': out['m_w_in'], 'm_gate_b': out['m_gate_b'], 'm_q_norm_g': out['m_q_norm_g'], 'm_w_uq': out['m_w_uq'], 'm_kv_norm_g': out['m_kv_norm_g'], 'm_w_ukv': out['m_w_ukv'], 'm_sg_ln_g': out['m_sg_ln_g'], 'm_sg_ln_b': out['m_sg_ln_b'], 'm_sg_w': out['m_sg_w'], 'm_sg_b': out['m_sg_b'], 'm_rw_mu': out['m_rw_mu'], 'm_rw_w0': out['m_rw_w0'], 'm_rw_w2': out['m_rw_w2'], 'm_rw_a0': out['m_rw_a0'], 'm_rw_a2': out['m_rw_a2'], 'm_rw_g2': out['m_rw_g2'], 'm_rw_k_k': out['m_rw_k_k'], 'm_rw_k_a': out['m_rw_k_a'], 'm_rw_r_k': out['m_rw_r_k'], 'm_rw_ln_g': out['m_rw_ln_g'], 'm_rw_ln_b': out['m_rw_ln_b'], 'm_w_branch': out['m_w_branch'], 'm_w_out': out['m_w_out'], 'm_ffn_norm_g': out['m_ffn_norm_g'], 'm_w_ffn_gate': out['m_w_ffn_gate'], 'm_w_ffn_up': out['m_w_ffn_up'], 'm_w_ffn_down': out['m_w_ffn_down'], 'm_final_norm_g': out['m_final_norm_g'], 'v_attn_norm_g': out['v_attn_norm_g'], 'v_w_in': out['v_w_in'], 'v_gate_b': out['v_gate_b'], 'v_q_norm_g': out['v_q_norm_g'], 'v_w_uq': out['v_w_uq'], 'v_kv_norm_g': out['v_kv_norm_g'], 'v_w_ukv': out['v_w_ukv'], 'v_sg_ln_g': out['v_sg_ln_g'], 'v_sg_ln_b': out['v_sg_ln_b'], 'v_sg_w': out['v_sg_w'], 'v_sg_b': out['v_sg_b'], 'v_rw_mu': out['v_rw_mu'], 'v_rw_w0': out['v_rw_w0'], 'v_rw_w2': out['v_rw_w2'], 'v_rw_a0': out['v_rw_a0'], 'v_rw_a2': out['v_rw_a2'], 'v_rw_g2': out['v_rw_g2'], 'v_rw_k_k': out['v_rw_k_k'], 'v_rw_k_a': out['v_rw_k_a'], 'v_rw_r_k': out['v_rw_r_k'], 'v_rw_ln_g': out['v_rw_ln_g'], 'v_rw_ln_b': out['v_rw_ln_b'], 'v_w_branch': out['v_w_branch'], 'v_w_out': out['v_w_out'], 'v_ffn_norm_g': out['v_ffn_norm_g'], 'v_w_ffn_gate': out['v_w_ffn_gate'], 'v_w_ffn_up': out['v_w_ffn_up'], 'v_w_ffn_down': out['v_w_ffn_down'], 'v_final_norm_g': out['v_final_norm_g']}


def _loss(weights, diff, rest, loss_target):
    with _jax.named_scope("forward"):
        args = {**rest, TWIN_DIFF_INPUT: diff, **{k: w.astype(_WEIGHT_DTYPES[k]) for k, w in weights.items()}}
        y = _forward(args)
    with _jax.named_scope("loss_head"):
        err = _jnp.square(y.astype(_jnp.float32) - loss_target)
        return 0.5 * _jnp.sum(_jnp.mean(err, axis=-1)) if err.ndim else 0.5 * err


def _adamw(w, g, m, v):
    m = ADAM_B1 * m + (1.0 - ADAM_B1) * g
    v = ADAM_B2 * v + (1.0 - ADAM_B2) * _jnp.square(g)
    m_hat = m / (1.0 - ADAM_B1 ** ADAM_STEP)
    v_hat = v / (1.0 - ADAM_B2 ** ADAM_STEP)
    delta = -ADAM_LR * (m_hat / (_jnp.sqrt(v_hat) + ADAM_EPS) + ADAM_WD * w)
    return delta, m, v


def reference(x, positions, attn_norm_g, w_in, gate_b, q_norm_g, w_uq, kv_norm_g, w_ukv, sg_ln_g, sg_ln_b, sg_w, sg_b, rw_mu, rw_w0, rw_w2, rw_a0, rw_a2, rw_g2, rw_k_k, rw_k_a, rw_r_k, rw_ln_g, rw_ln_b, w_branch, w_out, ffn_norm_g, w_ffn_gate, w_ffn_up, w_ffn_down, final_norm_g, loss_target, m_attn_norm_g, m_w_in, m_gate_b, m_q_norm_g, m_w_uq, m_kv_norm_g, m_w_ukv, m_sg_ln_g, m_sg_ln_b, m_sg_w, m_sg_b, m_rw_mu, m_rw_w0, m_rw_w2, m_rw_a0, m_rw_a2, m_rw_g2, m_rw_k_k, m_rw_k_a, m_rw_r_k, m_rw_ln_g, m_rw_ln_b, m_w_branch, m_w_out, m_ffn_norm_g, m_w_ffn_gate, m_w_ffn_up, m_w_ffn_down, m_final_norm_g, v_attn_norm_g, v_w_in, v_gate_b, v_q_norm_g, v_w_uq, v_kv_norm_g, v_w_ukv, v_sg_ln_g, v_sg_ln_b, v_sg_w, v_sg_b, v_rw_mu, v_rw_w0, v_rw_w2, v_rw_a0, v_rw_a2, v_rw_g2, v_rw_k_k, v_rw_k_a, v_rw_r_k, v_rw_ln_g, v_rw_ln_b, v_w_branch, v_w_out, v_ffn_norm_g, v_w_ffn_gate, v_w_ffn_up, v_w_ffn_down, v_final_norm_g):
    given = dict(x=x, positions=positions, attn_norm_g=attn_norm_g, w_in=w_in, gate_b=gate_b, q_norm_g=q_norm_g, w_uq=w_uq, kv_norm_g=kv_norm_g, w_ukv=w_ukv, sg_ln_g=sg_ln_g, sg_ln_b=sg_ln_b, sg_w=sg_w, sg_b=sg_b, rw_mu=rw_mu, rw_w0=rw_w0, rw_w2=rw_w2, rw_a0=rw_a0, rw_a2=rw_a2, rw_g2=rw_g2, rw_k_k=rw_k_k, rw_k_a=rw_k_a, rw_r_k=rw_r_k, rw_ln_g=rw_ln_g, rw_ln_b=rw_ln_b, w_branch=w_branch, w_out=w_out, ffn_norm_g=ffn_norm_g, w_ffn_gate=w_ffn_gate, w_ffn_up=w_ffn_up, w_ffn_down=w_ffn_down, final_norm_g=final_norm_g, loss_target=loss_target, m_attn_norm_g=m_attn_norm_g, m_w_in=m_w_in, m_gate_b=m_gate_b, m_q_norm_g=m_q_norm_g, m_w_uq=m_w_uq, m_kv_norm_g=m_kv_norm_g, m_w_ukv=m_w_ukv, m_sg_ln_g=m_sg_ln_g, m_sg_ln_b=m_sg_ln_b, m_sg_w=m_sg_w, m_sg_b=m_sg_b, m_rw_mu=m_rw_mu, m_rw_w0=m_rw_w0, m_rw_w2=m_rw_w2, m_rw_a0=m_rw_a0, m_rw_a2=m_rw_a2, m_rw_g2=m_rw_g2, m_rw_k_k=m_rw_k_k, m_rw_k_a=m_rw_k_a, m_rw_r_k=m_rw_r_k, m_rw_ln_g=m_rw_ln_g, m_rw_ln_b=m_rw_ln_b, m_w_branch=m_w_branch, m_w_out=m_w_out, m_ffn_norm_g=m_ffn_norm_g, m_w_ffn_gate=m_w_ffn_gate, m_w_ffn_up=m_w_ffn_up, m_w_ffn_down=m_w_ffn_down, m_final_norm_g=m_final_norm_g, v_attn_norm_g=v_attn_norm_g, v_w_in=v_w_in, v_gate_b=v_gate_b, v_q_norm_g=v_q_norm_g, v_w_uq=v_w_uq, v_kv_norm_g=v_kv_norm_g, v_w_ukv=v_w_ukv, v_sg_ln_g=v_sg_ln_g, v_sg_ln_b=v_sg_ln_b, v_sg_w=v_sg_w, v_sg_b=v_sg_b, v_rw_mu=v_rw_mu, v_rw_w0=v_rw_w0, v_rw_w2=v_rw_w2, v_rw_a0=v_rw_a0, v_rw_a2=v_rw_a2, v_rw_g2=v_rw_g2, v_rw_k_k=v_rw_k_k, v_rw_k_a=v_rw_k_a, v_rw_r_k=v_rw_r_k, v_rw_ln_g=v_rw_ln_g, v_rw_ln_b=v_rw_ln_b, v_w_branch=v_w_branch, v_w_out=v_w_out, v_ffn_norm_g=v_ffn_norm_g, v_w_ffn_gate=v_w_ffn_gate, v_w_ffn_up=v_w_ffn_up, v_w_ffn_down=v_w_ffn_down, v_final_norm_g=v_final_norm_g)
    weights = {n: given[n] for n in TWIN_WEIGHTS}
    shared = {n: given[n] for n in SHARED_INPUTS}
    per_example = {n: given[n] for n in ['x', 'positions']}
    grad_fn = _jax.value_and_grad(_loss, argnums=(0, 1))

    def one_microbatch(ex, loss_target):
        ex = dict(ex)
        diff = ex.pop(TWIN_DIFF_INPUT)
        return grad_fn(weights, diff, {**shared, **ex}, loss_target)

    if N_MICROBATCH == 1:
        loss, (grad_w, grad_x) = one_microbatch(per_example, given["loss_target"])
    else:
        def body(carry, xs):
            loss_sum, grad_sum = carry
            l_k, (gw_k, gx_k) = one_microbatch(xs[0], xs[1])
            with _jax.named_scope("update"):
                return (loss_sum + l_k, _jax.tree.map(_jnp.add, grad_sum, gw_k)), gx_k

        init = (_jnp.zeros((), _jnp.float32), _jax.tree.map(_jnp.zeros_like, weights))
        (loss, grad_w), grad_x = _jax.lax.scan(body, init, (per_example, given["loss_target"]))
    with _jax.named_scope("update"):
        delta_w, new_m, new_v = {}, {}, {}
        for n in TWIN_WEIGHTS:
            delta_w[n], new_m[n], new_v[n] = _adamw(weights[n], grad_w[n], given["m_" + n], given["v_" + n])
    return (loss, grad_x, *[grad_w[n] for n in TWIN_WEIGHTS], *[delta_w[n] for n in TWIN_WEIGHTS],
            *[new_m[n] for n in TWIN_WEIGHTS], *[new_v[n] for n in TWIN_WEIGHTS])
```

```python
import functools
import math

import numpy as np
import jax
import jax.numpy as jnp
from jax import lax
from jax.experimental import pallas as pl
from jax.experimental.pallas import tpu as pltpu

F32 = jnp.float32
BF16 = jnp.bfloat16

DEPTH = 2
MLA_HEADS = 8
Q_LORA = 384
KV_LORA = 256
QK_NOPE = 64
QK_ROPE = 32
V_HEAD = 64
ROPE_THETA = 10000.0
SG_GROUPS = 8
SG_DIM = 512
SG_CHUNK = 128
RWKV_HEADS = 8
RWKV_HEAD = 64
RWKV_DIM = 512
GN_EPS = 64e-5
NORM_EPS = 1e-6
D_FF = 2816
RWKV_IN = 1920
N_IN = 6688
ADAM_LR, ADAM_B1, ADAM_B2, ADAM_EPS, ADAM_WD, ADAM_STEP = 0.001, 0.9, 0.999, 1e-08, 0.01, 10

LANES = 128
HEAD_PAD = 128
VMEM_LIMIT = 56 * 1024 * 1024
MESH = pl.DeviceIdType.MESH

WEIGHTS = ['attn_norm_g', 'w_in', 'gate_b', 'q_norm_g', 'w_uq', 'kv_norm_g', 'w_ukv', 'sg_ln_g', 'sg_ln_b', 'sg_w',
           'sg_b', 'rw_mu', 'rw_w0', 'rw_w2', 'rw_a0', 'rw_a2', 'rw_g2', 'rw_k_k', 'rw_k_a', 'rw_r_k', 'rw_ln_g',
           'rw_ln_b', 'w_branch', 'w_out', 'ffn_norm_g', 'w_ffn_gate', 'w_ffn_up', 'w_ffn_down', 'final_norm_g']
SHARD_AXIS = {'w_in': 2, 'gate_b': 2, 'w_uq': 2, 'w_ukv': 2, 'rw_w0': 2, 'rw_w2': 3, 'rw_a0': 2, 'rw_a2': 3,
              'rw_g2': 2, 'w_branch': 3, 'w_out': 1, 'w_ffn_gate': 2, 'w_ffn_up': 2, 'w_ffn_down': 1}
SHARDED = [n for n in WEIGHTS if n in SHARD_AXIS]
REPLICATED = [n for n in WEIGHTS if n not in SHARD_AXIS]


def _params(sem=None):
    return pltpu.CompilerParams(dimension_semantics=sem, vmem_limit_bytes=VMEM_LIMIT)


def _pick(n, cands):
    for c in cands:
        if n % c == 0:
            return c
    return n


def _dot(a, b, dims):
    return lax.dot_general(a.astype(BF16), b.astype(BF16), (dims, ((), ())), preferred_element_type=F32)


def _nn(a, b):
    return _dot(a, b, ((1,), (0,)))


def _nt(a, b):
    return _dot(a, b, ((1,), (1,)))


def _tn(a, b):
    return _dot(a, b, ((0,), (0,)))


@jax.custom_vjp
def mm(a, b):
    return _nn(a, b)


mm.defvjp(lambda a, b: (_nn(a, b), (a, b)), lambda res, g: (_nt(g, res[1]), _tn(res[0], g)))


@jax.custom_vjp
def mm_nt(a, b):
    return _nt(a, b)


mm_nt.defvjp(lambda a, b: (_nt(a, b), (a, b)), lambda res, g: (_nn(g, res[1]), _tn(g, res[0])))


def _seg_raw(x, ones):
    hi = x.astype(BF16)
    lo = (x - hi.astype(F32)).astype(BF16)
    d = (((1,), (0,)), ((), ()))
    return (lax.dot_general(hi, ones, d, preferred_element_type=F32)
            + lax.dot_general(lo, ones, d, preferred_element_type=F32))


@jax.custom_vjp
def segsum(x, ones):
    return _seg_raw(x, ones)


segsum.defvjp(lambda x, ones: (_seg_raw(x, ones), ones),
              lambda ones, g: (_seg_raw(g, ones), jnp.zeros_like(ones)))


def _sigmoid(x):
    return 0.5 * (jnp.tanh(0.5 * x) + 1.0)


def _rms(x, g):
    return x * lax.rsqrt(jnp.mean(x * x, axis=-1, keepdims=True) + NORM_EPS) * g


def matmul(name, a, b, mode, add=None):
    if mode == 'nn':
        (M, K), (_, N) = a.shape, b.shape
    elif mode == 'nt':
        (M, K), (N, _) = a.shape, b.shape
    else:
        (K, M), (_, N) = a.shape, b.shape
    tm = _pick(M, (512, 384, 256, 128))
    tn = _pick(N, (512, 384, 256, 128))
    tk = _pick(K, (1024, 512, 384, 256, 128))
    nk = K // tk
    dims = {'nn': ((1,), (0,)), 'nt': ((1,), (1,)), 'tn': ((0,), (0,))}[mode]
    a_spec = pl.BlockSpec((tk, tm), lambda i, j, k: (k, i)) if mode == 'tn' else pl.BlockSpec((tm, tk), lambda i, j, k: (i, k))
    b_spec = pl.BlockSpec((tn, tk), lambda i, j, k: (j, k)) if mode == 'nt' else pl.BlockSpec((tk, tn), lambda i, j, k: (k, j))
    o_spec = pl.BlockSpec((tm, tn), lambda i, j, k: (i, j))
    has_add = add is not None

    def body(*refs):
        if has_add:
            a_ref, b_ref, add_ref, o_ref, acc = refs
        else:
            a_ref, b_ref, o_ref, acc = refs
        k = pl.program_id(2)

        @pl.when(k == 0)
        def _():
            acc[...] = jnp.zeros_like(acc)

        acc[...] += _dot(a_ref[...], b_ref[...], dims)

        @pl.when(k == nk - 1)
        def _():
            o_ref[...] = acc[...] + add_ref[...] if has_add else acc[...]

    ins = [a, b] + ([add] if has_add else [])
    specs = [a_spec, b_spec] + ([o_spec] if has_add else [])
    return pl.pallas_call(
        body, name=name, grid=(M // tm, N // tn, nk), in_specs=specs, out_specs=o_spec,
        out_shape=jax.ShapeDtypeStruct((M, N), F32), scratch_shapes=[pltpu.VMEM((tm, tn), F32)],
        compiler_params=_params(("parallel", "parallel", "arbitrary")))(*ins)


def _full_spec(p):
    nd = p.ndim
    return pl.BlockSpec(p.shape, lambda i, _nd=nd: (0,) * _nd)


def rowwise(name, fn, rows, params, consts, out_widths, tm):
    N = rows[0].shape[0]
    nr, npar, nc = len(rows), len(params), len(consts)

    def body(*refs):
        vals = [r[...] for r in refs[:nr + npar + nc]]
        res = fn(*vals)
        for o, v in zip(refs[nr + npar + nc:], res):
            o[...] = v

    in_specs = ([pl.BlockSpec((tm, r.shape[1]), lambda i: (i, 0)) for r in rows]
                + [_full_spec(p) for p in list(params) + list(consts)])
    out_specs = [pl.BlockSpec((tm, w), lambda i: (i, 0)) for w in out_widths]
    return pl.pallas_call(
        body, name=name, grid=(N // tm,), in_specs=in_specs, out_specs=out_specs,
        out_shape=[jax.ShapeDtypeStruct((N, w), F32) for w in out_widths],
        compiler_params=_params(("parallel",)))(*rows, *params, *consts)


def rowwise_bwd(name, fn, rows, params, consts, d_outs, tm, n_row_diff=None, extra=()):
    N = rows[0].shape[0]
    nr, npar, nc = len(rows), len(params), len(consts)
    nd = nr if n_row_diff is None else n_row_diff
    counts = [len(p) for p in d_outs]
    flat_d = [a for parts in d_outs for a in parts]
    nflat, nex = len(flat_d), len(extra)

    def body(*refs):
        pos = 0
        row_v = [r[...] for r in refs[pos:pos + nr]]; pos += nr
        par_v = [r[...] for r in refs[pos:pos + npar]]; pos += npar
        con_v = [r[...] for r in refs[pos:pos + nc]]; pos += nc
        d_refs = refs[pos:pos + nflat]; pos += nflat
        ex_refs = refs[pos:pos + nex]; pos += nex
        drow_refs = refs[pos:pos + nd]; pos += nd
        dpar_refs = refs[pos:pos + npar]

        def f(*diff):
            return fn(*diff[:nd], *row_v[nd:], *diff[nd:], *con_v)

        _, vjp = jax.vjp(f, *row_v[:nd], *par_v)
        cts, q = [], 0
        for c in counts:
            g = d_refs[q][...]
            for t in range(1, c):
                g = g + d_refs[q + t][...]
            cts.append(g)
            q += c
        grads = vjp(tuple(cts))
        drow = list(grads[:nd])
        for (idx, _), r in zip(extra, ex_refs):
            drow[idx] = drow[idx] + r[...]
        for o, v in zip(drow_refs, drow):
            o[...] = v

        @pl.when(pl.program_id(0) == 0)
        def _():
            for o in dpar_refs:
                o[...] = jnp.zeros_like(o)

        for o, v in zip(dpar_refs, grads[nd:]):
            o[...] += v

    ex_arrs = [a for _, a in extra]
    in_specs = ([pl.BlockSpec((tm, r.shape[1]), lambda i: (i, 0)) for r in rows]
                + [_full_spec(p) for p in list(params) + list(consts)]
                + [pl.BlockSpec((tm, a.shape[1]), lambda i: (i, 0)) for a in flat_d + ex_arrs])
    out_specs = ([pl.BlockSpec((tm, r.shape[1]), lambda i: (i, 0)) for r in rows[:nd]]
                 + [_full_spec(p) for p in params])
    out_shape = ([jax.ShapeDtypeStruct(r.shape, F32) for r in rows[:nd]]
                 + [jax.ShapeDtypeStruct(p.shape, F32) for p in params])
    res = pl.pallas_call(
        body, name=name, grid=(N // tm,), in_specs=in_specs, out_specs=out_specs, out_shape=out_shape,
        compiler_params=_params(("arbitrary",)))(*rows, *params, *consts, *flat_d, *ex_arrs)
    return list(res[:nd]), list(res[nd:])


def f_rms(x, g):
    return (_rms(x, g),)


def f_rope(qq, kv, krr, ct, st):
    hw = MLA_HEADS * HEAD_PAD
    c8 = jnp.tile(ct, (1, MLA_HEADS))
    s8 = jnp.tile(st, (1, MLA_HEADS))
    q = qq[:, :hw] * c8 + qq[:, hw:] * s8
    kr = krr[:, :HEAD_PAD] * ct + krr[:, HEAD_PAD:] * st
    lane = lax.broadcasted_iota(jnp.int32, kv.shape, 1) % HEAD_PAD
    k = jnp.where(lane < QK_NOPE, kv, jnp.tile(kr, (1, MLA_HEADS)))
    return q, k


def f_sg(p, ln_g, ln_b, w, bias):
    z = 0.5 * p * (1.0 + jnp.tanh(0.7978845608028654 * (p + 0.044715 * p * p * p)))
    u, v = z[:, :SG_DIM], z[:, SG_DIM:]
    mu = jnp.mean(v, axis=-1, keepdims=True)
    var = jnp.mean(jnp.square(v - mu), axis=-1, keepdims=True)
    v = (v - mu) * lax.rsqrt(var + 1e-5) * ln_g + ln_b
    lane = lax.broadcasted_iota(jnp.int32, (SG_CHUNK, LANES), 1)
    outs = []
    for c in range(p.shape[0] // SG_CHUNK):
        vc = v[c * SG_CHUNK:(c + 1) * SG_CHUNK]
        cols = []
        for m in range(SG_DIM // LANES):
            blk = vc[:, m * LANES:(m + 1) * LANES]
            cols.append(jnp.where(lane < 64, mm(w[2 * m], blk), mm(w[2 * m + 1], blk)))
        outs.append(jnp.concatenate(cols, axis=1) + bias)
    mixed = outs[0] if len(outs) == 1 else jnp.concatenate(outs, axis=0)
    return (u * mixed,)


def f_rw_pre(z, zp, zn, mu, w0, w2, a0, a2, g2, k_k, k_a, ones):
    z = z + mu * (0.5 * (zp + zn) - z)
    C = RWKV_DIM
    r, k, v = z[:, :C], z[:, C:2 * C], z[:, 2 * C:3 * C]
    wl, al, gl = z[:, 3 * C:3 * C + 128], z[:, 3 * C + 128:3 * C + 256], z[:, 3 * C + 256:]
    w = w0 + mm(jnp.tanh(wl), w2)
    decay = jnp.exp(-0.6065306597126334 * _sigmoid(w))
    a = _sigmoid(a0 + mm(al, a2))
    g = mm(_sigmoid(gl), g2)
    kk = k * k_k
    kk = kk / jnp.maximum(jnp.sqrt(segsum(kk * kk, ones)), 1e-12)
    k2 = jnp.concatenate([k, k], axis=1)
    kdir = k2 * (1.0 + (a - 1.0) * jnp.concatenate([k_a, k_a], axis=1))
    bdir = jnp.concatenate([kk, kk], axis=1) * a
    return r, v, decay, kdir, kk, bdir, g


def f_rw_post(y0, y1, r, v, kdir, g, ln_g, ln_b, r_k, ones):
    y = y0 + y1
    mean = segsum(y, ones) * (1.0 / RWKV_HEAD)
    yc = y - mean
    var = segsum(yc * yc, ones) * (1.0 / RWKV_HEAD)
    y = yc * lax.rsqrt(var + GN_EPS) * ln_g + ln_b
    C = RWKV_DIM
    bonus = segsum(r * kdir[:, :C] * r_k, ones) + segsum(r * kdir[:, C:] * r_k, ones)
    return ((y + bonus * v) * g,)


def f_merge(pg, b0, b1, b2, gate_b):
    D = b0.shape[1]
    gt = _sigmoid(pg + gate_b)
    return (gt[:, :D] * b0 + gt[:, D:2 * D] * b1 + gt[:, 2 * D:] * b2,)


def f_swiglu(au):
    a, u = au[:, :D_FF], au[:, D_FF:]
    return (a * _sigmoid(a) * u,)


def f_add3(a, b, c):
    return (a + b + c,)


def loss_head(x, tgt, g, tm):
    N, D = x.shape

    def body(x_ref, t_ref, g_ref, loss_ref, dx_ref, dg_ref):
        t = t_ref[...]

        def f(xv, gv):
            err = _rms(xv, gv) - t
            return 0.5 * jnp.sum(jnp.mean(err * err, axis=-1, keepdims=True))

        val, (dx, dg) = jax.value_and_grad(f, argnums=(0, 1))(x_ref[...], g_ref[...])
        dx_ref[...] = dx

        @pl.when(pl.program_id(0) == 0)
        def _():
            loss_ref[...] = jnp.zeros_like(loss_ref)
            dg_ref[...] = jnp.zeros_like(dg_ref)

        loss_ref[...] += jnp.full(loss_ref.shape, val, F32)
        dg_ref[...] += dg

    row = pl.BlockSpec((tm, D), lambda i: (i, 0))
    return pl.pallas_call(
        body, name="loss_head", grid=(N // tm,), in_specs=[row, row, _full_spec(g)],
        out_specs=[pl.BlockSpec((1, LANES), lambda i: (0, 0)), row, _full_spec(g)],
        out_shape=[jax.ShapeDtypeStruct((1, LANES), F32), jax.ShapeDtypeStruct((N, D), F32),
                   jax.ShapeDtypeStruct(g.shape, F32)],
        compiler_params=_params(("arbitrary",)))(x, tgt, g)


ATT_SCALE = float((QK_NOPE + QK_ROPE) ** -0.5)


def _attn_block(q, k, kv):
    s = mm_nt(q, k) * ATT_SCALE
    m = lax.stop_gradient(jnp.max(s, axis=-1, keepdims=True))
    e = jnp.exp(s - m)
    p = e / jnp.sum(e, axis=-1, keepdims=True)
    return mm(p, kv)


def attention_fwd(name, q, k, kv, B, S, tq):
    nq = S // tq
    qspec = pl.BlockSpec((tq, HEAD_PAD), lambda b, h, i: (b * nq + i, h))
    kspec = pl.BlockSpec((S, HEAD_PAD), lambda b, h, i: (b, h))

    def body(q_ref, k_ref, kv_ref, o_ref):
        o_ref[...] = _attn_block(q_ref[...], k_ref[...], kv_ref[...])

    return pl.pallas_call(
        body, name=name, grid=(B, MLA_HEADS, nq), in_specs=[qspec, kspec, kspec], out_specs=qspec,
        out_shape=jax.ShapeDtypeStruct(q.shape, F32),
        compiler_params=_params(("parallel", "parallel", "arbitrary")))(q, k, kv)


def attention_bwd(name, q, k, kv, do, B, S, tq):
    nq = S // tq
    qspec = pl.BlockSpec((tq, HEAD_PAD), lambda b, h, i: (b * nq + i, h))
    kspec = pl.BlockSpec((S, HEAD_PAD), lambda b, h, i: (b, h))

    def body(q_ref, k_ref, kv_ref, do_ref, dq_ref, dk_ref, dkv_ref):
        _, vjp = jax.vjp(_attn_block, q_ref[...], k_ref[...], kv_ref[...])
        dq, dk, dkv = vjp(do_ref[...])
        dq_ref[...] = dq

        @pl.when(pl.program_id(2) == 0)
        def _():
            dk_ref[...] = jnp.zeros_like(dk_ref)
            dkv_ref[...] = jnp.zeros_like(dkv_ref)

        dk_ref[...] += dk
        dkv_ref[...] += dkv

    sh = jax.ShapeDtypeStruct(q.shape, F32)
    return pl.pallas_call(
        body, name=name, grid=(B, MLA_HEADS, nq), in_specs=[qspec, kspec, kspec, qspec],
        out_specs=[qspec, kspec, kspec], out_shape=[sh, sh, sh],
        compiler_params=_params(("parallel", "parallel", "arbitrary")))(q, k, kv, do)


SCAN_TC = 8
SCAN_UNROLL = 4


def scan_fwd(name, w, k, b, kk, r, v):
    T, J, L = w.shape
    Ip = v.shape[1]
    jspec = pl.BlockSpec((SCAN_TC, J, L), lambda g: (g, 0, 0))
    ispec = pl.BlockSpec((SCAN_TC, Ip, L), lambda g: (g, 0, 0))
    sspec = pl.BlockSpec((SCAN_TC, J, Ip, L), lambda g: (g, 0, 0, 0))

    def body(w_ref, k_ref, b_ref, kk_ref, r_ref, v_ref, y_ref, sp_ref, s_ref):
        @pl.when(pl.program_id(0) == 0)
        def _():
            s_ref[...] = jnp.zeros_like(s_ref)

        def row(ref, tt, j):
            return jnp.broadcast_to(ref[tt, pl.ds(j, 1), :], (Ip, L))

        def step(tt, carry):
            def p1(j, sa):
                s = s_ref[j]
                sp_ref[tt, j] = s
                return sa + s * row(kk_ref, tt, j)

            sa = lax.fori_loop(0, J, p1, jnp.zeros((Ip, L), F32), unroll=SCAN_UNROLL)
            vt = v_ref[tt]

            def p2(j, y):
                s = s_ref[j] * row(w_ref, tt, j) - sa * row(b_ref, tt, j) + vt * row(k_ref, tt, j)
                s_ref[j] = s
                return y + s * row(r_ref, tt, j)

            y_ref[tt] = lax.fori_loop(0, J, p2, jnp.zeros((Ip, L), F32), unroll=SCAN_UNROLL)
            return carry

        lax.fori_loop(0, SCAN_TC, step, 0)

    return pl.pallas_call(
        body, name=name, grid=(T // SCAN_TC,), in_specs=[jspec] * 5 + [ispec], out_specs=[ispec, sspec],
        out_shape=[jax.ShapeDtypeStruct((T, Ip, L), F32), jax.ShapeDtypeStruct((T, J, Ip, L), F32)],
        scratch_shapes=[pltpu.VMEM((J, Ip, L), F32)],
        compiler_params=_params(("arbitrary",)))(w, k, b, kk, r, v)


def scan_bwd(name, w, k, b, kk, r, v, sp, dy):
    T, J, L = w.shape
    Ip = v.shape[1]
    nT = T // SCAN_TC
    jspec = pl.BlockSpec((SCAN_TC, J, L), lambda g: (nT - 1 - g, 0, 0))
    ispec = pl.BlockSpec((SCAN_TC, Ip, L), lambda g: (nT - 1 - g, 0, 0))
    sspec = pl.BlockSpec((SCAN_TC, J, Ip, L), lambda g: (nT - 1 - g, 0, 0, 0))

    def body(w_ref, k_ref, b_ref, kk_ref, r_ref, v_ref, sp_ref, dy_ref,
             dw_ref, dk_ref, db_ref, dkk_ref, dr_ref, dv_ref, ds_ref):
        @pl.when(pl.program_id(0) == 0)
        def _():
            ds_ref[...] = jnp.zeros_like(ds_ref)

        def row(ref, tt, j):
            return jnp.broadcast_to(ref[tt, pl.ds(j, 1), :], (Ip, L))

        def rsum(x):
            return jnp.sum(x, axis=0, keepdims=True)

        def step(n, carry):
            tt = SCAN_TC - 1 - n
            dyt = dy_ref[tt]
            vt = v_ref[tt]

            def p1(j, c):
                sa, dsa, dv = c
                ds = ds_ref[j] + dyt * row(r_ref, tt, j)
                ds_ref[j] = ds
                return (sa + sp_ref[tt, j] * row(kk_ref, tt, j), dsa - ds * row(b_ref, tt, j),
                        dv + ds * row(k_ref, tt, j))

            z = jnp.zeros((Ip, L), F32)
            sa, dsa, dv = lax.fori_loop(0, J, p1, (z, z, z), unroll=SCAN_UNROLL)
            dv_ref[tt] = dv

            def p2(j, c):
                ds = ds_ref[j]
                s0 = sp_ref[tt, j]
                wj, kkj = row(w_ref, tt, j), row(kk_ref, tt, j)
                s1 = s0 * wj - sa * row(b_ref, tt, j) + vt * row(k_ref, tt, j)
                dr_ref[tt, pl.ds(j, 1), :] = rsum(s1 * dyt)
                dk_ref[tt, pl.ds(j, 1), :] = rsum(ds * vt)
                db_ref[tt, pl.ds(j, 1), :] = -rsum(ds * sa)
                dw_ref[tt, pl.ds(j, 1), :] = rsum(ds * s0)
                dkk_ref[tt, pl.ds(j, 1), :] = rsum(s0 * dsa)
                ds_ref[j] = ds * wj + dsa * kkj
                return c

            lax.fori_loop(0, J, p2, 0, unroll=SCAN_UNROLL)
            return carry

        lax.fori_loop(0, SCAN_TC, step, 0)

    jsh = jax.ShapeDtypeStruct((T, J, L), F32)
    return pl.pallas_call(
        body, name=name, grid=(nT,), in_specs=[jspec] * 5 + [ispec, sspec, ispec],
        out_specs=[jspec] * 5 + [ispec], out_shape=[jsh] * 5 + [jax.ShapeDtypeStruct((T, Ip, L), F32)],
        scratch_shapes=[pltpu.VMEM((J, Ip, L), F32)],
        compiler_params=_params(("arbitrary",)))(w, k, b, kk, r, v, sp, dy)


def adamw(name, w, g, m, v):
    R, C = w.shape
    tr = _pick(R, (256, 128, 64, 32, 16, 8))
    c1 = 1.0 - ADAM_B1 ** ADAM_STEP
    c2 = 1.0 - ADAM_B2 ** ADAM_STEP

    def body(w_ref, g_ref, m_ref, v_ref, d_ref, nm_ref, nv_ref):
        gv = g_ref[...]
        nm = ADAM_B1 * m_ref[...] + (1.0 - ADAM_B1) * gv
        nv = ADAM_B2 * v_ref[...] + (1.0 - ADAM_B2) * jnp.square(gv)
        d_ref[...] = -ADAM_LR * ((nm / c1) / (jnp.sqrt(nv / c2) + ADAM_EPS) + ADAM_WD * w_ref[...])
        nm_ref[...] = nm
        nv_ref[...] = nv

    spec = pl.BlockSpec((tr, C), lambda i: (i, 0))
    sh = jax.ShapeDtypeStruct((R, C), F32)
    return pl.pallas_call(body, name=name, grid=(R // tr,), in_specs=[spec] * 4, out_specs=[spec] * 3,
                          out_shape=[sh] * 3, compiler_params=_params(("parallel",)))(w, g, m, v)


def sum_slots(name, x):
    n, R, C = x.shape
    tr = _pick(R, (256, 128, 64, 32, 16, 8))

    def body(x_ref, o_ref):
        acc = x_ref[0]
        for s in range(1, n):
            acc = acc + x_ref[s]
        o_ref[...] = acc

    return pl.pallas_call(
        body, name=name, grid=(R // tr,), in_specs=[pl.BlockSpec((n, tr, C), lambda i: (0, i, 0))],
        out_specs=pl.BlockSpec((tr, C), lambda i: (i, 0)), out_shape=jax.ShapeDtypeStruct((R, C), F32),
        compiler_params=_params(("parallel",)))(x)


ANY = pl.BlockSpec(memory_space=pl.ANY)


def _xyc():
    return lax.axis_index("x"), lax.axis_index("y"), lax.axis_index("c")


def gather_shards(shard):
    _, R, C = shard.shape

    def body(x_ref, out_ref, send_sems, recv_sems, local_sem):
        x, y, c = _xyc()
        me, sibling = (x, y, c), (x, y, 1 - c)
        chips = [(1 - x, y), (x, 1 - y), (1 - x, 1 - y)]

        def cp(k, cx, cy, half, to, src=None):
            dst = out_ref.at[2 * cx + cy, half]
            return pltpu.make_async_remote_copy(
                src_ref=dst if src is None else src, dst_ref=dst, send_sem=send_sems.at[k],
                recv_sem=recv_sems.at[k], device_id=to, device_id_type=MESH)

        mine = pltpu.make_async_copy(x_ref, out_ref.at[2 * x + y], local_sem)
        mine.start()
        first = [cp(j, x, y, c, (*chip, c), src=x_ref.at[c]) for j, chip in enumerate(chips)]
        for f in first:
            f.start()
        passed = [cp(3 + j, *chip, c, sibling) for j, chip in enumerate(chips)]
        for j, chip in enumerate(chips):
            cp(j, *chip, c, me).wait_recv()
            passed[j].start()
        for j, chip in enumerate(chips):
            cp(3 + j, *chip, 1 - c, me).wait_recv()
        for f in first + passed:
            f.wait_send()
        mine.wait()

    return pl.pallas_call(
        body, name="gather_shards", in_specs=[ANY], out_specs=ANY,
        out_shape=jax.ShapeDtypeStruct((4, 2, R, C), shard.dtype),
        scratch_shapes=[pltpu.SemaphoreType.DMA((6,)), pltpu.SemaphoreType.DMA((6,)), pltpu.SemaphoreType.DMA])(shard)


FLIPS = [(0, 0, 1), (0, 1, 0), (0, 1, 1), (1, 0, 0), (1, 0, 1), (1, 1, 0), (1, 1, 1)]


def scatter_partials(g):
    _, _, R, C = g.shape

    def body(g_ref, out_ref, send_sems, recv_sems, local_sem):
        x, y, c = _xyc()
        me_idx = 4 * x + 2 * y + c
        mine = pltpu.make_async_copy(g_ref.at[2 * x + y, c], out_ref.at[me_idx], local_sem)
        mine.start()
        sends = []
        for k, (fx, fy, fc) in enumerate(FLIPS):
            px, py, pc = (x + fx) % 2, (y + fy) % 2, (c + fc) % 2
            s = pltpu.make_async_remote_copy(
                src_ref=g_ref.at[2 * px + py, pc], dst_ref=out_ref.at[me_idx], send_sem=send_sems.at[k],
                recv_sem=recv_sems.at[k], device_id=(px, py, pc), device_id_type=MESH)
            s.start()
            sends.append(s)
        for k, (fx, fy, fc) in enumerate(FLIPS):
            px, py, pc = (x + fx) % 2, (y + fy) % 2, (c + fc) % 2
            slot = out_ref.at[4 * px + 2 * py + pc]
            pltpu.make_async_remote_copy(
                src_ref=slot, dst_ref=slot, send_sem=send_sems.at[k], recv_sem=recv_sems.at[k],
                device_id=(px, py, pc), device_id_type=MESH).wait_recv()
        for s in sends:
            s.wait_send()
        mine.wait()

    return pl.pallas_call(
        body, name="scatter_partials", in_specs=[ANY], out_specs=ANY,
        out_shape=jax.ShapeDtypeStruct((8, R, C), g.dtype),
        scratch_shapes=[pltpu.SemaphoreType.DMA((7,)), pltpu.SemaphoreType.DMA((7,)), pltpu.SemaphoreType.DMA])(g)


def sibling_join(half):
    R, C = half.shape

    def body(h_ref, out_ref, send_sem, recv_sem, local_sem):
        x, y, c = _xyc()
        mine = pltpu.make_async_copy(h_ref, out_ref.at[c], local_sem)
        mine.start()
        s = pltpu.make_async_remote_copy(src_ref=h_ref, dst_ref=out_ref.at[c], send_sem=send_sem, recv_sem=recv_sem,
                                         device_id=(x, y, 1 - c), device_id_type=MESH)
        s.start()
        theirs = out_ref.at[1 - c]
        pltpu.make_async_remote_copy(src_ref=theirs, dst_ref=theirs, send_sem=send_sem, recv_sem=recv_sem,
                                     device_id=(x, y, 1 - c), device_id_type=MESH).wait_recv()
        s.wait_send()
        mine.wait()

    return pl.pallas_call(
        body, name="sibling_join", in_specs=[ANY], out_specs=ANY,
        out_shape=jax.ShapeDtypeStruct((2, R, C), half.dtype),
        scratch_shapes=[pltpu.SemaphoreType.DMA, pltpu.SemaphoreType.DMA, pltpu.SemaphoreType.DMA])(half)


def gather_all(block):
    R, C = block.shape

    def body(x_ref, out_ref, send_sems, recv_sems, local_sem):
        x, y, c = _xyc()
        mine = pltpu.make_async_copy(x_ref, out_ref.at[4 * x + 2 * y + c], local_sem)
        mine.start()
        sends = []
        for k, (fx, fy, fc) in enumerate(FLIPS):
            px, py, pc = (x + fx) % 2, (y + fy) % 2, (c + fc) % 2
            s = pltpu.make_async_remote_copy(
                src_ref=x_ref, dst_ref=out_ref.at[4 * x + 2 * y + c], send_sem=send_sems.at[k],
                recv_sem=recv_sems.at[k], device_id=(px, py, pc), device_id_type=MESH)
            s.start()
            sends.append(s)
        for k, (fx, fy, fc) in enumerate(FLIPS):
            px, py, pc = (x + fx) % 2, (y + fy) % 2, (c + fc) % 2
            slot = out_ref.at[4 * px + 2 * py + pc]
            pltpu.make_async_remote_copy(
                src_ref=slot, dst_ref=slot, send_sem=send_sems.at[k], recv_sem=recv_sems.at[k],
                device_id=(px, py, pc), device_id_type=MESH).wait_recv()
        for s in sends:
            s.wait_send()
        mine.wait()

    return pl.pallas_call(
        body, name="gather_all", in_specs=[ANY], out_specs=ANY,
        out_shape=jax.ShapeDtypeStruct((8, R, C), block.dtype),
        scratch_shapes=[pltpu.SemaphoreType.DMA((7,)), pltpu.SemaphoreType.DMA((7,)), pltpu.SemaphoreType.DMA])(block)


PACK_C = 1024


def _pack(arrs, row_mult):
    flat = jnp.concatenate([a.reshape(-1) for a in arrs])
    n = flat.shape[0]
    rows = -(-n // PACK_C)
    rows = -(-rows // row_mult) * row_mult
    return jnp.pad(flat, (0, rows * PACK_C - n)).reshape(rows, PACK_C)


def _unpack(buf, shapes):
    flat = buf.reshape(-1)
    out, off = [], 0
    for s in shapes:
        n = int(np.prod(s))
        out.append(flat[off:off + n].reshape(s))
        off += n
    return out


OFF_Q, OFF_CKV, OFF_KR, OFF_SG, OFF_RW, OFF_GATE, N_IN_PAD = 0, 384, 640, 896, 1920, 3840, 6912
ROPE_LANE = QK_NOPE
HALF = QK_ROPE // 2


def _win_layout():
    src = np.full((N_IN_PAD,), -1, np.int64)
    sgn = np.ones((N_IN_PAD,), np.float32)
    src[0:640] = np.arange(0, 640)
    kr0 = Q_LORA + KV_LORA
    src[OFF_KR + ROPE_LANE:OFF_KR + ROPE_LANE + QK_ROPE] = kr0 + np.arange(QK_ROPE)
    sw = OFF_KR + HEAD_PAD + ROPE_LANE
    src[sw:sw + HALF] = kr0 + HALF + np.arange(HALF)
    sgn[sw:sw + HALF] = -1.0
    src[sw + HALF:sw + QK_ROPE] = kr0 + np.arange(HALF)
    src[OFF_SG:N_IN_PAD] = 672 + np.arange(N_IN_PAD - OFF_SG)
    return src, sgn


def _wuq_layout():
    hw = MLA_HEADS * HEAD_PAD
    src = np.full((2 * hw,), -1, np.int64)
    sgn = np.ones((2 * hw,), np.float32)
    per = QK_NOPE + QK_ROPE
    for h in range(MLA_HEADS):
        src[h * HEAD_PAD:h * HEAD_PAD + per] = h * per + np.arange(per)
        sw = hw + h * HEAD_PAD + ROPE_LANE
        src[sw:sw + HALF] = h * per + QK_NOPE + HALF + np.arange(HALF)
        sgn[sw:sw + HALF] = -1.0
        src[sw + HALF:sw + QK_ROPE] = h * per + QK_NOPE + np.arange(HALF)
    return src, sgn


def _permute_cols(w, src, sgn):
    cols = jnp.take(w, jnp.asarray(np.maximum(src, 0)), axis=1)
    return cols * jnp.asarray(np.where(src >= 0, sgn, 0.0).astype(np.float32))


def _unpermute_full(dw, src, sgn, n_cols):
    first = np.full((n_cols,), -1, np.int64)
    second = np.full((n_cols,), -1, np.int64)
    for pos, s in enumerate(src):
        if s < 0:
            continue
        if first[s] < 0:
            first[s] = pos
        else:
            second[s] = pos
    out = jnp.take(dw, jnp.asarray(first), axis=1) * jnp.asarray(sgn[first].astype(np.float32))
    m2 = (second >= 0)
    two = jnp.take(dw, jnp.asarray(np.maximum(second, 0)), axis=1) * jnp.asarray(
        np.where(m2, sgn[np.maximum(second, 0)], 0.0).astype(np.float32))
    return out, two


def _blockdiag(w):
    z = jnp.zeros_like(w[0])
    return jnp.concatenate([jnp.concatenate([w[0], z], axis=1), jnp.concatenate([z, w[1]], axis=1)], axis=0)


def _rope_tables(pos):
    inv = 1.0 / (ROPE_THETA ** (jnp.arange(0, QK_ROPE, 2, dtype=F32) / QK_ROPE))
    ang = pos.astype(F32)[:, None] * inv[None, :]
    cos, sin = jnp.cos(ang), jnp.sin(ang)
    pad = lambda t, fill: jnp.concatenate(
        [jnp.full((t.shape[0], ROPE_LANE), fill, F32), t, t, jnp.full((t.shape[0], HEAD_PAD - ROPE_LANE - QK_ROPE), fill, F32)], axis=1)
    return pad(cos, 1.0), pad(sin, 0.0)


def kernel(x, positions, attn_norm_g, w_in, gate_b, q_norm_g, w_uq, kv_norm_g, w_ukv, sg_ln_g, sg_ln_b, sg_w, sg_b, rw_mu, rw_w0, rw_w2, rw_a0, rw_a2, rw_g2, rw_k_k, rw_k_a, rw_r_k, rw_ln_g, rw_ln_b, w_branch, w_out, ffn_norm_g, w_ffn_gate, w_ffn_up, w_ffn_down, final_norm_g, loss_target, m_attn_norm_g, m_w_in, m_gate_b, m_q_norm_g, m_w_uq, m_kv_norm_g, m_w_ukv, m_sg_ln_g, m_sg_ln_b, m_sg_w, m_sg_b, m_rw_mu, m_rw_w0, m_rw_w2, m_rw_a0, m_rw_a2, m_rw_g2, m_rw_k_k, m_rw_k_a, m_rw_r_k, m_rw_ln_g, m_rw_ln_b, m_w_branch, m_w_out, m_ffn_norm_g, m_w_ffn_gate, m_w_ffn_up, m_w_ffn_down, m_final_norm_g, v_attn_norm_g, v_w_in, v_gate_b, v_q_norm_g, v_w_uq, v_kv_norm_g, v_w_ukv, v_sg_ln_g, v_sg_ln_b, v_sg_w, v_sg_b, v_rw_mu, v_rw_w0, v_rw_w2, v_rw_a0, v_rw_a2, v_rw_g2, v_rw_k_k, v_rw_k_a, v_rw_r_k, v_rw_ln_g, v_rw_ln_b, v_w_branch, v_w_out, v_ffn_norm_g, v_w_ffn_gate, v_w_ffn_up, v_w_ffn_down, v_final_norm_g):
    args = locals()
    W = {n: args[n] for n in WEIGHTS}
    M1 = {n: args['m_' + n] for n in WEIGHTS}
    M2 = {n: args['v_' + n] for n in WEIGHTS}
    B, S, D = x.shape
    N = B * S
    TM = _pick(N, (256, 128))
    TMH = 128
    TQ = _pick(S, (256, 128))

    shard_shapes = [W[n].shape for n in SHARDED]
    wpack = _pack([W[n] for n in SHARDED], 16)
    R = wpack.shape[0]
    gathered = gather_shards(wpack.reshape(2, R // 2, PACK_C)).reshape(4, R, PACK_C)
    full = {}
    pieces = [_unpack(gathered[q], shard_shapes) for q in range(4)]
    for i, n in enumerate(SHARDED):
        full[n] = jnp.concatenate([pieces[q][i] for q in range(4)], axis=SHARD_AXIS[n])
    for n in REPLICATED:
        full[n] = W[n]

    win_src, win_sgn = _win_layout()
    wuq_src, wuq_sgn = _wuq_layout()
    ones = jnp.asarray(np.kron(np.eye(RWKV_HEADS), np.ones((RWKV_HEAD, RWKV_HEAD))), BF16)
    ct, st = _rope_tables(positions.reshape(N))
    row = lambda v: v.reshape(1, -1)

    H, HD = RWKV_HEADS, RWKV_HEAD
    inst = 2 * B * H
    isplit = LANES // inst
    Ip = HD // isplit
    flip = lambda t: jnp.flip(t, axis=1)

    def dirs_of(c):
        c = c.reshape(B, S, -1)
        if c.shape[-1] == RWKV_DIM:
            return jnp.stack([c, flip(c)])
        return jnp.stack([c[..., :RWKV_DIM], flip(c[..., RWKV_DIM:])])

    def to_j(c):
        t = dirs_of(c).reshape(2, B, S, H, HD).transpose(2, 4, 0, 1, 3).reshape(S, HD, inst)
        return jnp.tile(t, (1, 1, isplit))

    def to_i(c):
        return dirs_of(c).reshape(2, B, S, H, isplit, Ip).transpose(2, 5, 4, 0, 1, 3).reshape(S, Ip, LANES)

    def from_i(t):
        return t.reshape(S, Ip, isplit, 2, B, H).transpose(3, 4, 0, 5, 2, 1).reshape(2, B, S, RWKV_DIM)

    def from_j(t):
        t = t.reshape(S, HD, isplit, 2, B, H)
        return [t[:, :, i2].transpose(2, 3, 0, 4, 1).reshape(2, B, S, RWKV_DIM) for i2 in range(isplit)]

    def parts_nodir(ds):
        out = []
        for d in ds:
            out += [d[0].reshape(N, -1), flip(d[1]).reshape(N, -1)]
        return out

    def parts_dir(ds):
        return [jnp.concatenate([d[0], flip(d[1])], axis=-1).reshape(N, -1) for d in ds]

    def shift_prev(z):
        z = z.reshape(B, S, -1)
        return jnp.pad(z[:, :-1], ((0, 0), (1, 0), (0, 0))).reshape(N, -1)

    def shift_next(z):
        z = z.reshape(B, S, -1)
        return jnp.pad(z[:, 1:], ((0, 0), (0, 1), (0, 0))).reshape(N, -1)

    LW = []
    for l in range(DEPTH):
        wb = full['w_branch'][l]
        wb0 = jnp.zeros((MLA_HEADS, HEAD_PAD, D), F32).at[:, QK_NOPE:].set(wb[0].reshape(MLA_HEADS, V_HEAD, D))
        LW.append(dict(
            attn_g=row(full['attn_norm_g'][l]),
            w_in=_permute_cols(full['w_in'][l], win_src, win_sgn),
            gate_b=row(full['gate_b'][l]),
            q_g=row(full['q_norm_g'][l]),
            w_uq=_permute_cols(full['w_uq'][l], wuq_src, wuq_sgn),
            kv_g=row(full['kv_norm_g'][l]),
            w_ukv=full['w_ukv'][l],
            sg_g=row(full['sg_ln_g'][l]), sg_b=row(full['sg_ln_b'][l]), sg_w=full['sg_w'][l],
            sg_bias=jnp.repeat(full['sg_b'][l].T, SG_DIM // SG_GROUPS, axis=1),
            mu=row(full['rw_mu'][l]), w0=row(full['rw_w0'][l]), w2=_blockdiag(full['rw_w2'][l]),
            a0=row(full['rw_a0'][l]), a2=_blockdiag(full['rw_a2'][l]), g2=full['rw_g2'][l],
            k_k=row(full['rw_k_k'][l]), k_a=row(full['rw_k_a'][l]), r_k=row(full['rw_r_k'][l]),
            ln_g=row(full['rw_ln_g'][l]), ln_b=row(full['rw_ln_b'][l]),
            wb0=wb0.reshape(MLA_HEADS * HEAD_PAD, D), wb1=wb[1], wb2=wb[2],
            w_out=full['w_out'][l], ffn_g=row(full['ffn_norm_g'][l]),
            w_gu=jnp.concatenate([full['w_ffn_gate'][l], full['w_ffn_up'][l]], axis=1),
            w_down=full['w_ffn_down'][l]))

    saved = []
    xc = x.reshape(N, D)
    for l in range(DEPTH):
        p = LW[l]
        t = 'l%d_' % l
        sv = dict(x=xc)
        (h,) = rowwise(t + 'attn_norm', f_rms, [xc], [p['attn_g']], [], [D], TM)
        pr = matmul(t + 'in_proj', h, p['w_in'], 'nn')
        sv['h'] = h
        p_q, p_ckv = pr[:, OFF_Q:OFF_CKV], pr[:, OFF_CKV:OFF_KR]
        p_kr = pr[:, OFF_KR:OFF_SG]
        (cq,) = rowwise(t + 'q_norm', f_rms, [p_q], [p['q_g']], [], [Q_LORA], TM)
        (ckv,) = rowwise(t + 'kv_norm', f_rms, [p_ckv], [p['kv_g']], [], [KV_LORA], TM)
        qq = matmul(t + 'uq', cq, p['w_uq'], 'nn')
        kv = matmul(t + 'ukv', ckv, p['w_ukv'], 'nn')
        qh, kh = rowwise(t + 'rope', f_rope, [qq, kv, p_kr, ct, st], [], [], [MLA_HEADS * HEAD_PAD] * 2, TM)
        ya = attention_fwd(t + 'attn', qh, kh, kv, B, S, TQ)
        sv.update(p_q=p_q, p_ckv=p_ckv, p_kr=p_kr, cq=cq, ckv=ckv, qq=qq, kv=kv, qh=qh, kh=kh, ya=ya)
        p_sg = pr[:, OFF_SG:OFF_RW]
        (yb,) = rowwise(t + 'sg', f_sg, [p_sg], [p['sg_g'], p['sg_b'], p['sg_w'], p['sg_bias']], [], [SG_DIM], TM)
        sv.update(p_sg=p_sg, yb=yb)
        z = pr[:, OFF_RW:OFF_GATE]
        zp, zn = shift_prev(z), shift_next(z)
        rw_par = [p['mu'], p['w0'], p['w2'], p['a0'], p['a2'], p['g2'], p['k_k'], p['k_a']]
        r_, v_, decay, kdir, kk, bdir, g_ = rowwise(
            t + 'rw_pre', f_rw_pre, [z, zp, zn], rw_par, [ones],
            [RWKV_DIM, RWKV_DIM, 2 * RWKV_DIM, 2 * RWKV_DIM, RWKV_DIM, 2 * RWKV_DIM, RWKV_DIM], TM)
        sc = dict(w=to_j(decay), k=to_j(kdir), b=to_j(bdir), kk=to_j(kk), r=to_j(r_), v=to_i(v_))
        ys, sp = scan_fwd(t + 'scan', sc['w'], sc['k'], sc['b'], sc['kk'], sc['r'], sc['v'])
        yd = from_i(ys)
        y0, y1 = yd[0].reshape(N, -1), flip(yd[1]).reshape(N, -1)
        (yc,) = rowwise(t + 'rw_post', f_rw_post, [y0, y1, r_, v_, kdir, g_], [p['ln_g'], p['ln_b'], p['r_k']],
                        [ones], [RWKV_DIM], TM)
        sv.update(z=z, zp=zp, zn=zn, r=r_, v=v_, kdir=kdir, g=g_, sc=sc, sp=sp, y0=y0, y1=y1, yc=yc)
        p_gate = pr[:, OFF_GATE:]
        b0 = matmul(t + 'br0', ya, p['wb0'], 'nn')
        b1 = matmul(t + 'br1', yb, p['wb1'], 'nn')
        b2 = matmul(t + 'br2', yc, p['wb2'], 'nn')
        (merged,) = rowwise(t + 'merge', f_merge, [p_gate, b0, b1, b2], [p['gate_b']], [], [D], TM)
        x2 = matmul(t + 'out_proj', merged, p['w_out'], 'nn', add=xc)
        sv.update(p_gate=p_gate, b0=b0, b1=b1, b2=b2, merged=merged, x2=x2)
        (h2,) = rowwise(t + 'ffn_norm', f_rms, [x2], [p['ffn_g']], [], [D], TM)
        au = matmul(t + 'ffn_in', h2, p['w_gu'], 'nn')
        (act,) = rowwise(t + 'swiglu', f_swiglu, [au], [], [], [D_FF], TM)
        xc = matmul(t + 'ffn_out', act, p['w_down'], 'nn', add=x2)
        sv.update(h2=h2, au=au, act=act)
        saved.append(sv)

    loss_part, dx, d_final_g = loss_head(xc, loss_target.reshape(N, D), row(full['final_norm_g']), TM)
    loss = lax.psum(loss_part[0, 0], ("x", "y", "c"))

    G = {n: [None] * DEPTH for n in WEIGHTS if n != 'final_norm_g'}
    for l in reversed(range(DEPTH)):
        p, sv = LW[l], saved[l]
        t = 'l%d_bwd_' % l
        d_act = matmul(t + 'ffn_out_dx', dx, p['w_down'], 'nt')
        G['w_ffn_down'][l] = matmul(t + 'ffn_out_dw', sv['act'], dx, 'tn')
        (d_au,), _ = rowwise_bwd(t + 'swiglu', f_swiglu, [sv['au']], [], [], [[d_act]], TMH)
        d_h2 = matmul(t + 'ffn_in_dx', d_au, p['w_gu'], 'nt')
        d_wgu = matmul(t + 'ffn_in_dw', sv['h2'], d_au, 'tn')
        G['w_ffn_gate'][l], G['w_ffn_up'][l] = d_wgu[:, :D_FF], d_wgu[:, D_FF:]
        (dx2,), (dg,) = rowwise_bwd(t + 'ffn_norm', f_rms, [sv['x2']], [p['ffn_g']], [], [[d_h2]], TM, extra=[(0, dx)])
        G['ffn_norm_g'][l] = dg.reshape(-1)
        d_merged = matmul(t + 'out_proj_dx', dx2, p['w_out'], 'nt')
        G['w_out'][l] = matmul(t + 'out_proj_dw', sv['merged'], dx2, 'tn')
        (d_pgate, d_b0, d_b1, d_b2), (d_gate_b,) = rowwise_bwd(
            t + 'merge', f_merge, [sv['p_gate'], sv['b0'], sv['b1'], sv['b2']], [p['gate_b']], [], [[d_merged]], TM)
        G['gate_b'][l] = d_gate_b.reshape(3, D)
        d_ya = matmul(t + 'br0_dx', d_b0, p['wb0'], 'nt')
        d_yb = matmul(t + 'br1_dx', d_b1, p['wb1'], 'nt')
        d_yc = matmul(t + 'br2_dx', d_b2, p['wb2'], 'nt')
        d_wb0 = matmul(t + 'br0_dw', sv['ya'], d_b0, 'tn').reshape(MLA_HEADS, HEAD_PAD, D)[:, QK_NOPE:].reshape(-1, D)
        G['w_branch'][l] = jnp.stack([d_wb0, matmul(t + 'br1_dw', sv['yb'], d_b1, 'tn'),
                                      matmul(t + 'br2_dw', sv['yc'], d_b2, 'tn')])
        (d_y0, _d_y1, d_r1, d_v1, d_kdir1, d_g), (d_ln_g, d_ln_b, d_r_k) = rowwise_bwd(
            t + 'rw_post', f_rw_post, [sv['y0'], sv['y1'], sv['r'], sv['v'], sv['kdir'], sv['g']],
            [p['ln_g'], p['ln_b'], p['r_k']], [ones], [[d_yc]], TMH)
        G['rw_ln_g'][l], G['rw_ln_b'][l] = d_ln_g.reshape(-1), d_ln_b.reshape(-1)
        G['rw_r_k'][l] = d_r_k.reshape(RWKV_HEADS, RWKV_HEAD)
        sc = sv['sc']
        s_dw, s_dk, s_db, s_dkk, s_dr, s_dv = scan_bwd(
            t + 'scan', sc['w'], sc['k'], sc['b'], sc['kk'], sc['r'], sc['v'], sv['sp'], to_i(d_y0))
        rw_par = [p['mu'], p['w0'], p['w2'], p['a0'], p['a2'], p['g2'], p['k_k'], p['k_a']]
        d_outs = [[d_r1] + parts_nodir(from_j(s_dr)), [d_v1] + parts_nodir([from_i(s_dv)]),
                  parts_dir(from_j(s_dw)), [d_kdir1] + parts_dir(from_j(s_dk)), parts_nodir(from_j(s_dkk)),
                  parts_dir(from_j(s_db)), [d_g]]
        (d_z, d_zp, d_zn), d_rw = rowwise_bwd(
            t + 'rw_pre', f_rw_pre, [sv['z'], sv['zp'], sv['zn']], rw_par, [ones], d_outs, TMH)
        (d_prw,) = rowwise(t + 'shift_sum', f_add3, [d_z, shift_next(d_zp), shift_prev(d_zn)], [], [], [RWKV_IN], TM)
        G['rw_mu'][l] = d_rw[0].reshape(-1)
        G['rw_w0'][l] = d_rw[1].reshape(2, RWKV_DIM)
        G['rw_w2'][l] = jnp.stack([d_rw[2][:64, :RWKV_DIM], d_rw[2][64:, RWKV_DIM:]])
        G['rw_a0'][l] = d_rw[3].reshape(2, RWKV_DIM)
        G['rw_a2'][l] = jnp.stack([d_rw[4][:64, :RWKV_DIM], d_rw[4][64:, RWKV_DIM:]])
        G['rw_g2'][l] = d_rw[5]
        G['rw_k_k'][l], G['rw_k_a'][l] = d_rw[6].reshape(-1), d_rw[7].reshape(-1)
        (d_psg,), (d_sg_g, d_sg_b, d_sg_w, d_sg_bias) = rowwise_bwd(
            t + 'sg', f_sg, [sv['p_sg']], [p['sg_g'], p['sg_b'], p['sg_w'], p['sg_bias']], [], [[d_yb]], TMH)
        G['sg_ln_g'][l], G['sg_ln_b'][l], G['sg_w'][l] = d_sg_g.reshape(-1), d_sg_b.reshape(-1), d_sg_w
        G['sg_b'][l] = d_sg_bias.reshape(SG_CHUNK, SG_GROUPS, SG_DIM // SG_GROUPS).sum(-1).T
        d_qh, d_kh, d_kvv = attention_bwd(t + 'attn', sv['qh'], sv['kh'], sv['kv'], d_ya, B, S, TQ)
        (d_qq, d_kv, d_pkr), _ = rowwise_bwd(
            t + 'rope', f_rope, [sv['qq'], sv['kv'], sv['p_kr'], ct, st], [], [], [[d_qh], [d_kh]], TM,
            n_row_diff=3, extra=[(1, d_kvv)])
        d_cq = matmul(t + 'uq_dx', d_qq, p['w_uq'], 'nt')
        d_wuq = matmul(t + 'uq_dw', sv['cq'], d_qq, 'tn')
        g1, g2_ = _unpermute_full(d_wuq, wuq_src, wuq_sgn, MLA_HEADS * (QK_NOPE + QK_ROPE))
        G['w_uq'][l] = g1 + g2_
        d_ckv = matmul(t + 'ukv_dx', d_kv, p['w_ukv'], 'nt')
        G['w_ukv'][l] = matmul(t + 'ukv_dw', sv['ckv'], d_kv, 'tn')
        (d_pq,), (dg,) = rowwise_bwd(t + 'q_norm', f_rms, [sv['p_q']], [p['q_g']], [], [[d_cq]], TM)
        G['q_norm_g'][l] = dg.reshape(-1)
        (d_pckv,), (dg,) = rowwise_bwd(t + 'kv_norm', f_rms, [sv['p_ckv']], [p['kv_g']], [], [[d_ckv]], TM)
        G['kv_norm_g'][l] = dg.reshape(-1)
        d_pr = jnp.concatenate([d_pq, d_pckv, d_pkr, d_psg, d_prw, d_pgate], axis=1)
        d_h = matmul(t + 'in_proj_dx', d_pr, p['w_in'], 'nt')
        d_win = matmul(t + 'in_proj_dw', sv['h'], d_pr, 'tn')
        g1, g2_ = _unpermute_full(d_win, win_src, win_sgn, N_IN)
        kr0 = Q_LORA + KV_LORA
        G['w_in'][l] = g1.at[:, kr0:kr0 + QK_ROPE].add(g2_[:, kr0:kr0 + QK_ROPE])
        (dx,), (dg,) = rowwise_bwd(t + 'attn_norm', f_rms, [sv['x']], [p['attn_g']], [], [[d_h]], TM, extra=[(0, dx2)])
        G['attn_norm_g'][l] = dg.reshape(-1)

    grads = {n: jnp.stack(G[n]) for n in G}
    grads['final_norm_g'] = d_final_g.reshape(-1)
    grad_x = dx.reshape(B, S, D)

    per_shard = []
    for q in range(4):
        sl = []
        for n in SHARDED:
            ax = SHARD_AXIS[n]
            w = W[n].shape[ax]
            sl.append(lax.slice_in_dim(grads[n], q * w, (q + 1) * w, axis=ax))
        per_shard.append(_pack(sl, 16))
    gpack = jnp.stack(per_shard).reshape(4, 2, R // 2, PACK_C)
    half_sum = sum_slots("sum_sharded", scatter_partials(gpack))
    g_shard = sibling_join(half_sum).reshape(R, PACK_C)
    rep_shapes = [W[n].shape for n in REPLICATED]
    rpack = _pack([grads[n] for n in REPLICATED], 8)
    g_rep = sum_slots("sum_replicated", gather_all(rpack))

    d_s, m_s, v_s = adamw("adamw_sharded", wpack, g_shard, _pack([M1[n] for n in SHARDED], 16),
                          _pack([M2[n] for n in SHARDED], 16))
    d_r, m_r, v_r = adamw("adamw_replicated", _pack([W[n] for n in REPLICATED], 8), g_rep,
                          _pack([M1[n] for n in REPLICATED], 8), _pack([M2[n] for n in REPLICATED], 8))
    outs = {}
    for key, bs, br in (('grad', g_shard, g_rep), ('delta', d_s, d_r), ('new_m', m_s, m_r), ('new_v', v_s, v_r)):
        for n, a in zip(SHARDED, _unpack(bs, shard_shapes)):
            outs[key, n] = a
        for n, a in zip(REPLICATED, _unpack(br, rep_shapes)):
            outs[key, n] = a
    return (loss, grad_x, *[outs['grad', n] for n in WEIGHTS], *[outs['delta', n] for n in WEIGHTS],
            *[outs['new_m', n] for n in WEIGHTS], *[outs['new_v', n] for n in WEIGHTS])
```

```python
import functools
import math

import numpy as np
import jax
import jax.numpy as jnp
from jax import lax
from jax.experimental import pallas as pl
from jax.experimental.pallas import tpu as pltpu

F32 = jnp.float32
BF16 = jnp.bfloat16

DEPTH = 2
MLA_HEADS = 8
Q_LORA = 384
KV_LORA = 256
QK_NOPE = 64
QK_ROPE = 32
V_HEAD = 64
ROPE_THETA = 10000.0
SG_GROUPS = 8
SG_DIM = 512
SG_CHUNK = 128
RWKV_HEADS = 8
RWKV_HEAD = 64
RWKV_DIM = 512
GN_EPS = 64e-5
NORM_EPS = 1e-6
D_FF = 2816
RWKV_IN = 1920
N_IN = 6688
ADAM_LR, ADAM_B1, ADAM_B2, ADAM_EPS, ADAM_WD, ADAM_STEP = 0.001, 0.9, 0.999, 1e-08, 0.01, 10

LANES = 128
HEAD_PAD = 128
VMEM_LIMIT = 56 * 1024 * 1024
MESH = pl.DeviceIdType.MESH

WEIGHTS = ['attn_norm_g', 'w_in', 'gate_b', 'q_norm_g', 'w_uq', 'kv_norm_g', 'w_ukv', 'sg_ln_g', 'sg_ln_b', 'sg_w',
           'sg_b', 'rw_mu', 'rw_w0', 'rw_w2', 'rw_a0', 'rw_a2', 'rw_g2', 'rw_k_k', 'rw_k_a', 'rw_r_k', 'rw_ln_g',
           'rw_ln_b', 'w_branch', 'w_out', 'ffn_norm_g', 'w_ffn_gate', 'w_ffn_up', 'w_ffn_down', 'final_norm_g']
SHARD_AXIS = {'w_in': 2, 'gate_b': 2, 'w_uq': 2, 'w_ukv': 2, 'rw_w0': 2, 'rw_w2': 3, 'rw_a0': 2, 'rw_a2': 3,
              'rw_g2': 2, 'w_branch': 3, 'w_out': 1, 'w_ffn_gate': 2, 'w_ffn_up': 2, 'w_ffn_down': 1}
SHARDED = [n for n in WEIGHTS if n in SHARD_AXIS]
REPLICATED = [n for n in WEIGHTS if n not in SHARD_AXIS]
SMALL_SHARDED = ['gate_b', 'rw_w0', 'rw_a0']
MATMUL_SHARDED = [n for n in SHARDED if n not in SMALL_SHARDED]


def _params(sem=None):
    return pltpu.CompilerParams(dimension_semantics=sem, vmem_limit_bytes=VMEM_LIMIT)


def _pick(n, cands):
    for c in cands:
        if n % c == 0:
            return c
    return n


def _dot(a, b, dims):
    return lax.dot_general(a.astype(BF16), b.astype(BF16), (dims, ((), ())), preferred_element_type=F32)


def _nn(a, b):
    return _dot(a, b, ((1,), (0,)))


def _nt(a, b):
    return _dot(a, b, ((1,), (1,)))


def _tn(a, b):
    return _dot(a, b, ((0,), (0,)))


@jax.custom_vjp
def mm(a, b):
    return _nn(a, b)


mm.defvjp(lambda a, b: (_nn(a, b), (a, b)), lambda res, g: (_nt(g, res[1]), _tn(res[0], g)))


@jax.custom_vjp
def mm_nt(a, b):
    return _nt(a, b)


mm_nt.defvjp(lambda a, b: (_nt(a, b), (a, b)), lambda res, g: (_nn(g, res[1]), _tn(g, res[0])))


def _seg_raw(x, ones):
    hi = x.astype(BF16)
    lo = (x - hi.astype(F32)).astype(BF16)
    d = (((1,), (0,)), ((), ()))
    return (lax.dot_general(hi, ones, d, preferred_element_type=F32)
            + lax.dot_general(lo, ones, d, preferred_element_type=F32))


@jax.custom_vjp
def segsum(x, ones):
    return _seg_raw(x, ones)


segsum.defvjp(lambda x, ones: (_seg_raw(x, ones), ones),
              lambda ones, g: (_seg_raw(g, ones), jnp.zeros_like(ones)))


def _sigmoid(x):
    return 0.5 * (jnp.tanh(0.5 * x) + 1.0)


def _rms(x, g):
    return x * lax.rsqrt(jnp.mean(x * x, axis=-1, keepdims=True) + NORM_EPS) * g


def matmul(name, a, b, mode, add=None):
    if mode == 'nn':
        (M, K), (_, N) = a.shape, b.shape
    elif mode == 'nt':
        (M, K), (N, _) = a.shape, b.shape
    else:
        (K, M), (_, N) = a.shape, b.shape
    tm = _pick(M, (512, 384, 256, 128))
    tn = _pick(N, (512, 384, 256, 128))
    tk = _pick(K, (1024, 512, 384, 256, 128))
    nk = K // tk
    dims = {'nn': ((1,), (0,)), 'nt': ((1,), (1,)), 'tn': ((0,), (0,))}[mode]
    a_spec = pl.BlockSpec((tk, tm), lambda i, j, k: (k, i)) if mode == 'tn' else pl.BlockSpec((tm, tk), lambda i, j, k: (i, k))
    b_spec = pl.BlockSpec((tn, tk), lambda i, j, k: (j, k)) if mode == 'nt' else pl.BlockSpec((tk, tn), lambda i, j, k: (k, j))
    o_spec = pl.BlockSpec((tm, tn), lambda i, j, k: (i, j))
    has_add = add is not None

    def body(*refs):
        if has_add:
            a_ref, b_ref, add_ref, o_ref, acc = refs
        else:
            a_ref, b_ref, o_ref, acc = refs
        k = pl.program_id(2)

        @pl.when(k == 0)
        def _():
            acc[...] = jnp.zeros_like(acc)

        acc[...] += _dot(a_ref[...], b_ref[...], dims)

        @pl.when(k == nk - 1)
        def _():
            o_ref[...] = acc[...] + add_ref[...] if has_add else acc[...]

    ins = [a, b] + ([add] if has_add else [])
    specs = [a_spec, b_spec] + ([o_spec] if has_add else [])
    return pl.pallas_call(
        body, name=name, grid=(M // tm, N // tn, nk), in_specs=specs, out_specs=o_spec,
        out_shape=jax.ShapeDtypeStruct((M, N), F32), scratch_shapes=[pltpu.VMEM((tm, tn), F32)],
        compiler_params=_params(("parallel", "parallel", "arbitrary")))(*ins)


def _full_spec(p):
    nd = p.ndim
    return pl.BlockSpec(p.shape, lambda i, _nd=nd: (0,) * _nd)


def rowwise(name, fn, rows, params, consts, out_widths, tm):
    N = rows[0].shape[0]
    nr, npar, nc = len(rows), len(params), len(consts)

    def body(*refs):
        vals = [r[...] for r in refs[:nr + npar + nc]]
        res = fn(*vals)
        for o, v in zip(refs[nr + npar + nc:], res):
            o[...] = v

    in_specs = ([pl.BlockSpec((tm, r.shape[1]), lambda i: (i, 0)) for r in rows]
                + [_full_spec(p) for p in list(params) + list(consts)])
    out_specs = [pl.BlockSpec((tm, w), lambda i: (i, 0)) for w in out_widths]
    return pl.pallas_call(
        body, name=name, grid=(N // tm,), in_specs=in_specs, out_specs=out_specs,
        out_shape=[jax.ShapeDtypeStruct((N, w), F32) for w in out_widths],
        compiler_params=_params(("parallel",)))(*rows, *params, *consts)


def rowwise_bwd(name, fn, rows, params, consts, d_outs, tm, n_row_diff=None, extra=()):
    N = rows[0].shape[0]
    nr, npar, nc = len(rows), len(params), len(consts)
    nd = nr if n_row_diff is None else n_row_diff
    counts = [len(p) for p in d_outs]
    flat_d = [a for parts in d_outs for a in parts]
    nflat, nex = len(flat_d), len(extra)

    def body(*refs):
        pos = 0
        row_v = [r[...] for r in refs[pos:pos + nr]]; pos += nr
        par_v = [r[...] for r in refs[pos:pos + npar]]; pos += npar
        con_v = [r[...] for r in refs[pos:pos + nc]]; pos += nc
        d_refs = refs[pos:pos + nflat]; pos += nflat
        ex_refs = refs[pos:pos + nex]; pos += nex
        drow_refs = refs[pos:pos + nd]; pos += nd
        dpar_refs = refs[pos:pos + npar]

        def f(*diff):
            return fn(*diff[:nd], *row_v[nd:], *diff[nd:], *con_v)

        _, vjp = jax.vjp(f, *row_v[:nd], *par_v)
        cts, q = [], 0
        for c in counts:
            g = d_refs[q][...]
            for t in range(1, c):
                g = g + d_refs[q + t][...]
            cts.append(g)
            q += c
        grads = vjp(tuple(cts))
        drow = list(grads[:nd])
        for (idx, _), r in zip(extra, ex_refs):
            drow[idx] = drow[idx] + r[...]
        for o, v in zip(drow_refs, drow):
            o[...] = v

        @pl.when(pl.program_id(0) == 0)
        def _():
            for o in dpar_refs:
                o[...] = jnp.zeros_like(o)

        for o, v in zip(dpar_refs, grads[nd:]):
            o[...] += v

    ex_arrs = [a for _, a in extra]
    in_specs = ([pl.BlockSpec((tm, r.shape[1]), lambda i: (i, 0)) for r in rows]
                + [_full_spec(p) for p in list(params) + list(consts)]
                + [pl.BlockSpec((tm, a.shape[1]), lambda i: (i, 0)) for a in flat_d + ex_arrs])
    out_specs = ([pl.BlockSpec((tm, r.shape[1]), lambda i: (i, 0)) for r in rows[:nd]]
                 + [_full_spec(p) for p in params])
    out_shape = ([jax.ShapeDtypeStruct(r.shape, F32) for r in rows[:nd]]
                 + [jax.ShapeDtypeStruct(p.shape, F32) for p in params])
    res = pl.pallas_call(
        body, name=name, grid=(N // tm,), in_specs=in_specs, out_specs=out_specs, out_shape=out_shape,
        compiler_params=_params(("arbitrary",)))(*rows, *params, *consts, *flat_d, *ex_arrs)
    return list(res[:nd]), list(res[nd:])


def f_rms(x, g):
    return (_rms(x, g),)


def f_rope(qq, kv, krr, ct, st):
    hw = MLA_HEADS * HEAD_PAD
    c8 = jnp.tile(ct, (1, MLA_HEADS))
    s8 = jnp.tile(st, (1, MLA_HEADS))
    q = qq[:, :hw] * c8 + qq[:, hw:] * s8
    kr = krr[:, :HEAD_PAD] * ct + krr[:, HEAD_PAD:] * st
    lane = lax.broadcasted_iota(jnp.int32, kv.shape, 1) % HEAD_PAD
    k = jnp.where(lane < QK_NOPE, kv, jnp.tile(kr, (1, MLA_HEADS)))
    return q, k


def f_sg(p, ln_g, ln_b, w, bias):
    z = 0.5 * p * (1.0 + jnp.tanh(0.7978845608028654 * (p + 0.044715 * p * p * p)))
    u, v = z[:, :SG_DIM], z[:, SG_DIM:]
    mu = jnp.mean(v, axis=-1, keepdims=True)
    var = jnp.mean(jnp.square(v - mu), axis=-1, keepdims=True)
    v = (v - mu) * lax.rsqrt(var + 1e-5) * ln_g + ln_b
    lane = lax.broadcasted_iota(jnp.int32, (SG_CHUNK, LANES), 1)
    outs = []
    for c in range(p.shape[0] // SG_CHUNK):
        vc = v[c * SG_CHUNK:(c + 1) * SG_CHUNK]
        cols = []
        for m in range(SG_DIM // LANES):
            blk = vc[:, m * LANES:(m + 1) * LANES]
            cols.append(jnp.where(lane < 64, mm(w[2 * m], blk), mm(w[2 * m + 1], blk)))
        outs.append(jnp.concatenate(cols, axis=1) + bias)
    mixed = outs[0] if len(outs) == 1 else jnp.concatenate(outs, axis=0)
    return (u * mixed,)


def f_rw_pre(z, zp, zn, mu, w0, w2, a0, a2, g2, k_k, k_a, ones):
    z = z + mu * (0.5 * (zp + zn) - z)
    C = RWKV_DIM
    r, k, v = z[:, :C], z[:, C:2 * C], z[:, 2 * C:3 * C]
    wl, al, gl = z[:, 3 * C:3 * C + 128], z[:, 3 * C + 128:3 * C + 256], z[:, 3 * C + 256:]
    w = w0 + mm(jnp.tanh(wl), w2)
    decay = jnp.exp(-0.6065306597126334 * _sigmoid(w))
    a = _sigmoid(a0 + mm(al, a2))
    g = mm(_sigmoid(gl), g2)
    kk = k * k_k
    kk = kk / jnp.maximum(jnp.sqrt(segsum(kk * kk, ones)), 1e-12)
    k2 = jnp.concatenate([k, k], axis=1)
    kdir = k2 * (1.0 + (a - 1.0) * jnp.concatenate([k_a, k_a], axis=1))
    bdir = jnp.concatenate([kk, kk], axis=1) * a
    return r, v, decay, kdir, kk, bdir, g


def f_rw_post(y0, y1, r, v, kdir, g, ln_g, ln_b, r_k, ones):
    y = y0 + y1
    mean = segsum(y, ones) * (1.0 / RWKV_HEAD)
    yc = y - mean
    var = segsum(yc * yc, ones) * (1.0 / RWKV_HEAD)
    y = yc * lax.rsqrt(var + GN_EPS) * ln_g + ln_b
    C = RWKV_DIM
    bonus = segsum(r * kdir[:, :C] * r_k, ones) + segsum(r * kdir[:, C:] * r_k, ones)
    return ((y + bonus * v) * g,)


def f_merge(pg, b0, b1, b2, gate_b):
    D = b0.shape[1]
    gt = _sigmoid(pg + gate_b)
    return (gt[:, :D] * b0 + gt[:, D:2 * D] * b1 + gt[:, 2 * D:] * b2,)


def f_swiglu(au):
    a, u = au[:, :D_FF], au[:, D_FF:]
    return (a * _sigmoid(a) * u,)


def f_add3(a, b, c):
    return (a + b + c,)


def loss_head(x, tgt, g, tm):
    N, D = x.shape

    def body(x_ref, t_ref, g_ref, loss_ref, dx_ref, dg_ref):
        t = t_ref[...]

        def f(xv, gv):
            err = _rms(xv, gv) - t
            return 0.5 * jnp.sum(jnp.mean(err * err, axis=-1, keepdims=True))

        val, (dx, dg) = jax.value_and_grad(f, argnums=(0, 1))(x_ref[...], g_ref[...])
        dx_ref[...] = dx

        @pl.when(pl.program_id(0) == 0)
        def _():
            loss_ref[...] = jnp.zeros_like(loss_ref)
            dg_ref[...] = jnp.zeros_like(dg_ref)

        loss_ref[...] += jnp.full(loss_ref.shape, val, F32)
        dg_ref[...] += dg

    row = pl.BlockSpec((tm, D), lambda i: (i, 0))
    return pl.pallas_call(
        body, name="loss_head", grid=(N // tm,), in_specs=[row, row, _full_spec(g)],
        out_specs=[pl.BlockSpec((1, LANES), lambda i: (0, 0)), row, _full_spec(g)],
        out_shape=[jax.ShapeDtypeStruct((1, LANES), F32), jax.ShapeDtypeStruct((N, D), F32),
                   jax.ShapeDtypeStruct(g.shape, F32)],
        compiler_params=_params(("arbitrary",)))(x, tgt, g)


ATT_SCALE = float((QK_NOPE + QK_ROPE) ** -0.5)


def _attn_block(q, k, kv):
    s = mm_nt(q, k) * ATT_SCALE
    m = lax.stop_gradient(jnp.max(s, axis=-1, keepdims=True))
    e = jnp.exp(s - m)
    p = e / jnp.sum(e, axis=-1, keepdims=True)
    return mm(p, kv)


def attention_fwd(name, q, k, kv, B, S, tq):
    nq = S // tq
    qspec = pl.BlockSpec((tq, HEAD_PAD), lambda b, h, i: (b * nq + i, h))
    kspec = pl.BlockSpec((S, HEAD_PAD), lambda b, h, i: (b, h))

    def body(q_ref, k_ref, kv_ref, o_ref):
        o_ref[...] = _attn_block(q_ref[...], k_ref[...], kv_ref[...])

    return pl.pallas_call(
        body, name=name, grid=(B, MLA_HEADS, nq), in_specs=[qspec, kspec, kspec], out_specs=qspec,
        out_shape=jax.ShapeDtypeStruct(q.shape, F32),
        compiler_params=_params(("parallel", "parallel", "arbitrary")))(q, k, kv)


def attention_bwd(name, q, k, kv, do, B, S, tq):
    nq = S // tq
    qspec = pl.BlockSpec((tq, HEAD_PAD), lambda b, h, i: (b * nq + i, h))
    kspec = pl.BlockSpec((S, HEAD_PAD), lambda b, h, i: (b, h))

    def body(q_ref, k_ref, kv_ref, do_ref, dq_ref, dk_ref, dkv_ref):
        _, vjp = jax.vjp(_attn_block, q_ref[...], k_ref[...], kv_ref[...])
        dq, dk, dkv = vjp(do_ref[...])
        dq_ref[...] = dq

        @pl.when(pl.program_id(2) == 0)
        def _():
            dk_ref[...] = jnp.zeros_like(dk_ref)
            dkv_ref[...] = jnp.zeros_like(dkv_ref)

        dk_ref[...] += dk
        dkv_ref[...] += dkv

    sh = jax.ShapeDtypeStruct(q.shape, F32)
    return pl.pallas_call(
        body, name=name, grid=(B, MLA_HEADS, nq), in_specs=[qspec, kspec, kspec, qspec],
        out_specs=[qspec, kspec, kspec], out_shape=[sh, sh, sh],
        compiler_params=_params(("parallel", "parallel", "arbitrary")))(q, k, kv, do)


SCAN_TC = 8
SCAN_UNROLL = 16


def _jloop(n, body, init):
    def outer(o, c):
        for u in range(SCAN_UNROLL):
            c = body(o * SCAN_UNROLL + u, c)
        return c

    return lax.fori_loop(0, n // SCAN_UNROLL, outer, init)


def scan_fwd(name, w, k, b, kk, r, v):
    T, J, L = w.shape
    Ip = v.shape[1]
    jspec = pl.BlockSpec((SCAN_TC, J, L), lambda g: (g, 0, 0))
    ispec = pl.BlockSpec((SCAN_TC, Ip, L), lambda g: (g, 0, 0))
    sspec = pl.BlockSpec((SCAN_TC, J, Ip, L), lambda g: (g, 0, 0, 0))

    def body(w_ref, k_ref, b_ref, kk_ref, r_ref, v_ref, y_ref, sp_ref, s_ref):
        @pl.when(pl.program_id(0) == 0)
        def _():
            s_ref[...] = jnp.zeros_like(s_ref)

        def row(ref, tt, j):
            return jnp.broadcast_to(ref[tt, pl.ds(j, 1), :], (Ip, L))

        def step(tt, carry):
            def p1(j, sa):
                s = s_ref[j]
                sp_ref[tt, j] = s
                return sa + s * row(kk_ref, tt, j)

            sa = _jloop(J, p1, jnp.zeros((Ip, L), F32))
            vt = v_ref[tt]

            def p2(j, y):
                s = s_ref[j] * row(w_ref, tt, j) - sa * row(b_ref, tt, j) + vt * row(k_ref, tt, j)
                s_ref[j] = s
                return y + s * row(r_ref, tt, j)

            y_ref[tt] = _jloop(J, p2, jnp.zeros((Ip, L), F32))
            return carry

        lax.fori_loop(0, SCAN_TC, step, 0)

    return pl.pallas_call(
        body, name=name, grid=(T // SCAN_TC,), in_specs=[jspec] * 5 + [ispec], out_specs=[ispec, sspec],
        out_shape=[jax.ShapeDtypeStruct((T, Ip, L), F32), jax.ShapeDtypeStruct((T, J, Ip, L), F32)],
        scratch_shapes=[pltpu.VMEM((J, Ip, L), F32)],
        compiler_params=_params(("arbitrary",)))(w, k, b, kk, r, v)


def scan_bwd(name, w, k, b, kk, r, v, sp, dy):
    T, J, L = w.shape
    Ip = v.shape[1]
    nT = T // SCAN_TC
    jspec = pl.BlockSpec((SCAN_TC, J, L), lambda g: (nT - 1 - g, 0, 0))
    ispec = pl.BlockSpec((SCAN_TC, Ip, L), lambda g: (nT - 1 - g, 0, 0))
    sspec = pl.BlockSpec((SCAN_TC, J, Ip, L), lambda g: (nT - 1 - g, 0, 0, 0))

    def body(w_ref, k_ref, b_ref, kk_ref, r_ref, v_ref, sp_ref, dy_ref,
             dw_ref, dk_ref, db_ref, dkk_ref, dr_ref, dv_ref, ds_ref):
        @pl.when(pl.program_id(0) == 0)
        def _():
            ds_ref[...] = jnp.zeros_like(ds_ref)

        def row(ref, tt, j):
            return jnp.broadcast_to(ref[tt, pl.ds(j, 1), :], (Ip, L))

        def rsum(x):
            return jnp.sum(x, axis=0, keepdims=True)

        def step(n, carry):
            tt = SCAN_TC - 1 - n
            dyt = dy_ref[tt]
            vt = v_ref[tt]

            def p1(j, c):
                sa, dsa, dv = c
                ds = ds_ref[j] + dyt * row(r_ref, tt, j)
                ds_ref[j] = ds
                return (sa + sp_ref[tt, j] * row(kk_ref, tt, j), dsa - ds * row(b_ref, tt, j),
                        dv + ds * row(k_ref, tt, j))

            z = jnp.zeros((Ip, L), F32)
            sa, dsa, dv = _jloop(J, p1, (z, z, z))
            dv_ref[tt] = dv

            def p2(j, c):
                ds = ds_ref[j]
                s0 = sp_ref[tt, j]
                wj, kkj = row(w_ref, tt, j), row(kk_ref, tt, j)
                s1 = s0 * wj - sa * row(b_ref, tt, j) + vt * row(k_ref, tt, j)
                dr_ref[tt, pl.ds(j, 1), :] = rsum(s1 * dyt)
                dk_ref[tt, pl.ds(j, 1), :] = rsum(ds * vt)
                db_ref[tt, pl.ds(j, 1), :] = -rsum(ds * sa)
                dw_ref[tt, pl.ds(j, 1), :] = rsum(ds * s0)
                dkk_ref[tt, pl.ds(j, 1), :] = rsum(s0 * dsa)
                ds_ref[j] = ds * wj + dsa * kkj
                return c

            _jloop(J, p2, 0)
            return carry

        lax.fori_loop(0, SCAN_TC, step, 0)

    jsh = jax.ShapeDtypeStruct((T, J, L), F32)
    return pl.pallas_call(
        body, name=name, grid=(nT,), in_specs=[jspec] * 5 + [ispec, sspec, ispec],
        out_specs=[jspec] * 5 + [ispec], out_shape=[jsh] * 5 + [jax.ShapeDtypeStruct((T, Ip, L), F32)],
        scratch_shapes=[pltpu.VMEM((J, Ip, L), F32)],
        compiler_params=_params(("arbitrary",)))(w, k, b, kk, r, v, sp, dy)


def flip_tokens(name, x, B, S):
    N, C = x.shape
    tb = _pick(S, (256, 128))
    nb = S // tb

    def body(x_ref, o_ref):
        r = lax.broadcasted_iota(jnp.int32, (tb, tb), 0)
        c = lax.broadcasted_iota(jnp.int32, (tb, tb), 1)
        anti = jnp.where(r + c == tb - 1, 1.0, 0.0).astype(BF16)
        xv = x_ref[...]
        hi = xv.astype(BF16)
        r1 = xv - hi.astype(F32)
        mid = r1.astype(BF16)
        lo = (r1 - mid.astype(F32)).astype(BF16)
        dot = lambda p: lax.dot_general(anti, p, (((1,), (0,)), ((), ())), preferred_element_type=F32)
        o_ref[...] = (dot(hi) + dot(mid)) + dot(lo)

    return pl.pallas_call(
        body, name=name, grid=(B, nb), in_specs=[pl.BlockSpec((tb, C), lambda b, i: (b * nb + i, 0))],
        out_specs=pl.BlockSpec((tb, C), lambda b, i: (b * nb + nb - 1 - i, 0)),
        out_shape=jax.ShapeDtypeStruct((N, C), F32), compiler_params=_params(("parallel", "parallel")))(x)


def adamw(name, w, g, m, v):
    R, C = w.shape
    tr = _pick(R, (256, 128, 64, 32, 16, 8))
    c1 = 1.0 - ADAM_B1 ** ADAM_STEP
    c2 = 1.0 - ADAM_B2 ** ADAM_STEP

    def body(w_ref, g_ref, m_ref, v_ref, d_ref, nm_ref, nv_ref):
        gv = g_ref[...]
        nm = ADAM_B1 * m_ref[...] + (1.0 - ADAM_B1) * gv
        nv = ADAM_B2 * v_ref[...] + (1.0 - ADAM_B2) * jnp.square(gv)
        d_ref[...] = -ADAM_LR * ((nm / c1) / (jnp.sqrt(nv / c2) + ADAM_EPS) + ADAM_WD * w_ref[...])
        nm_ref[...] = nm
        nv_ref[...] = nv

    spec = pl.BlockSpec((tr, C), lambda i: (i, 0))
    sh = jax.ShapeDtypeStruct((R, C), F32)
    return pl.pallas_call(body, name=name, grid=(R // tr,), in_specs=[spec] * 4, out_specs=[spec] * 3,
                          out_shape=[sh] * 3, compiler_params=_params(("parallel",)))(w, g, m, v)


def sum_slots(name, x):
    n, R, C = x.shape
    tr = _pick(R, (256, 128, 64, 32, 16, 8))

    def body(x_ref, o_ref):
        acc = x_ref[0]
        for s in range(1, n):
            acc = acc + x_ref[s]
        o_ref[...] = acc

    return pl.pallas_call(
        body, name=name, grid=(R // tr,), in_specs=[pl.BlockSpec((n, tr, C), lambda i: (0, i, 0))],
        out_specs=pl.BlockSpec((tr, C), lambda i: (i, 0)), out_shape=jax.ShapeDtypeStruct((R, C), F32),
        compiler_params=_params(("parallel",)))(x)


ANY = pl.BlockSpec(memory_space=pl.ANY)


def _xyc():
    return lax.axis_index("x"), lax.axis_index("y"), lax.axis_index("c")


def gather_shards(shard):
    _, R, C = shard.shape

    def body(x_ref, out_ref, send_sems, recv_sems, local_sem):
        x, y, c = _xyc()
        me, sibling = (x, y, c), (x, y, 1 - c)
        chips = [(1 - x, y), (x, 1 - y), (1 - x, 1 - y)]

        def cp(k, cx, cy, half, to, src=None):
            dst = out_ref.at[2 * cx + cy, half]
            return pltpu.make_async_remote_copy(
                src_ref=dst if src is None else src, dst_ref=dst, send_sem=send_sems.at[k],
                recv_sem=recv_sems.at[k], device_id=to, device_id_type=MESH)

        mine = pltpu.make_async_copy(x_ref, out_ref.at[2 * x + y], local_sem)
        mine.start()
        first = [cp(j, x, y, c, (*chip, c), src=x_ref.at[c]) for j, chip in enumerate(chips)]
        for f in first:
            f.start()
        passed = [cp(3 + j, *chip, c, sibling) for j, chip in enumerate(chips)]
        for j, chip in enumerate(chips):
            cp(j, *chip, c, me).wait_recv()
            passed[j].start()
        for j, chip in enumerate(chips):
            cp(3 + j, *chip, 1 - c, me).wait_recv()
        for f in first + passed:
            f.wait_send()
        mine.wait()

    return pl.pallas_call(
        body, name="gather_shards", in_specs=[ANY], out_specs=ANY,
        out_shape=jax.ShapeDtypeStruct((4, 2, R, C), shard.dtype),
        scratch_shapes=[pltpu.SemaphoreType.DMA((6,)), pltpu.SemaphoreType.DMA((6,)), pltpu.SemaphoreType.DMA])(shard)


FLIPS = [(0, 0, 1), (0, 1, 0), (0, 1, 1), (1, 0, 0), (1, 0, 1), (1, 1, 0), (1, 1, 1)]


def scatter_partials(g):
    _, _, R, C = g.shape

    def body(g_ref, out_ref, send_sems, recv_sems, local_sem):
        x, y, c = _xyc()
        me_idx = 4 * x + 2 * y + c
        mine = pltpu.make_async_copy(g_ref.at[2 * x + y, c], out_ref.at[me_idx], local_sem)
        mine.start()
        sends = []
        for k, (fx, fy, fc) in enumerate(FLIPS):
            px, py, pc = (x + fx) % 2, (y + fy) % 2, (c + fc) % 2
            s = pltpu.make_async_remote_copy(
                src_ref=g_ref.at[2 * px + py, pc], dst_ref=out_ref.at[me_idx], send_sem=send_sems.at[k],
                recv_sem=recv_sems.at[k], device_id=(px, py, pc), device_id_type=MESH)
            s.start()
            sends.append(s)
        for k, (fx, fy, fc) in enumerate(FLIPS):
            px, py, pc = (x + fx) % 2, (y + fy) % 2, (c + fc) % 2
            slot = out_ref.at[4 * px + 2 * py + pc]
            pltpu.make_async_remote_copy(
                src_ref=slot, dst_ref=slot, send_sem=send_sems.at[k], recv_sem=recv_sems.at[k],
                device_id=(px, py, pc), device_id_type=MESH).wait_recv()
        for s in sends:
            s.wait_send()
        mine.wait()

    return pl.pallas_call(
        body, name="scatter_partials", in_specs=[ANY], out_specs=ANY,
        out_shape=jax.ShapeDtypeStruct((8, R, C), g.dtype),
        scratch_shapes=[pltpu.SemaphoreType.DMA((7,)), pltpu.SemaphoreType.DMA((7,)), pltpu.SemaphoreType.DMA])(g)


def sibling_join(half):
    R, C = half.shape

    def body(h_ref, out_ref, send_sem, recv_sem, local_sem):
        x, y, c = _xyc()
        mine = pltpu.make_async_copy(h_ref, out_ref.at[c], local_sem)
        mine.start()
        s = pltpu.make_async_remote_copy(src_ref=h_ref, dst_ref=out_ref.at[c], send_sem=send_sem, recv_sem=recv_sem,
                                         device_id=(x, y, 1 - c), device_id_type=MESH)
        s.start()
        theirs = out_ref.at[1 - c]
        pltpu.make_async_remote_copy(src_ref=theirs, dst_ref=theirs, send_sem=send_sem, recv_sem=recv_sem,
                                     device_id=(x, y, 1 - c), device_id_type=MESH).wait_recv()
        s.wait_send()
        mine.wait()

    return pl.pallas_call(
        body, name="sibling_join", in_specs=[ANY], out_specs=ANY,
        out_shape=jax.ShapeDtypeStruct((2, R, C), half.dtype),
        scratch_shapes=[pltpu.SemaphoreType.DMA, pltpu.SemaphoreType.DMA, pltpu.SemaphoreType.DMA])(half)


def gather_all(name, block):
    R, C = block.shape

    def body(x_ref, out_ref, send_sems, recv_sems, local_sem):
        x, y, c = _xyc()
        mine = pltpu.make_async_copy(x_ref, out_ref.at[4 * x + 2 * y + c], local_sem)
        mine.start()
        sends = []
        for k, (fx, fy, fc) in enumerate(FLIPS):
            px, py, pc = (x + fx) % 2, (y + fy) % 2, (c + fc) % 2
            s = pltpu.make_async_remote_copy(
                src_ref=x_ref, dst_ref=out_ref.at[4 * x + 2 * y + c], send_sem=send_sems.at[k],
                recv_sem=recv_sems.at[k], device_id=(px, py, pc), device_id_type=MESH)
            s.start()
            sends.append(s)
        for k, (fx, fy, fc) in enumerate(FLIPS):
            px, py, pc = (x + fx) % 2, (y + fy) % 2, (c + fc) % 2
            slot = out_ref.at[4 * px + 2 * py + pc]
            pltpu.make_async_remote_copy(
                src_ref=slot, dst_ref=slot, send_sem=send_sems.at[k], recv_sem=recv_sems.at[k],
                device_id=(px, py, pc), device_id_type=MESH).wait_recv()
        for s in sends:
            s.wait_send()
        mine.wait()

    return pl.pallas_call(
        body, name=name, in_specs=[ANY], out_specs=ANY,
        out_shape=jax.ShapeDtypeStruct((8, R, C), block.dtype),
        scratch_shapes=[pltpu.SemaphoreType.DMA((7,)), pltpu.SemaphoreType.DMA((7,)), pltpu.SemaphoreType.DMA])(block)


PACK_C = 1024


def _pack(arrs, row_mult):
    flat = jnp.concatenate([a.reshape(-1) for a in arrs])
    n = flat.shape[0]
    rows = -(-n // PACK_C)
    rows = -(-rows // row_mult) * row_mult
    return jnp.pad(flat, (0, rows * PACK_C - n)).reshape(rows, PACK_C)


def _unpack(buf, shapes):
    flat = buf.reshape(-1)
    out, off = [], 0
    for s in shapes:
        n = int(np.prod(s))
        out.append(flat[off:off + n].reshape(s))
        off += n
    return out


OFF_Q, OFF_CKV, OFF_KR, OFF_SG, OFF_RW, OFF_GATE, N_IN_PAD = 0, 384, 640, 896, 1920, 3840, 6912
ROPE_LANE = QK_NOPE
HALF = QK_ROPE // 2


def _win_layout():
    src = np.full((N_IN_PAD,), -1, np.int64)
    sgn = np.ones((N_IN_PAD,), np.float32)
    src[0:640] = np.arange(0, 640)
    kr0 = Q_LORA + KV_LORA
    src[OFF_KR + ROPE_LANE:OFF_KR + ROPE_LANE + QK_ROPE] = kr0 + np.arange(QK_ROPE)
    sw = OFF_KR + HEAD_PAD + ROPE_LANE
    src[sw:sw + HALF] = kr0 + HALF + np.arange(HALF)
    sgn[sw:sw + HALF] = -1.0
    src[sw + HALF:sw + QK_ROPE] = kr0 + np.arange(HALF)
    src[OFF_SG:N_IN_PAD] = 672 + np.arange(N_IN_PAD - OFF_SG)
    return src, sgn


def _wuq_layout():
    hw = MLA_HEADS * HEAD_PAD
    src = np.full((2 * hw,), -1, np.int64)
    sgn = np.ones((2 * hw,), np.float32)
    per = QK_NOPE + QK_ROPE
    for h in range(MLA_HEADS):
        src[h * HEAD_PAD:h * HEAD_PAD + per] = h * per + np.arange(per)
        sw = hw + h * HEAD_PAD + ROPE_LANE
        src[sw:sw + HALF] = h * per + QK_NOPE + HALF + np.arange(HALF)
        sgn[sw:sw + HALF] = -1.0
        src[sw + HALF:sw + QK_ROPE] = h * per + QK_NOPE + np.arange(HALF)
    return src, sgn


def _permute_cols(w, src, sgn):
    cols = jnp.take(w, jnp.asarray(np.maximum(src, 0)), axis=1)
    return cols * jnp.asarray(np.where(src >= 0, sgn, 0.0).astype(np.float32)).astype(w.dtype)


def _unpermute_full(dw, src, sgn, n_cols):
    first = np.full((n_cols,), -1, np.int64)
    second = np.full((n_cols,), -1, np.int64)
    for pos, s in enumerate(src):
        if s < 0:
            continue
        if first[s] < 0:
            first[s] = pos
        else:
            second[s] = pos
    out = jnp.take(dw, jnp.asarray(first), axis=1) * jnp.asarray(sgn[first].astype(np.float32))
    m2 = (second >= 0)
    two = jnp.take(dw, jnp.asarray(np.maximum(second, 0)), axis=1) * jnp.asarray(
        np.where(m2, sgn[np.maximum(second, 0)], 0.0).astype(np.float32))
    return out, two


def _blockdiag(w):
    z = jnp.zeros_like(w[0])
    return jnp.concatenate([jnp.concatenate([w[0], z], axis=1), jnp.concatenate([z, w[1]], axis=1)], axis=0)


def _rope_tables(pos):
    inv = 1.0 / (ROPE_THETA ** (jnp.arange(0, QK_ROPE, 2, dtype=F32) / QK_ROPE))
    ang = pos.astype(F32)[:, None] * inv[None, :]
    cos, sin = jnp.cos(ang), jnp.sin(ang)
    pad = lambda t, fill: jnp.concatenate(
        [jnp.full((t.shape[0], ROPE_LANE), fill, F32), t, t, jnp.full((t.shape[0], HEAD_PAD - ROPE_LANE - QK_ROPE), fill, F32)], axis=1)
    return pad(cos, 1.0), pad(sin, 0.0)


def kernel(x, positions, attn_norm_g, w_in, gate_b, q_norm_g, w_uq, kv_norm_g, w_ukv, sg_ln_g, sg_ln_b, sg_w, sg_b, rw_mu, rw_w0, rw_w2, rw_a0, rw_a2, rw_g2, rw_k_k, rw_k_a, rw_r_k, rw_ln_g, rw_ln_b, w_branch, w_out, ffn_norm_g, w_ffn_gate, w_ffn_up, w_ffn_down, final_norm_g, loss_target, m_attn_norm_g, m_w_in, m_gate_b, m_q_norm_g, m_w_uq, m_kv_norm_g, m_w_ukv, m_sg_ln_g, m_sg_ln_b, m_sg_w, m_sg_b, m_rw_mu, m_rw_w0, m_rw_w2, m_rw_a0, m_rw_a2, m_rw_g2, m_rw_k_k, m_rw_k_a, m_rw_r_k, m_rw_ln_g, m_rw_ln_b, m_w_branch, m_w_out, m_ffn_norm_g, m_w_ffn_gate, m_w_ffn_up, m_w_ffn_down, m_final_norm_g, v_attn_norm_g, v_w_in, v_gate_b, v_q_norm_g, v_w_uq, v_kv_norm_g, v_w_ukv, v_sg_ln_g, v_sg_ln_b, v_sg_w, v_sg_b, v_rw_mu, v_rw_w0, v_rw_w2, v_rw_a0, v_rw_a2, v_rw_g2, v_rw_k_k, v_rw_k_a, v_rw_r_k, v_rw_ln_g, v_rw_ln_b, v_w_branch, v_w_out, v_ffn_norm_g, v_w_ffn_gate, v_w_ffn_up, v_w_ffn_down, v_final_norm_g):
    args = locals()
    W = {n: args[n] for n in WEIGHTS}
    M1 = {n: args['m_' + n] for n in WEIGHTS}
    M2 = {n: args['v_' + n] for n in WEIGHTS}
    B, S, D = x.shape
    N = B * S
    TM = _pick(N, (256, 128))
    TMH = 128
    TQ = _pick(S, (256, 128))

    shard_shapes = [W[n].shape for n in SHARDED]
    wpack = _pack([W[n] for n in SHARDED], 16)
    R = wpack.shape[0]
    full = {}
    mm_pack = _pack([W[n].astype(BF16) for n in MATMUL_SHARDED], 32)
    Rm = mm_pack.shape[0]
    gathered = gather_shards(mm_pack.reshape(2, Rm // 2, PACK_C)).reshape(4, Rm, PACK_C)
    pieces = [_unpack(gathered[q], [W[n].shape for n in MATMUL_SHARDED]) for q in range(4)]
    for i, n in enumerate(MATMUL_SHARDED):
        full[n] = jnp.concatenate([pieces[q][i] for q in range(4)], axis=SHARD_AXIS[n])
    small = gather_all("gather_small", _pack([W[n] for n in SMALL_SHARDED], 8))
    pieces = [_unpack(small[2 * q], [W[n].shape for n in SMALL_SHARDED]) for q in range(4)]
    for i, n in enumerate(SMALL_SHARDED):
        full[n] = jnp.concatenate([pieces[q][i] for q in range(4)], axis=SHARD_AXIS[n])
    for n in ('rw_w2', 'rw_a2', 'rw_g2'):
        full[n] = full[n].astype(F32)
    for n in REPLICATED:
        full[n] = W[n]

    win_src, win_sgn = _win_layout()
    wuq_src, wuq_sgn = _wuq_layout()
    ones = jnp.asarray(np.kron(np.eye(RWKV_HEADS), np.ones((RWKV_HEAD, RWKV_HEAD))), BF16)
    ct, st = _rope_tables(positions.reshape(N))
    row = lambda v: v.reshape(1, -1)

    H, HD = RWKV_HEADS, RWKV_HEAD
    inst = 2 * B * H
    isplit = LANES // inst
    Ip = HD // isplit
    flip_count = [0]

    def flip(t):
        flip_count[0] += 1
        return flip_tokens('flip%d' % flip_count[0], t.reshape(N, -1), B, S).reshape(t.shape)

    def dirs_of(c):
        c = c.reshape(B, S, -1)
        if c.shape[-1] == RWKV_DIM:
            return jnp.stack([c, flip(c)])
        return jnp.stack([c[..., :RWKV_DIM], flip(c[..., RWKV_DIM:])])

    def to_j(c):
        t = dirs_of(c).reshape(2, B, S, H, HD).transpose(2, 4, 0, 1, 3).reshape(S, HD, inst)
        return jnp.tile(t, (1, 1, isplit))

    def to_i(c):
        return dirs_of(c).reshape(2, B, S, H, isplit, Ip).transpose(2, 5, 4, 0, 1, 3).reshape(S, Ip, LANES)

    def from_i(t):
        return t.reshape(S, Ip, isplit, 2, B, H).transpose(3, 4, 0, 5, 2, 1).reshape(2, B, S, RWKV_DIM)

    def from_j(t):
        t = t.reshape(S, HD, isplit, 2, B, H)
        return [t[:, :, i2].transpose(2, 3, 0, 4, 1).reshape(2, B, S, RWKV_DIM) for i2 in range(isplit)]

    def parts_nodir(ds):
        out = []
        for d in ds:
            out += [d[0].reshape(N, -1), flip(d[1]).reshape(N, -1)]
        return out

    def parts_dir(ds):
        return [jnp.concatenate([d[0], flip(d[1])], axis=-1).reshape(N, -1) for d in ds]

    def shift_prev(z):
        z = z.reshape(B, S, -1)
        return jnp.pad(z[:, :-1], ((0, 0), (1, 0), (0, 0))).reshape(N, -1)

    def shift_next(z):
        z = z.reshape(B, S, -1)
        return jnp.pad(z[:, 1:], ((0, 0), (0, 1), (0, 0))).reshape(N, -1)

    LW = []
    for l in range(DEPTH):
        wb = full['w_branch'][l]
        wb0 = jnp.zeros((MLA_HEADS, HEAD_PAD, D), F32).at[:, QK_NOPE:].set(wb[0].reshape(MLA_HEADS, V_HEAD, D))
        LW.append(dict(
            attn_g=row(full['attn_norm_g'][l]),
            w_in=_permute_cols(full['w_in'][l], win_src, win_sgn),
            gate_b=row(full['gate_b'][l]),
            q_g=row(full['q_norm_g'][l]),
            w_uq=_permute_cols(full['w_uq'][l], wuq_src, wuq_sgn),
            kv_g=row(full['kv_norm_g'][l]),
            w_ukv=full['w_ukv'][l],
            sg_g=row(full['sg_ln_g'][l]), sg_b=row(full['sg_ln_b'][l]), sg_w=full['sg_w'][l],
            sg_bias=jnp.repeat(full['sg_b'][l].T, SG_DIM // SG_GROUPS, axis=1),
            mu=row(full['rw_mu'][l]), w0=row(full['rw_w0'][l]), w2=_blockdiag(full['rw_w2'][l]),
            a0=row(full['rw_a0'][l]), a2=_blockdiag(full['rw_a2'][l]), g2=full['rw_g2'][l],
            k_k=row(full['rw_k_k'][l]), k_a=row(full['rw_k_a'][l]), r_k=row(full['rw_r_k'][l]),
            ln_g=row(full['rw_ln_g'][l]), ln_b=row(full['rw_ln_b'][l]),
            wb0=wb0.reshape(MLA_HEADS * HEAD_PAD, D), wb1=wb[1], wb2=wb[2],
            w_out=full['w_out'][l], ffn_g=row(full['ffn_norm_g'][l]),
            w_gu=jnp.concatenate([full['w_ffn_gate'][l], full['w_ffn_up'][l]], axis=1),
            w_down=full['w_ffn_down'][l]))

    saved = []
    xc = x.reshape(N, D)
    for l in range(DEPTH):
        p = LW[l]
        t = 'l%d_' % l
        sv = dict(x=xc)
        (h,) = rowwise(t + 'attn_norm', f_rms, [xc], [p['attn_g']], [], [D], TM)
        pr = matmul(t + 'in_proj', h, p['w_in'], 'nn')
        sv['h'] = h
        p_q, p_ckv = pr[:, OFF_Q:OFF_CKV], pr[:, OFF_CKV:OFF_KR]
        p_kr = pr[:, OFF_KR:OFF_SG]
        (cq,) = rowwise(t + 'q_norm', f_rms, [p_q], [p['q_g']], [], [Q_LORA], TM)
        (ckv,) = rowwise(t + 'kv_norm', f_rms, [p_ckv], [p['kv_g']], [], [KV_LORA], TM)
        qq = matmul(t + 'uq', cq, p['w_uq'], 'nn')
        kv = matmul(t + 'ukv', ckv, p['w_ukv'], 'nn')
        qh, kh = rowwise(t + 'rope', f_rope, [qq, kv, p_kr, ct, st], [], [], [MLA_HEADS * HEAD_PAD] * 2, TM)
        ya = attention_fwd(t + 'attn', qh, kh, kv, B, S, TQ)
        sv.update(p_q=p_q, p_ckv=p_ckv, p_kr=p_kr, cq=cq, ckv=ckv, qq=qq, kv=kv, qh=qh, kh=kh, ya=ya)
        p_sg = pr[:, OFF_SG:OFF_RW]
        (yb,) = rowwise(t + 'sg', f_sg, [p_sg], [p['sg_g'], p['sg_b'], p['sg_w'], p['sg_bias']], [], [SG_DIM], TM)
        sv.update(p_sg=p_sg, yb=yb)
        z = pr[:, OFF_RW:OFF_GATE]
        zp, zn = shift_prev(z), shift_next(z)
        rw_par = [p['mu'], p['w0'], p['w2'], p['a0'], p['a2'], p['g2'], p['k_k'], p['k_a']]
        r_, v_, decay, kdir, kk, bdir, g_ = rowwise(
            t + 'rw_pre', f_rw_pre, [z, zp, zn], rw_par, [ones],
            [RWKV_DIM, RWKV_DIM, 2 * RWKV_DIM, 2 * RWKV_DIM, RWKV_DIM, 2 * RWKV_DIM, RWKV_DIM], TM)
        sc = dict(w=to_j(decay), k=to_j(kdir), b=to_j(bdir), kk=to_j(kk), r=to_j(r_), v=to_i(v_))
        ys, sp = scan_fwd(t + 'scan', sc['w'], sc['k'], sc['b'], sc['kk'], sc['r'], sc['v'])
        yd = from_i(ys)
        y0, y1 = yd[0].reshape(N, -1), flip(yd[1]).reshape(N, -1)
        (yc,) = rowwise(t + 'rw_post', f_rw_post, [y0, y1, r_, v_, kdir, g_], [p['ln_g'], p['ln_b'], p['r_k']],
                        [ones], [RWKV_DIM], TM)
        sv.update(z=z, zp=zp, zn=zn, r=r_, v=v_, kdir=kdir, g=g_, sc=sc, sp=sp, y0=y0, y1=y1, yc=yc)
        p_gate = pr[:, OFF_GATE:]
        b0 = matmul(t + 'br0', ya, p['wb0'], 'nn')
        b1 = matmul(t + 'br1', yb, p['wb1'], 'nn')
        b2 = matmul(t + 'br2', yc, p['wb2'], 'nn')
        (merged,) = rowwise(t + 'merge', f_merge, [p_gate, b0, b1, b2], [p['gate_b']], [], [D], TM)
        x2 = matmul(t + 'out_proj', merged, p['w_out'], 'nn', add=xc)
        sv.update(p_gate=p_gate, b0=b0, b1=b1, b2=b2, merged=merged, x2=x2)
        (h2,) = rowwise(t + 'ffn_norm', f_rms, [x2], [p['ffn_g']], [], [D], TM)
        au = matmul(t + 'ffn_in', h2, p['w_gu'], 'nn')
        (act,) = rowwise(t + 'swiglu', f_swiglu, [au], [], [], [D_FF], TM)
        xc = matmul(t + 'ffn_out', act, p['w_down'], 'nn', add=x2)
        sv.update(h2=h2, au=au, act=act)
        saved.append(sv)

    loss_part, dx, d_final_g = loss_head(xc, loss_target.reshape(N, D), row(full['final_norm_g']), TM)
    loss = lax.psum(loss_part[0, 0], ("x", "y", "c"))

    G = {n: [None] * DEPTH for n in WEIGHTS if n != 'final_norm_g'}
    for l in reversed(range(DEPTH)):
        p, sv = LW[l], saved[l]
        t = 'l%d_bwd_' % l
        d_act = matmul(t + 'ffn_out_dx', dx, p['w_down'], 'nt')
        G['w_ffn_down'][l] = matmul(t + 'ffn_out_dw', sv['act'], dx, 'tn')
        (d_au,), _ = rowwise_bwd(t + 'swiglu', f_swiglu, [sv['au']], [], [], [[d_act]], TMH)
        d_h2 = matmul(t + 'ffn_in_dx', d_au, p['w_gu'], 'nt')
        d_wgu = matmul(t + 'ffn_in_dw', sv['h2'], d_au, 'tn')
        G['w_ffn_gate'][l], G['w_ffn_up'][l] = d_wgu[:, :D_FF], d_wgu[:, D_FF:]
        (dx2,), (dg,) = rowwise_bwd(t + 'ffn_norm', f_rms, [sv['x2']], [p['ffn_g']], [], [[d_h2]], TM, extra=[(0, dx)])
        G['ffn_norm_g'][l] = dg.reshape(-1)
        d_merged = matmul(t + 'out_proj_dx', dx2, p['w_out'], 'nt')
        G['w_out'][l] = matmul(t + 'out_proj_dw', sv['merged'], dx2, 'tn')
        (d_pgate, d_b0, d_b1, d_b2), (d_gate_b,) = rowwise_bwd(
            t + 'merge', f_merge, [sv['p_gate'], sv['b0'], sv['b1'], sv['b2']], [p['gate_b']], [], [[d_merged]], TM)
        G['gate_b'][l] = d_gate_b.reshape(3, D)
        d_ya = matmul(t + 'br0_dx', d_b0, p['wb0'], 'nt')
        d_yb = matmul(t + 'br1_dx', d_b1, p['wb1'], 'nt')
        d_yc = matmul(t + 'br2_dx', d_b2, p['wb2'], 'nt')
        d_wb0 = matmul(t + 'br0_dw', sv['ya'], d_b0, 'tn').reshape(MLA_HEADS, HEAD_PAD, D)[:, QK_NOPE:].reshape(-1, D)
        G['w_branch'][l] = jnp.stack([d_wb0, matmul(t + 'br1_dw', sv['yb'], d_b1, 'tn'),
                                      matmul(t + 'br2_dw', sv['yc'], d_b2, 'tn')])
        (d_y0, _d_y1, d_r1, d_v1, d_kdir1, d_g), (d_ln_g, d_ln_b, d_r_k) = rowwise_bwd(
            t + 'rw_post', f_rw_post, [sv['y0'], sv['y1'], sv['r'], sv['v'], sv['kdir'], sv['g']],
            [p['ln_g'], p['ln_b'], p['r_k']], [ones], [[d_yc]], TMH)
        G['rw_ln_g'][l], G['rw_ln_b'][l] = d_ln_g.reshape(-1), d_ln_b.reshape(-1)
        G['rw_r_k'][l] = d_r_k.reshape(RWKV_HEADS, RWKV_HEAD)
        sc = sv['sc']
        s_dw, s_dk, s_db, s_dkk, s_dr, s_dv = scan_bwd(
            t + 'scan', sc['w'], sc['k'], sc['b'], sc['kk'], sc['r'], sc['v'], sv['sp'], to_i(d_y0))
        rw_par = [p['mu'], p['w0'], p['w2'], p['a0'], p['a2'], p['g2'], p['k_k'], p['k_a']]
        d_outs = [[d_r1] + parts_nodir(from_j(s_dr)), [d_v1] + parts_nodir([from_i(s_dv)]),
                  parts_dir(from_j(s_dw)), [d_kdir1] + parts_dir(from_j(s_dk)), parts_nodir(from_j(s_dkk)),
                  parts_dir(from_j(s_db)), [d_g]]
        (d_z, d_zp, d_zn), d_rw = rowwise_bwd(
            t + 'rw_pre', f_rw_pre, [sv['z'], sv['zp'], sv['zn']], rw_par, [ones], d_outs, TMH)
        (d_prw,) = rowwise(t + 'shift_sum', f_add3, [d_z, shift_next(d_zp), shift_prev(d_zn)], [], [], [RWKV_IN], TM)
        G['rw_mu'][l] = d_rw[0].reshape(-1)
        G['rw_w0'][l] = d_rw[1].reshape(2, RWKV_DIM)
        G['rw_w2'][l] = jnp.stack([d_rw[2][:64, :RWKV_DIM], d_rw[2][64:, RWKV_DIM:]])
        G['rw_a0'][l] = d_rw[3].reshape(2, RWKV_DIM)
        G['rw_a2'][l] = jnp.stack([d_rw[4][:64, :RWKV_DIM], d_rw[4][64:, RWKV_DIM:]])
        G['rw_g2'][l] = d_rw[5]
        G['rw_k_k'][l], G['rw_k_a'][l] = d_rw[6].reshape(-1), d_rw[7].reshape(-1)
        (d_psg,), (d_sg_g, d_sg_b, d_sg_w, d_sg_bias) = rowwise_bwd(
            t + 'sg', f_sg, [sv['p_sg']], [p['sg_g'], p['sg_b'], p['sg_w'], p['sg_bias']], [], [[d_yb]], TMH)
        G['sg_ln_g'][l], G['sg_ln_b'][l], G['sg_w'][l] = d_sg_g.reshape(-1), d_sg_b.reshape(-1), d_sg_w
        G['sg_b'][l] = d_sg_bias.reshape(SG_CHUNK, SG_GROUPS, SG_DIM // SG_GROUPS).sum(-1).T
        d_qh, d_kh, d_kvv = attention_bwd(t + 'attn', sv['qh'], sv['kh'], sv['kv'], d_ya, B, S, TQ)
        (d_qq, d_kv, d_pkr), _ = rowwise_bwd(
            t + 'rope', f_rope, [sv['qq'], sv['kv'], sv['p_kr'], ct, st], [], [], [[d_qh], [d_kh]], TM,
            n_row_diff=3, extra=[(1, d_kvv)])
        d_cq = matmul(t + 'uq_dx', d_qq, p['w_uq'], 'nt')
        d_wuq = matmul(t + 'uq_dw', sv['cq'], d_qq, 'tn')
        g1, g2_ = _unpermute_full(d_wuq, wuq_src, wuq_sgn, MLA_HEADS * (QK_NOPE + QK_ROPE))
        G['w_uq'][l] = g1 + g2_
        d_ckv = matmul(t + 'ukv_dx', d_kv, p['w_ukv'], 'nt')
        G['w_ukv'][l] = matmul(t + 'ukv_dw', sv['ckv'], d_kv, 'tn')
        (d_pq,), (dg,) = rowwise_bwd(t + 'q_norm', f_rms, [sv['p_q']], [p['q_g']], [], [[d_cq]], TM)
        G['q_norm_g'][l] = dg.reshape(-1)
        (d_pckv,), (dg,) = rowwise_bwd(t + 'kv_norm', f_rms, [sv['p_ckv']], [p['kv_g']], [], [[d_ckv]], TM)
        G['kv_norm_g'][l] = dg.reshape(-1)
        d_pr = jnp.concatenate([d_pq, d_pckv, d_pkr, d_psg, d_prw, d_pgate], axis=1)
        d_h = matmul(t + 'in_proj_dx', d_pr, p['w_in'], 'nt')
        d_win = matmul(t + 'in_proj_dw', sv['h'], d_pr, 'tn')
        g1, g2_ = _unpermute_full(d_win, win_src, win_sgn, N_IN)
        kr0 = Q_LORA + KV_LORA
        G['w_in'][l] = g1.at[:, kr0:kr0 + QK_ROPE].add(g2_[:, kr0:kr0 + QK_ROPE])
        (dx,), (dg,) = rowwise_bwd(t + 'attn_norm', f_rms, [sv['x']], [p['attn_g']], [], [[d_h]], TM, extra=[(0, dx2)])
        G['attn_norm_g'][l] = dg.reshape(-1)

    grads = {n: jnp.stack(G[n]) for n in G}
    grads['final_norm_g'] = d_final_g.reshape(-1)
    grad_x = dx.reshape(B, S, D)

    per_shard = []
    for q in range(4):
        sl = []
        for n in SHARDED:
            ax = SHARD_AXIS[n]
            w = W[n].shape[ax]
            sl.append(lax.slice_in_dim(grads[n], q * w, (q + 1) * w, axis=ax))
        per_shard.append(_pack(sl, 16))
    gpack = jnp.stack(per_shard).reshape(4, 2, R // 2, PACK_C)
    half_sum = sum_slots("sum_sharded", scatter_partials(gpack))
    g_shard = sibling_join(half_sum).reshape(R, PACK_C)
    rep_shapes = [W[n].shape for n in REPLICATED]
    rpack = _pack([grads[n] for n in REPLICATED], 8)
    g_rep = sum_slots("sum_replicated", gather_all("gather_replicated", rpack))

    d_s, m_s, v_s = adamw("adamw_sharded", wpack, g_shard, _pack([M1[n] for n in SHARDED], 16),
                          _pack([M2[n] for n in SHARDED], 16))
    d_r, m_r, v_r = adamw("adamw_replicated", _pack([W[n] for n in REPLICATED], 8), g_rep,
                          _pack([M1[n] for n in REPLICATED], 8), _pack([M2[n] for n in REPLICATED], 8))
    outs = {}
    for key, bs, br in (('grad', g_shard, g_rep), ('delta', d_s, d_r), ('new_m', m_s, m_r), ('new_v', v_s, v_r)):
        for n, a in zip(SHARDED, _unpack(bs, shard_shapes)):
            outs[key, n] = a
        for n, a in zip(REPLICATED, _unpack(br, rep_shapes)):
            outs[key, n] = a
    return (loss, grad_x, *[outs['grad', n] for n in WEIGHTS], *[outs['delta', n] for n in WEIGHTS],
            *[outs['new_m', n] for n in WEIGHTS], *[outs['new_v', n] for n in WEIGHTS])
```

```python
import functools
import math

import numpy as np
import jax
import jax.numpy as jnp
from jax import lax
from jax.experimental import pallas as pl
from jax.experimental.pallas import tpu as pltpu

F32 = jnp.float32
BF16 = jnp.bfloat16

DEPTH = 2
MLA_HEADS = 8
Q_LORA = 384
KV_LORA = 256
QK_NOPE = 64
QK_ROPE = 32
V_HEAD = 64
ROPE_THETA = 10000.0
SG_GROUPS = 8
SG_DIM = 512
SG_CHUNK = 128
RWKV_HEADS = 8
RWKV_HEAD = 64
RWKV_DIM = 512
GN_EPS = 64e-5
NORM_EPS = 1e-6
D_FF = 2816
RWKV_IN = 1920
N_IN = 6688
ADAM_LR, ADAM_B1, ADAM_B2, ADAM_EPS, ADAM_WD, ADAM_STEP = 0.001, 0.9, 0.999, 1e-08, 0.01, 10

LANES = 128
HEAD_PAD = 128
VMEM_LIMIT = 56 * 1024 * 1024
MESH = pl.DeviceIdType.MESH

WEIGHTS = ['attn_norm_g', 'w_in', 'gate_b', 'q_norm_g', 'w_uq', 'kv_norm_g', 'w_ukv', 'sg_ln_g', 'sg_ln_b', 'sg_w',
           'sg_b', 'rw_mu', 'rw_w0', 'rw_w2', 'rw_a0', 'rw_a2', 'rw_g2', 'rw_k_k', 'rw_k_a', 'rw_r_k', 'rw_ln_g',
           'rw_ln_b', 'w_branch', 'w_out', 'ffn_norm_g', 'w_ffn_gate', 'w_ffn_up', 'w_ffn_down', 'final_norm_g']
SHARD_AXIS = {'w_in': 2, 'gate_b': 2, 'w_uq': 2, 'w_ukv': 2, 'rw_w0': 2, 'rw_w2': 3, 'rw_a0': 2, 'rw_a2': 3,
              'rw_g2': 2, 'w_branch': 3, 'w_out': 1, 'w_ffn_gate': 2, 'w_ffn_up': 2, 'w_ffn_down': 1}
SHARDED = [n for n in WEIGHTS if n in SHARD_AXIS]
REPLICATED = [n for n in WEIGHTS if n not in SHARD_AXIS]
SMALL_SHARDED = ['gate_b', 'rw_w0', 'rw_a0']
MATMUL_SHARDED = [n for n in SHARDED if n not in SMALL_SHARDED]


def _params(sem=None):
    return pltpu.CompilerParams(dimension_semantics=sem, vmem_limit_bytes=VMEM_LIMIT)


def _pick(n, cands):
    for c in cands:
        if n % c == 0:
            return c
    return n


def _dot(a, b, dims):
    return lax.dot_general(a.astype(BF16), b.astype(BF16), (dims, ((), ())), preferred_element_type=F32)


def _nn(a, b):
    return _dot(a, b, ((1,), (0,)))


def _nt(a, b):
    return _dot(a, b, ((1,), (1,)))


def _tn(a, b):
    return _dot(a, b, ((0,), (0,)))


@jax.custom_vjp
def mm(a, b):
    return _nn(a, b)


mm.defvjp(lambda a, b: (_nn(a, b), (a, b)), lambda res, g: (_nt(g, res[1]), _tn(res[0], g)))


@jax.custom_vjp
def mm_nt(a, b):
    return _nt(a, b)


mm_nt.defvjp(lambda a, b: (_nt(a, b), (a, b)), lambda res, g: (_nn(g, res[1]), _tn(g, res[0])))


def _seg_raw(x, ones):
    hi = x.astype(BF16)
    lo = (x - hi.astype(F32)).astype(BF16)
    d = (((1,), (0,)), ((), ()))
    return (lax.dot_general(hi, ones, d, preferred_element_type=F32)
            + lax.dot_general(lo, ones, d, preferred_element_type=F32))


@jax.custom_vjp
def segsum(x, ones):
    return _seg_raw(x, ones)


segsum.defvjp(lambda x, ones: (_seg_raw(x, ones), ones),
              lambda ones, g: (_seg_raw(g, ones), jnp.zeros_like(ones)))


def _sigmoid(x):
    return 0.5 * (jnp.tanh(0.5 * x) + 1.0)


def _rms(x, g):
    return x * lax.rsqrt(jnp.mean(x * x, axis=-1, keepdims=True) + NORM_EPS) * g


def matmul(name, a, b, mode, add=None, out_dtype=F32):
    if mode == 'nn':
        (M, K), (_, N) = a.shape, b.shape
    elif mode == 'nt':
        (M, K), (N, _) = a.shape, b.shape
    else:
        (K, M), (_, N) = a.shape, b.shape
    tm = _pick(M, (1408, 1024, 512, 384, 256, 128))
    tn = _pick(N, (1408, 1024, 768, 512, 384, 256, 128))
    tk = _pick(K, (512, 384, 256, 128))
    nk = K // tk
    dims = {'nn': ((1,), (0,)), 'nt': ((1,), (1,)), 'tn': ((0,), (0,))}[mode]
    a_spec = pl.BlockSpec((tk, tm), lambda i, j, k: (k, i)) if mode == 'tn' else pl.BlockSpec((tm, tk), lambda i, j, k: (i, k))
    b_spec = pl.BlockSpec((tn, tk), lambda i, j, k: (j, k)) if mode == 'nt' else pl.BlockSpec((tk, tn), lambda i, j, k: (k, j))
    o_spec = pl.BlockSpec((tm, tn), lambda i, j, k: (i, j))
    has_add = add is not None

    def body(*refs):
        if has_add:
            a_ref, b_ref, add_ref, o_ref, acc = refs
        else:
            a_ref, b_ref, o_ref, acc = refs
        k = pl.program_id(2)

        @pl.when(k == 0)
        def _():
            acc[...] = jnp.zeros_like(acc)

        acc[...] += _dot(a_ref[...], b_ref[...], dims)

        @pl.when(k == nk - 1)
        def _():
            o_ref[...] = (acc[...] + add_ref[...] if has_add else acc[...]).astype(o_ref.dtype)

    ins = [a, b] + ([add] if has_add else [])
    specs = [a_spec, b_spec] + ([o_spec] if has_add else [])
    return pl.pallas_call(
        body, name=name, grid=(M // tm, N // tn, nk), in_specs=specs, out_specs=o_spec,
        out_shape=jax.ShapeDtypeStruct((M, N), out_dtype), scratch_shapes=[pltpu.VMEM((tm, tn), F32)],
        compiler_params=_params(("parallel", "parallel", "arbitrary")))(*ins)


def _full_spec(p):
    nd = p.ndim
    return pl.BlockSpec(p.shape, lambda i, _nd=nd: (0,) * _nd)


def rowwise(name, fn, rows, params, consts, out_widths, tm, out_dtypes=None):
    N = rows[0].shape[0]
    nr, npar, nc = len(rows), len(params), len(consts)

    def body(*refs):
        vals = [r[...] for r in refs[:nr + npar + nc]]
        res = fn(*vals)
        for o, v in zip(refs[nr + npar + nc:], res):
            o[...] = v.astype(o.dtype)

    in_specs = ([pl.BlockSpec((tm, r.shape[1]), lambda i: (i, 0)) for r in rows]
                + [_full_spec(p) for p in list(params) + list(consts)])
    out_specs = [pl.BlockSpec((tm, w), lambda i: (i, 0)) for w in out_widths]
    return pl.pallas_call(
        body, name=name, grid=(N // tm,), in_specs=in_specs, out_specs=out_specs,
        out_shape=[jax.ShapeDtypeStruct((N, w), d) for w, d in zip(out_widths, out_dtypes or [F32] * len(out_widths))],
        compiler_params=_params(("parallel",)))(*rows, *params, *consts)


def rowwise_bwd(name, fn, rows, params, consts, d_outs, tm, n_row_diff=None, extra=(), drow_dtypes=None):
    N = rows[0].shape[0]
    nr, npar, nc = len(rows), len(params), len(consts)
    nd = nr if n_row_diff is None else n_row_diff
    counts = [len(p) for p in d_outs]
    flat_d = [a for parts in d_outs for a in parts]
    nflat, nex = len(flat_d), len(extra)

    def body(*refs):
        pos = 0
        row_v = [r[...] for r in refs[pos:pos + nr]]; pos += nr
        par_v = [r[...] for r in refs[pos:pos + npar]]; pos += npar
        con_v = [r[...] for r in refs[pos:pos + nc]]; pos += nc
        d_refs = refs[pos:pos + nflat]; pos += nflat
        ex_refs = refs[pos:pos + nex]; pos += nex
        drow_refs = refs[pos:pos + nd]; pos += nd
        dpar_refs = refs[pos:pos + npar]

        def f(*diff):
            return fn(*diff[:nd], *row_v[nd:], *diff[nd:], *con_v)

        _, vjp = jax.vjp(f, *row_v[:nd], *par_v)
        cts, q = [], 0
        for c in counts:
            g = d_refs[q][...].astype(F32)
            for t in range(1, c):
                g = g + d_refs[q + t][...].astype(F32)
            cts.append(g)
            q += c
        grads = vjp(tuple(cts))
        drow = list(grads[:nd])
        for (idx, _), r in zip(extra, ex_refs):
            drow[idx] = drow[idx] + r[...].astype(F32)
        for o, v in zip(drow_refs, drow):
            o[...] = v.astype(o.dtype)

        @pl.when(pl.program_id(0) == 0)
        def _():
            for o in dpar_refs:
                o[...] = jnp.zeros_like(o)

        for o, v in zip(dpar_refs, grads[nd:]):
            o[...] += v

    ex_arrs = [a for _, a in extra]
    in_specs = ([pl.BlockSpec((tm, r.shape[1]), lambda i: (i, 0)) for r in rows]
                + [_full_spec(p) for p in list(params) + list(consts)]
                + [pl.BlockSpec((tm, a.shape[1]), lambda i: (i, 0)) for a in flat_d + ex_arrs])
    out_specs = ([pl.BlockSpec((tm, r.shape[1]), lambda i: (i, 0)) for r in rows[:nd]]
                 + [_full_spec(p) for p in params])
    out_shape = ([jax.ShapeDtypeStruct(r.shape, d) for r, d in zip(rows[:nd], drow_dtypes or [F32] * nd)]
                 + [jax.ShapeDtypeStruct(p.shape, F32) for p in params])
    res = pl.pallas_call(
        body, name=name, grid=(N // tm,), in_specs=in_specs, out_specs=out_specs, out_shape=out_shape,
        compiler_params=_params(("arbitrary",)))(*rows, *params, *consts, *flat_d, *ex_arrs)
    return list(res[:nd]), list(res[nd:])


def f_rms(x, g):
    return (_rms(x, g),)


def f_rope(qq, kv, krr, ct, st):
    hw = MLA_HEADS * HEAD_PAD
    c8 = jnp.tile(ct, (1, MLA_HEADS))
    s8 = jnp.tile(st, (1, MLA_HEADS))
    q = qq[:, :hw] * c8 + qq[:, hw:] * s8
    kr = krr[:, :HEAD_PAD] * ct + krr[:, HEAD_PAD:] * st
    lane = lax.broadcasted_iota(jnp.int32, kv.shape, 1) % HEAD_PAD
    k = jnp.where(lane < QK_NOPE, kv, jnp.tile(kr, (1, MLA_HEADS)))
    return q, k


def f_sg(p, ln_g, ln_b, w, bias):
    z = 0.5 * p * (1.0 + jnp.tanh(0.7978845608028654 * (p + 0.044715 * p * p * p)))
    u, v = z[:, :SG_DIM], z[:, SG_DIM:]
    mu = jnp.mean(v, axis=-1, keepdims=True)
    var = jnp.mean(jnp.square(v - mu), axis=-1, keepdims=True)
    v = (v - mu) * lax.rsqrt(var + 1e-5) * ln_g + ln_b
    lane = lax.broadcasted_iota(jnp.int32, (SG_CHUNK, LANES), 1)
    outs = []
    for c in range(p.shape[0] // SG_CHUNK):
        vc = v[c * SG_CHUNK:(c + 1) * SG_CHUNK]
        cols = []
        for m in range(SG_DIM // LANES):
            blk = vc[:, m * LANES:(m + 1) * LANES]
            cols.append(jnp.where(lane < 64, mm(w[2 * m], blk), mm(w[2 * m + 1], blk)))
        outs.append(jnp.concatenate(cols, axis=1) + bias)
    mixed = outs[0] if len(outs) == 1 else jnp.concatenate(outs, axis=0)
    return (u * mixed,)


def f_rw_pre(z, zp, zn, mu, w0, w2, a0, a2, g2, k_k, k_a, ones):
    z = z + mu * (0.5 * (zp + zn) - z)
    C = RWKV_DIM
    r, k, v = z[:, :C], z[:, C:2 * C], z[:, 2 * C:3 * C]
    wl, al, gl = z[:, 3 * C:3 * C + 128], z[:, 3 * C + 128:3 * C + 256], z[:, 3 * C + 256:]
    w = w0 + mm(jnp.tanh(wl), w2)
    decay = jnp.exp(-0.6065306597126334 * _sigmoid(w))
    a = _sigmoid(a0 + mm(al, a2))
    g = mm(_sigmoid(gl), g2)
    kk = k * k_k
    kk = kk / jnp.maximum(jnp.sqrt(segsum(kk * kk, ones)), 1e-12)
    k2 = jnp.concatenate([k, k], axis=1)
    kdir = k2 * (1.0 + (a - 1.0) * jnp.concatenate([k_a, k_a], axis=1))
    bdir = jnp.concatenate([kk, kk], axis=1) * a
    return r, v, decay, kdir, kk, bdir, g


def f_rw_post(y0, y1, r, v, kdir, g, ln_g, ln_b, r_k, ones):
    y = y0 + y1
    mean = segsum(y, ones) * (1.0 / RWKV_HEAD)
    yc = y - mean
    var = segsum(yc * yc, ones) * (1.0 / RWKV_HEAD)
    y = yc * lax.rsqrt(var + GN_EPS) * ln_g + ln_b
    C = RWKV_DIM
    bonus = segsum(r * kdir[:, :C] * r_k, ones) + segsum(r * kdir[:, C:] * r_k, ones)
    return ((y + bonus * v) * g,)


def f_merge(pg, b0, b1, b2, gate_b):
    D = b0.shape[1]
    gt = _sigmoid(pg + gate_b)
    return (gt[:, :D] * b0 + gt[:, D:2 * D] * b1 + gt[:, 2 * D:] * b2,)


def f_swiglu(au):
    a, u = au[:, :D_FF], au[:, D_FF:]
    return (a * _sigmoid(a) * u,)


def f_add3(a, b, c):
    return (a + b + c,)


def loss_head(x, tgt, g, tm):
    N, D = x.shape

    def body(x_ref, t_ref, g_ref, loss_ref, dx_ref, dg_ref):
        t = t_ref[...]

        def f(xv, gv):
            err = _rms(xv, gv) - t
            return 0.5 * jnp.sum(jnp.mean(err * err, axis=-1, keepdims=True))

        val, (dx, dg) = jax.value_and_grad(f, argnums=(0, 1))(x_ref[...], g_ref[...])
        dx_ref[...] = dx

        @pl.when(pl.program_id(0) == 0)
        def _():
            loss_ref[...] = jnp.zeros_like(loss_ref)
            dg_ref[...] = jnp.zeros_like(dg_ref)

        loss_ref[...] += jnp.full(loss_ref.shape, val, F32)
        dg_ref[...] += dg

    row = pl.BlockSpec((tm, D), lambda i: (i, 0))
    return pl.pallas_call(
        body, name="loss_head", grid=(N // tm,), in_specs=[row, row, _full_spec(g)],
        out_specs=[pl.BlockSpec((1, LANES), lambda i: (0, 0)), row, _full_spec(g)],
        out_shape=[jax.ShapeDtypeStruct((1, LANES), F32), jax.ShapeDtypeStruct((N, D), F32),
                   jax.ShapeDtypeStruct(g.shape, F32)],
        compiler_params=_params(("arbitrary",)))(x, tgt, g)


ATT_SCALE = float((QK_NOPE + QK_ROPE) ** -0.5)


def _attn_block(q, k, kv):
    s = mm_nt(q, k) * ATT_SCALE
    m = lax.stop_gradient(jnp.max(s, axis=-1, keepdims=True))
    e = jnp.exp(s - m)
    p = e / jnp.sum(e, axis=-1, keepdims=True)
    return mm(p, kv)


def attention_fwd(name, q, k, kv, B, S, tq, out_dtype):
    nq = S // tq
    qspec = pl.BlockSpec((tq, HEAD_PAD), lambda b, h, i: (b * nq + i, h))
    kspec = pl.BlockSpec((S, HEAD_PAD), lambda b, h, i: (b, h))

    def body(q_ref, k_ref, kv_ref, o_ref):
        o_ref[...] = _attn_block(q_ref[...], k_ref[...], kv_ref[...]).astype(o_ref.dtype)

    return pl.pallas_call(
        body, name=name, grid=(B, MLA_HEADS, nq), in_specs=[qspec, kspec, kspec], out_specs=qspec,
        out_shape=jax.ShapeDtypeStruct(q.shape, out_dtype),
        compiler_params=_params(("parallel", "parallel", "arbitrary")))(q, k, kv)


def attention_bwd(name, q, k, kv, do, B, S, tq):
    nq = S // tq
    qspec = pl.BlockSpec((tq, HEAD_PAD), lambda b, h, i: (b * nq + i, h))
    kspec = pl.BlockSpec((S, HEAD_PAD), lambda b, h, i: (b, h))

    def body(q_ref, k_ref, kv_ref, do_ref, dq_ref, dk_ref, dkv_ref):
        _, vjp = jax.vjp(_attn_block, q_ref[...], k_ref[...], kv_ref[...])
        dq, dk, dkv = vjp(do_ref[...])
        dq_ref[...] = dq

        @pl.when(pl.program_id(2) == 0)
        def _():
            dk_ref[...] = jnp.zeros_like(dk_ref)
            dkv_ref[...] = jnp.zeros_like(dkv_ref)

        dk_ref[...] += dk
        dkv_ref[...] += dkv

    sh = jax.ShapeDtypeStruct(q.shape, F32)
    return pl.pallas_call(
        body, name=name, grid=(B, MLA_HEADS, nq), in_specs=[qspec, kspec, kspec, qspec],
        out_specs=[qspec, kspec, kspec], out_shape=[sh, sh, sh],
        compiler_params=_params(("parallel", "parallel", "arbitrary")))(q, k, kv, do)


SCAN_TC = 8
SCAN_UNROLL = 16


def _jloop(n, body, init):
    def outer(o, c):
        for u in range(SCAN_UNROLL):
            c = body(o * SCAN_UNROLL + u, c)
        return c

    return lax.fori_loop(0, n // SCAN_UNROLL, outer, init)


def scan_fwd(name, w, k, b, kk, r, v):
    T, J, L = w.shape
    Ip = v.shape[1]
    jspec = pl.BlockSpec((SCAN_TC, J, L), lambda g: (g, 0, 0))
    ispec = pl.BlockSpec((SCAN_TC, Ip, L), lambda g: (g, 0, 0))
    sspec = pl.BlockSpec((SCAN_TC, J, Ip, L), lambda g: (g, 0, 0, 0))

    def body(w_ref, k_ref, b_ref, kk_ref, r_ref, v_ref, y_ref, sp_ref, s_ref):
        @pl.when(pl.program_id(0) == 0)
        def _():
            s_ref[...] = jnp.zeros_like(s_ref)

        def row(ref, tt, j):
            return jnp.broadcast_to(ref[tt, pl.ds(j, 1), :], (Ip, L))

        def step(tt, carry):
            def p1(j, sa):
                s = s_ref[j]
                sp_ref[tt, j] = s
                return sa + s * row(kk_ref, tt, j)

            sa = _jloop(J, p1, jnp.zeros((Ip, L), F32))
            vt = v_ref[tt]

            def p2(j, y):
                s = s_ref[j] * row(w_ref, tt, j) - sa * row(b_ref, tt, j) + vt * row(k_ref, tt, j)
                s_ref[j] = s
                return y + s * row(r_ref, tt, j)

            y_ref[tt] = _jloop(J, p2, jnp.zeros((Ip, L), F32))
            return carry

        lax.fori_loop(0, SCAN_TC, step, 0)

    return pl.pallas_call(
        body, name=name, grid=(T // SCAN_TC,), in_specs=[jspec] * 5 + [ispec], out_specs=[ispec, sspec],
        out_shape=[jax.ShapeDtypeStruct((T, Ip, L), F32), jax.ShapeDtypeStruct((T, J, Ip, L), F32)],
        scratch_shapes=[pltpu.VMEM((J, Ip, L), F32)],
        compiler_params=_params(("arbitrary",)))(w, k, b, kk, r, v)


def scan_bwd(name, w, k, b, kk, r, v, sp, dy):
    T, J, L = w.shape
    Ip = v.shape[1]
    nT = T // SCAN_TC
    jspec = pl.BlockSpec((SCAN_TC, J, L), lambda g: (nT - 1 - g, 0, 0))
    ispec = pl.BlockSpec((SCAN_TC, Ip, L), lambda g: (nT - 1 - g, 0, 0))
    sspec = pl.BlockSpec((SCAN_TC, J, Ip, L), lambda g: (nT - 1 - g, 0, 0, 0))

    def body(w_ref, k_ref, b_ref, kk_ref, r_ref, v_ref, sp_ref, dy_ref,
             dw_ref, dk_ref, db_ref, dkk_ref, dr_ref, dv_ref, ds_ref):
        @pl.when(pl.program_id(0) == 0)
        def _():
            ds_ref[...] = jnp.zeros_like(ds_ref)

        def row(ref, tt, j):
            return jnp.broadcast_to(ref[tt, pl.ds(j, 1), :], (Ip, L))

        def rsum(x):
            return jnp.sum(x, axis=0, keepdims=True)

        def step(n, carry):
            tt = SCAN_TC - 1 - n
            dyt = dy_ref[tt]
            vt = v_ref[tt]

            def p1(j, c):
                sa, dsa, dv = c
                ds = ds_ref[j] + dyt * row(r_ref, tt, j)
                ds_ref[j] = ds
                return (sa + sp_ref[tt, j] * row(kk_ref, tt, j), dsa - ds * row(b_ref, tt, j),
                        dv + ds * row(k_ref, tt, j))

            z = jnp.zeros((Ip, L), F32)
            sa, dsa, dv = _jloop(J, p1, (z, z, z))
            dv_ref[tt] = dv

            def p2(j, c):
                ds = ds_ref[j]
                s0 = sp_ref[tt, j]
                wj, kkj = row(w_ref, tt, j), row(kk_ref, tt, j)
                s1 = s0 * wj - sa * row(b_ref, tt, j) + vt * row(k_ref, tt, j)
                dr_ref[tt, pl.ds(j, 1), :] = rsum(s1 * dyt)
                dk_ref[tt, pl.ds(j, 1), :] = rsum(ds * vt)
                db_ref[tt, pl.ds(j, 1), :] = -rsum(ds * sa)
                dw_ref[tt, pl.ds(j, 1), :] = rsum(ds * s0)
                dkk_ref[tt, pl.ds(j, 1), :] = rsum(s0 * dsa)
                ds_ref[j] = ds * wj + dsa * kkj
                return c

            _jloop(J, p2, 0)
            return carry

        lax.fori_loop(0, SCAN_TC, step, 0)

    jsh = jax.ShapeDtypeStruct((T, J, L), F32)
    return pl.pallas_call(
        body, name=name, grid=(nT,), in_specs=[jspec] * 5 + [ispec, sspec, ispec],
        out_specs=[jspec] * 5 + [ispec], out_shape=[jsh] * 5 + [jax.ShapeDtypeStruct((T, Ip, L), F32)],
        scratch_shapes=[pltpu.VMEM((J, Ip, L), F32)],
        compiler_params=_params(("arbitrary",)))(w, k, b, kk, r, v, sp, dy)


def flip_tokens(name, x, B, S):
    N, C = x.shape
    tb = _pick(S, (256, 128))
    nb = S // tb

    def body(x_ref, o_ref):
        r = lax.broadcasted_iota(jnp.int32, (tb, tb), 0)
        c = lax.broadcasted_iota(jnp.int32, (tb, tb), 1)
        anti = jnp.where(r + c == tb - 1, 1.0, 0.0).astype(BF16)
        xv = x_ref[...]
        hi = xv.astype(BF16)
        r1 = xv - hi.astype(F32)
        mid = r1.astype(BF16)
        lo = (r1 - mid.astype(F32)).astype(BF16)
        dot = lambda p: lax.dot_general(anti, p, (((1,), (0,)), ((), ())), preferred_element_type=F32)
        o_ref[...] = (dot(hi) + dot(mid)) + dot(lo)

    return pl.pallas_call(
        body, name=name, grid=(B, nb), in_specs=[pl.BlockSpec((tb, C), lambda b, i: (b * nb + i, 0))],
        out_specs=pl.BlockSpec((tb, C), lambda b, i: (b * nb + nb - 1 - i, 0)),
        out_shape=jax.ShapeDtypeStruct((N, C), F32), compiler_params=_params(("parallel", "parallel")))(x)


def adamw(name, w, g, m, v):
    R, C = w.shape
    tr = _pick(R, (256, 128, 64, 32, 16, 8))
    c1 = 1.0 - ADAM_B1 ** ADAM_STEP
    c2 = 1.0 - ADAM_B2 ** ADAM_STEP

    def body(w_ref, g_ref, m_ref, v_ref, d_ref, nm_ref, nv_ref):
        gv = g_ref[...]
        nm = ADAM_B1 * m_ref[...] + (1.0 - ADAM_B1) * gv
        nv = ADAM_B2 * v_ref[...] + (1.0 - ADAM_B2) * jnp.square(gv)
        d_ref[...] = -ADAM_LR * ((nm / c1) / (jnp.sqrt(nv / c2) + ADAM_EPS) + ADAM_WD * w_ref[...])
        nm_ref[...] = nm
        nv_ref[...] = nv

    spec = pl.BlockSpec((tr, C), lambda i: (i, 0))
    sh = jax.ShapeDtypeStruct((R, C), F32)
    return pl.pallas_call(body, name=name, grid=(R // tr,), in_specs=[spec] * 4, out_specs=[spec] * 3,
                          out_shape=[sh] * 3, compiler_params=_params(("parallel",)))(w, g, m, v)


def sum_slots(name, x):
    n, R, C = x.shape
    tr = _pick(R, (256, 128, 64, 32, 16, 8))

    def body(x_ref, o_ref):
        acc = x_ref[0].astype(F32)
        for s in range(1, n):
            acc = acc + x_ref[s].astype(F32)
        o_ref[...] = acc

    return pl.pallas_call(
        body, name=name, grid=(R // tr,), in_specs=[pl.BlockSpec((n, tr, C), lambda i: (0, i, 0))],
        out_specs=pl.BlockSpec((tr, C), lambda i: (i, 0)), out_shape=jax.ShapeDtypeStruct((R, C), F32),
        compiler_params=_params(("parallel",)))(x)


ANY = pl.BlockSpec(memory_space=pl.ANY)


def _xyc():
    return lax.axis_index("x"), lax.axis_index("y"), lax.axis_index("c")


def gather_shards(shard):
    _, R, C = shard.shape

    def body(x_ref, out_ref, send_sems, recv_sems, local_sem):
        x, y, c = _xyc()
        me, sibling = (x, y, c), (x, y, 1 - c)
        chips = [(1 - x, y), (x, 1 - y), (1 - x, 1 - y)]

        def cp(k, cx, cy, half, to, src=None):
            dst = out_ref.at[2 * cx + cy, half]
            return pltpu.make_async_remote_copy(
                src_ref=dst if src is None else src, dst_ref=dst, send_sem=send_sems.at[k],
                recv_sem=recv_sems.at[k], device_id=to, device_id_type=MESH)

        mine = pltpu.make_async_copy(x_ref, out_ref.at[2 * x + y], local_sem)
        mine.start()
        first = [cp(j, x, y, c, (*chip, c), src=x_ref.at[c]) for j, chip in enumerate(chips)]
        for f in first:
            f.start()
        passed = [cp(3 + j, *chip, c, sibling) for j, chip in enumerate(chips)]
        for j, chip in enumerate(chips):
            cp(j, *chip, c, me).wait_recv()
            passed[j].start()
        for j, chip in enumerate(chips):
            cp(3 + j, *chip, 1 - c, me).wait_recv()
        for f in first + passed:
            f.wait_send()
        mine.wait()

    return pl.pallas_call(
        body, name="gather_shards", in_specs=[ANY], out_specs=ANY,
        out_shape=jax.ShapeDtypeStruct((4, 2, R, C), shard.dtype),
        scratch_shapes=[pltpu.SemaphoreType.DMA((6,)), pltpu.SemaphoreType.DMA((6,)), pltpu.SemaphoreType.DMA])(shard)


FLIPS = [(0, 0, 1), (0, 1, 0), (0, 1, 1), (1, 0, 0), (1, 0, 1), (1, 1, 0), (1, 1, 1)]


def scatter_partials(g):
    _, _, R, C = g.shape

    def body(g_ref, out_ref, send_sems, recv_sems, local_sem):
        x, y, c = _xyc()
        me_idx = 4 * x + 2 * y + c
        mine = pltpu.make_async_copy(g_ref.at[2 * x + y, c], out_ref.at[me_idx], local_sem)
        mine.start()
        sends = []
        for k, (fx, fy, fc) in enumerate(FLIPS):
            px, py, pc = (x + fx) % 2, (y + fy) % 2, (c + fc) % 2
            s = pltpu.make_async_remote_copy(
                src_ref=g_ref.at[2 * px + py, pc], dst_ref=out_ref.at[me_idx], send_sem=send_sems.at[k],
                recv_sem=recv_sems.at[k], device_id=(px, py, pc), device_id_type=MESH)
            s.start()
            sends.append(s)
        for k, (fx, fy, fc) in enumerate(FLIPS):
            px, py, pc = (x + fx) % 2, (y + fy) % 2, (c + fc) % 2
            slot = out_ref.at[4 * px + 2 * py + pc]
            pltpu.make_async_remote_copy(
                src_ref=slot, dst_ref=slot, send_sem=send_sems.at[k], recv_sem=recv_sems.at[k],
                device_id=(px, py, pc), device_id_type=MESH).wait_recv()
        for s in sends:
            s.wait_send()
        mine.wait()

    return pl.pallas_call(
        body, name="scatter_partials", in_specs=[ANY], out_specs=ANY,
        out_shape=jax.ShapeDtypeStruct((8, R, C), g.dtype),
        scratch_shapes=[pltpu.SemaphoreType.DMA((7,)), pltpu.SemaphoreType.DMA((7,)), pltpu.SemaphoreType.DMA])(g)


JOIN_CHUNKS = 8


def sibling_join(half):
    R, C = half.shape
    rows = R // JOIN_CHUNKS

    def body(h_ref, out_ref, send_sems, recv_sems, local_sem):
        x, y, c = _xyc()
        mine = pltpu.make_async_copy(h_ref, out_ref.at[c], local_sem)
        mine.start()
        sends = []
        for k in range(JOIN_CHUNKS):
            s = pltpu.make_async_remote_copy(
                src_ref=h_ref.at[pl.ds(k * rows, rows)], dst_ref=out_ref.at[c, pl.ds(k * rows, rows)],
                send_sem=send_sems.at[k], recv_sem=recv_sems.at[k], device_id=(x, y, 1 - c), device_id_type=MESH)
            s.start()
            sends.append(s)
        for k in range(JOIN_CHUNKS):
            theirs = out_ref.at[1 - c, pl.ds(k * rows, rows)]
            pltpu.make_async_remote_copy(src_ref=theirs, dst_ref=theirs, send_sem=send_sems.at[k],
                                         recv_sem=recv_sems.at[k], device_id=(x, y, 1 - c),
                                         device_id_type=MESH).wait_recv()
        for s in sends:
            s.wait_send()
        mine.wait()

    return pl.pallas_call(
        body, name="sibling_join", in_specs=[ANY], out_specs=ANY,
        out_shape=jax.ShapeDtypeStruct((2, R, C), half.dtype),
        scratch_shapes=[pltpu.SemaphoreType.DMA((JOIN_CHUNKS,)), pltpu.SemaphoreType.DMA((JOIN_CHUNKS,)),
                        pltpu.SemaphoreType.DMA])(half)


def gather_all(name, block):
    R, C = block.shape

    def body(x_ref, out_ref, send_sems, recv_sems, local_sem):
        x, y, c = _xyc()
        mine = pltpu.make_async_copy(x_ref, out_ref.at[4 * x + 2 * y + c], local_sem)
        mine.start()
        sends = []
        for k, (fx, fy, fc) in enumerate(FLIPS):
            px, py, pc = (x + fx) % 2, (y + fy) % 2, (c + fc) % 2
            s = pltpu.make_async_remote_copy(
                src_ref=x_ref, dst_ref=out_ref.at[4 * x + 2 * y + c], send_sem=send_sems.at[k],
                recv_sem=recv_sems.at[k], device_id=(px, py, pc), device_id_type=MESH)
            s.start()
            sends.append(s)
        for k, (fx, fy, fc) in enumerate(FLIPS):
            px, py, pc = (x + fx) % 2, (y + fy) % 2, (c + fc) % 2
            slot = out_ref.at[4 * px + 2 * py + pc]
            pltpu.make_async_remote_copy(
                src_ref=slot, dst_ref=slot, send_sem=send_sems.at[k], recv_sem=recv_sems.at[k],
                device_id=(px, py, pc), device_id_type=MESH).wait_recv()
        for s in sends:
            s.wait_send()
        mine.wait()

    return pl.pallas_call(
        body, name=name, in_specs=[ANY], out_specs=ANY,
        out_shape=jax.ShapeDtypeStruct((8, R, C), block.dtype),
        scratch_shapes=[pltpu.SemaphoreType.DMA((7,)), pltpu.SemaphoreType.DMA((7,)), pltpu.SemaphoreType.DMA])(block)


PACK_C = 1024


def _pack(arrs, row_mult):
    flat = jnp.concatenate([a.reshape(-1) for a in arrs])
    n = flat.shape[0]
    rows = -(-n // PACK_C)
    rows = -(-rows // row_mult) * row_mult
    return jnp.pad(flat, (0, rows * PACK_C - n)).reshape(rows, PACK_C)


def _unpack(buf, shapes):
    flat = buf.reshape(-1)
    out, off = [], 0
    for s in shapes:
        n = int(np.prod(s))
        out.append(flat[off:off + n].reshape(s))
        off += n
    return out


OFF_Q, OFF_CKV, OFF_KR, OFF_SG, OFF_RW, OFF_GATE, N_IN_PAD = 0, 384, 640, 896, 1920, 3840, 6912
IN_SEGMENTS = [('q', OFF_Q, OFF_CKV), ('ckv', OFF_CKV, OFF_KR), ('kr', OFF_KR, OFF_SG), ('sg', OFF_SG, OFF_RW),
               ('rw', OFF_RW, OFF_GATE), ('gate', OFF_GATE, N_IN_PAD)]
ROPE_LANE = QK_NOPE
HALF = QK_ROPE // 2


def _win_layout():
    src = np.full((N_IN_PAD,), -1, np.int64)
    sgn = np.ones((N_IN_PAD,), np.float32)
    src[0:640] = np.arange(0, 640)
    kr0 = Q_LORA + KV_LORA
    src[OFF_KR + ROPE_LANE:OFF_KR + ROPE_LANE + QK_ROPE] = kr0 + np.arange(QK_ROPE)
    sw = OFF_KR + HEAD_PAD + ROPE_LANE
    src[sw:sw + HALF] = kr0 + HALF + np.arange(HALF)
    sgn[sw:sw + HALF] = -1.0
    src[sw + HALF:sw + QK_ROPE] = kr0 + np.arange(HALF)
    src[OFF_SG:N_IN_PAD] = 672 + np.arange(N_IN_PAD - OFF_SG)
    return src, sgn


def _wuq_layout():
    hw = MLA_HEADS * HEAD_PAD
    src = np.full((2 * hw,), -1, np.int64)
    sgn = np.ones((2 * hw,), np.float32)
    per = QK_NOPE + QK_ROPE
    for h in range(MLA_HEADS):
        src[h * HEAD_PAD:h * HEAD_PAD + per] = h * per + np.arange(per)
        sw = hw + h * HEAD_PAD + ROPE_LANE
        src[sw:sw + HALF] = h * per + QK_NOPE + HALF + np.arange(HALF)
        sgn[sw:sw + HALF] = -1.0
        src[sw + HALF:sw + QK_ROPE] = h * per + QK_NOPE + np.arange(HALF)
    return src, sgn


def _permute_cols(w, src, sgn):
    cols = jnp.take(w, jnp.asarray(np.maximum(src, 0)), axis=1)
    return cols * jnp.asarray(np.where(src >= 0, sgn, 0.0).astype(np.float32)).astype(w.dtype)


def _unpermute_full(dw, src, sgn, n_cols):
    first = np.full((n_cols,), -1, np.int64)
    second = np.full((n_cols,), -1, np.int64)
    for pos, s in enumerate(src):
        if s < 0:
            continue
        if first[s] < 0:
            first[s] = pos
        else:
            second[s] = pos
    out = jnp.take(dw, jnp.asarray(first), axis=1) * jnp.asarray(sgn[first].astype(np.float32))
    m2 = (second >= 0)
    two = jnp.take(dw, jnp.asarray(np.maximum(second, 0)), axis=1) * jnp.asarray(
        np.where(m2, sgn[np.maximum(second, 0)], 0.0).astype(np.float32))
    return out, two


def _blockdiag(w):
    z = jnp.zeros_like(w[0])
    return jnp.concatenate([jnp.concatenate([w[0], z], axis=1), jnp.concatenate([z, w[1]], axis=1)], axis=0)


def _rope_tables(pos):
    inv = 1.0 / (ROPE_THETA ** (jnp.arange(0, QK_ROPE, 2, dtype=F32) / QK_ROPE))
    ang = pos.astype(F32)[:, None] * inv[None, :]
    cos, sin = jnp.cos(ang), jnp.sin(ang)
    pad = lambda t, fill: jnp.concatenate(
        [jnp.full((t.shape[0], ROPE_LANE), fill, F32), t, t, jnp.full((t.shape[0], HEAD_PAD - ROPE_LANE - QK_ROPE), fill, F32)], axis=1)
    return pad(cos, 1.0), pad(sin, 0.0)


def kernel(x, positions, attn_norm_g, w_in, gate_b, q_norm_g, w_uq, kv_norm_g, w_ukv, sg_ln_g, sg_ln_b, sg_w, sg_b, rw_mu, rw_w0, rw_w2, rw_a0, rw_a2, rw_g2, rw_k_k, rw_k_a, rw_r_k, rw_ln_g, rw_ln_b, w_branch, w_out, ffn_norm_g, w_ffn_gate, w_ffn_up, w_ffn_down, final_norm_g, loss_target, m_attn_norm_g, m_w_in, m_gate_b, m_q_norm_g, m_w_uq, m_kv_norm_g, m_w_ukv, m_sg_ln_g, m_sg_ln_b, m_sg_w, m_sg_b, m_rw_mu, m_rw_w0, m_rw_w2, m_rw_a0, m_rw_a2, m_rw_g2, m_rw_k_k, m_rw_k_a, m_rw_r_k, m_rw_ln_g, m_rw_ln_b, m_w_branch, m_w_out, m_ffn_norm_g, m_w_ffn_gate, m_w_ffn_up, m_w_ffn_down, m_final_norm_g, v_attn_norm_g, v_w_in, v_gate_b, v_q_norm_g, v_w_uq, v_kv_norm_g, v_w_ukv, v_sg_ln_g, v_sg_ln_b, v_sg_w, v_sg_b, v_rw_mu, v_rw_w0, v_rw_w2, v_rw_a0, v_rw_a2, v_rw_g2, v_rw_k_k, v_rw_k_a, v_rw_r_k, v_rw_ln_g, v_rw_ln_b, v_w_branch, v_w_out, v_ffn_norm_g, v_w_ffn_gate, v_w_ffn_up, v_w_ffn_down, v_final_norm_g):
    args = locals()
    W = {n: args[n] for n in WEIGHTS}
    M1 = {n: args['m_' + n] for n in WEIGHTS}
    M2 = {n: args['v_' + n] for n in WEIGHTS}
    B, S, D = x.shape
    N = B * S
    TM = _pick(N, (256, 128))
    TMH = 128
    TQ = _pick(S, (256, 128))

    shard_shapes = [W[n].shape for n in SHARDED]
    full = {}
    mm_pack = _pack([W[n].astype(BF16) for n in MATMUL_SHARDED], 32)
    Rm = mm_pack.shape[0]
    gathered = gather_shards(mm_pack.reshape(2, Rm // 2, PACK_C)).reshape(4, Rm, PACK_C)
    pieces = [_unpack(gathered[q], [W[n].shape for n in MATMUL_SHARDED]) for q in range(4)]
    for i, n in enumerate(MATMUL_SHARDED):
        full[n] = jnp.concatenate([pieces[q][i] for q in range(4)], axis=SHARD_AXIS[n])
    small = gather_all("gather_small", _pack([W[n] for n in SMALL_SHARDED], 8))
    pieces = [_unpack(small[2 * q], [W[n].shape for n in SMALL_SHARDED]) for q in range(4)]
    for i, n in enumerate(SMALL_SHARDED):
        full[n] = jnp.concatenate([pieces[q][i] for q in range(4)], axis=SHARD_AXIS[n])
    for n in ('rw_w2', 'rw_a2', 'rw_g2'):
        full[n] = full[n].astype(F32)
    for n in REPLICATED:
        full[n] = W[n]

    win_src, win_sgn = _win_layout()
    wuq_src, wuq_sgn = _wuq_layout()
    ones = jnp.asarray(np.kron(np.eye(RWKV_HEADS), np.ones((RWKV_HEAD, RWKV_HEAD))), BF16)
    ct, st = _rope_tables(positions.reshape(N))
    row = lambda v: v.reshape(1, -1)

    H, HD = RWKV_HEADS, RWKV_HEAD
    inst = 2 * B * H
    isplit = LANES // inst
    Ip = HD // isplit
    flip_count = [0]

    def flip(t):
        flip_count[0] += 1
        return flip_tokens('flip%d' % flip_count[0], t.reshape(N, -1), B, S).reshape(t.shape)

    def dirs_of(c):
        c = c.reshape(B, S, -1)
        if c.shape[-1] == RWKV_DIM:
            return jnp.stack([c, flip(c)])
        return jnp.stack([c[..., :RWKV_DIM], flip(c[..., RWKV_DIM:])])

    def to_j(c):
        t = dirs_of(c).reshape(2, B, S, H, HD).transpose(2, 4, 0, 1, 3).reshape(S, HD, inst)
        return jnp.tile(t, (1, 1, isplit))

    def to_i(c):
        return dirs_of(c).reshape(2, B, S, H, isplit, Ip).transpose(2, 5, 4, 0, 1, 3).reshape(S, Ip, LANES)

    def from_i(t):
        return t.reshape(S, Ip, isplit, 2, B, H).transpose(3, 4, 0, 5, 2, 1).reshape(2, B, S, RWKV_DIM)

    def from_j(t):
        t = t.reshape(S, HD, isplit, 2, B, H)
        return [t[:, :, i2].transpose(2, 3, 0, 4, 1).reshape(2, B, S, RWKV_DIM) for i2 in range(isplit)]

    def parts_nodir(ds):
        out = []
        for d in ds:
            out += [d[0].reshape(N, -1), flip(d[1]).reshape(N, -1)]
        return out

    def parts_dir(ds):
        return [jnp.concatenate([d[0], flip(d[1])], axis=-1).reshape(N, -1) for d in ds]

    def shift_prev(z):
        z = z.reshape(B, S, -1)
        return jnp.pad(z[:, :-1], ((0, 0), (1, 0), (0, 0))).reshape(N, -1)

    def shift_next(z):
        z = z.reshape(B, S, -1)
        return jnp.pad(z[:, 1:], ((0, 0), (0, 1), (0, 0))).reshape(N, -1)

    LW = []
    for l in range(DEPTH):
        wb = full['w_branch'][l]
        wb0 = jnp.zeros((MLA_HEADS, HEAD_PAD, D), F32).at[:, QK_NOPE:].set(wb[0].reshape(MLA_HEADS, V_HEAD, D))
        LW.append(dict(
            attn_g=row(full['attn_norm_g'][l]),
            w_in=_permute_cols(full['w_in'][l], win_src, win_sgn),
            gate_b=row(full['gate_b'][l]),
            q_g=row(full['q_norm_g'][l]),
            w_uq=_permute_cols(full['w_uq'][l], wuq_src, wuq_sgn),
            kv_g=row(full['kv_norm_g'][l]),
            w_ukv=full['w_ukv'][l],
            sg_g=row(full['sg_ln_g'][l]), sg_b=row(full['sg_ln_b'][l]), sg_w=full['sg_w'][l],
            sg_bias=jnp.repeat(full['sg_b'][l].T, SG_DIM // SG_GROUPS, axis=1),
            mu=row(full['rw_mu'][l]), w0=row(full['rw_w0'][l]), w2=_blockdiag(full['rw_w2'][l]),
            a0=row(full['rw_a0'][l]), a2=_blockdiag(full['rw_a2'][l]), g2=full['rw_g2'][l],
            k_k=row(full['rw_k_k'][l]), k_a=row(full['rw_k_a'][l]), r_k=row(full['rw_r_k'][l]),
            ln_g=row(full['rw_ln_g'][l]), ln_b=row(full['rw_ln_b'][l]),
            wb0=wb0.reshape(MLA_HEADS * HEAD_PAD, D), wb1=wb[1], wb2=wb[2],
            w_out=full['w_out'][l], ffn_g=row(full['ffn_norm_g'][l]),
            w_gu=jnp.concatenate([full['w_ffn_gate'][l], full['w_ffn_up'][l]], axis=1),
            w_down=full['w_ffn_down'][l]))

    saved = []
    xc = x.reshape(N, D)
    for l in range(DEPTH):
        p = LW[l]
        t = 'l%d_' % l
        sv = dict(x=xc)
        (h,) = rowwise(t + 'attn_norm', f_rms, [xc], [p['attn_g']], [], [D], TM, [BF16])
        p_q, p_ckv, p_kr, p_sg, z, p_gate = [
            matmul(t + 'in_proj_' + sn, h, p['w_in'][:, a:b], 'nn') for sn, a, b in IN_SEGMENTS]
        sv['h'] = h
        (cq,) = rowwise(t + 'q_norm', f_rms, [p_q], [p['q_g']], [], [Q_LORA], TM, [BF16])
        (ckv,) = rowwise(t + 'kv_norm', f_rms, [p_ckv], [p['kv_g']], [], [KV_LORA], TM, [BF16])
        qq = matmul(t + 'uq', cq, p['w_uq'], 'nn')
        kv = matmul(t + 'ukv', ckv, p['w_ukv'], 'nn')
        qh, kh = rowwise(t + 'rope', f_rope, [qq, kv, p_kr, ct, st], [], [], [MLA_HEADS * HEAD_PAD] * 2, TM)
        ya = attention_fwd(t + 'attn', qh, kh, kv, B, S, TQ, BF16)
        sv.update(p_q=p_q, p_ckv=p_ckv, p_kr=p_kr, cq=cq, ckv=ckv, qq=qq, kv=kv, qh=qh, kh=kh, ya=ya)
        (yb,) = rowwise(t + 'sg', f_sg, [p_sg], [p['sg_g'], p['sg_b'], p['sg_w'], p['sg_bias']], [], [SG_DIM], TM,
                        [BF16])
        sv.update(p_sg=p_sg, yb=yb)
        zp, zn = shift_prev(z), shift_next(z)
        rw_par = [p['mu'], p['w0'], p['w2'], p['a0'], p['a2'], p['g2'], p['k_k'], p['k_a']]
        r_, v_, decay, kdir, kk, bdir, g_ = rowwise(
            t + 'rw_pre', f_rw_pre, [z, zp, zn], rw_par, [ones],
            [RWKV_DIM, RWKV_DIM, 2 * RWKV_DIM, 2 * RWKV_DIM, RWKV_DIM, 2 * RWKV_DIM, RWKV_DIM], TM)
        sc = dict(w=to_j(decay), k=to_j(kdir), b=to_j(bdir), kk=to_j(kk), r=to_j(r_), v=to_i(v_))
        ys, sp = scan_fwd(t + 'scan', sc['w'], sc['k'], sc['b'], sc['kk'], sc['r'], sc['v'])
        yd = from_i(ys)
        y0, y1 = yd[0].reshape(N, -1), flip(yd[1]).reshape(N, -1)
        (yc,) = rowwise(t + 'rw_post', f_rw_post, [y0, y1, r_, v_, kdir, g_], [p['ln_g'], p['ln_b'], p['r_k']],
                        [ones], [RWKV_DIM], TM, [BF16])
        sv.update(z=z, zp=zp, zn=zn, r=r_, v=v_, kdir=kdir, g=g_, sc=sc, sp=sp, y0=y0, y1=y1, yc=yc)
        b0 = matmul(t + 'br0', ya, p['wb0'], 'nn')
        b1 = matmul(t + 'br1', yb, p['wb1'], 'nn')
        b2 = matmul(t + 'br2', yc, p['wb2'], 'nn')
        (merged,) = rowwise(t + 'merge', f_merge, [p_gate, b0, b1, b2], [p['gate_b']], [], [D], TM, [BF16])
        x2 = matmul(t + 'out_proj', merged, p['w_out'], 'nn', add=xc)
        sv.update(p_gate=p_gate, b0=b0, b1=b1, b2=b2, merged=merged, x2=x2)
        (h2,) = rowwise(t + 'ffn_norm', f_rms, [x2], [p['ffn_g']], [], [D], TM, [BF16])
        au = matmul(t + 'ffn_in', h2, p['w_gu'], 'nn')
        (act,) = rowwise(t + 'swiglu', f_swiglu, [au], [], [], [D_FF], TM, [BF16])
        xc = matmul(t + 'ffn_out', act, p['w_down'], 'nn', add=x2)
        sv.update(h2=h2, au=au, act=act)
        saved.append(sv)

    loss_part, dx, d_final_g = loss_head(xc, loss_target.reshape(N, D), row(full['final_norm_g']), TM)
    loss = lax.psum(loss_part[0, 0], ("x", "y", "c"))

    G = {n: [None] * DEPTH for n in WEIGHTS if n != 'final_norm_g'}
    for l in reversed(range(DEPTH)):
        p, sv = LW[l], saved[l]
        t = 'l%d_bwd_' % l
        d_act = matmul(t + 'ffn_out_dx', dx, p['w_down'], 'nt')
        G['w_ffn_down'][l] = matmul(t + 'ffn_out_dw', sv['act'], dx, 'tn')
        (d_au,), _ = rowwise_bwd(t + 'swiglu', f_swiglu, [sv['au']], [], [], [[d_act]], TMH, drow_dtypes=[BF16])
        d_h2 = matmul(t + 'ffn_in_dx', d_au, p['w_gu'], 'nt')
        d_wgu = matmul(t + 'ffn_in_dw', sv['h2'], d_au, 'tn')
        G['w_ffn_gate'][l], G['w_ffn_up'][l] = d_wgu[:, :D_FF], d_wgu[:, D_FF:]
        (dx2,), (dg,) = rowwise_bwd(t + 'ffn_norm', f_rms, [sv['x2']], [p['ffn_g']], [], [[d_h2]], TM, extra=[(0, dx)])
        G['ffn_norm_g'][l] = dg.reshape(-1)
        d_merged = matmul(t + 'out_proj_dx', dx2, p['w_out'], 'nt')
        G['w_out'][l] = matmul(t + 'out_proj_dw', sv['merged'], dx2, 'tn')
        (d_pgate, d_b0, d_b1, d_b2), (d_gate_b,) = rowwise_bwd(
            t + 'merge', f_merge, [sv['p_gate'], sv['b0'], sv['b1'], sv['b2']], [p['gate_b']], [], [[d_merged]], TM,
            drow_dtypes=[BF16] * 4)
        G['gate_b'][l] = d_gate_b.reshape(3, D)
        d_ya = matmul(t + 'br0_dx', d_b0, p['wb0'], 'nt')
        d_yb = matmul(t + 'br1_dx', d_b1, p['wb1'], 'nt')
        d_yc = matmul(t + 'br2_dx', d_b2, p['wb2'], 'nt')
        d_wb0 = matmul(t + 'br0_dw', sv['ya'], d_b0, 'tn').reshape(MLA_HEADS, HEAD_PAD, D)[:, QK_NOPE:].reshape(-1, D)
        G['w_branch'][l] = jnp.stack([d_wb0, matmul(t + 'br1_dw', sv['yb'], d_b1, 'tn'),
                                      matmul(t + 'br2_dw', sv['yc'], d_b2, 'tn')])
        (d_y0, _d_y1, d_r1, d_v1, d_kdir1, d_g), (d_ln_g, d_ln_b, d_r_k) = rowwise_bwd(
            t + 'rw_post', f_rw_post, [sv['y0'], sv['y1'], sv['r'], sv['v'], sv['kdir'], sv['g']],
            [p['ln_g'], p['ln_b'], p['r_k']], [ones], [[d_yc]], TMH)
        G['rw_ln_g'][l], G['rw_ln_b'][l] = d_ln_g.reshape(-1), d_ln_b.reshape(-1)
        G['rw_r_k'][l] = d_r_k.reshape(RWKV_HEADS, RWKV_HEAD)
        sc = sv['sc']
        s_dw, s_dk, s_db, s_dkk, s_dr, s_dv = scan_bwd(
            t + 'scan', sc['w'], sc['k'], sc['b'], sc['kk'], sc['r'], sc['v'], sv['sp'], to_i(d_y0))
        rw_par = [p['mu'], p['w0'], p['w2'], p['a0'], p['a2'], p['g2'], p['k_k'], p['k_a']]
        d_outs = [[d_r1] + parts_nodir(from_j(s_dr)), [d_v1] + parts_nodir([from_i(s_dv)]),
                  parts_dir(from_j(s_dw)), [d_kdir1] + parts_dir(from_j(s_dk)), parts_nodir(from_j(s_dkk)),
                  parts_dir(from_j(s_db)), [d_g]]
        (d_z, d_zp, d_zn), d_rw = rowwise_bwd(
            t + 'rw_pre', f_rw_pre, [sv['z'], sv['zp'], sv['zn']], rw_par, [ones], d_outs, TMH)
        (d_prw,) = rowwise(t + 'shift_sum', f_add3, [d_z, shift_next(d_zp), shift_prev(d_zn)], [], [], [RWKV_IN], TM,
                           [BF16])
        G['rw_mu'][l] = d_rw[0].reshape(-1)
        G['rw_w0'][l] = d_rw[1].reshape(2, RWKV_DIM)
        G['rw_w2'][l] = jnp.stack([d_rw[2][:64, :RWKV_DIM], d_rw[2][64:, RWKV_DIM:]])
        G['rw_a0'][l] = d_rw[3].reshape(2, RWKV_DIM)
        G['rw_a2'][l] = jnp.stack([d_rw[4][:64, :RWKV_DIM], d_rw[4][64:, RWKV_DIM:]])
        G['rw_g2'][l] = d_rw[5]
        G['rw_k_k'][l], G['rw_k_a'][l] = d_rw[6].reshape(-1), d_rw[7].reshape(-1)
        (d_psg,), (d_sg_g, d_sg_b, d_sg_w, d_sg_bias) = rowwise_bwd(
            t + 'sg', f_sg, [sv['p_sg']], [p['sg_g'], p['sg_b'], p['sg_w'], p['sg_bias']], [], [[d_yb]], TMH,
            drow_dtypes=[BF16])
        G['sg_ln_g'][l], G['sg_ln_b'][l], G['sg_w'][l] = d_sg_g.reshape(-1), d_sg_b.reshape(-1), d_sg_w
        G['sg_b'][l] = d_sg_bias.reshape(SG_CHUNK, SG_GROUPS, SG_DIM // SG_GROUPS).sum(-1).T
        d_qh, d_kh, d_kvv = attention_bwd(t + 'attn', sv['qh'], sv['kh'], sv['kv'], d_ya, B, S, TQ)
        (d_qq, d_kv, d_pkr), _ = rowwise_bwd(
            t + 'rope', f_rope, [sv['qq'], sv['kv'], sv['p_kr'], ct, st], [], [], [[d_qh], [d_kh]], TM,
            n_row_diff=3, extra=[(1, d_kvv)], drow_dtypes=[BF16] * 3)
        d_cq = matmul(t + 'uq_dx', d_qq, p['w_uq'], 'nt')
        d_wuq = matmul(t + 'uq_dw', sv['cq'], d_qq, 'tn')
        g1, g2_ = _unpermute_full(d_wuq, wuq_src, wuq_sgn, MLA_HEADS * (QK_NOPE + QK_ROPE))
        G['w_uq'][l] = g1 + g2_
        d_ckv = matmul(t + 'ukv_dx', d_kv, p['w_ukv'], 'nt')
        G['w_ukv'][l] = matmul(t + 'ukv_dw', sv['ckv'], d_kv, 'tn')
        (d_pq,), (dg,) = rowwise_bwd(t + 'q_norm', f_rms, [sv['p_q']], [p['q_g']], [], [[d_cq]], TM,
                                     drow_dtypes=[BF16])
        G['q_norm_g'][l] = dg.reshape(-1)
        (d_pckv,), (dg,) = rowwise_bwd(t + 'kv_norm', f_rms, [sv['p_ckv']], [p['kv_g']], [], [[d_ckv]], TM,
                                       drow_dtypes=[BF16])
        G['kv_norm_g'][l] = dg.reshape(-1)
        d_h, d_cols = None, []
        for (sn, a, b), d_seg in zip(IN_SEGMENTS, [d_pq, d_pckv, d_pkr, d_psg, d_prw, d_pgate]):
            d_h = matmul(t + 'in_proj_dx_' + sn, d_seg, p['w_in'][:, a:b], 'nt', add=d_h)
            d_cols.append(matmul(t + 'in_proj_dw_' + sn, sv['h'], d_seg, 'tn'))
        d_win = jnp.concatenate(d_cols, axis=1)
        g1, g2_ = _unpermute_full(d_win, win_src, win_sgn, N_IN)
        kr0 = Q_LORA + KV_LORA
        G['w_in'][l] = g1.at[:, kr0:kr0 + QK_ROPE].add(g2_[:, kr0:kr0 + QK_ROPE])
        (dx,), (dg,) = rowwise_bwd(t + 'attn_norm', f_rms, [sv['x']], [p['attn_g']], [], [[d_h]], TM, extra=[(0, dx2)])
        G['attn_norm_g'][l] = dg.reshape(-1)

    grads = {n: jnp.stack(G[n]) for n in G}
    grads['final_norm_g'] = d_final_g.reshape(-1)
    grad_x = dx.reshape(B, S, D)

    per_shard = []
    for q in range(4):
        sl = []
        for n in SHARDED:
            ax = SHARD_AXIS[n]
            w = W[n].shape[ax]
            sl.append(lax.slice_in_dim(grads[n], q * w, (q + 1) * w, axis=ax))
        per_shard.append(_pack(sl, 2 * 8 * JOIN_CHUNKS))
    R = per_shard[0].shape[0]
    gpack = jnp.stack(per_shard).astype(BF16).reshape(4, 2, R // 2, PACK_C)
    half_sum = sum_slots("sum_sharded", scatter_partials(gpack))
    g_shard = dict(zip(SHARDED, _unpack(sibling_join(half_sum).reshape(R, PACK_C), shard_shapes)))
    rep_shapes = [W[n].shape for n in REPLICATED]
    rpack = _pack([grads[n] for n in REPLICATED], 8)
    g_rep = sum_slots("sum_replicated", gather_all("gather_replicated", rpack))

    outs = {}
    for n in MATMUL_SHARDED:
        shp = W[n].shape
        two = lambda a: a.reshape(-1, shp[-1])
        res = adamw("adamw_" + n, two(W[n]), two(g_shard[n]), two(M1[n]), two(M2[n]))
        outs['grad', n] = g_shard[n]
        for key, a in zip(('delta', 'new_m', 'new_v'), res):
            outs[key, n] = a.reshape(shp)
    small = SMALL_SHARDED + REPLICATED
    small_shapes = [W[n].shape for n in small]
    g_small = [g_shard[n] for n in SMALL_SHARDED] + _unpack(g_rep, rep_shapes)
    res = adamw("adamw_small", _pack([W[n] for n in small], 8), _pack(g_small, 8),
                _pack([M1[n] for n in small], 8), _pack([M2[n] for n in small], 8))
    for n, a in zip(small, g_small):
        outs['grad', n] = a
    for key, buf in zip(('delta', 'new_m', 'new_v'), res):
        for n, a in zip(small, _unpack(buf, small_shapes)):
            outs[key, n] = a
    return (loss, grad_x, *[outs['grad', n] for n in WEIGHTS], *[outs['delta', n] for n in WEIGHTS],
            *[outs['new_m', n] for n in WEIGHTS], *[outs['new_v', n] for n in WEIGHTS])
```

```python
import functools
import math

import numpy as np
import jax
import jax.numpy as jnp
from jax import lax
from jax.experimental import pallas as pl
from jax.experimental.pallas import tpu as pltpu

F32 = jnp.float32
BF16 = jnp.bfloat16

DEPTH = 2
MLA_HEADS = 8
Q_LORA = 384
KV_LORA = 256
QK_NOPE = 64
QK_ROPE = 32
V_HEAD = 64
ROPE_THETA = 10000.0
SG_GROUPS = 8
SG_DIM = 512
SG_CHUNK = 128
RWKV_HEADS = 8
RWKV_HEAD = 64
RWKV_DIM = 512
GN_EPS = 64e-5
NORM_EPS = 1e-6
D_FF = 2816
RWKV_IN = 1920
N_IN = 6688
ADAM_LR, ADAM_B1, ADAM_B2, ADAM_EPS, ADAM_WD, ADAM_STEP = 0.001, 0.9, 0.999, 1e-08, 0.01, 10

LANES = 128
HEAD_PAD = 128
VMEM_LIMIT = 56 * 1024 * 1024
MESH = pl.DeviceIdType.MESH

WEIGHTS = ['attn_norm_g', 'w_in', 'gate_b', 'q_norm_g', 'w_uq', 'kv_norm_g', 'w_ukv', 'sg_ln_g', 'sg_ln_b', 'sg_w',
           'sg_b', 'rw_mu', 'rw_w0', 'rw_w2', 'rw_a0', 'rw_a2', 'rw_g2', 'rw_k_k', 'rw_k_a', 'rw_r_k', 'rw_ln_g',
           'rw_ln_b', 'w_branch', 'w_out', 'ffn_norm_g', 'w_ffn_gate', 'w_ffn_up', 'w_ffn_down', 'final_norm_g']
SHARD_AXIS = {'w_in': 2, 'gate_b': 2, 'w_uq': 2, 'w_ukv': 2, 'rw_w0': 2, 'rw_w2': 3, 'rw_a0': 2, 'rw_a2': 3,
              'rw_g2': 2, 'w_branch': 3, 'w_out': 1, 'w_ffn_gate': 2, 'w_ffn_up': 2, 'w_ffn_down': 1}
SHARDED = [n for n in WEIGHTS if n in SHARD_AXIS]
REPLICATED = [n for n in WEIGHTS if n not in SHARD_AXIS]
SMALL_SHARDED = ['gate_b', 'rw_w0', 'rw_a0']
MATMUL_SHARDED = [n for n in SHARDED if n not in SMALL_SHARDED]


def _params(sem=None):
    return pltpu.CompilerParams(dimension_semantics=sem, vmem_limit_bytes=VMEM_LIMIT)


def _pick(n, cands):
    for c in cands:
        if n % c == 0:
            return c
    return n


def _dot(a, b, dims):
    return lax.dot_general(a.astype(BF16), b.astype(BF16), (dims, ((), ())), preferred_element_type=F32)


def _nn(a, b):
    return _dot(a, b, ((1,), (0,)))


def _nt(a, b):
    return _dot(a, b, ((1,), (1,)))


def _tn(a, b):
    return _dot(a, b, ((0,), (0,)))


@jax.custom_vjp
def mm(a, b):
    return _nn(a, b)


mm.defvjp(lambda a, b: (_nn(a, b), (a, b)), lambda res, g: (_nt(g, res[1]), _tn(res[0], g)))


@jax.custom_vjp
def mm_nt(a, b):
    return _nt(a, b)


mm_nt.defvjp(lambda a, b: (_nt(a, b), (a, b)), lambda res, g: (_nn(g, res[1]), _tn(g, res[0])))


def _seg_raw(x, ones):
    hi = x.astype(BF16)
    lo = (x - hi.astype(F32)).astype(BF16)
    d = (((1,), (0,)), ((), ()))
    return (lax.dot_general(hi, ones, d, preferred_element_type=F32)
            + lax.dot_general(lo, ones, d, preferred_element_type=F32))


@jax.custom_vjp
def segsum(x, ones):
    return _seg_raw(x, ones)


segsum.defvjp(lambda x, ones: (_seg_raw(x, ones), ones),
              lambda ones, g: (_seg_raw(g, ones), jnp.zeros_like(ones)))


def _sigmoid(x):
    return 0.5 * (jnp.tanh(0.5 * x) + 1.0)


def _rms(x, g):
    return x * lax.rsqrt(jnp.mean(x * x, axis=-1, keepdims=True) + NORM_EPS) * g


def matmul(name, a, b, mode, add=None, out_dtype=F32):
    if mode == 'nn':
        (M, K), (_, N) = a.shape, b.shape
    elif mode == 'nt':
        (M, K), (N, _) = a.shape, b.shape
    else:
        (K, M), (_, N) = a.shape, b.shape
    tm = _pick(M, (1408, 1024, 512, 384, 256, 128))
    tn = _pick(N, (1408, 1024, 768, 512, 384, 256, 128))
    tk = _pick(K, (512, 384, 256, 128))
    nk = K // tk
    dims = {'nn': ((1,), (0,)), 'nt': ((1,), (1,)), 'tn': ((0,), (0,))}[mode]
    a_spec = pl.BlockSpec((tk, tm), lambda i, j, k: (k, i)) if mode == 'tn' else pl.BlockSpec((tm, tk), lambda i, j, k: (i, k))
    b_spec = pl.BlockSpec((tn, tk), lambda i, j, k: (j, k)) if mode == 'nt' else pl.BlockSpec((tk, tn), lambda i, j, k: (k, j))
    o_spec = pl.BlockSpec((tm, tn), lambda i, j, k: (i, j))
    has_add = add is not None

    def body(*refs):
        if has_add:
            a_ref, b_ref, add_ref, o_ref, acc = refs
        else:
            a_ref, b_ref, o_ref, acc = refs
        k = pl.program_id(2)

        @pl.when(k == 0)
        def _():
            acc[...] = jnp.zeros_like(acc)

        acc[...] += _dot(a_ref[...], b_ref[...], dims)

        @pl.when(k == nk - 1)
        def _():
            o_ref[...] = (acc[...] + add_ref[...] if has_add else acc[...]).astype(o_ref.dtype)

    ins = [a, b] + ([add] if has_add else [])
    specs = [a_spec, b_spec] + ([o_spec] if has_add else [])
    return pl.pallas_call(
        body, name=name, grid=(M // tm, N // tn, nk), in_specs=specs, out_specs=o_spec,
        out_shape=jax.ShapeDtypeStruct((M, N), out_dtype), scratch_shapes=[pltpu.VMEM((tm, tn), F32)],
        compiler_params=_params(("parallel", "parallel", "arbitrary")))(*ins)


def _full_spec(p):
    nd = p.ndim
    return pl.BlockSpec(p.shape, lambda i, _nd=nd: (0,) * _nd)


def rowwise(name, fn, rows, params, consts, out_widths, tm, out_dtypes=None):
    N = rows[0].shape[0]
    nr, npar, nc = len(rows), len(params), len(consts)

    def body(*refs):
        vals = [r[...] for r in refs[:nr + npar + nc]]
        res = fn(*vals)
        for o, v in zip(refs[nr + npar + nc:], res):
            o[...] = v.astype(o.dtype)

    in_specs = ([pl.BlockSpec((tm, r.shape[1]), lambda i: (i, 0)) for r in rows]
                + [_full_spec(p) for p in list(params) + list(consts)])
    out_specs = [pl.BlockSpec((tm, w), lambda i: (i, 0)) for w in out_widths]
    return pl.pallas_call(
        body, name=name, grid=(N // tm,), in_specs=in_specs, out_specs=out_specs,
        out_shape=[jax.ShapeDtypeStruct((N, w), d) for w, d in zip(out_widths, out_dtypes or [F32] * len(out_widths))],
        compiler_params=_params(("parallel",)))(*rows, *params, *consts)


def rowwise_bwd(name, fn, rows, params, consts, d_outs, tm, n_row_diff=None, extra=(), drow_dtypes=None):
    N = rows[0].shape[0]
    nr, npar, nc = len(rows), len(params), len(consts)
    nd = nr if n_row_diff is None else n_row_diff
    counts = [len(p) for p in d_outs]
    flat_d = [a for parts in d_outs for a in parts]
    nflat, nex = len(flat_d), len(extra)

    def body(*refs):
        pos = 0
        row_v = [r[...] for r in refs[pos:pos + nr]]; pos += nr
        par_v = [r[...] for r in refs[pos:pos + npar]]; pos += npar
        con_v = [r[...] for r in refs[pos:pos + nc]]; pos += nc
        d_refs = refs[pos:pos + nflat]; pos += nflat
        ex_refs = refs[pos:pos + nex]; pos += nex
        drow_refs = refs[pos:pos + nd]; pos += nd
        dpar_refs = refs[pos:pos + npar]

        def f(*diff):
            return fn(*diff[:nd], *row_v[nd:], *diff[nd:], *con_v)

        _, vjp = jax.vjp(f, *row_v[:nd], *par_v)
        cts, q = [], 0
        for c in counts:
            g = d_refs[q][...].astype(F32)
            for t in range(1, c):
                g = g + d_refs[q + t][...].astype(F32)
            cts.append(g)
            q += c
        grads = vjp(tuple(cts))
        drow = list(grads[:nd])
        for (idx, _), r in zip(extra, ex_refs):
            drow[idx] = drow[idx] + r[...].astype(F32)
        for o, v in zip(drow_refs, drow):
            o[...] = v.astype(o.dtype)

        @pl.when(pl.program_id(0) == 0)
        def _():
            for o in dpar_refs:
                o[...] = jnp.zeros_like(o)

        for o, v in zip(dpar_refs, grads[nd:]):
            o[...] += v

    ex_arrs = [a for _, a in extra]
    in_specs = ([pl.BlockSpec((tm, r.shape[1]), lambda i: (i, 0)) for r in rows]
                + [_full_spec(p) for p in list(params) + list(consts)]
                + [pl.BlockSpec((tm, a.shape[1]), lambda i: (i, 0)) for a in flat_d + ex_arrs])
    out_specs = ([pl.BlockSpec((tm, r.shape[1]), lambda i: (i, 0)) for r in rows[:nd]]
                 + [_full_spec(p) for p in params])
    out_shape = ([jax.ShapeDtypeStruct(r.shape, d) for r, d in zip(rows[:nd], drow_dtypes or [F32] * nd)]
                 + [jax.ShapeDtypeStruct(p.shape, F32) for p in params])
    res = pl.pallas_call(
        body, name=name, grid=(N // tm,), in_specs=in_specs, out_specs=out_specs, out_shape=out_shape,
        compiler_params=_params(("arbitrary",)))(*rows, *params, *consts, *flat_d, *ex_arrs)
    return list(res[:nd]), list(res[nd:])


def f_rms(x, g):
    return (_rms(x, g),)


def f_rope(qq, kv, krr, ct, st):
    hw = MLA_HEADS * HEAD_PAD
    c8 = jnp.tile(ct, (1, MLA_HEADS))
    s8 = jnp.tile(st, (1, MLA_HEADS))
    q = qq[:, :hw] * c8 + qq[:, hw:] * s8
    kr = krr[:, :HEAD_PAD] * ct + krr[:, HEAD_PAD:] * st
    lane = lax.broadcasted_iota(jnp.int32, kv.shape, 1) % HEAD_PAD
    k = jnp.where(lane < QK_NOPE, kv, jnp.tile(kr, (1, MLA_HEADS)))
    return q, k


def f_sg(p, ln_g, ln_b, w, bias):
    z = 0.5 * p * (1.0 + jnp.tanh(0.7978845608028654 * (p + 0.044715 * p * p * p)))
    u, v = z[:, :SG_DIM], z[:, SG_DIM:]
    mu = jnp.mean(v, axis=-1, keepdims=True)
    var = jnp.mean(jnp.square(v - mu), axis=-1, keepdims=True)
    v = (v - mu) * lax.rsqrt(var + 1e-5) * ln_g + ln_b
    lane = lax.broadcasted_iota(jnp.int32, (SG_CHUNK, LANES), 1)
    outs = []
    for c in range(p.shape[0] // SG_CHUNK):
        vc = v[c * SG_CHUNK:(c + 1) * SG_CHUNK]
        cols = []
        for m in range(SG_DIM // LANES):
            blk = vc[:, m * LANES:(m + 1) * LANES]
            cols.append(jnp.where(lane < 64, mm(w[2 * m], blk), mm(w[2 * m + 1], blk)))
        outs.append(jnp.concatenate(cols, axis=1) + bias)
    mixed = outs[0] if len(outs) == 1 else jnp.concatenate(outs, axis=0)
    return (u * mixed,)


def f_rw_pre(z, zp, zn, mu, w0, w2, a0, a2, g2, k_k, k_a, ones):
    z = z + mu * (0.5 * (zp + zn) - z)
    C = RWKV_DIM
    r, k, v = z[:, :C], z[:, C:2 * C], z[:, 2 * C:3 * C]
    wl, al, gl = z[:, 3 * C:3 * C + 128], z[:, 3 * C + 128:3 * C + 256], z[:, 3 * C + 256:]
    w = w0 + mm(jnp.tanh(wl), w2)
    decay = jnp.exp(-0.6065306597126334 * _sigmoid(w))
    a = _sigmoid(a0 + mm(al, a2))
    g = mm(_sigmoid(gl), g2)
    kk = k * k_k
    kk = kk / jnp.maximum(jnp.sqrt(segsum(kk * kk, ones)), 1e-12)
    k2 = jnp.concatenate([k, k], axis=1)
    kdir = k2 * (1.0 + (a - 1.0) * jnp.concatenate([k_a, k_a], axis=1))
    bdir = jnp.concatenate([kk, kk], axis=1) * a
    return r, v, decay, kdir, kk, bdir, g


def f_rw_post(y0, y1, r, v, kdir, g, ln_g, ln_b, r_k, ones):
    y = y0 + y1
    mean = segsum(y, ones) * (1.0 / RWKV_HEAD)
    yc = y - mean
    var = segsum(yc * yc, ones) * (1.0 / RWKV_HEAD)
    y = yc * lax.rsqrt(var + GN_EPS) * ln_g + ln_b
    C = RWKV_DIM
    bonus = segsum(r * kdir[:, :C] * r_k, ones) + segsum(r * kdir[:, C:] * r_k, ones)
    return ((y + bonus * v) * g,)


def f_merge(pg, b0, b1, b2, gate_b):
    D = b0.shape[1]
    gt = _sigmoid(pg + gate_b)
    return (gt[:, :D] * b0 + gt[:, D:2 * D] * b1 + gt[:, 2 * D:] * b2,)


def f_swiglu(au):
    a, u = au[:, :D_FF], au[:, D_FF:]
    return (a * _sigmoid(a) * u,)


def f_add3(a, b, c):
    return (a + b + c,)


def loss_head(x, tgt, g, tm):
    N, D = x.shape

    def body(x_ref, t_ref, g_ref, loss_ref, dx_ref, dg_ref):
        t = t_ref[...]

        def f(xv, gv):
            err = _rms(xv, gv) - t
            return 0.5 * jnp.sum(jnp.mean(err * err, axis=-1, keepdims=True))

        val, (dx, dg) = jax.value_and_grad(f, argnums=(0, 1))(x_ref[...], g_ref[...])
        dx_ref[...] = dx

        @pl.when(pl.program_id(0) == 0)
        def _():
            loss_ref[...] = jnp.zeros_like(loss_ref)
            dg_ref[...] = jnp.zeros_like(dg_ref)

        loss_ref[...] += jnp.full(loss_ref.shape, val, F32)
        dg_ref[...] += dg

    row = pl.BlockSpec((tm, D), lambda i: (i, 0))
    return pl.pallas_call(
        body, name="loss_head", grid=(N // tm,), in_specs=[row, row, _full_spec(g)],
        out_specs=[pl.BlockSpec((1, LANES), lambda i: (0, 0)), row, _full_spec(g)],
        out_shape=[jax.ShapeDtypeStruct((1, LANES), F32), jax.ShapeDtypeStruct((N, D), F32),
                   jax.ShapeDtypeStruct(g.shape, F32)],
        compiler_params=_params(("arbitrary",)))(x, tgt, g)


ATT_SCALE = float((QK_NOPE + QK_ROPE) ** -0.5)


def _attn_block(q, k, kv):
    s = mm_nt(q, k) * ATT_SCALE
    m = lax.stop_gradient(jnp.max(s, axis=-1, keepdims=True))
    e = jnp.exp(s - m)
    p = e / jnp.sum(e, axis=-1, keepdims=True)
    return mm(p, kv)


def attention_fwd(name, q, k, kv, B, S, tq, out_dtype):
    nq = S // tq
    qspec = pl.BlockSpec((tq, HEAD_PAD), lambda b, h, i: (b * nq + i, h))
    kspec = pl.BlockSpec((S, HEAD_PAD), lambda b, h, i: (b, h))

    def body(q_ref, k_ref, kv_ref, o_ref):
        o_ref[...] = _attn_block(q_ref[...], k_ref[...], kv_ref[...]).astype(o_ref.dtype)

    return pl.pallas_call(
        body, name=name, grid=(B, MLA_HEADS, nq), in_specs=[qspec, kspec, kspec], out_specs=qspec,
        out_shape=jax.ShapeDtypeStruct(q.shape, out_dtype),
        compiler_params=_params(("parallel", "parallel", "arbitrary")))(q, k, kv)


def attention_bwd(name, q, k, kv, do, B, S, tq):
    nq = S // tq
    qspec = pl.BlockSpec((tq, HEAD_PAD), lambda b, h, i: (b * nq + i, h))
    kspec = pl.BlockSpec((S, HEAD_PAD), lambda b, h, i: (b, h))

    def body(q_ref, k_ref, kv_ref, do_ref, dq_ref, dk_ref, dkv_ref):
        _, vjp = jax.vjp(_attn_block, q_ref[...], k_ref[...], kv_ref[...])
        dq, dk, dkv = vjp(do_ref[...])
        dq_ref[...] = dq

        @pl.when(pl.program_id(2) == 0)
        def _():
            dk_ref[...] = jnp.zeros_like(dk_ref)
            dkv_ref[...] = jnp.zeros_like(dkv_ref)

        dk_ref[...] += dk
        dkv_ref[...] += dkv

    sh = jax.ShapeDtypeStruct(q.shape, F32)
    return pl.pallas_call(
        body, name=name, grid=(B, MLA_HEADS, nq), in_specs=[qspec, kspec, kspec, qspec],
        out_specs=[qspec, kspec, kspec], out_shape=[sh, sh, sh],
        compiler_params=_params(("parallel", "parallel", "arbitrary")))(q, k, kv, do)


SCAN_TC = 8
SCAN_UNROLL = 16


def _jloop(n, body, init):
    def outer(o, c):
        for u in range(SCAN_UNROLL):
            c = body(o * SCAN_UNROLL + u, c)
        return c

    return lax.fori_loop(0, n // SCAN_UNROLL, outer, init)


def _dir_mask(L, Ip):
    per_dir = L // (2 * (RWKV_HEAD // Ip))
    lane = lax.broadcasted_iota(jnp.int32, (1, L), 1)
    return (lane // per_dir) % 2 == 1


def _merge_dirs(mask, fwd_ref, rev_ref, out_ref):
    for tt in range(SCAN_TC):
        out_ref[tt] = jnp.where(mask, rev_ref[SCAN_TC - 1 - tt], fwd_ref[tt])


def scan_fwd(name, w, k, b, kk, r, v):
    T, J, L = w.shape
    Ip = v.shape[1]
    nT = T // SCAN_TC
    fwd3, rev3 = (lambda g: (g, 0, 0)), (lambda g: (nT - 1 - g, 0, 0))
    jf, jr = pl.BlockSpec((SCAN_TC, J, L), fwd3), pl.BlockSpec((SCAN_TC, J, L), rev3)
    i_f, i_r = pl.BlockSpec((SCAN_TC, Ip, L), fwd3), pl.BlockSpec((SCAN_TC, Ip, L), rev3)
    sspec = pl.BlockSpec((SCAN_TC, J, Ip, L), lambda g: (g, 0, 0, 0))
    last_spec = pl.BlockSpec((J, Ip, L), lambda g: (0, 0, 0))

    def body(wf, wr, kf, kr, bf, br, kkf, kkr, rf, rr, vf, vr, yf_ref, yr_ref, sp_ref, sa_ref, last_ref,
             s_ref, w_ref, k_ref, b_ref, kk_ref, r_ref, v_ref):
        @pl.when(pl.program_id(0) == 0)
        def _():
            s_ref[...] = jnp.zeros_like(s_ref)

        mask = _dir_mask(L, Ip)
        for f_, r_, m_ in ((wf, wr, w_ref), (kf, kr, k_ref), (bf, br, b_ref), (kkf, kkr, kk_ref), (rf, rr, r_ref),
                           (vf, vr, v_ref)):
            _merge_dirs(mask, f_, r_, m_)

        def row(ref, tt, j):
            return jnp.broadcast_to(ref[tt, pl.ds(j, 1), :], (Ip, L))

        def step(tt, carry):
            def p1(j, sa):
                s = s_ref[j]
                sp_ref[tt, j] = s
                return sa + s * row(kk_ref, tt, j)

            sa = _jloop(J, p1, jnp.zeros((Ip, L), F32))
            sa_ref[tt] = sa
            vt = v_ref[tt]

            def p2(j, y):
                s = s_ref[j] * row(w_ref, tt, j) - sa * row(b_ref, tt, j) + vt * row(k_ref, tt, j)
                s_ref[j] = s
                return y + s * row(r_ref, tt, j)

            y = _jloop(J, p2, jnp.zeros((Ip, L), F32))
            yf_ref[tt] = y
            yr_ref[SCAN_TC - 1 - tt] = y
            return carry

        lax.fori_loop(0, SCAN_TC, step, 0)

        @pl.when(pl.program_id(0) == nT - 1)
        def _():
            last_ref[...] = s_ref[...]

    ish = jax.ShapeDtypeStruct((T, Ip, L), F32)
    jscr = pltpu.VMEM((SCAN_TC, J, L), F32)
    return pl.pallas_call(
        body, name=name, grid=(nT,), in_specs=[jf, jr] * 5 + [i_f, i_r],
        out_specs=[i_f, i_r, sspec, i_f, last_spec],
        out_shape=[ish, ish, jax.ShapeDtypeStruct((T, J, Ip, L), F32), ish, jax.ShapeDtypeStruct((J, Ip, L), F32)],
        scratch_shapes=[pltpu.VMEM((J, Ip, L), F32)] + [jscr] * 5 + [pltpu.VMEM((SCAN_TC, Ip, L), F32)],
        compiler_params=_params(("arbitrary",)))(w, w, k, k, b, b, kk, kk, r, r, v, v)


def scan_bwd(name, w, k, b, kk, r, v, sp, sa_all, s_last, dy):
    T, J, L = w.shape
    Ip = v.shape[1]
    nT = T // SCAN_TC
    stp3, mir3 = (lambda g: (nT - 1 - g, 0, 0)), (lambda g: (g, 0, 0))
    jf, jr = pl.BlockSpec((SCAN_TC, J, L), stp3), pl.BlockSpec((SCAN_TC, J, L), mir3)
    i_f, i_r = pl.BlockSpec((SCAN_TC, Ip, L), stp3), pl.BlockSpec((SCAN_TC, Ip, L), mir3)
    sspec = pl.BlockSpec((SCAN_TC, J, Ip, L), lambda g: (nT - 1 - g, 0, 0, 0))
    last_spec = pl.BlockSpec((J, Ip, L), lambda g: (0, 0, 0))

    def body(wf, wr, kf, kr, bf, br, kkf, kkr, rf, rr, vf, vr, dyf, dyr, sp_ref, sa_ref, last_ref,
             dwf, dwr, dkf, dkr, dbf, dbr, dkkf, dkkr, drf, drr, dvf, dvr,
             ds_ref, nxt_ref, w_ref, k_ref, b_ref, kk_ref, r_ref, v_ref, dy_ref):
        @pl.when(pl.program_id(0) == 0)
        def _():
            ds_ref[...] = jnp.zeros_like(ds_ref)
            nxt_ref[...] = last_ref[...]

        mask = _dir_mask(L, Ip)
        for f_, r_, m_ in ((wf, wr, w_ref), (kf, kr, k_ref), (bf, br, b_ref), (kkf, kkr, kk_ref), (rf, rr, r_ref),
                           (vf, vr, v_ref), (dyf, dyr, dy_ref)):
            _merge_dirs(mask, f_, r_, m_)

        def row(ref, tt, j):
            return jnp.broadcast_to(ref[tt, pl.ds(j, 1), :], (Ip, L))

        def rsum(x):
            return jnp.sum(x, axis=0, keepdims=True)

        def make_step(first):
            def step(n, carry):
                tt = SCAN_TC - 1 - n
                dyt, vt, sa = dy_ref[tt], v_ref[tt], sa_ref[tt]

                def p1(j, c):
                    dsa, dv = c
                    ds = ds_ref[j] + dyt * row(r_ref, tt, j)
                    ds_ref[j] = ds
                    return dsa - ds * row(b_ref, tt, j), dv + ds * row(k_ref, tt, j)

                z = jnp.zeros((Ip, L), F32)
                dsa, dv = _jloop(J, p1, (z, z))
                dvf[tt] = dv
                dvr[SCAN_TC - 1 - tt] = dv

                def put(f_ref, r_ref_, j, val):
                    f_ref[tt, pl.ds(j, 1), :] = val
                    r_ref_[SCAN_TC - 1 - tt, pl.ds(j, 1), :] = val

                def p2(j, c):
                    ds = ds_ref[j]
                    s0 = sp_ref[tt, j]
                    s1 = nxt_ref[j] if first else sp_ref[tt + 1, j]
                    put(drf, drr, j, rsum(s1 * dyt))
                    put(dkf, dkr, j, rsum(ds * vt))
                    put(dbf, dbr, j, -rsum(ds * sa))
                    put(dwf, dwr, j, rsum(ds * s0))
                    put(dkkf, dkkr, j, rsum(s0 * dsa))
                    ds_ref[j] = ds * row(w_ref, tt, j) + dsa * row(kk_ref, tt, j)
                    return c

                _jloop(J, p2, 0)
                return carry

            return step

        make_step(True)(0, 0)
        lax.fori_loop(1, SCAN_TC, make_step(False), 0)
        nxt_ref[...] = sp_ref[0]

    jsh = jax.ShapeDtypeStruct((T, J, L), F32)
    ish = jax.ShapeDtypeStruct((T, Ip, L), F32)
    jscr = pltpu.VMEM((SCAN_TC, J, L), F32)
    iscr = pltpu.VMEM((SCAN_TC, Ip, L), F32)
    return pl.pallas_call(
        body, name=name, grid=(nT,), in_specs=[jf, jr] * 5 + [i_f, i_r] * 2 + [sspec, i_f, last_spec],
        out_specs=[jf, jr] * 5 + [i_f, i_r], out_shape=[jsh] * 10 + [ish] * 2,
        scratch_shapes=[pltpu.VMEM((J, Ip, L), F32)] * 2 + [jscr] * 5 + [iscr] * 2,
        compiler_params=_params(("arbitrary",)))(w, w, k, k, b, b, kk, kk, r, r, v, v, dy, dy, sp, sa_all, s_last)


def adamw(name, w, g, m, v):
    R, C = w.shape
    tr = _pick(R, (256, 128, 64, 32, 16, 8))
    c1 = 1.0 - ADAM_B1 ** ADAM_STEP
    c2 = 1.0 - ADAM_B2 ** ADAM_STEP

    def body(w_ref, g_ref, m_ref, v_ref, d_ref, nm_ref, nv_ref):
        gv = g_ref[...]
        nm = ADAM_B1 * m_ref[...] + (1.0 - ADAM_B1) * gv
        nv = ADAM_B2 * v_ref[...] + (1.0 - ADAM_B2) * jnp.square(gv)
        d_ref[...] = -ADAM_LR * ((nm / c1) / (jnp.sqrt(nv / c2) + ADAM_EPS) + ADAM_WD * w_ref[...])
        nm_ref[...] = nm
        nv_ref[...] = nv

    spec = pl.BlockSpec((tr, C), lambda i: (i, 0))
    sh = jax.ShapeDtypeStruct((R, C), F32)
    return pl.pallas_call(body, name=name, grid=(R // tr,), in_specs=[spec] * 4, out_specs=[spec] * 3,
                          out_shape=[sh] * 3, compiler_params=_params(("parallel",)))(w, g, m, v)


def sum_slots(name, x):
    n, R, C = x.shape
    tr = _pick(R, (256, 128, 64, 32, 16, 8))

    def body(x_ref, o_ref):
        acc = x_ref[0].astype(F32)
        for s in range(1, n):
            acc = acc + x_ref[s].astype(F32)
        o_ref[...] = acc

    return pl.pallas_call(
        body, name=name, grid=(R // tr,), in_specs=[pl.BlockSpec((n, tr, C), lambda i: (0, i, 0))],
        out_specs=pl.BlockSpec((tr, C), lambda i: (i, 0)), out_shape=jax.ShapeDtypeStruct((R, C), F32),
        compiler_params=_params(("parallel",)))(x)


ANY = pl.BlockSpec(memory_space=pl.ANY)


def _xyc():
    return lax.axis_index("x"), lax.axis_index("y"), lax.axis_index("c")


def gather_shards(shard):
    _, R, C = shard.shape

    def body(x_ref, out_ref, send_sems, recv_sems, local_sem):
        x, y, c = _xyc()
        me, sibling = (x, y, c), (x, y, 1 - c)
        chips = [(1 - x, y), (x, 1 - y), (1 - x, 1 - y)]

        def cp(k, cx, cy, half, to, src=None):
            dst = out_ref.at[2 * cx + cy, half]
            return pltpu.make_async_remote_copy(
                src_ref=dst if src is None else src, dst_ref=dst, send_sem=send_sems.at[k],
                recv_sem=recv_sems.at[k], device_id=to, device_id_type=MESH)

        mine = pltpu.make_async_copy(x_ref, out_ref.at[2 * x + y], local_sem)
        mine.start()
        first = [cp(j, x, y, c, (*chip, c), src=x_ref.at[c]) for j, chip in enumerate(chips)]
        for f in first:
            f.start()
        passed = [cp(3 + j, *chip, c, sibling) for j, chip in enumerate(chips)]
        for j, chip in enumerate(chips):
            cp(j, *chip, c, me).wait_recv()
            passed[j].start()
        for j, chip in enumerate(chips):
            cp(3 + j, *chip, 1 - c, me).wait_recv()
        for f in first + passed:
            f.wait_send()
        mine.wait()

    return pl.pallas_call(
        body, name="gather_shards", in_specs=[ANY], out_specs=ANY,
        out_shape=jax.ShapeDtypeStruct((4, 2, R, C), shard.dtype),
        scratch_shapes=[pltpu.SemaphoreType.DMA((6,)), pltpu.SemaphoreType.DMA((6,)), pltpu.SemaphoreType.DMA])(shard)


FLIPS = [(0, 0, 1), (0, 1, 0), (0, 1, 1), (1, 0, 0), (1, 0, 1), (1, 1, 0), (1, 1, 1)]


def scatter_partials(g):
    _, _, R, C = g.shape

    def body(g_ref, out_ref, send_sems, recv_sems, local_sem):
        x, y, c = _xyc()
        me_idx = 4 * x + 2 * y + c
        mine = pltpu.make_async_copy(g_ref.at[2 * x + y, c], out_ref.at[me_idx], local_sem)
        mine.start()
        sends = []
        for k, (fx, fy, fc) in enumerate(FLIPS):
            px, py, pc = (x + fx) % 2, (y + fy) % 2, (c + fc) % 2
            s = pltpu.make_async_remote_copy(
                src_ref=g_ref.at[2 * px + py, pc], dst_ref=out_ref.at[me_idx], send_sem=send_sems.at[k],
                recv_sem=recv_sems.at[k], device_id=(px, py, pc), device_id_type=MESH)
            s.start()
            sends.append(s)
        for k, (fx, fy, fc) in enumerate(FLIPS):
            px, py, pc = (x + fx) % 2, (y + fy) % 2, (c + fc) % 2
            slot = out_ref.at[4 * px + 2 * py + pc]
            pltpu.make_async_remote_copy(
                src_ref=slot, dst_ref=slot, send_sem=send_sems.at[k], recv_sem=recv_sems.at[k],
                device_id=(px, py, pc), device_id_type=MESH).wait_recv()
        for s in sends:
            s.wait_send()
        mine.wait()

    return pl.pallas_call(
        body, name="scatter_partials", in_specs=[ANY], out_specs=ANY,
        out_shape=jax.ShapeDtypeStruct((8, R, C), g.dtype),
        scratch_shapes=[pltpu.SemaphoreType.DMA((7,)), pltpu.SemaphoreType.DMA((7,)), pltpu.SemaphoreType.DMA])(g)


JOIN_CHUNKS = 8


def sibling_join(half):
    R, C = half.shape
    rows = R // JOIN_CHUNKS

    def body(h_ref, out_ref, send_sems, recv_sems, local_sem):
        x, y, c = _xyc()
        mine = pltpu.make_async_copy(h_ref, out_ref.at[c], local_sem)
        mine.start()
        sends = []
        for k in range(JOIN_CHUNKS):
            s = pltpu.make_async_remote_copy(
                src_ref=h_ref.at[pl.ds(k * rows, rows)], dst_ref=out_ref.at[c, pl.ds(k * rows, rows)],
                send_sem=send_sems.at[k], recv_sem=recv_sems.at[k], device_id=(x, y, 1 - c), device_id_type=MESH)
            s.start()
            sends.append(s)
        for k in range(JOIN_CHUNKS):
            theirs = out_ref.at[1 - c, pl.ds(k * rows, rows)]
            pltpu.make_async_remote_copy(src_ref=theirs, dst_ref=theirs, send_sem=send_sems.at[k],
                                         recv_sem=recv_sems.at[k], device_id=(x, y, 1 - c),
                                         device_id_type=MESH).wait_recv()
        for s in sends:
            s.wait_send()
        mine.wait()

    return pl.pallas_call(
        body, name="sibling_join", in_specs=[ANY], out_specs=ANY,
        out_shape=jax.ShapeDtypeStruct((2, R, C), half.dtype),
        scratch_shapes=[pltpu.SemaphoreType.DMA((JOIN_CHUNKS,)), pltpu.SemaphoreType.DMA((JOIN_CHUNKS,)),
                        pltpu.SemaphoreType.DMA])(half)


def gather_all(name, block):
    R, C = block.shape

    def body(x_ref, out_ref, send_sems, recv_sems, local_sem):
        x, y, c = _xyc()
        mine = pltpu.make_async_copy(x_ref, out_ref.at[4 * x + 2 * y + c], local_sem)
        mine.start()
        sends = []
        for k, (fx, fy, fc) in enumerate(FLIPS):
            px, py, pc = (x + fx) % 2, (y + fy) % 2, (c + fc) % 2
            s = pltpu.make_async_remote_copy(
                src_ref=x_ref, dst_ref=out_ref.at[4 * x + 2 * y + c], send_sem=send_sems.at[k],
                recv_sem=recv_sems.at[k], device_id=(px, py, pc), device_id_type=MESH)
            s.start()
            sends.append(s)
        for k, (fx, fy, fc) in enumerate(FLIPS):
            px, py, pc = (x + fx) % 2, (y + fy) % 2, (c + fc) % 2
            slot = out_ref.at[4 * px + 2 * py + pc]
            pltpu.make_async_remote_copy(
                src_ref=slot, dst_ref=slot, send_sem=send_sems.at[k], recv_sem=recv_sems.at[k],
                device_id=(px, py, pc), device_id_type=MESH).wait_recv()
        for s in sends:
            s.wait_send()
        mine.wait()

    return pl.pallas_call(
        body, name=name, in_specs=[ANY], out_specs=ANY,
        out_shape=jax.ShapeDtypeStruct((8, R, C), block.dtype),
        scratch_shapes=[pltpu.SemaphoreType.DMA((7,)), pltpu.SemaphoreType.DMA((7,)), pltpu.SemaphoreType.DMA])(block)


PACK_C = 1024


def _pack(arrs, row_mult):
    flat = jnp.concatenate([a.reshape(-1) for a in arrs])
    n = flat.shape[0]
    rows = -(-n // PACK_C)
    rows = -(-rows // row_mult) * row_mult
    return jnp.pad(flat, (0, rows * PACK_C - n)).reshape(rows, PACK_C)


def _unpack(buf, shapes):
    flat = buf.reshape(-1)
    out, off = [], 0
    for s in shapes:
        n = int(np.prod(s))
        out.append(flat[off:off + n].reshape(s))
        off += n
    return out


OFF_Q, OFF_CKV, OFF_KR, OFF_SG, OFF_RW, OFF_GATE, N_IN_PAD = 0, 384, 640, 896, 1920, 3840, 6912
IN_SEGMENTS = [('q', OFF_Q, OFF_CKV), ('ckv', OFF_CKV, OFF_KR), ('kr', OFF_KR, OFF_SG), ('sg', OFF_SG, OFF_RW),
               ('rw', OFF_RW, OFF_GATE), ('gate', OFF_GATE, N_IN_PAD)]
ROPE_LANE = QK_NOPE
HALF = QK_ROPE // 2


def _win_layout():
    src = np.full((N_IN_PAD,), -1, np.int64)
    sgn = np.ones((N_IN_PAD,), np.float32)
    src[0:640] = np.arange(0, 640)
    kr0 = Q_LORA + KV_LORA
    src[OFF_KR + ROPE_LANE:OFF_KR + ROPE_LANE + QK_ROPE] = kr0 + np.arange(QK_ROPE)
    sw = OFF_KR + HEAD_PAD + ROPE_LANE
    src[sw:sw + HALF] = kr0 + HALF + np.arange(HALF)
    sgn[sw:sw + HALF] = -1.0
    src[sw + HALF:sw + QK_ROPE] = kr0 + np.arange(HALF)
    src[OFF_SG:N_IN_PAD] = 672 + np.arange(N_IN_PAD - OFF_SG)
    return src, sgn


def _wuq_layout():
    hw = MLA_HEADS * HEAD_PAD
    src = np.full((2 * hw,), -1, np.int64)
    sgn = np.ones((2 * hw,), np.float32)
    per = QK_NOPE + QK_ROPE
    for h in range(MLA_HEADS):
        src[h * HEAD_PAD:h * HEAD_PAD + per] = h * per + np.arange(per)
        sw = hw + h * HEAD_PAD + ROPE_LANE
        src[sw:sw + HALF] = h * per + QK_NOPE + HALF + np.arange(HALF)
        sgn[sw:sw + HALF] = -1.0
        src[sw + HALF:sw + QK_ROPE] = h * per + QK_NOPE + np.arange(HALF)
    return src, sgn


def _permute_cols(w, src, sgn):
    cols = jnp.take(w, jnp.asarray(np.maximum(src, 0)), axis=1)
    return cols * jnp.asarray(np.where(src >= 0, sgn, 0.0).astype(np.float32)).astype(w.dtype)


def _unpermute_full(dw, src, sgn, n_cols):
    first = np.full((n_cols,), -1, np.int64)
    second = np.full((n_cols,), -1, np.int64)
    for pos, s in enumerate(src):
        if s < 0:
            continue
        if first[s] < 0:
            first[s] = pos
        else:
            second[s] = pos
    out = jnp.take(dw, jnp.asarray(first), axis=1) * jnp.asarray(sgn[first].astype(np.float32))
    m2 = (second >= 0)
    two = jnp.take(dw, jnp.asarray(np.maximum(second, 0)), axis=1) * jnp.asarray(
        np.where(m2, sgn[np.maximum(second, 0)], 0.0).astype(np.float32))
    return out, two


def _blockdiag(w):
    z = jnp.zeros_like(w[0])
    return jnp.concatenate([jnp.concatenate([w[0], z], axis=1), jnp.concatenate([z, w[1]], axis=1)], axis=0)


def _rope_tables(pos):
    inv = 1.0 / (ROPE_THETA ** (jnp.arange(0, QK_ROPE, 2, dtype=F32) / QK_ROPE))
    ang = pos.astype(F32)[:, None] * inv[None, :]
    cos, sin = jnp.cos(ang), jnp.sin(ang)
    pad = lambda t, fill: jnp.concatenate(
        [jnp.full((t.shape[0], ROPE_LANE), fill, F32), t, t, jnp.full((t.shape[0], HEAD_PAD - ROPE_LANE - QK_ROPE), fill, F32)], axis=1)
    return pad(cos, 1.0), pad(sin, 0.0)


def kernel(x, positions, attn_norm_g, w_in, gate_b, q_norm_g, w_uq, kv_norm_g, w_ukv, sg_ln_g, sg_ln_b, sg_w, sg_b, rw_mu, rw_w0, rw_w2, rw_a0, rw_a2, rw_g2, rw_k_k, rw_k_a, rw_r_k, rw_ln_g, rw_ln_b, w_branch, w_out, ffn_norm_g, w_ffn_gate, w_ffn_up, w_ffn_down, final_norm_g, loss_target, m_attn_norm_g, m_w_in, m_gate_b, m_q_norm_g, m_w_uq, m_kv_norm_g, m_w_ukv, m_sg_ln_g, m_sg_ln_b, m_sg_w, m_sg_b, m_rw_mu, m_rw_w0, m_rw_w2, m_rw_a0, m_rw_a2, m_rw_g2, m_rw_k_k, m_rw_k_a, m_rw_r_k, m_rw_ln_g, m_rw_ln_b, m_w_branch, m_w_out, m_ffn_norm_g, m_w_ffn_gate, m_w_ffn_up, m_w_ffn_down, m_final_norm_g, v_attn_norm_g, v_w_in, v_gate_b, v_q_norm_g, v_w_uq, v_kv_norm_g, v_w_ukv, v_sg_ln_g, v_sg_ln_b, v_sg_w, v_sg_b, v_rw_mu, v_rw_w0, v_rw_w2, v_rw_a0, v_rw_a2, v_rw_g2, v_rw_k_k, v_rw_k_a, v_rw_r_k, v_rw_ln_g, v_rw_ln_b, v_w_branch, v_w_out, v_ffn_norm_g, v_w_ffn_gate, v_w_ffn_up, v_w_ffn_down, v_final_norm_g):
    args = locals()
    W = {n: args[n] for n in WEIGHTS}
    M1 = {n: args['m_' + n] for n in WEIGHTS}
    M2 = {n: args['v_' + n] for n in WEIGHTS}
    B, S, D = x.shape
    N = B * S
    TM = _pick(N, (256, 128))
    TMH = 128
    TQ = _pick(S, (256, 128))

    shard_shapes = [W[n].shape for n in SHARDED]
    full = {}
    mm_pack = _pack([W[n].astype(BF16) for n in MATMUL_SHARDED], 32)
    Rm = mm_pack.shape[0]
    gathered = gather_shards(mm_pack.reshape(2, Rm // 2, PACK_C)).reshape(4, Rm, PACK_C)
    pieces = [_unpack(gathered[q], [W[n].shape for n in MATMUL_SHARDED]) for q in range(4)]
    for i, n in enumerate(MATMUL_SHARDED):
        full[n] = jnp.concatenate([pieces[q][i] for q in range(4)], axis=SHARD_AXIS[n])
    small = gather_all("gather_small", _pack([W[n] for n in SMALL_SHARDED], 8))
    pieces = [_unpack(small[2 * q], [W[n].shape for n in SMALL_SHARDED]) for q in range(4)]
    for i, n in enumerate(SMALL_SHARDED):
        full[n] = jnp.concatenate([pieces[q][i] for q in range(4)], axis=SHARD_AXIS[n])
    for n in ('rw_w2', 'rw_a2', 'rw_g2'):
        full[n] = full[n].astype(F32)
    for n in REPLICATED:
        full[n] = W[n]

    win_src, win_sgn = _win_layout()
    wuq_src, wuq_sgn = _wuq_layout()
    ones = jnp.asarray(np.kron(np.eye(RWKV_HEADS), np.ones((RWKV_HEAD, RWKV_HEAD))), BF16)
    ct, st = _rope_tables(positions.reshape(N))
    row = lambda v: v.reshape(1, -1)

    H, HD = RWKV_HEADS, RWKV_HEAD
    inst = 2 * B * H
    isplit = LANES // inst
    Ip = HD // isplit

    def to_j(c):
        if c.shape[1] == RWKV_DIM:
            t = c.reshape(B, S, H, HD).transpose(1, 3, 0, 2).reshape(S, HD, B * H)
            return jnp.tile(t, (1, 1, 2 * isplit))
        t = c.reshape(B, S, 2, H, HD).transpose(1, 4, 2, 0, 3).reshape(S, HD, inst)
        return jnp.tile(t, (1, 1, isplit))

    def to_i(c):
        t = c.reshape(B, S, H, isplit, Ip).transpose(1, 4, 3, 0, 2)
        return jnp.broadcast_to(t[:, :, :, None], (S, Ip, isplit, 2, B, H)).reshape(S, Ip, LANES)

    def from_i(tf, tr):
        pick = lambda t, d: t.reshape(S, Ip, isplit, 2, B, H)[:, :, :, d].transpose(3, 0, 4, 2, 1).reshape(N, RWKV_DIM)
        return pick(tf, 0), pick(tr, 1)

    def from_j(tf, tr):
        f6, r6 = tf.reshape(S, HD, isplit, 2, B, H), tr.reshape(S, HD, isplit, 2, B, H)
        pick = lambda t, i2, d: t[:, :, i2, d].transpose(2, 0, 3, 1).reshape(N, RWKV_DIM)
        return [(pick(f6, i2, 0), pick(r6, i2, 1)) for i2 in range(isplit)]

    flat = lambda pairs: [a for pair in pairs for a in pair]
    cat = lambda pairs: [jnp.concatenate(pair, axis=1) for pair in pairs]

    def shift_prev(z):
        z = z.reshape(B, S, -1)
        return jnp.pad(z[:, :-1], ((0, 0), (1, 0), (0, 0))).reshape(N, -1)

    def shift_next(z):
        z = z.reshape(B, S, -1)
        return jnp.pad(z[:, 1:], ((0, 0), (0, 1), (0, 0))).reshape(N, -1)

    LW = []
    for l in range(DEPTH):
        wb = full['w_branch'][l]
        wb0 = jnp.zeros((MLA_HEADS, HEAD_PAD, D), F32).at[:, QK_NOPE:].set(wb[0].reshape(MLA_HEADS, V_HEAD, D))
        LW.append(dict(
            attn_g=row(full['attn_norm_g'][l]),
            w_in=_permute_cols(full['w_in'][l], win_src, win_sgn),
            gate_b=row(full['gate_b'][l]),
            q_g=row(full['q_norm_g'][l]),
            w_uq=_permute_cols(full['w_uq'][l], wuq_src, wuq_sgn),
            kv_g=row(full['kv_norm_g'][l]),
            w_ukv=full['w_ukv'][l],
            sg_g=row(full['sg_ln_g'][l]), sg_b=row(full['sg_ln_b'][l]), sg_w=full['sg_w'][l],
            sg_bias=jnp.repeat(full['sg_b'][l].T, SG_DIM // SG_GROUPS, axis=1),
            mu=row(full['rw_mu'][l]), w0=row(full['rw_w0'][l]), w2=_blockdiag(full['rw_w2'][l]),
            a0=row(full['rw_a0'][l]), a2=_blockdiag(full['rw_a2'][l]), g2=full['rw_g2'][l],
            k_k=row(full['rw_k_k'][l]), k_a=row(full['rw_k_a'][l]), r_k=row(full['rw_r_k'][l]),
            ln_g=row(full['rw_ln_g'][l]), ln_b=row(full['rw_ln_b'][l]),
            wb0=wb0.reshape(MLA_HEADS * HEAD_PAD, D), wb1=wb[1], wb2=wb[2],
            w_out=full['w_out'][l], ffn_g=row(full['ffn_norm_g'][l]),
            w_gu=jnp.concatenate([full['w_ffn_gate'][l], full['w_ffn_up'][l]], axis=1),
            w_down=full['w_ffn_down'][l]))

    saved = []
    xc = x.reshape(N, D)
    for l in range(DEPTH):
        p = LW[l]
        t = 'l%d_' % l
        sv = dict(x=xc)
        (h,) = rowwise(t + 'attn_norm', f_rms, [xc], [p['attn_g']], [], [D], TM, [BF16])
        p_q, p_ckv, p_kr, p_sg, z, p_gate = [
            matmul(t + 'in_proj_' + sn, h, p['w_in'][:, a:b], 'nn') for sn, a, b in IN_SEGMENTS]
        sv['h'] = h
        (cq,) = rowwise(t + 'q_norm', f_rms, [p_q], [p['q_g']], [], [Q_LORA], TM, [BF16])
        (ckv,) = rowwise(t + 'kv_norm', f_rms, [p_ckv], [p['kv_g']], [], [KV_LORA], TM, [BF16])
        qq = matmul(t + 'uq', cq, p['w_uq'], 'nn')
        kv = matmul(t + 'ukv', ckv, p['w_ukv'], 'nn')
        qh, kh = rowwise(t + 'rope', f_rope, [qq, kv, p_kr, ct, st], [], [], [MLA_HEADS * HEAD_PAD] * 2, TM)
        ya = attention_fwd(t + 'attn', qh, kh, kv, B, S, TQ, BF16)
        sv.update(p_q=p_q, p_ckv=p_ckv, p_kr=p_kr, cq=cq, ckv=ckv, qq=qq, kv=kv, qh=qh, kh=kh, ya=ya)
        (yb,) = rowwise(t + 'sg', f_sg, [p_sg], [p['sg_g'], p['sg_b'], p['sg_w'], p['sg_bias']], [], [SG_DIM], TM,
                        [BF16])
        sv.update(p_sg=p_sg, yb=yb)
        zp, zn = shift_prev(z), shift_next(z)
        rw_par = [p['mu'], p['w0'], p['w2'], p['a0'], p['a2'], p['g2'], p['k_k'], p['k_a']]
        r_, v_, decay, kdir, kk, bdir, g_ = rowwise(
            t + 'rw_pre', f_rw_pre, [z, zp, zn], rw_par, [ones],
            [RWKV_DIM, RWKV_DIM, 2 * RWKV_DIM, 2 * RWKV_DIM, RWKV_DIM, 2 * RWKV_DIM, RWKV_DIM], TM)
        sc = dict(w=to_j(decay), k=to_j(kdir), b=to_j(bdir), kk=to_j(kk), r=to_j(r_), v=to_i(v_))
        y_f, y_r, sp, sa_all, s_last = scan_fwd(t + 'scan', sc['w'], sc['k'], sc['b'], sc['kk'], sc['r'], sc['v'])
        y0, y1 = from_i(y_f, y_r)
        (yc,) = rowwise(t + 'rw_post', f_rw_post, [y0, y1, r_, v_, kdir, g_], [p['ln_g'], p['ln_b'], p['r_k']],
                        [ones], [RWKV_DIM], TM, [BF16])
        sv.update(z=z, zp=zp, zn=zn, r=r_, v=v_, kdir=kdir, g=g_, sc=sc, sp=sp, sa=sa_all, s_last=s_last, y0=y0,
                  y1=y1, yc=yc)
        b0 = matmul(t + 'br0', ya, p['wb0'], 'nn')
        b1 = matmul(t + 'br1', yb, p['wb1'], 'nn')
        b2 = matmul(t + 'br2', yc, p['wb2'], 'nn')
        (merged,) = rowwise(t + 'merge', f_merge, [p_gate, b0, b1, b2], [p['gate_b']], [], [D], TM, [BF16])
        x2 = matmul(t + 'out_proj', merged, p['w_out'], 'nn', add=xc)
        sv.update(p_gate=p_gate, b0=b0, b1=b1, b2=b2, merged=merged, x2=x2)
        (h2,) = rowwise(t + 'ffn_norm', f_rms, [x2], [p['ffn_g']], [], [D], TM, [BF16])
        au = matmul(t + 'ffn_in', h2, p['w_gu'], 'nn')
        (act,) = rowwise(t + 'swiglu', f_swiglu, [au], [], [], [D_FF], TM, [BF16])
        xc = matmul(t + 'ffn_out', act, p['w_down'], 'nn', add=x2)
        sv.update(h2=h2, au=au, act=act)
        saved.append(sv)

    loss_part, dx, d_final_g = loss_head(xc, loss_target.reshape(N, D), row(full['final_norm_g']), TM)
    loss = lax.psum(loss_part[0, 0], ("x", "y", "c"))

    G = {n: [None] * DEPTH for n in WEIGHTS if n != 'final_norm_g'}
    for l in reversed(range(DEPTH)):
        p, sv = LW[l], saved[l]
        t = 'l%d_bwd_' % l
        d_act = matmul(t + 'ffn_out_dx', dx, p['w_down'], 'nt')
        G['w_ffn_down'][l] = matmul(t + 'ffn_out_dw', sv['act'], dx, 'tn')
        (d_au,), _ = rowwise_bwd(t + 'swiglu', f_swiglu, [sv['au']], [], [], [[d_act]], TMH, drow_dtypes=[BF16])
        d_h2 = matmul(t + 'ffn_in_dx', d_au, p['w_gu'], 'nt')
        d_wgu = matmul(t + 'ffn_in_dw', sv['h2'], d_au, 'tn')
        G['w_ffn_gate'][l], G['w_ffn_up'][l] = d_wgu[:, :D_FF], d_wgu[:, D_FF:]
        (dx2,), (dg,) = rowwise_bwd(t + 'ffn_norm', f_rms, [sv['x2']], [p['ffn_g']], [], [[d_h2]], TM, extra=[(0, dx)])
        G['ffn_norm_g'][l] = dg.reshape(-1)
        d_merged = matmul(t + 'out_proj_dx', dx2, p['w_out'], 'nt')
        G['w_out'][l] = matmul(t + 'out_proj_dw', sv['merged'], dx2, 'tn')
        (d_pgate, d_b0, d_b1, d_b2), (d_gate_b,) = rowwise_bwd(
            t + 'merge', f_merge, [sv['p_gate'], sv['b0'], sv['b1'], sv['b2']], [p['gate_b']], [], [[d_merged]], TM,
            drow_dtypes=[BF16] * 4)
        G['gate_b'][l] = d_gate_b.reshape(3, D)
        d_ya = matmul(t + 'br0_dx', d_b0, p['wb0'], 'nt')
        d_yb = matmul(t + 'br1_dx', d_b1, p['wb1'], 'nt')
        d_yc = matmul(t + 'br2_dx', d_b2, p['wb2'], 'nt')
        d_wb0 = matmul(t + 'br0_dw', sv['ya'], d_b0, 'tn').reshape(MLA_HEADS, HEAD_PAD, D)[:, QK_NOPE:].reshape(-1, D)
        G['w_branch'][l] = jnp.stack([d_wb0, matmul(t + 'br1_dw', sv['yb'], d_b1, 'tn'),
                                      matmul(t + 'br2_dw', sv['yc'], d_b2, 'tn')])
        (d_y0, _d_y1, d_r1, d_v1, d_kdir1, d_g), (d_ln_g, d_ln_b, d_r_k) = rowwise_bwd(
            t + 'rw_post', f_rw_post, [sv['y0'], sv['y1'], sv['r'], sv['v'], sv['kdir'], sv['g']],
            [p['ln_g'], p['ln_b'], p['r_k']], [ones], [[d_yc]], TMH)
        G['rw_ln_g'][l], G['rw_ln_b'][l] = d_ln_g.reshape(-1), d_ln_b.reshape(-1)
        G['rw_r_k'][l] = d_r_k.reshape(RWKV_HEADS, RWKV_HEAD)
        sc = sv['sc']
        res = scan_bwd(t + 'scan', sc['w'], sc['k'], sc['b'], sc['kk'], sc['r'], sc['v'], sv['sp'], sv['sa'],
                       sv['s_last'], to_i(d_y0))
        s_dw, s_dk, s_db, s_dkk, s_dr = [from_j(res[2 * i], res[2 * i + 1]) for i in range(5)]
        s_dv = from_i(res[10], res[11])
        rw_par = [p['mu'], p['w0'], p['w2'], p['a0'], p['a2'], p['g2'], p['k_k'], p['k_a']]
        d_outs = [[d_r1] + flat(s_dr), [d_v1] + list(s_dv), cat(s_dw), [d_kdir1] + cat(s_dk), flat(s_dkk),
                  cat(s_db), [d_g]]
        (d_z, d_zp, d_zn), d_rw = rowwise_bwd(
            t + 'rw_pre', f_rw_pre, [sv['z'], sv['zp'], sv['zn']], rw_par, [ones], d_outs, TMH)
        (d_prw,) = rowwise(t + 'shift_sum', f_add3, [d_z, shift_next(d_zp), shift_prev(d_zn)], [], [], [RWKV_IN], TM,
                           [BF16])
        G['rw_mu'][l] = d_rw[0].reshape(-1)
        G['rw_w0'][l] = d_rw[1].reshape(2, RWKV_DIM)
        G['rw_w2'][l] = jnp.stack([d_rw[2][:64, :RWKV_DIM], d_rw[2][64:, RWKV_DIM:]])
        G['rw_a0'][l] = d_rw[3].reshape(2, RWKV_DIM)
        G['rw_a2'][l] = jnp.stack([d_rw[4][:64, :RWKV_DIM], d_rw[4][64:, RWKV_DIM:]])
        G['rw_g2'][l] = d_rw[5]
        G['rw_k_k'][l], G['rw_k_a'][l] = d_rw[6].reshape(-1), d_rw[7].reshape(-1)
        (d_psg,), (d_sg_g, d_sg_b, d_sg_w, d_sg_bias) = rowwise_bwd(
            t + 'sg', f_sg, [sv['p_sg']], [p['sg_g'], p['sg_b'], p['sg_w'], p['sg_bias']], [], [[d_yb]], TMH,
            drow_dtypes=[BF16])
        G['sg_ln_g'][l], G['sg_ln_b'][l], G['sg_w'][l] = d_sg_g.reshape(-1), d_sg_b.reshape(-1), d_sg_w
        G['sg_b'][l] = d_sg_bias.reshape(SG_CHUNK, SG_GROUPS, SG_DIM // SG_GROUPS).sum(-1).T
        d_qh, d_kh, d_kvv = attention_bwd(t + 'attn', sv['qh'], sv['kh'], sv['kv'], d_ya, B, S, TQ)
        (d_qq, d_kv, d_pkr), _ = rowwise_bwd(
            t + 'rope', f_rope, [sv['qq'], sv['kv'], sv['p_kr'], ct, st], [], [], [[d_qh], [d_kh]], TM,
            n_row_diff=3, extra=[(1, d_kvv)], drow_dtypes=[BF16] * 3)
        d_cq = matmul(t + 'uq_dx', d_qq, p['w_uq'], 'nt')
        d_wuq = matmul(t + 'uq_dw', sv['cq'], d_qq, 'tn')
        g1, g2_ = _unpermute_full(d_wuq, wuq_src, wuq_sgn, MLA_HEADS * (QK_NOPE + QK_ROPE))
        G['w_uq'][l] = g1 + g2_
        d_ckv = matmul(t + 'ukv_dx', d_kv, p['w_ukv'], 'nt')
        G['w_ukv'][l] = matmul(t + 'ukv_dw', sv['ckv'], d_kv, 'tn')
        (d_pq,), (dg,) = rowwise_bwd(t + 'q_norm', f_rms, [sv['p_q']], [p['q_g']], [], [[d_cq]], TM,
                                     drow_dtypes=[BF16])
        G['q_norm_g'][l] = dg.reshape(-1)
        (d_pckv,), (dg,) = rowwise_bwd(t + 'kv_norm', f_rms, [sv['p_ckv']], [p['kv_g']], [], [[d_ckv]], TM,
                                       drow_dtypes=[BF16])
        G['kv_norm_g'][l] = dg.reshape(-1)
        d_h, d_cols = None, []
        for (sn, a, b), d_seg in zip(IN_SEGMENTS, [d_pq, d_pckv, d_pkr, d_psg, d_prw, d_pgate]):
            d_h = matmul(t + 'in_proj_dx_' + sn, d_seg, p['w_in'][:, a:b], 'nt', add=d_h)
            d_cols.append(matmul(t + 'in_proj_dw_' + sn, sv['h'], d_seg, 'tn'))
        d_win = jnp.concatenate(d_cols, axis=1)
        g1, g2_ = _unpermute_full(d_win, win_src, win_sgn, N_IN)
        kr0 = Q_LORA + KV_LORA
        G['w_in'][l] = g1.at[:, kr0:kr0 + QK_ROPE].add(g2_[:, kr0:kr0 + QK_ROPE])
        (dx,), (dg,) = rowwise_bwd(t + 'attn_norm', f_rms, [sv['x']], [p['attn_g']], [], [[d_h]], TM, extra=[(0, dx2)])
        G['attn_norm_g'][l] = dg.reshape(-1)

    grads = {n: jnp.stack(G[n]) for n in G}
    grads['final_norm_g'] = d_final_g.reshape(-1)
    grad_x = dx.reshape(B, S, D)

    per_shard = []
    for q in range(4):
        sl = []
        for n in SHARDED:
            ax = SHARD_AXIS[n]
            w = W[n].shape[ax]
            sl.append(lax.slice_in_dim(grads[n], q * w, (q + 1) * w, axis=ax))
        per_shard.append(_pack(sl, 2 * 8 * JOIN_CHUNKS))
    R = per_shard[0].shape[0]
    gpack = jnp.stack(per_shard).astype(BF16).reshape(4, 2, R // 2, PACK_C)
    half_sum = sum_slots("sum_sharded", scatter_partials(gpack))
    g_shard = dict(zip(SHARDED, _unpack(sibling_join(half_sum).reshape(R, PACK_C), shard_shapes)))
    rep_shapes = [W[n].shape for n in REPLICATED]
    rpack = _pack([grads[n] for n in REPLICATED], 8)
    g_rep = sum_slots("sum_replicated", gather_all("gather_replicated", rpack))

    outs = {}
    for n in MATMUL_SHARDED:
        shp = W[n].shape
        two = lambda a: a.reshape(-1, shp[-1])
        res = adamw("adamw_" + n, two(W[n]), two(g_shard[n]), two(M1[n]), two(M2[n]))
        outs['grad', n] = g_shard[n]
        for key, a in zip(('delta', 'new_m', 'new_v'), res):
            outs[key, n] = a.reshape(shp)
    small = SMALL_SHARDED + REPLICATED
    small_shapes = [W[n].shape for n in small]
    g_small = [g_shard[n] for n in SMALL_SHARDED] + _unpack(g_rep, rep_shapes)
    res = adamw("adamw_small", _pack([W[n] for n in small], 8), _pack(g_small, 8),
                _pack([M1[n] for n in small], 8), _pack([M2[n] for n in small], 8))
    for n, a in zip(small, g_small):
        outs['grad', n] = a
    for key, buf in zip(('delta', 'new_m', 'new_v'), res):
        for n, a in zip(small, _unpack(buf, small_shapes)):
            outs[key, n] = a
    return (loss, grad_x, *[outs['grad', n] for n in WEIGHTS], *[outs['delta', n] for n in WEIGHTS],
            *[outs['new_m', n] for n in WEIGHTS], *[outs['new_v', n] for n in WEIGHTS])
```

```python
import functools
import math

import numpy as np
import jax
import jax.numpy as jnp
from jax import lax
from jax.experimental import pallas as pl
from jax.experimental.pallas import tpu as pltpu

F32 = jnp.float32
BF16 = jnp.bfloat16

DEPTH = 2
MLA_HEADS = 8
Q_LORA = 384
KV_LORA = 256
QK_NOPE = 64
QK_ROPE = 32
V_HEAD = 64
ROPE_THETA = 10000.0
SG_GROUPS = 8
SG_DIM = 512
SG_CHUNK = 128
RWKV_HEADS = 8
RWKV_HEAD = 64
RWKV_DIM = 512
GN_EPS = 64e-5
NORM_EPS = 1e-6
D_FF = 2816
RWKV_IN = 1920
N_IN = 6688
ADAM_LR, ADAM_B1, ADAM_B2, ADAM_EPS, ADAM_WD, ADAM_STEP = 0.001, 0.9, 0.999, 1e-08, 0.01, 10

LANES = 128
HEAD_PAD = 128
VMEM_LIMIT = 56 * 1024 * 1024
MESH = pl.DeviceIdType.MESH

WEIGHTS = ['attn_norm_g', 'w_in', 'gate_b', 'q_norm_g', 'w_uq', 'kv_norm_g', 'w_ukv', 'sg_ln_g', 'sg_ln_b', 'sg_w',
           'sg_b', 'rw_mu', 'rw_w0', 'rw_w2', 'rw_a0', 'rw_a2', 'rw_g2', 'rw_k_k', 'rw_k_a', 'rw_r_k', 'rw_ln_g',
           'rw_ln_b', 'w_branch', 'w_out', 'ffn_norm_g', 'w_ffn_gate', 'w_ffn_up', 'w_ffn_down', 'final_norm_g']
SHARD_AXIS = {'w_in': 2, 'gate_b': 2, 'w_uq': 2, 'w_ukv': 2, 'rw_w0': 2, 'rw_w2': 3, 'rw_a0': 2, 'rw_a2': 3,
              'rw_g2': 2, 'w_branch': 3, 'w_out': 1, 'w_ffn_gate': 2, 'w_ffn_up': 2, 'w_ffn_down': 1}
SHARDED = [n for n in WEIGHTS if n in SHARD_AXIS]
REPLICATED = [n for n in WEIGHTS if n not in SHARD_AXIS]
SMALL_SHARDED = ['gate_b', 'rw_w0', 'rw_a0']
MATMUL_SHARDED = [n for n in SHARDED if n not in SMALL_SHARDED]


def _params(sem=None):
    return pltpu.CompilerParams(dimension_semantics=sem, vmem_limit_bytes=VMEM_LIMIT)


def _pick(n, cands):
    for c in cands:
        if n % c == 0:
            return c
    return n


def _dot(a, b, dims):
    return lax.dot_general(a.astype(BF16), b.astype(BF16), (dims, ((), ())), preferred_element_type=F32)


def _nn(a, b):
    return _dot(a, b, ((1,), (0,)))


def _nt(a, b):
    return _dot(a, b, ((1,), (1,)))


def _tn(a, b):
    return _dot(a, b, ((0,), (0,)))


@jax.custom_vjp
def mm(a, b):
    return _nn(a, b)


mm.defvjp(lambda a, b: (_nn(a, b), (a, b)), lambda res, g: (_nt(g, res[1]), _tn(res[0], g)))


@jax.custom_vjp
def mm_nt(a, b):
    return _nt(a, b)


mm_nt.defvjp(lambda a, b: (_nt(a, b), (a, b)), lambda res, g: (_nn(g, res[1]), _tn(g, res[0])))


def _seg_raw(x, ones):
    hi = x.astype(BF16)
    lo = (x - hi.astype(F32)).astype(BF16)
    d = (((1,), (0,)), ((), ()))
    return (lax.dot_general(hi, ones, d, preferred_element_type=F32)
            + lax.dot_general(lo, ones, d, preferred_element_type=F32))


@jax.custom_vjp
def segsum(x, ones):
    return _seg_raw(x, ones)


segsum.defvjp(lambda x, ones: (_seg_raw(x, ones), ones),
              lambda ones, g: (_seg_raw(g, ones), jnp.zeros_like(ones)))


def _sigmoid(x):
    return 0.5 * (jnp.tanh(0.5 * x) + 1.0)


def _rms(x, g):
    return x * lax.rsqrt(jnp.mean(x * x, axis=-1, keepdims=True) + NORM_EPS) * g


def matmul(name, a, b, mode, add=None, out_dtype=F32):
    if mode == 'nn':
        (M, K), (_, N) = a.shape, b.shape
    elif mode == 'nt':
        (M, K), (N, _) = a.shape, b.shape
    else:
        (K, M), (_, N) = a.shape, b.shape
    tm = _pick(M, (1408, 1024, 512, 384, 256, 128))
    tn = _pick(N, (1408, 1024, 768, 512, 384, 256, 128))
    tk = _pick(K, (512, 384, 256, 128))
    nk = K // tk
    dims = {'nn': ((1,), (0,)), 'nt': ((1,), (1,)), 'tn': ((0,), (0,))}[mode]
    a_spec = pl.BlockSpec((tk, tm), lambda i, j, k: (k, i)) if mode == 'tn' else pl.BlockSpec((tm, tk), lambda i, j, k: (i, k))
    b_spec = pl.BlockSpec((tn, tk), lambda i, j, k: (j, k)) if mode == 'nt' else pl.BlockSpec((tk, tn), lambda i, j, k: (k, j))
    o_spec = pl.BlockSpec((tm, tn), lambda i, j, k: (i, j))
    has_add = add is not None

    def body(*refs):
        if has_add:
            a_ref, b_ref, add_ref, o_ref, acc = refs
        else:
            a_ref, b_ref, o_ref, acc = refs
        k = pl.program_id(2)

        @pl.when(k == 0)
        def _():
            acc[...] = jnp.zeros_like(acc)

        acc[...] += _dot(a_ref[...], b_ref[...], dims)

        @pl.when(k == nk - 1)
        def _():
            o_ref[...] = (acc[...] + add_ref[...] if has_add else acc[...]).astype(o_ref.dtype)

    ins = [a, b] + ([add] if has_add else [])
    specs = [a_spec, b_spec] + ([o_spec] if has_add else [])
    return pl.pallas_call(
        body, name=name, grid=(M // tm, N // tn, nk), in_specs=specs, out_specs=o_spec,
        out_shape=jax.ShapeDtypeStruct((M, N), out_dtype), scratch_shapes=[pltpu.VMEM((tm, tn), F32)],
        compiler_params=_params(("parallel", "parallel", "arbitrary")))(*ins)


def _full_spec(p):
    nd = p.ndim
    return pl.BlockSpec(p.shape, lambda i, _nd=nd: (0,) * _nd)


def rowwise(name, fn, rows, params, consts, out_widths, tm, out_dtypes=None):
    N = rows[0].shape[0]
    nr, npar, nc = len(rows), len(params), len(consts)

    def body(*refs):
        vals = [r[...] for r in refs[:nr + npar + nc]]
        res = fn(*vals)
        for o, v in zip(refs[nr + npar + nc:], res):
            o[...] = v.astype(o.dtype)

    in_specs = ([pl.BlockSpec((tm, r.shape[1]), lambda i: (i, 0)) for r in rows]
                + [_full_spec(p) for p in list(params) + list(consts)])
    out_specs = [pl.BlockSpec((tm, w), lambda i: (i, 0)) for w in out_widths]
    return pl.pallas_call(
        body, name=name, grid=(N // tm,), in_specs=in_specs, out_specs=out_specs,
        out_shape=[jax.ShapeDtypeStruct((N, w), d) for w, d in zip(out_widths, out_dtypes or [F32] * len(out_widths))],
        compiler_params=_params(("parallel",)))(*rows, *params, *consts)


def rowwise_bwd(name, fn, rows, params, consts, d_outs, tm, n_row_diff=None, extra=(), drow_dtypes=None):
    N = rows[0].shape[0]
    nr, npar, nc = len(rows), len(params), len(consts)
    nd = nr if n_row_diff is None else n_row_diff
    counts = [len(p) for p in d_outs]
    flat_d = [a for parts in d_outs for a in parts]
    nflat, nex = len(flat_d), len(extra)

    def body(*refs):
        pos = 0
        row_v = [r[...] for r in refs[pos:pos + nr]]; pos += nr
        par_v = [r[...] for r in refs[pos:pos + npar]]; pos += npar
        con_v = [r[...] for r in refs[pos:pos + nc]]; pos += nc
        d_refs = refs[pos:pos + nflat]; pos += nflat
        ex_refs = refs[pos:pos + nex]; pos += nex
        drow_refs = refs[pos:pos + nd]; pos += nd
        dpar_refs = refs[pos:pos + npar]

        def f(*diff):
            return fn(*diff[:nd], *row_v[nd:], *diff[nd:], *con_v)

        _, vjp = jax.vjp(f, *row_v[:nd], *par_v)
        cts, q = [], 0
        for c in counts:
            g = d_refs[q][...].astype(F32)
            for t in range(1, c):
                g = g + d_refs[q + t][...].astype(F32)
            cts.append(g)
            q += c
        grads = vjp(tuple(cts))
        drow = list(grads[:nd])
        for (idx, _), r in zip(extra, ex_refs):
            drow[idx] = drow[idx] + r[...].astype(F32)
        for o, v in zip(drow_refs, drow):
            o[...] = v.astype(o.dtype)

        @pl.when(pl.program_id(0) == 0)
        def _():
            for o in dpar_refs:
                o[...] = jnp.zeros_like(o)

        for o, v in zip(dpar_refs, grads[nd:]):
            o[...] += v

    ex_arrs = [a for _, a in extra]
    in_specs = ([pl.BlockSpec((tm, r.shape[1]), lambda i: (i, 0)) for r in rows]
                + [_full_spec(p) for p in list(params) + list(consts)]
                + [pl.BlockSpec((tm, a.shape[1]), lambda i: (i, 0)) for a in flat_d + ex_arrs])
    out_specs = ([pl.BlockSpec((tm, r.shape[1]), lambda i: (i, 0)) for r in rows[:nd]]
                 + [_full_spec(p) for p in params])
    out_shape = ([jax.ShapeDtypeStruct(r.shape, d) for r, d in zip(rows[:nd], drow_dtypes or [F32] * nd)]
                 + [jax.ShapeDtypeStruct(p.shape, F32) for p in params])
    res = pl.pallas_call(
        body, name=name, grid=(N // tm,), in_specs=in_specs, out_specs=out_specs, out_shape=out_shape,
        compiler_params=_params(("arbitrary",)))(*rows, *params, *consts, *flat_d, *ex_arrs)
    return list(res[:nd]), list(res[nd:])


def f_rms(x, g):
    return (_rms(x, g),)


def f_rope(qq, kv, krr, ct, st):
    hw = MLA_HEADS * HEAD_PAD
    c8 = jnp.tile(ct, (1, MLA_HEADS))
    s8 = jnp.tile(st, (1, MLA_HEADS))
    q = qq[:, :hw] * c8 + qq[:, hw:] * s8
    kr = krr[:, :HEAD_PAD] * ct + krr[:, HEAD_PAD:] * st
    lane = lax.broadcasted_iota(jnp.int32, kv.shape, 1) % HEAD_PAD
    k = jnp.where(lane < QK_NOPE, kv, jnp.tile(kr, (1, MLA_HEADS)))
    return q, k


def f_sg(p, ln_g, ln_b, w, bias):
    z = 0.5 * p * (1.0 + jnp.tanh(0.7978845608028654 * (p + 0.044715 * p * p * p)))
    u, v = z[:, :SG_DIM], z[:, SG_DIM:]
    mu = jnp.mean(v, axis=-1, keepdims=True)
    var = jnp.mean(jnp.square(v - mu), axis=-1, keepdims=True)
    v = (v - mu) * lax.rsqrt(var + 1e-5) * ln_g + ln_b
    lane = lax.broadcasted_iota(jnp.int32, (SG_CHUNK, LANES), 1)
    outs = []
    for c in range(p.shape[0] // SG_CHUNK):
        vc = v[c * SG_CHUNK:(c + 1) * SG_CHUNK]
        cols = []
        for m in range(SG_DIM // LANES):
            blk = vc[:, m * LANES:(m + 1) * LANES]
            cols.append(jnp.where(lane < 64, mm(w[2 * m], blk), mm(w[2 * m + 1], blk)))
        outs.append(jnp.concatenate(cols, axis=1) + bias)
    mixed = outs[0] if len(outs) == 1 else jnp.concatenate(outs, axis=0)
    return (u * mixed,)


def f_rw_pre(z, zp, zn, mu, w0, w2, a0, a2, g2, k_k, k_a, ones):
    z = z + mu * (0.5 * (zp + zn) - z)
    C = RWKV_DIM
    r, k, v = z[:, :C], z[:, C:2 * C], z[:, 2 * C:3 * C]
    wl, al, gl = z[:, 3 * C:3 * C + 128], z[:, 3 * C + 128:3 * C + 256], z[:, 3 * C + 256:]
    w = w0 + mm(jnp.tanh(wl), w2)
    decay = jnp.exp(-0.6065306597126334 * _sigmoid(w))
    a = _sigmoid(a0 + mm(al, a2))
    g = mm(_sigmoid(gl), g2)
    kk = k * k_k
    kk = kk / jnp.maximum(jnp.sqrt(segsum(kk * kk, ones)), 1e-12)
    k2 = jnp.concatenate([k, k], axis=1)
    kdir = k2 * (1.0 + (a - 1.0) * jnp.concatenate([k_a, k_a], axis=1))
    bdir = jnp.concatenate([kk, kk], axis=1) * a
    return r, v, decay, kdir, kk, bdir, g


def f_rw_post(y, r, v, kdir, g, ln_g, ln_b, r_k, ones):
    mean = segsum(y, ones) * (1.0 / RWKV_HEAD)
    yc = y - mean
    var = segsum(yc * yc, ones) * (1.0 / RWKV_HEAD)
    y = yc * lax.rsqrt(var + GN_EPS) * ln_g + ln_b
    C = RWKV_DIM
    bonus = segsum(r * kdir[:, :C] * r_k, ones) + segsum(r * kdir[:, C:] * r_k, ones)
    return ((y + bonus * v) * g,)


def f_merge(pg, b0, b1, b2, gate_b):
    D = b0.shape[1]
    gt = _sigmoid(pg + gate_b)
    return (gt[:, :D] * b0 + gt[:, D:2 * D] * b1 + gt[:, 2 * D:] * b2,)


def f_swiglu(au):
    a, u = au[:, :D_FF], au[:, D_FF:]
    return (a * _sigmoid(a) * u,)


def f_add3(a, b, c):
    return (a + b + c,)


def loss_head(x, tgt, g, tm):
    N, D = x.shape

    def body(x_ref, t_ref, g_ref, loss_ref, dx_ref, dg_ref):
        t = t_ref[...]

        def f(xv, gv):
            err = _rms(xv, gv) - t
            return 0.5 * jnp.sum(jnp.mean(err * err, axis=-1, keepdims=True))

        val, (dx, dg) = jax.value_and_grad(f, argnums=(0, 1))(x_ref[...], g_ref[...])
        dx_ref[...] = dx

        @pl.when(pl.program_id(0) == 0)
        def _():
            loss_ref[...] = jnp.zeros_like(loss_ref)
            dg_ref[...] = jnp.zeros_like(dg_ref)

        loss_ref[...] += jnp.full(loss_ref.shape, val, F32)
        dg_ref[...] += dg

    row = pl.BlockSpec((tm, D), lambda i: (i, 0))
    return pl.pallas_call(
        body, name="loss_head", grid=(N // tm,), in_specs=[row, row, _full_spec(g)],
        out_specs=[pl.BlockSpec((1, LANES), lambda i: (0, 0)), row, _full_spec(g)],
        out_shape=[jax.ShapeDtypeStruct((1, LANES), F32), jax.ShapeDtypeStruct((N, D), F32),
                   jax.ShapeDtypeStruct(g.shape, F32)],
        compiler_params=_params(("arbitrary",)))(x, tgt, g)


ATT_SCALE = float((QK_NOPE + QK_ROPE) ** -0.5)


def _attn_block(q, k, kv):
    s = mm_nt(q, k) * ATT_SCALE
    m = lax.stop_gradient(jnp.max(s, axis=-1, keepdims=True))
    e = jnp.exp(s - m)
    p = e / jnp.sum(e, axis=-1, keepdims=True)
    return mm(p, kv)


def attention_fwd(name, q, k, kv, B, S, tq, out_dtype):
    nq = S // tq
    qspec = pl.BlockSpec((tq, HEAD_PAD), lambda b, h, i: (b * nq + i, h))
    kspec = pl.BlockSpec((S, HEAD_PAD), lambda b, h, i: (b, h))

    def body(q_ref, k_ref, kv_ref, o_ref):
        o_ref[...] = _attn_block(q_ref[...], k_ref[...], kv_ref[...]).astype(o_ref.dtype)

    return pl.pallas_call(
        body, name=name, grid=(B, MLA_HEADS, nq), in_specs=[qspec, kspec, kspec], out_specs=qspec,
        out_shape=jax.ShapeDtypeStruct(q.shape, out_dtype),
        compiler_params=_params(("parallel", "parallel", "arbitrary")))(q, k, kv)


def attention_bwd(name, q, k, kv, do, B, S, tq):
    nq = S // tq
    qspec = pl.BlockSpec((tq, HEAD_PAD), lambda b, h, i: (b * nq + i, h))
    kspec = pl.BlockSpec((S, HEAD_PAD), lambda b, h, i: (b, h))

    def body(q_ref, k_ref, kv_ref, do_ref, dq_ref, dk_ref, dkv_ref):
        _, vjp = jax.vjp(_attn_block, q_ref[...], k_ref[...], kv_ref[...])
        dq, dk, dkv = vjp(do_ref[...])
        dq_ref[...] = dq

        @pl.when(pl.program_id(2) == 0)
        def _():
            dk_ref[...] = jnp.zeros_like(dk_ref)
            dkv_ref[...] = jnp.zeros_like(dkv_ref)

        dk_ref[...] += dk
        dkv_ref[...] += dkv

    sh = jax.ShapeDtypeStruct(q.shape, F32)
    return pl.pallas_call(
        body, name=name, grid=(B, MLA_HEADS, nq), in_specs=[qspec, kspec, kspec, qspec],
        out_specs=[qspec, kspec, kspec], out_shape=[sh, sh, sh],
        compiler_params=_params(("parallel", "parallel", "arbitrary")))(q, k, kv, do)


SCAN_TC = 8
SCAN_UNROLL = 16


def _jloop(n, body, init):
    def outer(o, c):
        for u in range(SCAN_UNROLL):
            c = body(o * SCAN_UNROLL + u, c)
        return c

    return lax.fori_loop(0, n // SCAN_UNROLL, outer, init)


def _dir_mask(L, Ip):
    per_dir = L // (2 * (RWKV_HEAD // Ip))
    lane = lax.broadcasted_iota(jnp.int32, (1, L), 1)
    return (lane // per_dir) % 2 == 1


def _merge_dirs(mask, fwd_ref, rev_ref, out_ref):
    for tt in range(SCAN_TC):
        out_ref[tt] = jnp.where(mask, rev_ref[SCAN_TC - 1 - tt], fwd_ref[tt])


def scan_fwd(name, w, k, b, kk, r, v):
    T, J, L = w.shape
    Ip = v.shape[1]
    nT = T // SCAN_TC
    fwd3, rev3 = (lambda g: (g, 0, 0)), (lambda g: (nT - 1 - g, 0, 0))
    jf, jr = pl.BlockSpec((SCAN_TC, J, L), fwd3), pl.BlockSpec((SCAN_TC, J, L), rev3)
    i_f, i_r = pl.BlockSpec((SCAN_TC, Ip, L), fwd3), pl.BlockSpec((SCAN_TC, Ip, L), rev3)
    sspec = pl.BlockSpec((SCAN_TC, J, Ip, L), lambda g: (g, 0, 0, 0))
    last_spec = pl.BlockSpec((J, Ip, L), lambda g: (0, 0, 0))

    def body(wf, wr, kf, kr, bf, br, kkf, kkr, rf, rr, vf, vr, yf_ref, yr_ref, sp_ref, sa_ref, last_ref,
             s_ref, w_ref, k_ref, b_ref, kk_ref, r_ref, v_ref):
        @pl.when(pl.program_id(0) == 0)
        def _():
            s_ref[...] = jnp.zeros_like(s_ref)

        mask = _dir_mask(L, Ip)
        for f_, r_, m_ in ((wf, wr, w_ref), (kf, kr, k_ref), (bf, br, b_ref), (kkf, kkr, kk_ref), (rf, rr, r_ref),
                           (vf, vr, v_ref)):
            _merge_dirs(mask, f_, r_, m_)

        def row(ref, tt, j):
            return jnp.broadcast_to(ref[tt, pl.ds(j, 1), :], (Ip, L))

        def step(tt, carry):
            def p1(j, sa):
                s = s_ref[j]
                sp_ref[tt, j] = s
                return sa + s * row(kk_ref, tt, j)

            sa = _jloop(J, p1, jnp.zeros((Ip, L), F32))
            sa_ref[tt] = sa
            vt = v_ref[tt]

            def p2(j, y):
                s = s_ref[j] * row(w_ref, tt, j) - sa * row(b_ref, tt, j) + vt * row(k_ref, tt, j)
                s_ref[j] = s
                return y + s * row(r_ref, tt, j)

            y = _jloop(J, p2, jnp.zeros((Ip, L), F32))
            yf_ref[tt] = y
            yr_ref[SCAN_TC - 1 - tt] = y
            return carry

        lax.fori_loop(0, SCAN_TC, step, 0)

        @pl.when(pl.program_id(0) == nT - 1)
        def _():
            last_ref[...] = s_ref[...]

    ish = jax.ShapeDtypeStruct((T, Ip, L), F32)
    jscr = pltpu.VMEM((SCAN_TC, J, L), F32)
    return pl.pallas_call(
        body, name=name, grid=(nT,), in_specs=[jf, jr] * 5 + [i_f, i_r],
        out_specs=[i_f, i_r, sspec, i_f, last_spec],
        out_shape=[ish, ish, jax.ShapeDtypeStruct((T, J, Ip, L), F32), ish, jax.ShapeDtypeStruct((J, Ip, L), F32)],
        scratch_shapes=[pltpu.VMEM((J, Ip, L), F32)] + [jscr] * 5 + [pltpu.VMEM((SCAN_TC, Ip, L), F32)],
        compiler_params=_params(("arbitrary",)))(w, w, k, k, b, b, kk, kk, r, r, v, v)


def scan_bwd(name, w, k, b, kk, r, v, sp, sa_all, s_last, dy):
    T, J, L = w.shape
    Ip = v.shape[1]
    nT = T // SCAN_TC
    stp3, mir3 = (lambda g: (nT - 1 - g, 0, 0)), (lambda g: (g, 0, 0))
    jf, jr = pl.BlockSpec((SCAN_TC, J, L), stp3), pl.BlockSpec((SCAN_TC, J, L), mir3)
    i_f, i_r = pl.BlockSpec((SCAN_TC, Ip, L), stp3), pl.BlockSpec((SCAN_TC, Ip, L), mir3)
    sspec = pl.BlockSpec((SCAN_TC, J, Ip, L), lambda g: (nT - 1 - g, 0, 0, 0))
    last_spec = pl.BlockSpec((J, Ip, L), lambda g: (0, 0, 0))

    def body(wf, wr, kf, kr, bf, br, kkf, kkr, rf, rr, vf, vr, dyf, dyr, sp_ref, sa_ref, last_ref,
             dwf, dwr, dkf, dkr, dbf, dbr, dkkf, dkkr, drf, drr, dvf, dvr,
             ds_ref, nxt_ref, w_ref, k_ref, b_ref, kk_ref, r_ref, v_ref, dy_ref):
        @pl.when(pl.program_id(0) == 0)
        def _():
            ds_ref[...] = jnp.zeros_like(ds_ref)
            nxt_ref[...] = last_ref[...]

        mask = _dir_mask(L, Ip)
        for f_, r_, m_ in ((wf, wr, w_ref), (kf, kr, k_ref), (bf, br, b_ref), (kkf, kkr, kk_ref), (rf, rr, r_ref),
                           (vf, vr, v_ref), (dyf, dyr, dy_ref)):
            _merge_dirs(mask, f_, r_, m_)

        def row(ref, tt, j):
            return jnp.broadcast_to(ref[tt, pl.ds(j, 1), :], (Ip, L))

        def rsum(x):
            return jnp.sum(x, axis=0, keepdims=True)

        def make_step(first):
            def step(n, carry):
                tt = SCAN_TC - 1 - n
                dyt, vt, sa = dy_ref[tt], v_ref[tt], sa_ref[tt]

                def p1(j, c):
                    dsa, dv = c
                    ds = ds_ref[j] + dyt * row(r_ref, tt, j)
                    ds_ref[j] = ds
                    return dsa - ds * row(b_ref, tt, j), dv + ds * row(k_ref, tt, j)

                z = jnp.zeros((Ip, L), F32)
                dsa, dv = _jloop(J, p1, (z, z))
                dvf[tt] = dv
                dvr[SCAN_TC - 1 - tt] = dv

                def put(f_ref, r_ref_, j, val):
                    f_ref[tt, pl.ds(j, 1), :] = val
                    r_ref_[SCAN_TC - 1 - tt, pl.ds(j, 1), :] = val

                def p2(j, c):
                    ds = ds_ref[j]
                    s0 = sp_ref[tt, j]
                    s1 = nxt_ref[j] if first else sp_ref[tt + 1, j]
                    put(drf, drr, j, rsum(s1 * dyt))
                    put(dkf, dkr, j, rsum(ds * vt))
                    put(dbf, dbr, j, -rsum(ds * sa))
                    put(dwf, dwr, j, rsum(ds * s0))
                    put(dkkf, dkkr, j, rsum(s0 * dsa))
                    ds_ref[j] = ds * row(w_ref, tt, j) + dsa * row(kk_ref, tt, j)
                    return c

                _jloop(J, p2, 0)
                return carry

            return step

        make_step(True)(0, 0)
        lax.fori_loop(1, SCAN_TC, make_step(False), 0)
        nxt_ref[...] = sp_ref[0]

    jsh = jax.ShapeDtypeStruct((T, J, L), F32)
    ish = jax.ShapeDtypeStruct((T, Ip, L), F32)
    jscr = pltpu.VMEM((SCAN_TC, J, L), F32)
    iscr = pltpu.VMEM((SCAN_TC, Ip, L), F32)
    return pl.pallas_call(
        body, name=name, grid=(nT,), in_specs=[jf, jr] * 5 + [i_f, i_r] * 2 + [sspec, i_f, last_spec],
        out_specs=[jf, jr] * 5 + [i_f, i_r], out_shape=[jsh] * 10 + [ish] * 2,
        scratch_shapes=[pltpu.VMEM((J, Ip, L), F32)] * 2 + [jscr] * 5 + [iscr] * 2,
        compiler_params=_params(("arbitrary",)))(w, w, k, k, b, b, kk, kk, r, r, v, v, dy, dy, sp, sa_all, s_last)


LAYOUT_TT = 32


def _split3(x):
    hi = x.astype(BF16)
    r1 = x - hi.astype(F32)
    mid = r1.astype(BF16)
    return hi, mid, (r1 - mid.astype(F32)).astype(BF16)


def _eye(n):
    r = lax.broadcasted_iota(jnp.int32, (n, n), 0)
    c = lax.broadcasted_iota(jnp.int32, (n, n), 1)
    return jnp.where(r == c, 1.0, 0.0).astype(BF16)


def _rows_to_lanes(z, eye):
    dot = lambda p: lax.dot_general(eye, p, (((1,), (1,)), ((), ())), preferred_element_type=F32)
    hi, mid, lo = _split3(z)
    return (dot(hi) + dot(mid)) + dot(lo)


def _lanes_to_rows(m, eye):
    dot = lambda p: lax.dot_general(p, eye, (((0,), (0,)), ((), ())), preferred_element_type=F32)
    hi, mid, lo = _split3(m)
    return (dot(hi) + dot(mid)) + dot(lo)


def _lane_group(L, n):
    return lax.broadcasted_iota(jnp.int32, (1, L), 1) // (L // n)


def to_scan(name, x, B, S, nd, Ip):
    isplit = LANES // (2 * B * RWKV_HEADS)
    tt_n = _pick(S, (LAYOUT_TT, 16, 8))
    x5 = x.reshape(B, S, nd, RWKV_HEADS, RWKV_HEAD)
    rows_out = RWKV_HEAD if Ip is None else Ip

    def body(x_ref, o_ref):
        eye = _eye(RWKV_HEAD)
        group = _lane_group(LANES, isplit)

        def step(tt, carry):
            z = jnp.concatenate([x_ref[b, tt, min(d, nd - 1)] for _ in range(isplit) for d in range(2)
                                 for b in range(B)], axis=0)
            m = _rows_to_lanes(z, eye)
            if Ip is not None:
                m = sum(jnp.where(group == i2, m[i2 * Ip:(i2 + 1) * Ip], 0.0) for i2 in range(isplit))
            o_ref[tt] = m
            return carry

        lax.fori_loop(0, tt_n, step, 0)

    return pl.pallas_call(
        body, name=name, grid=(S // tt_n,),
        in_specs=[pl.BlockSpec((B, tt_n, nd, RWKV_HEADS, RWKV_HEAD), lambda g: (0, g, 0, 0, 0))],
        out_specs=pl.BlockSpec((tt_n, rows_out, LANES), lambda g: (g, 0, 0)),
        out_shape=jax.ShapeDtypeStruct((S, rows_out, LANES), F32), compiler_params=_params(("parallel",)))(x5)


def from_scan(name, f, r, B, S, nd, i_indexed):
    rows_in = f.shape[1]
    isplit = LANES // (2 * B * RWKV_HEADS)
    Ip = rows_in if i_indexed else RWKV_HEAD // isplit
    tt_n = _pick(S, (LAYOUT_TT, 16, 8))
    per_i2 = LANES // isplit

    def body(f_ref, r_ref, o_ref):
        eye = _eye(RWKV_HEAD)
        mask = _dir_mask(LANES, Ip)
        group = _lane_group(LANES, isplit)

        def step(tt, carry):
            m = jnp.where(mask, r_ref[tt], f_ref[tt])
            if i_indexed:
                m = jnp.concatenate([jnp.where(group == i2, m, 0.0) for i2 in range(isplit)], axis=0)
            z = _lanes_to_rows(m, eye)
            zf = sum(z[i2 * per_i2:(i2 + 1) * per_i2] for i2 in range(isplit))
            for b in range(B):
                d0 = zf[b * RWKV_HEADS:(b + 1) * RWKV_HEADS]
                d1 = zf[(B + b) * RWKV_HEADS:(B + b + 1) * RWKV_HEADS]
                if nd == 1:
                    o_ref[b, tt, 0] = d0 + d1
                else:
                    o_ref[b, tt, 0] = d0
                    o_ref[b, tt, 1] = d1
            return carry

        lax.fori_loop(0, tt_n, step, 0)

    spec = pl.BlockSpec((tt_n, rows_in, LANES), lambda g: (g, 0, 0))
    out = pl.pallas_call(
        body, name=name, grid=(S // tt_n,), in_specs=[spec, spec],
        out_specs=pl.BlockSpec((B, tt_n, nd, RWKV_HEADS, RWKV_HEAD), lambda g: (0, g, 0, 0, 0)),
        out_shape=jax.ShapeDtypeStruct((B, S, nd, RWKV_HEADS, RWKV_HEAD), F32),
        compiler_params=_params(("parallel",)))(f, r)
    return out.reshape(B * S, nd * RWKV_DIM)


def adamw(name, w, g, m, v):
    R, C = w.shape
    tr = _pick(R, (256, 128, 64, 32, 16, 8))
    c1 = 1.0 - ADAM_B1 ** ADAM_STEP
    c2 = 1.0 - ADAM_B2 ** ADAM_STEP

    def body(w_ref, g_ref, m_ref, v_ref, d_ref, nm_ref, nv_ref):
        gv = g_ref[...]
        nm = ADAM_B1 * m_ref[...] + (1.0 - ADAM_B1) * gv
        nv = ADAM_B2 * v_ref[...] + (1.0 - ADAM_B2) * jnp.square(gv)
        d_ref[...] = -ADAM_LR * ((nm / c1) / (jnp.sqrt(nv / c2) + ADAM_EPS) + ADAM_WD * w_ref[...])
        nm_ref[...] = nm
        nv_ref[...] = nv

    spec = pl.BlockSpec((tr, C), lambda i: (i, 0))
    sh = jax.ShapeDtypeStruct((R, C), F32)
    return pl.pallas_call(body, name=name, grid=(R // tr,), in_specs=[spec] * 4, out_specs=[spec] * 3,
                          out_shape=[sh] * 3, compiler_params=_params(("parallel",)))(w, g, m, v)


def sum_slots(name, x):
    n, R, C = x.shape
    tr = _pick(R, (256, 128, 64, 32, 16, 8))

    def body(x_ref, o_ref):
        acc = x_ref[0].astype(F32)
        for s in range(1, n):
            acc = acc + x_ref[s].astype(F32)
        o_ref[...] = acc

    return pl.pallas_call(
        body, name=name, grid=(R // tr,), in_specs=[pl.BlockSpec((n, tr, C), lambda i: (0, i, 0))],
        out_specs=pl.BlockSpec((tr, C), lambda i: (i, 0)), out_shape=jax.ShapeDtypeStruct((R, C), F32),
        compiler_params=_params(("parallel",)))(x)


ANY = pl.BlockSpec(memory_space=pl.ANY)


def _xyc():
    return lax.axis_index("x"), lax.axis_index("y"), lax.axis_index("c")


def gather_shards(shard):
    _, R, C = shard.shape

    def body(x_ref, out_ref, send_sems, recv_sems, local_sem):
        x, y, c = _xyc()
        me, sibling = (x, y, c), (x, y, 1 - c)
        chips = [(1 - x, y), (x, 1 - y), (1 - x, 1 - y)]

        def cp(k, cx, cy, half, to, src=None):
            dst = out_ref.at[2 * cx + cy, half]
            return pltpu.make_async_remote_copy(
                src_ref=dst if src is None else src, dst_ref=dst, send_sem=send_sems.at[k],
                recv_sem=recv_sems.at[k], device_id=to, device_id_type=MESH)

        mine = pltpu.make_async_copy(x_ref, out_ref.at[2 * x + y], local_sem)
        mine.start()
        first = [cp(j, x, y, c, (*chip, c), src=x_ref.at[c]) for j, chip in enumerate(chips)]
        for f in first:
            f.start()
        passed = [cp(3 + j, *chip, c, sibling) for j, chip in enumerate(chips)]
        for j, chip in enumerate(chips):
            cp(j, *chip, c, me).wait_recv()
            passed[j].start()
        for j, chip in enumerate(chips):
            cp(3 + j, *chip, 1 - c, me).wait_recv()
        for f in first + passed:
            f.wait_send()
        mine.wait()

    return pl.pallas_call(
        body, name="gather_shards", in_specs=[ANY], out_specs=ANY,
        out_shape=jax.ShapeDtypeStruct((4, 2, R, C), shard.dtype),
        scratch_shapes=[pltpu.SemaphoreType.DMA((6,)), pltpu.SemaphoreType.DMA((6,)), pltpu.SemaphoreType.DMA])(shard)


FLIPS = [(0, 0, 1), (0, 1, 0), (0, 1, 1), (1, 0, 0), (1, 0, 1), (1, 1, 0), (1, 1, 1)]


def scatter_partials(g):
    _, _, R, C = g.shape

    def body(g_ref, out_ref, send_sems, recv_sems, local_sem):
        x, y, c = _xyc()
        me_idx = 4 * x + 2 * y + c
        mine = pltpu.make_async_copy(g_ref.at[2 * x + y, c], out_ref.at[me_idx], local_sem)
        mine.start()
        sends = []
        for k, (fx, fy, fc) in enumerate(FLIPS):
            px, py, pc = (x + fx) % 2, (y + fy) % 2, (c + fc) % 2
            s = pltpu.make_async_remote_copy(
                src_ref=g_ref.at[2 * px + py, pc], dst_ref=out_ref.at[me_idx], send_sem=send_sems.at[k],
                recv_sem=recv_sems.at[k], device_id=(px, py, pc), device_id_type=MESH)
            s.start()
            sends.append(s)
        for k, (fx, fy, fc) in enumerate(FLIPS):
            px, py, pc = (x + fx) % 2, (y + fy) % 2, (c + fc) % 2
            slot = out_ref.at[4 * px + 2 * py + pc]
            pltpu.make_async_remote_copy(
                src_ref=slot, dst_ref=slot, send_sem=send_sems.at[k], recv_sem=recv_sems.at[k],
                device_id=(px, py, pc), device_id_type=MESH).wait_recv()
        for s in sends:
            s.wait_send()
        mine.wait()

    return pl.pallas_call(
        body, name="scatter_partials", in_specs=[ANY], out_specs=ANY,
        out_shape=jax.ShapeDtypeStruct((8, R, C), g.dtype),
        scratch_shapes=[pltpu.SemaphoreType.DMA((7,)), pltpu.SemaphoreType.DMA((7,)), pltpu.SemaphoreType.DMA])(g)


JOIN_CHUNKS = 8


def sibling_join(half):
    R, C = half.shape
    rows = R // JOIN_CHUNKS

    def body(h_ref, out_ref, send_sems, recv_sems, local_sem):
        x, y, c = _xyc()
        mine = pltpu.make_async_copy(h_ref, out_ref.at[c], local_sem)
        mine.start()
        sends = []
        for k in range(JOIN_CHUNKS):
            s = pltpu.make_async_remote_copy(
                src_ref=h_ref.at[pl.ds(k * rows, rows)], dst_ref=out_ref.at[c, pl.ds(k * rows, rows)],
                send_sem=send_sems.at[k], recv_sem=recv_sems.at[k], device_id=(x, y, 1 - c), device_id_type=MESH)
            s.start()
            sends.append(s)
        for k in range(JOIN_CHUNKS):
            theirs = out_ref.at[1 - c, pl.ds(k * rows, rows)]
            pltpu.make_async_remote_copy(src_ref=theirs, dst_ref=theirs, send_sem=send_sems.at[k],
                                         recv_sem=recv_sems.at[k], device_id=(x, y, 1 - c),
                                         device_id_type=MESH).wait_recv()
        for s in sends:
            s.wait_send()
        mine.wait()

    return pl.pallas_call(
        body, name="sibling_join", in_specs=[ANY], out_specs=ANY,
        out_shape=jax.ShapeDtypeStruct((2, R, C), half.dtype),
        scratch_shapes=[pltpu.SemaphoreType.DMA((JOIN_CHUNKS,)), pltpu.SemaphoreType.DMA((JOIN_CHUNKS,)),
                        pltpu.SemaphoreType.DMA])(half)


def gather_all(name, block):
    R, C = block.shape

    def body(x_ref, out_ref, send_sems, recv_sems, local_sem):
        x, y, c = _xyc()
        mine = pltpu.make_async_copy(x_ref, out_ref.at[4 * x + 2 * y + c], local_sem)
        mine.start()
        sends = []
        for k, (fx, fy, fc) in enumerate(FLIPS):
            px, py, pc = (x + fx) % 2, (y + fy) % 2, (c + fc) % 2
            s = pltpu.make_async_remote_copy(
                src_ref=x_ref, dst_ref=out_ref.at[4 * x + 2 * y + c], send_sem=send_sems.at[k],
                recv_sem=recv_sems.at[k], device_id=(px, py, pc), device_id_type=MESH)
            s.start()
            sends.append(s)
        for k, (fx, fy, fc) in enumerate(FLIPS):
            px, py, pc = (x + fx) % 2, (y + fy) % 2, (c + fc) % 2
            slot = out_ref.at[4 * px + 2 * py + pc]
            pltpu.make_async_remote_copy(
                src_ref=slot, dst_ref=slot, send_sem=send_sems.at[k], recv_sem=recv_sems.at[k],
                device_id=(px, py, pc), device_id_type=MESH).wait_recv()
        for s in sends:
            s.wait_send()
        mine.wait()

    return pl.pallas_call(
        body, name=name, in_specs=[ANY], out_specs=ANY,
        out_shape=jax.ShapeDtypeStruct((8, R, C), block.dtype),
        scratch_shapes=[pltpu.SemaphoreType.DMA((7,)), pltpu.SemaphoreType.DMA((7,)), pltpu.SemaphoreType.DMA])(block)


PACK_C = 1024


def _pack(arrs, row_mult):
    flat = jnp.concatenate([a.reshape(-1) for a in arrs])
    n = flat.shape[0]
    rows = -(-n // PACK_C)
    rows = -(-rows // row_mult) * row_mult
    return jnp.pad(flat, (0, rows * PACK_C - n)).reshape(rows, PACK_C)


def _unpack(buf, shapes):
    flat = buf.reshape(-1)
    out, off = [], 0
    for s in shapes:
        n = int(np.prod(s))
        out.append(flat[off:off + n].reshape(s))
        off += n
    return out


OFF_Q, OFF_CKV, OFF_KR, OFF_SG, OFF_RW, OFF_GATE, N_IN_PAD = 0, 384, 640, 896, 1920, 3840, 6912
IN_SEGMENTS = [('q', OFF_Q, OFF_CKV), ('ckv', OFF_CKV, OFF_KR), ('kr', OFF_KR, OFF_SG), ('sg', OFF_SG, OFF_RW),
               ('rw', OFF_RW, OFF_GATE), ('gate', OFF_GATE, N_IN_PAD)]
ROPE_LANE = QK_NOPE
HALF = QK_ROPE // 2


def _win_layout():
    src = np.full((N_IN_PAD,), -1, np.int64)
    sgn = np.ones((N_IN_PAD,), np.float32)
    src[0:640] = np.arange(0, 640)
    kr0 = Q_LORA + KV_LORA
    src[OFF_KR + ROPE_LANE:OFF_KR + ROPE_LANE + QK_ROPE] = kr0 + np.arange(QK_ROPE)
    sw = OFF_KR + HEAD_PAD + ROPE_LANE
    src[sw:sw + HALF] = kr0 + HALF + np.arange(HALF)
    sgn[sw:sw + HALF] = -1.0
    src[sw + HALF:sw + QK_ROPE] = kr0 + np.arange(HALF)
    src[OFF_SG:N_IN_PAD] = 672 + np.arange(N_IN_PAD - OFF_SG)
    return src, sgn


def _wuq_layout():
    hw = MLA_HEADS * HEAD_PAD
    src = np.full((2 * hw,), -1, np.int64)
    sgn = np.ones((2 * hw,), np.float32)
    per = QK_NOPE + QK_ROPE
    for h in range(MLA_HEADS):
        src[h * HEAD_PAD:h * HEAD_PAD + per] = h * per + np.arange(per)
        sw = hw + h * HEAD_PAD + ROPE_LANE
        src[sw:sw + HALF] = h * per + QK_NOPE + HALF + np.arange(HALF)
        sgn[sw:sw + HALF] = -1.0
        src[sw + HALF:sw + QK_ROPE] = h * per + QK_NOPE + np.arange(HALF)
    return src, sgn


def _permute_cols(w, src, sgn):
    cols = jnp.take(w, jnp.asarray(np.maximum(src, 0)), axis=1)
    return cols * jnp.asarray(np.where(src >= 0, sgn, 0.0).astype(np.float32)).astype(w.dtype)


def _unpermute_full(dw, src, sgn, n_cols):
    first = np.full((n_cols,), -1, np.int64)
    second = np.full((n_cols,), -1, np.int64)
    for pos, s in enumerate(src):
        if s < 0:
            continue
        if first[s] < 0:
            first[s] = pos
        else:
            second[s] = pos
    out = jnp.take(dw, jnp.asarray(first), axis=1) * jnp.asarray(sgn[first].astype(np.float32))
    m2 = (second >= 0)
    two = jnp.take(dw, jnp.asarray(np.maximum(second, 0)), axis=1) * jnp.asarray(
        np.where(m2, sgn[np.maximum(second, 0)], 0.0).astype(np.float32))
    return out, two


def _blockdiag(w):
    z = jnp.zeros_like(w[0])
    return jnp.concatenate([jnp.concatenate([w[0], z], axis=1), jnp.concatenate([z, w[1]], axis=1)], axis=0)


def _rope_tables(pos):
    inv = 1.0 / (ROPE_THETA ** (jnp.arange(0, QK_ROPE, 2, dtype=F32) / QK_ROPE))
    ang = pos.astype(F32)[:, None] * inv[None, :]
    cos, sin = jnp.cos(ang), jnp.sin(ang)
    pad = lambda t, fill: jnp.concatenate(
        [jnp.full((t.shape[0], ROPE_LANE), fill, F32), t, t, jnp.full((t.shape[0], HEAD_PAD - ROPE_LANE - QK_ROPE), fill, F32)], axis=1)
    return pad(cos, 1.0), pad(sin, 0.0)


def kernel(x, positions, attn_norm_g, w_in, gate_b, q_norm_g, w_uq, kv_norm_g, w_ukv, sg_ln_g, sg_ln_b, sg_w, sg_b, rw_mu, rw_w0, rw_w2, rw_a0, rw_a2, rw_g2, rw_k_k, rw_k_a, rw_r_k, rw_ln_g, rw_ln_b, w_branch, w_out, ffn_norm_g, w_ffn_gate, w_ffn_up, w_ffn_down, final_norm_g, loss_target, m_attn_norm_g, m_w_in, m_gate_b, m_q_norm_g, m_w_uq, m_kv_norm_g, m_w_ukv, m_sg_ln_g, m_sg_ln_b, m_sg_w, m_sg_b, m_rw_mu, m_rw_w0, m_rw_w2, m_rw_a0, m_rw_a2, m_rw_g2, m_rw_k_k, m_rw_k_a, m_rw_r_k, m_rw_ln_g, m_rw_ln_b, m_w_branch, m_w_out, m_ffn_norm_g, m_w_ffn_gate, m_w_ffn_up, m_w_ffn_down, m_final_norm_g, v_attn_norm_g, v_w_in, v_gate_b, v_q_norm_g, v_w_uq, v_kv_norm_g, v_w_ukv, v_sg_ln_g, v_sg_ln_b, v_sg_w, v_sg_b, v_rw_mu, v_rw_w0, v_rw_w2, v_rw_a0, v_rw_a2, v_rw_g2, v_rw_k_k, v_rw_k_a, v_rw_r_k, v_rw_ln_g, v_rw_ln_b, v_w_branch, v_w_out, v_ffn_norm_g, v_w_ffn_gate, v_w_ffn_up, v_w_ffn_down, v_final_norm_g):
    args = locals()
    W = {n: args[n] for n in WEIGHTS}
    M1 = {n: args['m_' + n] for n in WEIGHTS}
    M2 = {n: args['v_' + n] for n in WEIGHTS}
    B, S, D = x.shape
    N = B * S
    TM = _pick(N, (256, 128))
    TMH = 128
    TQ = _pick(S, (256, 128))

    shard_shapes = [W[n].shape for n in SHARDED]
    full = {}
    mm_pack = _pack([W[n].astype(BF16) for n in MATMUL_SHARDED], 32)
    Rm = mm_pack.shape[0]
    gathered = gather_shards(mm_pack.reshape(2, Rm // 2, PACK_C)).reshape(4, Rm, PACK_C)
    pieces = [_unpack(gathered[q], [W[n].shape for n in MATMUL_SHARDED]) for q in range(4)]
    for i, n in enumerate(MATMUL_SHARDED):
        full[n] = jnp.concatenate([pieces[q][i] for q in range(4)], axis=SHARD_AXIS[n])
    small = gather_all("gather_small", _pack([W[n] for n in SMALL_SHARDED], 8))
    pieces = [_unpack(small[2 * q], [W[n].shape for n in SMALL_SHARDED]) for q in range(4)]
    for i, n in enumerate(SMALL_SHARDED):
        full[n] = jnp.concatenate([pieces[q][i] for q in range(4)], axis=SHARD_AXIS[n])
    for n in ('rw_w2', 'rw_a2', 'rw_g2'):
        full[n] = full[n].astype(F32)
    for n in REPLICATED:
        full[n] = W[n]

    win_src, win_sgn = _win_layout()
    wuq_src, wuq_sgn = _wuq_layout()
    ones = jnp.asarray(np.kron(np.eye(RWKV_HEADS), np.ones((RWKV_HEAD, RWKV_HEAD))), BF16)
    ct, st = _rope_tables(positions.reshape(N))
    row = lambda v: v.reshape(1, -1)

    H, HD = RWKV_HEADS, RWKV_HEAD
    inst = 2 * B * H
    isplit = LANES // inst
    Ip = HD // isplit
    to_j = lambda nm, c: to_scan(nm, c, B, S, c.shape[1] // RWKV_DIM, None)
    to_i = lambda nm, c: to_scan(nm, c, B, S, 1, Ip)

    def shift_prev(z):
        z = z.reshape(B, S, -1)
        return jnp.pad(z[:, :-1], ((0, 0), (1, 0), (0, 0))).reshape(N, -1)

    def shift_next(z):
        z = z.reshape(B, S, -1)
        return jnp.pad(z[:, 1:], ((0, 0), (0, 1), (0, 0))).reshape(N, -1)

    LW = []
    for l in range(DEPTH):
        wb = full['w_branch'][l]
        wb0 = jnp.zeros((MLA_HEADS, HEAD_PAD, D), F32).at[:, QK_NOPE:].set(wb[0].reshape(MLA_HEADS, V_HEAD, D))
        LW.append(dict(
            attn_g=row(full['attn_norm_g'][l]),
            w_in=_permute_cols(full['w_in'][l], win_src, win_sgn),
            gate_b=row(full['gate_b'][l]),
            q_g=row(full['q_norm_g'][l]),
            w_uq=_permute_cols(full['w_uq'][l], wuq_src, wuq_sgn),
            kv_g=row(full['kv_norm_g'][l]),
            w_ukv=full['w_ukv'][l],
            sg_g=row(full['sg_ln_g'][l]), sg_b=row(full['sg_ln_b'][l]), sg_w=full['sg_w'][l],
            sg_bias=jnp.repeat(full['sg_b'][l].T, SG_DIM // SG_GROUPS, axis=1),
            mu=row(full['rw_mu'][l]), w0=row(full['rw_w0'][l]), w2=_blockdiag(full['rw_w2'][l]),
            a0=row(full['rw_a0'][l]), a2=_blockdiag(full['rw_a2'][l]), g2=full['rw_g2'][l],
            k_k=row(full['rw_k_k'][l]), k_a=row(full['rw_k_a'][l]), r_k=row(full['rw_r_k'][l]),
            ln_g=row(full['rw_ln_g'][l]), ln_b=row(full['rw_ln_b'][l]),
            wb0=wb0.reshape(MLA_HEADS * HEAD_PAD, D), wb1=wb[1], wb2=wb[2],
            w_out=full['w_out'][l], ffn_g=row(full['ffn_norm_g'][l]),
            w_gu=jnp.concatenate([full['w_ffn_gate'][l], full['w_ffn_up'][l]], axis=1),
            w_down=full['w_ffn_down'][l]))

    saved = []
    xc = x.reshape(N, D)
    for l in range(DEPTH):
        p = LW[l]
        t = 'l%d_' % l
        sv = dict(x=xc)
        (h,) = rowwise(t + 'attn_norm', f_rms, [xc], [p['attn_g']], [], [D], TM, [BF16])
        p_q, p_ckv, p_kr, p_sg, z, p_gate = [
            matmul(t + 'in_proj_' + sn, h, p['w_in'][:, a:b], 'nn') for sn, a, b in IN_SEGMENTS]
        sv['h'] = h
        (cq,) = rowwise(t + 'q_norm', f_rms, [p_q], [p['q_g']], [], [Q_LORA], TM, [BF16])
        (ckv,) = rowwise(t + 'kv_norm', f_rms, [p_ckv], [p['kv_g']], [], [KV_LORA], TM, [BF16])
        qq = matmul(t + 'uq', cq, p['w_uq'], 'nn')
        kv = matmul(t + 'ukv', ckv, p['w_ukv'], 'nn')
        qh, kh = rowwise(t + 'rope', f_rope, [qq, kv, p_kr, ct, st], [], [], [MLA_HEADS * HEAD_PAD] * 2, TM)
        ya = attention_fwd(t + 'attn', qh, kh, kv, B, S, TQ, BF16)
        sv.update(p_q=p_q, p_ckv=p_ckv, p_kr=p_kr, cq=cq, ckv=ckv, qq=qq, kv=kv, qh=qh, kh=kh, ya=ya)
        (yb,) = rowwise(t + 'sg', f_sg, [p_sg], [p['sg_g'], p['sg_b'], p['sg_w'], p['sg_bias']], [], [SG_DIM], TM,
                        [BF16])
        sv.update(p_sg=p_sg, yb=yb)
        zp, zn = shift_prev(z), shift_next(z)
        rw_par = [p['mu'], p['w0'], p['w2'], p['a0'], p['a2'], p['g2'], p['k_k'], p['k_a']]
        r_, v_, decay, kdir, kk, bdir, g_ = rowwise(
            t + 'rw_pre', f_rw_pre, [z, zp, zn], rw_par, [ones],
            [RWKV_DIM, RWKV_DIM, 2 * RWKV_DIM, 2 * RWKV_DIM, RWKV_DIM, 2 * RWKV_DIM, RWKV_DIM], TM)
        sc = dict(w=to_j(t + 'lay_w', decay), k=to_j(t + 'lay_k', kdir), b=to_j(t + 'lay_b', bdir),
                  kk=to_j(t + 'lay_kk', kk), r=to_j(t + 'lay_r', r_), v=to_i(t + 'lay_v', v_))
        y_f, y_r, sp, sa_all, s_last = scan_fwd(t + 'scan', sc['w'], sc['k'], sc['b'], sc['kk'], sc['r'], sc['v'])
        ysum = from_scan(t + 'lay_y', y_f, y_r, B, S, 1, True)
        (yc,) = rowwise(t + 'rw_post', f_rw_post, [ysum, r_, v_, kdir, g_], [p['ln_g'], p['ln_b'], p['r_k']],
                        [ones], [RWKV_DIM], TM, [BF16])
        sv.update(z=z, zp=zp, zn=zn, r=r_, v=v_, kdir=kdir, g=g_, sc=sc, sp=sp, sa=sa_all, s_last=s_last,
                  ysum=ysum, yc=yc)
        b0 = matmul(t + 'br0', ya, p['wb0'], 'nn')
        b1 = matmul(t + 'br1', yb, p['wb1'], 'nn')
        b2 = matmul(t + 'br2', yc, p['wb2'], 'nn')
        (merged,) = rowwise(t + 'merge', f_merge, [p_gate, b0, b1, b2], [p['gate_b']], [], [D], TM, [BF16])
        x2 = matmul(t + 'out_proj', merged, p['w_out'], 'nn', add=xc)
        sv.update(p_gate=p_gate, b0=b0, b1=b1, b2=b2, merged=merged, x2=x2)
        (h2,) = rowwise(t + 'ffn_norm', f_rms, [x2], [p['ffn_g']], [], [D], TM, [BF16])
        au = matmul(t + 'ffn_in', h2, p['w_gu'], 'nn')
        (act,) = rowwise(t + 'swiglu', f_swiglu, [au], [], [], [D_FF], TM, [BF16])
        xc = matmul(t + 'ffn_out', act, p['w_down'], 'nn', add=x2)
        sv.update(h2=h2, au=au, act=act)
        saved.append(sv)

    loss_part, dx, d_final_g = loss_head(xc, loss_target.reshape(N, D), row(full['final_norm_g']), TM)
    loss = lax.psum(loss_part[0, 0], ("x", "y", "c"))

    G = {n: [None] * DEPTH for n in WEIGHTS if n != 'final_norm_g'}
    for l in reversed(range(DEPTH)):
        p, sv = LW[l], saved[l]
        t = 'l%d_bwd_' % l
        d_act = matmul(t + 'ffn_out_dx', dx, p['w_down'], 'nt')
        G['w_ffn_down'][l] = matmul(t + 'ffn_out_dw', sv['act'], dx, 'tn')
        (d_au,), _ = rowwise_bwd(t + 'swiglu', f_swiglu, [sv['au']], [], [], [[d_act]], TMH, drow_dtypes=[BF16])
        d_h2 = matmul(t + 'ffn_in_dx', d_au, p['w_gu'], 'nt')
        d_wgu = matmul(t + 'ffn_in_dw', sv['h2'], d_au, 'tn')
        G['w_ffn_gate'][l], G['w_ffn_up'][l] = d_wgu[:, :D_FF], d_wgu[:, D_FF:]
        (dx2,), (dg,) = rowwise_bwd(t + 'ffn_norm', f_rms, [sv['x2']], [p['ffn_g']], [], [[d_h2]], TM, extra=[(0, dx)])
        G['ffn_norm_g'][l] = dg.reshape(-1)
        d_merged = matmul(t + 'out_proj_dx', dx2, p['w_out'], 'nt')
        G['w_out'][l] = matmul(t + 'out_proj_dw', sv['merged'], dx2, 'tn')
        (d_pgate, d_b0, d_b1, d_b2), (d_gate_b,) = rowwise_bwd(
            t + 'merge', f_merge, [sv['p_gate'], sv['b0'], sv['b1'], sv['b2']], [p['gate_b']], [], [[d_merged]], TM,
            drow_dtypes=[BF16] * 4)
        G['gate_b'][l] = d_gate_b.reshape(3, D)
        d_ya = matmul(t + 'br0_dx', d_b0, p['wb0'], 'nt')
        d_yb = matmul(t + 'br1_dx', d_b1, p['wb1'], 'nt')
        d_yc = matmul(t + 'br2_dx', d_b2, p['wb2'], 'nt')
        d_wb0 = matmul(t + 'br0_dw', sv['ya'], d_b0, 'tn').reshape(MLA_HEADS, HEAD_PAD, D)[:, QK_NOPE:].reshape(-1, D)
        G['w_branch'][l] = jnp.stack([d_wb0, matmul(t + 'br1_dw', sv['yb'], d_b1, 'tn'),
                                      matmul(t + 'br2_dw', sv['yc'], d_b2, 'tn')])
        (d_y, d_r1, d_v1, d_kdir1, d_g), (d_ln_g, d_ln_b, d_r_k) = rowwise_bwd(
            t + 'rw_post', f_rw_post, [sv['ysum'], sv['r'], sv['v'], sv['kdir'], sv['g']],
            [p['ln_g'], p['ln_b'], p['r_k']], [ones], [[d_yc]], TMH)
        G['rw_ln_g'][l], G['rw_ln_b'][l] = d_ln_g.reshape(-1), d_ln_b.reshape(-1)
        G['rw_r_k'][l] = d_r_k.reshape(RWKV_HEADS, RWKV_HEAD)
        sc = sv['sc']
        res = scan_bwd(t + 'scan', sc['w'], sc['k'], sc['b'], sc['kk'], sc['r'], sc['v'], sv['sp'], sv['sa'],
                       sv['s_last'], to_i(t + 'lay_dy', d_y))
        s_dw, s_dk, s_db, s_dkk, s_dr, s_dv = [
            from_scan(t + 'lay_' + nm, res[2 * i], res[2 * i + 1], B, S, nd, nm == 'dv')
            for i, (nm, nd) in enumerate((('dw', 2), ('dk', 2), ('db', 2), ('dkk', 1), ('dr', 1), ('dv', 1)))]
        rw_par = [p['mu'], p['w0'], p['w2'], p['a0'], p['a2'], p['g2'], p['k_k'], p['k_a']]
        d_outs = [[d_r1, s_dr], [d_v1, s_dv], [s_dw], [d_kdir1, s_dk], [s_dkk], [s_db], [d_g]]
        (d_z, d_zp, d_zn), d_rw = rowwise_bwd(
            t + 'rw_pre', f_rw_pre, [sv['z'], sv['zp'], sv['zn']], rw_par, [ones], d_outs, TMH)
        (d_prw,) = rowwise(t + 'shift_sum', f_add3, [d_z, shift_next(d_zp), shift_prev(d_zn)], [], [], [RWKV_IN], TM,
                           [BF16])
        G['rw_mu'][l] = d_rw[0].reshape(-1)
        G['rw_w0'][l] = d_rw[1].reshape(2, RWKV_DIM)
        G['rw_w2'][l] = jnp.stack([d_rw[2][:64, :RWKV_DIM], d_rw[2][64:, RWKV_DIM:]])
        G['rw_a0'][l] = d_rw[3].reshape(2, RWKV_DIM)
        G['rw_a2'][l] = jnp.stack([d_rw[4][:64, :RWKV_DIM], d_rw[4][64:, RWKV_DIM:]])
        G['rw_g2'][l] = d_rw[5]
        G['rw_k_k'][l], G['rw_k_a'][l] = d_rw[6].reshape(-1), d_rw[7].reshape(-1)
        (d_psg,), (d_sg_g, d_sg_b, d_sg_w, d_sg_bias) = rowwise_bwd(
            t + 'sg', f_sg, [sv['p_sg']], [p['sg_g'], p['sg_b'], p['sg_w'], p['sg_bias']], [], [[d_yb]], TMH,
            drow_dtypes=[BF16])
        G['sg_ln_g'][l], G['sg_ln_b'][l], G['sg_w'][l] = d_sg_g.reshape(-1), d_sg_b.reshape(-1), d_sg_w
        G['sg_b'][l] = d_sg_bias.reshape(SG_CHUNK, SG_GROUPS, SG_DIM // SG_GROUPS).sum(-1).T
        d_qh, d_kh, d_kvv = attention_bwd(t + 'attn', sv['qh'], sv['kh'], sv['kv'], d_ya, B, S, TQ)
        (d_qq, d_kv, d_pkr), _ = rowwise_bwd(
            t + 'rope', f_rope, [sv['qq'], sv['kv'], sv['p_kr'], ct, st], [], [], [[d_qh], [d_kh]], TM,
            n_row_diff=3, extra=[(1, d_kvv)], drow_dtypes=[BF16] * 3)
        d_cq = matmul(t + 'uq_dx', d_qq, p['w_uq'], 'nt')
        d_wuq = matmul(t + 'uq_dw', sv['cq'], d_qq, 'tn')
        g1, g2_ = _unpermute_full(d_wuq, wuq_src, wuq_sgn, MLA_HEADS * (QK_NOPE + QK_ROPE))
        G['w_uq'][l] = g1 + g2_
        d_ckv = matmul(t + 'ukv_dx', d_kv, p['w_ukv'], 'nt')
        G['w_ukv'][l] = matmul(t + 'ukv_dw', sv['ckv'], d_kv, 'tn')
        (d_pq,), (dg,) = rowwise_bwd(t + 'q_norm', f_rms, [sv['p_q']], [p['q_g']], [], [[d_cq]], TM,
                                     drow_dtypes=[BF16])
        G['q_norm_g'][l] = dg.reshape(-1)
        (d_pckv,), (dg,) = rowwise_bwd(t + 'kv_norm', f_rms, [sv['p_ckv']], [p['kv_g']], [], [[d_ckv]], TM,
                                       drow_dtypes=[BF16])
        G['kv_norm_g'][l] = dg.reshape(-1)
        d_h, d_cols = None, []
        for (sn, a, b), d_seg in zip(IN_SEGMENTS, [d_pq, d_pckv, d_pkr, d_psg, d_prw, d_pgate]):
            d_h = matmul(t + 'in_proj_dx_' + sn, d_seg, p['w_in'][:, a:b], 'nt', add=d_h)
            d_cols.append(matmul(t + 'in_proj_dw_' + sn, sv['h'], d_seg, 'tn'))
        d_win = jnp.concatenate(d_cols, axis=1)
        g1, g2_ = _unpermute_full(d_win, win_src, win_sgn, N_IN)
        kr0 = Q_LORA + KV_LORA
        G['w_in'][l] = g1.at[:, kr0:kr0 + QK_ROPE].add(g2_[:, kr0:kr0 + QK_ROPE])
        (dx,), (dg,) = rowwise_bwd(t + 'attn_norm', f_rms, [sv['x']], [p['attn_g']], [], [[d_h]], TM, extra=[(0, dx2)])
        G['attn_norm_g'][l] = dg.reshape(-1)

    grads = {n: jnp.stack(G[n]) for n in G}
    grads['final_norm_g'] = d_final_g.reshape(-1)
    grad_x = dx.reshape(B, S, D)

    per_shard = []
    for q in range(4):
        sl = []
        for n in SHARDED:
            ax = SHARD_AXIS[n]
            w = W[n].shape[ax]
            sl.append(lax.slice_in_dim(grads[n], q * w, (q + 1) * w, axis=ax))
        per_shard.append(_pack(sl, 2 * 8 * JOIN_CHUNKS))
    R = per_shard[0].shape[0]
    gpack = jnp.stack(per_shard).astype(BF16).reshape(4, 2, R // 2, PACK_C)
    half_sum = sum_slots("sum_sharded", scatter_partials(gpack))
    g_shard = dict(zip(SHARDED, _unpack(sibling_join(half_sum).reshape(R, PACK_C), shard_shapes)))
    rep_shapes = [W[n].shape for n in REPLICATED]
    rpack = _pack([grads[n] for n in REPLICATED], 8)
    g_rep = sum_slots("sum_replicated", gather_all("gather_replicated", rpack))

    outs = {}
    for n in MATMUL_SHARDED:
        shp = W[n].shape
        two = lambda a: a.reshape(-1, shp[-1])
        res = adamw("adamw_" + n, two(W[n]), two(g_shard[n]), two(M1[n]), two(M2[n]))
        outs['grad', n] = g_shard[n]
        for key, a in zip(('delta', 'new_m', 'new_v'), res):
            outs[key, n] = a.reshape(shp)
    small = SMALL_SHARDED + REPLICATED
    small_shapes = [W[n].shape for n in small]
    g_small = [g_shard[n] for n in SMALL_SHARDED] + _unpack(g_rep, rep_shapes)
    res = adamw("adamw_small", _pack([W[n] for n in small], 8), _pack(g_small, 8),
                _pack([M1[n] for n in small], 8), _pack([M2[n] for n in small], 8))
    for n, a in zip(small, g_small):
        outs['grad', n] = a
    for key, buf in zip(('delta', 'new_m', 'new_v'), res):
        for n, a in zip(small, _unpack(buf, small_shapes)):
            outs[key, n] = a
    return (loss, grad_x, *[outs['grad', n] for n in WEIGHTS], *[outs['delta', n] for n in WEIGHTS],
            *[outs['new_m', n] for n in WEIGHTS], *[outs['new_v', n] for n in WEIGHTS])
```

```python
import functools
import math

import numpy as np
import jax
import jax.numpy as jnp
from jax import lax
from jax.experimental import pallas as pl
from jax.experimental.pallas import tpu as pltpu

F32 = jnp.float32
BF16 = jnp.bfloat16

DEPTH = 2
MLA_HEADS = 8
Q_LORA = 384
KV_LORA = 256
QK_NOPE = 64
QK_ROPE = 32
V_HEAD = 64
ROPE_THETA = 10000.0
SG_GROUPS = 8
SG_DIM = 512
SG_CHUNK = 128
RWKV_HEADS = 8
RWKV_HEAD = 64
RWKV_DIM = 512
GN_EPS = 64e-5
NORM_EPS = 1e-6
D_FF = 2816
RWKV_IN = 1920
N_IN = 6688
ADAM_LR, ADAM_B1, ADAM_B2, ADAM_EPS, ADAM_WD, ADAM_STEP = 0.001, 0.9, 0.999, 1e-08, 0.01, 10

LANES = 128
HEAD_PAD = 128
VMEM_LIMIT = 56 * 1024 * 1024
MESH = pl.DeviceIdType.MESH

WEIGHTS = ['attn_norm_g', 'w_in', 'gate_b', 'q_norm_g', 'w_uq', 'kv_norm_g', 'w_ukv', 'sg_ln_g', 'sg_ln_b', 'sg_w',
           'sg_b', 'rw_mu', 'rw_w0', 'rw_w2', 'rw_a0', 'rw_a2', 'rw_g2', 'rw_k_k', 'rw_k_a', 'rw_r_k', 'rw_ln_g',
           'rw_ln_b', 'w_branch', 'w_out', 'ffn_norm_g', 'w_ffn_gate', 'w_ffn_up', 'w_ffn_down', 'final_norm_g']
SHARD_AXIS = {'w_in': 2, 'gate_b': 2, 'w_uq': 2, 'w_ukv': 2, 'rw_w0': 2, 'rw_w2': 3, 'rw_a0': 2, 'rw_a2': 3,
              'rw_g2': 2, 'w_branch': 3, 'w_out': 1, 'w_ffn_gate': 2, 'w_ffn_up': 2, 'w_ffn_down': 1}
SHARDED = [n for n in WEIGHTS if n in SHARD_AXIS]
REPLICATED = [n for n in WEIGHTS if n not in SHARD_AXIS]
SMALL_SHARDED = ['gate_b', 'rw_w0', 'rw_a0']
MATMUL_SHARDED = [n for n in SHARDED if n not in SMALL_SHARDED]


def _params(sem=None):
    return pltpu.CompilerParams(dimension_semantics=sem, vmem_limit_bytes=VMEM_LIMIT)


def _pick(n, cands):
    for c in cands:
        if n % c == 0:
            return c
    return n


def _dot(a, b, dims):
    return lax.dot_general(a.astype(BF16), b.astype(BF16), (dims, ((), ())), preferred_element_type=F32)


def _nn(a, b):
    return _dot(a, b, ((1,), (0,)))


def _nt(a, b):
    return _dot(a, b, ((1,), (1,)))


def _tn(a, b):
    return _dot(a, b, ((0,), (0,)))


@jax.custom_vjp
def mm(a, b):
    return _nn(a, b)


mm.defvjp(lambda a, b: (_nn(a, b), (a, b)), lambda res, g: (_nt(g, res[1]), _tn(res[0], g)))


@jax.custom_vjp
def mm_nt(a, b):
    return _nt(a, b)


mm_nt.defvjp(lambda a, b: (_nt(a, b), (a, b)), lambda res, g: (_nn(g, res[1]), _tn(g, res[0])))


def _seg_raw(x, ones):
    hi = x.astype(BF16)
    lo = (x - hi.astype(F32)).astype(BF16)
    d = (((1,), (0,)), ((), ()))
    return (lax.dot_general(hi, ones, d, preferred_element_type=F32)
            + lax.dot_general(lo, ones, d, preferred_element_type=F32))


@jax.custom_vjp
def segsum(x, ones):
    return _seg_raw(x, ones)


segsum.defvjp(lambda x, ones: (_seg_raw(x, ones), ones),
              lambda ones, g: (_seg_raw(g, ones), jnp.zeros_like(ones)))


def _sigmoid(x):
    return 0.5 * (jnp.tanh(0.5 * x) + 1.0)


def _rms(x, g):
    return x * lax.rsqrt(jnp.mean(x * x, axis=-1, keepdims=True) + NORM_EPS) * g


def matmul(name, a, b, mode, add=None, out_dtype=F32):
    if mode == 'nn':
        (M, K), (_, N) = a.shape, b.shape
    elif mode == 'nt':
        (M, K), (N, _) = a.shape, b.shape
    else:
        (K, M), (_, N) = a.shape, b.shape
    tm = _pick(M, (1408, 1024, 512, 384, 256, 128))
    tn = _pick(N, (1408, 1024, 768, 512, 384, 256, 128))
    tk = _pick(K, (512, 384, 256, 128))
    nk = K // tk
    dims = {'nn': ((1,), (0,)), 'nt': ((1,), (1,)), 'tn': ((0,), (0,))}[mode]
    a_spec = pl.BlockSpec((tk, tm), lambda i, j, k: (k, i)) if mode == 'tn' else pl.BlockSpec((tm, tk), lambda i, j, k: (i, k))
    b_spec = pl.BlockSpec((tn, tk), lambda i, j, k: (j, k)) if mode == 'nt' else pl.BlockSpec((tk, tn), lambda i, j, k: (k, j))
    o_spec = pl.BlockSpec((tm, tn), lambda i, j, k: (i, j))
    has_add = add is not None

    def body(*refs):
        if has_add:
            a_ref, b_ref, add_ref, o_ref, acc = refs
        else:
            a_ref, b_ref, o_ref, acc = refs
        k = pl.program_id(2)

        @pl.when(k == 0)
        def _():
            acc[...] = jnp.zeros_like(acc)

        acc[...] += _dot(a_ref[...], b_ref[...], dims)

        @pl.when(k == nk - 1)
        def _():
            o_ref[...] = (acc[...] + add_ref[...] if has_add else acc[...]).astype(o_ref.dtype)

    ins = [a, b] + ([add] if has_add else [])
    specs = [a_spec, b_spec] + ([o_spec] if has_add else [])
    return pl.pallas_call(
        body, name=name, grid=(M // tm, N // tn, nk), in_specs=specs, out_specs=o_spec,
        out_shape=jax.ShapeDtypeStruct((M, N), out_dtype), scratch_shapes=[pltpu.VMEM((tm, tn), F32)],
        compiler_params=_params(("parallel", "parallel", "arbitrary")))(*ins)


def _full_spec(p):
    nd = p.ndim
    return pl.BlockSpec(p.shape, lambda i, _nd=nd: (0,) * _nd)


def rowwise(name, fn, rows, params, consts, out_widths, tm, out_dtypes=None):
    N = rows[0].shape[0]
    nr, npar, nc = len(rows), len(params), len(consts)

    def body(*refs):
        vals = [r[...] for r in refs[:nr + npar + nc]]
        res = fn(*vals)
        for o, v in zip(refs[nr + npar + nc:], res):
            o[...] = v.astype(o.dtype)

    in_specs = ([pl.BlockSpec((tm, r.shape[1]), lambda i: (i, 0)) for r in rows]
                + [_full_spec(p) for p in list(params) + list(consts)])
    out_specs = [pl.BlockSpec((tm, w), lambda i: (i, 0)) for w in out_widths]
    return pl.pallas_call(
        body, name=name, grid=(N // tm,), in_specs=in_specs, out_specs=out_specs,
        out_shape=[jax.ShapeDtypeStruct((N, w), d) for w, d in zip(out_widths, out_dtypes or [F32] * len(out_widths))],
        compiler_params=_params(("parallel",)))(*rows, *params, *consts)


def rowwise_bwd(name, fn, rows, params, consts, d_outs, tm, n_row_diff=None, extra=(), drow_dtypes=None):
    N = rows[0].shape[0]
    nr, npar, nc = len(rows), len(params), len(consts)
    nd = nr if n_row_diff is None else n_row_diff
    counts = [len(p) for p in d_outs]
    flat_d = [a for parts in d_outs for a in parts]
    nflat, nex = len(flat_d), len(extra)

    def body(*refs):
        pos = 0
        row_v = [r[...] for r in refs[pos:pos + nr]]; pos += nr
        par_v = [r[...] for r in refs[pos:pos + npar]]; pos += npar
        con_v = [r[...] for r in refs[pos:pos + nc]]; pos += nc
        d_refs = refs[pos:pos + nflat]; pos += nflat
        ex_refs = refs[pos:pos + nex]; pos += nex
        drow_refs = refs[pos:pos + nd]; pos += nd
        dpar_refs = refs[pos:pos + npar]

        def f(*diff):
            return fn(*diff[:nd], *row_v[nd:], *diff[nd:], *con_v)

        _, vjp = jax.vjp(f, *row_v[:nd], *par_v)
        cts, q = [], 0
        for c in counts:
            g = d_refs[q][...].astype(F32)
            for t in range(1, c):
                g = g + d_refs[q + t][...].astype(F32)
            cts.append(g)
            q += c
        grads = vjp(tuple(cts))
        drow = list(grads[:nd])
        for (idx, _), r in zip(extra, ex_refs):
            drow[idx] = drow[idx] + r[...].astype(F32)
        for o, v in zip(drow_refs, drow):
            o[...] = v.astype(o.dtype)

        @pl.when(pl.program_id(0) == 0)
        def _():
            for o in dpar_refs:
                o[...] = jnp.zeros_like(o)

        for o, v in zip(dpar_refs, grads[nd:]):
            o[...] += v

    ex_arrs = [a for _, a in extra]
    in_specs = ([pl.BlockSpec((tm, r.shape[1]), lambda i: (i, 0)) for r in rows]
                + [_full_spec(p) for p in list(params) + list(consts)]
                + [pl.BlockSpec((tm, a.shape[1]), lambda i: (i, 0)) for a in flat_d + ex_arrs])
    out_specs = ([pl.BlockSpec((tm, r.shape[1]), lambda i: (i, 0)) for r in rows[:nd]]
                 + [_full_spec(p) for p in params])
    out_shape = ([jax.ShapeDtypeStruct(r.shape, d) for r, d in zip(rows[:nd], drow_dtypes or [F32] * nd)]
                 + [jax.ShapeDtypeStruct(p.shape, F32) for p in params])
    res = pl.pallas_call(
        body, name=name, grid=(N // tm,), in_specs=in_specs, out_specs=out_specs, out_shape=out_shape,
        compiler_params=_params(("arbitrary",)))(*rows, *params, *consts, *flat_d, *ex_arrs)
    return list(res[:nd]), list(res[nd:])


def f_rms(x, g):
    return (_rms(x, g),)


def f_rope(qq, kv, krr, ct, st):
    hw = MLA_HEADS * HEAD_PAD
    c8 = jnp.tile(ct, (1, MLA_HEADS))
    s8 = jnp.tile(st, (1, MLA_HEADS))
    q = qq[:, :hw] * c8 + qq[:, hw:] * s8
    kr = krr[:, :HEAD_PAD] * ct + krr[:, HEAD_PAD:] * st
    lane = lax.broadcasted_iota(jnp.int32, kv.shape, 1) % HEAD_PAD
    k = jnp.where(lane < QK_NOPE, kv, jnp.tile(kr, (1, MLA_HEADS)))
    return q, k


def f_sg(p, ln_g, ln_b, w, bias):
    z = 0.5 * p * (1.0 + jnp.tanh(0.7978845608028654 * (p + 0.044715 * p * p * p)))
    u, v = z[:, :SG_DIM], z[:, SG_DIM:]
    mu = jnp.mean(v, axis=-1, keepdims=True)
    var = jnp.mean(jnp.square(v - mu), axis=-1, keepdims=True)
    v = (v - mu) * lax.rsqrt(var + 1e-5) * ln_g + ln_b
    lane = lax.broadcasted_iota(jnp.int32, (SG_CHUNK, LANES), 1)
    outs = []
    for c in range(p.shape[0] // SG_CHUNK):
        vc = v[c * SG_CHUNK:(c + 1) * SG_CHUNK]
        cols = []
        for m in range(SG_DIM // LANES):
            blk = vc[:, m * LANES:(m + 1) * LANES]
            cols.append(jnp.where(lane < 64, mm(w[2 * m], blk), mm(w[2 * m + 1], blk)))
        outs.append(jnp.concatenate(cols, axis=1) + bias)
    mixed = outs[0] if len(outs) == 1 else jnp.concatenate(outs, axis=0)
    return (u * mixed,)


def f_rw_pre(z, zp, zn, mu, w0, w2, a0, a2, g2, k_k, k_a, ones):
    z = z + mu * (0.5 * (zp + zn) - z)
    C = RWKV_DIM
    r, k, v = z[:, :C], z[:, C:2 * C], z[:, 2 * C:3 * C]
    wl, al, gl = z[:, 3 * C:3 * C + 128], z[:, 3 * C + 128:3 * C + 256], z[:, 3 * C + 256:]
    w = w0 + mm(jnp.tanh(wl), w2)
    decay = jnp.exp(-0.6065306597126334 * _sigmoid(w))
    a = _sigmoid(a0 + mm(al, a2))
    g = mm(_sigmoid(gl), g2)
    kk = k * k_k
    kk = kk / jnp.maximum(jnp.sqrt(segsum(kk * kk, ones)), 1e-12)
    k2 = jnp.concatenate([k, k], axis=1)
    kdir = k2 * (1.0 + (a - 1.0) * jnp.concatenate([k_a, k_a], axis=1))
    bdir = jnp.concatenate([kk, kk], axis=1) * a
    return r, v, decay, kdir, kk, bdir, g


def f_rw_post(y, r, v, kdir, g, ln_g, ln_b, r_k, ones):
    mean = segsum(y, ones) * (1.0 / RWKV_HEAD)
    yc = y - mean
    var = segsum(yc * yc, ones) * (1.0 / RWKV_HEAD)
    y = yc * lax.rsqrt(var + GN_EPS) * ln_g + ln_b
    C = RWKV_DIM
    bonus = segsum(r * kdir[:, :C] * r_k, ones) + segsum(r * kdir[:, C:] * r_k, ones)
    return ((y + bonus * v) * g,)


def f_merge(pg, b0, b1, b2, gate_b):
    D = b0.shape[1]
    gt = _sigmoid(pg + gate_b)
    return (gt[:, :D] * b0 + gt[:, D:2 * D] * b1 + gt[:, 2 * D:] * b2,)


def f_swiglu(au):
    a, u = au[:, :D_FF], au[:, D_FF:]
    return (a * _sigmoid(a) * u,)


def f_add3(a, b, c):
    return (a + b + c,)


def loss_head(x, tgt, g, tm):
    N, D = x.shape

    def body(x_ref, t_ref, g_ref, loss_ref, dx_ref, dg_ref):
        t = t_ref[...]

        def f(xv, gv):
            err = _rms(xv, gv) - t
            return 0.5 * jnp.sum(jnp.mean(err * err, axis=-1, keepdims=True))

        val, (dx, dg) = jax.value_and_grad(f, argnums=(0, 1))(x_ref[...], g_ref[...])
        dx_ref[...] = dx

        @pl.when(pl.program_id(0) == 0)
        def _():
            loss_ref[...] = jnp.zeros_like(loss_ref)
            dg_ref[...] = jnp.zeros_like(dg_ref)

        loss_ref[...] += jnp.full(loss_ref.shape, val, F32)
        dg_ref[...] += dg

    row = pl.BlockSpec((tm, D), lambda i: (i, 0))
    return pl.pallas_call(
        body, name="loss_head", grid=(N // tm,), in_specs=[row, row, _full_spec(g)],
        out_specs=[pl.BlockSpec((1, LANES), lambda i: (0, 0)), row, _full_spec(g)],
        out_shape=[jax.ShapeDtypeStruct((1, LANES), F32), jax.ShapeDtypeStruct((N, D), F32),
                   jax.ShapeDtypeStruct(g.shape, F32)],
        compiler_params=_params(("arbitrary",)))(x, tgt, g)


ATT_SCALE = float((QK_NOPE + QK_ROPE) ** -0.5)


def _attn_block(q, k, kv):
    s = mm_nt(q, k) * ATT_SCALE
    m = lax.stop_gradient(jnp.max(s, axis=-1, keepdims=True))
    e = jnp.exp(s - m)
    p = e / jnp.sum(e, axis=-1, keepdims=True)
    return mm(p, kv)


def attention_fwd(name, q, k, kv, B, S, tq, out_dtype):
    nq = S // tq
    qspec = pl.BlockSpec((tq, HEAD_PAD), lambda b, h, i: (b * nq + i, h))
    kspec = pl.BlockSpec((S, HEAD_PAD), lambda b, h, i: (b, h))

    def body(q_ref, k_ref, kv_ref, o_ref):
        o_ref[...] = _attn_block(q_ref[...], k_ref[...], kv_ref[...]).astype(o_ref.dtype)

    return pl.pallas_call(
        body, name=name, grid=(B, MLA_HEADS, nq), in_specs=[qspec, kspec, kspec], out_specs=qspec,
        out_shape=jax.ShapeDtypeStruct(q.shape, out_dtype),
        compiler_params=_params(("parallel", "parallel", "arbitrary")))(q, k, kv)


def attention_bwd(name, q, k, kv, do, B, S, tq):
    nq = S // tq
    qspec = pl.BlockSpec((tq, HEAD_PAD), lambda b, h, i: (b * nq + i, h))
    kspec = pl.BlockSpec((S, HEAD_PAD), lambda b, h, i: (b, h))

    def body(q_ref, k_ref, kv_ref, do_ref, dq_ref, dk_ref, dkv_ref):
        _, vjp = jax.vjp(_attn_block, q_ref[...], k_ref[...], kv_ref[...])
        dq, dk, dkv = vjp(do_ref[...])
        dq_ref[...] = dq

        @pl.when(pl.program_id(2) == 0)
        def _():
            dk_ref[...] = jnp.zeros_like(dk_ref)
            dkv_ref[...] = jnp.zeros_like(dkv_ref)

        dk_ref[...] += dk
        dkv_ref[...] += dkv

    sh = jax.ShapeDtypeStruct(q.shape, F32)
    return pl.pallas_call(
        body, name=name, grid=(B, MLA_HEADS, nq), in_specs=[qspec, kspec, kspec, qspec],
        out_specs=[qspec, kspec, kspec], out_shape=[sh, sh, sh],
        compiler_params=_params(("parallel", "parallel", "arbitrary")))(q, k, kv, do)


SCAN_TC = 8
SCAN_UNROLL = 16


def _jloop(n, body, init):
    def outer(o, c):
        for u in range(SCAN_UNROLL):
            c = body(o * SCAN_UNROLL + u, c)
        return c

    return lax.fori_loop(0, n // SCAN_UNROLL, outer, init)


def _dir_mask(L, Ip):
    per_dir = L // (2 * (RWKV_HEAD // Ip))
    lane = lax.broadcasted_iota(jnp.int32, (1, L), 1)
    return (lane // per_dir) % 2 == 1


def _merge_dirs(mask, fwd_ref, rev_ref, out_ref):
    for tt in range(SCAN_TC):
        out_ref[tt] = jnp.where(mask, rev_ref[SCAN_TC - 1 - tt], fwd_ref[tt])


def scan_fwd(name, w, k, b, kk, r, v):
    T, J, L = w.shape
    Ip = v.shape[1]
    nT = T // SCAN_TC
    fwd3, rev3 = (lambda g: (g, 0, 0)), (lambda g: (nT - 1 - g, 0, 0))
    jf, jr = pl.BlockSpec((SCAN_TC, J, L), fwd3), pl.BlockSpec((SCAN_TC, J, L), rev3)
    i_f, i_r = pl.BlockSpec((SCAN_TC, Ip, L), fwd3), pl.BlockSpec((SCAN_TC, Ip, L), rev3)
    sspec = pl.BlockSpec((SCAN_TC, J, Ip, L), lambda g: (g, 0, 0, 0))
    last_spec = pl.BlockSpec((J, Ip, L), lambda g: (0, 0, 0))

    def body(wf, wr, kf, kr, bf, br, kkf, kkr, rf, rr, vf, vr, yf_ref, yr_ref, sp_ref, sa_ref, last_ref,
             s_ref, w_ref, k_ref, b_ref, kk_ref, r_ref, v_ref):
        @pl.when(pl.program_id(0) == 0)
        def _():
            s_ref[...] = jnp.zeros_like(s_ref)

        mask = _dir_mask(L, Ip)
        for f_, r_, m_ in ((wf, wr, w_ref), (kf, kr, k_ref), (bf, br, b_ref), (kkf, kkr, kk_ref), (rf, rr, r_ref),
                           (vf, vr, v_ref)):
            _merge_dirs(mask, f_, r_, m_)

        def row(ref, tt, j):
            return jnp.broadcast_to(ref[tt, pl.ds(j, 1), :], (Ip, L))

        def step(tt, carry):
            def p1(j, sa):
                s = s_ref[j]
                sp_ref[tt, j] = s
                return sa + s * row(kk_ref, tt, j)

            sa = _jloop(J, p1, jnp.zeros((Ip, L), F32))
            sa_ref[tt] = sa
            vt = v_ref[tt]

            def p2(j, y):
                s = s_ref[j] * row(w_ref, tt, j) - sa * row(b_ref, tt, j) + vt * row(k_ref, tt, j)
                s_ref[j] = s
                return y + s * row(r_ref, tt, j)

            y = _jloop(J, p2, jnp.zeros((Ip, L), F32))
            yf_ref[tt] = y
            yr_ref[SCAN_TC - 1 - tt] = y
            return carry

        lax.fori_loop(0, SCAN_TC, step, 0)

        @pl.when(pl.program_id(0) == nT - 1)
        def _():
            last_ref[...] = s_ref[...]

    ish = jax.ShapeDtypeStruct((T, Ip, L), F32)
    jscr = pltpu.VMEM((SCAN_TC, J, L), F32)
    return pl.pallas_call(
        body, name=name, grid=(nT,), in_specs=[jf, jr] * 5 + [i_f, i_r],
        out_specs=[i_f, i_r, sspec, i_f, last_spec],
        out_shape=[ish, ish, jax.ShapeDtypeStruct((T, J, Ip, L), F32), ish, jax.ShapeDtypeStruct((J, Ip, L), F32)],
        scratch_shapes=[pltpu.VMEM((J, Ip, L), F32)] + [jscr] * 5 + [pltpu.VMEM((SCAN_TC, Ip, L), F32)],
        compiler_params=_params(("arbitrary",)))(w, w, k, k, b, b, kk, kk, r, r, v, v)


def scan_bwd(name, w, k, b, kk, r, v, sp, sa_all, s_last, dy):
    T, J, L = w.shape
    Ip = v.shape[1]
    nT = T // SCAN_TC
    stp3, mir3 = (lambda g: (nT - 1 - g, 0, 0)), (lambda g: (g, 0, 0))
    jf, jr = pl.BlockSpec((SCAN_TC, J, L), stp3), pl.BlockSpec((SCAN_TC, J, L), mir3)
    i_f, i_r = pl.BlockSpec((SCAN_TC, Ip, L), stp3), pl.BlockSpec((SCAN_TC, Ip, L), mir3)
    sspec = pl.BlockSpec((SCAN_TC, J, Ip, L), lambda g: (nT - 1 - g, 0, 0, 0))
    last_spec = pl.BlockSpec((J, Ip, L), lambda g: (0, 0, 0))

    def body(wf, wr, kf, kr, bf, br, kkf, kkr, rf, rr, vf, vr, dyf, dyr, sp_ref, sa_ref, last_ref,
             dwf, dwr, dkf, dkr, dbf, dbr, dkkf, dkkr, drf, drr, dvf, dvr,
             ds_ref, nxt_ref, w_ref, k_ref, b_ref, kk_ref, r_ref, v_ref, dy_ref):
        @pl.when(pl.program_id(0) == 0)
        def _():
            ds_ref[...] = jnp.zeros_like(ds_ref)
            nxt_ref[...] = last_ref[...]

        mask = _dir_mask(L, Ip)
        for f_, r_, m_ in ((wf, wr, w_ref), (kf, kr, k_ref), (bf, br, b_ref), (kkf, kkr, kk_ref), (rf, rr, r_ref),
                           (vf, vr, v_ref), (dyf, dyr, dy_ref)):
            _merge_dirs(mask, f_, r_, m_)

        def row(ref, tt, j):
            return jnp.broadcast_to(ref[tt, pl.ds(j, 1), :], (Ip, L))

        def rsum(x):
            return jnp.sum(x, axis=0, keepdims=True)

        def make_step(first):
            def step(n, carry):
                tt = SCAN_TC - 1 - n
                dyt, vt, sa = dy_ref[tt], v_ref[tt], sa_ref[tt]

                def p1(j, c):
                    dsa, dv = c
                    ds = ds_ref[j] + dyt * row(r_ref, tt, j)
                    ds_ref[j] = ds
                    return dsa - ds * row(b_ref, tt, j), dv + ds * row(k_ref, tt, j)

                z = jnp.zeros((Ip, L), F32)
                dsa, dv = _jloop(J, p1, (z, z))
                dvf[tt] = dv
                dvr[SCAN_TC - 1 - tt] = dv

                def put(f_ref, r_ref_, j, val):
                    f_ref[tt, pl.ds(j, 1), :] = val
                    r_ref_[SCAN_TC - 1 - tt, pl.ds(j, 1), :] = val

                def p2(j, c):
                    ds = ds_ref[j]
                    s0 = sp_ref[tt, j]
                    s1 = nxt_ref[j] if first else sp_ref[tt + 1, j]
                    put(drf, drr, j, rsum(s1 * dyt))
                    put(dkf, dkr, j, rsum(ds * vt))
                    put(dbf, dbr, j, -rsum(ds * sa))
                    put(dwf, dwr, j, rsum(ds * s0))
                    put(dkkf, dkkr, j, rsum(s0 * dsa))
                    ds_ref[j] = ds * row(w_ref, tt, j) + dsa * row(kk_ref, tt, j)
                    return c

                _jloop(J, p2, 0)
                return carry

            return step

        make_step(True)(0, 0)
        lax.fori_loop(1, SCAN_TC, make_step(False), 0)
        nxt_ref[...] = sp_ref[0]

    jsh = jax.ShapeDtypeStruct((T, J, L), F32)
    ish = jax.ShapeDtypeStruct((T, Ip, L), F32)
    jscr = pltpu.VMEM((SCAN_TC, J, L), F32)
    iscr = pltpu.VMEM((SCAN_TC, Ip, L), F32)
    return pl.pallas_call(
        body, name=name, grid=(nT,), in_specs=[jf, jr] * 5 + [i_f, i_r] * 2 + [sspec, i_f, last_spec],
        out_specs=[jf, jr] * 5 + [i_f, i_r], out_shape=[jsh] * 10 + [ish] * 2,
        scratch_shapes=[pltpu.VMEM((J, Ip, L), F32)] * 2 + [jscr] * 5 + [iscr] * 2,
        compiler_params=_params(("arbitrary",)))(w, w, k, k, b, b, kk, kk, r, r, v, v, dy, dy, sp, sa_all, s_last)


LAYOUT_TT = 32
LAYOUT_U = 8


def _lane_group(L, n):
    return lax.broadcasted_iota(jnp.int32, (1, L), 1) // (L // n)


def to_scan(name, x, B, S, nd, Ip):
    isplit = LANES // (2 * B * RWKV_HEADS)
    tt_n = _pick(S, (LAYOUT_TT, 16, 8))
    x5 = x.reshape(B, S, nd, RWKV_HEADS, RWKV_HEAD)
    rows_out = RWKV_HEAD if Ip is None else Ip

    def body(x_ref, o_ref):
        group = _lane_group(LANES, isplit)

        def chunk(c, carry):
            t0 = c * LAYOUT_U
            z = jnp.concatenate([x_ref[b, t0 + u, min(d, nd - 1)] for u in range(LAYOUT_U) for _ in range(isplit)
                                 for d in range(2) for b in range(B)], axis=0)
            mt = z.T
            for u in range(LAYOUT_U):
                m = mt[:, u * LANES:(u + 1) * LANES]
                if Ip is not None:
                    m = sum(jnp.where(group == i2, m[i2 * Ip:(i2 + 1) * Ip], 0.0) for i2 in range(isplit))
                o_ref[t0 + u] = m
            return carry

        lax.fori_loop(0, tt_n // LAYOUT_U, chunk, 0)

    return pl.pallas_call(
        body, name=name, grid=(S // tt_n,),
        in_specs=[pl.BlockSpec((B, tt_n, nd, RWKV_HEADS, RWKV_HEAD), lambda g: (0, g, 0, 0, 0))],
        out_specs=pl.BlockSpec((tt_n, rows_out, LANES), lambda g: (g, 0, 0)),
        out_shape=jax.ShapeDtypeStruct((S, rows_out, LANES), F32), compiler_params=_params(("parallel",)))(x5)


def from_scan(name, f, r, B, S, nd, i_indexed):
    rows_in = f.shape[1]
    isplit = LANES // (2 * B * RWKV_HEADS)
    Ip = rows_in if i_indexed else RWKV_HEAD // isplit
    tt_n = _pick(S, (LAYOUT_TT, 16, 8))
    per_i2 = LANES // isplit

    def body(f_ref, r_ref, o_ref):
        mask = _dir_mask(LANES, Ip)
        group = _lane_group(LANES, isplit)

        def chunk(c, carry):
            t0 = c * LAYOUT_U
            ms = []
            for u in range(LAYOUT_U):
                m = jnp.where(mask, r_ref[t0 + u], f_ref[t0 + u])
                if i_indexed:
                    m = jnp.concatenate([jnp.where(group == i2, m, 0.0) for i2 in range(isplit)], axis=0)
                ms.append(m)
            zt = jnp.concatenate(ms, axis=1).T
            for u in range(LAYOUT_U):
                z = zt[u * LANES:(u + 1) * LANES]
                zf = sum(z[i2 * per_i2:(i2 + 1) * per_i2] for i2 in range(isplit))
                for b in range(B):
                    d0 = zf[b * RWKV_HEADS:(b + 1) * RWKV_HEADS]
                    d1 = zf[(B + b) * RWKV_HEADS:(B + b + 1) * RWKV_HEADS]
                    if nd == 1:
                        o_ref[b, t0 + u, 0] = d0 + d1
                    else:
                        o_ref[b, t0 + u, 0] = d0
                        o_ref[b, t0 + u, 1] = d1
            return carry

        lax.fori_loop(0, tt_n // LAYOUT_U, chunk, 0)

    spec = pl.BlockSpec((tt_n, rows_in, LANES), lambda g: (g, 0, 0))
    out = pl.pallas_call(
        body, name=name, grid=(S // tt_n,), in_specs=[spec, spec],
        out_specs=pl.BlockSpec((B, tt_n, nd, RWKV_HEADS, RWKV_HEAD), lambda g: (0, g, 0, 0, 0)),
        out_shape=jax.ShapeDtypeStruct((B, S, nd, RWKV_HEADS, RWKV_HEAD), F32),
        compiler_params=_params(("parallel",)))(f, r)
    return out.reshape(B * S, nd * RWKV_DIM)


def adamw(name, w, g, m, v):
    R, C = w.shape
    tr = _pick(R, (256, 128, 64, 32, 16, 8))
    c1 = 1.0 - ADAM_B1 ** ADAM_STEP
    c2 = 1.0 - ADAM_B2 ** ADAM_STEP

    def body(w_ref, g_ref, m_ref, v_ref, d_ref, nm_ref, nv_ref):
        gv = g_ref[...]
        nm = ADAM_B1 * m_ref[...] + (1.0 - ADAM_B1) * gv
        nv = ADAM_B2 * v_ref[...] + (1.0 - ADAM_B2) * jnp.square(gv)
        d_ref[...] = -ADAM_LR * ((nm / c1) / (jnp.sqrt(nv / c2) + ADAM_EPS) + ADAM_WD * w_ref[...])
        nm_ref[...] = nm
        nv_ref[...] = nv

    spec = pl.BlockSpec((tr, C), lambda i: (i, 0))
    sh = jax.ShapeDtypeStruct((R, C), F32)
    return pl.pallas_call(body, name=name, grid=(R // tr,), in_specs=[spec] * 4, out_specs=[spec] * 3,
                          out_shape=[sh] * 3, compiler_params=_params(("parallel",)))(w, g, m, v)


def sum_slots(name, x):
    n, R, C = x.shape
    tr = _pick(R, (256, 128, 64, 32, 16, 8))

    def body(x_ref, o_ref):
        acc = x_ref[0].astype(F32)
        for s in range(1, n):
            acc = acc + x_ref[s].astype(F32)
        o_ref[...] = acc

    return pl.pallas_call(
        body, name=name, grid=(R // tr,), in_specs=[pl.BlockSpec((n, tr, C), lambda i: (0, i, 0))],
        out_specs=pl.BlockSpec((tr, C), lambda i: (i, 0)), out_shape=jax.ShapeDtypeStruct((R, C), F32),
        compiler_params=_params(("parallel",)))(x)


ANY = pl.BlockSpec(memory_space=pl.ANY)


def _xyc():
    return lax.axis_index("x"), lax.axis_index("y"), lax.axis_index("c")


def gather_shards(shard):
    _, R, C = shard.shape

    def body(x_ref, out_ref, send_sems, recv_sems, local_sem):
        x, y, c = _xyc()
        me, sibling = (x, y, c), (x, y, 1 - c)
        chips = [(1 - x, y), (x, 1 - y), (1 - x, 1 - y)]

        def cp(k, cx, cy, half, to, src=None):
            dst = out_ref.at[2 * cx + cy, half]
            return pltpu.make_async_remote_copy(
                src_ref=dst if src is None else src, dst_ref=dst, send_sem=send_sems.at[k],
                recv_sem=recv_sems.at[k], device_id=to, device_id_type=MESH)

        mine = pltpu.make_async_copy(x_ref, out_ref.at[2 * x + y], local_sem)
        mine.start()
        first = [cp(j, x, y, c, (*chip, c), src=x_ref.at[c]) for j, chip in enumerate(chips)]
        for f in first:
            f.start()
        passed = [cp(3 + j, *chip, c, sibling) for j, chip in enumerate(chips)]
        for j, chip in enumerate(chips):
            cp(j, *chip, c, me).wait_recv()
            passed[j].start()
        for j, chip in enumerate(chips):
            cp(3 + j, *chip, 1 - c, me).wait_recv()
        for f in first + passed:
            f.wait_send()
        mine.wait()

    return pl.pallas_call(
        body, name="gather_shards", in_specs=[ANY], out_specs=ANY,
        out_shape=jax.ShapeDtypeStruct((4, 2, R, C), shard.dtype),
        scratch_shapes=[pltpu.SemaphoreType.DMA((6,)), pltpu.SemaphoreType.DMA((6,)), pltpu.SemaphoreType.DMA])(shard)


FLIPS = [(0, 0, 1), (0, 1, 0), (0, 1, 1), (1, 0, 0), (1, 0, 1), (1, 1, 0), (1, 1, 1)]


def scatter_partials(g):
    _, _, R, C = g.shape

    def body(g_ref, out_ref, send_sems, recv_sems, local_sem):
        x, y, c = _xyc()
        me_idx = 4 * x + 2 * y + c
        mine = pltpu.make_async_copy(g_ref.at[2 * x + y, c], out_ref.at[me_idx], local_sem)
        mine.start()
        sends = []
        for k, (fx, fy, fc) in enumerate(FLIPS):
            px, py, pc = (x + fx) % 2, (y + fy) % 2, (c + fc) % 2
            s = pltpu.make_async_remote_copy(
                src_ref=g_ref.at[2 * px + py, pc], dst_ref=out_ref.at[me_idx], send_sem=send_sems.at[k],
                recv_sem=recv_sems.at[k], device_id=(px, py, pc), device_id_type=MESH)
            s.start()
            sends.append(s)
        for k, (fx, fy, fc) in enumerate(FLIPS):
            px, py, pc = (x + fx) % 2, (y + fy) % 2, (c + fc) % 2
            slot = out_ref.at[4 * px + 2 * py + pc]
            pltpu.make_async_remote_copy(
                src_ref=slot, dst_ref=slot, send_sem=send_sems.at[k], recv_sem=recv_sems.at[k],
                device_id=(px, py, pc), device_id_type=MESH).wait_recv()
        for s in sends:
            s.wait_send()
        mine.wait()

    return pl.pallas_call(
        body, name="scatter_partials", in_specs=[ANY], out_specs=ANY,
        out_shape=jax.ShapeDtypeStruct((8, R, C), g.dtype),
        scratch_shapes=[pltpu.SemaphoreType.DMA((7,)), pltpu.SemaphoreType.DMA((7,)), pltpu.SemaphoreType.DMA])(g)


JOIN_CHUNKS = 8


def sibling_join(half):
    R, C = half.shape
    rows = R // JOIN_CHUNKS

    def body(h_ref, out_ref, send_sems, recv_sems, local_sem):
        x, y, c = _xyc()
        mine = pltpu.make_async_copy(h_ref, out_ref.at[c], local_sem)
        mine.start()
        sends = []
        for k in range(JOIN_CHUNKS):
            s = pltpu.make_async_remote_copy(
                src_ref=h_ref.at[pl.ds(k * rows, rows)], dst_ref=out_ref.at[c, pl.ds(k * rows, rows)],
                send_sem=send_sems.at[k], recv_sem=recv_sems.at[k], device_id=(x, y, 1 - c), device_id_type=MESH)
            s.start()
            sends.append(s)
        for k in range(JOIN_CHUNKS):
            theirs = out_ref.at[1 - c, pl.ds(k * rows, rows)]
            pltpu.make_async_remote_copy(src_ref=theirs, dst_ref=theirs, send_sem=send_sems.at[k],
                                         recv_sem=recv_sems.at[k], device_id=(x, y, 1 - c),
                                         device_id_type=MESH).wait_recv()
        for s in sends:
            s.wait_send()
        mine.wait()

    return pl.pallas_call(
        body, name="sibling_join", in_specs=[ANY], out_specs=ANY,
        out_shape=jax.ShapeDtypeStruct((2, R, C), half.dtype),
        scratch_shapes=[pltpu.SemaphoreType.DMA((JOIN_CHUNKS,)), pltpu.SemaphoreType.DMA((JOIN_CHUNKS,)),
                        pltpu.SemaphoreType.DMA])(half)


def gather_all(name, block):
    R, C = block.shape

    def body(x_ref, out_ref, send_sems, recv_sems, local_sem):
        x, y, c = _xyc()
        mine = pltpu.make_async_copy(x_ref, out_ref.at[4 * x + 2 * y + c], local_sem)
        mine.start()
        sends = []
        for k, (fx, fy, fc) in enumerate(FLIPS):
            px, py, pc = (x + fx) % 2, (y + fy) % 2, (c + fc) % 2
            s = pltpu.make_async_remote_copy(
                src_ref=x_ref, dst_ref=out_ref.at[4 * x + 2 * y + c], send_sem=send_sems.at[k],
                recv_sem=recv_sems.at[k], device_id=(px, py, pc), device_id_type=MESH)
            s.start()
            sends.append(s)
        for k, (fx, fy, fc) in enumerate(FLIPS):
            px, py, pc = (x + fx) % 2, (y + fy) % 2, (c + fc) % 2
            slot = out_ref.at[4 * px + 2 * py + pc]
            pltpu.make_async_remote_copy(
                src_ref=slot, dst_ref=slot, send_sem=send_sems.at[k], recv_sem=recv_sems.at[k],
                device_id=(px, py, pc), device_id_type=MESH).wait_recv()
        for s in sends:
            s.wait_send()
        mine.wait()

    return pl.pallas_call(
        body, name=name, in_specs=[ANY], out_specs=ANY,
        out_shape=jax.ShapeDtypeStruct((8, R, C), block.dtype),
        scratch_shapes=[pltpu.SemaphoreType.DMA((7,)), pltpu.SemaphoreType.DMA((7,)), pltpu.SemaphoreType.DMA])(block)


PACK_C = 1024


def _pack(arrs, row_mult):
    flat = jnp.concatenate([a.reshape(-1) for a in arrs])
    n = flat.shape[0]
    rows = -(-n // PACK_C)
    rows = -(-rows // row_mult) * row_mult
    return jnp.pad(flat, (0, rows * PACK_C - n)).reshape(rows, PACK_C)


def _unpack(buf, shapes):
    flat = buf.reshape(-1)
    out, off = [], 0
    for s in shapes:
        n = int(np.prod(s))
        out.append(flat[off:off + n].reshape(s))
        off += n
    return out


OFF_Q, OFF_CKV, OFF_KR, OFF_SG, OFF_RW, OFF_GATE, N_IN_PAD = 0, 384, 640, 896, 1920, 3840, 6912
IN_SEGMENTS = [('q', OFF_Q, OFF_CKV), ('ckv', OFF_CKV, OFF_KR), ('kr', OFF_KR, OFF_SG), ('sg', OFF_SG, OFF_RW),
               ('rw', OFF_RW, OFF_GATE), ('gate', OFF_GATE, N_IN_PAD)]
ROPE_LANE = QK_NOPE
HALF = QK_ROPE // 2


def _win_layout():
    src = np.full((N_IN_PAD,), -1, np.int64)
    sgn = np.ones((N_IN_PAD,), np.float32)
    src[0:640] = np.arange(0, 640)
    kr0 = Q_LORA + KV_LORA
    src[OFF_KR + ROPE_LANE:OFF_KR + ROPE_LANE + QK_ROPE] = kr0 + np.arange(QK_ROPE)
    sw = OFF_KR + HEAD_PAD + ROPE_LANE
    src[sw:sw + HALF] = kr0 + HALF + np.arange(HALF)
    sgn[sw:sw + HALF] = -1.0
    src[sw + HALF:sw + QK_ROPE] = kr0 + np.arange(HALF)
    src[OFF_SG:N_IN_PAD] = 672 + np.arange(N_IN_PAD - OFF_SG)
    return src, sgn


def _wuq_layout():
    hw = MLA_HEADS * HEAD_PAD
    src = np.full((2 * hw,), -1, np.int64)
    sgn = np.ones((2 * hw,), np.float32)
    per = QK_NOPE + QK_ROPE
    for h in range(MLA_HEADS):
        src[h * HEAD_PAD:h * HEAD_PAD + per] = h * per + np.arange(per)
        sw = hw + h * HEAD_PAD + ROPE_LANE
        src[sw:sw + HALF] = h * per + QK_NOPE + HALF + np.arange(HALF)
        sgn[sw:sw + HALF] = -1.0
        src[sw + HALF:sw + QK_ROPE] = h * per + QK_NOPE + np.arange(HALF)
    return src, sgn


def _permute_cols(w, src, sgn):
    cols = jnp.take(w, jnp.asarray(np.maximum(src, 0)), axis=1)
    return cols * jnp.asarray(np.where(src >= 0, sgn, 0.0).astype(np.float32)).astype(w.dtype)


def _unpermute_full(dw, src, sgn, n_cols):
    first = np.full((n_cols,), -1, np.int64)
    second = np.full((n_cols,), -1, np.int64)
    for pos, s in enumerate(src):
        if s < 0:
            continue
        if first[s] < 0:
            first[s] = pos
        else:
            second[s] = pos
    out = jnp.take(dw, jnp.asarray(first), axis=1) * jnp.asarray(sgn[first].astype(np.float32))
    m2 = (second >= 0)
    two = jnp.take(dw, jnp.asarray(np.maximum(second, 0)), axis=1) * jnp.asarray(
        np.where(m2, sgn[np.maximum(second, 0)], 0.0).astype(np.float32))
    return out, two


def _blockdiag(w):
    z = jnp.zeros_like(w[0])
    return jnp.concatenate([jnp.concatenate([w[0], z], axis=1), jnp.concatenate([z, w[1]], axis=1)], axis=0)


def _rope_tables(pos):
    inv = 1.0 / (ROPE_THETA ** (jnp.arange(0, QK_ROPE, 2, dtype=F32) / QK_ROPE))
    ang = pos.astype(F32)[:, None] * inv[None, :]
    cos, sin = jnp.cos(ang), jnp.sin(ang)
    pad = lambda t, fill: jnp.concatenate(
        [jnp.full((t.shape[0], ROPE_LANE), fill, F32), t, t, jnp.full((t.shape[0], HEAD_PAD - ROPE_LANE - QK_ROPE), fill, F32)], axis=1)
    return pad(cos, 1.0), pad(sin, 0.0)


def kernel(x, positions, attn_norm_g, w_in, gate_b, q_norm_g, w_uq, kv_norm_g, w_ukv, sg_ln_g, sg_ln_b, sg_w, sg_b, rw_mu, rw_w0, rw_w2, rw_a0, rw_a2, rw_g2, rw_k_k, rw_k_a, rw_r_k, rw_ln_g, rw_ln_b, w_branch, w_out, ffn_norm_g, w_ffn_gate, w_ffn_up, w_ffn_down, final_norm_g, loss_target, m_attn_norm_g, m_w_in, m_gate_b, m_q_norm_g, m_w_uq, m_kv_norm_g, m_w_ukv, m_sg_ln_g, m_sg_ln_b, m_sg_w, m_sg_b, m_rw_mu, m_rw_w0, m_rw_w2, m_rw_a0, m_rw_a2, m_rw_g2, m_rw_k_k, m_rw_k_a, m_rw_r_k, m_rw_ln_g, m_rw_ln_b, m_w_branch, m_w_out, m_ffn_norm_g, m_w_ffn_gate, m_w_ffn_up, m_w_ffn_down, m_final_norm_g, v_attn_norm_g, v_w_in, v_gate_b, v_q_norm_g, v_w_uq, v_kv_norm_g, v_w_ukv, v_sg_ln_g, v_sg_ln_b, v_sg_w, v_sg_b, v_rw_mu, v_rw_w0, v_rw_w2, v_rw_a0, v_rw_a2, v_rw_g2, v_rw_k_k, v_rw_k_a, v_rw_r_k, v_rw_ln_g, v_rw_ln_b, v_w_branch, v_w_out, v_ffn_norm_g, v_w_ffn_gate, v_w_ffn_up, v_w_ffn_down, v_final_norm_g):
    args = locals()
    W = {n: args[n] for n in WEIGHTS}
    M1 = {n: args['m_' + n] for n in WEIGHTS}
    M2 = {n: args['v_' + n] for n in WEIGHTS}
    B, S, D = x.shape
    N = B * S
    TM = _pick(N, (256, 128))
    TMH = 128
    TQ = _pick(S, (256, 128))

    shard_shapes = [W[n].shape for n in SHARDED]
    full = {}
    mm_pack = _pack([W[n].astype(BF16) for n in MATMUL_SHARDED], 32)
    Rm = mm_pack.shape[0]
    gathered = gather_shards(mm_pack.reshape(2, Rm // 2, PACK_C)).reshape(4, Rm, PACK_C)
    pieces = [_unpack(gathered[q], [W[n].shape for n in MATMUL_SHARDED]) for q in range(4)]
    for i, n in enumerate(MATMUL_SHARDED):
        full[n] = jnp.concatenate([pieces[q][i] for q in range(4)], axis=SHARD_AXIS[n])
    small = gather_all("gather_small", _pack([W[n] for n in SMALL_SHARDED], 8))
    pieces = [_unpack(small[2 * q], [W[n].shape for n in SMALL_SHARDED]) for q in range(4)]
    for i, n in enumerate(SMALL_SHARDED):
        full[n] = jnp.concatenate([pieces[q][i] for q in range(4)], axis=SHARD_AXIS[n])
    for n in ('rw_w2', 'rw_a2', 'rw_g2'):
        full[n] = full[n].astype(F32)
    for n in REPLICATED:
        full[n] = W[n]

    win_src, win_sgn = _win_layout()
    wuq_src, wuq_sgn = _wuq_layout()
    ones = jnp.asarray(np.kron(np.eye(RWKV_HEADS), np.ones((RWKV_HEAD, RWKV_HEAD))), BF16)
    ct, st = _rope_tables(positions.reshape(N))
    row = lambda v: v.reshape(1, -1)

    H, HD = RWKV_HEADS, RWKV_HEAD
    inst = 2 * B * H
    isplit = LANES // inst
    Ip = HD // isplit
    to_j = lambda nm, c: to_scan(nm, c, B, S, c.shape[1] // RWKV_DIM, None)
    to_i = lambda nm, c: to_scan(nm, c, B, S, 1, Ip)

    def shift_prev(z):
        z = z.reshape(B, S, -1)
        return jnp.pad(z[:, :-1], ((0, 0), (1, 0), (0, 0))).reshape(N, -1)

    def shift_next(z):
        z = z.reshape(B, S, -1)
        return jnp.pad(z[:, 1:], ((0, 0), (0, 1), (0, 0))).reshape(N, -1)

    LW = []
    for l in range(DEPTH):
        wb = full['w_branch'][l]
        wb0 = jnp.zeros((MLA_HEADS, HEAD_PAD, D), F32).at[:, QK_NOPE:].set(wb[0].reshape(MLA_HEADS, V_HEAD, D))
        LW.append(dict(
            attn_g=row(full['attn_norm_g'][l]),
            w_in=_permute_cols(full['w_in'][l], win_src, win_sgn),
            gate_b=row(full['gate_b'][l]),
            q_g=row(full['q_norm_g'][l]),
            w_uq=_permute_cols(full['w_uq'][l], wuq_src, wuq_sgn),
            kv_g=row(full['kv_norm_g'][l]),
            w_ukv=full['w_ukv'][l],
            sg_g=row(full['sg_ln_g'][l]), sg_b=row(full['sg_ln_b'][l]), sg_w=full['sg_w'][l],
            sg_bias=jnp.repeat(full['sg_b'][l].T, SG_DIM // SG_GROUPS, axis=1),
            mu=row(full['rw_mu'][l]), w0=row(full['rw_w0'][l]), w2=_blockdiag(full['rw_w2'][l]),
            a0=row(full['rw_a0'][l]), a2=_blockdiag(full['rw_a2'][l]), g2=full['rw_g2'][l],
            k_k=row(full['rw_k_k'][l]), k_a=row(full['rw_k_a'][l]), r_k=row(full['rw_r_k'][l]),
            ln_g=row(full['rw_ln_g'][l]), ln_b=row(full['rw_ln_b'][l]),
            wb0=wb0.reshape(MLA_HEADS * HEAD_PAD, D), wb1=wb[1], wb2=wb[2],
            w_out=full['w_out'][l], ffn_g=row(full['ffn_norm_g'][l]),
            w_gu=jnp.concatenate([full['w_ffn_gate'][l], full['w_ffn_up'][l]], axis=1),
            w_down=full['w_ffn_down'][l]))

    saved = []
    xc = x.reshape(N, D)
    for l in range(DEPTH):
        p = LW[l]
        t = 'l%d_' % l
        sv = dict(x=xc)
        (h,) = rowwise(t + 'attn_norm', f_rms, [xc], [p['attn_g']], [], [D], TM, [BF16])
        p_q, p_ckv, p_kr, p_sg, z, p_gate = [
            matmul(t + 'in_proj_' + sn, h, p['w_in'][:, a:b], 'nn') for sn, a, b in IN_SEGMENTS]
        sv['h'] = h
        (cq,) = rowwise(t + 'q_norm', f_rms, [p_q], [p['q_g']], [], [Q_LORA], TM, [BF16])
        (ckv,) = rowwise(t + 'kv_norm', f_rms, [p_ckv], [p['kv_g']], [], [KV_LORA], TM, [BF16])
        qq = matmul(t + 'uq', cq, p['w_uq'], 'nn')
        kv = matmul(t + 'ukv', ckv, p['w_ukv'], 'nn')
        qh, kh = rowwise(t + 'rope', f_rope, [qq, kv, p_kr, ct, st], [], [], [MLA_HEADS * HEAD_PAD] * 2, TM)
        ya = attention_fwd(t + 'attn', qh, kh, kv, B, S, TQ, BF16)
        sv.update(p_q=p_q, p_ckv=p_ckv, p_kr=p_kr, cq=cq, ckv=ckv, qq=qq, kv=kv, qh=qh, kh=kh, ya=ya)
        (yb,) = rowwise(t + 'sg', f_sg, [p_sg], [p['sg_g'], p['sg_b'], p['sg_w'], p['sg_bias']], [], [SG_DIM], TM,
                        [BF16])
        sv.update(p_sg=p_sg, yb=yb)
        zp, zn = shift_prev(z), shift_next(z)
        rw_par = [p['mu'], p['w0'], p['w2'], p['a0'], p['a2'], p['g2'], p['k_k'], p['k_a']]
        r_, v_, decay, kdir, kk, bdir, g_ = rowwise(
            t + 'rw_pre', f_rw_pre, [z, zp, zn], rw_par, [ones],
            [RWKV_DIM, RWKV_DIM, 2 * RWKV_DIM, 2 * RWKV_DIM, RWKV_DIM, 2 * RWKV_DIM, RWKV_DIM], TM)
        sc = dict(w=to_j(t + 'lay_w', decay), k=to_j(t + 'lay_k', kdir), b=to_j(t + 'lay_b', bdir),
                  kk=to_j(t + 'lay_kk', kk), r=to_j(t + 'lay_r', r_), v=to_i(t + 'lay_v', v_))
        y_f, y_r, sp, sa_all, s_last = scan_fwd(t + 'scan', sc['w'], sc['k'], sc['b'], sc['kk'], sc['r'], sc['v'])
        ysum = from_scan(t + 'lay_y', y_f, y_r, B, S, 1, True)
        (yc,) = rowwise(t + 'rw_post', f_rw_post, [ysum, r_, v_, kdir, g_], [p['ln_g'], p['ln_b'], p['r_k']],
                        [ones], [RWKV_DIM], TM, [BF16])
        sv.update(z=z, zp=zp, zn=zn, r=r_, v=v_, kdir=kdir, g=g_, sc=sc, sp=sp, sa=sa_all, s_last=s_last,
                  ysum=ysum, yc=yc)
        b0 = matmul(t + 'br0', ya, p['wb0'], 'nn')
        b1 = matmul(t + 'br1', yb, p['wb1'], 'nn')
        b2 = matmul(t + 'br2', yc, p['wb2'], 'nn')
        (merged,) = rowwise(t + 'merge', f_merge, [p_gate, b0, b1, b2], [p['gate_b']], [], [D], TM, [BF16])
        x2 = matmul(t + 'out_proj', merged, p['w_out'], 'nn', add=xc)
        sv.update(p_gate=p_gate, b0=b0, b1=b1, b2=b2, merged=merged, x2=x2)
        (h2,) = rowwise(t + 'ffn_norm', f_rms, [x2], [p['ffn_g']], [], [D], TM, [BF16])
        au = matmul(t + 'ffn_in', h2, p['w_gu'], 'nn')
        (act,) = rowwise(t + 'swiglu', f_swiglu, [au], [], [], [D_FF], TM, [BF16])
        xc = matmul(t + 'ffn_out', act, p['w_down'], 'nn', add=x2)
        sv.update(h2=h2, au=au, act=act)
        saved.append(sv)

    loss_part, dx, d_final_g = loss_head(xc, loss_target.reshape(N, D), row(full['final_norm_g']), TM)
    loss = lax.psum(loss_part[0, 0], ("x", "y", "c"))

    G = {n: [None] * DEPTH for n in WEIGHTS if n != 'final_norm_g'}
    for l in reversed(range(DEPTH)):
        p, sv = LW[l], saved[l]
        t = 'l%d_bwd_' % l
        d_act = matmul(t + 'ffn_out_dx', dx, p['w_down'], 'nt')
        G['w_ffn_down'][l] = matmul(t + 'ffn_out_dw', sv['act'], dx, 'tn')
        (d_au,), _ = rowwise_bwd(t + 'swiglu', f_swiglu, [sv['au']], [], [], [[d_act]], TMH, drow_dtypes=[BF16])
        d_h2 = matmul(t + 'ffn_in_dx', d_au, p['w_gu'], 'nt')
        d_wgu = matmul(t + 'ffn_in_dw', sv['h2'], d_au, 'tn')
        G['w_ffn_gate'][l], G['w_ffn_up'][l] = d_wgu[:, :D_FF], d_wgu[:, D_FF:]
        (dx2,), (dg,) = rowwise_bwd(t + 'ffn_norm', f_rms, [sv['x2']], [p['ffn_g']], [], [[d_h2]], TM, extra=[(0, dx)])
        G['ffn_norm_g'][l] = dg.reshape(-1)
        d_merged = matmul(t + 'out_proj_dx', dx2, p['w_out'], 'nt')
        G['w_out'][l] = matmul(t + 'out_proj_dw', sv['merged'], dx2, 'tn')
        (d_pgate, d_b0, d_b1, d_b2), (d_gate_b,) = rowwise_bwd(
            t + 'merge', f_merge, [sv['p_gate'], sv['b0'], sv['b1'], sv['b2']], [p['gate_b']], [], [[d_merged]], TM,
            drow_dtypes=[BF16] * 4)
        G['gate_b'][l] = d_gate_b.reshape(3, D)
        d_ya = matmul(t + 'br0_dx', d_b0, p['wb0'], 'nt')
        d_yb = matmul(t + 'br1_dx', d_b1, p['wb1'], 'nt')
        d_yc = matmul(t + 'br2_dx', d_b2, p['wb2'], 'nt')
        d_wb0 = matmul(t + 'br0_dw', sv['ya'], d_b0, 'tn').reshape(MLA_HEADS, HEAD_PAD, D)[:, QK_NOPE:].reshape(-1, D)
        G['w_branch'][l] = jnp.stack([d_wb0, matmul(t + 'br1_dw', sv['yb'], d_b1, 'tn'),
                                      matmul(t + 'br2_dw', sv['yc'], d_b2, 'tn')])
        (d_y, d_r1, d_v1, d_kdir1, d_g), (d_ln_g, d_ln_b, d_r_k) = rowwise_bwd(
            t + 'rw_post', f_rw_post, [sv['ysum'], sv['r'], sv['v'], sv['kdir'], sv['g']],
            [p['ln_g'], p['ln_b'], p['r_k']], [ones], [[d_yc]], TMH)
        G['rw_ln_g'][l], G['rw_ln_b'][l] = d_ln_g.reshape(-1), d_ln_b.reshape(-1)
        G['rw_r_k'][l] = d_r_k.reshape(RWKV_HEADS, RWKV_HEAD)
        sc = sv['sc']
        res = scan_bwd(t + 'scan', sc['w'], sc['k'], sc['b'], sc['kk'], sc['r'], sc['v'], sv['sp'], sv['sa'],
                       sv['s_last'], to_i(t + 'lay_dy', d_y))
        s_dw, s_dk, s_db, s_dkk, s_dr, s_dv = [
            from_scan(t + 'lay_' + nm, res[2 * i], res[2 * i + 1], B, S, nd, nm == 'dv')
            for i, (nm, nd) in enumerate((('dw', 2), ('dk', 2), ('db', 2), ('dkk', 1), ('dr', 1), ('dv', 1)))]
        rw_par = [p['mu'], p['w0'], p['w2'], p['a0'], p['a2'], p['g2'], p['k_k'], p['k_a']]
        d_outs = [[d_r1, s_dr], [d_v1, s_dv], [s_dw], [d_kdir1, s_dk], [s_dkk], [s_db], [d_g]]
        (d_z, d_zp, d_zn), d_rw = rowwise_bwd(
            t + 'rw_pre', f_rw_pre, [sv['z'], sv['zp'], sv['zn']], rw_par, [ones], d_outs, TMH)
        (d_prw,) = rowwise(t + 'shift_sum', f_add3, [d_z, shift_next(d_zp), shift_prev(d_zn)], [], [], [RWKV_IN], TM,
                           [BF16])
        G['rw_mu'][l] = d_rw[0].reshape(-1)
        G['rw_w0'][l] = d_rw[1].reshape(2, RWKV_DIM)
        G['rw_w2'][l] = jnp.stack([d_rw[2][:64, :RWKV_DIM], d_rw[2][64:, RWKV_DIM:]])
        G['rw_a0'][l] = d_rw[3].reshape(2, RWKV_DIM)
        G['rw_a2'][l] = jnp.stack([d_rw[4][:64, :RWKV_DIM], d_rw[4][64:, RWKV_DIM:]])
        G['rw_g2'][l] = d_rw[5]
        G['rw_k_k'][l], G['rw_k_a'][l] = d_rw[6].reshape(-1), d_rw[7].reshape(-1)
        (d_psg,), (d_sg_g, d_sg_b, d_sg_w, d_sg_bias) = rowwise_bwd(
            t + 'sg', f_sg, [sv['p_sg']], [p['sg_g'], p['sg_b'], p['sg_w'], p['sg_bias']], [], [[d_yb]], TMH,
            drow_dtypes=[BF16])
        G['sg_ln_g'][l], G['sg_ln_b'][l], G['sg_w'][l] = d_sg_g.reshape(-1), d_sg_b.reshape(-1), d_sg_w
        G['sg_b'][l] = d_sg_bias.reshape(SG_CHUNK, SG_GROUPS, SG_DIM // SG_GROUPS).sum(-1).T
        d_qh, d_kh, d_kvv = attention_bwd(t + 'attn', sv['qh'], sv['kh'], sv['kv'], d_ya, B, S, TQ)
        (d_qq, d_kv, d_pkr), _ = rowwise_bwd(
            t + 'rope', f_rope, [sv['qq'], sv['kv'], sv['p_kr'], ct, st], [], [], [[d_qh], [d_kh]], TM,
            n_row_diff=3, extra=[(1, d_kvv)], drow_dtypes=[BF16] * 3)
        d_cq = matmul(t + 'uq_dx', d_qq, p['w_uq'], 'nt')
        d_wuq = matmul(t + 'uq_dw', sv['cq'], d_qq, 'tn')
        g1, g2_ = _unpermute_full(d_wuq, wuq_src, wuq_sgn, MLA_HEADS * (QK_NOPE + QK_ROPE))
        G['w_uq'][l] = g1 + g2_
        d_ckv = matmul(t + 'ukv_dx', d_kv, p['w_ukv'], 'nt')
        G['w_ukv'][l] = matmul(t + 'ukv_dw', sv['ckv'], d_kv, 'tn')
        (d_pq,), (dg,) = rowwise_bwd(t + 'q_norm', f_rms, [sv['p_q']], [p['q_g']], [], [[d_cq]], TM,
                                     drow_dtypes=[BF16])
        G['q_norm_g'][l] = dg.reshape(-1)
        (d_pckv,), (dg,) = rowwise_bwd(t + 'kv_norm', f_rms, [sv['p_ckv']], [p['kv_g']], [], [[d_ckv]], TM,
                                       drow_dtypes=[BF16])
        G['kv_norm_g'][l] = dg.reshape(-1)
        d_h, d_cols = None, []
        for (sn, a, b), d_seg in zip(IN_SEGMENTS, [d_pq, d_pckv, d_pkr, d_psg, d_prw, d_pgate]):
            d_h = matmul(t + 'in_proj_dx_' + sn, d_seg, p['w_in'][:, a:b], 'nt', add=d_h)
            d_cols.append(matmul(t + 'in_proj_dw_' + sn, sv['h'], d_seg, 'tn'))
        d_win = jnp.concatenate(d_cols, axis=1)
        g1, g2_ = _unpermute_full(d_win, win_src, win_sgn, N_IN)
        kr0 = Q_LORA + KV_LORA
        G['w_in'][l] = g1.at[:, kr0:kr0 + QK_ROPE].add(g2_[:, kr0:kr0 + QK_ROPE])
        (dx,), (dg,) = rowwise_bwd(t + 'attn_norm', f_rms, [sv['x']], [p['attn_g']], [], [[d_h]], TM, extra=[(0, dx2)])
        G['attn_norm_g'][l] = dg.reshape(-1)

    grads = {n: jnp.stack(G[n]) for n in G}
    grads['final_norm_g'] = d_final_g.reshape(-1)
    grad_x = dx.reshape(B, S, D)

    per_shard = []
    for q in range(4):
        sl = []
        for n in SHARDED:
            ax = SHARD_AXIS[n]
            w = W[n].shape[ax]
            sl.append(lax.slice_in_dim(grads[n], q * w, (q + 1) * w, axis=ax))
        per_shard.append(_pack(sl, 2 * 8 * JOIN_CHUNKS))
    R = per_shard[0].shape[0]
    gpack = jnp.stack(per_shard).astype(BF16).reshape(4, 2, R // 2, PACK_C)
    half_sum = sum_slots("sum_sharded", scatter_partials(gpack))
    g_shard = dict(zip(SHARDED, _unpack(sibling_join(half_sum).reshape(R, PACK_C), shard_shapes)))
    rep_shapes = [W[n].shape for n in REPLICATED]
    rpack = _pack([grads[n] for n in REPLICATED], 8)
    g_rep = sum_slots("sum_replicated", gather_all("gather_replicated", rpack))

    outs = {}
    for n in MATMUL_SHARDED:
        shp = W[n].shape
        two = lambda a: a.reshape(-1, shp[-1])
        res = adamw("adamw_" + n, two(W[n]), two(g_shard[n]), two(M1[n]), two(M2[n]))
        outs['grad', n] = g_shard[n]
        for key, a in zip(('delta', 'new_m', 'new_v'), res):
            outs[key, n] = a.reshape(shp)
    small = SMALL_SHARDED + REPLICATED
    small_shapes = [W[n].shape for n in small]
    g_small = [g_shard[n] for n in SMALL_SHARDED] + _unpack(g_rep, rep_shapes)
    res = adamw("adamw_small", _pack([W[n] for n in small], 8), _pack(g_small, 8),
                _pack([M1[n] for n in small], 8), _pack([M2[n] for n in small], 8))
    for n, a in zip(small, g_small):
        outs['grad', n] = a
    for key, buf in zip(('delta', 'new_m', 'new_v'), res):
        for n, a in zip(small, _unpack(buf, small_shapes)):
            outs[key, n] = a
    return (loss, grad_x, *[outs['grad', n] for n in WEIGHTS], *[outs['delta', n] for n in WEIGHTS],
            *[outs['new_m', n] for n in WEIGHTS], *[outs['new_v', n] for n in WEIGHTS])
```

```python
import functools
import math

import numpy as np
import jax
import jax.numpy as jnp
from jax import lax
from jax.experimental import pallas as pl
from jax.experimental.pallas import tpu as pltpu

F32 = jnp.float32
BF16 = jnp.bfloat16

DEPTH = 2
MLA_HEADS = 8
Q_LORA = 384
KV_LORA = 256
QK_NOPE = 64
QK_ROPE = 32
V_HEAD = 64
ROPE_THETA = 10000.0
SG_GROUPS = 8
SG_DIM = 512
SG_CHUNK = 128
RWKV_HEADS = 8
RWKV_HEAD = 64
RWKV_DIM = 512
GN_EPS = 64e-5
NORM_EPS = 1e-6
D_FF = 2816
RWKV_IN = 1920
N_IN = 6688
ADAM_LR, ADAM_B1, ADAM_B2, ADAM_EPS, ADAM_WD, ADAM_STEP = 0.001, 0.9, 0.999, 1e-08, 0.01, 10

LANES = 128
HEAD_PAD = 128
VMEM_LIMIT = 56 * 1024 * 1024
MESH = pl.DeviceIdType.MESH

WEIGHTS = ['attn_norm_g', 'w_in', 'gate_b', 'q_norm_g', 'w_uq', 'kv_norm_g', 'w_ukv', 'sg_ln_g', 'sg_ln_b', 'sg_w',
           'sg_b', 'rw_mu', 'rw_w0', 'rw_w2', 'rw_a0', 'rw_a2', 'rw_g2', 'rw_k_k', 'rw_k_a', 'rw_r_k', 'rw_ln_g',
           'rw_ln_b', 'w_branch', 'w_out', 'ffn_norm_g', 'w_ffn_gate', 'w_ffn_up', 'w_ffn_down', 'final_norm_g']
SHARD_AXIS = {'w_in': 2, 'gate_b': 2, 'w_uq': 2, 'w_ukv': 2, 'rw_w0': 2, 'rw_w2': 3, 'rw_a0': 2, 'rw_a2': 3,
              'rw_g2': 2, 'w_branch': 3, 'w_out': 1, 'w_ffn_gate': 2, 'w_ffn_up': 2, 'w_ffn_down': 1}
SHARDED = [n for n in WEIGHTS if n in SHARD_AXIS]
REPLICATED = [n for n in WEIGHTS if n not in SHARD_AXIS]
SMALL_SHARDED = ['gate_b', 'rw_w0', 'rw_a0']
MATMUL_SHARDED = [n for n in SHARDED if n not in SMALL_SHARDED]


def _params(sem=None):
    return pltpu.CompilerParams(dimension_semantics=sem, vmem_limit_bytes=VMEM_LIMIT)


def _pick(n, cands):
    for c in cands:
        if n % c == 0:
            return c
    return n


def _dot(a, b, dims):
    return lax.dot_general(a.astype(BF16), b.astype(BF16), (dims, ((), ())), preferred_element_type=F32)


def _nn(a, b):
    return _dot(a, b, ((1,), (0,)))


def _nt(a, b):
    return _dot(a, b, ((1,), (1,)))


def _tn(a, b):
    return _dot(a, b, ((0,), (0,)))


@jax.custom_vjp
def mm(a, b):
    return _nn(a, b)


mm.defvjp(lambda a, b: (_nn(a, b), (a, b)), lambda res, g: (_nt(g, res[1]), _tn(res[0], g)))


@jax.custom_vjp
def mm_nt(a, b):
    return _nt(a, b)


mm_nt.defvjp(lambda a, b: (_nt(a, b), (a, b)), lambda res, g: (_nn(g, res[1]), _tn(g, res[0])))


def _seg_raw(x, ones):
    hi = x.astype(BF16)
    lo = (x - hi.astype(F32)).astype(BF16)
    d = (((1,), (0,)), ((), ()))
    return (lax.dot_general(hi, ones, d, preferred_element_type=F32)
            + lax.dot_general(lo, ones, d, preferred_element_type=F32))


@jax.custom_vjp
def segsum(x, ones):
    return _seg_raw(x, ones)


segsum.defvjp(lambda x, ones: (_seg_raw(x, ones), ones),
              lambda ones, g: (_seg_raw(g, ones), jnp.zeros_like(ones)))


def _sigmoid(x):
    return 0.5 * (jnp.tanh(0.5 * x) + 1.0)


def _rms(x, g):
    return x * lax.rsqrt(jnp.mean(x * x, axis=-1, keepdims=True) + NORM_EPS) * g


def matmul(name, a, b, mode, add=None, out_dtype=F32):
    if mode == 'nn':
        (M, K), (_, N) = a.shape, b.shape
    elif mode == 'nt':
        (M, K), (N, _) = a.shape, b.shape
    else:
        (K, M), (_, N) = a.shape, b.shape
    tm = _pick(M, (1408, 1024, 512, 384, 256, 128))
    tn = _pick(N, (1408, 1024, 768, 512, 384, 256, 128))
    tk = _pick(K, (512, 384, 256, 128))
    nk = K // tk
    dims = {'nn': ((1,), (0,)), 'nt': ((1,), (1,)), 'tn': ((0,), (0,))}[mode]
    a_spec = pl.BlockSpec((tk, tm), lambda i, j, k: (k, i)) if mode == 'tn' else pl.BlockSpec((tm, tk), lambda i, j, k: (i, k))
    b_spec = pl.BlockSpec((tn, tk), lambda i, j, k: (j, k)) if mode == 'nt' else pl.BlockSpec((tk, tn), lambda i, j, k: (k, j))
    o_spec = pl.BlockSpec((tm, tn), lambda i, j, k: (i, j))
    has_add = add is not None

    def body(*refs):
        if has_add:
            a_ref, b_ref, add_ref, o_ref, acc = refs
        else:
            a_ref, b_ref, o_ref, acc = refs
        k = pl.program_id(2)

        @pl.when(k == 0)
        def _():
            acc[...] = jnp.zeros_like(acc)

        acc[...] += _dot(a_ref[...], b_ref[...], dims)

        @pl.when(k == nk - 1)
        def _():
            o_ref[...] = (acc[...] + add_ref[...] if has_add else acc[...]).astype(o_ref.dtype)

    ins = [a, b] + ([add] if has_add else [])
    specs = [a_spec, b_spec] + ([o_spec] if has_add else [])
    return pl.pallas_call(
        body, name=name, grid=(M // tm, N // tn, nk), in_specs=specs, out_specs=o_spec,
        out_shape=jax.ShapeDtypeStruct((M, N), out_dtype), scratch_shapes=[pltpu.VMEM((tm, tn), F32)],
        compiler_params=_params(("parallel", "parallel", "arbitrary")))(*ins)


def _full_spec(p):
    nd = p.ndim
    return pl.BlockSpec(p.shape, lambda i, _nd=nd: (0,) * _nd)


def rowwise(name, fn, rows, params, consts, out_widths, tm, out_dtypes=None):
    N = rows[0].shape[0]
    nr, npar, nc = len(rows), len(params), len(consts)

    def body(*refs):
        vals = [r[...] for r in refs[:nr + npar + nc]]
        res = fn(*vals)
        for o, v in zip(refs[nr + npar + nc:], res):
            o[...] = v.astype(o.dtype)

    in_specs = ([pl.BlockSpec((tm, r.shape[1]), lambda i: (i, 0)) for r in rows]
                + [_full_spec(p) for p in list(params) + list(consts)])
    out_specs = [pl.BlockSpec((tm, w), lambda i: (i, 0)) for w in out_widths]
    return pl.pallas_call(
        body, name=name, grid=(N // tm,), in_specs=in_specs, out_specs=out_specs,
        out_shape=[jax.ShapeDtypeStruct((N, w), d) for w, d in zip(out_widths, out_dtypes or [F32] * len(out_widths))],
        compiler_params=_params(("parallel",)))(*rows, *params, *consts)


def rowwise_bwd(name, fn, rows, params, consts, d_outs, tm, n_row_diff=None, extra=(), drow_dtypes=None):
    N = rows[0].shape[0]
    nr, npar, nc = len(rows), len(params), len(consts)
    nd = nr if n_row_diff is None else n_row_diff
    counts = [len(p) for p in d_outs]
    flat_d = [a for parts in d_outs for a in parts]
    nflat, nex = len(flat_d), len(extra)

    def body(*refs):
        pos = 0
        row_v = [r[...] for r in refs[pos:pos + nr]]; pos += nr
        par_v = [r[...] for r in refs[pos:pos + npar]]; pos += npar
        con_v = [r[...] for r in refs[pos:pos + nc]]; pos += nc
        d_refs = refs[pos:pos + nflat]; pos += nflat
        ex_refs = refs[pos:pos + nex]; pos += nex
        drow_refs = refs[pos:pos + nd]; pos += nd
        dpar_refs = refs[pos:pos + npar]

        def f(*diff):
            return fn(*diff[:nd], *row_v[nd:], *diff[nd:], *con_v)

        _, vjp = jax.vjp(f, *row_v[:nd], *par_v)
        cts, q = [], 0
        for c in counts:
            g = d_refs[q][...].astype(F32)
            for t in range(1, c):
                g = g + d_refs[q + t][...].astype(F32)
            cts.append(g)
            q += c
        grads = vjp(tuple(cts))
        drow = list(grads[:nd])
        for (idx, _), r in zip(extra, ex_refs):
            drow[idx] = drow[idx] + r[...].astype(F32)
        for o, v in zip(drow_refs, drow):
            o[...] = v.astype(o.dtype)

        @pl.when(pl.program_id(0) == 0)
        def _():
            for o in dpar_refs:
                o[...] = jnp.zeros_like(o)

        for o, v in zip(dpar_refs, grads[nd:]):
            o[...] += v

    ex_arrs = [a for _, a in extra]
    in_specs = ([pl.BlockSpec((tm, r.shape[1]), lambda i: (i, 0)) for r in rows]
                + [_full_spec(p) for p in list(params) + list(consts)]
                + [pl.BlockSpec((tm, a.shape[1]), lambda i: (i, 0)) for a in flat_d + ex_arrs])
    out_specs = ([pl.BlockSpec((tm, r.shape[1]), lambda i: (i, 0)) for r in rows[:nd]]
                 + [_full_spec(p) for p in params])
    out_shape = ([jax.ShapeDtypeStruct(r.shape, d) for r, d in zip(rows[:nd], drow_dtypes or [F32] * nd)]
                 + [jax.ShapeDtypeStruct(p.shape, F32) for p in params])
    res = pl.pallas_call(
        body, name=name, grid=(N // tm,), in_specs=in_specs, out_specs=out_specs, out_shape=out_shape,
        compiler_params=_params(("arbitrary",)))(*rows, *params, *consts, *flat_d, *ex_arrs)
    return list(res[:nd]), list(res[nd:])


def f_rms(x, g):
    return (_rms(x, g),)


def f_rope(qq, kv, krr, ct, st):
    hw = MLA_HEADS * HEAD_PAD
    c8 = jnp.tile(ct, (1, MLA_HEADS))
    s8 = jnp.tile(st, (1, MLA_HEADS))
    q = qq[:, :hw] * c8 + qq[:, hw:] * s8
    kr = krr[:, :HEAD_PAD] * ct + krr[:, HEAD_PAD:] * st
    lane = lax.broadcasted_iota(jnp.int32, kv.shape, 1) % HEAD_PAD
    k = jnp.where(lane < QK_NOPE, kv, jnp.tile(kr, (1, MLA_HEADS)))
    return q, k


def f_sg(p, ln_g, ln_b, w, bias):
    z = 0.5 * p * (1.0 + jnp.tanh(0.7978845608028654 * (p + 0.044715 * p * p * p)))
    u, v = z[:, :SG_DIM], z[:, SG_DIM:]
    mu = jnp.mean(v, axis=-1, keepdims=True)
    var = jnp.mean(jnp.square(v - mu), axis=-1, keepdims=True)
    v = (v - mu) * lax.rsqrt(var + 1e-5) * ln_g + ln_b
    lane = lax.broadcasted_iota(jnp.int32, (SG_CHUNK, LANES), 1)
    outs = []
    for c in range(p.shape[0] // SG_CHUNK):
        vc = v[c * SG_CHUNK:(c + 1) * SG_CHUNK]
        cols = []
        for m in range(SG_DIM // LANES):
            blk = vc[:, m * LANES:(m + 1) * LANES]
            cols.append(jnp.where(lane < 64, mm(w[2 * m], blk), mm(w[2 * m + 1], blk)))
        outs.append(jnp.concatenate(cols, axis=1) + bias)
    mixed = outs[0] if len(outs) == 1 else jnp.concatenate(outs, axis=0)
    return (u * mixed,)


def f_rw_pre(z, zp, zn, mu, w0, w2, a0, a2, g2, k_k, k_a, ones):
    z = z + mu * (0.5 * (zp + zn) - z)
    C = RWKV_DIM
    r, k, v = z[:, :C], z[:, C:2 * C], z[:, 2 * C:3 * C]
    wl, al, gl = z[:, 3 * C:3 * C + 128], z[:, 3 * C + 128:3 * C + 256], z[:, 3 * C + 256:]
    w = w0 + mm(jnp.tanh(wl), w2)
    decay = jnp.exp(-0.6065306597126334 * _sigmoid(w))
    a = _sigmoid(a0 + mm(al, a2))
    g = mm(_sigmoid(gl), g2)
    kk = k * k_k
    kk = kk / jnp.maximum(jnp.sqrt(segsum(kk * kk, ones)), 1e-12)
    k2 = jnp.concatenate([k, k], axis=1)
    kdir = k2 * (1.0 + (a - 1.0) * jnp.concatenate([k_a, k_a], axis=1))
    bdir = jnp.concatenate([kk, kk], axis=1) * a
    return r, v, decay, kdir, kk, bdir, g


def f_rw_post(y, r, v, kdir, g, ln_g, ln_b, r_k, ones):
    mean = segsum(y, ones) * (1.0 / RWKV_HEAD)
    yc = y - mean
    var = segsum(yc * yc, ones) * (1.0 / RWKV_HEAD)
    y = yc * lax.rsqrt(var + GN_EPS) * ln_g + ln_b
    C = RWKV_DIM
    bonus = segsum(r * kdir[:, :C] * r_k, ones) + segsum(r * kdir[:, C:] * r_k, ones)
    return ((y + bonus * v) * g,)


def f_merge(pg, b0, b1, b2, gate_b):
    D = b0.shape[1]
    gt = _sigmoid(pg + gate_b)
    return (gt[:, :D] * b0 + gt[:, D:2 * D] * b1 + gt[:, 2 * D:] * b2,)


def f_swiglu(au):
    a, u = au[:, :D_FF], au[:, D_FF:]
    return (a * _sigmoid(a) * u,)


def f_add3(a, b, c):
    return (a + b + c,)


def loss_head(x, tgt, g, tm):
    N, D = x.shape

    def body(x_ref, t_ref, g_ref, loss_ref, dx_ref, dg_ref):
        t = t_ref[...]

        def f(xv, gv):
            err = _rms(xv, gv) - t
            return 0.5 * jnp.sum(jnp.mean(err * err, axis=-1, keepdims=True))

        val, (dx, dg) = jax.value_and_grad(f, argnums=(0, 1))(x_ref[...], g_ref[...])
        dx_ref[...] = dx

        @pl.when(pl.program_id(0) == 0)
        def _():
            loss_ref[...] = jnp.zeros_like(loss_ref)
            dg_ref[...] = jnp.zeros_like(dg_ref)

        loss_ref[...] += jnp.full(loss_ref.shape, val, F32)
        dg_ref[...] += dg

    row = pl.BlockSpec((tm, D), lambda i: (i, 0))
    return pl.pallas_call(
        body, name="loss_head", grid=(N // tm,), in_specs=[row, row, _full_spec(g)],
        out_specs=[pl.BlockSpec((1, LANES), lambda i: (0, 0)), row, _full_spec(g)],
        out_shape=[jax.ShapeDtypeStruct((1, LANES), F32), jax.ShapeDtypeStruct((N, D), F32),
                   jax.ShapeDtypeStruct(g.shape, F32)],
        compiler_params=_params(("arbitrary",)))(x, tgt, g)


ATT_SCALE = float((QK_NOPE + QK_ROPE) ** -0.5)


def _attn_block(q, k, kv):
    s = mm_nt(q, k) * ATT_SCALE
    m = lax.stop_gradient(jnp.max(s, axis=-1, keepdims=True))
    e = jnp.exp(s - m)
    return mm(e, kv) * (1.0 / jnp.sum(e, axis=-1, keepdims=True))


def attention_fwd(name, q, k, kv, B, S, tq, out_dtype):
    nq = S // tq
    qspec = pl.BlockSpec((tq, HEAD_PAD), lambda b, h, i: (b * nq + i, h))
    kspec = pl.BlockSpec((S, HEAD_PAD), lambda b, h, i: (b, h))

    def body(q_ref, k_ref, kv_ref, o_ref):
        o_ref[...] = _attn_block(q_ref[...], k_ref[...], kv_ref[...]).astype(o_ref.dtype)

    return pl.pallas_call(
        body, name=name, grid=(B, MLA_HEADS, nq), in_specs=[qspec, kspec, kspec], out_specs=qspec,
        out_shape=jax.ShapeDtypeStruct(q.shape, out_dtype),
        compiler_params=_params(("parallel", "parallel", "arbitrary")))(q, k, kv)


def attention_bwd(name, q, k, kv, do, B, S, tq):
    nq = S // tq
    qspec = pl.BlockSpec((tq, HEAD_PAD), lambda b, h, i: (b * nq + i, h))
    kspec = pl.BlockSpec((S, HEAD_PAD), lambda b, h, i: (b, h))

    def body(q_ref, k_ref, kv_ref, do_ref, dq_ref, dk_ref, dkv_ref):
        _, vjp = jax.vjp(_attn_block, q_ref[...], k_ref[...], kv_ref[...])
        dq, dk, dkv = vjp(do_ref[...])
        dq_ref[...] = dq

        @pl.when(pl.program_id(2) == 0)
        def _():
            dk_ref[...] = jnp.zeros_like(dk_ref)
            dkv_ref[...] = jnp.zeros_like(dkv_ref)

        dk_ref[...] += dk
        dkv_ref[...] += dkv

    sh = jax.ShapeDtypeStruct(q.shape, F32)
    return pl.pallas_call(
        body, name=name, grid=(B, MLA_HEADS, nq), in_specs=[qspec, kspec, kspec, qspec],
        out_specs=[qspec, kspec, kspec], out_shape=[sh, sh, sh],
        compiler_params=_params(("parallel", "parallel", "arbitrary")))(q, k, kv, do)


SCAN_TC = 8
SCAN_UNROLL = 16


def _jloop(n, body, init):
    def outer(o, c):
        for u in range(SCAN_UNROLL):
            c = body(o * SCAN_UNROLL + u, c)
        return c

    return lax.fori_loop(0, n // SCAN_UNROLL, outer, init)


def _dir_mask(L, Ip):
    per_dir = L // (2 * (RWKV_HEAD // Ip))
    lane = lax.broadcasted_iota(jnp.int32, (1, L), 1)
    return (lane // per_dir) % 2 == 1


def _merge_dirs(mask, fwd_ref, rev_ref, out_ref):
    for tt in range(SCAN_TC):
        out_ref[tt] = jnp.where(mask, rev_ref[SCAN_TC - 1 - tt], fwd_ref[tt])


def scan_fwd(name, w, k, b, kk, r, v):
    T, J, L = w.shape
    Ip = v.shape[1]
    nT = T // SCAN_TC
    fwd3, rev3 = (lambda g: (g, 0, 0)), (lambda g: (nT - 1 - g, 0, 0))
    jf, jr = pl.BlockSpec((SCAN_TC, J, L), fwd3), pl.BlockSpec((SCAN_TC, J, L), rev3)
    i_f, i_r = pl.BlockSpec((SCAN_TC, Ip, L), fwd3), pl.BlockSpec((SCAN_TC, Ip, L), rev3)
    sspec = pl.BlockSpec((SCAN_TC, J, Ip, L), lambda g: (g, 0, 0, 0))
    last_spec = pl.BlockSpec((J, Ip, L), lambda g: (0, 0, 0))

    def body(wf, wr, kf, kr, bf, br, kkf, kkr, rf, rr, vf, vr, yf_ref, yr_ref, sp_ref, sa_ref, last_ref,
             s_ref, w_ref, k_ref, b_ref, kk_ref, r_ref, v_ref):
        @pl.when(pl.program_id(0) == 0)
        def _():
            s_ref[...] = jnp.zeros_like(s_ref)

        mask = _dir_mask(L, Ip)
        for f_, r_, m_ in ((wf, wr, w_ref), (kf, kr, k_ref), (bf, br, b_ref), (kkf, kkr, kk_ref), (rf, rr, r_ref),
                           (vf, vr, v_ref)):
            _merge_dirs(mask, f_, r_, m_)

        def row(ref, tt, j):
            return jnp.broadcast_to(ref[tt, pl.ds(j, 1), :], (Ip, L))

        def step(tt, carry):
            def p1(j, sa):
                s = s_ref[j]
                sp_ref[tt, j] = s
                return sa + s * row(kk_ref, tt, j)

            sa = _jloop(J, p1, jnp.zeros((Ip, L), F32))
            sa_ref[tt] = sa
            vt = v_ref[tt]

            def p2(j, y):
                s = s_ref[j] * row(w_ref, tt, j) - sa * row(b_ref, tt, j) + vt * row(k_ref, tt, j)
                s_ref[j] = s
                return y + s * row(r_ref, tt, j)

            y = _jloop(J, p2, jnp.zeros((Ip, L), F32))
            yf_ref[tt] = y
            yr_ref[SCAN_TC - 1 - tt] = y
            return carry

        lax.fori_loop(0, SCAN_TC, step, 0)

        @pl.when(pl.program_id(0) == nT - 1)
        def _():
            last_ref[...] = s_ref[...]

    ish = jax.ShapeDtypeStruct((T, Ip, L), F32)
    jscr = pltpu.VMEM((SCAN_TC, J, L), F32)
    return pl.pallas_call(
        body, name=name, grid=(nT,), in_specs=[jf, jr] * 5 + [i_f, i_r],
        out_specs=[i_f, i_r, sspec, i_f, last_spec],
        out_shape=[ish, ish, jax.ShapeDtypeStruct((T, J, Ip, L), F32), ish, jax.ShapeDtypeStruct((J, Ip, L), F32)],
        scratch_shapes=[pltpu.VMEM((J, Ip, L), F32)] + [jscr] * 5 + [pltpu.VMEM((SCAN_TC, Ip, L), F32)],
        compiler_params=_params(("arbitrary",)))(w, w, k, k, b, b, kk, kk, r, r, v, v)


def scan_bwd(name, w, k, b, kk, r, v, sp, sa_all, s_last, dy):
    T, J, L = w.shape
    Ip = v.shape[1]
    nT = T // SCAN_TC
    stp3, mir3 = (lambda g: (nT - 1 - g, 0, 0)), (lambda g: (g, 0, 0))
    jf, jr = pl.BlockSpec((SCAN_TC, J, L), stp3), pl.BlockSpec((SCAN_TC, J, L), mir3)
    i_f, i_r = pl.BlockSpec((SCAN_TC, Ip, L), stp3), pl.BlockSpec((SCAN_TC, Ip, L), mir3)
    sspec = pl.BlockSpec((SCAN_TC, J, Ip, L), lambda g: (nT - 1 - g, 0, 0, 0))
    last_spec = pl.BlockSpec((J, Ip, L), lambda g: (0, 0, 0))

    def body(wf, wr, kf, kr, bf, br, kkf, kkr, rf, rr, vf, vr, dyf, dyr, sp_ref, sa_ref, last_ref,
             dwf, dwr, dkf, dkr, dbf, dbr, dkkf, dkkr, drf, drr, dvf, dvr,
             ds_ref, nxt_ref, w_ref, k_ref, b_ref, kk_ref, r_ref, v_ref, dy_ref):
        @pl.when(pl.program_id(0) == 0)
        def _():
            ds_ref[...] = jnp.zeros_like(ds_ref)
            nxt_ref[...] = last_ref[...]

        mask = _dir_mask(L, Ip)
        for f_, r_, m_ in ((wf, wr, w_ref), (kf, kr, k_ref), (bf, br, b_ref), (kkf, kkr, kk_ref), (rf, rr, r_ref),
                           (vf, vr, v_ref), (dyf, dyr, dy_ref)):
            _merge_dirs(mask, f_, r_, m_)

        def row(ref, tt, j):
            return jnp.broadcast_to(ref[tt, pl.ds(j, 1), :], (Ip, L))

        def rsum(x):
            return jnp.sum(x, axis=0, keepdims=True)

        def make_step(first):
            def step(n, carry):
                tt = SCAN_TC - 1 - n
                dyt, vt, sa = dy_ref[tt], v_ref[tt], sa_ref[tt]

                def p1(j, c):
                    dsa, dv = c
                    ds = ds_ref[j] + dyt * row(r_ref, tt, j)
                    ds_ref[j] = ds
                    return dsa - ds * row(b_ref, tt, j), dv + ds * row(k_ref, tt, j)

                z = jnp.zeros((Ip, L), F32)
                dsa, dv = _jloop(J, p1, (z, z))
                dvf[tt] = dv
                dvr[SCAN_TC - 1 - tt] = dv

                def put(f_ref, r_ref_, j, val):
                    f_ref[tt, pl.ds(j, 1), :] = val
                    r_ref_[SCAN_TC - 1 - tt, pl.ds(j, 1), :] = val

                def p2(j, c):
                    ds = ds_ref[j]
                    s0 = sp_ref[tt, j]
                    s1 = nxt_ref[j] if first else sp_ref[tt + 1, j]
                    put(drf, drr, j, rsum(s1 * dyt))
                    put(dkf, dkr, j, rsum(ds * vt))
                    put(dbf, dbr, j, -rsum(ds * sa))
                    put(dwf, dwr, j, rsum(ds * s0))
                    put(dkkf, dkkr, j, rsum(s0 * dsa))
                    ds_ref[j] = ds * row(w_ref, tt, j) + dsa * row(kk_ref, tt, j)
                    return c

                _jloop(J, p2, 0)
                return carry

            return step

        make_step(True)(0, 0)
        lax.fori_loop(1, SCAN_TC, make_step(False), 0)
        nxt_ref[...] = sp_ref[0]

    jsh = jax.ShapeDtypeStruct((T, J, L), F32)
    ish = jax.ShapeDtypeStruct((T, Ip, L), F32)
    jscr = pltpu.VMEM((SCAN_TC, J, L), F32)
    iscr = pltpu.VMEM((SCAN_TC, Ip, L), F32)
    return pl.pallas_call(
        body, name=name, grid=(nT,), in_specs=[jf, jr] * 5 + [i_f, i_r] * 2 + [sspec, i_f, last_spec],
        out_specs=[jf, jr] * 5 + [i_f, i_r], out_shape=[jsh] * 10 + [ish] * 2,
        scratch_shapes=[pltpu.VMEM((J, Ip, L), F32)] * 2 + [jscr] * 5 + [iscr] * 2,
        compiler_params=_params(("arbitrary",)))(w, w, k, k, b, b, kk, kk, r, r, v, v, dy, dy, sp, sa_all, s_last)


LAYOUT_TT = 32
LAYOUT_U = 8


def _lane_group(L, n):
    return lax.broadcasted_iota(jnp.int32, (1, L), 1) // (L // n)


def to_scan(name, x, B, S, nd, Ip):
    isplit = LANES // (2 * B * RWKV_HEADS)
    tt_n = _pick(S, (LAYOUT_TT, 16, 8))
    x5 = x.reshape(B, S, nd, RWKV_HEADS, RWKV_HEAD)
    rows_out = RWKV_HEAD if Ip is None else Ip

    def body(x_ref, o_ref):
        group = _lane_group(LANES, isplit)

        def chunk(c, carry):
            t0 = c * LAYOUT_U
            z = jnp.concatenate([x_ref[b, t0 + u, min(d, nd - 1)] for u in range(LAYOUT_U) for _ in range(isplit)
                                 for d in range(2) for b in range(B)], axis=0)
            mt = z.T
            for u in range(LAYOUT_U):
                m = mt[:, u * LANES:(u + 1) * LANES]
                if Ip is not None:
                    m = sum(jnp.where(group == i2, m[i2 * Ip:(i2 + 1) * Ip], 0.0) for i2 in range(isplit))
                o_ref[t0 + u] = m
            return carry

        lax.fori_loop(0, tt_n // LAYOUT_U, chunk, 0)

    return pl.pallas_call(
        body, name=name, grid=(S // tt_n,),
        in_specs=[pl.BlockSpec((B, tt_n, nd, RWKV_HEADS, RWKV_HEAD), lambda g: (0, g, 0, 0, 0))],
        out_specs=pl.BlockSpec((tt_n, rows_out, LANES), lambda g: (g, 0, 0)),
        out_shape=jax.ShapeDtypeStruct((S, rows_out, LANES), F32), compiler_params=_params(("parallel",)))(x5)


def from_scan(name, f, r, B, S, nd, i_indexed):
    rows_in = f.shape[1]
    isplit = LANES // (2 * B * RWKV_HEADS)
    Ip = rows_in if i_indexed else RWKV_HEAD // isplit
    tt_n = _pick(S, (LAYOUT_TT, 16, 8))
    per_i2 = LANES // isplit

    def body(f_ref, r_ref, o_ref):
        mask = _dir_mask(LANES, Ip)
        group = _lane_group(LANES, isplit)

        def chunk(c, carry):
            t0 = c * LAYOUT_U
            ms = []
            for u in range(LAYOUT_U):
                m = jnp.where(mask, r_ref[t0 + u], f_ref[t0 + u])
                if i_indexed:
                    m = jnp.concatenate([jnp.where(group == i2, m, 0.0) for i2 in range(isplit)], axis=0)
                ms.append(m)
            zt = jnp.concatenate(ms, axis=1).T
            for u in range(LAYOUT_U):
                z = zt[u * LANES:(u + 1) * LANES]
                zf = sum(z[i2 * per_i2:(i2 + 1) * per_i2] for i2 in range(isplit))
                for b in range(B):
                    d0 = zf[b * RWKV_HEADS:(b + 1) * RWKV_HEADS]
                    d1 = zf[(B + b) * RWKV_HEADS:(B + b + 1) * RWKV_HEADS]
                    if nd == 1:
                        o_ref[b, t0 + u, 0] = d0 + d1
                    else:
                        o_ref[b, t0 + u, 0] = d0
                        o_ref[b, t0 + u, 1] = d1
            return carry

        lax.fori_loop(0, tt_n // LAYOUT_U, chunk, 0)

    spec = pl.BlockSpec((tt_n, rows_in, LANES), lambda g: (g, 0, 0))
    out = pl.pallas_call(
        body, name=name, grid=(S // tt_n,), in_specs=[spec, spec],
        out_specs=pl.BlockSpec((B, tt_n, nd, RWKV_HEADS, RWKV_HEAD), lambda g: (0, g, 0, 0, 0)),
        out_shape=jax.ShapeDtypeStruct((B, S, nd, RWKV_HEADS, RWKV_HEAD), F32),
        compiler_params=_params(("parallel",)))(f, r)
    return out.reshape(B * S, nd * RWKV_DIM)


def adamw(name, w, g, m, v):
    R, C = w.shape
    tr = _pick(R, (256, 128, 64, 32, 16, 8))
    c1 = 1.0 - ADAM_B1 ** ADAM_STEP
    c2 = 1.0 - ADAM_B2 ** ADAM_STEP

    def body(w_ref, g_ref, m_ref, v_ref, d_ref, nm_ref, nv_ref):
        gv = g_ref[...]
        nm = ADAM_B1 * m_ref[...] + (1.0 - ADAM_B1) * gv
        nv = ADAM_B2 * v_ref[...] + (1.0 - ADAM_B2) * jnp.square(gv)
        d_ref[...] = -ADAM_LR * ((nm / c1) / (jnp.sqrt(nv / c2) + ADAM_EPS) + ADAM_WD * w_ref[...])
        nm_ref[...] = nm
        nv_ref[...] = nv

    spec = pl.BlockSpec((tr, C), lambda i: (i, 0))
    sh = jax.ShapeDtypeStruct((R, C), F32)
    return pl.pallas_call(body, name=name, grid=(R // tr,), in_specs=[spec] * 4, out_specs=[spec] * 3,
                          out_shape=[sh] * 3, compiler_params=_params(("parallel",)))(w, g, m, v)


def sum_slots(name, x):
    n, R, C = x.shape
    tr = _pick(R, (256, 128, 64, 32, 16, 8))

    def body(x_ref, o_ref):
        acc = x_ref[0].astype(F32)
        for s in range(1, n):
            acc = acc + x_ref[s].astype(F32)
        o_ref[...] = acc

    return pl.pallas_call(
        body, name=name, grid=(R // tr,), in_specs=[pl.BlockSpec((n, tr, C), lambda i: (0, i, 0))],
        out_specs=pl.BlockSpec((tr, C), lambda i: (i, 0)), out_shape=jax.ShapeDtypeStruct((R, C), F32),
        compiler_params=_params(("parallel",)))(x)


ANY = pl.BlockSpec(memory_space=pl.ANY)


def _xyc():
    return lax.axis_index("x"), lax.axis_index("y"), lax.axis_index("c")


def gather_shards(shard):
    _, R, C = shard.shape

    def body(x_ref, out_ref, send_sems, recv_sems, local_sem):
        x, y, c = _xyc()
        me, sibling = (x, y, c), (x, y, 1 - c)
        chips = [(1 - x, y), (x, 1 - y), (1 - x, 1 - y)]

        def cp(k, cx, cy, half, to, src=None):
            dst = out_ref.at[2 * cx + cy, half]
            return pltpu.make_async_remote_copy(
                src_ref=dst if src is None else src, dst_ref=dst, send_sem=send_sems.at[k],
                recv_sem=recv_sems.at[k], device_id=to, device_id_type=MESH)

        mine = pltpu.make_async_copy(x_ref, out_ref.at[2 * x + y], local_sem)
        mine.start()
        first = [cp(j, x, y, c, (*chip, c), src=x_ref.at[c]) for j, chip in enumerate(chips)]
        for f in first:
            f.start()
        passed = [cp(3 + j, *chip, c, sibling) for j, chip in enumerate(chips)]
        for j, chip in enumerate(chips):
            cp(j, *chip, c, me).wait_recv()
            passed[j].start()
        for j, chip in enumerate(chips):
            cp(3 + j, *chip, 1 - c, me).wait_recv()
        for f in first + passed:
            f.wait_send()
        mine.wait()

    return pl.pallas_call(
        body, name="gather_shards", in_specs=[ANY], out_specs=ANY,
        out_shape=jax.ShapeDtypeStruct((4, 2, R, C), shard.dtype),
        scratch_shapes=[pltpu.SemaphoreType.DMA((6,)), pltpu.SemaphoreType.DMA((6,)), pltpu.SemaphoreType.DMA])(shard)


FLIPS = [(0, 0, 1), (0, 1, 0), (0, 1, 1), (1, 0, 0), (1, 0, 1), (1, 1, 0), (1, 1, 1)]


def scatter_partials(g):
    _, _, R, C = g.shape

    def body(g_ref, out_ref, send_sems, recv_sems, local_sem):
        x, y, c = _xyc()
        me_idx = 4 * x + 2 * y + c
        mine = pltpu.make_async_copy(g_ref.at[2 * x + y, c], out_ref.at[me_idx], local_sem)
        mine.start()
        sends = []
        for k, (fx, fy, fc) in enumerate(FLIPS):
            px, py, pc = (x + fx) % 2, (y + fy) % 2, (c + fc) % 2
            s = pltpu.make_async_remote_copy(
                src_ref=g_ref.at[2 * px + py, pc], dst_ref=out_ref.at[me_idx], send_sem=send_sems.at[k],
                recv_sem=recv_sems.at[k], device_id=(px, py, pc), device_id_type=MESH)
            s.start()
            sends.append(s)
        for k, (fx, fy, fc) in enumerate(FLIPS):
            px, py, pc = (x + fx) % 2, (y + fy) % 2, (c + fc) % 2
            slot = out_ref.at[4 * px + 2 * py + pc]
            pltpu.make_async_remote_copy(
                src_ref=slot, dst_ref=slot, send_sem=send_sems.at[k], recv_sem=recv_sems.at[k],
                device_id=(px, py, pc), device_id_type=MESH).wait_recv()
        for s in sends:
            s.wait_send()
        mine.wait()

    return pl.pallas_call(
        body, name="scatter_partials", in_specs=[ANY], out_specs=ANY,
        out_shape=jax.ShapeDtypeStruct((8, R, C), g.dtype),
        scratch_shapes=[pltpu.SemaphoreType.DMA((7,)), pltpu.SemaphoreType.DMA((7,)), pltpu.SemaphoreType.DMA])(g)


JOIN_CHUNKS = 8


def sibling_join(half):
    R, C = half.shape
    rows = R // JOIN_CHUNKS

    def body(h_ref, out_ref, send_sems, recv_sems, local_sem):
        x, y, c = _xyc()
        mine = pltpu.make_async_copy(h_ref, out_ref.at[c], local_sem)
        mine.start()
        sends = []
        for k in range(JOIN_CHUNKS):
            s = pltpu.make_async_remote_copy(
                src_ref=h_ref.at[pl.ds(k * rows, rows)], dst_ref=out_ref.at[c, pl.ds(k * rows, rows)],
                send_sem=send_sems.at[k], recv_sem=recv_sems.at[k], device_id=(x, y, 1 - c), device_id_type=MESH)
            s.start()
            sends.append(s)
        for k in range(JOIN_CHUNKS):
            theirs = out_ref.at[1 - c, pl.ds(k * rows, rows)]
            pltpu.make_async_remote_copy(src_ref=theirs, dst_ref=theirs, send_sem=send_sems.at[k],
                                         recv_sem=recv_sems.at[k], device_id=(x, y, 1 - c),
                                         device_id_type=MESH).wait_recv()
        for s in sends:
            s.wait_send()
        mine.wait()

    return pl.pallas_call(
        body, name="sibling_join", in_specs=[ANY], out_specs=ANY,
        out_shape=jax.ShapeDtypeStruct((2, R, C), half.dtype),
        scratch_shapes=[pltpu.SemaphoreType.DMA((JOIN_CHUNKS,)), pltpu.SemaphoreType.DMA((JOIN_CHUNKS,)),
                        pltpu.SemaphoreType.DMA])(half)


def gather_all(name, block):
    R, C = block.shape

    def body(x_ref, out_ref, send_sems, recv_sems, local_sem):
        x, y, c = _xyc()
        mine = pltpu.make_async_copy(x_ref, out_ref.at[4 * x + 2 * y + c], local_sem)
        mine.start()
        sends = []
        for k, (fx, fy, fc) in enumerate(FLIPS):
            px, py, pc = (x + fx) % 2, (y + fy) % 2, (c + fc) % 2
            s = pltpu.make_async_remote_copy(
                src_ref=x_ref, dst_ref=out_ref.at[4 * x + 2 * y + c], send_sem=send_sems.at[k],
                recv_sem=recv_sems.at[k], device_id=(px, py, pc), device_id_type=MESH)
            s.start()
            sends.append(s)
        for k, (fx, fy, fc) in enumerate(FLIPS):
            px, py, pc = (x + fx) % 2, (y + fy) % 2, (c + fc) % 2
            slot = out_ref.at[4 * px + 2 * py + pc]
            pltpu.make_async_remote_copy(
                src_ref=slot, dst_ref=slot, send_sem=send_sems.at[k], recv_sem=recv_sems.at[k],
                device_id=(px, py, pc), device_id_type=MESH).wait_recv()
        for s in sends:
            s.wait_send()
        mine.wait()

    return pl.pallas_call(
        body, name=name, in_specs=[ANY], out_specs=ANY,
        out_shape=jax.ShapeDtypeStruct((8, R, C), block.dtype),
        scratch_shapes=[pltpu.SemaphoreType.DMA((7,)), pltpu.SemaphoreType.DMA((7,)), pltpu.SemaphoreType.DMA])(block)


PACK_C = 1024


def _pack(arrs, row_mult):
    flat = jnp.concatenate([a.reshape(-1) for a in arrs])
    n = flat.shape[0]
    rows = -(-n // PACK_C)
    rows = -(-rows // row_mult) * row_mult
    return jnp.pad(flat, (0, rows * PACK_C - n)).reshape(rows, PACK_C)


def _unpack(buf, shapes):
    flat = buf.reshape(-1)
    out, off = [], 0
    for s in shapes:
        n = int(np.prod(s))
        out.append(flat[off:off + n].reshape(s))
        off += n
    return out


OFF_Q, OFF_CKV, OFF_KR, OFF_SG, OFF_RW, OFF_GATE, N_IN_PAD = 0, 384, 640, 896, 1920, 3840, 6912
IN_SEGMENTS = [('q', OFF_Q, OFF_CKV), ('ckv', OFF_CKV, OFF_KR), ('kr', OFF_KR, OFF_SG), ('sg', OFF_SG, OFF_RW),
               ('rw', OFF_RW, OFF_GATE), ('gate', OFF_GATE, N_IN_PAD)]
ROPE_LANE = QK_NOPE
HALF = QK_ROPE // 2


def _win_layout():
    src = np.full((N_IN_PAD,), -1, np.int64)
    sgn = np.ones((N_IN_PAD,), np.float32)
    src[0:640] = np.arange(0, 640)
    kr0 = Q_LORA + KV_LORA
    src[OFF_KR + ROPE_LANE:OFF_KR + ROPE_LANE + QK_ROPE] = kr0 + np.arange(QK_ROPE)
    sw = OFF_KR + HEAD_PAD + ROPE_LANE
    src[sw:sw + HALF] = kr0 + HALF + np.arange(HALF)
    sgn[sw:sw + HALF] = -1.0
    src[sw + HALF:sw + QK_ROPE] = kr0 + np.arange(HALF)
    src[OFF_SG:N_IN_PAD] = 672 + np.arange(N_IN_PAD - OFF_SG)
    return src, sgn


def _wuq_layout():
    hw = MLA_HEADS * HEAD_PAD
    src = np.full((2 * hw,), -1, np.int64)
    sgn = np.ones((2 * hw,), np.float32)
    per = QK_NOPE + QK_ROPE
    for h in range(MLA_HEADS):
        src[h * HEAD_PAD:h * HEAD_PAD + per] = h * per + np.arange(per)
        sw = hw + h * HEAD_PAD + ROPE_LANE
        src[sw:sw + HALF] = h * per + QK_NOPE + HALF + np.arange(HALF)
        sgn[sw:sw + HALF] = -1.0
        src[sw + HALF:sw + QK_ROPE] = h * per + QK_NOPE + np.arange(HALF)
    return src, sgn


def _runs(idx, sgn):
    out, lo = [], 0
    for pos in range(1, len(idx) + 1):
        if pos == len(idx) or not (
                (idx[pos] == -1 and idx[pos - 1] == -1)
                or (idx[pos - 1] >= 0 and idx[pos] == idx[pos - 1] + 1 and sgn[pos] == sgn[pos - 1])):
            out.append((lo, pos, int(idx[lo]), float(sgn[lo])))
            lo = pos
    return out


def _select_cols(w, idx, sgn):
    pieces = []
    for lo, hi, s0, sg in _runs(idx, sgn):
        if s0 < 0:
            pieces.append(jnp.zeros((w.shape[0], hi - lo), w.dtype))
        else:
            piece = w[:, s0:s0 + hi - lo]
            pieces.append(piece if sg > 0 else -piece)
    return jnp.concatenate(pieces, axis=1)


def _permute_cols(w, src, sgn):
    return _select_cols(w, src, sgn)


def _unpermute_full(dw, src, sgn, n_cols):
    first = np.full((n_cols,), -1, np.int64)
    second = np.full((n_cols,), -1, np.int64)
    for pos, s in enumerate(src):
        if s < 0:
            continue
        if first[s] < 0:
            first[s] = pos
        else:
            second[s] = pos
    sg2 = np.where(second >= 0, sgn[np.maximum(second, 0)], 1.0)
    return _select_cols(dw, first, sgn[first]), _select_cols(dw, second, sg2)


def _blockdiag(w):
    z = jnp.zeros_like(w[0])
    return jnp.concatenate([jnp.concatenate([w[0], z], axis=1), jnp.concatenate([z, w[1]], axis=1)], axis=0)


def _rope_tables(pos):
    inv = 1.0 / (ROPE_THETA ** (jnp.arange(0, QK_ROPE, 2, dtype=F32) / QK_ROPE))
    ang = pos.astype(F32)[:, None] * inv[None, :]
    cos, sin = jnp.cos(ang), jnp.sin(ang)
    pad = lambda t, fill: jnp.concatenate(
        [jnp.full((t.shape[0], ROPE_LANE), fill, F32), t, t, jnp.full((t.shape[0], HEAD_PAD - ROPE_LANE - QK_ROPE), fill, F32)], axis=1)
    return pad(cos, 1.0), pad(sin, 0.0)


def kernel(x, positions, attn_norm_g, w_in, gate_b, q_norm_g, w_uq, kv_norm_g, w_ukv, sg_ln_g, sg_ln_b, sg_w, sg_b, rw_mu, rw_w0, rw_w2, rw_a0, rw_a2, rw_g2, rw_k_k, rw_k_a, rw_r_k, rw_ln_g, rw_ln_b, w_branch, w_out, ffn_norm_g, w_ffn_gate, w_ffn_up, w_ffn_down, final_norm_g, loss_target, m_attn_norm_g, m_w_in, m_gate_b, m_q_norm_g, m_w_uq, m_kv_norm_g, m_w_ukv, m_sg_ln_g, m_sg_ln_b, m_sg_w, m_sg_b, m_rw_mu, m_rw_w0, m_rw_w2, m_rw_a0, m_rw_a2, m_rw_g2, m_rw_k_k, m_rw_k_a, m_rw_r_k, m_rw_ln_g, m_rw_ln_b, m_w_branch, m_w_out, m_ffn_norm_g, m_w_ffn_gate, m_w_ffn_up, m_w_ffn_down, m_final_norm_g, v_attn_norm_g, v_w_in, v_gate_b, v_q_norm_g, v_w_uq, v_kv_norm_g, v_w_ukv, v_sg_ln_g, v_sg_ln_b, v_sg_w, v_sg_b, v_rw_mu, v_rw_w0, v_rw_w2, v_rw_a0, v_rw_a2, v_rw_g2, v_rw_k_k, v_rw_k_a, v_rw_r_k, v_rw_ln_g, v_rw_ln_b, v_w_branch, v_w_out, v_ffn_norm_g, v_w_ffn_gate, v_w_ffn_up, v_w_ffn_down, v_final_norm_g):
    args = locals()
    W = {n: args[n] for n in WEIGHTS}
    M1 = {n: args['m_' + n] for n in WEIGHTS}
    M2 = {n: args['v_' + n] for n in WEIGHTS}
    B, S, D = x.shape
    N = B * S
    TM = _pick(N, (256, 128))
    TMH = 128
    TQ = _pick(S, (512, 256, 128))

    shard_shapes = [W[n].shape for n in SHARDED]
    full = {}
    mm_pack = _pack([W[n].astype(BF16) for n in MATMUL_SHARDED], 32)
    Rm = mm_pack.shape[0]
    gathered = gather_shards(mm_pack.reshape(2, Rm // 2, PACK_C)).reshape(4, Rm, PACK_C)
    pieces = [_unpack(gathered[q], [W[n].shape for n in MATMUL_SHARDED]) for q in range(4)]
    for i, n in enumerate(MATMUL_SHARDED):
        full[n] = jnp.concatenate([pieces[q][i] for q in range(4)], axis=SHARD_AXIS[n])
    small = gather_all("gather_small", _pack([W[n] for n in SMALL_SHARDED], 8))
    pieces = [_unpack(small[2 * q], [W[n].shape for n in SMALL_SHARDED]) for q in range(4)]
    for i, n in enumerate(SMALL_SHARDED):
        full[n] = jnp.concatenate([pieces[q][i] for q in range(4)], axis=SHARD_AXIS[n])
    for n in ('rw_w2', 'rw_a2', 'rw_g2'):
        full[n] = full[n].astype(F32)
    for n in REPLICATED:
        full[n] = W[n]

    win_src, win_sgn = _win_layout()
    wuq_src, wuq_sgn = _wuq_layout()
    ones = jnp.asarray(np.kron(np.eye(RWKV_HEADS), np.ones((RWKV_HEAD, RWKV_HEAD))), BF16)
    ct, st = _rope_tables(positions.reshape(N))
    row = lambda v: v.reshape(1, -1)

    H, HD = RWKV_HEADS, RWKV_HEAD
    inst = 2 * B * H
    isplit = LANES // inst
    Ip = HD // isplit
    to_j = lambda nm, c: to_scan(nm, c, B, S, c.shape[1] // RWKV_DIM, None)
    to_i = lambda nm, c: to_scan(nm, c, B, S, 1, Ip)

    def shift_prev(z):
        z = z.reshape(B, S, -1)
        return jnp.pad(z[:, :-1], ((0, 0), (1, 0), (0, 0))).reshape(N, -1)

    def shift_next(z):
        z = z.reshape(B, S, -1)
        return jnp.pad(z[:, 1:], ((0, 0), (0, 1), (0, 0))).reshape(N, -1)

    LW = []
    for l in range(DEPTH):
        wb = full['w_branch'][l]
        wb0 = jnp.zeros((MLA_HEADS, HEAD_PAD, D), F32).at[:, QK_NOPE:].set(wb[0].reshape(MLA_HEADS, V_HEAD, D))
        LW.append(dict(
            attn_g=row(full['attn_norm_g'][l]),
            w_in=_permute_cols(full['w_in'][l], win_src, win_sgn),
            gate_b=row(full['gate_b'][l]),
            q_g=row(full['q_norm_g'][l]),
            w_uq=_permute_cols(full['w_uq'][l], wuq_src, wuq_sgn),
            kv_g=row(full['kv_norm_g'][l]),
            w_ukv=full['w_ukv'][l],
            sg_g=row(full['sg_ln_g'][l]), sg_b=row(full['sg_ln_b'][l]), sg_w=full['sg_w'][l],
            sg_bias=jnp.repeat(full['sg_b'][l].T, SG_DIM // SG_GROUPS, axis=1),
            mu=row(full['rw_mu'][l]), w0=row(full['rw_w0'][l]), w2=_blockdiag(full['rw_w2'][l]),
            a0=row(full['rw_a0'][l]), a2=_blockdiag(full['rw_a2'][l]), g2=full['rw_g2'][l],
            k_k=row(full['rw_k_k'][l]), k_a=row(full['rw_k_a'][l]), r_k=row(full['rw_r_k'][l]),
            ln_g=row(full['rw_ln_g'][l]), ln_b=row(full['rw_ln_b'][l]),
            wb0=wb0.reshape(MLA_HEADS * HEAD_PAD, D), wb1=wb[1], wb2=wb[2],
            w_out=full['w_out'][l], ffn_g=row(full['ffn_norm_g'][l]),
            w_gu=jnp.concatenate([full['w_ffn_gate'][l], full['w_ffn_up'][l]], axis=1),
            w_down=full['w_ffn_down'][l]))

    saved = []
    xc = x.reshape(N, D)
    for l in range(DEPTH):
        p = LW[l]
        t = 'l%d_' % l
        sv = dict(x=xc)
        (h,) = rowwise(t + 'attn_norm', f_rms, [xc], [p['attn_g']], [], [D], TM, [BF16])
        p_q, p_ckv, p_kr, p_sg, z, p_gate = [
            matmul(t + 'in_proj_' + sn, h, p['w_in'][:, a:b], 'nn') for sn, a, b in IN_SEGMENTS]
        sv['h'] = h
        (cq,) = rowwise(t + 'q_norm', f_rms, [p_q], [p['q_g']], [], [Q_LORA], TM, [BF16])
        (ckv,) = rowwise(t + 'kv_norm', f_rms, [p_ckv], [p['kv_g']], [], [KV_LORA], TM, [BF16])
        qq = matmul(t + 'uq', cq, p['w_uq'], 'nn')
        kv = matmul(t + 'ukv', ckv, p['w_ukv'], 'nn')
        qh, kh = rowwise(t + 'rope', f_rope, [qq, kv, p_kr, ct, st], [], [], [MLA_HEADS * HEAD_PAD] * 2, TM)
        ya = attention_fwd(t + 'attn', qh, kh, kv, B, S, TQ, BF16)
        sv.update(p_q=p_q, p_ckv=p_ckv, p_kr=p_kr, cq=cq, ckv=ckv, qq=qq, kv=kv, qh=qh, kh=kh, ya=ya)
        (yb,) = rowwise(t + 'sg', f_sg, [p_sg], [p['sg_g'], p['sg_b'], p['sg_w'], p['sg_bias']], [], [SG_DIM], TM,
                        [BF16])
        sv.update(p_sg=p_sg, yb=yb)
        zp, zn = shift_prev(z), shift_next(z)
        rw_par = [p['mu'], p['w0'], p['w2'], p['a0'], p['a2'], p['g2'], p['k_k'], p['k_a']]
        r_, v_, decay, kdir, kk, bdir, g_ = rowwise(
            t + 'rw_pre', f_rw_pre, [z, zp, zn], rw_par, [ones],
            [RWKV_DIM, RWKV_DIM, 2 * RWKV_DIM, 2 * RWKV_DIM, RWKV_DIM, 2 * RWKV_DIM, RWKV_DIM], TM)
        sc = dict(w=to_j(t + 'lay_w', decay), k=to_j(t + 'lay_k', kdir), b=to_j(t + 'lay_b', bdir),
                  kk=to_j(t + 'lay_kk', kk), r=to_j(t + 'lay_r', r_), v=to_i(t + 'lay_v', v_))
        y_f, y_r, sp, sa_all, s_last = scan_fwd(t + 'scan', sc['w'], sc['k'], sc['b'], sc['kk'], sc['r'], sc['v'])
        ysum = from_scan(t + 'lay_y', y_f, y_r, B, S, 1, True)
        (yc,) = rowwise(t + 'rw_post', f_rw_post, [ysum, r_, v_, kdir, g_], [p['ln_g'], p['ln_b'], p['r_k']],
                        [ones], [RWKV_DIM], TM, [BF16])
        sv.update(z=z, zp=zp, zn=zn, r=r_, v=v_, kdir=kdir, g=g_, sc=sc, sp=sp, sa=sa_all, s_last=s_last,
                  ysum=ysum, yc=yc)
        b0 = matmul(t + 'br0', ya, p['wb0'], 'nn')
        b1 = matmul(t + 'br1', yb, p['wb1'], 'nn')
        b2 = matmul(t + 'br2', yc, p['wb2'], 'nn')
        (merged,) = rowwise(t + 'merge', f_merge, [p_gate, b0, b1, b2], [p['gate_b']], [], [D], TM, [BF16])
        x2 = matmul(t + 'out_proj', merged, p['w_out'], 'nn', add=xc)
        sv.update(p_gate=p_gate, b0=b0, b1=b1, b2=b2, merged=merged, x2=x2)
        (h2,) = rowwise(t + 'ffn_norm', f_rms, [x2], [p['ffn_g']], [], [D], TM, [BF16])
        au = matmul(t + 'ffn_in', h2, p['w_gu'], 'nn')
        (act,) = rowwise(t + 'swiglu', f_swiglu, [au], [], [], [D_FF], TM, [BF16])
        xc = matmul(t + 'ffn_out', act, p['w_down'], 'nn', add=x2)
        sv.update(h2=h2, au=au, act=act)
        saved.append(sv)

    loss_part, dx, d_final_g = loss_head(xc, loss_target.reshape(N, D), row(full['final_norm_g']), TM)
    loss = lax.psum(loss_part[0, 0], ("x", "y", "c"))

    G = {n: [None] * DEPTH for n in WEIGHTS if n != 'final_norm_g'}
    for l in reversed(range(DEPTH)):
        p, sv = LW[l], saved[l]
        t = 'l%d_bwd_' % l
        d_act = matmul(t + 'ffn_out_dx', dx, p['w_down'], 'nt')
        G['w_ffn_down'][l] = matmul(t + 'ffn_out_dw', sv['act'], dx, 'tn')
        (d_au,), _ = rowwise_bwd(t + 'swiglu', f_swiglu, [sv['au']], [], [], [[d_act]], TMH, drow_dtypes=[BF16])
        d_h2 = matmul(t + 'ffn_in_dx', d_au, p['w_gu'], 'nt')
        d_wgu = matmul(t + 'ffn_in_dw', sv['h2'], d_au, 'tn')
        G['w_ffn_gate'][l], G['w_ffn_up'][l] = d_wgu[:, :D_FF], d_wgu[:, D_FF:]
        (dx2,), (dg,) = rowwise_bwd(t + 'ffn_norm', f_rms, [sv['x2']], [p['ffn_g']], [], [[d_h2]], TM, extra=[(0, dx)])
        G['ffn_norm_g'][l] = dg.reshape(-1)
        d_merged = matmul(t + 'out_proj_dx', dx2, p['w_out'], 'nt')
        G['w_out'][l] = matmul(t + 'out_proj_dw', sv['merged'], dx2, 'tn')
        (d_pgate, d_b0, d_b1, d_b2), (d_gate_b,) = rowwise_bwd(
            t + 'merge', f_merge, [sv['p_gate'], sv['b0'], sv['b1'], sv['b2']], [p['gate_b']], [], [[d_merged]], TM,
            drow_dtypes=[BF16] * 4)
        G['gate_b'][l] = d_gate_b.reshape(3, D)
        d_ya = matmul(t + 'br0_dx', d_b0, p['wb0'], 'nt')
        d_yb = matmul(t + 'br1_dx', d_b1, p['wb1'], 'nt')
        d_yc = matmul(t + 'br2_dx', d_b2, p['wb2'], 'nt')
        d_wb0 = matmul(t + 'br0_dw', sv['ya'], d_b0, 'tn').reshape(MLA_HEADS, HEAD_PAD, D)[:, QK_NOPE:].reshape(-1, D)
        G['w_branch'][l] = jnp.stack([d_wb0, matmul(t + 'br1_dw', sv['yb'], d_b1, 'tn'),
                                      matmul(t + 'br2_dw', sv['yc'], d_b2, 'tn')])
        (d_y, d_r1, d_v1, d_kdir1, d_g), (d_ln_g, d_ln_b, d_r_k) = rowwise_bwd(
            t + 'rw_post', f_rw_post, [sv['ysum'], sv['r'], sv['v'], sv['kdir'], sv['g']],
            [p['ln_g'], p['ln_b'], p['r_k']], [ones], [[d_yc]], TMH)
        G['rw_ln_g'][l], G['rw_ln_b'][l] = d_ln_g.reshape(-1), d_ln_b.reshape(-1)
        G['rw_r_k'][l] = d_r_k.reshape(RWKV_HEADS, RWKV_HEAD)
        sc = sv['sc']
        res = scan_bwd(t + 'scan', sc['w'], sc['k'], sc['b'], sc['kk'], sc['r'], sc['v'], sv['sp'], sv['sa'],
                       sv['s_last'], to_i(t + 'lay_dy', d_y))
        s_dw, s_dk, s_db, s_dkk, s_dr, s_dv = [
            from_scan(t + 'lay_' + nm, res[2 * i], res[2 * i + 1], B, S, nd, nm == 'dv')
            for i, (nm, nd) in enumerate((('dw', 2), ('dk', 2), ('db', 2), ('dkk', 1), ('dr', 1), ('dv', 1)))]
        rw_par = [p['mu'], p['w0'], p['w2'], p['a0'], p['a2'], p['g2'], p['k_k'], p['k_a']]
        d_outs = [[d_r1, s_dr], [d_v1, s_dv], [s_dw], [d_kdir1, s_dk], [s_dkk], [s_db], [d_g]]
        (d_z, d_zp, d_zn), d_rw = rowwise_bwd(
            t + 'rw_pre', f_rw_pre, [sv['z'], sv['zp'], sv['zn']], rw_par, [ones], d_outs, TMH)
        (d_prw,) = rowwise(t + 'shift_sum', f_add3, [d_z, shift_next(d_zp), shift_prev(d_zn)], [], [], [RWKV_IN], TM,
                           [BF16])
        G['rw_mu'][l] = d_rw[0].reshape(-1)
        G['rw_w0'][l] = d_rw[1].reshape(2, RWKV_DIM)
        G['rw_w2'][l] = jnp.stack([d_rw[2][:64, :RWKV_DIM], d_rw[2][64:, RWKV_DIM:]])
        G['rw_a0'][l] = d_rw[3].reshape(2, RWKV_DIM)
        G['rw_a2'][l] = jnp.stack([d_rw[4][:64, :RWKV_DIM], d_rw[4][64:, RWKV_DIM:]])
        G['rw_g2'][l] = d_rw[5]
        G['rw_k_k'][l], G['rw_k_a'][l] = d_rw[6].reshape(-1), d_rw[7].reshape(-1)
        (d_psg,), (d_sg_g, d_sg_b, d_sg_w, d_sg_bias) = rowwise_bwd(
            t + 'sg', f_sg, [sv['p_sg']], [p['sg_g'], p['sg_b'], p['sg_w'], p['sg_bias']], [], [[d_yb]], TMH,
            drow_dtypes=[BF16])
        G['sg_ln_g'][l], G['sg_ln_b'][l], G['sg_w'][l] = d_sg_g.reshape(-1), d_sg_b.reshape(-1), d_sg_w
        G['sg_b'][l] = d_sg_bias.reshape(SG_CHUNK, SG_GROUPS, SG_DIM // SG_GROUPS).sum(-1).T
        d_qh, d_kh, d_kvv = attention_bwd(t + 'attn', sv['qh'], sv['kh'], sv['kv'], d_ya, B, S, TQ)
        (d_qq, d_kv, d_pkr), _ = rowwise_bwd(
            t + 'rope', f_rope, [sv['qq'], sv['kv'], sv['p_kr'], ct, st], [], [], [[d_qh], [d_kh]], TM,
            n_row_diff=3, extra=[(1, d_kvv)], drow_dtypes=[BF16] * 3)
        d_cq = matmul(t + 'uq_dx', d_qq, p['w_uq'], 'nt')
        d_wuq = matmul(t + 'uq_dw', sv['cq'], d_qq, 'tn')
        g1, g2_ = _unpermute_full(d_wuq, wuq_src, wuq_sgn, MLA_HEADS * (QK_NOPE + QK_ROPE))
        G['w_uq'][l] = g1 + g2_
        d_ckv = matmul(t + 'ukv_dx', d_kv, p['w_ukv'], 'nt')
        G['w_ukv'][l] = matmul(t + 'ukv_dw', sv['ckv'], d_kv, 'tn')
        (d_pq,), (dg,) = rowwise_bwd(t + 'q_norm', f_rms, [sv['p_q']], [p['q_g']], [], [[d_cq]], TM,
                                     drow_dtypes=[BF16])
        G['q_norm_g'][l] = dg.reshape(-1)
        (d_pckv,), (dg,) = rowwise_bwd(t + 'kv_norm', f_rms, [sv['p_ckv']], [p['kv_g']], [], [[d_ckv]], TM,
                                       drow_dtypes=[BF16])
        G['kv_norm_g'][l] = dg.reshape(-1)
        d_h, d_cols = None, []
        for (sn, a, b), d_seg in zip(IN_SEGMENTS, [d_pq, d_pckv, d_pkr, d_psg, d_prw, d_pgate]):
            d_h = matmul(t + 'in_proj_dx_' + sn, d_seg, p['w_in'][:, a:b], 'nt', add=d_h)
            d_cols.append(matmul(t + 'in_proj_dw_' + sn, sv['h'], d_seg, 'tn'))
        d_win = jnp.concatenate(d_cols, axis=1)
        g1, g2_ = _unpermute_full(d_win, win_src, win_sgn, N_IN)
        kr0 = Q_LORA + KV_LORA
        G['w_in'][l] = g1.at[:, kr0:kr0 + QK_ROPE].add(g2_[:, kr0:kr0 + QK_ROPE])
        (dx,), (dg,) = rowwise_bwd(t + 'attn_norm', f_rms, [sv['x']], [p['attn_g']], [], [[d_h]], TM, extra=[(0, dx2)])
        G['attn_norm_g'][l] = dg.reshape(-1)

    grads = {n: jnp.stack(G[n]) for n in G}
    grads['final_norm_g'] = d_final_g.reshape(-1)
    grad_x = dx.reshape(B, S, D)

    per_shard = []
    for q in range(4):
        sl = []
        for n in SHARDED:
            ax = SHARD_AXIS[n]
            w = W[n].shape[ax]
            sl.append(lax.slice_in_dim(grads[n], q * w, (q + 1) * w, axis=ax).astype(BF16))
        per_shard.append(_pack(sl, 2 * 8 * JOIN_CHUNKS))
    R = per_shard[0].shape[0]
    gpack = jnp.stack(per_shard).astype(BF16).reshape(4, 2, R // 2, PACK_C)
    half_sum = sum_slots("sum_sharded", scatter_partials(gpack))
    g_shard = dict(zip(SHARDED, _unpack(sibling_join(half_sum).reshape(R, PACK_C), shard_shapes)))
    rep_shapes = [W[n].shape for n in REPLICATED]
    rpack = _pack([grads[n] for n in REPLICATED], 8)
    g_rep = sum_slots("sum_replicated", gather_all("gather_replicated", rpack))

    outs = {}
    for n in MATMUL_SHARDED:
        shp = W[n].shape
        two = lambda a: a.reshape(-1, shp[-1])
        res = adamw("adamw_" + n, two(W[n]), two(g_shard[n]), two(M1[n]), two(M2[n]))
        outs['grad', n] = g_shard[n]
        for key, a in zip(('delta', 'new_m', 'new_v'), res):
            outs[key, n] = a.reshape(shp)
    small = SMALL_SHARDED + REPLICATED
    small_shapes = [W[n].shape for n in small]
    g_small = [g_shard[n] for n in SMALL_SHARDED] + _unpack(g_rep, rep_shapes)
    res = adamw("adamw_small", _pack([W[n] for n in small], 8), _pack(g_small, 8),
                _pack([M1[n] for n in small], 8), _pack([M2[n] for n in small], 8))
    for n, a in zip(small, g_small):
        outs['grad', n] = a
    for key, buf in zip(('delta', 'new_m', 'new_v'), res):
        for n, a in zip(small, _unpack(buf, small_shapes)):
            outs[key, n] = a
    return (loss, grad_x, *[outs['grad', n] for n in WEIGHTS], *[outs['delta', n] for n in WEIGHTS],
            *[outs['new_m', n] for n in WEIGHTS], *[outs['new_v', n] for n in WEIGHTS])
```

```python
import functools
import math

import numpy as np
import jax
import jax.numpy as jnp
from jax import lax
from jax.experimental import pallas as pl
from jax.experimental.pallas import tpu as pltpu

F32 = jnp.float32
BF16 = jnp.bfloat16

DEPTH = 2
MLA_HEADS = 8
Q_LORA = 384
KV_LORA = 256
QK_NOPE = 64
QK_ROPE = 32
V_HEAD = 64
ROPE_THETA = 10000.0
SG_GROUPS = 8
SG_DIM = 512
SG_CHUNK = 128
RWKV_HEADS = 8
RWKV_HEAD = 64
RWKV_DIM = 512
GN_EPS = 64e-5
NORM_EPS = 1e-6
D_FF = 2816
RWKV_IN = 1920
N_IN = 6688
ADAM_LR, ADAM_B1, ADAM_B2, ADAM_EPS, ADAM_WD, ADAM_STEP = 0.001, 0.9, 0.999, 1e-08, 0.01, 10

LANES = 128
HEAD_PAD = 128
VMEM_LIMIT = 56 * 1024 * 1024
MESH = pl.DeviceIdType.MESH

WEIGHTS = ['attn_norm_g', 'w_in', 'gate_b', 'q_norm_g', 'w_uq', 'kv_norm_g', 'w_ukv', 'sg_ln_g', 'sg_ln_b', 'sg_w',
           'sg_b', 'rw_mu', 'rw_w0', 'rw_w2', 'rw_a0', 'rw_a2', 'rw_g2', 'rw_k_k', 'rw_k_a', 'rw_r_k', 'rw_ln_g',
           'rw_ln_b', 'w_branch', 'w_out', 'ffn_norm_g', 'w_ffn_gate', 'w_ffn_up', 'w_ffn_down', 'final_norm_g']
SHARD_AXIS = {'w_in': 2, 'gate_b': 2, 'w_uq': 2, 'w_ukv': 2, 'rw_w0': 2, 'rw_w2': 3, 'rw_a0': 2, 'rw_a2': 3,
              'rw_g2': 2, 'w_branch': 3, 'w_out': 1, 'w_ffn_gate': 2, 'w_ffn_up': 2, 'w_ffn_down': 1}
SHARDED = [n for n in WEIGHTS if n in SHARD_AXIS]
REPLICATED = [n for n in WEIGHTS if n not in SHARD_AXIS]
SMALL_SHARDED = ['gate_b', 'rw_w0', 'rw_a0']
MATMUL_SHARDED = [n for n in SHARDED if n not in SMALL_SHARDED]


def _params(sem=None):
    return pltpu.CompilerParams(dimension_semantics=sem, vmem_limit_bytes=VMEM_LIMIT)


def _pick(n, cands):
    for c in cands:
        if n % c == 0:
            return c
    return n


def _dot(a, b, dims):
    return lax.dot_general(a.astype(BF16), b.astype(BF16), (dims, ((), ())), preferred_element_type=F32)


def _nn(a, b):
    return _dot(a, b, ((1,), (0,)))


def _nt(a, b):
    return _dot(a, b, ((1,), (1,)))


def _tn(a, b):
    return _dot(a, b, ((0,), (0,)))


@jax.custom_vjp
def mm(a, b):
    return _nn(a, b)


mm.defvjp(lambda a, b: (_nn(a, b), (a, b)), lambda res, g: (_nt(g, res[1]), _tn(res[0], g)))


@jax.custom_vjp
def mm_nt(a, b):
    return _nt(a, b)


mm_nt.defvjp(lambda a, b: (_nt(a, b), (a, b)), lambda res, g: (_nn(g, res[1]), _tn(g, res[0])))


def _seg_raw(x, ones):
    hi = x.astype(BF16)
    lo = (x - hi.astype(F32)).astype(BF16)
    d = (((1,), (0,)), ((), ()))
    return (lax.dot_general(hi, ones, d, preferred_element_type=F32)
            + lax.dot_general(lo, ones, d, preferred_element_type=F32))


@jax.custom_vjp
def segsum(x, ones):
    return _seg_raw(x, ones)


segsum.defvjp(lambda x, ones: (_seg_raw(x, ones), ones),
              lambda ones, g: (_seg_raw(g, ones), jnp.zeros_like(ones)))


def _sigmoid(x):
    return 0.5 * (jnp.tanh(0.5 * x) + 1.0)


def _rms(x, g):
    return x * lax.rsqrt(jnp.mean(x * x, axis=-1, keepdims=True) + NORM_EPS) * g


def matmul(name, a, b, mode, add=None, out_dtype=F32):
    if mode == 'nn':
        (M, K), (_, N) = a.shape, b.shape
    elif mode == 'nt':
        (M, K), (N, _) = a.shape, b.shape
    else:
        (K, M), (_, N) = a.shape, b.shape
    tm = _pick(M, (1408, 1024, 512, 384, 256, 128))
    tn = _pick(N, (1408, 1024, 768, 512, 384, 256, 128))
    tk = _pick(K, (512, 384, 256, 128))
    nk = K // tk
    dims = {'nn': ((1,), (0,)), 'nt': ((1,), (1,)), 'tn': ((0,), (0,))}[mode]
    a_spec = pl.BlockSpec((tk, tm), lambda i, j, k: (k, i)) if mode == 'tn' else pl.BlockSpec((tm, tk), lambda i, j, k: (i, k))
    b_spec = pl.BlockSpec((tn, tk), lambda i, j, k: (j, k)) if mode == 'nt' else pl.BlockSpec((tk, tn), lambda i, j, k: (k, j))
    o_spec = pl.BlockSpec((tm, tn), lambda i, j, k: (i, j))
    has_add = add is not None

    def body(*refs):
        if has_add:
            a_ref, b_ref, add_ref, o_ref, acc = refs
        else:
            a_ref, b_ref, o_ref, acc = refs
        k = pl.program_id(2)

        @pl.when(k == 0)
        def _():
            acc[...] = jnp.zeros_like(acc)

        acc[...] += _dot(a_ref[...], b_ref[...], dims)

        @pl.when(k == nk - 1)
        def _():
            o_ref[...] = (acc[...] + add_ref[...] if has_add else acc[...]).astype(o_ref.dtype)

    ins = [a, b] + ([add] if has_add else [])
    specs = [a_spec, b_spec] + ([o_spec] if has_add else [])
    return pl.pallas_call(
        body, name=name, grid=(M // tm, N // tn, nk), in_specs=specs, out_specs=o_spec,
        out_shape=jax.ShapeDtypeStruct((M, N), out_dtype), scratch_shapes=[pltpu.VMEM((tm, tn), F32)],
        compiler_params=_params(("parallel", "parallel", "arbitrary")))(*ins)


def _full_spec(p):
    nd = p.ndim
    return pl.BlockSpec(p.shape, lambda i, _nd=nd: (0,) * _nd)


def rowwise(name, fn, rows, params, consts, out_widths, tm, out_dtypes=None):
    N = rows[0].shape[0]
    nr, npar, nc = len(rows), len(params), len(consts)

    def body(*refs):
        vals = [r[...] for r in refs[:nr + npar + nc]]
        res = fn(*vals)
        for o, v in zip(refs[nr + npar + nc:], res):
            o[...] = v.astype(o.dtype)

    in_specs = ([pl.BlockSpec((tm, r.shape[1]), lambda i: (i, 0)) for r in rows]
                + [_full_spec(p) for p in list(params) + list(consts)])
    out_specs = [pl.BlockSpec((tm, w), lambda i: (i, 0)) for w in out_widths]
    return pl.pallas_call(
        body, name=name, grid=(N // tm,), in_specs=in_specs, out_specs=out_specs,
        out_shape=[jax.ShapeDtypeStruct((N, w), d) for w, d in zip(out_widths, out_dtypes or [F32] * len(out_widths))],
        compiler_params=_params(("parallel",)))(*rows, *params, *consts)


def rowwise_bwd(name, fn, rows, params, consts, d_outs, tm, n_row_diff=None, extra=(), drow_dtypes=None):
    N = rows[0].shape[0]
    nr, npar, nc = len(rows), len(params), len(consts)
    nd = nr if n_row_diff is None else n_row_diff
    counts = [len(p) for p in d_outs]
    flat_d = [a for parts in d_outs for a in parts]
    nflat, nex = len(flat_d), len(extra)

    def body(*refs):
        pos = 0
        row_v = [r[...] for r in refs[pos:pos + nr]]; pos += nr
        par_v = [r[...] for r in refs[pos:pos + npar]]; pos += npar
        con_v = [r[...] for r in refs[pos:pos + nc]]; pos += nc
        d_refs = refs[pos:pos + nflat]; pos += nflat
        ex_refs = refs[pos:pos + nex]; pos += nex
        drow_refs = refs[pos:pos + nd]; pos += nd
        dpar_refs = refs[pos:pos + npar]

        def f(*diff):
            return fn(*diff[:nd], *row_v[nd:], *diff[nd:], *con_v)

        _, vjp = jax.vjp(f, *row_v[:nd], *par_v)
        cts, q = [], 0
        for c in counts:
            g = d_refs[q][...].astype(F32)
            for t in range(1, c):
                g = g + d_refs[q + t][...].astype(F32)
            cts.append(g)
            q += c
        grads = vjp(tuple(cts))
        drow = list(grads[:nd])
        for (idx, _), r in zip(extra, ex_refs):
            drow[idx] = drow[idx] + r[...].astype(F32)
        for o, v in zip(drow_refs, drow):
            o[...] = v.astype(o.dtype)

        @pl.when(pl.program_id(0) == 0)
        def _():
            for o in dpar_refs:
                o[...] = jnp.zeros_like(o)

        for o, v in zip(dpar_refs, grads[nd:]):
            o[...] += v

    ex_arrs = [a for _, a in extra]
    in_specs = ([pl.BlockSpec((tm, r.shape[1]), lambda i: (i, 0)) for r in rows]
                + [_full_spec(p) for p in list(params) + list(consts)]
                + [pl.BlockSpec((tm, a.shape[1]), lambda i: (i, 0)) for a in flat_d + ex_arrs])
    out_specs = ([pl.BlockSpec((tm, r.shape[1]), lambda i: (i, 0)) for r in rows[:nd]]
                 + [_full_spec(p) for p in params])
    out_shape = ([jax.ShapeDtypeStruct(r.shape, d) for r, d in zip(rows[:nd], drow_dtypes or [F32] * nd)]
                 + [jax.ShapeDtypeStruct(p.shape, F32) for p in params])
    res = pl.pallas_call(
        body, name=name, grid=(N // tm,), in_specs=in_specs, out_specs=out_specs, out_shape=out_shape,
        compiler_params=_params(("arbitrary",)))(*rows, *params, *consts, *flat_d, *ex_arrs)
    return list(res[:nd]), list(res[nd:])


def f_rms(x, g):
    return (_rms(x, g),)


def f_rope(qq, kv, krr, ct, st):
    hw = MLA_HEADS * HEAD_PAD
    c8 = jnp.tile(ct, (1, MLA_HEADS))
    s8 = jnp.tile(st, (1, MLA_HEADS))
    q = qq[:, :hw] * c8 + qq[:, hw:] * s8
    kr = krr[:, :HEAD_PAD] * ct + krr[:, HEAD_PAD:] * st
    lane = lax.broadcasted_iota(jnp.int32, kv.shape, 1) % HEAD_PAD
    k = jnp.where(lane < QK_NOPE, kv, jnp.tile(kr, (1, MLA_HEADS)))
    return q, k


def f_sg(p, ln_g, ln_b, w, bias):
    z = 0.5 * p * (1.0 + jnp.tanh(0.7978845608028654 * (p + 0.044715 * p * p * p)))
    u, v = z[:, :SG_DIM], z[:, SG_DIM:]
    mu = jnp.mean(v, axis=-1, keepdims=True)
    var = jnp.mean(jnp.square(v - mu), axis=-1, keepdims=True)
    v = (v - mu) * lax.rsqrt(var + 1e-5) * ln_g + ln_b
    lane = lax.broadcasted_iota(jnp.int32, (SG_CHUNK, LANES), 1)
    outs = []
    for c in range(p.shape[0] // SG_CHUNK):
        vc = v[c * SG_CHUNK:(c + 1) * SG_CHUNK]
        cols = []
        for m in range(SG_DIM // LANES):
            blk = vc[:, m * LANES:(m + 1) * LANES]
            cols.append(jnp.where(lane < 64, mm(w[2 * m], blk), mm(w[2 * m + 1], blk)))
        outs.append(jnp.concatenate(cols, axis=1) + bias)
    mixed = outs[0] if len(outs) == 1 else jnp.concatenate(outs, axis=0)
    return (u * mixed,)


def f_rw_pre(z, zp, zn, mu, w0, w2, a0, a2, g2, k_k, k_a, ones):
    z = z + mu * (0.5 * (zp + zn) - z)
    C = RWKV_DIM
    r, k, v = z[:, :C], z[:, C:2 * C], z[:, 2 * C:3 * C]
    wl, al, gl = z[:, 3 * C:3 * C + 128], z[:, 3 * C + 128:3 * C + 256], z[:, 3 * C + 256:]
    w = w0 + mm(jnp.tanh(wl), w2)
    decay = jnp.exp(-0.6065306597126334 * _sigmoid(w))
    a = _sigmoid(a0 + mm(al, a2))
    g = mm(_sigmoid(gl), g2)
    kk = k * k_k
    kk = kk / jnp.maximum(jnp.sqrt(segsum(kk * kk, ones)), 1e-12)
    k2 = jnp.concatenate([k, k], axis=1)
    kdir = k2 * (1.0 + (a - 1.0) * jnp.concatenate([k_a, k_a], axis=1))
    bdir = jnp.concatenate([kk, kk], axis=1) * a
    return r, v, decay, kdir, kk, bdir, g


def f_rw_post(y, r, v, kdir, g, ln_g, ln_b, r_k, ones):
    mean = segsum(y, ones) * (1.0 / RWKV_HEAD)
    yc = y - mean
    var = segsum(yc * yc, ones) * (1.0 / RWKV_HEAD)
    y = yc * lax.rsqrt(var + GN_EPS) * ln_g + ln_b
    C = RWKV_DIM
    bonus = segsum(r * kdir[:, :C] * r_k, ones) + segsum(r * kdir[:, C:] * r_k, ones)
    return ((y + bonus * v) * g,)


def f_merge(pg, b0, b1, b2, gate_b):
    D = b0.shape[1]
    gt = _sigmoid(pg + gate_b)
    return (gt[:, :D] * b0 + gt[:, D:2 * D] * b1 + gt[:, 2 * D:] * b2,)


def f_swiglu(au):
    a, u = au[:, :D_FF], au[:, D_FF:]
    return (a * _sigmoid(a) * u,)


def f_add3(a, b, c):
    return (a + b + c,)


def loss_head(x, tgt, g, tm):
    N, D = x.shape

    def body(x_ref, t_ref, g_ref, loss_ref, dx_ref, dg_ref):
        t = t_ref[...]

        def f(xv, gv):
            err = _rms(xv, gv) - t
            return 0.5 * jnp.sum(jnp.mean(err * err, axis=-1, keepdims=True))

        val, (dx, dg) = jax.value_and_grad(f, argnums=(0, 1))(x_ref[...], g_ref[...])
        dx_ref[...] = dx

        @pl.when(pl.program_id(0) == 0)
        def _():
            loss_ref[...] = jnp.zeros_like(loss_ref)
            dg_ref[...] = jnp.zeros_like(dg_ref)

        loss_ref[...] += jnp.full(loss_ref.shape, val, F32)
        dg_ref[...] += dg

    row = pl.BlockSpec((tm, D), lambda i: (i, 0))
    return pl.pallas_call(
        body, name="loss_head", grid=(N // tm,), in_specs=[row, row, _full_spec(g)],
        out_specs=[pl.BlockSpec((1, LANES), lambda i: (0, 0)), row, _full_spec(g)],
        out_shape=[jax.ShapeDtypeStruct((1, LANES), F32), jax.ShapeDtypeStruct((N, D), F32),
                   jax.ShapeDtypeStruct(g.shape, F32)],
        compiler_params=_params(("arbitrary",)))(x, tgt, g)


ATT_SCALE = float((QK_NOPE + QK_ROPE) ** -0.5)


def _attn_block(q, k, kv):
    s = mm_nt(q, k) * ATT_SCALE
    m = lax.stop_gradient(jnp.max(s, axis=-1, keepdims=True))
    e = jnp.exp(s - m)
    return mm(e, kv) * (1.0 / jnp.sum(e, axis=-1, keepdims=True))


def attention_fwd(name, q, k, kv, B, S, tq, out_dtype):
    nq = S // tq
    qspec = pl.BlockSpec((tq, HEAD_PAD), lambda b, h, i: (b * nq + i, h))
    kspec = pl.BlockSpec((S, HEAD_PAD), lambda b, h, i: (b, h))

    def body(q_ref, k_ref, kv_ref, o_ref):
        o_ref[...] = _attn_block(q_ref[...], k_ref[...], kv_ref[...]).astype(o_ref.dtype)

    return pl.pallas_call(
        body, name=name, grid=(B, MLA_HEADS, nq), in_specs=[qspec, kspec, kspec], out_specs=qspec,
        out_shape=jax.ShapeDtypeStruct(q.shape, out_dtype),
        compiler_params=_params(("parallel", "parallel", "arbitrary")))(q, k, kv)


def attention_bwd(name, q, k, kv, do, B, S, tq):
    nq = S // tq
    qspec = pl.BlockSpec((tq, HEAD_PAD), lambda b, h, i: (b * nq + i, h))
    kspec = pl.BlockSpec((S, HEAD_PAD), lambda b, h, i: (b, h))

    def body(q_ref, k_ref, kv_ref, do_ref, dq_ref, dk_ref, dkv_ref):
        _, vjp = jax.vjp(_attn_block, q_ref[...], k_ref[...], kv_ref[...])
        dq, dk, dkv = vjp(do_ref[...])
        dq_ref[...] = dq

        @pl.when(pl.program_id(2) == 0)
        def _():
            dk_ref[...] = jnp.zeros_like(dk_ref)
            dkv_ref[...] = jnp.zeros_like(dkv_ref)

        dk_ref[...] += dk
        dkv_ref[...] += dkv

    sh = jax.ShapeDtypeStruct(q.shape, F32)
    return pl.pallas_call(
        body, name=name, grid=(B, MLA_HEADS, nq), in_specs=[qspec, kspec, kspec, qspec],
        out_specs=[qspec, kspec, kspec], out_shape=[sh, sh, sh],
        compiler_params=_params(("parallel", "parallel", "arbitrary")))(q, k, kv, do)


SCAN_TC = 8
SCAN_UNROLL = 16


def _jloop(n, body, init):
    def outer(o, c):
        for u in range(SCAN_UNROLL):
            c = body(o * SCAN_UNROLL + u, c)
        return c

    return lax.fori_loop(0, n // SCAN_UNROLL, outer, init)


def _dir_mask(L, Ip):
    per_dir = L // (2 * (RWKV_HEAD // Ip))
    lane = lax.broadcasted_iota(jnp.int32, (1, L), 1)
    return (lane // per_dir) % 2 == 1


def _merge_dirs(mask, fwd_ref, rev_ref, out_ref):
    for tt in range(SCAN_TC):
        out_ref[tt] = jnp.where(mask, rev_ref[SCAN_TC - 1 - tt], fwd_ref[tt])


def scan_fwd(name, w, k, b, kk, r, v):
    T, J, L = w.shape
    Ip = v.shape[1]
    nT = T // SCAN_TC
    fwd3, rev3 = (lambda g: (g, 0, 0)), (lambda g: (nT - 1 - g, 0, 0))
    jf, jr = pl.BlockSpec((SCAN_TC, J, L), fwd3), pl.BlockSpec((SCAN_TC, J, L), rev3)
    i_f, i_r = pl.BlockSpec((SCAN_TC, Ip, L), fwd3), pl.BlockSpec((SCAN_TC, Ip, L), rev3)
    sspec = pl.BlockSpec((SCAN_TC, J, Ip, L), lambda g: (g, 0, 0, 0))
    last_spec = pl.BlockSpec((J, Ip, L), lambda g: (0, 0, 0))

    def body(wf, wr, kf, kr, bf, br, kkf, kkr, rf, rr, vf, vr, yf_ref, yr_ref, sp_ref, sa_ref, last_ref,
             s_ref, w_ref, k_ref, b_ref, kk_ref, r_ref, v_ref):
        @pl.when(pl.program_id(0) == 0)
        def _():
            s_ref[...] = jnp.zeros_like(s_ref)

        mask = _dir_mask(L, Ip)
        for f_, r_, m_ in ((wf, wr, w_ref), (kf, kr, k_ref), (bf, br, b_ref), (kkf, kkr, kk_ref), (rf, rr, r_ref),
                           (vf, vr, v_ref)):
            _merge_dirs(mask, f_, r_, m_)

        def row(ref, tt, j):
            return jnp.broadcast_to(ref[tt, pl.ds(j, 1), :], (Ip, L))

        def step(tt, carry):
            def p1(j, sa):
                s = s_ref[j]
                sp_ref[tt, j] = s
                return sa + s * row(kk_ref, tt, j)

            sa = _jloop(J, p1, jnp.zeros((Ip, L), F32))
            sa_ref[tt] = sa
            vt = v_ref[tt]

            def p2(j, y):
                s = s_ref[j] * row(w_ref, tt, j) - sa * row(b_ref, tt, j) + vt * row(k_ref, tt, j)
                s_ref[j] = s
                return y + s * row(r_ref, tt, j)

            y = _jloop(J, p2, jnp.zeros((Ip, L), F32))
            yf_ref[tt] = y
            yr_ref[SCAN_TC - 1 - tt] = y
            return carry

        lax.fori_loop(0, SCAN_TC, step, 0)

        @pl.when(pl.program_id(0) == nT - 1)
        def _():
            last_ref[...] = s_ref[...]

    ish = jax.ShapeDtypeStruct((T, Ip, L), F32)
    jscr = pltpu.VMEM((SCAN_TC, J, L), F32)
    return pl.pallas_call(
        body, name=name, grid=(nT,), in_specs=[jf, jr] * 5 + [i_f, i_r],
        out_specs=[i_f, i_r, sspec, i_f, last_spec],
        out_shape=[ish, ish, jax.ShapeDtypeStruct((T, J, Ip, L), F32), ish, jax.ShapeDtypeStruct((J, Ip, L), F32)],
        scratch_shapes=[pltpu.VMEM((J, Ip, L), F32)] + [jscr] * 5 + [pltpu.VMEM((SCAN_TC, Ip, L), F32)],
        compiler_params=_params(("arbitrary",)))(w, w, k, k, b, b, kk, kk, r, r, v, v)


def scan_bwd(name, w, k, b, kk, r, v, sp, sa_all, s_last, dy):
    T, J, L = w.shape
    Ip = v.shape[1]
    nT = T // SCAN_TC
    stp3, mir3 = (lambda g: (nT - 1 - g, 0, 0)), (lambda g: (g, 0, 0))
    jf, jr = pl.BlockSpec((SCAN_TC, J, L), stp3), pl.BlockSpec((SCAN_TC, J, L), mir3)
    i_f, i_r = pl.BlockSpec((SCAN_TC, Ip, L), stp3), pl.BlockSpec((SCAN_TC, Ip, L), mir3)
    sspec = pl.BlockSpec((SCAN_TC, J, Ip, L), lambda g: (nT - 1 - g, 0, 0, 0))
    last_spec = pl.BlockSpec((J, Ip, L), lambda g: (0, 0, 0))

    def body(wf, wr, kf, kr, bf, br, kkf, kkr, rf, rr, vf, vr, dyf, dyr, sp_ref, sa_ref, last_ref,
             dwf, dwr, dkf, dkr, dbf, dbr, dkkf, dkkr, drf, drr, dvf, dvr,
             ds_ref, nxt_ref, w_ref, k_ref, b_ref, kk_ref, r_ref, v_ref, dy_ref):
        @pl.when(pl.program_id(0) == 0)
        def _():
            ds_ref[...] = jnp.zeros_like(ds_ref)
            nxt_ref[...] = last_ref[...]

        mask = _dir_mask(L, Ip)
        for f_, r_, m_ in ((wf, wr, w_ref), (kf, kr, k_ref), (bf, br, b_ref), (kkf, kkr, kk_ref), (rf, rr, r_ref),
                           (vf, vr, v_ref), (dyf, dyr, dy_ref)):
            _merge_dirs(mask, f_, r_, m_)

        def row(ref, tt, j):
            return jnp.broadcast_to(ref[tt, pl.ds(j, 1), :], (Ip, L))

        def rsum(x):
            return jnp.sum(x, axis=0, keepdims=True)

        def make_step(first):
            def step(n, carry):
                tt = SCAN_TC - 1 - n
                dyt, vt, sa = dy_ref[tt], v_ref[tt], sa_ref[tt]

                def p1(j, c):
                    dsa, dv = c
                    ds = ds_ref[j] + dyt * row(r_ref, tt, j)
                    ds_ref[j] = ds
                    return dsa - ds * row(b_ref, tt, j), dv + ds * row(k_ref, tt, j)

                z = jnp.zeros((Ip, L), F32)
                dsa, dv = _jloop(J, p1, (z, z))
                dvf[tt] = dv
                dvr[SCAN_TC - 1 - tt] = dv

                def put(f_ref, r_ref_, j, val):
                    f_ref[tt, pl.ds(j, 1), :] = val
                    r_ref_[SCAN_TC - 1 - tt, pl.ds(j, 1), :] = val

                def p2(j, c):
                    ds = ds_ref[j]
                    s0 = sp_ref[tt, j]
                    s1 = nxt_ref[j] if first else sp_ref[tt + 1, j]
                    put(drf, drr, j, rsum(s1 * dyt))
                    put(dkf, dkr, j, rsum(ds * vt))
                    put(dbf, dbr, j, -rsum(ds * sa))
                    put(dwf, dwr, j, rsum(ds * s0))
                    put(dkkf, dkkr, j, rsum(s0 * dsa))
                    ds_ref[j] = ds * row(w_ref, tt, j) + dsa * row(kk_ref, tt, j)
                    return c

                _jloop(J, p2, 0)
                return carry

            return step

        make_step(True)(0, 0)
        lax.fori_loop(1, SCAN_TC, make_step(False), 0)
        nxt_ref[...] = sp_ref[0]

    jsh = jax.ShapeDtypeStruct((T, J, L), F32)
    ish = jax.ShapeDtypeStruct((T, Ip, L), F32)
    jscr = pltpu.VMEM((SCAN_TC, J, L), F32)
    iscr = pltpu.VMEM((SCAN_TC, Ip, L), F32)
    return pl.pallas_call(
        body, name=name, grid=(nT,), in_specs=[jf, jr] * 5 + [i_f, i_r] * 2 + [sspec, i_f, last_spec],
        out_specs=[jf, jr] * 5 + [i_f, i_r], out_shape=[jsh] * 10 + [ish] * 2,
        scratch_shapes=[pltpu.VMEM((J, Ip, L), F32)] * 2 + [jscr] * 5 + [iscr] * 2,
        compiler_params=_params(("arbitrary",)))(w, w, k, k, b, b, kk, kk, r, r, v, v, dy, dy, sp, sa_all, s_last)


LAYOUT_TT = 32
LAYOUT_U = 8


def _lane_group(L, n):
    return lax.broadcasted_iota(jnp.int32, (1, L), 1) // (L // n)


def to_scan(name, x, B, S, nd, Ip):
    isplit = LANES // (2 * B * RWKV_HEADS)
    tt_n = _pick(S, (LAYOUT_TT, 16, 8))
    x5 = x.reshape(B, S, nd, RWKV_HEADS, RWKV_HEAD)
    rows_out = RWKV_HEAD if Ip is None else Ip

    def body(x_ref, o_ref):
        group = _lane_group(LANES, isplit)

        def chunk(c, carry):
            t0 = c * LAYOUT_U
            z = jnp.concatenate([x_ref[b, t0 + u, min(d, nd - 1)] for u in range(LAYOUT_U) for _ in range(isplit)
                                 for d in range(2) for b in range(B)], axis=0)
            mt = z.T
            for u in range(LAYOUT_U):
                m = mt[:, u * LANES:(u + 1) * LANES]
                if Ip is not None:
                    m = sum(jnp.where(group == i2, m[i2 * Ip:(i2 + 1) * Ip], 0.0) for i2 in range(isplit))
                o_ref[t0 + u] = m
            return carry

        lax.fori_loop(0, tt_n // LAYOUT_U, chunk, 0)

    return pl.pallas_call(
        body, name=name, grid=(S // tt_n,),
        in_specs=[pl.BlockSpec((B, tt_n, nd, RWKV_HEADS, RWKV_HEAD), lambda g: (0, g, 0, 0, 0))],
        out_specs=pl.BlockSpec((tt_n, rows_out, LANES), lambda g: (g, 0, 0)),
        out_shape=jax.ShapeDtypeStruct((S, rows_out, LANES), F32), compiler_params=_params(("parallel",)))(x5)


def from_scan(name, f, r, B, S, nd, i_indexed):
    rows_in = f.shape[1]
    isplit = LANES // (2 * B * RWKV_HEADS)
    Ip = rows_in if i_indexed else RWKV_HEAD // isplit
    tt_n = _pick(S, (LAYOUT_TT, 16, 8))
    per_i2 = LANES // isplit

    def body(f_ref, r_ref, o_ref):
        mask = _dir_mask(LANES, Ip)
        group = _lane_group(LANES, isplit)

        def chunk(c, carry):
            t0 = c * LAYOUT_U
            ms = []
            for u in range(LAYOUT_U):
                m = jnp.where(mask, r_ref[t0 + u], f_ref[t0 + u])
                if i_indexed:
                    m = jnp.concatenate([jnp.where(group == i2, m, 0.0) for i2 in range(isplit)], axis=0)
                ms.append(m)
            zt = jnp.concatenate(ms, axis=1).T
            for u in range(LAYOUT_U):
                z = zt[u * LANES:(u + 1) * LANES]
                zf = sum(z[i2 * per_i2:(i2 + 1) * per_i2] for i2 in range(isplit))
                for b in range(B):
                    d0 = zf[b * RWKV_HEADS:(b + 1) * RWKV_HEADS]
                    d1 = zf[(B + b) * RWKV_HEADS:(B + b + 1) * RWKV_HEADS]
                    if nd == 1:
                        o_ref[b, t0 + u, 0] = d0 + d1
                    else:
                        o_ref[b, t0 + u, 0] = d0
                        o_ref[b, t0 + u, 1] = d1
            return carry

        lax.fori_loop(0, tt_n // LAYOUT_U, chunk, 0)

    spec = pl.BlockSpec((tt_n, rows_in, LANES), lambda g: (g, 0, 0))
    out = pl.pallas_call(
        body, name=name, grid=(S // tt_n,), in_specs=[spec, spec],
        out_specs=pl.BlockSpec((B, tt_n, nd, RWKV_HEADS, RWKV_HEAD), lambda g: (0, g, 0, 0, 0)),
        out_shape=jax.ShapeDtypeStruct((B, S, nd, RWKV_HEADS, RWKV_HEAD), F32),
        compiler_params=_params(("parallel",)))(f, r)
    return out.reshape(B * S, nd * RWKV_DIM)


def adamw(name, w, g, m, v):
    R, C = w.shape
    tr = _pick(R, (256, 128, 64, 32, 16, 8))
    c1 = 1.0 - ADAM_B1 ** ADAM_STEP
    c2 = 1.0 - ADAM_B2 ** ADAM_STEP

    def body(w_ref, g_ref, m_ref, v_ref, d_ref, nm_ref, nv_ref):
        gv = g_ref[...]
        nm = ADAM_B1 * m_ref[...] + (1.0 - ADAM_B1) * gv
        nv = ADAM_B2 * v_ref[...] + (1.0 - ADAM_B2) * jnp.square(gv)
        d_ref[...] = -ADAM_LR * ((nm / c1) / (jnp.sqrt(nv / c2) + ADAM_EPS) + ADAM_WD * w_ref[...])
        nm_ref[...] = nm
        nv_ref[...] = nv

    spec = pl.BlockSpec((tr, C), lambda i: (i, 0))
    sh = jax.ShapeDtypeStruct((R, C), F32)
    return pl.pallas_call(body, name=name, grid=(R // tr,), in_specs=[spec] * 4, out_specs=[spec] * 3,
                          out_shape=[sh] * 3, compiler_params=_params(("parallel",)))(w, g, m, v)


def sum_slots(name, x):
    n, R, C = x.shape
    tr = _pick(R, (256, 128, 64, 32, 16, 8))

    def body(x_ref, o_ref):
        acc = x_ref[0].astype(F32)
        for s in range(1, n):
            acc = acc + x_ref[s].astype(F32)
        o_ref[...] = acc

    return pl.pallas_call(
        body, name=name, grid=(R // tr,), in_specs=[pl.BlockSpec((n, tr, C), lambda i: (0, i, 0))],
        out_specs=pl.BlockSpec((tr, C), lambda i: (i, 0)), out_shape=jax.ShapeDtypeStruct((R, C), F32),
        compiler_params=_params(("parallel",)))(x)


ANY = pl.BlockSpec(memory_space=pl.ANY)


def _xyc():
    return lax.axis_index("x"), lax.axis_index("y"), lax.axis_index("c")


def gather_shards(shard):
    _, R, C = shard.shape

    def body(x_ref, out_ref, send_sems, recv_sems, local_sem):
        x, y, c = _xyc()
        me, sibling = (x, y, c), (x, y, 1 - c)
        chips = [(1 - x, y), (x, 1 - y), (1 - x, 1 - y)]

        def cp(k, cx, cy, half, to, src=None):
            dst = out_ref.at[2 * cx + cy, half]
            return pltpu.make_async_remote_copy(
                src_ref=dst if src is None else src, dst_ref=dst, send_sem=send_sems.at[k],
                recv_sem=recv_sems.at[k], device_id=to, device_id_type=MESH)

        mine = pltpu.make_async_copy(x_ref, out_ref.at[2 * x + y], local_sem)
        mine.start()
        first = [cp(j, x, y, c, (*chip, c), src=x_ref.at[c]) for j, chip in enumerate(chips)]
        for f in first:
            f.start()
        passed = [cp(3 + j, *chip, c, sibling) for j, chip in enumerate(chips)]
        for j, chip in enumerate(chips):
            cp(j, *chip, c, me).wait_recv()
            passed[j].start()
        for j, chip in enumerate(chips):
            cp(3 + j, *chip, 1 - c, me).wait_recv()
        for f in first + passed:
            f.wait_send()
        mine.wait()

    return pl.pallas_call(
        body, name="gather_shards", in_specs=[ANY], out_specs=ANY,
        out_shape=jax.ShapeDtypeStruct((4, 2, R, C), shard.dtype),
        scratch_shapes=[pltpu.SemaphoreType.DMA((6,)), pltpu.SemaphoreType.DMA((6,)), pltpu.SemaphoreType.DMA])(shard)


FLIPS = [(0, 0, 1), (0, 1, 0), (0, 1, 1), (1, 0, 0), (1, 0, 1), (1, 1, 0), (1, 1, 1)]


def scatter_partials(g):
    _, _, R, C = g.shape

    def body(g_ref, out_ref, send_sems, recv_sems, local_sem):
        x, y, c = _xyc()
        me_idx = 4 * x + 2 * y + c
        mine = pltpu.make_async_copy(g_ref.at[2 * x + y, c], out_ref.at[me_idx], local_sem)
        mine.start()
        sends = []
        for k, (fx, fy, fc) in enumerate(FLIPS):
            px, py, pc = (x + fx) % 2, (y + fy) % 2, (c + fc) % 2
            s = pltpu.make_async_remote_copy(
                src_ref=g_ref.at[2 * px + py, pc], dst_ref=out_ref.at[me_idx], send_sem=send_sems.at[k],
                recv_sem=recv_sems.at[k], device_id=(px, py, pc), device_id_type=MESH)
            s.start()
            sends.append(s)
        for k, (fx, fy, fc) in enumerate(FLIPS):
            px, py, pc = (x + fx) % 2, (y + fy) % 2, (c + fc) % 2
            slot = out_ref.at[4 * px + 2 * py + pc]
            pltpu.make_async_remote_copy(
                src_ref=slot, dst_ref=slot, send_sem=send_sems.at[k], recv_sem=recv_sems.at[k],
                device_id=(px, py, pc), device_id_type=MESH).wait_recv()
        for s in sends:
            s.wait_send()
        mine.wait()

    return pl.pallas_call(
        body, name="scatter_partials", in_specs=[ANY], out_specs=ANY,
        out_shape=jax.ShapeDtypeStruct((8, R, C), g.dtype),
        scratch_shapes=[pltpu.SemaphoreType.DMA((7,)), pltpu.SemaphoreType.DMA((7,)), pltpu.SemaphoreType.DMA])(g)


JOIN_ROWS = 256


def sum_join(x):
    n, R, C = x.shape
    steps = R // JOIN_ROWS

    def body(x_ref, out_ref, acc, local_sems, send_sems, recv_sem):
        i = pl.program_id(0)
        slot = lax.rem(i, 2)
        xx, yy, c = _xyc()
        sibling = (xx, yy, 1 - c)

        def copies(k, s):
            dst = out_ref.at[c, pl.ds(k * JOIN_ROWS, JOIN_ROWS)]
            return (pltpu.make_async_copy(acc.at[s], dst, local_sems.at[s]),
                    pltpu.make_async_remote_copy(src_ref=acc.at[s], dst_ref=dst, send_sem=send_sems.at[s],
                                                 recv_sem=recv_sem, device_id=sibling, device_id_type=MESH))

        def drain(k, s):
            loc, rem = copies(k, s)
            loc.wait()
            rem.wait_send()

        @pl.when(i >= 2)
        def _():
            drain(i - 2, slot)

        a = x_ref[0].astype(F32)
        for s in range(1, n):
            a = a + x_ref[s].astype(F32)
        acc[slot] = a
        loc, rem = copies(i, slot)
        loc.start()
        rem.start()

        @pl.when(i == steps - 1)
        def _():
            if steps >= 2:
                drain(i - 1, 1 - slot)
            drain(i, slot)
            theirs = out_ref.at[1 - c]
            pltpu.make_async_remote_copy(src_ref=theirs, dst_ref=theirs, send_sem=send_sems.at[0], recv_sem=recv_sem,
                                         device_id=sibling, device_id_type=MESH).wait_recv()

    return pl.pallas_call(
        body, name="sum_join", grid=(steps,),
        in_specs=[pl.BlockSpec((n, JOIN_ROWS, C), lambda i: (0, i, 0))], out_specs=ANY,
        out_shape=jax.ShapeDtypeStruct((2, R, C), F32),
        scratch_shapes=[pltpu.VMEM((2, JOIN_ROWS, C), F32), pltpu.SemaphoreType.DMA((2,)),
                        pltpu.SemaphoreType.DMA((2,)), pltpu.SemaphoreType.DMA],
        compiler_params=_params(("arbitrary",)))(x)


def gather_all(name, block):
    R, C = block.shape

    def body(x_ref, out_ref, send_sems, recv_sems, local_sem):
        x, y, c = _xyc()
        mine = pltpu.make_async_copy(x_ref, out_ref.at[4 * x + 2 * y + c], local_sem)
        mine.start()
        sends = []
        for k, (fx, fy, fc) in enumerate(FLIPS):
            px, py, pc = (x + fx) % 2, (y + fy) % 2, (c + fc) % 2
            s = pltpu.make_async_remote_copy(
                src_ref=x_ref, dst_ref=out_ref.at[4 * x + 2 * y + c], send_sem=send_sems.at[k],
                recv_sem=recv_sems.at[k], device_id=(px, py, pc), device_id_type=MESH)
            s.start()
            sends.append(s)
        for k, (fx, fy, fc) in enumerate(FLIPS):
            px, py, pc = (x + fx) % 2, (y + fy) % 2, (c + fc) % 2
            slot = out_ref.at[4 * px + 2 * py + pc]
            pltpu.make_async_remote_copy(
                src_ref=slot, dst_ref=slot, send_sem=send_sems.at[k], recv_sem=recv_sems.at[k],
                device_id=(px, py, pc), device_id_type=MESH).wait_recv()
        for s in sends:
            s.wait_send()
        mine.wait()

    return pl.pallas_call(
        body, name=name, in_specs=[ANY], out_specs=ANY,
        out_shape=jax.ShapeDtypeStruct((8, R, C), block.dtype),
        scratch_shapes=[pltpu.SemaphoreType.DMA((7,)), pltpu.SemaphoreType.DMA((7,)), pltpu.SemaphoreType.DMA])(block)


PACK_C = 1024


def _pack(arrs, row_mult):
    flat = jnp.concatenate([a.reshape(-1) for a in arrs])
    n = flat.shape[0]
    rows = -(-n // PACK_C)
    rows = -(-rows // row_mult) * row_mult
    return jnp.pad(flat, (0, rows * PACK_C - n)).reshape(rows, PACK_C)


def _unpack(buf, shapes):
    flat = buf.reshape(-1)
    out, off = [], 0
    for s in shapes:
        n = int(np.prod(s))
        out.append(flat[off:off + n].reshape(s))
        off += n
    return out


OFF_Q, OFF_CKV, OFF_KR, OFF_SG, OFF_RW, OFF_GATE, N_IN_PAD = 0, 384, 640, 896, 1920, 3840, 6912
IN_SEGMENTS = [('q', OFF_Q, OFF_CKV), ('ckv', OFF_CKV, OFF_KR), ('kr', OFF_KR, OFF_SG), ('sg', OFF_SG, OFF_RW),
               ('rw', OFF_RW, OFF_GATE), ('gate', OFF_GATE, N_IN_PAD)]
ROPE_LANE = QK_NOPE
HALF = QK_ROPE // 2


def _win_layout():
    src = np.full((N_IN_PAD,), -1, np.int64)
    sgn = np.ones((N_IN_PAD,), np.float32)
    src[0:640] = np.arange(0, 640)
    kr0 = Q_LORA + KV_LORA
    src[OFF_KR + ROPE_LANE:OFF_KR + ROPE_LANE + QK_ROPE] = kr0 + np.arange(QK_ROPE)
    sw = OFF_KR + HEAD_PAD + ROPE_LANE
    src[sw:sw + HALF] = kr0 + HALF + np.arange(HALF)
    sgn[sw:sw + HALF] = -1.0
    src[sw + HALF:sw + QK_ROPE] = kr0 + np.arange(HALF)
    src[OFF_SG:N_IN_PAD] = 672 + np.arange(N_IN_PAD - OFF_SG)
    return src, sgn


def _wuq_layout():
    hw = MLA_HEADS * HEAD_PAD
    src = np.full((2 * hw,), -1, np.int64)
    sgn = np.ones((2 * hw,), np.float32)
    per = QK_NOPE + QK_ROPE
    for h in range(MLA_HEADS):
        src[h * HEAD_PAD:h * HEAD_PAD + per] = h * per + np.arange(per)
        sw = hw + h * HEAD_PAD + ROPE_LANE
        src[sw:sw + HALF] = h * per + QK_NOPE + HALF + np.arange(HALF)
        sgn[sw:sw + HALF] = -1.0
        src[sw + HALF:sw + QK_ROPE] = h * per + QK_NOPE + np.arange(HALF)
    return src, sgn


def _runs(idx, sgn):
    out, lo = [], 0
    for pos in range(1, len(idx) + 1):
        if pos == len(idx) or not (
                (idx[pos] == -1 and idx[pos - 1] == -1)
                or (idx[pos - 1] >= 0 and idx[pos] == idx[pos - 1] + 1 and sgn[pos] == sgn[pos - 1])):
            out.append((lo, pos, int(idx[lo]), float(sgn[lo])))
            lo = pos
    return out


def _select_cols(w, idx, sgn):
    pieces = []
    for lo, hi, s0, sg in _runs(idx, sgn):
        if s0 < 0:
            pieces.append(jnp.zeros((w.shape[0], hi - lo), w.dtype))
        else:
            piece = w[:, s0:s0 + hi - lo]
            pieces.append(piece if sg > 0 else -piece)
    return jnp.concatenate(pieces, axis=1)


def _permute_cols(w, src, sgn):
    return _select_cols(w, src, sgn)


def _unpermute_full(dw, src, sgn, n_cols):
    first = np.full((n_cols,), -1, np.int64)
    second = np.full((n_cols,), -1, np.int64)
    for pos, s in enumerate(src):
        if s < 0:
            continue
        if first[s] < 0:
            first[s] = pos
        else:
            second[s] = pos
    sg2 = np.where(second >= 0, sgn[np.maximum(second, 0)], 1.0)
    return _select_cols(dw, first, sgn[first]), _select_cols(dw, second, sg2)


def _blockdiag(w):
    z = jnp.zeros_like(w[0])
    return jnp.concatenate([jnp.concatenate([w[0], z], axis=1), jnp.concatenate([z, w[1]], axis=1)], axis=0)


def _rope_tables(pos):
    inv = 1.0 / (ROPE_THETA ** (jnp.arange(0, QK_ROPE, 2, dtype=F32) / QK_ROPE))
    ang = pos.astype(F32)[:, None] * inv[None, :]
    cos, sin = jnp.cos(ang), jnp.sin(ang)
    pad = lambda t, fill: jnp.concatenate(
        [jnp.full((t.shape[0], ROPE_LANE), fill, F32), t, t, jnp.full((t.shape[0], HEAD_PAD - ROPE_LANE - QK_ROPE), fill, F32)], axis=1)
    return pad(cos, 1.0), pad(sin, 0.0)


def kernel(x, positions, attn_norm_g, w_in, gate_b, q_norm_g, w_uq, kv_norm_g, w_ukv, sg_ln_g, sg_ln_b, sg_w, sg_b, rw_mu, rw_w0, rw_w2, rw_a0, rw_a2, rw_g2, rw_k_k, rw_k_a, rw_r_k, rw_ln_g, rw_ln_b, w_branch, w_out, ffn_norm_g, w_ffn_gate, w_ffn_up, w_ffn_down, final_norm_g, loss_target, m_attn_norm_g, m_w_in, m_gate_b, m_q_norm_g, m_w_uq, m_kv_norm_g, m_w_ukv, m_sg_ln_g, m_sg_ln_b, m_sg_w, m_sg_b, m_rw_mu, m_rw_w0, m_rw_w2, m_rw_a0, m_rw_a2, m_rw_g2, m_rw_k_k, m_rw_k_a, m_rw_r_k, m_rw_ln_g, m_rw_ln_b, m_w_branch, m_w_out, m_ffn_norm_g, m_w_ffn_gate, m_w_ffn_up, m_w_ffn_down, m_final_norm_g, v_attn_norm_g, v_w_in, v_gate_b, v_q_norm_g, v_w_uq, v_kv_norm_g, v_w_ukv, v_sg_ln_g, v_sg_ln_b, v_sg_w, v_sg_b, v_rw_mu, v_rw_w0, v_rw_w2, v_rw_a0, v_rw_a2, v_rw_g2, v_rw_k_k, v_rw_k_a, v_rw_r_k, v_rw_ln_g, v_rw_ln_b, v_w_branch, v_w_out, v_ffn_norm_g, v_w_ffn_gate, v_w_ffn_up, v_w_ffn_down, v_final_norm_g):
    args = locals()
    W = {n: args[n] for n in WEIGHTS}
    M1 = {n: args['m_' + n] for n in WEIGHTS}
    M2 = {n: args['v_' + n] for n in WEIGHTS}
    B, S, D = x.shape
    N = B * S
    TM = _pick(N, (256, 128))
    TMH = 128
    TQ = _pick(S, (512, 256, 128))

    shard_shapes = [W[n].shape for n in SHARDED]
    full = {}
    mm_pack = _pack([W[n].astype(BF16) for n in MATMUL_SHARDED], 32)
    Rm = mm_pack.shape[0]
    gathered = gather_shards(mm_pack.reshape(2, Rm // 2, PACK_C)).reshape(4, Rm, PACK_C)
    pieces = [_unpack(gathered[q], [W[n].shape for n in MATMUL_SHARDED]) for q in range(4)]
    for i, n in enumerate(MATMUL_SHARDED):
        full[n] = jnp.concatenate([pieces[q][i] for q in range(4)], axis=SHARD_AXIS[n])
    small = gather_all("gather_small", _pack([W[n] for n in SMALL_SHARDED], 8))
    pieces = [_unpack(small[2 * q], [W[n].shape for n in SMALL_SHARDED]) for q in range(4)]
    for i, n in enumerate(SMALL_SHARDED):
        full[n] = jnp.concatenate([pieces[q][i] for q in range(4)], axis=SHARD_AXIS[n])
    for n in ('rw_w2', 'rw_a2', 'rw_g2'):
        full[n] = full[n].astype(F32)
    for n in REPLICATED:
        full[n] = W[n]

    win_src, win_sgn = _win_layout()
    wuq_src, wuq_sgn = _wuq_layout()
    ones = jnp.asarray(np.kron(np.eye(RWKV_HEADS), np.ones((RWKV_HEAD, RWKV_HEAD))), BF16)
    ct, st = _rope_tables(positions.reshape(N))
    row = lambda v: v.reshape(1, -1)

    H, HD = RWKV_HEADS, RWKV_HEAD
    inst = 2 * B * H
    isplit = LANES // inst
    Ip = HD // isplit
    to_j = lambda nm, c: to_scan(nm, c, B, S, c.shape[1] // RWKV_DIM, None)
    to_i = lambda nm, c: to_scan(nm, c, B, S, 1, Ip)

    def shift_prev(z):
        z = z.reshape(B, S, -1)
        return jnp.pad(z[:, :-1], ((0, 0), (1, 0), (0, 0))).reshape(N, -1)

    def shift_next(z):
        z = z.reshape(B, S, -1)
        return jnp.pad(z[:, 1:], ((0, 0), (0, 1), (0, 0))).reshape(N, -1)

    LW = []
    for l in range(DEPTH):
        wb = full['w_branch'][l]
        wb0 = jnp.zeros((MLA_HEADS, HEAD_PAD, D), F32).at[:, QK_NOPE:].set(wb[0].reshape(MLA_HEADS, V_HEAD, D))
        LW.append(dict(
            attn_g=row(full['attn_norm_g'][l]),
            w_in=_permute_cols(full['w_in'][l], win_src, win_sgn),
            gate_b=row(full['gate_b'][l]),
            q_g=row(full['q_norm_g'][l]),
            w_uq=_permute_cols(full['w_uq'][l], wuq_src, wuq_sgn),
            kv_g=row(full['kv_norm_g'][l]),
            w_ukv=full['w_ukv'][l],
            sg_g=row(full['sg_ln_g'][l]), sg_b=row(full['sg_ln_b'][l]), sg_w=full['sg_w'][l],
            sg_bias=jnp.repeat(full['sg_b'][l].T, SG_DIM // SG_GROUPS, axis=1),
            mu=row(full['rw_mu'][l]), w0=row(full['rw_w0'][l]), w2=_blockdiag(full['rw_w2'][l]),
            a0=row(full['rw_a0'][l]), a2=_blockdiag(full['rw_a2'][l]), g2=full['rw_g2'][l],
            k_k=row(full['rw_k_k'][l]), k_a=row(full['rw_k_a'][l]), r_k=row(full['rw_r_k'][l]),
            ln_g=row(full['rw_ln_g'][l]), ln_b=row(full['rw_ln_b'][l]),
            wb0=wb0.reshape(MLA_HEADS * HEAD_PAD, D), wb1=wb[1], wb2=wb[2],
            w_out=full['w_out'][l], ffn_g=row(full['ffn_norm_g'][l]),
            w_gu=jnp.concatenate([full['w_ffn_gate'][l], full['w_ffn_up'][l]], axis=1),
            w_down=full['w_ffn_down'][l]))

    saved = []
    xc = x.reshape(N, D)
    for l in range(DEPTH):
        p = LW[l]
        t = 'l%d_' % l
        sv = dict(x=xc)
        (h,) = rowwise(t + 'attn_norm', f_rms, [xc], [p['attn_g']], [], [D], TM, [BF16])
        p_q, p_ckv, p_kr, p_sg, z, p_gate = [
            matmul(t + 'in_proj_' + sn, h, p['w_in'][:, a:b], 'nn') for sn, a, b in IN_SEGMENTS]
        sv['h'] = h
        (cq,) = rowwise(t + 'q_norm', f_rms, [p_q], [p['q_g']], [], [Q_LORA], TM, [BF16])
        (ckv,) = rowwise(t + 'kv_norm', f_rms, [p_ckv], [p['kv_g']], [], [KV_LORA], TM, [BF16])
        qq = matmul(t + 'uq', cq, p['w_uq'], 'nn')
        kv = matmul(t + 'ukv', ckv, p['w_ukv'], 'nn')
        qh, kh = rowwise(t + 'rope', f_rope, [qq, kv, p_kr, ct, st], [], [], [MLA_HEADS * HEAD_PAD] * 2, TM)
        ya = attention_fwd(t + 'attn', qh, kh, kv, B, S, TQ, BF16)
        sv.update(p_q=p_q, p_ckv=p_ckv, p_kr=p_kr, cq=cq, ckv=ckv, qq=qq, kv=kv, qh=qh, kh=kh, ya=ya)
        (yb,) = rowwise(t + 'sg', f_sg, [p_sg], [p['sg_g'], p['sg_b'], p['sg_w'], p['sg_bias']], [], [SG_DIM], TM,
                        [BF16])
        sv.update(p_sg=p_sg, yb=yb)
        zp, zn = shift_prev(z), shift_next(z)
        rw_par = [p['mu'], p['w0'], p['w2'], p['a0'], p['a2'], p['g2'], p['k_k'], p['k_a']]
        r_, v_, decay, kdir, kk, bdir, g_ = rowwise(
            t + 'rw_pre', f_rw_pre, [z, zp, zn], rw_par, [ones],
            [RWKV_DIM, RWKV_DIM, 2 * RWKV_DIM, 2 * RWKV_DIM, RWKV_DIM, 2 * RWKV_DIM, RWKV_DIM], TM)
        sc = dict(w=to_j(t + 'lay_w', decay), k=to_j(t + 'lay_k', kdir), b=to_j(t + 'lay_b', bdir),
                  kk=to_j(t + 'lay_kk', kk), r=to_j(t + 'lay_r', r_), v=to_i(t + 'lay_v', v_))
        y_f, y_r, sp, sa_all, s_last = scan_fwd(t + 'scan', sc['w'], sc['k'], sc['b'], sc['kk'], sc['r'], sc['v'])
        ysum = from_scan(t + 'lay_y', y_f, y_r, B, S, 1, True)
        (yc,) = rowwise(t + 'rw_post', f_rw_post, [ysum, r_, v_, kdir, g_], [p['ln_g'], p['ln_b'], p['r_k']],
                        [ones], [RWKV_DIM], TM, [BF16])
        sv.update(z=z, zp=zp, zn=zn, r=r_, v=v_, kdir=kdir, g=g_, sc=sc, sp=sp, sa=sa_all, s_last=s_last,
                  ysum=ysum, yc=yc)
        b0 = matmul(t + 'br0', ya, p['wb0'], 'nn')
        b1 = matmul(t + 'br1', yb, p['wb1'], 'nn')
        b2 = matmul(t + 'br2', yc, p['wb2'], 'nn')
        (merged,) = rowwise(t + 'merge', f_merge, [p_gate, b0, b1, b2], [p['gate_b']], [], [D], TM, [BF16])
        x2 = matmul(t + 'out_proj', merged, p['w_out'], 'nn', add=xc)
        sv.update(p_gate=p_gate, b0=b0, b1=b1, b2=b2, merged=merged, x2=x2)
        (h2,) = rowwise(t + 'ffn_norm', f_rms, [x2], [p['ffn_g']], [], [D], TM, [BF16])
        au = matmul(t + 'ffn_in', h2, p['w_gu'], 'nn')
        (act,) = rowwise(t + 'swiglu', f_swiglu, [au], [], [], [D_FF], TM, [BF16])
        xc = matmul(t + 'ffn_out', act, p['w_down'], 'nn', add=x2)
        sv.update(h2=h2, au=au, act=act)
        saved.append(sv)

    loss_part, dx, d_final_g = loss_head(xc, loss_target.reshape(N, D), row(full['final_norm_g']), TM)
    loss = lax.psum(loss_part[0, 0], ("x", "y", "c"))

    G = {n: [None] * DEPTH for n in WEIGHTS if n != 'final_norm_g'}
    for l in reversed(range(DEPTH)):
        p, sv = LW[l], saved[l]
        t = 'l%d_bwd_' % l
        d_act = matmul(t + 'ffn_out_dx', dx, p['w_down'], 'nt')
        G['w_ffn_down'][l] = matmul(t + 'ffn_out_dw', sv['act'], dx, 'tn')
        (d_au,), _ = rowwise_bwd(t + 'swiglu', f_swiglu, [sv['au']], [], [], [[d_act]], TMH, drow_dtypes=[BF16])
        d_h2 = matmul(t + 'ffn_in_dx', d_au, p['w_gu'], 'nt')
        d_wgu = matmul(t + 'ffn_in_dw', sv['h2'], d_au, 'tn')
        G['w_ffn_gate'][l], G['w_ffn_up'][l] = d_wgu[:, :D_FF], d_wgu[:, D_FF:]
        (dx2,), (dg,) = rowwise_bwd(t + 'ffn_norm', f_rms, [sv['x2']], [p['ffn_g']], [], [[d_h2]], TM, extra=[(0, dx)])
        G['ffn_norm_g'][l] = dg.reshape(-1)
        d_merged = matmul(t + 'out_proj_dx', dx2, p['w_out'], 'nt')
        G['w_out'][l] = matmul(t + 'out_proj_dw', sv['merged'], dx2, 'tn')
        (d_pgate, d_b0, d_b1, d_b2), (d_gate_b,) = rowwise_bwd(
            t + 'merge', f_merge, [sv['p_gate'], sv['b0'], sv['b1'], sv['b2']], [p['gate_b']], [], [[d_merged]], TM,
            drow_dtypes=[BF16] * 4)
        G['gate_b'][l] = d_gate_b.reshape(3, D)
        d_ya = matmul(t + 'br0_dx', d_b0, p['wb0'], 'nt')
        d_yb = matmul(t + 'br1_dx', d_b1, p['wb1'], 'nt')
        d_yc = matmul(t + 'br2_dx', d_b2, p['wb2'], 'nt')
        d_wb0 = matmul(t + 'br0_dw', sv['ya'], d_b0, 'tn').reshape(MLA_HEADS, HEAD_PAD, D)[:, QK_NOPE:].reshape(-1, D)
        G['w_branch'][l] = jnp.stack([d_wb0, matmul(t + 'br1_dw', sv['yb'], d_b1, 'tn'),
                                      matmul(t + 'br2_dw', sv['yc'], d_b2, 'tn')])
        (d_y, d_r1, d_v1, d_kdir1, d_g), (d_ln_g, d_ln_b, d_r_k) = rowwise_bwd(
            t + 'rw_post', f_rw_post, [sv['ysum'], sv['r'], sv['v'], sv['kdir'], sv['g']],
            [p['ln_g'], p['ln_b'], p['r_k']], [ones], [[d_yc]], TMH)
        G['rw_ln_g'][l], G['rw_ln_b'][l] = d_ln_g.reshape(-1), d_ln_b.reshape(-1)
        G['rw_r_k'][l] = d_r_k.reshape(RWKV_HEADS, RWKV_HEAD)
        sc = sv['sc']
        res = scan_bwd(t + 'scan', sc['w'], sc['k'], sc['b'], sc['kk'], sc['r'], sc['v'], sv['sp'], sv['sa'],
                       sv['s_last'], to_i(t + 'lay_dy', d_y))
        s_dw, s_dk, s_db, s_dkk, s_dr, s_dv = [
            from_scan(t + 'lay_' + nm, res[2 * i], res[2 * i + 1], B, S, nd, nm == 'dv')
            for i, (nm, nd) in enumerate((('dw', 2), ('dk', 2), ('db', 2), ('dkk', 1), ('dr', 1), ('dv', 1)))]
        rw_par = [p['mu'], p['w0'], p['w2'], p['a0'], p['a2'], p['g2'], p['k_k'], p['k_a']]
        d_outs = [[d_r1, s_dr], [d_v1, s_dv], [s_dw], [d_kdir1, s_dk], [s_dkk], [s_db], [d_g]]
        (d_z, d_zp, d_zn), d_rw = rowwise_bwd(
            t + 'rw_pre', f_rw_pre, [sv['z'], sv['zp'], sv['zn']], rw_par, [ones], d_outs, TMH)
        (d_prw,) = rowwise(t + 'shift_sum', f_add3, [d_z, shift_next(d_zp), shift_prev(d_zn)], [], [], [RWKV_IN], TM,
                           [BF16])
        G['rw_mu'][l] = d_rw[0].reshape(-1)
        G['rw_w0'][l] = d_rw[1].reshape(2, RWKV_DIM)
        G['rw_w2'][l] = jnp.stack([d_rw[2][:64, :RWKV_DIM], d_rw[2][64:, RWKV_DIM:]])
        G['rw_a0'][l] = d_rw[3].reshape(2, RWKV_DIM)
        G['rw_a2'][l] = jnp.stack([d_rw[4][:64, :RWKV_DIM], d_rw[4][64:, RWKV_DIM:]])
        G['rw_g2'][l] = d_rw[5]
        G['rw_k_k'][l], G['rw_k_a'][l] = d_rw[6].reshape(-1), d_rw[7].reshape(-1)
        (d_psg,), (d_sg_g, d_sg_b, d_sg_w, d_sg_bias) = rowwise_bwd(
            t + 'sg', f_sg, [sv['p_sg']], [p['sg_g'], p['sg_b'], p['sg_w'], p['sg_bias']], [], [[d_yb]], TMH,
            drow_dtypes=[BF16])
        G['sg_ln_g'][l], G['sg_ln_b'][l], G['sg_w'][l] = d_sg_g.reshape(-1), d_sg_b.reshape(-1), d_sg_w
        G['sg_b'][l] = d_sg_bias.reshape(SG_CHUNK, SG_GROUPS, SG_DIM // SG_GROUPS).sum(-1).T
        d_qh, d_kh, d_kvv = attention_bwd(t + 'attn', sv['qh'], sv['kh'], sv['kv'], d_ya, B, S, TQ)
        (d_qq, d_kv, d_pkr), _ = rowwise_bwd(
            t + 'rope', f_rope, [sv['qq'], sv['kv'], sv['p_kr'], ct, st], [], [], [[d_qh], [d_kh]], TM,
            n_row_diff=3, extra=[(1, d_kvv)], drow_dtypes=[BF16] * 3)
        d_cq = matmul(t + 'uq_dx', d_qq, p['w_uq'], 'nt')
        d_wuq = matmul(t + 'uq_dw', sv['cq'], d_qq, 'tn')
        g1, g2_ = _unpermute_full(d_wuq, wuq_src, wuq_sgn, MLA_HEADS * (QK_NOPE + QK_ROPE))
        G['w_uq'][l] = g1 + g2_
        d_ckv = matmul(t + 'ukv_dx', d_kv, p['w_ukv'], 'nt')
        G['w_ukv'][l] = matmul(t + 'ukv_dw', sv['ckv'], d_kv, 'tn')
        (d_pq,), (dg,) = rowwise_bwd(t + 'q_norm', f_rms, [sv['p_q']], [p['q_g']], [], [[d_cq]], TM,
                                     drow_dtypes=[BF16])
        G['q_norm_g'][l] = dg.reshape(-1)
        (d_pckv,), (dg,) = rowwise_bwd(t + 'kv_norm', f_rms, [sv['p_ckv']], [p['kv_g']], [], [[d_ckv]], TM,
                                       drow_dtypes=[BF16])
        G['kv_norm_g'][l] = dg.reshape(-1)
        d_h, d_cols = None, []
        for (sn, a, b), d_seg in zip(IN_SEGMENTS, [d_pq, d_pckv, d_pkr, d_psg, d_prw, d_pgate]):
            d_h = matmul(t + 'in_proj_dx_' + sn, d_seg, p['w_in'][:, a:b], 'nt', add=d_h)
            d_cols.append(matmul(t + 'in_proj_dw_' + sn, sv['h'], d_seg, 'tn'))
        d_win = jnp.concatenate(d_cols, axis=1)
        g1, g2_ = _unpermute_full(d_win, win_src, win_sgn, N_IN)
        kr0 = Q_LORA + KV_LORA
        G['w_in'][l] = g1.at[:, kr0:kr0 + QK_ROPE].add(g2_[:, kr0:kr0 + QK_ROPE])
        (dx,), (dg,) = rowwise_bwd(t + 'attn_norm', f_rms, [sv['x']], [p['attn_g']], [], [[d_h]], TM, extra=[(0, dx2)])
        G['attn_norm_g'][l] = dg.reshape(-1)

    grads = {n: jnp.stack(G[n]) for n in G}
    grads['final_norm_g'] = d_final_g.reshape(-1)
    grad_x = dx.reshape(B, S, D)

    per_shard = []
    for q in range(4):
        sl = []
        for n in SHARDED:
            ax = SHARD_AXIS[n]
            w = W[n].shape[ax]
            sl.append(lax.slice_in_dim(grads[n], q * w, (q + 1) * w, axis=ax).astype(BF16))
        per_shard.append(_pack(sl, 2 * JOIN_ROWS))
    R = per_shard[0].shape[0]
    gpack = jnp.stack(per_shard).astype(BF16).reshape(4, 2, R // 2, PACK_C)
    both = sum_join(scatter_partials(gpack))
    g_shard = dict(zip(SHARDED, _unpack(both.reshape(R, PACK_C), shard_shapes)))
    rep_shapes = [W[n].shape for n in REPLICATED]
    rpack = _pack([grads[n] for n in REPLICATED], 8)
    g_rep = sum_slots("sum_replicated", gather_all("gather_replicated", rpack))

    outs = {}
    for n in MATMUL_SHARDED:
        shp = W[n].shape
        two = lambda a: a.reshape(-1, shp[-1])
        res = adamw("adamw_" + n, two(W[n]), two(g_shard[n]), two(M1[n]), two(M2[n]))
        outs['grad', n] = g_shard[n]
        for key, a in zip(('delta', 'new_m', 'new_v'), res):
            outs[key, n] = a.reshape(shp)
    small = SMALL_SHARDED + REPLICATED
    small_shapes = [W[n].shape for n in small]
    g_small = [g_shard[n] for n in SMALL_SHARDED] + _unpack(g_rep, rep_shapes)
    res = adamw("adamw_small", _pack([W[n] for n in small], 8), _pack(g_small, 8),
                _pack([M1[n] for n in small], 8), _pack([M2[n] for n in small], 8))
    for n, a in zip(small, g_small):
        outs['grad', n] = a
    for key, buf in zip(('delta', 'new_m', 'new_v'), res):
        for n, a in zip(small, _unpack(buf, small_shapes)):
            outs[key, n] = a
    return (loss, grad_x, *[outs['grad', n] for n in WEIGHTS], *[outs['delta', n] for n in WEIGHTS],
            *[outs['new_m', n] for n in WEIGHTS], *[outs['new_v', n] for n in WEIGHTS])
```

```python
import functools
import math

import numpy as np
import jax
import jax.numpy as jnp
from jax import lax
from jax.experimental import pallas as pl
from jax.experimental.pallas import tpu as pltpu

F32 = jnp.float32
BF16 = jnp.bfloat16

DEPTH = 2
MLA_HEADS = 8
Q_LORA = 384
KV_LORA = 256
QK_NOPE = 64
QK_ROPE = 32
V_HEAD = 64
ROPE_THETA = 10000.0
SG_GROUPS = 8
SG_DIM = 512
SG_CHUNK = 128
RWKV_HEADS = 8
RWKV_HEAD = 64
RWKV_DIM = 512
GN_EPS = 64e-5
NORM_EPS = 1e-6
D_FF = 2816
RWKV_IN = 1920
N_IN = 6688
ADAM_LR, ADAM_B1, ADAM_B2, ADAM_EPS, ADAM_WD, ADAM_STEP = 0.001, 0.9, 0.999, 1e-08, 0.01, 10

LANES = 128
HEAD_PAD = 128
VMEM_LIMIT = 56 * 1024 * 1024
MESH = pl.DeviceIdType.MESH

WEIGHTS = ['attn_norm_g', 'w_in', 'gate_b', 'q_norm_g', 'w_uq', 'kv_norm_g', 'w_ukv', 'sg_ln_g', 'sg_ln_b', 'sg_w',
           'sg_b', 'rw_mu', 'rw_w0', 'rw_w2', 'rw_a0', 'rw_a2', 'rw_g2', 'rw_k_k', 'rw_k_a', 'rw_r_k', 'rw_ln_g',
           'rw_ln_b', 'w_branch', 'w_out', 'ffn_norm_g', 'w_ffn_gate', 'w_ffn_up', 'w_ffn_down', 'final_norm_g']
SHARD_AXIS = {'w_in': 2, 'gate_b': 2, 'w_uq': 2, 'w_ukv': 2, 'rw_w0': 2, 'rw_w2': 3, 'rw_a0': 2, 'rw_a2': 3,
              'rw_g2': 2, 'w_branch': 3, 'w_out': 1, 'w_ffn_gate': 2, 'w_ffn_up': 2, 'w_ffn_down': 1}
SHARDED = [n for n in WEIGHTS if n in SHARD_AXIS]
REPLICATED = [n for n in WEIGHTS if n not in SHARD_AXIS]
SMALL_SHARDED = ['gate_b', 'rw_w0', 'rw_a0']
MATMUL_SHARDED = [n for n in SHARDED if n not in SMALL_SHARDED]


def _params(sem=None):
    return pltpu.CompilerParams(dimension_semantics=sem, vmem_limit_bytes=VMEM_LIMIT)


def _pick(n, cands):
    for c in cands:
        if n % c == 0:
            return c
    return n


def _dot(a, b, dims):
    return lax.dot_general(a.astype(BF16), b.astype(BF16), (dims, ((), ())), preferred_element_type=F32)


def _nn(a, b):
    return _dot(a, b, ((1,), (0,)))


def _nt(a, b):
    return _dot(a, b, ((1,), (1,)))


def _tn(a, b):
    return _dot(a, b, ((0,), (0,)))


@jax.custom_vjp
def mm(a, b):
    return _nn(a, b)


mm.defvjp(lambda a, b: (_nn(a, b), (a, b)), lambda res, g: (_nt(g, res[1]), _tn(res[0], g)))


@jax.custom_vjp
def mm_nt(a, b):
    return _nt(a, b)


mm_nt.defvjp(lambda a, b: (_nt(a, b), (a, b)), lambda res, g: (_nn(g, res[1]), _tn(g, res[0])))


def _seg_raw(x, ones):
    hi = x.astype(BF16)
    lo = (x - hi.astype(F32)).astype(BF16)
    d = (((1,), (0,)), ((), ()))
    return (lax.dot_general(hi, ones, d, preferred_element_type=F32)
            + lax.dot_general(lo, ones, d, preferred_element_type=F32))


@jax.custom_vjp
def segsum(x, ones):
    return _seg_raw(x, ones)


segsum.defvjp(lambda x, ones: (_seg_raw(x, ones), ones),
              lambda ones, g: (_seg_raw(g, ones), jnp.zeros_like(ones)))


def _sigmoid(x):
    return 0.5 * (jnp.tanh(0.5 * x) + 1.0)


def _rms(x, g):
    return x * lax.rsqrt(jnp.mean(x * x, axis=-1, keepdims=True) + NORM_EPS) * g


def matmul(name, a, b, mode, add=None, out_dtype=F32):
    if mode == 'nn':
        (M, K), (_, N) = a.shape, b.shape
    elif mode == 'nt':
        (M, K), (N, _) = a.shape, b.shape
    else:
        (K, M), (_, N) = a.shape, b.shape
    tm = _pick(M, (1408, 1024, 512, 384, 256, 128))
    tn = _pick(N, (1408, 1024, 768, 512, 384, 256, 128))
    tk = _pick(K, (512, 384, 256, 128))
    nk = K // tk
    dims = {'nn': ((1,), (0,)), 'nt': ((1,), (1,)), 'tn': ((0,), (0,))}[mode]
    a_spec = pl.BlockSpec((tk, tm), lambda i, j, k: (k, i)) if mode == 'tn' else pl.BlockSpec((tm, tk), lambda i, j, k: (i, k))
    b_spec = pl.BlockSpec((tn, tk), lambda i, j, k: (j, k)) if mode == 'nt' else pl.BlockSpec((tk, tn), lambda i, j, k: (k, j))
    o_spec = pl.BlockSpec((tm, tn), lambda i, j, k: (i, j))
    has_add = add is not None

    def body(*refs):
        if has_add:
            a_ref, b_ref, add_ref, o_ref, acc = refs
        else:
            a_ref, b_ref, o_ref, acc = refs
        k = pl.program_id(2)

        @pl.when(k == 0)
        def _():
            acc[...] = jnp.zeros_like(acc)

        acc[...] += _dot(a_ref[...], b_ref[...], dims)

        @pl.when(k == nk - 1)
        def _():
            o_ref[...] = (acc[...] + add_ref[...] if has_add else acc[...]).astype(o_ref.dtype)

    ins = [a, b] + ([add] if has_add else [])
    specs = [a_spec, b_spec] + ([o_spec] if has_add else [])
    return pl.pallas_call(
        body, name=name, grid=(M // tm, N // tn, nk), in_specs=specs, out_specs=o_spec,
        out_shape=jax.ShapeDtypeStruct((M, N), out_dtype), scratch_shapes=[pltpu.VMEM((tm, tn), F32)],
        compiler_params=_params(("parallel", "parallel", "arbitrary")))(*ins)


def _full_spec(p):
    nd = p.ndim
    return pl.BlockSpec(p.shape, lambda i, _nd=nd: (0,) * _nd)


def rowwise(name, fn, rows, params, consts, out_widths, tm, out_dtypes=None):
    N = rows[0].shape[0]
    nr, npar, nc = len(rows), len(params), len(consts)

    def body(*refs):
        vals = [r[...] for r in refs[:nr + npar + nc]]
        res = fn(*vals)
        for o, v in zip(refs[nr + npar + nc:], res):
            o[...] = v.astype(o.dtype)

    in_specs = ([pl.BlockSpec((tm, r.shape[1]), lambda i: (i, 0)) for r in rows]
                + [_full_spec(p) for p in list(params) + list(consts)])
    out_specs = [pl.BlockSpec((tm, w), lambda i: (i, 0)) for w in out_widths]
    return pl.pallas_call(
        body, name=name, grid=(N // tm,), in_specs=in_specs, out_specs=out_specs,
        out_shape=[jax.ShapeDtypeStruct((N, w), d) for w, d in zip(out_widths, out_dtypes or [F32] * len(out_widths))],
        compiler_params=_params(("parallel",)))(*rows, *params, *consts)


def rowwise_bwd(name, fn, rows, params, consts, d_outs, tm, n_row_diff=None, extra=(), drow_dtypes=None):
    N = rows[0].shape[0]
    nr, npar, nc = len(rows), len(params), len(consts)
    nd = nr if n_row_diff is None else n_row_diff
    counts = [len(p) for p in d_outs]
    flat_d = [a for parts in d_outs for a in parts]
    nflat, nex = len(flat_d), len(extra)

    def body(*refs):
        pos = 0
        row_v = [r[...] for r in refs[pos:pos + nr]]; pos += nr
        par_v = [r[...] for r in refs[pos:pos + npar]]; pos += npar
        con_v = [r[...] for r in refs[pos:pos + nc]]; pos += nc
        d_refs = refs[pos:pos + nflat]; pos += nflat
        ex_refs = refs[pos:pos + nex]; pos += nex
        drow_refs = refs[pos:pos + nd]; pos += nd
        dpar_refs = refs[pos:pos + npar]

        def f(*diff):
            return fn(*diff[:nd], *row_v[nd:], *diff[nd:], *con_v)

        _, vjp = jax.vjp(f, *row_v[:nd], *par_v)
        cts, q = [], 0
        for c in counts:
            g = d_refs[q][...].astype(F32)
            for t in range(1, c):
                g = g + d_refs[q + t][...].astype(F32)
            cts.append(g)
            q += c
        grads = vjp(tuple(cts))
        drow = list(grads[:nd])
        for (idx, _), r in zip(extra, ex_refs):
            drow[idx] = drow[idx] + r[...].astype(F32)
        for o, v in zip(drow_refs, drow):
            o[...] = v.astype(o.dtype)

        @pl.when(pl.program_id(0) == 0)
        def _():
            for o in dpar_refs:
                o[...] = jnp.zeros_like(o)

        for o, v in zip(dpar_refs, grads[nd:]):
            o[...] += v

    ex_arrs = [a for _, a in extra]
    in_specs = ([pl.BlockSpec((tm, r.shape[1]), lambda i: (i, 0)) for r in rows]
                + [_full_spec(p) for p in list(params) + list(consts)]
                + [pl.BlockSpec((tm, a.shape[1]), lambda i: (i, 0)) for a in flat_d + ex_arrs])
    out_specs = ([pl.BlockSpec((tm, r.shape[1]), lambda i: (i, 0)) for r in rows[:nd]]
                 + [_full_spec(p) for p in params])
    out_shape = ([jax.ShapeDtypeStruct(r.shape, d) for r, d in zip(rows[:nd], drow_dtypes or [F32] * nd)]
                 + [jax.ShapeDtypeStruct(p.shape, F32) for p in params])
    res = pl.pallas_call(
        body, name=name, grid=(N // tm,), in_specs=in_specs, out_specs=out_specs, out_shape=out_shape,
        compiler_params=_params(("arbitrary",)))(*rows, *params, *consts, *flat_d, *ex_arrs)
    return list(res[:nd]), list(res[nd:])


def f_rms(x, g):
    return (_rms(x, g),)


def f_rope(qq, kv, krr, ct, st):
    hw = MLA_HEADS * HEAD_PAD
    c8 = jnp.tile(ct, (1, MLA_HEADS))
    s8 = jnp.tile(st, (1, MLA_HEADS))
    q = qq[:, :hw] * c8 + qq[:, hw:] * s8
    kr = krr[:, :HEAD_PAD] * ct + krr[:, HEAD_PAD:] * st
    lane = lax.broadcasted_iota(jnp.int32, kv.shape, 1) % HEAD_PAD
    k = jnp.where(lane < QK_NOPE, kv, jnp.tile(kr, (1, MLA_HEADS)))
    return q, k


def f_sg(p, ln_g, ln_b, w, bias):
    z = 0.5 * p * (1.0 + jnp.tanh(0.7978845608028654 * (p + 0.044715 * p * p * p)))
    u, v = z[:, :SG_DIM], z[:, SG_DIM:]
    mu = jnp.mean(v, axis=-1, keepdims=True)
    var = jnp.mean(jnp.square(v - mu), axis=-1, keepdims=True)
    v = (v - mu) * lax.rsqrt(var + 1e-5) * ln_g + ln_b
    lane = lax.broadcasted_iota(jnp.int32, (SG_CHUNK, LANES), 1)
    outs = []
    for c in range(p.shape[0] // SG_CHUNK):
        vc = v[c * SG_CHUNK:(c + 1) * SG_CHUNK]
        cols = []
        for m in range(SG_DIM // LANES):
            blk = vc[:, m * LANES:(m + 1) * LANES]
            cols.append(jnp.where(lane < 64, mm(w[2 * m], blk), mm(w[2 * m + 1], blk)))
        outs.append(jnp.concatenate(cols, axis=1) + bias)
    mixed = outs[0] if len(outs) == 1 else jnp.concatenate(outs, axis=0)
    return (u * mixed,)


def f_rw_pre(z, zp, zn, mu, w0, w2, a0, a2, g2, k_k, k_a, ones):
    z = z + mu * (0.5 * (zp + zn) - z)
    C = RWKV_DIM
    r, k, v = z[:, :C], z[:, C:2 * C], z[:, 2 * C:3 * C]
    wl, al, gl = z[:, 3 * C:3 * C + 128], z[:, 3 * C + 128:3 * C + 256], z[:, 3 * C + 256:]
    w = w0 + mm(jnp.tanh(wl), w2)
    decay = jnp.exp(-0.6065306597126334 * _sigmoid(w))
    a = _sigmoid(a0 + mm(al, a2))
    g = mm(_sigmoid(gl), g2)
    kk = k * k_k
    kk = kk / jnp.maximum(jnp.sqrt(segsum(kk * kk, ones)), 1e-12)
    k2 = jnp.concatenate([k, k], axis=1)
    kdir = k2 * (1.0 + (a - 1.0) * jnp.concatenate([k_a, k_a], axis=1))
    bdir = jnp.concatenate([kk, kk], axis=1) * a
    return r, v, decay, kdir, kk, bdir, g


def f_rw_post(y, r, v, kdir, g, ln_g, ln_b, r_k, ones):
    mean = segsum(y, ones) * (1.0 / RWKV_HEAD)
    yc = y - mean
    var = segsum(yc * yc, ones) * (1.0 / RWKV_HEAD)
    y = yc * lax.rsqrt(var + GN_EPS) * ln_g + ln_b
    C = RWKV_DIM
    bonus = segsum(r * kdir[:, :C] * r_k, ones) + segsum(r * kdir[:, C:] * r_k, ones)
    return ((y + bonus * v) * g,)


def f_merge(pg, b0, b1, b2, gate_b):
    D = b0.shape[1]
    gt = _sigmoid(pg + gate_b)
    return (gt[:, :D] * b0 + gt[:, D:2 * D] * b1 + gt[:, 2 * D:] * b2,)


def f_swiglu(au):
    a, u = au[:, :D_FF], au[:, D_FF:]
    return (a * _sigmoid(a) * u,)


def f_add3(a, b, c):
    return (a + b + c,)


def loss_head(x, tgt, g, tm):
    N, D = x.shape

    def body(x_ref, t_ref, g_ref, loss_ref, dx_ref, dg_ref):
        t = t_ref[...]

        def f(xv, gv):
            err = _rms(xv, gv) - t
            return 0.5 * jnp.sum(jnp.mean(err * err, axis=-1, keepdims=True))

        val, (dx, dg) = jax.value_and_grad(f, argnums=(0, 1))(x_ref[...], g_ref[...])
        dx_ref[...] = dx

        @pl.when(pl.program_id(0) == 0)
        def _():
            loss_ref[...] = jnp.zeros_like(loss_ref)
            dg_ref[...] = jnp.zeros_like(dg_ref)

        loss_ref[...] += jnp.full(loss_ref.shape, val, F32)
        dg_ref[...] += dg

    row = pl.BlockSpec((tm, D), lambda i: (i, 0))
    return pl.pallas_call(
        body, name="loss_head", grid=(N // tm,), in_specs=[row, row, _full_spec(g)],
        out_specs=[pl.BlockSpec((1, LANES), lambda i: (0, 0)), row, _full_spec(g)],
        out_shape=[jax.ShapeDtypeStruct((1, LANES), F32), jax.ShapeDtypeStruct((N, D), F32),
                   jax.ShapeDtypeStruct(g.shape, F32)],
        compiler_params=_params(("arbitrary",)))(x, tgt, g)


ATT_SCALE = float((QK_NOPE + QK_ROPE) ** -0.5)


def _attn_block(q, k, kv):
    s = mm_nt(q, k) * ATT_SCALE
    m = lax.stop_gradient(jnp.max(s, axis=-1, keepdims=True))
    e = jnp.exp(s - m)
    return mm(e, kv) * (1.0 / jnp.sum(e, axis=-1, keepdims=True))


def attention_fwd(name, q, k, kv, B, S, tq, out_dtype):
    nq = S // tq
    qspec = pl.BlockSpec((tq, HEAD_PAD), lambda b, h, i: (b * nq + i, h))
    kspec = pl.BlockSpec((S, HEAD_PAD), lambda b, h, i: (b, h))

    def body(q_ref, k_ref, kv_ref, o_ref):
        o_ref[...] = _attn_block(q_ref[...], k_ref[...], kv_ref[...]).astype(o_ref.dtype)

    return pl.pallas_call(
        body, name=name, grid=(B, MLA_HEADS, nq), in_specs=[qspec, kspec, kspec], out_specs=qspec,
        out_shape=jax.ShapeDtypeStruct(q.shape, out_dtype),
        compiler_params=_params(("parallel", "parallel", "arbitrary")))(q, k, kv)


def attention_bwd(name, q, k, kv, do, B, S, tq):
    nq = S // tq
    qspec = pl.BlockSpec((tq, HEAD_PAD), lambda b, h, i: (b * nq + i, h))
    kspec = pl.BlockSpec((S, HEAD_PAD), lambda b, h, i: (b, h))

    def body(q_ref, k_ref, kv_ref, do_ref, dq_ref, dk_ref, dkv_ref):
        _, vjp = jax.vjp(_attn_block, q_ref[...], k_ref[...], kv_ref[...])
        dq, dk, dkv = vjp(do_ref[...])
        dq_ref[...] = dq

        @pl.when(pl.program_id(2) == 0)
        def _():
            dk_ref[...] = jnp.zeros_like(dk_ref)
            dkv_ref[...] = jnp.zeros_like(dkv_ref)

        dk_ref[...] += dk
        dkv_ref[...] += dkv

    sh = jax.ShapeDtypeStruct(q.shape, F32)
    return pl.pallas_call(
        body, name=name, grid=(B, MLA_HEADS, nq), in_specs=[qspec, kspec, kspec, qspec],
        out_specs=[qspec, kspec, kspec], out_shape=[sh, sh, sh],
        compiler_params=_params(("parallel", "parallel", "arbitrary")))(q, k, kv, do)


SCAN_TC = 8
SCAN_UNROLL = 16


def _jloop(n, body, init):
    def outer(o, c):
        for u in range(SCAN_UNROLL):
            c = body(o * SCAN_UNROLL + u, c)
        return c

    return lax.fori_loop(0, n // SCAN_UNROLL, outer, init)


def _dir_mask(L, Ip):
    per_dir = L // (2 * (RWKV_HEAD // Ip))
    lane = lax.broadcasted_iota(jnp.int32, (1, L), 1)
    return (lane // per_dir) % 2 == 1


def _merge_dirs(mask, fwd_ref, rev_ref, out_ref):
    for tt in range(SCAN_TC):
        out_ref[tt] = jnp.where(mask, rev_ref[SCAN_TC - 1 - tt], fwd_ref[tt])


def scan_fwd(name, w, k, b, kk, r, v):
    T, J, L = w.shape
    Ip = v.shape[1]
    nT = T // SCAN_TC
    fwd3, rev3 = (lambda g: (g, 0, 0)), (lambda g: (nT - 1 - g, 0, 0))
    jf, jr = pl.BlockSpec((SCAN_TC, J, L), fwd3), pl.BlockSpec((SCAN_TC, J, L), rev3)
    i_f, i_r = pl.BlockSpec((SCAN_TC, Ip, L), fwd3), pl.BlockSpec((SCAN_TC, Ip, L), rev3)
    sspec = pl.BlockSpec((SCAN_TC, J, Ip, L), lambda g: (g, 0, 0, 0))
    last_spec = pl.BlockSpec((J, Ip, L), lambda g: (0, 0, 0))

    def body(wf, wr, kf, kr, bf, br, kkf, kkr, rf, rr, vf, vr, yf_ref, yr_ref, sp_ref, sa_ref, last_ref,
             s_ref, w_ref, k_ref, b_ref, kk_ref, r_ref, v_ref):
        @pl.when(pl.program_id(0) == 0)
        def _():
            s_ref[...] = jnp.zeros_like(s_ref)

        mask = _dir_mask(L, Ip)
        for f_, r_, m_ in ((wf, wr, w_ref), (kf, kr, k_ref), (bf, br, b_ref), (kkf, kkr, kk_ref), (rf, rr, r_ref),
                           (vf, vr, v_ref)):
            _merge_dirs(mask, f_, r_, m_)

        def row(ref, tt, j):
            return jnp.broadcast_to(ref[tt, pl.ds(j, 1), :], (Ip, L))

        def step(tt, carry):
            def p1(j, sa):
                s = s_ref[j]
                sp_ref[tt, j] = s
                return sa + s * row(kk_ref, tt, j)

            sa = _jloop(J, p1, jnp.zeros((Ip, L), F32))
            sa_ref[tt] = sa
            vt = v_ref[tt]

            def p2(j, y):
                s = s_ref[j] * row(w_ref, tt, j) - sa * row(b_ref, tt, j) + vt * row(k_ref, tt, j)
                s_ref[j] = s
                return y + s * row(r_ref, tt, j)

            y = _jloop(J, p2, jnp.zeros((Ip, L), F32))
            yf_ref[tt] = y
            yr_ref[SCAN_TC - 1 - tt] = y
            return carry

        lax.fori_loop(0, SCAN_TC, step, 0)

        @pl.when(pl.program_id(0) == nT - 1)
        def _():
            last_ref[...] = s_ref[...]

    ish = jax.ShapeDtypeStruct((T, Ip, L), F32)
    jscr = pltpu.VMEM((SCAN_TC, J, L), F32)
    return pl.pallas_call(
        body, name=name, grid=(nT,), in_specs=[jf, jr] * 5 + [i_f, i_r],
        out_specs=[i_f, i_r, sspec, i_f, last_spec],
        out_shape=[ish, ish, jax.ShapeDtypeStruct((T, J, Ip, L), F32), ish, jax.ShapeDtypeStruct((J, Ip, L), F32)],
        scratch_shapes=[pltpu.VMEM((J, Ip, L), F32)] + [jscr] * 5 + [pltpu.VMEM((SCAN_TC, Ip, L), F32)],
        compiler_params=_params(("arbitrary",)))(w, w, k, k, b, b, kk, kk, r, r, v, v)


def scan_bwd(name, w, k, b, kk, r, v, sp, sa_all, s_last, dy):
    T, J, L = w.shape
    Ip = v.shape[1]
    nT = T // SCAN_TC
    stp3, mir3 = (lambda g: (nT - 1 - g, 0, 0)), (lambda g: (g, 0, 0))
    jf, jr = pl.BlockSpec((SCAN_TC, J, L), stp3), pl.BlockSpec((SCAN_TC, J, L), mir3)
    i_f, i_r = pl.BlockSpec((SCAN_TC, Ip, L), stp3), pl.BlockSpec((SCAN_TC, Ip, L), mir3)
    sspec = pl.BlockSpec((SCAN_TC, J, Ip, L), lambda g: (nT - 1 - g, 0, 0, 0))
    last_spec = pl.BlockSpec((J, Ip, L), lambda g: (0, 0, 0))

    def body(wf, wr, kf, kr, bf, br, kkf, kkr, rf, rr, vf, vr, dyf, dyr, sp_ref, sa_ref, last_ref,
             dwf, dwr, dkf, dkr, dbf, dbr, dkkf, dkkr, drf, drr, dvf, dvr,
             ds_ref, nxt_ref, w_ref, k_ref, b_ref, kk_ref, r_ref, v_ref, dy_ref):
        @pl.when(pl.program_id(0) == 0)
        def _():
            ds_ref[...] = jnp.zeros_like(ds_ref)
            nxt_ref[...] = last_ref[...]

        mask = _dir_mask(L, Ip)
        for f_, r_, m_ in ((wf, wr, w_ref), (kf, kr, k_ref), (bf, br, b_ref), (kkf, kkr, kk_ref), (rf, rr, r_ref),
                           (vf, vr, v_ref), (dyf, dyr, dy_ref)):
            _merge_dirs(mask, f_, r_, m_)

        def row(ref, tt, j):
            return jnp.broadcast_to(ref[tt, pl.ds(j, 1), :], (Ip, L))

        def rsum(x):
            return jnp.sum(x, axis=0, keepdims=True)

        def make_step(first):
            def step(n, carry):
                tt = SCAN_TC - 1 - n
                dyt, vt, sa = dy_ref[tt], v_ref[tt], sa_ref[tt]

                def p1(j, c):
                    dsa, dv = c
                    ds = ds_ref[j] + dyt * row(r_ref, tt, j)
                    ds_ref[j] = ds
                    return dsa - ds * row(b_ref, tt, j), dv + ds * row(k_ref, tt, j)

                z = jnp.zeros((Ip, L), F32)
                dsa, dv = _jloop(J, p1, (z, z))
                dvf[tt] = dv
                dvr[SCAN_TC - 1 - tt] = dv

                def put(f_ref, r_ref_, j, val):
                    f_ref[tt, pl.ds(j, 1), :] = val
                    r_ref_[SCAN_TC - 1 - tt, pl.ds(j, 1), :] = val

                def p2(j, c):
                    ds = ds_ref[j]
                    s0 = sp_ref[tt, j]
                    s1 = nxt_ref[j] if first else sp_ref[tt + 1, j]
                    put(drf, drr, j, rsum(s1 * dyt))
                    put(dkf, dkr, j, rsum(ds * vt))
                    put(dbf, dbr, j, -rsum(ds * sa))
                    put(dwf, dwr, j, rsum(ds * s0))
                    put(dkkf, dkkr, j, rsum(s0 * dsa))
                    ds_ref[j] = ds * row(w_ref, tt, j) + dsa * row(kk_ref, tt, j)
                    return c

                _jloop(J, p2, 0)
                return carry

            return step

        make_step(True)(0, 0)
        lax.fori_loop(1, SCAN_TC, make_step(False), 0)
        nxt_ref[...] = sp_ref[0]

    jsh = jax.ShapeDtypeStruct((T, J, L), F32)
    ish = jax.ShapeDtypeStruct((T, Ip, L), F32)
    jscr = pltpu.VMEM((SCAN_TC, J, L), F32)
    iscr = pltpu.VMEM((SCAN_TC, Ip, L), F32)
    return pl.pallas_call(
        body, name=name, grid=(nT,), in_specs=[jf, jr] * 5 + [i_f, i_r] * 2 + [sspec, i_f, last_spec],
        out_specs=[jf, jr] * 5 + [i_f, i_r], out_shape=[jsh] * 10 + [ish] * 2,
        scratch_shapes=[pltpu.VMEM((J, Ip, L), F32)] * 2 + [jscr] * 5 + [iscr] * 2,
        compiler_params=_params(("arbitrary",)))(w, w, k, k, b, b, kk, kk, r, r, v, v, dy, dy, sp, sa_all, s_last)


LAYOUT_TT = 32
LAYOUT_U = 8


def _lane_group(L, n):
    return lax.broadcasted_iota(jnp.int32, (1, L), 1) // (L // n)


def to_scan(name, x, B, S, nd, Ip):
    isplit = LANES // (2 * B * RWKV_HEADS)
    tt_n = _pick(S, (LAYOUT_TT, 16, 8))
    x5 = x.reshape(B, S, nd, RWKV_HEADS, RWKV_HEAD)
    rows_out = RWKV_HEAD if Ip is None else Ip

    def body(x_ref, o_ref):
        group = _lane_group(LANES, isplit)

        def chunk(c, carry):
            t0 = c * LAYOUT_U
            z = jnp.concatenate([x_ref[b, t0 + u, min(d, nd - 1)] for u in range(LAYOUT_U) for _ in range(isplit)
                                 for d in range(2) for b in range(B)], axis=0)
            mt = z.T
            for u in range(LAYOUT_U):
                m = mt[:, u * LANES:(u + 1) * LANES]
                if Ip is not None:
                    m = sum(jnp.where(group == i2, m[i2 * Ip:(i2 + 1) * Ip], 0.0) for i2 in range(isplit))
                o_ref[t0 + u] = m
            return carry

        lax.fori_loop(0, tt_n // LAYOUT_U, chunk, 0)

    return pl.pallas_call(
        body, name=name, grid=(S // tt_n,),
        in_specs=[pl.BlockSpec((B, tt_n, nd, RWKV_HEADS, RWKV_HEAD), lambda g: (0, g, 0, 0, 0))],
        out_specs=pl.BlockSpec((tt_n, rows_out, LANES), lambda g: (g, 0, 0)),
        out_shape=jax.ShapeDtypeStruct((S, rows_out, LANES), F32), compiler_params=_params(("parallel",)))(x5)


def from_scan(name, f, r, B, S, nd, i_indexed):
    rows_in = f.shape[1]
    isplit = LANES // (2 * B * RWKV_HEADS)
    Ip = rows_in if i_indexed else RWKV_HEAD // isplit
    tt_n = _pick(S, (LAYOUT_TT, 16, 8))
    per_i2 = LANES // isplit

    def body(f_ref, r_ref, o_ref):
        mask = _dir_mask(LANES, Ip)
        group = _lane_group(LANES, isplit)

        def chunk(c, carry):
            t0 = c * LAYOUT_U
            ms = []
            for u in range(LAYOUT_U):
                m = jnp.where(mask, r_ref[t0 + u], f_ref[t0 + u])
                if i_indexed:
                    m = jnp.concatenate([jnp.where(group == i2, m, 0.0) for i2 in range(isplit)], axis=0)
                ms.append(m)
            zt = jnp.concatenate(ms, axis=1).T
            for u in range(LAYOUT_U):
                z = zt[u * LANES:(u + 1) * LANES]
                zf = sum(z[i2 * per_i2:(i2 + 1) * per_i2] for i2 in range(isplit))
                for b in range(B):
                    d0 = zf[b * RWKV_HEADS:(b + 1) * RWKV_HEADS]
                    d1 = zf[(B + b) * RWKV_HEADS:(B + b + 1) * RWKV_HEADS]
                    if nd == 1:
                        o_ref[b, t0 + u, 0] = d0 + d1
                    else:
                        o_ref[b, t0 + u, 0] = d0
                        o_ref[b, t0 + u, 1] = d1
            return carry

        lax.fori_loop(0, tt_n // LAYOUT_U, chunk, 0)

    spec = pl.BlockSpec((tt_n, rows_in, LANES), lambda g: (g, 0, 0))
    out = pl.pallas_call(
        body, name=name, grid=(S // tt_n,), in_specs=[spec, spec],
        out_specs=pl.BlockSpec((B, tt_n, nd, RWKV_HEADS, RWKV_HEAD), lambda g: (0, g, 0, 0, 0)),
        out_shape=jax.ShapeDtypeStruct((B, S, nd, RWKV_HEADS, RWKV_HEAD), F32),
        compiler_params=_params(("parallel",)))(f, r)
    return out.reshape(B * S, nd * RWKV_DIM)


def adamw(name, w, g, m, v):
    R, C = w.shape
    tr = _pick(R, (256, 128, 64, 32, 16, 8))
    c1 = 1.0 - ADAM_B1 ** ADAM_STEP
    c2 = 1.0 - ADAM_B2 ** ADAM_STEP

    def body(w_ref, g_ref, m_ref, v_ref, d_ref, nm_ref, nv_ref):
        gv = g_ref[...]
        nm = ADAM_B1 * m_ref[...] + (1.0 - ADAM_B1) * gv
        nv = ADAM_B2 * v_ref[...] + (1.0 - ADAM_B2) * jnp.square(gv)
        d_ref[...] = -ADAM_LR * ((nm / c1) / (jnp.sqrt(nv / c2) + ADAM_EPS) + ADAM_WD * w_ref[...])
        nm_ref[...] = nm
        nv_ref[...] = nv

    spec = pl.BlockSpec((tr, C), lambda i: (i, 0))
    sh = jax.ShapeDtypeStruct((R, C), F32)
    return pl.pallas_call(body, name=name, grid=(R // tr,), in_specs=[spec] * 4, out_specs=[spec] * 3,
                          out_shape=[sh] * 3, compiler_params=_params(("parallel",)))(w, g, m, v)


def sum_slots(name, x):
    n, R, C = x.shape
    tr = _pick(R, (256, 128, 64, 32, 16, 8))

    def body(x_ref, o_ref):
        acc = x_ref[0].astype(F32)
        for s in range(1, n):
            acc = acc + x_ref[s].astype(F32)
        o_ref[...] = acc

    return pl.pallas_call(
        body, name=name, grid=(R // tr,), in_specs=[pl.BlockSpec((n, tr, C), lambda i: (0, i, 0))],
        out_specs=pl.BlockSpec((tr, C), lambda i: (i, 0)), out_shape=jax.ShapeDtypeStruct((R, C), F32),
        compiler_params=_params(("parallel",)))(x)


ANY = pl.BlockSpec(memory_space=pl.ANY)


def _xyc():
    return lax.axis_index("x"), lax.axis_index("y"), lax.axis_index("c")


def gather_shards(shard):
    _, R, C = shard.shape

    def body(x_ref, out_ref, send_sems, recv_sems, local_sem):
        x, y, c = _xyc()
        me, sibling = (x, y, c), (x, y, 1 - c)
        chips = [(1 - x, y), (x, 1 - y), (1 - x, 1 - y)]

        def cp(k, cx, cy, half, to, src=None):
            dst = out_ref.at[2 * cx + cy, half]
            return pltpu.make_async_remote_copy(
                src_ref=dst if src is None else src, dst_ref=dst, send_sem=send_sems.at[k],
                recv_sem=recv_sems.at[k], device_id=to, device_id_type=MESH)

        mine = pltpu.make_async_copy(x_ref, out_ref.at[2 * x + y], local_sem)
        mine.start()
        first = [cp(j, x, y, c, (*chip, c), src=x_ref.at[c]) for j, chip in enumerate(chips)]
        for f in first:
            f.start()
        passed = [cp(3 + j, *chip, c, sibling) for j, chip in enumerate(chips)]
        for j, chip in enumerate(chips):
            cp(j, *chip, c, me).wait_recv()
            passed[j].start()
        for j, chip in enumerate(chips):
            cp(3 + j, *chip, 1 - c, me).wait_recv()
        for f in first + passed:
            f.wait_send()
        mine.wait()

    return pl.pallas_call(
        body, name="gather_shards", in_specs=[ANY], out_specs=ANY,
        out_shape=jax.ShapeDtypeStruct((4, 2, R, C), shard.dtype),
        scratch_shapes=[pltpu.SemaphoreType.DMA((6,)), pltpu.SemaphoreType.DMA((6,)), pltpu.SemaphoreType.DMA])(shard)


FLIPS = [(0, 0, 1), (0, 1, 0), (0, 1, 1), (1, 0, 0), (1, 0, 1), (1, 1, 0), (1, 1, 1)]


def scatter_partials(g):
    _, _, R, C = g.shape

    def body(g_ref, out_ref, send_sems, recv_sems, local_sem):
        x, y, c = _xyc()
        me_idx = 4 * x + 2 * y + c
        mine = pltpu.make_async_copy(g_ref.at[2 * x + y, c], out_ref.at[me_idx], local_sem)
        mine.start()
        sends = []
        for k, (fx, fy, fc) in enumerate(FLIPS):
            px, py, pc = (x + fx) % 2, (y + fy) % 2, (c + fc) % 2
            s = pltpu.make_async_remote_copy(
                src_ref=g_ref.at[2 * px + py, pc], dst_ref=out_ref.at[me_idx], send_sem=send_sems.at[k],
                recv_sem=recv_sems.at[k], device_id=(px, py, pc), device_id_type=MESH)
            s.start()
            sends.append(s)
        for k, (fx, fy, fc) in enumerate(FLIPS):
            px, py, pc = (x + fx) % 2, (y + fy) % 2, (c + fc) % 2
            slot = out_ref.at[4 * px + 2 * py + pc]
            pltpu.make_async_remote_copy(
                src_ref=slot, dst_ref=slot, send_sem=send_sems.at[k], recv_sem=recv_sems.at[k],
                device_id=(px, py, pc), device_id_type=MESH).wait_recv()
        for s in sends:
            s.wait_send()
        mine.wait()

    return pl.pallas_call(
        body, name="scatter_partials", in_specs=[ANY], out_specs=ANY,
        out_shape=jax.ShapeDtypeStruct((8, R, C), g.dtype),
        scratch_shapes=[pltpu.SemaphoreType.DMA((7,)), pltpu.SemaphoreType.DMA((7,)), pltpu.SemaphoreType.DMA])(g)


JOIN_ROWS = 256


def sum_join(x):
    n, R, C = x.shape
    steps = R // JOIN_ROWS

    def body(x_ref, out_ref, acc, local_sems, send_sems, recv_sem):
        i = pl.program_id(0)
        slot = lax.rem(i, 2)
        xx, yy, c = _xyc()
        sibling = (xx, yy, 1 - c)

        def copies(k, s):
            dst = out_ref.at[c, pl.ds(k * JOIN_ROWS, JOIN_ROWS)]
            return (pltpu.make_async_copy(acc.at[s], dst, local_sems.at[s]),
                    pltpu.make_async_remote_copy(src_ref=acc.at[s], dst_ref=dst, send_sem=send_sems.at[s],
                                                 recv_sem=recv_sem, device_id=sibling, device_id_type=MESH))

        def drain(k, s):
            loc, rem = copies(k, s)
            loc.wait()
            rem.wait_send()

        @pl.when(i >= 2)
        def _():
            drain(i - 2, slot)

        a = x_ref[0].astype(F32)
        for s in range(1, n):
            a = a + x_ref[s].astype(F32)
        acc[slot] = a
        loc, rem = copies(i, slot)
        loc.start()
        rem.start()

        @pl.when(i == steps - 1)
        def _():
            if steps >= 2:
                drain(i - 1, 1 - slot)
            drain(i, slot)
            theirs = out_ref.at[1 - c]
            pltpu.make_async_remote_copy(src_ref=theirs, dst_ref=theirs, send_sem=send_sems.at[0], recv_sem=recv_sem,
                                         device_id=sibling, device_id_type=MESH).wait_recv()

    return pl.pallas_call(
        body, name="sum_join", grid=(steps,),
        in_specs=[pl.BlockSpec((n, JOIN_ROWS, C), lambda i: (0, i, 0))], out_specs=ANY,
        out_shape=jax.ShapeDtypeStruct((2, R, C), F32),
        scratch_shapes=[pltpu.VMEM((2, JOIN_ROWS, C), F32), pltpu.SemaphoreType.DMA((2,)),
                        pltpu.SemaphoreType.DMA((2,)), pltpu.SemaphoreType.DMA],
        compiler_params=_params(("arbitrary",)))(x)


def gather_all(name, block):
    R, C = block.shape

    def body(x_ref, out_ref, send_sems, recv_sems, local_sem):
        x, y, c = _xyc()
        mine = pltpu.make_async_copy(x_ref, out_ref.at[4 * x + 2 * y + c], local_sem)
        mine.start()
        sends = []
        for k, (fx, fy, fc) in enumerate(FLIPS):
            px, py, pc = (x + fx) % 2, (y + fy) % 2, (c + fc) % 2
            s = pltpu.make_async_remote_copy(
                src_ref=x_ref, dst_ref=out_ref.at[4 * x + 2 * y + c], send_sem=send_sems.at[k],
                recv_sem=recv_sems.at[k], device_id=(px, py, pc), device_id_type=MESH)
            s.start()
            sends.append(s)
        for k, (fx, fy, fc) in enumerate(FLIPS):
            px, py, pc = (x + fx) % 2, (y + fy) % 2, (c + fc) % 2
            slot = out_ref.at[4 * px + 2 * py + pc]
            pltpu.make_async_remote_copy(
                src_ref=slot, dst_ref=slot, send_sem=send_sems.at[k], recv_sem=recv_sems.at[k],
                device_id=(px, py, pc), device_id_type=MESH).wait_recv()
        for s in sends:
            s.wait_send()
        mine.wait()

    return pl.pallas_call(
        body, name=name, in_specs=[ANY], out_specs=ANY,
        out_shape=jax.ShapeDtypeStruct((8, R, C), block.dtype),
        scratch_shapes=[pltpu.SemaphoreType.DMA((7,)), pltpu.SemaphoreType.DMA((7,)), pltpu.SemaphoreType.DMA])(block)


PACK_C = 1024


def _pack(arrs, row_mult):
    flat = jnp.concatenate([a.reshape(-1) for a in arrs])
    n = flat.shape[0]
    rows = -(-n // PACK_C)
    rows = -(-rows // row_mult) * row_mult
    return jnp.pad(flat, (0, rows * PACK_C - n)).reshape(rows, PACK_C)


def _unpack(buf, shapes):
    flat = buf.reshape(-1)
    out, off = [], 0
    for s in shapes:
        n = int(np.prod(s))
        out.append(flat[off:off + n].reshape(s))
        off += n
    return out


OFF_Q, OFF_CKV, OFF_KR, OFF_SG, OFF_RW, OFF_GATE, N_IN_PAD = 0, 384, 640, 896, 1920, 3840, 6912
IN_SEGMENTS = [('q', OFF_Q, OFF_CKV), ('ckv', OFF_CKV, OFF_KR), ('kr', OFF_KR, OFF_SG), ('sg', OFF_SG, OFF_RW),
               ('rw', OFF_RW, OFF_GATE), ('gate', OFF_GATE, N_IN_PAD)]
ROPE_LANE = QK_NOPE
HALF = QK_ROPE // 2


def _win_layout():
    src = np.full((N_IN_PAD,), -1, np.int64)
    sgn = np.ones((N_IN_PAD,), np.float32)
    src[0:640] = np.arange(0, 640)
    kr0 = Q_LORA + KV_LORA
    src[OFF_KR + ROPE_LANE:OFF_KR + ROPE_LANE + QK_ROPE] = kr0 + np.arange(QK_ROPE)
    sw = OFF_KR + HEAD_PAD + ROPE_LANE
    src[sw:sw + HALF] = kr0 + HALF + np.arange(HALF)
    sgn[sw:sw + HALF] = -1.0
    src[sw + HALF:sw + QK_ROPE] = kr0 + np.arange(HALF)
    src[OFF_SG:N_IN_PAD] = 672 + np.arange(N_IN_PAD - OFF_SG)
    return src, sgn


def _wuq_layout():
    hw = MLA_HEADS * HEAD_PAD
    src = np.full((2 * hw,), -1, np.int64)
    sgn = np.ones((2 * hw,), np.float32)
    per = QK_NOPE + QK_ROPE
    for h in range(MLA_HEADS):
        src[h * HEAD_PAD:h * HEAD_PAD + per] = h * per + np.arange(per)
        sw = hw + h * HEAD_PAD + ROPE_LANE
        src[sw:sw + HALF] = h * per + QK_NOPE + HALF + np.arange(HALF)
        sgn[sw:sw + HALF] = -1.0
        src[sw + HALF:sw + QK_ROPE] = h * per + QK_NOPE + np.arange(HALF)
    return src, sgn


def _runs(idx, sgn):
    out, lo = [], 0
    for pos in range(1, len(idx) + 1):
        if pos == len(idx) or not (
                (idx[pos] == -1 and idx[pos - 1] == -1)
                or (idx[pos - 1] >= 0 and idx[pos] == idx[pos - 1] + 1 and sgn[pos] == sgn[pos - 1])):
            out.append((lo, pos, int(idx[lo]), float(sgn[lo])))
            lo = pos
    return out


def _select_cols(w, idx, sgn):
    pieces = []
    for lo, hi, s0, sg in _runs(idx, sgn):
        if s0 < 0:
            pieces.append(jnp.zeros((w.shape[0], hi - lo), w.dtype))
        else:
            piece = w[:, s0:s0 + hi - lo]
            pieces.append(piece if sg > 0 else -piece)
    return jnp.concatenate(pieces, axis=1)


def _permute_cols(w, src, sgn):
    return _select_cols(w, src, sgn)


def _unpermute_full(dw, src, sgn, n_cols):
    first = np.full((n_cols,), -1, np.int64)
    second = np.full((n_cols,), -1, np.int64)
    for pos, s in enumerate(src):
        if s < 0:
            continue
        if first[s] < 0:
            first[s] = pos
        else:
            second[s] = pos
    sg2 = np.where(second >= 0, sgn[np.maximum(second, 0)], 1.0)
    return _select_cols(dw, first, sgn[first]), _select_cols(dw, second, sg2)


def _blockdiag(w):
    z = jnp.zeros_like(w[0])
    return jnp.concatenate([jnp.concatenate([w[0], z], axis=1), jnp.concatenate([z, w[1]], axis=1)], axis=0)


def _rope_tables(pos):
    inv = 1.0 / (ROPE_THETA ** (jnp.arange(0, QK_ROPE, 2, dtype=F32) / QK_ROPE))
    ang = pos.astype(F32)[:, None] * inv[None, :]
    cos, sin = jnp.cos(ang), jnp.sin(ang)
    pad = lambda t, fill: jnp.concatenate(
        [jnp.full((t.shape[0], ROPE_LANE), fill, F32), t, t, jnp.full((t.shape[0], HEAD_PAD - ROPE_LANE - QK_ROPE), fill, F32)], axis=1)
    return pad(cos, 1.0), pad(sin, 0.0)


def kernel(x, positions, attn_norm_g, w_in, gate_b, q_norm_g, w_uq, kv_norm_g, w_ukv, sg_ln_g, sg_ln_b, sg_w, sg_b, rw_mu, rw_w0, rw_w2, rw_a0, rw_a2, rw_g2, rw_k_k, rw_k_a, rw_r_k, rw_ln_g, rw_ln_b, w_branch, w_out, ffn_norm_g, w_ffn_gate, w_ffn_up, w_ffn_down, final_norm_g, loss_target, m_attn_norm_g, m_w_in, m_gate_b, m_q_norm_g, m_w_uq, m_kv_norm_g, m_w_ukv, m_sg_ln_g, m_sg_ln_b, m_sg_w, m_sg_b, m_rw_mu, m_rw_w0, m_rw_w2, m_rw_a0, m_rw_a2, m_rw_g2, m_rw_k_k, m_rw_k_a, m_rw_r_k, m_rw_ln_g, m_rw_ln_b, m_w_branch, m_w_out, m_ffn_norm_g, m_w_ffn_gate, m_w_ffn_up, m_w_ffn_down, m_final_norm_g, v_attn_norm_g, v_w_in, v_gate_b, v_q_norm_g, v_w_uq, v_kv_norm_g, v_w_ukv, v_sg_ln_g, v_sg_ln_b, v_sg_w, v_sg_b, v_rw_mu, v_rw_w0, v_rw_w2, v_rw_a0, v_rw_a2, v_rw_g2, v_rw_k_k, v_rw_k_a, v_rw_r_k, v_rw_ln_g, v_rw_ln_b, v_w_branch, v_w_out, v_ffn_norm_g, v_w_ffn_gate, v_w_ffn_up, v_w_ffn_down, v_final_norm_g):
    args = locals()
    W = {n: args[n] for n in WEIGHTS}
    M1 = {n: args['m_' + n] for n in WEIGHTS}
    M2 = {n: args['v_' + n] for n in WEIGHTS}
    B, S, D = x.shape
    N = B * S
    TM = _pick(N, (256, 128))
    TMH = 128
    TQ = _pick(S, (512, 256, 128))

    shard_shapes = [W[n].shape for n in SHARDED]
    full = {}
    mm_pack = _pack([W[n].astype(BF16) for n in MATMUL_SHARDED], 32)
    Rm = mm_pack.shape[0]
    gathered = gather_shards(mm_pack.reshape(2, Rm // 2, PACK_C)).reshape(4, Rm, PACK_C)
    flat, off = gathered.reshape(4, -1), 0
    for n in MATMUL_SHARDED:
        shp, ax = W[n].shape, SHARD_AXIS[n]
        n_el = int(np.prod(shp))
        t = jnp.moveaxis(flat[:, off:off + n_el].reshape((4,) + shp), 0, ax)
        full[n] = t.reshape(shp[:ax] + (4 * shp[ax],) + shp[ax + 1:])
        off += n_el
    small = gather_all("gather_small", _pack([W[n] for n in SMALL_SHARDED], 8))
    pieces = [_unpack(small[2 * q], [W[n].shape for n in SMALL_SHARDED]) for q in range(4)]
    for i, n in enumerate(SMALL_SHARDED):
        full[n] = jnp.concatenate([pieces[q][i] for q in range(4)], axis=SHARD_AXIS[n])
    for n in ('rw_w2', 'rw_a2', 'rw_g2'):
        full[n] = full[n].astype(F32)
    for n in REPLICATED:
        full[n] = W[n]

    win_src, win_sgn = _win_layout()
    wuq_src, wuq_sgn = _wuq_layout()
    ones = jnp.asarray(np.kron(np.eye(RWKV_HEADS), np.ones((RWKV_HEAD, RWKV_HEAD))), BF16)
    ct, st = _rope_tables(positions.reshape(N))
    row = lambda v: v.reshape(1, -1)

    H, HD = RWKV_HEADS, RWKV_HEAD
    inst = 2 * B * H
    isplit = LANES // inst
    Ip = HD // isplit
    to_j = lambda nm, c: to_scan(nm, c, B, S, c.shape[1] // RWKV_DIM, None)
    to_i = lambda nm, c: to_scan(nm, c, B, S, 1, Ip)

    def shift_prev(z):
        z = z.reshape(B, S, -1)
        return jnp.pad(z[:, :-1], ((0, 0), (1, 0), (0, 0))).reshape(N, -1)

    def shift_next(z):
        z = z.reshape(B, S, -1)
        return jnp.pad(z[:, 1:], ((0, 0), (0, 1), (0, 0))).reshape(N, -1)

    LW = []
    for l in range(DEPTH):
        wb = full['w_branch'][l]
        wb0 = jnp.zeros((MLA_HEADS, HEAD_PAD, D), F32).at[:, QK_NOPE:].set(wb[0].reshape(MLA_HEADS, V_HEAD, D))
        LW.append(dict(
            attn_g=row(full['attn_norm_g'][l]),
            w_in=_permute_cols(full['w_in'][l], win_src, win_sgn),
            gate_b=row(full['gate_b'][l]),
            q_g=row(full['q_norm_g'][l]),
            w_uq=_permute_cols(full['w_uq'][l], wuq_src, wuq_sgn),
            kv_g=row(full['kv_norm_g'][l]),
            w_ukv=full['w_ukv'][l],
            sg_g=row(full['sg_ln_g'][l]), sg_b=row(full['sg_ln_b'][l]), sg_w=full['sg_w'][l],
            sg_bias=jnp.repeat(full['sg_b'][l].T, SG_DIM // SG_GROUPS, axis=1),
            mu=row(full['rw_mu'][l]), w0=row(full['rw_w0'][l]), w2=_blockdiag(full['rw_w2'][l]),
            a0=row(full['rw_a0'][l]), a2=_blockdiag(full['rw_a2'][l]), g2=full['rw_g2'][l],
            k_k=row(full['rw_k_k'][l]), k_a=row(full['rw_k_a'][l]), r_k=row(full['rw_r_k'][l]),
            ln_g=row(full['rw_ln_g'][l]), ln_b=row(full['rw_ln_b'][l]),
            wb0=wb0.reshape(MLA_HEADS * HEAD_PAD, D), wb1=wb[1], wb2=wb[2],
            w_out=full['w_out'][l], ffn_g=row(full['ffn_norm_g'][l]),
            w_gu=jnp.concatenate([full['w_ffn_gate'][l], full['w_ffn_up'][l]], axis=1),
            w_down=full['w_ffn_down'][l]))

    saved = []
    xc = x.reshape(N, D)
    for l in range(DEPTH):
        p = LW[l]
        t = 'l%d_' % l
        sv = dict(x=xc)
        (h,) = rowwise(t + 'attn_norm', f_rms, [xc], [p['attn_g']], [], [D], TM, [BF16])
        p_q, p_ckv, p_kr, p_sg, z, p_gate = [
            matmul(t + 'in_proj_' + sn, h, p['w_in'][:, a:b], 'nn') for sn, a, b in IN_SEGMENTS]
        sv['h'] = h
        (cq,) = rowwise(t + 'q_norm', f_rms, [p_q], [p['q_g']], [], [Q_LORA], TM, [BF16])
        (ckv,) = rowwise(t + 'kv_norm', f_rms, [p_ckv], [p['kv_g']], [], [KV_LORA], TM, [BF16])
        qq = matmul(t + 'uq', cq, p['w_uq'], 'nn')
        kv = matmul(t + 'ukv', ckv, p['w_ukv'], 'nn')
        qh, kh = rowwise(t + 'rope', f_rope, [qq, kv, p_kr, ct, st], [], [], [MLA_HEADS * HEAD_PAD] * 2, TM)
        ya = attention_fwd(t + 'attn', qh, kh, kv, B, S, TQ, BF16)
        sv.update(p_q=p_q, p_ckv=p_ckv, p_kr=p_kr, cq=cq, ckv=ckv, qq=qq, kv=kv, qh=qh, kh=kh, ya=ya)
        (yb,) = rowwise(t + 'sg', f_sg, [p_sg], [p['sg_g'], p['sg_b'], p['sg_w'], p['sg_bias']], [], [SG_DIM], TM,
                        [BF16])
        sv.update(p_sg=p_sg, yb=yb)
        zp, zn = shift_prev(z), shift_next(z)
        rw_par = [p['mu'], p['w0'], p['w2'], p['a0'], p['a2'], p['g2'], p['k_k'], p['k_a']]
        r_, v_, decay, kdir, kk, bdir, g_ = rowwise(
            t + 'rw_pre', f_rw_pre, [z, zp, zn], rw_par, [ones],
            [RWKV_DIM, RWKV_DIM, 2 * RWKV_DIM, 2 * RWKV_DIM, RWKV_DIM, 2 * RWKV_DIM, RWKV_DIM], TM)
        sc = dict(w=to_j(t + 'lay_w', decay), k=to_j(t + 'lay_k', kdir), b=to_j(t + 'lay_b', bdir),
                  kk=to_j(t + 'lay_kk', kk), r=to_j(t + 'lay_r', r_), v=to_i(t + 'lay_v', v_))
        y_f, y_r, sp, sa_all, s_last = scan_fwd(t + 'scan', sc['w'], sc['k'], sc['b'], sc['kk'], sc['r'], sc['v'])
        ysum = from_scan(t + 'lay_y', y_f, y_r, B, S, 1, True)
        (yc,) = rowwise(t + 'rw_post', f_rw_post, [ysum, r_, v_, kdir, g_], [p['ln_g'], p['ln_b'], p['r_k']],
                        [ones], [RWKV_DIM], TM, [BF16])
        sv.update(z=z, zp=zp, zn=zn, r=r_, v=v_, kdir=kdir, g=g_, sc=sc, sp=sp, sa=sa_all, s_last=s_last,
                  ysum=ysum, yc=yc)
        b0 = matmul(t + 'br0', ya, p['wb0'], 'nn')
        b1 = matmul(t + 'br1', yb, p['wb1'], 'nn')
        b2 = matmul(t + 'br2', yc, p['wb2'], 'nn')
        (merged,) = rowwise(t + 'merge', f_merge, [p_gate, b0, b1, b2], [p['gate_b']], [], [D], TM, [BF16])
        x2 = matmul(t + 'out_proj', merged, p['w_out'], 'nn', add=xc)
        sv.update(p_gate=p_gate, b0=b0, b1=b1, b2=b2, merged=merged, x2=x2)
        (h2,) = rowwise(t + 'ffn_norm', f_rms, [x2], [p['ffn_g']], [], [D], TM, [BF16])
        au = matmul(t + 'ffn_in', h2, p['w_gu'], 'nn')
        (act,) = rowwise(t + 'swiglu', f_swiglu, [au], [], [], [D_FF], TM, [BF16])
        xc = matmul(t + 'ffn_out', act, p['w_down'], 'nn', add=x2)
        sv.update(h2=h2, au=au, act=act)
        saved.append(sv)

    loss_part, dx, d_final_g = loss_head(xc, loss_target.reshape(N, D), row(full['final_norm_g']), TM)
    loss = lax.psum(loss_part[0, 0], ("x", "y", "c"))

    G = {n: [None] * DEPTH for n in WEIGHTS if n != 'final_norm_g'}
    for l in reversed(range(DEPTH)):
        p, sv = LW[l], saved[l]
        t = 'l%d_bwd_' % l
        d_act = matmul(t + 'ffn_out_dx', dx, p['w_down'], 'nt')
        G['w_ffn_down'][l] = matmul(t + 'ffn_out_dw', sv['act'], dx, 'tn')
        (d_au,), _ = rowwise_bwd(t + 'swiglu', f_swiglu, [sv['au']], [], [], [[d_act]], TMH, drow_dtypes=[BF16])
        d_h2 = matmul(t + 'ffn_in_dx', d_au, p['w_gu'], 'nt')
        d_wgu = matmul(t + 'ffn_in_dw', sv['h2'], d_au, 'tn')
        G['w_ffn_gate'][l], G['w_ffn_up'][l] = d_wgu[:, :D_FF], d_wgu[:, D_FF:]
        (dx2,), (dg,) = rowwise_bwd(t + 'ffn_norm', f_rms, [sv['x2']], [p['ffn_g']], [], [[d_h2]], TM, extra=[(0, dx)])
        G['ffn_norm_g'][l] = dg.reshape(-1)
        d_merged = matmul(t + 'out_proj_dx', dx2, p['w_out'], 'nt')
        G['w_out'][l] = matmul(t + 'out_proj_dw', sv['merged'], dx2, 'tn')
        (d_pgate, d_b0, d_b1, d_b2), (d_gate_b,) = rowwise_bwd(
            t + 'merge', f_merge, [sv['p_gate'], sv['b0'], sv['b1'], sv['b2']], [p['gate_b']], [], [[d_merged]], TM,
            drow_dtypes=[BF16] * 4)
        G['gate_b'][l] = d_gate_b.reshape(3, D)
        d_ya = matmul(t + 'br0_dx', d_b0, p['wb0'], 'nt')
        d_yb = matmul(t + 'br1_dx', d_b1, p['wb1'], 'nt')
        d_yc = matmul(t + 'br2_dx', d_b2, p['wb2'], 'nt')
        d_wb0 = matmul(t + 'br0_dw', sv['ya'], d_b0, 'tn').reshape(MLA_HEADS, HEAD_PAD, D)[:, QK_NOPE:].reshape(-1, D)
        G['w_branch'][l] = jnp.stack([d_wb0, matmul(t + 'br1_dw', sv['yb'], d_b1, 'tn'),
                                      matmul(t + 'br2_dw', sv['yc'], d_b2, 'tn')])
        (d_y, d_r1, d_v1, d_kdir1, d_g), (d_ln_g, d_ln_b, d_r_k) = rowwise_bwd(
            t + 'rw_post', f_rw_post, [sv['ysum'], sv['r'], sv['v'], sv['kdir'], sv['g']],
            [p['ln_g'], p['ln_b'], p['r_k']], [ones], [[d_yc]], TMH)
        G['rw_ln_g'][l], G['rw_ln_b'][l] = d_ln_g.reshape(-1), d_ln_b.reshape(-1)
        G['rw_r_k'][l] = d_r_k.reshape(RWKV_HEADS, RWKV_HEAD)
        sc = sv['sc']
        res = scan_bwd(t + 'scan', sc['w'], sc['k'], sc['b'], sc['kk'], sc['r'], sc['v'], sv['sp'], sv['sa'],
                       sv['s_last'], to_i(t + 'lay_dy', d_y))
        s_dw, s_dk, s_db, s_dkk, s_dr, s_dv = [
            from_scan(t + 'lay_' + nm, res[2 * i], res[2 * i + 1], B, S, nd, nm == 'dv')
            for i, (nm, nd) in enumerate((('dw', 2), ('dk', 2), ('db', 2), ('dkk', 1), ('dr', 1), ('dv', 1)))]
        rw_par = [p['mu'], p['w0'], p['w2'], p['a0'], p['a2'], p['g2'], p['k_k'], p['k_a']]
        d_outs = [[d_r1, s_dr], [d_v1, s_dv], [s_dw], [d_kdir1, s_dk], [s_dkk], [s_db], [d_g]]
        (d_z, d_zp, d_zn), d_rw = rowwise_bwd(
            t + 'rw_pre', f_rw_pre, [sv['z'], sv['zp'], sv['zn']], rw_par, [ones], d_outs, TMH)
        (d_prw,) = rowwise(t + 'shift_sum', f_add3, [d_z, shift_next(d_zp), shift_prev(d_zn)], [], [], [RWKV_IN], TM,
                           [BF16])
        G['rw_mu'][l] = d_rw[0].reshape(-1)
        G['rw_w0'][l] = d_rw[1].reshape(2, RWKV_DIM)
        G['rw_w2'][l] = jnp.stack([d_rw[2][:64, :RWKV_DIM], d_rw[2][64:, RWKV_DIM:]])
        G['rw_a0'][l] = d_rw[3].reshape(2, RWKV_DIM)
        G['rw_a2'][l] = jnp.stack([d_rw[4][:64, :RWKV_DIM], d_rw[4][64:, RWKV_DIM:]])
        G['rw_g2'][l] = d_rw[5]
        G['rw_k_k'][l], G['rw_k_a'][l] = d_rw[6].reshape(-1), d_rw[7].reshape(-1)
        (d_psg,), (d_sg_g, d_sg_b, d_sg_w, d_sg_bias) = rowwise_bwd(
            t + 'sg', f_sg, [sv['p_sg']], [p['sg_g'], p['sg_b'], p['sg_w'], p['sg_bias']], [], [[d_yb]], TMH,
            drow_dtypes=[BF16])
        G['sg_ln_g'][l], G['sg_ln_b'][l], G['sg_w'][l] = d_sg_g.reshape(-1), d_sg_b.reshape(-1), d_sg_w
        G['sg_b'][l] = d_sg_bias.reshape(SG_CHUNK, SG_GROUPS, SG_DIM // SG_GROUPS).sum(-1).T
        d_qh, d_kh, d_kvv = attention_bwd(t + 'attn', sv['qh'], sv['kh'], sv['kv'], d_ya, B, S, TQ)
        (d_qq, d_kv, d_pkr), _ = rowwise_bwd(
            t + 'rope', f_rope, [sv['qq'], sv['kv'], sv['p_kr'], ct, st], [], [], [[d_qh], [d_kh]], TM,
            n_row_diff=3, extra=[(1, d_kvv)], drow_dtypes=[BF16] * 3)
        d_cq = matmul(t + 'uq_dx', d_qq, p['w_uq'], 'nt')
        d_wuq = matmul(t + 'uq_dw', sv['cq'], d_qq, 'tn')
        g1, g2_ = _unpermute_full(d_wuq, wuq_src, wuq_sgn, MLA_HEADS * (QK_NOPE + QK_ROPE))
        G['w_uq'][l] = g1 + g2_
        d_ckv = matmul(t + 'ukv_dx', d_kv, p['w_ukv'], 'nt')
        G['w_ukv'][l] = matmul(t + 'ukv_dw', sv['ckv'], d_kv, 'tn')
        (d_pq,), (dg,) = rowwise_bwd(t + 'q_norm', f_rms, [sv['p_q']], [p['q_g']], [], [[d_cq]], TM,
                                     drow_dtypes=[BF16])
        G['q_norm_g'][l] = dg.reshape(-1)
        (d_pckv,), (dg,) = rowwise_bwd(t + 'kv_norm', f_rms, [sv['p_ckv']], [p['kv_g']], [], [[d_ckv]], TM,
                                       drow_dtypes=[BF16])
        G['kv_norm_g'][l] = dg.reshape(-1)
        d_h, d_cols = None, []
        for (sn, a, b), d_seg in zip(IN_SEGMENTS, [d_pq, d_pckv, d_pkr, d_psg, d_prw, d_pgate]):
            d_h = matmul(t + 'in_proj_dx_' + sn, d_seg, p['w_in'][:, a:b], 'nt', add=d_h)
            d_cols.append(matmul(t + 'in_proj_dw_' + sn, sv['h'], d_seg, 'tn'))
        d_win = jnp.concatenate(d_cols, axis=1)
        g1, g2_ = _unpermute_full(d_win, win_src, win_sgn, N_IN)
        kr0 = Q_LORA + KV_LORA
        G['w_in'][l] = g1.at[:, kr0:kr0 + QK_ROPE].add(g2_[:, kr0:kr0 + QK_ROPE])
        (dx,), (dg,) = rowwise_bwd(t + 'attn_norm', f_rms, [sv['x']], [p['attn_g']], [], [[d_h]], TM, extra=[(0, dx2)])
        G['attn_norm_g'][l] = dg.reshape(-1)

    grads = {n: jnp.stack(G[n]) for n in G}
    grads['final_norm_g'] = d_final_g.reshape(-1)
    grad_x = dx.reshape(B, S, D)

    cols = []
    for n in SHARDED:
        g, ax = grads[n], SHARD_AXIS[n]
        g4 = g.reshape(g.shape[:ax] + (4, W[n].shape[ax]) + g.shape[ax + 1:])
        cols.append(jnp.moveaxis(g4, ax, 0).reshape(4, -1).astype(BF16))
    flat = jnp.concatenate(cols, axis=1)
    chunk = 2 * JOIN_ROWS * PACK_C
    R = -(-flat.shape[1] // chunk) * (2 * JOIN_ROWS)
    gpack = jnp.pad(flat, ((0, 0), (0, R * PACK_C - flat.shape[1]))).reshape(4, 2, R // 2, PACK_C)
    both = sum_join(scatter_partials(gpack))
    g_shard = dict(zip(SHARDED, _unpack(both.reshape(R, PACK_C), shard_shapes)))
    rep_shapes = [W[n].shape for n in REPLICATED]
    rpack = _pack([grads[n] for n in REPLICATED], 8)
    g_rep = sum_slots("sum_replicated", gather_all("gather_replicated", rpack))

    outs = {}
    for n in MATMUL_SHARDED:
        shp = W[n].shape
        two = lambda a: a.reshape(-1, shp[-1])
        res = adamw("adamw_" + n, two(W[n]), two(g_shard[n]), two(M1[n]), two(M2[n]))
        outs['grad', n] = g_shard[n]
        for key, a in zip(('delta', 'new_m', 'new_v'), res):
            outs[key, n] = a.reshape(shp)
    small = SMALL_SHARDED + REPLICATED
    small_shapes = [W[n].shape for n in small]
    g_small = [g_shard[n] for n in SMALL_SHARDED] + _unpack(g_rep, rep_shapes)
    res = adamw("adamw_small", _pack([W[n] for n in small], 8), _pack(g_small, 8),
                _pack([M1[n] for n in small], 8), _pack([M2[n] for n in small], 8))
    for n, a in zip(small, g_small):
        outs['grad', n] = a
    for key, buf in zip(('delta', 'new_m', 'new_v'), res):
        for n, a in zip(small, _unpack(buf, small_shapes)):
            outs[key, n] = a
    return (loss, grad_x, *[outs['grad', n] for n in WEIGHTS], *[outs['delta', n] for n in WEIGHTS],
            *[outs['new_m', n] for n in WEIGHTS], *[outs['new_v', n] for n in WEIGHTS])
```

```python
import functools
import math

import numpy as np
import jax
import jax.numpy as jnp
from jax import lax
from jax.experimental import pallas as pl
from jax.experimental.pallas import tpu as pltpu

F32 = jnp.float32
BF16 = jnp.bfloat16

DEPTH = 2
MLA_HEADS = 8
Q_LORA = 384
KV_LORA = 256
QK_NOPE = 64
QK_ROPE = 32
V_HEAD = 64
ROPE_THETA = 10000.0
SG_GROUPS = 8
SG_DIM = 512
SG_CHUNK = 128
RWKV_HEADS = 8
RWKV_HEAD = 64
RWKV_DIM = 512
GN_EPS = 64e-5
NORM_EPS = 1e-6
D_FF = 2816
RWKV_IN = 1920
N_IN = 6688
ADAM_LR, ADAM_B1, ADAM_B2, ADAM_EPS, ADAM_WD, ADAM_STEP = 0.001, 0.9, 0.999, 1e-08, 0.01, 10

LANES = 128
HEAD_PAD = 128
VMEM_LIMIT = 56 * 1024 * 1024
MESH = pl.DeviceIdType.MESH

WEIGHTS = ['attn_norm_g', 'w_in', 'gate_b', 'q_norm_g', 'w_uq', 'kv_norm_g', 'w_ukv', 'sg_ln_g', 'sg_ln_b', 'sg_w',
           'sg_b', 'rw_mu', 'rw_w0', 'rw_w2', 'rw_a0', 'rw_a2', 'rw_g2', 'rw_k_k', 'rw_k_a', 'rw_r_k', 'rw_ln_g',
           'rw_ln_b', 'w_branch', 'w_out', 'ffn_norm_g', 'w_ffn_gate', 'w_ffn_up', 'w_ffn_down', 'final_norm_g']
SHARD_AXIS = {'w_in': 2, 'gate_b': 2, 'w_uq': 2, 'w_ukv': 2, 'rw_w0': 2, 'rw_w2': 3, 'rw_a0': 2, 'rw_a2': 3,
              'rw_g2': 2, 'w_branch': 3, 'w_out': 1, 'w_ffn_gate': 2, 'w_ffn_up': 2, 'w_ffn_down': 1}
SHARDED = [n for n in WEIGHTS if n in SHARD_AXIS]
REPLICATED = [n for n in WEIGHTS if n not in SHARD_AXIS]
SMALL_SHARDED = ['gate_b', 'rw_w0', 'rw_a0']
MATMUL_SHARDED = [n for n in SHARDED if n not in SMALL_SHARDED]


def _params(sem=None):
    return pltpu.CompilerParams(dimension_semantics=sem, vmem_limit_bytes=VMEM_LIMIT)


def _pick(n, cands):
    for c in cands:
        if n % c == 0:
            return c
    return n


def _dot(a, b, dims):
    return lax.dot_general(a.astype(BF16), b.astype(BF16), (dims, ((), ())), preferred_element_type=F32)


def _nn(a, b):
    return _dot(a, b, ((1,), (0,)))


def _nt(a, b):
    return _dot(a, b, ((1,), (1,)))


def _tn(a, b):
    return _dot(a, b, ((0,), (0,)))


@jax.custom_vjp
def mm(a, b):
    return _nn(a, b)


mm.defvjp(lambda a, b: (_nn(a, b), (a, b)), lambda res, g: (_nt(g, res[1]), _tn(res[0], g)))


@jax.custom_vjp
def mm_nt(a, b):
    return _nt(a, b)


mm_nt.defvjp(lambda a, b: (_nt(a, b), (a, b)), lambda res, g: (_nn(g, res[1]), _tn(g, res[0])))


def _seg_raw(x, ones):
    hi = x.astype(BF16)
    lo = (x - hi.astype(F32)).astype(BF16)
    d = (((1,), (0,)), ((), ()))
    return (lax.dot_general(hi, ones, d, preferred_element_type=F32)
            + lax.dot_general(lo, ones, d, preferred_element_type=F32))


@jax.custom_vjp
def segsum(x, ones):
    return _seg_raw(x, ones)


segsum.defvjp(lambda x, ones: (_seg_raw(x, ones), ones),
              lambda ones, g: (_seg_raw(g, ones), jnp.zeros_like(ones)))


def _sigmoid(x):
    return 0.5 * (jnp.tanh(0.5 * x) + 1.0)


def _rms(x, g):
    return x * lax.rsqrt(jnp.mean(x * x, axis=-1, keepdims=True) + NORM_EPS) * g


def matmul(name, a, b, mode, add=None, out_dtype=F32):
    if mode == 'nn':
        (M, K), (_, N) = a.shape, b.shape
    elif mode == 'nt':
        (M, K), (N, _) = a.shape, b.shape
    else:
        (K, M), (_, N) = a.shape, b.shape
    tm = _pick(M, (1408, 1024, 512, 384, 256, 128))
    tn = _pick(N, (1408, 1024, 768, 512, 384, 256, 128))
    tk = _pick(K, (512, 384, 256, 128))
    nk = K // tk
    dims = {'nn': ((1,), (0,)), 'nt': ((1,), (1,)), 'tn': ((0,), (0,))}[mode]
    a_spec = pl.BlockSpec((tk, tm), lambda i, j, k: (k, i)) if mode == 'tn' else pl.BlockSpec((tm, tk), lambda i, j, k: (i, k))
    b_spec = pl.BlockSpec((tn, tk), lambda i, j, k: (j, k)) if mode == 'nt' else pl.BlockSpec((tk, tn), lambda i, j, k: (k, j))
    o_spec = pl.BlockSpec((tm, tn), lambda i, j, k: (i, j))
    has_add = add is not None

    def body(*refs):
        if has_add:
            a_ref, b_ref, add_ref, o_ref, acc = refs
        else:
            a_ref, b_ref, o_ref, acc = refs
        k = pl.program_id(2)

        @pl.when(k == 0)
        def _():
            acc[...] = jnp.zeros_like(acc)

        acc[...] += _dot(a_ref[...], b_ref[...], dims)

        @pl.when(k == nk - 1)
        def _():
            o_ref[...] = (acc[...] + add_ref[...] if has_add else acc[...]).astype(o_ref.dtype)

    ins = [a, b] + ([add] if has_add else [])
    specs = [a_spec, b_spec] + ([o_spec] if has_add else [])
    return pl.pallas_call(
        body, name=name, grid=(M // tm, N // tn, nk), in_specs=specs, out_specs=o_spec,
        out_shape=jax.ShapeDtypeStruct((M, N), out_dtype), scratch_shapes=[pltpu.VMEM((tm, tn), F32)],
        compiler_params=_params(("parallel", "parallel", "arbitrary")))(*ins)


def _full_spec(p):
    nd = p.ndim
    return pl.BlockSpec(p.shape, lambda i, _nd=nd: (0,) * _nd)


def rowwise(name, fn, rows, params, consts, out_widths, tm, out_dtypes=None):
    N = rows[0].shape[0]
    nr, npar, nc = len(rows), len(params), len(consts)

    def body(*refs):
        vals = [r[...] for r in refs[:nr + npar + nc]]
        res = fn(*vals)
        for o, v in zip(refs[nr + npar + nc:], res):
            o[...] = v.astype(o.dtype)

    in_specs = ([pl.BlockSpec((tm, r.shape[1]), lambda i: (i, 0)) for r in rows]
                + [_full_spec(p) for p in list(params) + list(consts)])
    out_specs = [pl.BlockSpec((tm, w), lambda i: (i, 0)) for w in out_widths]
    return pl.pallas_call(
        body, name=name, grid=(N // tm,), in_specs=in_specs, out_specs=out_specs,
        out_shape=[jax.ShapeDtypeStruct((N, w), d) for w, d in zip(out_widths, out_dtypes or [F32] * len(out_widths))],
        compiler_params=_params(("parallel",)))(*rows, *params, *consts)


def rowwise_bwd(name, fn, rows, params, consts, d_outs, tm, n_row_diff=None, extra=(), drow_dtypes=None):
    N = rows[0].shape[0]
    nr, npar, nc = len(rows), len(params), len(consts)
    nd = nr if n_row_diff is None else n_row_diff
    counts = [len(p) for p in d_outs]
    flat_d = [a for parts in d_outs for a in parts]
    nflat, nex = len(flat_d), len(extra)

    def body(*refs):
        pos = 0
        row_v = [r[...] for r in refs[pos:pos + nr]]; pos += nr
        par_v = [r[...] for r in refs[pos:pos + npar]]; pos += npar
        con_v = [r[...] for r in refs[pos:pos + nc]]; pos += nc
        d_refs = refs[pos:pos + nflat]; pos += nflat
        ex_refs = refs[pos:pos + nex]; pos += nex
        drow_refs = refs[pos:pos + nd]; pos += nd
        dpar_refs = refs[pos:pos + npar]

        def f(*diff):
            return fn(*diff[:nd], *row_v[nd:], *diff[nd:], *con_v)

        _, vjp = jax.vjp(f, *row_v[:nd], *par_v)
        cts, q = [], 0
        for c in counts:
            g = d_refs[q][...].astype(F32)
            for t in range(1, c):
                g = g + d_refs[q + t][...].astype(F32)
            cts.append(g)
            q += c
        grads = vjp(tuple(cts))
        drow = list(grads[:nd])
        for (idx, _), r in zip(extra, ex_refs):
            drow[idx] = drow[idx] + r[...].astype(F32)
        for o, v in zip(drow_refs, drow):
            o[...] = v.astype(o.dtype)

        @pl.when(pl.program_id(0) == 0)
        def _():
            for o in dpar_refs:
                o[...] = jnp.zeros_like(o)

        for o, v in zip(dpar_refs, grads[nd:]):
            o[...] += v

    ex_arrs = [a for _, a in extra]
    in_specs = ([pl.BlockSpec((tm, r.shape[1]), lambda i: (i, 0)) for r in rows]
                + [_full_spec(p) for p in list(params) + list(consts)]
                + [pl.BlockSpec((tm, a.shape[1]), lambda i: (i, 0)) for a in flat_d + ex_arrs])
    out_specs = ([pl.BlockSpec((tm, r.shape[1]), lambda i: (i, 0)) for r in rows[:nd]]
                 + [_full_spec(p) for p in params])
    out_shape = ([jax.ShapeDtypeStruct(r.shape, d) for r, d in zip(rows[:nd], drow_dtypes or [F32] * nd)]
                 + [jax.ShapeDtypeStruct(p.shape, F32) for p in params])
    res = pl.pallas_call(
        body, name=name, grid=(N // tm,), in_specs=in_specs, out_specs=out_specs, out_shape=out_shape,
        compiler_params=_params(("arbitrary",)))(*rows, *params, *consts, *flat_d, *ex_arrs)
    return list(res[:nd]), list(res[nd:])


def f_rms(x, g):
    return (_rms(x, g),)


def f_rope(qq, kv, krr, ct, st):
    hw = MLA_HEADS * HEAD_PAD
    c8 = jnp.tile(ct, (1, MLA_HEADS))
    s8 = jnp.tile(st, (1, MLA_HEADS))
    q = qq[:, :hw] * c8 + qq[:, hw:] * s8
    kr = krr[:, :HEAD_PAD] * ct + krr[:, HEAD_PAD:] * st
    lane = lax.broadcasted_iota(jnp.int32, kv.shape, 1) % HEAD_PAD
    k = jnp.where(lane < QK_NOPE, kv, jnp.tile(kr, (1, MLA_HEADS)))
    return q, k


def f_sg(p, ln_g, ln_b, w, bias):
    z = 0.5 * p * (1.0 + jnp.tanh(0.7978845608028654 * (p + 0.044715 * p * p * p)))
    u, v = z[:, :SG_DIM], z[:, SG_DIM:]
    mu = jnp.mean(v, axis=-1, keepdims=True)
    var = jnp.mean(jnp.square(v - mu), axis=-1, keepdims=True)
    v = (v - mu) * lax.rsqrt(var + 1e-5) * ln_g + ln_b
    lane = lax.broadcasted_iota(jnp.int32, (SG_CHUNK, LANES), 1)
    outs = []
    for c in range(p.shape[0] // SG_CHUNK):
        vc = v[c * SG_CHUNK:(c + 1) * SG_CHUNK]
        cols = []
        for m in range(SG_DIM // LANES):
            blk = vc[:, m * LANES:(m + 1) * LANES]
            cols.append(jnp.where(lane < 64, mm(w[2 * m], blk), mm(w[2 * m + 1], blk)))
        outs.append(jnp.concatenate(cols, axis=1) + bias)
    mixed = outs[0] if len(outs) == 1 else jnp.concatenate(outs, axis=0)
    return (u * mixed,)


def f_rw_pre(z, zp, zn, mu, w0, w2, a0, a2, g2, k_k, k_a, ones):
    z = z + mu * (0.5 * (zp + zn) - z)
    C = RWKV_DIM
    r, k, v = z[:, :C], z[:, C:2 * C], z[:, 2 * C:3 * C]
    wl, al, gl = z[:, 3 * C:3 * C + 128], z[:, 3 * C + 128:3 * C + 256], z[:, 3 * C + 256:]
    w = w0 + mm(jnp.tanh(wl), w2)
    decay = jnp.exp(-0.6065306597126334 * _sigmoid(w))
    a = _sigmoid(a0 + mm(al, a2))
    g = mm(_sigmoid(gl), g2)
    kk = k * k_k
    kk = kk / jnp.maximum(jnp.sqrt(segsum(kk * kk, ones)), 1e-12)
    k2 = jnp.concatenate([k, k], axis=1)
    kdir = k2 * (1.0 + (a - 1.0) * jnp.concatenate([k_a, k_a], axis=1))
    bdir = jnp.concatenate([kk, kk], axis=1) * a
    return r, v, decay, kdir, kk, bdir, g


def f_rw_post(y, r, v, kdir, g, ln_g, ln_b, r_k, ones):
    mean = segsum(y, ones) * (1.0 / RWKV_HEAD)
    yc = y - mean
    var = segsum(yc * yc, ones) * (1.0 / RWKV_HEAD)
    y = yc * lax.rsqrt(var + GN_EPS) * ln_g + ln_b
    C = RWKV_DIM
    bonus = segsum(r * kdir[:, :C] * r_k, ones) + segsum(r * kdir[:, C:] * r_k, ones)
    return ((y + bonus * v) * g,)


def f_merge(pg, b0, b1, b2, gate_b):
    D = b0.shape[1]
    gt = _sigmoid(pg + gate_b)
    return (gt[:, :D] * b0 + gt[:, D:2 * D] * b1 + gt[:, 2 * D:] * b2,)


def f_swiglu(au):
    a, u = au[:, :D_FF], au[:, D_FF:]
    return (a * _sigmoid(a) * u,)


def f_add3(a, b, c):
    return (a + b + c,)


def loss_head(x, tgt, g, tm):
    N, D = x.shape

    def body(x_ref, t_ref, g_ref, loss_ref, dx_ref, dg_ref):
        t = t_ref[...]

        def f(xv, gv):
            err = _rms(xv, gv) - t
            return 0.5 * jnp.sum(jnp.mean(err * err, axis=-1, keepdims=True))

        val, (dx, dg) = jax.value_and_grad(f, argnums=(0, 1))(x_ref[...], g_ref[...])
        dx_ref[...] = dx

        @pl.when(pl.program_id(0) == 0)
        def _():
            loss_ref[...] = jnp.zeros_like(loss_ref)
            dg_ref[...] = jnp.zeros_like(dg_ref)

        loss_ref[...] += jnp.full(loss_ref.shape, val, F32)
        dg_ref[...] += dg

    row = pl.BlockSpec((tm, D), lambda i: (i, 0))
    return pl.pallas_call(
        body, name="loss_head", grid=(N // tm,), in_specs=[row, row, _full_spec(g)],
        out_specs=[pl.BlockSpec((1, LANES), lambda i: (0, 0)), row, _full_spec(g)],
        out_shape=[jax.ShapeDtypeStruct((1, LANES), F32), jax.ShapeDtypeStruct((N, D), F32),
                   jax.ShapeDtypeStruct(g.shape, F32)],
        compiler_params=_params(("arbitrary",)))(x, tgt, g)


ATT_SCALE = float((QK_NOPE + QK_ROPE) ** -0.5)


def _attn_block(q, k, kv):
    s = mm_nt(q, k) * ATT_SCALE
    m = lax.stop_gradient(jnp.max(s, axis=-1, keepdims=True))
    e = jnp.exp(s - m)
    return mm(e, kv) * (1.0 / jnp.sum(e, axis=-1, keepdims=True))


def attention_fwd(name, q, k, kv, B, S, tq, out_dtype):
    nq = S // tq
    qspec = pl.BlockSpec((tq, HEAD_PAD), lambda b, h, i: (b * nq + i, h))
    kspec = pl.BlockSpec((S, HEAD_PAD), lambda b, h, i: (b, h))

    def body(q_ref, k_ref, kv_ref, o_ref):
        o_ref[...] = _attn_block(q_ref[...], k_ref[...], kv_ref[...]).astype(o_ref.dtype)

    return pl.pallas_call(
        body, name=name, grid=(B, MLA_HEADS, nq), in_specs=[qspec, kspec, kspec], out_specs=qspec,
        out_shape=jax.ShapeDtypeStruct(q.shape, out_dtype),
        compiler_params=_params(("parallel", "parallel", "arbitrary")))(q, k, kv)


def attention_bwd(name, q, k, kv, do, B, S, tq):
    nq = S // tq
    qspec = pl.BlockSpec((tq, HEAD_PAD), lambda b, h, i: (b * nq + i, h))
    kspec = pl.BlockSpec((S, HEAD_PAD), lambda b, h, i: (b, h))

    def body(q_ref, k_ref, kv_ref, do_ref, dq_ref, dk_ref, dkv_ref):
        _, vjp = jax.vjp(_attn_block, q_ref[...], k_ref[...], kv_ref[...])
        dq, dk, dkv = vjp(do_ref[...])
        dq_ref[...] = dq

        @pl.when(pl.program_id(2) == 0)
        def _():
            dk_ref[...] = jnp.zeros_like(dk_ref)
            dkv_ref[...] = jnp.zeros_like(dkv_ref)

        dk_ref[...] += dk
        dkv_ref[...] += dkv

    sh = jax.ShapeDtypeStruct(q.shape, F32)
    return pl.pallas_call(
        body, name=name, grid=(B, MLA_HEADS, nq), in_specs=[qspec, kspec, kspec, qspec],
        out_specs=[qspec, kspec, kspec], out_shape=[sh, sh, sh],
        compiler_params=_params(("parallel", "parallel", "arbitrary")))(q, k, kv, do)


SCAN_TC = 8
SCAN_UNROLL = 16


def _jloop(n, body, init):
    def outer(o, c):
        for u in range(SCAN_UNROLL):
            c = body(o * SCAN_UNROLL + u, c)
        return c

    return lax.fori_loop(0, n // SCAN_UNROLL, outer, init)


def _dir_mask(L, Ip):
    per_dir = L // (2 * (RWKV_HEAD // Ip))
    lane = lax.broadcasted_iota(jnp.int32, (1, L), 1)
    return (lane // per_dir) % 2 == 1


def _merge_dirs(mask, fwd_ref, rev_ref, out_ref):
    for tt in range(SCAN_TC):
        out_ref[tt] = jnp.where(mask, rev_ref[SCAN_TC - 1 - tt], fwd_ref[tt])


def scan_fwd(name, w, k, b, kk, r, v):
    T, J, L = w.shape
    Ip = v.shape[1]
    nT = T // SCAN_TC
    fwd3, rev3 = (lambda g: (g, 0, 0)), (lambda g: (nT - 1 - g, 0, 0))
    jf, jr = pl.BlockSpec((SCAN_TC, J, L), fwd3), pl.BlockSpec((SCAN_TC, J, L), rev3)
    i_f, i_r = pl.BlockSpec((SCAN_TC, Ip, L), fwd3), pl.BlockSpec((SCAN_TC, Ip, L), rev3)
    sspec = pl.BlockSpec((SCAN_TC, J, Ip, L), lambda g: (g, 0, 0, 0))
    last_spec = pl.BlockSpec((J, Ip, L), lambda g: (0, 0, 0))

    def body(wf, wr, kf, kr, bf, br, kkf, kkr, rf, rr, vf, vr, yf_ref, yr_ref, sp_ref, sa_ref, last_ref,
             s_ref, w_ref, k_ref, b_ref, kk_ref, r_ref, v_ref):
        @pl.when(pl.program_id(0) == 0)
        def _():
            s_ref[...] = jnp.zeros_like(s_ref)

        mask = _dir_mask(L, Ip)
        for f_, r_, m_ in ((wf, wr, w_ref), (kf, kr, k_ref), (bf, br, b_ref), (kkf, kkr, kk_ref), (rf, rr, r_ref),
                           (vf, vr, v_ref)):
            _merge_dirs(mask, f_, r_, m_)

        def row(ref, tt, j):
            return jnp.broadcast_to(ref[tt, pl.ds(j, 1), :], (Ip, L))

        def step(tt, carry):
            def p1(j, sa):
                s = s_ref[j]
                sp_ref[tt, j] = s
                return sa + s * row(kk_ref, tt, j)

            sa = _jloop(J, p1, jnp.zeros((Ip, L), F32))
            sa_ref[tt] = sa
            vt = v_ref[tt]

            def p2(j, y):
                s = s_ref[j] * row(w_ref, tt, j) - sa * row(b_ref, tt, j) + vt * row(k_ref, tt, j)
                s_ref[j] = s
                return y + s * row(r_ref, tt, j)

            y = _jloop(J, p2, jnp.zeros((Ip, L), F32))
            yf_ref[tt] = y
            yr_ref[SCAN_TC - 1 - tt] = y
            return carry

        lax.fori_loop(0, SCAN_TC, step, 0)

        @pl.when(pl.program_id(0) == nT - 1)
        def _():
            last_ref[...] = s_ref[...]

    ish = jax.ShapeDtypeStruct((T, Ip, L), F32)
    jscr = pltpu.VMEM((SCAN_TC, J, L), F32)
    return pl.pallas_call(
        body, name=name, grid=(nT,), in_specs=[jf, jr] * 5 + [i_f, i_r],
        out_specs=[i_f, i_r, sspec, i_f, last_spec],
        out_shape=[ish, ish, jax.ShapeDtypeStruct((T, J, Ip, L), F32), ish, jax.ShapeDtypeStruct((J, Ip, L), F32)],
        scratch_shapes=[pltpu.VMEM((J, Ip, L), F32)] + [jscr] * 5 + [pltpu.VMEM((SCAN_TC, Ip, L), F32)],
        compiler_params=_params(("arbitrary",)))(w, w, k, k, b, b, kk, kk, r, r, v, v)


def scan_bwd(name, w, k, b, kk, r, v, sp, sa_all, s_last, dy):
    T, J, L = w.shape
    Ip = v.shape[1]
    nT = T // SCAN_TC
    stp3, mir3 = (lambda g: (nT - 1 - g, 0, 0)), (lambda g: (g, 0, 0))
    jf, jr = pl.BlockSpec((SCAN_TC, J, L), stp3), pl.BlockSpec((SCAN_TC, J, L), mir3)
    i_f, i_r = pl.BlockSpec((SCAN_TC, Ip, L), stp3), pl.BlockSpec((SCAN_TC, Ip, L), mir3)
    sspec = pl.BlockSpec((SCAN_TC, J, Ip, L), lambda g: (nT - 1 - g, 0, 0, 0))
    last_spec = pl.BlockSpec((J, Ip, L), lambda g: (0, 0, 0))

    def body(wf, wr, kf, kr, bf, br, kkf, kkr, rf, rr, vf, vr, dyf, dyr, sp_ref, sa_ref, last_ref,
             dwf, dwr, dkf, dkr, dbf, dbr, dkkf, dkkr, drf, drr, dvf, dvr,
             ds_ref, nxt_ref, w_ref, k_ref, b_ref, kk_ref, r_ref, v_ref, dy_ref):
        @pl.when(pl.program_id(0) == 0)
        def _():
            ds_ref[...] = jnp.zeros_like(ds_ref)
            nxt_ref[...] = last_ref[...]

        mask = _dir_mask(L, Ip)
        for f_, r_, m_ in ((wf, wr, w_ref), (kf, kr, k_ref), (bf, br, b_ref), (kkf, kkr, kk_ref), (rf, rr, r_ref),
                           (vf, vr, v_ref), (dyf, dyr, dy_ref)):
            _merge_dirs(mask, f_, r_, m_)

        def row(ref, tt, j):
            return jnp.broadcast_to(ref[tt, pl.ds(j, 1), :], (Ip, L))

        def rsum(x):
            return jnp.sum(x, axis=0, keepdims=True)

        def make_step(first):
            def step(n, carry):
                tt = SCAN_TC - 1 - n
                dyt, vt, sa = dy_ref[tt], v_ref[tt], sa_ref[tt]

                def p1(j, c):
                    dsa, dv = c
                    ds = ds_ref[j] + dyt * row(r_ref, tt, j)
                    ds_ref[j] = ds
                    return dsa - ds * row(b_ref, tt, j), dv + ds * row(k_ref, tt, j)

                z = jnp.zeros((Ip, L), F32)
                dsa, dv = _jloop(J, p1, (z, z))
                dvf[tt] = dv
                dvr[SCAN_TC - 1 - tt] = dv

                def put(f_ref, r_ref_, j, val):
                    f_ref[tt, pl.ds(j, 1), :] = val
                    r_ref_[SCAN_TC - 1 - tt, pl.ds(j, 1), :] = val

                def p2(j, c):
                    ds = ds_ref[j]
                    s0 = sp_ref[tt, j]
                    s1 = nxt_ref[j] if first else sp_ref[tt + 1, j]
                    put(drf, drr, j, rsum(s1 * dyt))
                    put(dkf, dkr, j, rsum(ds * vt))
                    put(dbf, dbr, j, -rsum(ds * sa))
                    put(dwf, dwr, j, rsum(ds * s0))
                    put(dkkf, dkkr, j, rsum(s0 * dsa))
                    ds_ref[j] = ds * row(w_ref, tt, j) + dsa * row(kk_ref, tt, j)
                    return c

                _jloop(J, p2, 0)
                return carry

            return step

        make_step(True)(0, 0)
        lax.fori_loop(1, SCAN_TC, make_step(False), 0)
        nxt_ref[...] = sp_ref[0]

    jsh = jax.ShapeDtypeStruct((T, J, L), F32)
    ish = jax.ShapeDtypeStruct((T, Ip, L), F32)
    jscr = pltpu.VMEM((SCAN_TC, J, L), F32)
    iscr = pltpu.VMEM((SCAN_TC, Ip, L), F32)
    return pl.pallas_call(
        body, name=name, grid=(nT,), in_specs=[jf, jr] * 5 + [i_f, i_r] * 2 + [sspec, i_f, last_spec],
        out_specs=[jf, jr] * 5 + [i_f, i_r], out_shape=[jsh] * 10 + [ish] * 2,
        scratch_shapes=[pltpu.VMEM((J, Ip, L), F32)] * 2 + [jscr] * 5 + [iscr] * 2,
        compiler_params=_params(("arbitrary",)))(w, w, k, k, b, b, kk, kk, r, r, v, v, dy, dy, sp, sa_all, s_last)


LAYOUT_TT = 64
LAYOUT_U = 8


def _lane_group(L, n):
    return lax.broadcasted_iota(jnp.int32, (1, L), 1) // (L // n)


def to_scan(name, x, B, S, nd, Ip):
    isplit = LANES // (2 * B * RWKV_HEADS)
    tt_n = _pick(S, (LAYOUT_TT, 16, 8))
    x5 = x.reshape(B, S, nd, RWKV_HEADS, RWKV_HEAD)
    rows_out = RWKV_HEAD if Ip is None else Ip

    def body(x_ref, o_ref):
        group = _lane_group(LANES, isplit)

        def chunk(c, carry):
            t0 = c * LAYOUT_U
            z = jnp.concatenate([x_ref[b, t0 + u, min(d, nd - 1)] for u in range(LAYOUT_U) for _ in range(isplit)
                                 for d in range(2) for b in range(B)], axis=0)
            mt = z.T
            for u in range(LAYOUT_U):
                m = mt[:, u * LANES:(u + 1) * LANES]
                if Ip is not None:
                    m = sum(jnp.where(group == i2, m[i2 * Ip:(i2 + 1) * Ip], 0.0) for i2 in range(isplit))
                o_ref[t0 + u] = m
            return carry

        lax.fori_loop(0, tt_n // LAYOUT_U, chunk, 0)

    return pl.pallas_call(
        body, name=name, grid=(S // tt_n,),
        in_specs=[pl.BlockSpec((B, tt_n, nd, RWKV_HEADS, RWKV_HEAD), lambda g: (0, g, 0, 0, 0))],
        out_specs=pl.BlockSpec((tt_n, rows_out, LANES), lambda g: (g, 0, 0)),
        out_shape=jax.ShapeDtypeStruct((S, rows_out, LANES), F32), compiler_params=_params(("parallel",)))(x5)


def from_scan(name, f, r, B, S, nd, i_indexed):
    rows_in = f.shape[1]
    isplit = LANES // (2 * B * RWKV_HEADS)
    Ip = rows_in if i_indexed else RWKV_HEAD // isplit
    tt_n = _pick(S, (LAYOUT_TT, 16, 8))
    per_i2 = LANES // isplit

    def body(f_ref, r_ref, o_ref):
        mask = _dir_mask(LANES, Ip)
        group = _lane_group(LANES, isplit)

        def chunk(c, carry):
            t0 = c * LAYOUT_U
            ms = []
            for u in range(LAYOUT_U):
                m = jnp.where(mask, r_ref[t0 + u], f_ref[t0 + u])
                if i_indexed:
                    m = jnp.concatenate([jnp.where(group == i2, m, 0.0) for i2 in range(isplit)], axis=0)
                ms.append(m)
            zt = jnp.concatenate(ms, axis=1).T
            for u in range(LAYOUT_U):
                z = zt[u * LANES:(u + 1) * LANES]
                zf = sum(z[i2 * per_i2:(i2 + 1) * per_i2] for i2 in range(isplit))
                for b in range(B):
                    d0 = zf[b * RWKV_HEADS:(b + 1) * RWKV_HEADS]
                    d1 = zf[(B + b) * RWKV_HEADS:(B + b + 1) * RWKV_HEADS]
                    if nd == 1:
                        o_ref[b, t0 + u, 0] = d0 + d1
                    else:
                        o_ref[b, t0 + u, 0] = d0
                        o_ref[b, t0 + u, 1] = d1
            return carry

        lax.fori_loop(0, tt_n // LAYOUT_U, chunk, 0)

    spec = pl.BlockSpec((tt_n, rows_in, LANES), lambda g: (g, 0, 0))
    out = pl.pallas_call(
        body, name=name, grid=(S // tt_n,), in_specs=[spec, spec],
        out_specs=pl.BlockSpec((B, tt_n, nd, RWKV_HEADS, RWKV_HEAD), lambda g: (0, g, 0, 0, 0)),
        out_shape=jax.ShapeDtypeStruct((B, S, nd, RWKV_HEADS, RWKV_HEAD), F32),
        compiler_params=_params(("parallel",)))(f, r)
    return out.reshape(B * S, nd * RWKV_DIM)


def adamw(name, w, g, m, v):
    R, C = w.shape
    tr = _pick(R, (256, 128, 64, 32, 16, 8))
    c1 = 1.0 - ADAM_B1 ** ADAM_STEP
    c2 = 1.0 - ADAM_B2 ** ADAM_STEP

    def body(w_ref, g_ref, m_ref, v_ref, d_ref, nm_ref, nv_ref):
        gv = g_ref[...]
        nm = ADAM_B1 * m_ref[...] + (1.0 - ADAM_B1) * gv
        nv = ADAM_B2 * v_ref[...] + (1.0 - ADAM_B2) * jnp.square(gv)
        d_ref[...] = -ADAM_LR * ((nm / c1) / (jnp.sqrt(nv / c2) + ADAM_EPS) + ADAM_WD * w_ref[...])
        nm_ref[...] = nm
        nv_ref[...] = nv

    spec = pl.BlockSpec((tr, C), lambda i: (i, 0))
    sh = jax.ShapeDtypeStruct((R, C), F32)
    return pl.pallas_call(body, name=name, grid=(R // tr,), in_specs=[spec] * 4, out_specs=[spec] * 3,
                          out_shape=[sh] * 3, compiler_params=_params(("parallel",)))(w, g, m, v)


def sum_slots(name, x):
    n, R, C = x.shape
    tr = _pick(R, (256, 128, 64, 32, 16, 8))

    def body(x_ref, o_ref):
        acc = x_ref[0].astype(F32)
        for s in range(1, n):
            acc = acc + x_ref[s].astype(F32)
        o_ref[...] = acc

    return pl.pallas_call(
        body, name=name, grid=(R // tr,), in_specs=[pl.BlockSpec((n, tr, C), lambda i: (0, i, 0))],
        out_specs=pl.BlockSpec((tr, C), lambda i: (i, 0)), out_shape=jax.ShapeDtypeStruct((R, C), F32),
        compiler_params=_params(("parallel",)))(x)


ANY = pl.BlockSpec(memory_space=pl.ANY)


def _xyc():
    return lax.axis_index("x"), lax.axis_index("y"), lax.axis_index("c")


def gather_shards(shard):
    _, R, C = shard.shape

    def body(x_ref, out_ref, send_sems, recv_sems, local_sem):
        x, y, c = _xyc()
        me, sibling = (x, y, c), (x, y, 1 - c)
        chips = [(1 - x, y), (x, 1 - y), (1 - x, 1 - y)]

        def cp(k, cx, cy, half, to, src=None):
            dst = out_ref.at[2 * cx + cy, half]
            return pltpu.make_async_remote_copy(
                src_ref=dst if src is None else src, dst_ref=dst, send_sem=send_sems.at[k],
                recv_sem=recv_sems.at[k], device_id=to, device_id_type=MESH)

        mine = pltpu.make_async_copy(x_ref, out_ref.at[2 * x + y], local_sem)
        mine.start()
        first = [cp(j, x, y, c, (*chip, c), src=x_ref.at[c]) for j, chip in enumerate(chips)]
        for f in first:
            f.start()
        passed = [cp(3 + j, *chip, c, sibling) for j, chip in enumerate(chips)]
        for j, chip in enumerate(chips):
            cp(j, *chip, c, me).wait_recv()
            passed[j].start()
        for j, chip in enumerate(chips):
            cp(3 + j, *chip, 1 - c, me).wait_recv()
        for f in first + passed:
            f.wait_send()
        mine.wait()

    return pl.pallas_call(
        body, name="gather_shards", in_specs=[ANY], out_specs=ANY,
        out_shape=jax.ShapeDtypeStruct((4, 2, R, C), shard.dtype),
        scratch_shapes=[pltpu.SemaphoreType.DMA((6,)), pltpu.SemaphoreType.DMA((6,)), pltpu.SemaphoreType.DMA])(shard)


FLIPS = [(0, 0, 1), (0, 1, 0), (0, 1, 1), (1, 0, 0), (1, 0, 1), (1, 1, 0), (1, 1, 1)]


def scatter_partials(g):
    _, _, R, C = g.shape

    def body(g_ref, out_ref, send_sems, recv_sems, local_sem):
        x, y, c = _xyc()
        me_idx = 4 * x + 2 * y + c
        mine = pltpu.make_async_copy(g_ref.at[2 * x + y, c], out_ref.at[me_idx], local_sem)
        mine.start()
        sends = []
        for k, (fx, fy, fc) in enumerate(FLIPS):
            px, py, pc = (x + fx) % 2, (y + fy) % 2, (c + fc) % 2
            s = pltpu.make_async_remote_copy(
                src_ref=g_ref.at[2 * px + py, pc], dst_ref=out_ref.at[me_idx], send_sem=send_sems.at[k],
                recv_sem=recv_sems.at[k], device_id=(px, py, pc), device_id_type=MESH)
            s.start()
            sends.append(s)
        for k, (fx, fy, fc) in enumerate(FLIPS):
            px, py, pc = (x + fx) % 2, (y + fy) % 2, (c + fc) % 2
            slot = out_ref.at[4 * px + 2 * py + pc]
            pltpu.make_async_remote_copy(
                src_ref=slot, dst_ref=slot, send_sem=send_sems.at[k], recv_sem=recv_sems.at[k],
                device_id=(px, py, pc), device_id_type=MESH).wait_recv()
        for s in sends:
            s.wait_send()
        mine.wait()

    return pl.pallas_call(
        body, name="scatter_partials", in_specs=[ANY], out_specs=ANY,
        out_shape=jax.ShapeDtypeStruct((8, R, C), g.dtype),
        scratch_shapes=[pltpu.SemaphoreType.DMA((7,)), pltpu.SemaphoreType.DMA((7,)), pltpu.SemaphoreType.DMA])(g)


JOIN_ROWS = 256


def sum_join(x):
    n, R, C = x.shape
    steps = R // JOIN_ROWS

    def body(x_ref, out_ref, acc, local_sems, send_sems, recv_sem):
        i = pl.program_id(0)
        slot = lax.rem(i, 2)
        xx, yy, c = _xyc()
        sibling = (xx, yy, 1 - c)

        def copies(k, s):
            dst = out_ref.at[c, pl.ds(k * JOIN_ROWS, JOIN_ROWS)]
            return (pltpu.make_async_copy(acc.at[s], dst, local_sems.at[s]),
                    pltpu.make_async_remote_copy(src_ref=acc.at[s], dst_ref=dst, send_sem=send_sems.at[s],
                                                 recv_sem=recv_sem, device_id=sibling, device_id_type=MESH))

        def drain(k, s):
            loc, rem = copies(k, s)
            loc.wait()
            rem.wait_send()

        @pl.when(i >= 2)
        def _():
            drain(i - 2, slot)

        a = x_ref[0].astype(F32)
        for s in range(1, n):
            a = a + x_ref[s].astype(F32)
        acc[slot] = a
        loc, rem = copies(i, slot)
        loc.start()
        rem.start()

        @pl.when(i == steps - 1)
        def _():
            if steps >= 2:
                drain(i - 1, 1 - slot)
            drain(i, slot)
            theirs = out_ref.at[1 - c]
            pltpu.make_async_remote_copy(src_ref=theirs, dst_ref=theirs, send_sem=send_sems.at[0], recv_sem=recv_sem,
                                         device_id=sibling, device_id_type=MESH).wait_recv()

    return pl.pallas_call(
        body, name="sum_join", grid=(steps,),
        in_specs=[pl.BlockSpec((n, JOIN_ROWS, C), lambda i: (0, i, 0))], out_specs=ANY,
        out_shape=jax.ShapeDtypeStruct((2, R, C), F32),
        scratch_shapes=[pltpu.VMEM((2, JOIN_ROWS, C), F32), pltpu.SemaphoreType.DMA((2,)),
                        pltpu.SemaphoreType.DMA((2,)), pltpu.SemaphoreType.DMA],
        compiler_params=_params(("arbitrary",)))(x)


def gather_all(name, block):
    R, C = block.shape

    def body(x_ref, out_ref, send_sems, recv_sems, local_sem):
        x, y, c = _xyc()
        mine = pltpu.make_async_copy(x_ref, out_ref.at[4 * x + 2 * y + c], local_sem)
        mine.start()
        sends = []
        for k, (fx, fy, fc) in enumerate(FLIPS):
            px, py, pc = (x + fx) % 2, (y + fy) % 2, (c + fc) % 2
            s = pltpu.make_async_remote_copy(
                src_ref=x_ref, dst_ref=out_ref.at[4 * x + 2 * y + c], send_sem=send_sems.at[k],
                recv_sem=recv_sems.at[k], device_id=(px, py, pc), device_id_type=MESH)
            s.start()
            sends.append(s)
        for k, (fx, fy, fc) in enumerate(FLIPS):
            px, py, pc = (x + fx) % 2, (y + fy) % 2, (c + fc) % 2
            slot = out_ref.at[4 * px + 2 * py + pc]
            pltpu.make_async_remote_copy(
                src_ref=slot, dst_ref=slot, send_sem=send_sems.at[k], recv_sem=recv_sems.at[k],
                device_id=(px, py, pc), device_id_type=MESH).wait_recv()
        for s in sends:
            s.wait_send()
        mine.wait()

    return pl.pallas_call(
        body, name=name, in_specs=[ANY], out_specs=ANY,
        out_shape=jax.ShapeDtypeStruct((8, R, C), block.dtype),
        scratch_shapes=[pltpu.SemaphoreType.DMA((7,)), pltpu.SemaphoreType.DMA((7,)), pltpu.SemaphoreType.DMA])(block)


PACK_C = 1024


def _pack(arrs, row_mult):
    flat = jnp.concatenate([a.reshape(-1) for a in arrs])
    n = flat.shape[0]
    rows = -(-n // PACK_C)
    rows = -(-rows // row_mult) * row_mult
    return jnp.pad(flat, (0, rows * PACK_C - n)).reshape(rows, PACK_C)


def _unpack(buf, shapes):
    flat = buf.reshape(-1)
    out, off = [], 0
    for s in shapes:
        n = int(np.prod(s))
        out.append(flat[off:off + n].reshape(s))
        off += n
    return out


OFF_Q, OFF_CKV, OFF_KR, OFF_SG, OFF_RW, OFF_GATE, N_IN_PAD = 0, 384, 640, 896, 1920, 3840, 6912
IN_SEGMENTS = [('q', OFF_Q, OFF_CKV), ('ckv', OFF_CKV, OFF_KR), ('kr', OFF_KR, OFF_SG), ('sg', OFF_SG, OFF_RW),
               ('rw', OFF_RW, OFF_GATE), ('gate', OFF_GATE, N_IN_PAD)]
ROPE_LANE = QK_NOPE
HALF = QK_ROPE // 2


def _win_layout():
    src = np.full((N_IN_PAD,), -1, np.int64)
    sgn = np.ones((N_IN_PAD,), np.float32)
    src[0:640] = np.arange(0, 640)
    kr0 = Q_LORA + KV_LORA
    src[OFF_KR + ROPE_LANE:OFF_KR + ROPE_LANE + QK_ROPE] = kr0 + np.arange(QK_ROPE)
    sw = OFF_KR + HEAD_PAD + ROPE_LANE
    src[sw:sw + HALF] = kr0 + HALF + np.arange(HALF)
    sgn[sw:sw + HALF] = -1.0
    src[sw + HALF:sw + QK_ROPE] = kr0 + np.arange(HALF)
    src[OFF_SG:N_IN_PAD] = 672 + np.arange(N_IN_PAD - OFF_SG)
    return src, sgn


def _wuq_layout():
    hw = MLA_HEADS * HEAD_PAD
    src = np.full((2 * hw,), -1, np.int64)
    sgn = np.ones((2 * hw,), np.float32)
    per = QK_NOPE + QK_ROPE
    for h in range(MLA_HEADS):
        src[h * HEAD_PAD:h * HEAD_PAD + per] = h * per + np.arange(per)
        sw = hw + h * HEAD_PAD + ROPE_LANE
        src[sw:sw + HALF] = h * per + QK_NOPE + HALF + np.arange(HALF)
        sgn[sw:sw + HALF] = -1.0
        src[sw + HALF:sw + QK_ROPE] = h * per + QK_NOPE + np.arange(HALF)
    return src, sgn


def _runs(idx, sgn):
    out, lo = [], 0
    for pos in range(1, len(idx) + 1):
        if pos == len(idx) or not (
                (idx[pos] == -1 and idx[pos - 1] == -1)
                or (idx[pos - 1] >= 0 and idx[pos] == idx[pos - 1] + 1 and sgn[pos] == sgn[pos - 1])):
            out.append((lo, pos, int(idx[lo]), float(sgn[lo])))
            lo = pos
    return out


def _select_cols(w, idx, sgn):
    pieces = []
    for lo, hi, s0, sg in _runs(idx, sgn):
        if s0 < 0:
            pieces.append(jnp.zeros((w.shape[0], hi - lo), w.dtype))
        else:
            piece = w[:, s0:s0 + hi - lo]
            pieces.append(piece if sg > 0 else -piece)
    return jnp.concatenate(pieces, axis=1)


def _permute_cols(w, src, sgn):
    return _select_cols(w, src, sgn)


def _unpermute_full(dw, src, sgn, n_cols):
    first = np.full((n_cols,), -1, np.int64)
    second = np.full((n_cols,), -1, np.int64)
    for pos, s in enumerate(src):
        if s < 0:
            continue
        if first[s] < 0:
            first[s] = pos
        else:
            second[s] = pos
    sg2 = np.where(second >= 0, sgn[np.maximum(second, 0)], 1.0)
    return _select_cols(dw, first, sgn[first]), _select_cols(dw, second, sg2)


def _blockdiag(w):
    z = jnp.zeros_like(w[0])
    return jnp.concatenate([jnp.concatenate([w[0], z], axis=1), jnp.concatenate([z, w[1]], axis=1)], axis=0)


def _rope_tables(pos):
    inv = 1.0 / (ROPE_THETA ** (jnp.arange(0, QK_ROPE, 2, dtype=F32) / QK_ROPE))
    ang = pos.astype(F32)[:, None] * inv[None, :]
    cos, sin = jnp.cos(ang), jnp.sin(ang)
    pad = lambda t, fill: jnp.concatenate(
        [jnp.full((t.shape[0], ROPE_LANE), fill, F32), t, t, jnp.full((t.shape[0], HEAD_PAD - ROPE_LANE - QK_ROPE), fill, F32)], axis=1)
    return pad(cos, 1.0), pad(sin, 0.0)


def kernel(x, positions, attn_norm_g, w_in, gate_b, q_norm_g, w_uq, kv_norm_g, w_ukv, sg_ln_g, sg_ln_b, sg_w, sg_b, rw_mu, rw_w0, rw_w2, rw_a0, rw_a2, rw_g2, rw_k_k, rw_k_a, rw_r_k, rw_ln_g, rw_ln_b, w_branch, w_out, ffn_norm_g, w_ffn_gate, w_ffn_up, w_ffn_down, final_norm_g, loss_target, m_attn_norm_g, m_w_in, m_gate_b, m_q_norm_g, m_w_uq, m_kv_norm_g, m_w_ukv, m_sg_ln_g, m_sg_ln_b, m_sg_w, m_sg_b, m_rw_mu, m_rw_w0, m_rw_w2, m_rw_a0, m_rw_a2, m_rw_g2, m_rw_k_k, m_rw_k_a, m_rw_r_k, m_rw_ln_g, m_rw_ln_b, m_w_branch, m_w_out, m_ffn_norm_g, m_w_ffn_gate, m_w_ffn_up, m_w_ffn_down, m_final_norm_g, v_attn_norm_g, v_w_in, v_gate_b, v_q_norm_g, v_w_uq, v_kv_norm_g, v_w_ukv, v_sg_ln_g, v_sg_ln_b, v_sg_w, v_sg_b, v_rw_mu, v_rw_w0, v_rw_w2, v_rw_a0, v_rw_a2, v_rw_g2, v_rw_k_k, v_rw_k_a, v_rw_r_k, v_rw_ln_g, v_rw_ln_b, v_w_branch, v_w_out, v_ffn_norm_g, v_w_ffn_gate, v_w_ffn_up, v_w_ffn_down, v_final_norm_g):
    args = locals()
    W = {n: args[n] for n in WEIGHTS}
    M1 = {n: args['m_' + n] for n in WEIGHTS}
    M2 = {n: args['v_' + n] for n in WEIGHTS}
    B, S, D = x.shape
    N = B * S
    TM = _pick(N, (256, 128))
    TMH = 128
    TQ = _pick(S, (512, 256, 128))

    shard_shapes = [W[n].shape for n in SHARDED]
    full = {}
    mm_pack = _pack([W[n].astype(BF16) for n in MATMUL_SHARDED], 32)
    Rm = mm_pack.shape[0]
    gathered = gather_shards(mm_pack.reshape(2, Rm // 2, PACK_C)).reshape(4, Rm, PACK_C)
    pieces = [_unpack(gathered[q], [W[n].shape for n in MATMUL_SHARDED]) for q in range(4)]
    for i, n in enumerate(MATMUL_SHARDED):
        full[n] = jnp.concatenate([pieces[q][i] for q in range(4)], axis=SHARD_AXIS[n])
    small = gather_all("gather_small", _pack([W[n] for n in SMALL_SHARDED], 8))
    pieces = [_unpack(small[2 * q], [W[n].shape for n in SMALL_SHARDED]) for q in range(4)]
    for i, n in enumerate(SMALL_SHARDED):
        full[n] = jnp.concatenate([pieces[q][i] for q in range(4)], axis=SHARD_AXIS[n])
    for n in ('rw_w2', 'rw_a2', 'rw_g2'):
        full[n] = full[n].astype(F32)
    for n in REPLICATED:
        full[n] = W[n]

    win_src, win_sgn = _win_layout()
    wuq_src, wuq_sgn = _wuq_layout()
    ones = jnp.asarray(np.kron(np.eye(RWKV_HEADS), np.ones((RWKV_HEAD, RWKV_HEAD))), BF16)
    ct, st = _rope_tables(positions.reshape(N))
    row = lambda v: v.reshape(1, -1)

    H, HD = RWKV_HEADS, RWKV_HEAD
    inst = 2 * B * H
    isplit = LANES // inst
    Ip = HD // isplit
    to_j = lambda nm, c: to_scan(nm, c, B, S, c.shape[1] // RWKV_DIM, None)
    to_i = lambda nm, c: to_scan(nm, c, B, S, 1, Ip)

    def shift_prev(z):
        z = z.reshape(B, S, -1)
        return jnp.pad(z[:, :-1], ((0, 0), (1, 0), (0, 0))).reshape(N, -1)

    def shift_next(z):
        z = z.reshape(B, S, -1)
        return jnp.pad(z[:, 1:], ((0, 0), (0, 1), (0, 0))).reshape(N, -1)

    LW = []
    for l in range(DEPTH):
        wb = full['w_branch'][l]
        wb0 = jnp.zeros((MLA_HEADS, HEAD_PAD, D), F32).at[:, QK_NOPE:].set(wb[0].reshape(MLA_HEADS, V_HEAD, D))
        LW.append(dict(
            attn_g=row(full['attn_norm_g'][l]),
            w_in=_permute_cols(full['w_in'][l], win_src, win_sgn),
            gate_b=row(full['gate_b'][l]),
            q_g=row(full['q_norm_g'][l]),
            w_uq=_permute_cols(full['w_uq'][l], wuq_src, wuq_sgn),
            kv_g=row(full['kv_norm_g'][l]),
            w_ukv=full['w_ukv'][l],
            sg_g=row(full['sg_ln_g'][l]), sg_b=row(full['sg_ln_b'][l]), sg_w=full['sg_w'][l],
            sg_bias=jnp.repeat(full['sg_b'][l].T, SG_DIM // SG_GROUPS, axis=1),
            mu=row(full['rw_mu'][l]), w0=row(full['rw_w0'][l]), w2=_blockdiag(full['rw_w2'][l]),
            a0=row(full['rw_a0'][l]), a2=_blockdiag(full['rw_a2'][l]), g2=full['rw_g2'][l],
            k_k=row(full['rw_k_k'][l]), k_a=row(full['rw_k_a'][l]), r_k=row(full['rw_r_k'][l]),
            ln_g=row(full['rw_ln_g'][l]), ln_b=row(full['rw_ln_b'][l]),
            wb0=wb0.reshape(MLA_HEADS * HEAD_PAD, D), wb1=wb[1], wb2=wb[2],
            w_out=full['w_out'][l], ffn_g=row(full['ffn_norm_g'][l]),
            w_gu=jnp.concatenate([full['w_ffn_gate'][l], full['w_ffn_up'][l]], axis=1),
            w_down=full['w_ffn_down'][l]))

    saved = []
    xc = x.reshape(N, D)
    for l in range(DEPTH):
        p = LW[l]
        t = 'l%d_' % l
        sv = dict(x=xc)
        (h,) = rowwise(t + 'attn_norm', f_rms, [xc], [p['attn_g']], [], [D], TM, [BF16])
        p_q, p_ckv, p_kr, p_sg, z, p_gate = [
            matmul(t + 'in_proj_' + sn, h, p['w_in'][:, a:b], 'nn') for sn, a, b in IN_SEGMENTS]
        sv['h'] = h
        (cq,) = rowwise(t + 'q_norm', f_rms, [p_q], [p['q_g']], [], [Q_LORA], TM, [BF16])
        (ckv,) = rowwise(t + 'kv_norm', f_rms, [p_ckv], [p['kv_g']], [], [KV_LORA], TM, [BF16])
        qq = matmul(t + 'uq', cq, p['w_uq'], 'nn')
        kv = matmul(t + 'ukv', ckv, p['w_ukv'], 'nn')
        qh, kh = rowwise(t + 'rope', f_rope, [qq, kv, p_kr, ct, st], [], [], [MLA_HEADS * HEAD_PAD] * 2, TM)
        ya = attention_fwd(t + 'attn', qh, kh, kv, B, S, TQ, BF16)
        sv.update(p_q=p_q, p_ckv=p_ckv, p_kr=p_kr, cq=cq, ckv=ckv, qq=qq, kv=kv, qh=qh, kh=kh, ya=ya)
        (yb,) = rowwise(t + 'sg', f_sg, [p_sg], [p['sg_g'], p['sg_b'], p['sg_w'], p['sg_bias']], [], [SG_DIM], TM,
                        [BF16])
        sv.update(p_sg=p_sg, yb=yb)
        zp, zn = shift_prev(z), shift_next(z)
        rw_par = [p['mu'], p['w0'], p['w2'], p['a0'], p['a2'], p['g2'], p['k_k'], p['k_a']]
        r_, v_, decay, kdir, kk, bdir, g_ = rowwise(
            t + 'rw_pre', f_rw_pre, [z, zp, zn], rw_par, [ones],
            [RWKV_DIM, RWKV_DIM, 2 * RWKV_DIM, 2 * RWKV_DIM, RWKV_DIM, 2 * RWKV_DIM, RWKV_DIM], TM)
        sc = dict(w=to_j(t + 'lay_w', decay), k=to_j(t + 'lay_k', kdir), b=to_j(t + 'lay_b', bdir),
                  kk=to_j(t + 'lay_kk', kk), r=to_j(t + 'lay_r', r_), v=to_i(t + 'lay_v', v_))
        y_f, y_r, sp, sa_all, s_last = scan_fwd(t + 'scan', sc['w'], sc['k'], sc['b'], sc['kk'], sc['r'], sc['v'])
        ysum = from_scan(t + 'lay_y', y_f, y_r, B, S, 1, True)
        (yc,) = rowwise(t + 'rw_post', f_rw_post, [ysum, r_, v_, kdir, g_], [p['ln_g'], p['ln_b'], p['r_k']],
                        [ones], [RWKV_DIM], TM, [BF16])
        sv.update(z=z, zp=zp, zn=zn, r=r_, v=v_, kdir=kdir, g=g_, sc=sc, sp=sp, sa=sa_all, s_last=s_last,
                  ysum=ysum, yc=yc)
        b0 = matmul(t + 'br0', ya, p['wb0'], 'nn')
        b1 = matmul(t + 'br1', yb, p['wb1'], 'nn')
        b2 = matmul(t + 'br2', yc, p['wb2'], 'nn')
        (merged,) = rowwise(t + 'merge', f_merge, [p_gate, b0, b1, b2], [p['gate_b']], [], [D], TM, [BF16])
        x2 = matmul(t + 'out_proj', merged, p['w_out'], 'nn', add=xc)
        sv.update(p_gate=p_gate, b0=b0, b1=b1, b2=b2, merged=merged, x2=x2)
        (h2,) = rowwise(t + 'ffn_norm', f_rms, [x2], [p['ffn_g']], [], [D], TM, [BF16])
        au = matmul(t + 'ffn_in', h2, p['w_gu'], 'nn')
        (act,) = rowwise(t + 'swiglu', f_swiglu, [au], [], [], [D_FF], TM, [BF16])
        xc = matmul(t + 'ffn_out', act, p['w_down'], 'nn', add=x2)
        sv.update(h2=h2, au=au, act=act)
        saved.append(sv)

    loss_part, dx, d_final_g = loss_head(xc, loss_target.reshape(N, D), row(full['final_norm_g']), TM)
    loss = lax.psum(loss_part[0, 0], ("x", "y", "c"))

    G = {n: [None] * DEPTH for n in WEIGHTS if n != 'final_norm_g'}
    for l in reversed(range(DEPTH)):
        p, sv = LW[l], saved[l]
        t = 'l%d_bwd_' % l
        d_act = matmul(t + 'ffn_out_dx', dx, p['w_down'], 'nt')
        G['w_ffn_down'][l] = matmul(t + 'ffn_out_dw', sv['act'], dx, 'tn')
        (d_au,), _ = rowwise_bwd(t + 'swiglu', f_swiglu, [sv['au']], [], [], [[d_act]], TMH, drow_dtypes=[BF16])
        d_h2 = matmul(t + 'ffn_in_dx', d_au, p['w_gu'], 'nt')
        d_wgu = matmul(t + 'ffn_in_dw', sv['h2'], d_au, 'tn')
        G['w_ffn_gate'][l], G['w_ffn_up'][l] = d_wgu[:, :D_FF], d_wgu[:, D_FF:]
        (dx2,), (dg,) = rowwise_bwd(t + 'ffn_norm', f_rms, [sv['x2']], [p['ffn_g']], [], [[d_h2]], TM, extra=[(0, dx)])
        G['ffn_norm_g'][l] = dg.reshape(-1)
        d_merged = matmul(t + 'out_proj_dx', dx2, p['w_out'], 'nt')
        G['w_out'][l] = matmul(t + 'out_proj_dw', sv['merged'], dx2, 'tn')
        (d_pgate, d_b0, d_b1, d_b2), (d_gate_b,) = rowwise_bwd(
            t + 'merge', f_merge, [sv['p_gate'], sv['b0'], sv['b1'], sv['b2']], [p['gate_b']], [], [[d_merged]], TM,
            drow_dtypes=[BF16] * 4)
        G['gate_b'][l] = d_gate_b.reshape(3, D)
        d_ya = matmul(t + 'br0_dx', d_b0, p['wb0'], 'nt')
        d_yb = matmul(t + 'br1_dx', d_b1, p['wb1'], 'nt')
        d_yc = matmul(t + 'br2_dx', d_b2, p['wb2'], 'nt')
        d_wb0 = matmul(t + 'br0_dw', sv['ya'], d_b0, 'tn').reshape(MLA_HEADS, HEAD_PAD, D)[:, QK_NOPE:].reshape(-1, D)
        G['w_branch'][l] = jnp.stack([d_wb0, matmul(t + 'br1_dw', sv['yb'], d_b1, 'tn'),
                                      matmul(t + 'br2_dw', sv['yc'], d_b2, 'tn')])
        (d_y, d_r1, d_v1, d_kdir1, d_g), (d_ln_g, d_ln_b, d_r_k) = rowwise_bwd(
            t + 'rw_post', f_rw_post, [sv['ysum'], sv['r'], sv['v'], sv['kdir'], sv['g']],
            [p['ln_g'], p['ln_b'], p['r_k']], [ones], [[d_yc]], TMH)
        G['rw_ln_g'][l], G['rw_ln_b'][l] = d_ln_g.reshape(-1), d_ln_b.reshape(-1)
        G['rw_r_k'][l] = d_r_k.reshape(RWKV_HEADS, RWKV_HEAD)
        sc = sv['sc']
        res = scan_bwd(t + 'scan', sc['w'], sc['k'], sc['b'], sc['kk'], sc['r'], sc['v'], sv['sp'], sv['sa'],
                       sv['s_last'], to_i(t + 'lay_dy', d_y))
        s_dw, s_dk, s_db, s_dkk, s_dr, s_dv = [
            from_scan(t + 'lay_' + nm, res[2 * i], res[2 * i + 1], B, S, nd, nm == 'dv')
            for i, (nm, nd) in enumerate((('dw', 2), ('dk', 2), ('db', 2), ('dkk', 1), ('dr', 1), ('dv', 1)))]
        rw_par = [p['mu'], p['w0'], p['w2'], p['a0'], p['a2'], p['g2'], p['k_k'], p['k_a']]
        d_outs = [[d_r1, s_dr], [d_v1, s_dv], [s_dw], [d_kdir1, s_dk], [s_dkk], [s_db], [d_g]]
        (d_z, d_zp, d_zn), d_rw = rowwise_bwd(
            t + 'rw_pre', f_rw_pre, [sv['z'], sv['zp'], sv['zn']], rw_par, [ones], d_outs, TMH)
        (d_prw,) = rowwise(t + 'shift_sum', f_add3, [d_z, shift_next(d_zp), shift_prev(d_zn)], [], [], [RWKV_IN], TM,
                           [BF16])
        G['rw_mu'][l] = d_rw[0].reshape(-1)
        G['rw_w0'][l] = d_rw[1].reshape(2, RWKV_DIM)
        G['rw_w2'][l] = jnp.stack([d_rw[2][:64, :RWKV_DIM], d_rw[2][64:, RWKV_DIM:]])
        G['rw_a0'][l] = d_rw[3].reshape(2, RWKV_DIM)
        G['rw_a2'][l] = jnp.stack([d_rw[4][:64, :RWKV_DIM], d_rw[4][64:, RWKV_DIM:]])
        G['rw_g2'][l] = d_rw[5]
        G['rw_k_k'][l], G['rw_k_a'][l] = d_rw[6].reshape(-1), d_rw[7].reshape(-1)
        (d_psg,), (d_sg_g, d_sg_b, d_sg_w, d_sg_bias) = rowwise_bwd(
            t + 'sg', f_sg, [sv['p_sg']], [p['sg_g'], p['sg_b'], p['sg_w'], p['sg_bias']], [], [[d_yb]], TMH,
            drow_dtypes=[BF16])
        G['sg_ln_g'][l], G['sg_ln_b'][l], G['sg_w'][l] = d_sg_g.reshape(-1), d_sg_b.reshape(-1), d_sg_w
        G['sg_b'][l] = d_sg_bias.reshape(SG_CHUNK, SG_GROUPS, SG_DIM // SG_GROUPS).sum(-1).T
        d_qh, d_kh, d_kvv = attention_bwd(t + 'attn', sv['qh'], sv['kh'], sv['kv'], d_ya, B, S, TQ)
        (d_qq, d_kv, d_pkr), _ = rowwise_bwd(
            t + 'rope', f_rope, [sv['qq'], sv['kv'], sv['p_kr'], ct, st], [], [], [[d_qh], [d_kh]], TM,
            n_row_diff=3, extra=[(1, d_kvv)], drow_dtypes=[BF16] * 3)
        d_cq = matmul(t + 'uq_dx', d_qq, p['w_uq'], 'nt')
        d_wuq = matmul(t + 'uq_dw', sv['cq'], d_qq, 'tn')
        g1, g2_ = _unpermute_full(d_wuq, wuq_src, wuq_sgn, MLA_HEADS * (QK_NOPE + QK_ROPE))
        G['w_uq'][l] = g1 + g2_
        d_ckv = matmul(t + 'ukv_dx', d_kv, p['w_ukv'], 'nt')
        G['w_ukv'][l] = matmul(t + 'ukv_dw', sv['ckv'], d_kv, 'tn')
        (d_pq,), (dg,) = rowwise_bwd(t + 'q_norm', f_rms, [sv['p_q']], [p['q_g']], [], [[d_cq]], TM,
                                     drow_dtypes=[BF16])
        G['q_norm_g'][l] = dg.reshape(-1)
        (d_pckv,), (dg,) = rowwise_bwd(t + 'kv_norm', f_rms, [sv['p_ckv']], [p['kv_g']], [], [[d_ckv]], TM,
                                       drow_dtypes=[BF16])
        G['kv_norm_g'][l] = dg.reshape(-1)
        d_h, d_cols = None, []
        for (sn, a, b), d_seg in zip(IN_SEGMENTS, [d_pq, d_pckv, d_pkr, d_psg, d_prw, d_pgate]):
            d_h = matmul(t + 'in_proj_dx_' + sn, d_seg, p['w_in'][:, a:b], 'nt', add=d_h)
            d_cols.append(matmul(t + 'in_proj_dw_' + sn, sv['h'], d_seg, 'tn'))
        d_win = jnp.concatenate(d_cols, axis=1)
        g1, g2_ = _unpermute_full(d_win, win_src, win_sgn, N_IN)
        kr0 = Q_LORA + KV_LORA
        G['w_in'][l] = g1.at[:, kr0:kr0 + QK_ROPE].add(g2_[:, kr0:kr0 + QK_ROPE])
        (dx,), (dg,) = rowwise_bwd(t + 'attn_norm', f_rms, [sv['x']], [p['attn_g']], [], [[d_h]], TM, extra=[(0, dx2)])
        G['attn_norm_g'][l] = dg.reshape(-1)

    grads = {n: jnp.stack(G[n]) for n in G}
    grads['final_norm_g'] = d_final_g.reshape(-1)
    grad_x = dx.reshape(B, S, D)

    per_shard = []
    for q in range(4):
        sl = []
        for n in SHARDED:
            ax = SHARD_AXIS[n]
            w = W[n].shape[ax]
            sl.append(lax.slice_in_dim(grads[n], q * w, (q + 1) * w, axis=ax).astype(BF16))
        per_shard.append(_pack(sl, 2 * JOIN_ROWS))
    R = per_shard[0].shape[0]
    gpack = jnp.stack(per_shard).astype(BF16).reshape(4, 2, R // 2, PACK_C)
    both = sum_join(scatter_partials(gpack))
    g_shard = dict(zip(SHARDED, _unpack(both.reshape(R, PACK_C), shard_shapes)))
    rep_shapes = [W[n].shape for n in REPLICATED]
    rpack = _pack([grads[n] for n in REPLICATED], 8)
    g_rep = sum_slots("sum_replicated", gather_all("gather_replicated", rpack))

    outs = {}
    for n in MATMUL_SHARDED:
        shp = W[n].shape
        two = lambda a: a.reshape(-1, shp[-1])
        res = adamw("adamw_" + n, two(W[n]), two(g_shard[n]), two(M1[n]), two(M2[n]))
        outs['grad', n] = g_shard[n]
        for key, a in zip(('delta', 'new_m', 'new_v'), res):
            outs[key, n] = a.reshape(shp)
    small = SMALL_SHARDED + REPLICATED
    small_shapes = [W[n].shape for n in small]
    g_small = [g_shard[n] for n in SMALL_SHARDED] + _unpack(g_rep, rep_shapes)
    res = adamw("adamw_small", _pack([W[n] for n in small], 8), _pack(g_small, 8),
                _pack([M1[n] for n in small], 8), _pack([M2[n] for n in small], 8))
    for n, a in zip(small, g_small):
        outs['grad', n] = a
    for key, buf in zip(('delta', 'new_m', 'new_v'), res):
        for n, a in zip(small, _unpack(buf, small_shapes)):
            outs[key, n] = a
    return (loss, grad_x, *[outs['grad', n] for n in WEIGHTS], *[outs['delta', n] for n in WEIGHTS],
            *[outs['new_m', n] for n in WEIGHTS], *[outs['new_v', n] for n in WEIGHTS])
```

```python
import functools
import math

import numpy as np
import jax
import jax.numpy as jnp
from jax import lax
from jax.experimental import pallas as pl
from jax.experimental.pallas import tpu as pltpu

F32 = jnp.float32
BF16 = jnp.bfloat16

DEPTH = 2
MLA_HEADS = 8
Q_LORA = 384
KV_LORA = 256
QK_NOPE = 64
QK_ROPE = 32
V_HEAD = 64
ROPE_THETA = 10000.0
SG_GROUPS = 8
SG_DIM = 512
SG_CHUNK = 128
RWKV_HEADS = 8
RWKV_HEAD = 64
RWKV_DIM = 512
GN_EPS = 64e-5
NORM_EPS = 1e-6
D_FF = 2816
RWKV_IN = 1920
N_IN = 6688
ADAM_LR, ADAM_B1, ADAM_B2, ADAM_EPS, ADAM_WD, ADAM_STEP = 0.001, 0.9, 0.999, 1e-08, 0.01, 10

LANES = 128
HEAD_PAD = 128
VMEM_LIMIT = 56 * 1024 * 1024
MESH = pl.DeviceIdType.MESH

WEIGHTS = ['attn_norm_g', 'w_in', 'gate_b', 'q_norm_g', 'w_uq', 'kv_norm_g', 'w_ukv', 'sg_ln_g', 'sg_ln_b', 'sg_w',
           'sg_b', 'rw_mu', 'rw_w0', 'rw_w2', 'rw_a0', 'rw_a2', 'rw_g2', 'rw_k_k', 'rw_k_a', 'rw_r_k', 'rw_ln_g',
           'rw_ln_b', 'w_branch', 'w_out', 'ffn_norm_g', 'w_ffn_gate', 'w_ffn_up', 'w_ffn_down', 'final_norm_g']
SHARD_AXIS = {'w_in': 2, 'gate_b': 2, 'w_uq': 2, 'w_ukv': 2, 'rw_w0': 2, 'rw_w2': 3, 'rw_a0': 2, 'rw_a2': 3,
              'rw_g2': 2, 'w_branch': 3, 'w_out': 1, 'w_ffn_gate': 2, 'w_ffn_up': 2, 'w_ffn_down': 1}
SHARDED = [n for n in WEIGHTS if n in SHARD_AXIS]
REPLICATED = [n for n in WEIGHTS if n not in SHARD_AXIS]
SMALL_SHARDED = ['gate_b', 'rw_w0', 'rw_a0']
MATMUL_SHARDED = [n for n in SHARDED if n not in SMALL_SHARDED]


def _params(sem=None):
    return pltpu.CompilerParams(dimension_semantics=sem, vmem_limit_bytes=VMEM_LIMIT)


def _pick(n, cands):
    for c in cands:
        if n % c == 0:
            return c
    return n


def _dot(a, b, dims):
    return lax.dot_general(a.astype(BF16), b.astype(BF16), (dims, ((), ())), preferred_element_type=F32)


def _nn(a, b):
    return _dot(a, b, ((1,), (0,)))


def _nt(a, b):
    return _dot(a, b, ((1,), (1,)))


def _tn(a, b):
    return _dot(a, b, ((0,), (0,)))


@jax.custom_vjp
def mm(a, b):
    return _nn(a, b)


mm.defvjp(lambda a, b: (_nn(a, b), (a, b)), lambda res, g: (_nt(g, res[1]), _tn(res[0], g)))


@jax.custom_vjp
def mm_nt(a, b):
    return _nt(a, b)


mm_nt.defvjp(lambda a, b: (_nt(a, b), (a, b)), lambda res, g: (_nn(g, res[1]), _tn(g, res[0])))


def _seg_raw(x, ones):
    hi = x.astype(BF16)
    lo = (x - hi.astype(F32)).astype(BF16)
    d = (((1,), (0,)), ((), ()))
    return (lax.dot_general(hi, ones, d, preferred_element_type=F32)
            + lax.dot_general(lo, ones, d, preferred_element_type=F32))


@jax.custom_vjp
def segsum(x, ones):
    return _seg_raw(x, ones)


segsum.defvjp(lambda x, ones: (_seg_raw(x, ones), ones),
              lambda ones, g: (_seg_raw(g, ones), jnp.zeros_like(ones)))


def _sigmoid(x):
    return 0.5 * (jnp.tanh(0.5 * x) + 1.0)


def _rms(x, g):
    return x * lax.rsqrt(jnp.mean(x * x, axis=-1, keepdims=True) + NORM_EPS) * g


def matmul(name, a, b, mode, add=None, out_dtype=F32):
    if mode == 'nn':
        (M, K), (_, N) = a.shape, b.shape
    elif mode == 'nt':
        (M, K), (N, _) = a.shape, b.shape
    else:
        (K, M), (_, N) = a.shape, b.shape
    tm = _pick(M, (1408, 1024, 512, 384, 256, 128))
    tn = _pick(N, (1408, 1024, 768, 512, 384, 256, 128))
    tk = _pick(K, (512, 384, 256, 128))
    nk = K // tk
    dims = {'nn': ((1,), (0,)), 'nt': ((1,), (1,)), 'tn': ((0,), (0,))}[mode]
    a_spec = pl.BlockSpec((tk, tm), lambda i, j, k: (k, i)) if mode == 'tn' else pl.BlockSpec((tm, tk), lambda i, j, k: (i, k))
    b_spec = pl.BlockSpec((tn, tk), lambda i, j, k: (j, k)) if mode == 'nt' else pl.BlockSpec((tk, tn), lambda i, j, k: (k, j))
    o_spec = pl.BlockSpec((tm, tn), lambda i, j, k: (i, j))
    has_add = add is not None

    def body(*refs):
        if has_add:
            a_ref, b_ref, add_ref, o_ref, acc = refs
        else:
            a_ref, b_ref, o_ref, acc = refs
        k = pl.program_id(2)

        @pl.when(k == 0)
        def _():
            acc[...] = jnp.zeros_like(acc)

        acc[...] += _dot(a_ref[...], b_ref[...], dims)

        @pl.when(k == nk - 1)
        def _():
            o_ref[...] = (acc[...] + add_ref[...] if has_add else acc[...]).astype(o_ref.dtype)

    ins = [a, b] + ([add] if has_add else [])
    specs = [a_spec, b_spec] + ([o_spec] if has_add else [])
    return pl.pallas_call(
        body, name=name, grid=(M // tm, N // tn, nk), in_specs=specs, out_specs=o_spec,
        out_shape=jax.ShapeDtypeStruct((M, N), out_dtype), scratch_shapes=[pltpu.VMEM((tm, tn), F32)],
        compiler_params=_params(("parallel", "parallel", "arbitrary")))(*ins)


def _full_spec(p):
    nd = p.ndim
    return pl.BlockSpec(p.shape, lambda i, _nd=nd: (0,) * _nd)


def rowwise(name, fn, rows, params, consts, out_widths, tm, out_dtypes=None):
    N = rows[0].shape[0]
    nr, npar, nc = len(rows), len(params), len(consts)

    def body(*refs):
        vals = [r[...] for r in refs[:nr + npar + nc]]
        res = fn(*vals)
        for o, v in zip(refs[nr + npar + nc:], res):
            o[...] = v.astype(o.dtype)

    in_specs = ([pl.BlockSpec((tm, r.shape[1]), lambda i: (i, 0)) for r in rows]
                + [_full_spec(p) for p in list(params) + list(consts)])
    out_specs = [pl.BlockSpec((tm, w), lambda i: (i, 0)) for w in out_widths]
    return pl.pallas_call(
        body, name=name, grid=(N // tm,), in_specs=in_specs, out_specs=out_specs,
        out_shape=[jax.ShapeDtypeStruct((N, w), d) for w, d in zip(out_widths, out_dtypes or [F32] * len(out_widths))],
        compiler_params=_params(("parallel",)))(*rows, *params, *consts)


def rowwise_bwd(name, fn, rows, params, consts, d_outs, tm, n_row_diff=None, extra=(), drow_dtypes=None):
    N = rows[0].shape[0]
    nr, npar, nc = len(rows), len(params), len(consts)
    nd = nr if n_row_diff is None else n_row_diff
    counts = [len(p) for p in d_outs]
    flat_d = [a for parts in d_outs for a in parts]
    nflat, nex = len(flat_d), len(extra)

    def body(*refs):
        pos = 0
        row_v = [r[...] for r in refs[pos:pos + nr]]; pos += nr
        par_v = [r[...] for r in refs[pos:pos + npar]]; pos += npar
        con_v = [r[...] for r in refs[pos:pos + nc]]; pos += nc
        d_refs = refs[pos:pos + nflat]; pos += nflat
        ex_refs = refs[pos:pos + nex]; pos += nex
        drow_refs = refs[pos:pos + nd]; pos += nd
        dpar_refs = refs[pos:pos + npar]

        def f(*diff):
            return fn(*diff[:nd], *row_v[nd:], *diff[nd:], *con_v)

        _, vjp = jax.vjp(f, *row_v[:nd], *par_v)
        cts, q = [], 0
        for c in counts:
            g = d_refs[q][...].astype(F32)
            for t in range(1, c):
                g = g + d_refs[q + t][...].astype(F32)
            cts.append(g)
            q += c
        grads = vjp(tuple(cts))
        drow = list(grads[:nd])
        for (idx, _), r in zip(extra, ex_refs):
            drow[idx] = drow[idx] + r[...].astype(F32)
        for o, v in zip(drow_refs, drow):
            o[...] = v.astype(o.dtype)

        @pl.when(pl.program_id(0) == 0)
        def _():
            for o in dpar_refs:
                o[...] = jnp.zeros_like(o)

        for o, v in zip(dpar_refs, grads[nd:]):
            o[...] += v

    ex_arrs = [a for _, a in extra]
    in_specs = ([pl.BlockSpec((tm, r.shape[1]), lambda i: (i, 0)) for r in rows]
                + [_full_spec(p) for p in list(params) + list(consts)]
                + [pl.BlockSpec((tm, a.shape[1]), lambda i: (i, 0)) for a in flat_d + ex_arrs])
    out_specs = ([pl.BlockSpec((tm, r.shape[1]), lambda i: (i, 0)) for r in rows[:nd]]
                 + [_full_spec(p) for p in params])
    out_shape = ([jax.ShapeDtypeStruct(r.shape, d) for r, d in zip(rows[:nd], drow_dtypes or [F32] * nd)]
                 + [jax.ShapeDtypeStruct(p.shape, F32) for p in params])
    res = pl.pallas_call(
        body, name=name, grid=(N // tm,), in_specs=in_specs, out_specs=out_specs, out_shape=out_shape,
        compiler_params=_params(("arbitrary",)))(*rows, *params, *consts, *flat_d, *ex_arrs)
    return list(res[:nd]), list(res[nd:])


def f_rms(x, g):
    return (_rms(x, g),)


def f_rope(qq, kv, krr, ct, st):
    hw = MLA_HEADS * HEAD_PAD
    c8 = jnp.tile(ct, (1, MLA_HEADS))
    s8 = jnp.tile(st, (1, MLA_HEADS))
    q = qq[:, :hw] * c8 + qq[:, hw:] * s8
    kr = krr[:, :HEAD_PAD] * ct + krr[:, HEAD_PAD:] * st
    lane = lax.broadcasted_iota(jnp.int32, kv.shape, 1) % HEAD_PAD
    k = jnp.where(lane < QK_NOPE, kv, jnp.tile(kr, (1, MLA_HEADS)))
    return q, k


def f_sg(p, ln_g, ln_b, w, bias):
    z = 0.5 * p * (1.0 + jnp.tanh(0.7978845608028654 * (p + 0.044715 * p * p * p)))
    u, v = z[:, :SG_DIM], z[:, SG_DIM:]
    mu = jnp.mean(v, axis=-1, keepdims=True)
    var = jnp.mean(jnp.square(v - mu), axis=-1, keepdims=True)
    v = (v - mu) * lax.rsqrt(var + 1e-5) * ln_g + ln_b
    lane = lax.broadcasted_iota(jnp.int32, (SG_CHUNK, LANES), 1)
    outs = []
    for c in range(p.shape[0] // SG_CHUNK):
        vc = v[c * SG_CHUNK:(c + 1) * SG_CHUNK]
        cols = []
        for m in range(SG_DIM // LANES):
            blk = vc[:, m * LANES:(m + 1) * LANES]
            cols.append(jnp.where(lane < 64, mm(w[2 * m], blk), mm(w[2 * m + 1], blk)))
        outs.append(jnp.concatenate(cols, axis=1) + bias)
    mixed = outs[0] if len(outs) == 1 else jnp.concatenate(outs, axis=0)
    return (u * mixed,)


def f_rw_pre(z, zp, zn, mu, w0, w2, a0, a2, g2, k_k, k_a, ones):
    z = z + mu * (0.5 * (zp + zn) - z)
    C = RWKV_DIM
    r, k, v = z[:, :C], z[:, C:2 * C], z[:, 2 * C:3 * C]
    wl, al, gl = z[:, 3 * C:3 * C + 128], z[:, 3 * C + 128:3 * C + 256], z[:, 3 * C + 256:]
    w = w0 + mm(jnp.tanh(wl), w2)
    decay = jnp.exp(-0.6065306597126334 * _sigmoid(w))
    a = _sigmoid(a0 + mm(al, a2))
    g = mm(_sigmoid(gl), g2)
    kk = k * k_k
    kk = kk / jnp.maximum(jnp.sqrt(segsum(kk * kk, ones)), 1e-12)
    k2 = jnp.concatenate([k, k], axis=1)
    kdir = k2 * (1.0 + (a - 1.0) * jnp.concatenate([k_a, k_a], axis=1))
    bdir = jnp.concatenate([kk, kk], axis=1) * a
    return r, v, decay, kdir, kk, bdir, g


def f_rw_post(y, r, v, kdir, g, ln_g, ln_b, r_k, ones):
    mean = segsum(y, ones) * (1.0 / RWKV_HEAD)
    yc = y - mean
    var = segsum(yc * yc, ones) * (1.0 / RWKV_HEAD)
    y = yc * lax.rsqrt(var + GN_EPS) * ln_g + ln_b
    C = RWKV_DIM
    bonus = segsum(r * kdir[:, :C] * r_k, ones) + segsum(r * kdir[:, C:] * r_k, ones)
    return ((y + bonus * v) * g,)


def f_merge(pg, b0, b1, b2, gate_b):
    D = b0.shape[1]
    gt = _sigmoid(pg + gate_b)
    return (gt[:, :D] * b0 + gt[:, D:2 * D] * b1 + gt[:, 2 * D:] * b2,)


def f_swiglu(au):
    a, u = au[:, :D_FF], au[:, D_FF:]
    return (a * _sigmoid(a) * u,)


def f_add3(a, b, c):
    return (a + b + c,)


def loss_head(x, tgt, g, tm):
    N, D = x.shape

    def body(x_ref, t_ref, g_ref, loss_ref, dx_ref, dg_ref):
        t = t_ref[...]

        def f(xv, gv):
            err = _rms(xv, gv) - t
            return 0.5 * jnp.sum(jnp.mean(err * err, axis=-1, keepdims=True))

        val, (dx, dg) = jax.value_and_grad(f, argnums=(0, 1))(x_ref[...], g_ref[...])
        dx_ref[...] = dx

        @pl.when(pl.program_id(0) == 0)
        def _():
            loss_ref[...] = jnp.zeros_like(loss_ref)
            dg_ref[...] = jnp.zeros_like(dg_ref)

        loss_ref[...] += jnp.full(loss_ref.shape, val, F32)
        dg_ref[...] += dg

    row = pl.BlockSpec((tm, D), lambda i: (i, 0))
    return pl.pallas_call(
        body, name="loss_head", grid=(N // tm,), in_specs=[row, row, _full_spec(g)],
        out_specs=[pl.BlockSpec((1, LANES), lambda i: (0, 0)), row, _full_spec(g)],
        out_shape=[jax.ShapeDtypeStruct((1, LANES), F32), jax.ShapeDtypeStruct((N, D), F32),
                   jax.ShapeDtypeStruct(g.shape, F32)],
        compiler_params=_params(("arbitrary",)))(x, tgt, g)


ATT_SCALE = float((QK_NOPE + QK_ROPE) ** -0.5)


def _attn_block(q, k, kv):
    s = mm_nt(q, k) * ATT_SCALE
    m = lax.stop_gradient(jnp.max(s, axis=-1, keepdims=True))
    e = jnp.exp(s - m)
    return mm(e, kv) * (1.0 / jnp.sum(e, axis=-1, keepdims=True))


def attention_fwd(name, q, k, kv, B, S, tq, out_dtype):
    nq = S // tq
    qspec = pl.BlockSpec((tq, HEAD_PAD), lambda b, h, i: (b * nq + i, h))
    kspec = pl.BlockSpec((S, HEAD_PAD), lambda b, h, i: (b, h))

    def body(q_ref, k_ref, kv_ref, o_ref):
        o_ref[...] = _attn_block(q_ref[...], k_ref[...], kv_ref[...]).astype(o_ref.dtype)

    return pl.pallas_call(
        body, name=name, grid=(B, MLA_HEADS, nq), in_specs=[qspec, kspec, kspec], out_specs=qspec,
        out_shape=jax.ShapeDtypeStruct(q.shape, out_dtype),
        compiler_params=_params(("parallel", "parallel", "arbitrary")))(q, k, kv)


def attention_bwd(name, q, k, kv, do, B, S, tq):
    nq = S // tq
    qspec = pl.BlockSpec((tq, HEAD_PAD), lambda b, h, i: (b * nq + i, h))
    kspec = pl.BlockSpec((S, HEAD_PAD), lambda b, h, i: (b, h))

    def body(q_ref, k_ref, kv_ref, do_ref, dq_ref, dk_ref, dkv_ref):
        _, vjp = jax.vjp(_attn_block, q_ref[...], k_ref[...], kv_ref[...])
        dq, dk, dkv = vjp(do_ref[...])
        dq_ref[...] = dq

        @pl.when(pl.program_id(2) == 0)
        def _():
            dk_ref[...] = jnp.zeros_like(dk_ref)
            dkv_ref[...] = jnp.zeros_like(dkv_ref)

        dk_ref[...] += dk
        dkv_ref[...] += dkv

    sh = jax.ShapeDtypeStruct(q.shape, F32)
    return pl.pallas_call(
        body, name=name, grid=(B, MLA_HEADS, nq), in_specs=[qspec, kspec, kspec, qspec],
        out_specs=[qspec, kspec, kspec], out_shape=[sh, sh, sh],
        compiler_params=_params(("parallel", "parallel", "arbitrary")))(q, k, kv, do)


SCAN_TC = 8
SCAN_UNROLL = 16


def _jloop(n, body, init):
    def outer(o, c):
        for u in range(SCAN_UNROLL):
            c = body(o * SCAN_UNROLL + u, c)
        return c

    return lax.fori_loop(0, n // SCAN_UNROLL, outer, init)


def _dir_mask(L, Ip):
    per_dir = L // (2 * (RWKV_HEAD // Ip))
    lane = lax.broadcasted_iota(jnp.int32, (1, L), 1)
    return (lane // per_dir) % 2 == 1


def _merge_dirs(mask, fwd_ref, rev_ref, out_ref):
    for tt in range(SCAN_TC):
        out_ref[tt] = jnp.where(mask, rev_ref[SCAN_TC - 1 - tt], fwd_ref[tt])


def scan_fwd(name, w, k, b, kk, r, v):
    T, J, L = w.shape
    Ip = v.shape[1]
    nT = T // SCAN_TC
    fwd3, rev3 = (lambda g: (g, 0, 0)), (lambda g: (nT - 1 - g, 0, 0))
    jf, jr = pl.BlockSpec((SCAN_TC, J, L), fwd3), pl.BlockSpec((SCAN_TC, J, L), rev3)
    i_f, i_r = pl.BlockSpec((SCAN_TC, Ip, L), fwd3), pl.BlockSpec((SCAN_TC, Ip, L), rev3)
    sspec = pl.BlockSpec((SCAN_TC, J, Ip, L), lambda g: (g, 0, 0, 0))
    last_spec = pl.BlockSpec((J, Ip, L), lambda g: (0, 0, 0))

    def body(wf, wr, kf, kr, bf, br, kkf, kkr, rf, rr, vf, vr, yf_ref, yr_ref, sp_ref, sa_ref, last_ref,
             s_ref, w_ref, k_ref, b_ref, kk_ref, r_ref, v_ref):
        @pl.when(pl.program_id(0) == 0)
        def _():
            s_ref[...] = jnp.zeros_like(s_ref)

        mask = _dir_mask(L, Ip)
        for f_, r_, m_ in ((wf, wr, w_ref), (kf, kr, k_ref), (bf, br, b_ref), (kkf, kkr, kk_ref), (rf, rr, r_ref),
                           (vf, vr, v_ref)):
            _merge_dirs(mask, f_, r_, m_)

        def row(ref, tt, j):
            return jnp.broadcast_to(ref[tt, pl.ds(j, 1), :], (Ip, L))

        def step(tt, carry):
            def p1(j, sa):
                s = s_ref[j]
                sp_ref[tt, j] = s
                return sa + s * row(kk_ref, tt, j)

            sa = _jloop(J, p1, jnp.zeros((Ip, L), F32))
            sa_ref[tt] = sa
            vt = v_ref[tt]

            def p2(j, y):
                s = s_ref[j] * row(w_ref, tt, j) - sa * row(b_ref, tt, j) + vt * row(k_ref, tt, j)
                s_ref[j] = s
                return y + s * row(r_ref, tt, j)

            y = _jloop(J, p2, jnp.zeros((Ip, L), F32))
            yf_ref[tt] = y
            yr_ref[SCAN_TC - 1 - tt] = y
            return carry

        lax.fori_loop(0, SCAN_TC, step, 0)

        @pl.when(pl.program_id(0) == nT - 1)
        def _():
            last_ref[...] = s_ref[...]

    ish = jax.ShapeDtypeStruct((T, Ip, L), F32)
    jscr = pltpu.VMEM((SCAN_TC, J, L), F32)
    return pl.pallas_call(
        body, name=name, grid=(nT,), in_specs=[jf, jr] * 5 + [i_f, i_r],
        out_specs=[i_f, i_r, sspec, i_f, last_spec],
        out_shape=[ish, ish, jax.ShapeDtypeStruct((T, J, Ip, L), F32), ish, jax.ShapeDtypeStruct((J, Ip, L), F32)],
        scratch_shapes=[pltpu.VMEM((J, Ip, L), F32)] + [jscr] * 5 + [pltpu.VMEM((SCAN_TC, Ip, L), F32)],
        compiler_params=_params(("arbitrary",)))(w, w, k, k, b, b, kk, kk, r, r, v, v)


def scan_bwd(name, w, k, b, kk, r, v, sp, sa_all, s_last, dy):
    T, J, L = w.shape
    Ip = v.shape[1]
    nT = T // SCAN_TC
    stp3, mir3 = (lambda g: (nT - 1 - g, 0, 0)), (lambda g: (g, 0, 0))
    jf, jr = pl.BlockSpec((SCAN_TC, J, L), stp3), pl.BlockSpec((SCAN_TC, J, L), mir3)
    i_f, i_r = pl.BlockSpec((SCAN_TC, Ip, L), stp3), pl.BlockSpec((SCAN_TC, Ip, L), mir3)
    sspec = pl.BlockSpec((SCAN_TC, J, Ip, L), lambda g: (nT - 1 - g, 0, 0, 0))
    last_spec = pl.BlockSpec((J, Ip, L), lambda g: (0, 0, 0))

    def body(wf, wr, kf, kr, bf, br, kkf, kkr, rf, rr, vf, vr, dyf, dyr, sp_ref, sa_ref, last_ref,
             dwf, dwr, dkf, dkr, dbf, dbr, dkkf, dkkr, drf, drr, dvf, dvr,
             ds_ref, nxt_ref, w_ref, k_ref, b_ref, kk_ref, r_ref, v_ref, dy_ref):
        @pl.when(pl.program_id(0) == 0)
        def _():
            ds_ref[...] = jnp.zeros_like(ds_ref)
            nxt_ref[...] = last_ref[...]

        mask = _dir_mask(L, Ip)
        for f_, r_, m_ in ((wf, wr, w_ref), (kf, kr, k_ref), (bf, br, b_ref), (kkf, kkr, kk_ref), (rf, rr, r_ref),
                           (vf, vr, v_ref), (dyf, dyr, dy_ref)):
            _merge_dirs(mask, f_, r_, m_)

        def row(ref, tt, j):
            return jnp.broadcast_to(ref[tt, pl.ds(j, 1), :], (Ip, L))

        def rsum(x):
            return jnp.sum(x, axis=0, keepdims=True)

        def make_step(first):
            def step(n, carry):
                tt = SCAN_TC - 1 - n
                dyt, vt, sa = dy_ref[tt], v_ref[tt], sa_ref[tt]

                def p1(j, c):
                    dsa, dv = c
                    ds = ds_ref[j] + dyt * row(r_ref, tt, j)
                    ds_ref[j] = ds
                    return dsa - ds * row(b_ref, tt, j), dv + ds * row(k_ref, tt, j)

                z = jnp.zeros((Ip, L), F32)
                dsa, dv = _jloop(J, p1, (z, z))
                dvf[tt] = dv
                dvr[SCAN_TC - 1 - tt] = dv

                def put(f_ref, r_ref_, j, val):
                    f_ref[tt, pl.ds(j, 1), :] = val
                    r_ref_[SCAN_TC - 1 - tt, pl.ds(j, 1), :] = val

                def p2(j, c):
                    ds = ds_ref[j]
                    s0 = sp_ref[tt, j]
                    s1 = nxt_ref[j] if first else sp_ref[tt + 1, j]
                    put(drf, drr, j, rsum(s1 * dyt))
                    put(dkf, dkr, j, rsum(ds * vt))
                    put(dbf, dbr, j, -rsum(ds * sa))
                    put(dwf, dwr, j, rsum(ds * s0))
                    put(dkkf, dkkr, j, rsum(s0 * dsa))
                    ds_ref[j] = ds * row(w_ref, tt, j) + dsa * row(kk_ref, tt, j)
                    return c

                _jloop(J, p2, 0)
                return carry

            return step

        make_step(True)(0, 0)
        lax.fori_loop(1, SCAN_TC, make_step(False), 0)
        nxt_ref[...] = sp_ref[0]

    jsh = jax.ShapeDtypeStruct((T, J, L), F32)
    ish = jax.ShapeDtypeStruct((T, Ip, L), F32)
    jscr = pltpu.VMEM((SCAN_TC, J, L), F32)
    iscr = pltpu.VMEM((SCAN_TC, Ip, L), F32)
    return pl.pallas_call(
        body, name=name, grid=(nT,), in_specs=[jf, jr] * 5 + [i_f, i_r] * 2 + [sspec, i_f, last_spec],
        out_specs=[jf, jr] * 5 + [i_f, i_r], out_shape=[jsh] * 10 + [ish] * 2,
        scratch_shapes=[pltpu.VMEM((J, Ip, L), F32)] * 2 + [jscr] * 5 + [iscr] * 2,
        compiler_params=_params(("arbitrary",)))(w, w, k, k, b, b, kk, kk, r, r, v, v, dy, dy, sp, sa_all, s_last)


LAYOUT_TT = 128
LAYOUT_U = 8


def _lane_group(L, n):
    return lax.broadcasted_iota(jnp.int32, (1, L), 1) // (L // n)


def to_scan(name, x, B, S, nd, Ip):
    isplit = LANES // (2 * B * RWKV_HEADS)
    tt_n = _pick(S, (LAYOUT_TT, 64, 16, 8))
    x5 = x.reshape(B, S, nd, RWKV_HEADS, RWKV_HEAD)
    rows_out = RWKV_HEAD if Ip is None else Ip

    def body(x_ref, o_ref):
        group = _lane_group(LANES, isplit)

        def chunk(c, carry):
            t0 = c * LAYOUT_U
            z = jnp.concatenate([x_ref[b, t0 + u, min(d, nd - 1)] for u in range(LAYOUT_U) for _ in range(isplit)
                                 for d in range(2) for b in range(B)], axis=0)
            mt = z.T
            for u in range(LAYOUT_U):
                m = mt[:, u * LANES:(u + 1) * LANES]
                if Ip is not None:
                    m = sum(jnp.where(group == i2, m[i2 * Ip:(i2 + 1) * Ip], 0.0) for i2 in range(isplit))
                o_ref[t0 + u] = m
            return carry

        lax.fori_loop(0, tt_n // LAYOUT_U, chunk, 0)

    return pl.pallas_call(
        body, name=name, grid=(S // tt_n,),
        in_specs=[pl.BlockSpec((B, tt_n, nd, RWKV_HEADS, RWKV_HEAD), lambda g: (0, g, 0, 0, 0))],
        out_specs=pl.BlockSpec((tt_n, rows_out, LANES), lambda g: (g, 0, 0)),
        out_shape=jax.ShapeDtypeStruct((S, rows_out, LANES), F32), compiler_params=_params(("parallel",)))(x5)


def from_scan(name, f, r, B, S, nd, i_indexed):
    rows_in = f.shape[1]
    isplit = LANES // (2 * B * RWKV_HEADS)
    Ip = rows_in if i_indexed else RWKV_HEAD // isplit
    tt_n = _pick(S, (LAYOUT_TT, 64, 16, 8))
    per_i2 = LANES // isplit

    def body(f_ref, r_ref, o_ref):
        mask = _dir_mask(LANES, Ip)
        group = _lane_group(LANES, isplit)

        def chunk(c, carry):
            t0 = c * LAYOUT_U
            ms = []
            for u in range(LAYOUT_U):
                m = jnp.where(mask, r_ref[t0 + u], f_ref[t0 + u])
                if i_indexed:
                    m = jnp.concatenate([jnp.where(group == i2, m, 0.0) for i2 in range(isplit)], axis=0)
                ms.append(m)
            zt = jnp.concatenate(ms, axis=1).T
            for u in range(LAYOUT_U):
                z = zt[u * LANES:(u + 1) * LANES]
                zf = sum(z[i2 * per_i2:(i2 + 1) * per_i2] for i2 in range(isplit))
                for b in range(B):
                    d0 = zf[b * RWKV_HEADS:(b + 1) * RWKV_HEADS]
                    d1 = zf[(B + b) * RWKV_HEADS:(B + b + 1) * RWKV_HEADS]
                    if nd == 1:
                        o_ref[b, t0 + u, 0] = d0 + d1
                    else:
                        o_ref[b, t0 + u, 0] = d0
                        o_ref[b, t0 + u, 1] = d1
            return carry

        lax.fori_loop(0, tt_n // LAYOUT_U, chunk, 0)

    spec = pl.BlockSpec((tt_n, rows_in, LANES), lambda g: (g, 0, 0))
    out = pl.pallas_call(
        body, name=name, grid=(S // tt_n,), in_specs=[spec, spec],
        out_specs=pl.BlockSpec((B, tt_n, nd, RWKV_HEADS, RWKV_HEAD), lambda g: (0, g, 0, 0, 0)),
        out_shape=jax.ShapeDtypeStruct((B, S, nd, RWKV_HEADS, RWKV_HEAD), F32),
        compiler_params=_params(("parallel",)))(f, r)
    return out.reshape(B * S, nd * RWKV_DIM)


def adamw(name, w, g, m, v):
    R, C = w.shape
    tr = _pick(R, (256, 128, 64, 32, 16, 8))
    c1 = 1.0 - ADAM_B1 ** ADAM_STEP
    c2 = 1.0 - ADAM_B2 ** ADAM_STEP

    def body(w_ref, g_ref, m_ref, v_ref, d_ref, nm_ref, nv_ref):
        gv = g_ref[...]
        nm = ADAM_B1 * m_ref[...] + (1.0 - ADAM_B1) * gv
        nv = ADAM_B2 * v_ref[...] + (1.0 - ADAM_B2) * jnp.square(gv)
        d_ref[...] = -ADAM_LR * ((nm / c1) / (jnp.sqrt(nv / c2) + ADAM_EPS) + ADAM_WD * w_ref[...])
        nm_ref[...] = nm
        nv_ref[...] = nv

    spec = pl.BlockSpec((tr, C), lambda i: (i, 0))
    sh = jax.ShapeDtypeStruct((R, C), F32)
    return pl.pallas_call(body, name=name, grid=(R // tr,), in_specs=[spec] * 4, out_specs=[spec] * 3,
                          out_shape=[sh] * 3, compiler_params=_params(("parallel",)))(w, g, m, v)


def sum_slots(name, x):
    n, R, C = x.shape
    tr = _pick(R, (256, 128, 64, 32, 16, 8))

    def body(x_ref, o_ref):
        acc = x_ref[0].astype(F32)
        for s in range(1, n):
            acc = acc + x_ref[s].astype(F32)
        o_ref[...] = acc

    return pl.pallas_call(
        body, name=name, grid=(R // tr,), in_specs=[pl.BlockSpec((n, tr, C), lambda i: (0, i, 0))],
        out_specs=pl.BlockSpec((tr, C), lambda i: (i, 0)), out_shape=jax.ShapeDtypeStruct((R, C), F32),
        compiler_params=_params(("parallel",)))(x)


ANY = pl.BlockSpec(memory_space=pl.ANY)


def _xyc():
    return lax.axis_index("x"), lax.axis_index("y"), lax.axis_index("c")


def gather_shards(shard):
    _, R, C = shard.shape

    def body(x_ref, out_ref, send_sems, recv_sems, local_sem):
        x, y, c = _xyc()
        me, sibling = (x, y, c), (x, y, 1 - c)
        chips = [(1 - x, y), (x, 1 - y), (1 - x, 1 - y)]

        def cp(k, cx, cy, half, to, src=None):
            dst = out_ref.at[2 * cx + cy, half]
            return pltpu.make_async_remote_copy(
                src_ref=dst if src is None else src, dst_ref=dst, send_sem=send_sems.at[k],
                recv_sem=recv_sems.at[k], device_id=to, device_id_type=MESH)

        mine = pltpu.make_async_copy(x_ref, out_ref.at[2 * x + y], local_sem)
        mine.start()
        first = [cp(j, x, y, c, (*chip, c), src=x_ref.at[c]) for j, chip in enumerate(chips)]
        for f in first:
            f.start()
        passed = [cp(3 + j, *chip, c, sibling) for j, chip in enumerate(chips)]
        for j, chip in enumerate(chips):
            cp(j, *chip, c, me).wait_recv()
            passed[j].start()
        for j, chip in enumerate(chips):
            cp(3 + j, *chip, 1 - c, me).wait_recv()
        for f in first + passed:
            f.wait_send()
        mine.wait()

    return pl.pallas_call(
        body, name="gather_shards", in_specs=[ANY], out_specs=ANY,
        out_shape=jax.ShapeDtypeStruct((4, 2, R, C), shard.dtype),
        scratch_shapes=[pltpu.SemaphoreType.DMA((6,)), pltpu.SemaphoreType.DMA((6,)), pltpu.SemaphoreType.DMA])(shard)


FLIPS = [(0, 0, 1), (0, 1, 0), (0, 1, 1), (1, 0, 0), (1, 0, 1), (1, 1, 0), (1, 1, 1)]


def scatter_partials(g):
    _, _, R, C = g.shape

    def body(g_ref, out_ref, send_sems, recv_sems, local_sem):
        x, y, c = _xyc()
        me_idx = 4 * x + 2 * y + c
        mine = pltpu.make_async_copy(g_ref.at[2 * x + y, c], out_ref.at[me_idx], local_sem)
        mine.start()
        sends = []
        for k, (fx, fy, fc) in enumerate(FLIPS):
            px, py, pc = (x + fx) % 2, (y + fy) % 2, (c + fc) % 2
            s = pltpu.make_async_remote_copy(
                src_ref=g_ref.at[2 * px + py, pc], dst_ref=out_ref.at[me_idx], send_sem=send_sems.at[k],
                recv_sem=recv_sems.at[k], device_id=(px, py, pc), device_id_type=MESH)
            s.start()
            sends.append(s)
        for k, (fx, fy, fc) in enumerate(FLIPS):
            px, py, pc = (x + fx) % 2, (y + fy) % 2, (c + fc) % 2
            slot = out_ref.at[4 * px + 2 * py + pc]
            pltpu.make_async_remote_copy(
                src_ref=slot, dst_ref=slot, send_sem=send_sems.at[k], recv_sem=recv_sems.at[k],
                device_id=(px, py, pc), device_id_type=MESH).wait_recv()
        for s in sends:
            s.wait_send()
        mine.wait()

    return pl.pallas_call(
        body, name="scatter_partials", in_specs=[ANY], out_specs=ANY,
        out_shape=jax.ShapeDtypeStruct((8, R, C), g.dtype),
        scratch_shapes=[pltpu.SemaphoreType.DMA((7,)), pltpu.SemaphoreType.DMA((7,)), pltpu.SemaphoreType.DMA])(g)


JOIN_ROWS = 256


def sum_join(x):
    n, R, C = x.shape
    steps = R // JOIN_ROWS

    def body(x_ref, out_ref, acc, local_sems, send_sems, recv_sem):
        i = pl.program_id(0)
        slot = lax.rem(i, 2)
        xx, yy, c = _xyc()
        sibling = (xx, yy, 1 - c)

        def copies(k, s):
            dst = out_ref.at[c, pl.ds(k * JOIN_ROWS, JOIN_ROWS)]
            return (pltpu.make_async_copy(acc.at[s], dst, local_sems.at[s]),
                    pltpu.make_async_remote_copy(src_ref=acc.at[s], dst_ref=dst, send_sem=send_sems.at[s],
                                                 recv_sem=recv_sem, device_id=sibling, device_id_type=MESH))

        def drain(k, s):
            loc, rem = copies(k, s)
            loc.wait()
            rem.wait_send()

        @pl.when(i >= 2)
        def _():
            drain(i - 2, slot)

        a = x_ref[0].astype(F32)
        for s in range(1, n):
            a = a + x_ref[s].astype(F32)
        acc[slot] = a
        loc, rem = copies(i, slot)
        loc.start()
        rem.start()

        @pl.when(i == steps - 1)
        def _():
            if steps >= 2:
                drain(i - 1, 1 - slot)
            drain(i, slot)
            theirs = out_ref.at[1 - c]
            pltpu.make_async_remote_copy(src_ref=theirs, dst_ref=theirs, send_sem=send_sems.at[0], recv_sem=recv_sem,
                                         device_id=sibling, device_id_type=MESH).wait_recv()

    return pl.pallas_call(
        body, name="sum_join", grid=(steps,),
        in_specs=[pl.BlockSpec((n, JOIN_ROWS, C), lambda i: (0, i, 0))], out_specs=ANY,
        out_shape=jax.ShapeDtypeStruct((2, R, C), F32),
        scratch_shapes=[pltpu.VMEM((2, JOIN_ROWS, C), F32), pltpu.SemaphoreType.DMA((2,)),
                        pltpu.SemaphoreType.DMA((2,)), pltpu.SemaphoreType.DMA],
        compiler_params=_params(("arbitrary",)))(x)


def gather_all(name, block):
    R, C = block.shape

    def body(x_ref, out_ref, send_sems, recv_sems, local_sem):
        x, y, c = _xyc()
        mine = pltpu.make_async_copy(x_ref, out_ref.at[4 * x + 2 * y + c], local_sem)
        mine.start()
        sends = []
        for k, (fx, fy, fc) in enumerate(FLIPS):
            px, py, pc = (x + fx) % 2, (y + fy) % 2, (c + fc) % 2
            s = pltpu.make_async_remote_copy(
                src_ref=x_ref, dst_ref=out_ref.at[4 * x + 2 * y + c], send_sem=send_sems.at[k],
                recv_sem=recv_sems.at[k], device_id=(px, py, pc), device_id_type=MESH)
            s.start()
            sends.append(s)
        for k, (fx, fy, fc) in enumerate(FLIPS):
            px, py, pc = (x + fx) % 2, (y + fy) % 2, (c + fc) % 2
            slot = out_ref.at[4 * px + 2 * py + pc]
            pltpu.make_async_remote_copy(
                src_ref=slot, dst_ref=slot, send_sem=send_sems.at[k], recv_sem=recv_sems.at[k],
                device_id=(px, py, pc), device_id_type=MESH).wait_recv()
        for s in sends:
            s.wait_send()
        mine.wait()

    return pl.pallas_call(
        body, name=name, in_specs=[ANY], out_specs=ANY,
        out_shape=jax.ShapeDtypeStruct((8, R, C), block.dtype),
        scratch_shapes=[pltpu.SemaphoreType.DMA((7,)), pltpu.SemaphoreType.DMA((7,)), pltpu.SemaphoreType.DMA])(block)


PACK_C = 1024


def _pack(arrs, row_mult):
    flat = jnp.concatenate([a.reshape(-1) for a in arrs])
    n = flat.shape[0]
    rows = -(-n // PACK_C)
    rows = -(-rows // row_mult) * row_mult
    return jnp.pad(flat, (0, rows * PACK_C - n)).reshape(rows, PACK_C)


def _unpack(buf, shapes):
    flat = buf.reshape(-1)
    out, off = [], 0
    for s in shapes:
        n = int(np.prod(s))
        out.append(flat[off:off + n].reshape(s))
        off += n
    return out


OFF_Q, OFF_CKV, OFF_KR, OFF_SG, OFF_RW, OFF_GATE, N_IN_PAD = 0, 384, 640, 896, 1920, 3840, 6912
IN_SEGMENTS = [('q', OFF_Q, OFF_CKV), ('ckv', OFF_CKV, OFF_KR), ('kr', OFF_KR, OFF_SG), ('sg', OFF_SG, OFF_RW),
               ('rw', OFF_RW, OFF_GATE), ('gate', OFF_GATE, N_IN_PAD)]
ROPE_LANE = QK_NOPE
HALF = QK_ROPE // 2


def _win_layout():
    src = np.full((N_IN_PAD,), -1, np.int64)
    sgn = np.ones((N_IN_PAD,), np.float32)
    src[0:640] = np.arange(0, 640)
    kr0 = Q_LORA + KV_LORA
    src[OFF_KR + ROPE_LANE:OFF_KR + ROPE_LANE + QK_ROPE] = kr0 + np.arange(QK_ROPE)
    sw = OFF_KR + HEAD_PAD + ROPE_LANE
    src[sw:sw + HALF] = kr0 + HALF + np.arange(HALF)
    sgn[sw:sw + HALF] = -1.0
    src[sw + HALF:sw + QK_ROPE] = kr0 + np.arange(HALF)
    src[OFF_SG:N_IN_PAD] = 672 + np.arange(N_IN_PAD - OFF_SG)
    return src, sgn


def _wuq_layout():
    hw = MLA_HEADS * HEAD_PAD
    src = np.full((2 * hw,), -1, np.int64)
    sgn = np.ones((2 * hw,), np.float32)
    per = QK_NOPE + QK_ROPE
    for h in range(MLA_HEADS):
        src[h * HEAD_PAD:h * HEAD_PAD + per] = h * per + np.arange(per)
        sw = hw + h * HEAD_PAD + ROPE_LANE
        src[sw:sw + HALF] = h * per + QK_NOPE + HALF + np.arange(HALF)
        sgn[sw:sw + HALF] = -1.0
        src[sw + HALF:sw + QK_ROPE] = h * per + QK_NOPE + np.arange(HALF)
    return src, sgn


def _runs(idx, sgn):
    out, lo = [], 0
    for pos in range(1, len(idx) + 1):
        if pos == len(idx) or not (
                (idx[pos] == -1 and idx[pos - 1] == -1)
                or (idx[pos - 1] >= 0 and idx[pos] == idx[pos - 1] + 1 and sgn[pos] == sgn[pos - 1])):
            out.append((lo, pos, int(idx[lo]), float(sgn[lo])))
            lo = pos
    return out


def _select_cols(w, idx, sgn):
    pieces = []
    for lo, hi, s0, sg in _runs(idx, sgn):
        if s0 < 0:
            pieces.append(jnp.zeros((w.shape[0], hi - lo), w.dtype))
        else:
            piece = w[:, s0:s0 + hi - lo]
            pieces.append(piece if sg > 0 else -piece)
    return jnp.concatenate(pieces, axis=1)


def _permute_cols(w, src, sgn):
    return _select_cols(w, src, sgn)


def _unpermute_full(dw, src, sgn, n_cols):
    first = np.full((n_cols,), -1, np.int64)
    second = np.full((n_cols,), -1, np.int64)
    for pos, s in enumerate(src):
        if s < 0:
            continue
        if first[s] < 0:
            first[s] = pos
        else:
            second[s] = pos
    sg2 = np.where(second >= 0, sgn[np.maximum(second, 0)], 1.0)
    return _select_cols(dw, first, sgn[first]), _select_cols(dw, second, sg2)


def _blockdiag(w):
    z = jnp.zeros_like(w[0])
    return jnp.concatenate([jnp.concatenate([w[0], z], axis=1), jnp.concatenate([z, w[1]], axis=1)], axis=0)


def _rope_tables(pos):
    inv = 1.0 / (ROPE_THETA ** (jnp.arange(0, QK_ROPE, 2, dtype=F32) / QK_ROPE))
    ang = pos.astype(F32)[:, None] * inv[None, :]
    cos, sin = jnp.cos(ang), jnp.sin(ang)
    pad = lambda t, fill: jnp.concatenate(
        [jnp.full((t.shape[0], ROPE_LANE), fill, F32), t, t, jnp.full((t.shape[0], HEAD_PAD - ROPE_LANE - QK_ROPE), fill, F32)], axis=1)
    return pad(cos, 1.0), pad(sin, 0.0)


def kernel(x, positions, attn_norm_g, w_in, gate_b, q_norm_g, w_uq, kv_norm_g, w_ukv, sg_ln_g, sg_ln_b, sg_w, sg_b, rw_mu, rw_w0, rw_w2, rw_a0, rw_a2, rw_g2, rw_k_k, rw_k_a, rw_r_k, rw_ln_g, rw_ln_b, w_branch, w_out, ffn_norm_g, w_ffn_gate, w_ffn_up, w_ffn_down, final_norm_g, loss_target, m_attn_norm_g, m_w_in, m_gate_b, m_q_norm_g, m_w_uq, m_kv_norm_g, m_w_ukv, m_sg_ln_g, m_sg_ln_b, m_sg_w, m_sg_b, m_rw_mu, m_rw_w0, m_rw_w2, m_rw_a0, m_rw_a2, m_rw_g2, m_rw_k_k, m_rw_k_a, m_rw_r_k, m_rw_ln_g, m_rw_ln_b, m_w_branch, m_w_out, m_ffn_norm_g, m_w_ffn_gate, m_w_ffn_up, m_w_ffn_down, m_final_norm_g, v_attn_norm_g, v_w_in, v_gate_b, v_q_norm_g, v_w_uq, v_kv_norm_g, v_w_ukv, v_sg_ln_g, v_sg_ln_b, v_sg_w, v_sg_b, v_rw_mu, v_rw_w0, v_rw_w2, v_rw_a0, v_rw_a2, v_rw_g2, v_rw_k_k, v_rw_k_a, v_rw_r_k, v_rw_ln_g, v_rw_ln_b, v_w_branch, v_w_out, v_ffn_norm_g, v_w_ffn_gate, v_w_ffn_up, v_w_ffn_down, v_final_norm_g):
    args = locals()
    W = {n: args[n] for n in WEIGHTS}
    M1 = {n: args['m_' + n] for n in WEIGHTS}
    M2 = {n: args['v_' + n] for n in WEIGHTS}
    B, S, D = x.shape
    N = B * S
    TM = _pick(N, (256, 128))
    TMH = 128
    TQ = _pick(S, (512, 256, 128))

    shard_shapes = [W[n].shape for n in SHARDED]
    full = {}
    mm_pack = _pack([W[n].astype(BF16) for n in MATMUL_SHARDED], 32)
    Rm = mm_pack.shape[0]
    gathered = gather_shards(mm_pack.reshape(2, Rm // 2, PACK_C)).reshape(4, Rm, PACK_C)
    pieces = [_unpack(gathered[q], [W[n].shape for n in MATMUL_SHARDED]) for q in range(4)]
    for i, n in enumerate(MATMUL_SHARDED):
        full[n] = jnp.concatenate([pieces[q][i] for q in range(4)], axis=SHARD_AXIS[n])
    small = gather_all("gather_small", _pack([W[n] for n in SMALL_SHARDED], 8))
    pieces = [_unpack(small[2 * q], [W[n].shape for n in SMALL_SHARDED]) for q in range(4)]
    for i, n in enumerate(SMALL_SHARDED):
        full[n] = jnp.concatenate([pieces[q][i] for q in range(4)], axis=SHARD_AXIS[n])
    for n in ('rw_w2', 'rw_a2', 'rw_g2'):
        full[n] = full[n].astype(F32)
    for n in REPLICATED:
        full[n] = W[n]

    win_src, win_sgn = _win_layout()
    wuq_src, wuq_sgn = _wuq_layout()
    ones = jnp.asarray(np.kron(np.eye(RWKV_HEADS), np.ones((RWKV_HEAD, RWKV_HEAD))), BF16)
    ct, st = _rope_tables(positions.reshape(N))
    row = lambda v: v.reshape(1, -1)

    H, HD = RWKV_HEADS, RWKV_HEAD
    inst = 2 * B * H
    isplit = LANES // inst
    Ip = HD // isplit
    to_j = lambda nm, c: to_scan(nm, c, B, S, c.shape[1] // RWKV_DIM, None)
    to_i = lambda nm, c: to_scan(nm, c, B, S, 1, Ip)

    def shift_prev(z):
        z = z.reshape(B, S, -1)
        return jnp.pad(z[:, :-1], ((0, 0), (1, 0), (0, 0))).reshape(N, -1)

    def shift_next(z):
        z = z.reshape(B, S, -1)
        return jnp.pad(z[:, 1:], ((0, 0), (0, 1), (0, 0))).reshape(N, -1)

    LW = []
    for l in range(DEPTH):
        wb = full['w_branch'][l]
        wb0 = jnp.zeros((MLA_HEADS, HEAD_PAD, D), F32).at[:, QK_NOPE:].set(wb[0].reshape(MLA_HEADS, V_HEAD, D))
        LW.append(dict(
            attn_g=row(full['attn_norm_g'][l]),
            w_in=_permute_cols(full['w_in'][l], win_src, win_sgn),
            gate_b=row(full['gate_b'][l]),
            q_g=row(full['q_norm_g'][l]),
            w_uq=_permute_cols(full['w_uq'][l], wuq_src, wuq_sgn),
            kv_g=row(full['kv_norm_g'][l]),
            w_ukv=full['w_ukv'][l],
            sg_g=row(full['sg_ln_g'][l]), sg_b=row(full['sg_ln_b'][l]), sg_w=full['sg_w'][l],
            sg_bias=jnp.repeat(full['sg_b'][l].T, SG_DIM // SG_GROUPS, axis=1),
            mu=row(full['rw_mu'][l]), w0=row(full['rw_w0'][l]), w2=_blockdiag(full['rw_w2'][l]),
            a0=row(full['rw_a0'][l]), a2=_blockdiag(full['rw_a2'][l]), g2=full['rw_g2'][l],
            k_k=row(full['rw_k_k'][l]), k_a=row(full['rw_k_a'][l]), r_k=row(full['rw_r_k'][l]),
            ln_g=row(full['rw_ln_g'][l]), ln_b=row(full['rw_ln_b'][l]),
            wb0=wb0.reshape(MLA_HEADS * HEAD_PAD, D), wb1=wb[1], wb2=wb[2],
            w_out=full['w_out'][l], ffn_g=row(full['ffn_norm_g'][l]),
            w_gu=jnp.concatenate([full['w_ffn_gate'][l], full['w_ffn_up'][l]], axis=1),
            w_down=full['w_ffn_down'][l]))

    saved = []
    xc = x.reshape(N, D)
    for l in range(DEPTH):
        p = LW[l]
        t = 'l%d_' % l
        sv = dict(x=xc)
        (h,) = rowwise(t + 'attn_norm', f_rms, [xc], [p['attn_g']], [], [D], TM, [BF16])
        p_q, p_ckv, p_kr, p_sg, z, p_gate = [
            matmul(t + 'in_proj_' + sn, h, p['w_in'][:, a:b], 'nn') for sn, a, b in IN_SEGMENTS]
        sv['h'] = h
        (cq,) = rowwise(t + 'q_norm', f_rms, [p_q], [p['q_g']], [], [Q_LORA], TM, [BF16])
        (ckv,) = rowwise(t + 'kv_norm', f_rms, [p_ckv], [p['kv_g']], [], [KV_LORA], TM, [BF16])
        qq = matmul(t + 'uq', cq, p['w_uq'], 'nn')
        kv = matmul(t + 'ukv', ckv, p['w_ukv'], 'nn')
        qh, kh = rowwise(t + 'rope', f_rope, [qq, kv, p_kr, ct, st], [], [], [MLA_HEADS * HEAD_PAD] * 2, TM)
        ya = attention_fwd(t + 'attn', qh, kh, kv, B, S, TQ, BF16)
        sv.update(p_q=p_q, p_ckv=p_ckv, p_kr=p_kr, cq=cq, ckv=ckv, qq=qq, kv=kv, qh=qh, kh=kh, ya=ya)
        (yb,) = rowwise(t + 'sg', f_sg, [p_sg], [p['sg_g'], p['sg_b'], p['sg_w'], p['sg_bias']], [], [SG_DIM], TM,
                        [BF16])
        sv.update(p_sg=p_sg, yb=yb)
        zp, zn = shift_prev(z), shift_next(z)
        rw_par = [p['mu'], p['w0'], p['w2'], p['a0'], p['a2'], p['g2'], p['k_k'], p['k_a']]
        r_, v_, decay, kdir, kk, bdir, g_ = rowwise(
            t + 'rw_pre', f_rw_pre, [z, zp, zn], rw_par, [ones],
            [RWKV_DIM, RWKV_DIM, 2 * RWKV_DIM, 2 * RWKV_DIM, RWKV_DIM, 2 * RWKV_DIM, RWKV_DIM], TM)
        sc = dict(w=to_j(t + 'lay_w', decay), k=to_j(t + 'lay_k', kdir), b=to_j(t + 'lay_b', bdir),
                  kk=to_j(t + 'lay_kk', kk), r=to_j(t + 'lay_r', r_), v=to_i(t + 'lay_v', v_))
        y_f, y_r, sp, sa_all, s_last = scan_fwd(t + 'scan', sc['w'], sc['k'], sc['b'], sc['kk'], sc['r'], sc['v'])
        ysum = from_scan(t + 'lay_y', y_f, y_r, B, S, 1, True)
        (yc,) = rowwise(t + 'rw_post', f_rw_post, [ysum, r_, v_, kdir, g_], [p['ln_g'], p['ln_b'], p['r_k']],
                        [ones], [RWKV_DIM], TM, [BF16])
        sv.update(z=z, zp=zp, zn=zn, r=r_, v=v_, kdir=kdir, g=g_, sc=sc, sp=sp, sa=sa_all, s_last=s_last,
                  ysum=ysum, yc=yc)
        b0 = matmul(t + 'br0', ya, p['wb0'], 'nn')
        b1 = matmul(t + 'br1', yb, p['wb1'], 'nn')
        b2 = matmul(t + 'br2', yc, p['wb2'], 'nn')
        (merged,) = rowwise(t + 'merge', f_merge, [p_gate, b0, b1, b2], [p['gate_b']], [], [D], TM, [BF16])
        x2 = matmul(t + 'out_proj', merged, p['w_out'], 'nn', add=xc)
        sv.update(p_gate=p_gate, b0=b0, b1=b1, b2=b2, merged=merged, x2=x2)
        (h2,) = rowwise(t + 'ffn_norm', f_rms, [x2], [p['ffn_g']], [], [D], TM, [BF16])
        au = matmul(t + 'ffn_in', h2, p['w_gu'], 'nn')
        (act,) = rowwise(t + 'swiglu', f_swiglu, [au], [], [], [D_FF], TM, [BF16])
        xc = matmul(t + 'ffn_out', act, p['w_down'], 'nn', add=x2)
        sv.update(h2=h2, au=au, act=act)
        saved.append(sv)

    loss_part, dx, d_final_g = loss_head(xc, loss_target.reshape(N, D), row(full['final_norm_g']), TM)
    loss = lax.psum(loss_part[0, 0], ("x", "y", "c"))

    G = {n: [None] * DEPTH for n in WEIGHTS if n != 'final_norm_g'}
    for l in reversed(range(DEPTH)):
        p, sv = LW[l], saved[l]
        t = 'l%d_bwd_' % l
        d_act = matmul(t + 'ffn_out_dx', dx, p['w_down'], 'nt')
        G['w_ffn_down'][l] = matmul(t + 'ffn_out_dw', sv['act'], dx, 'tn')
        (d_au,), _ = rowwise_bwd(t + 'swiglu', f_swiglu, [sv['au']], [], [], [[d_act]], TM, drow_dtypes=[BF16])
        d_h2 = matmul(t + 'ffn_in_dx', d_au, p['w_gu'], 'nt')
        d_wgu = matmul(t + 'ffn_in_dw', sv['h2'], d_au, 'tn')
        G['w_ffn_gate'][l], G['w_ffn_up'][l] = d_wgu[:, :D_FF], d_wgu[:, D_FF:]
        (dx2,), (dg,) = rowwise_bwd(t + 'ffn_norm', f_rms, [sv['x2']], [p['ffn_g']], [], [[d_h2]], TM, extra=[(0, dx)])
        G['ffn_norm_g'][l] = dg.reshape(-1)
        d_merged = matmul(t + 'out_proj_dx', dx2, p['w_out'], 'nt')
        G['w_out'][l] = matmul(t + 'out_proj_dw', sv['merged'], dx2, 'tn')
        (d_pgate, d_b0, d_b1, d_b2), (d_gate_b,) = rowwise_bwd(
            t + 'merge', f_merge, [sv['p_gate'], sv['b0'], sv['b1'], sv['b2']], [p['gate_b']], [], [[d_merged]], TM,
            drow_dtypes=[BF16] * 4)
        G['gate_b'][l] = d_gate_b.reshape(3, D)
        d_ya = matmul(t + 'br0_dx', d_b0, p['wb0'], 'nt')
        d_yb = matmul(t + 'br1_dx', d_b1, p['wb1'], 'nt')
        d_yc = matmul(t + 'br2_dx', d_b2, p['wb2'], 'nt')
        d_wb0 = matmul(t + 'br0_dw', sv['ya'], d_b0, 'tn').reshape(MLA_HEADS, HEAD_PAD, D)[:, QK_NOPE:].reshape(-1, D)
        G['w_branch'][l] = jnp.stack([d_wb0, matmul(t + 'br1_dw', sv['yb'], d_b1, 'tn'),
                                      matmul(t + 'br2_dw', sv['yc'], d_b2, 'tn')])
        (d_y, d_r1, d_v1, d_kdir1, d_g), (d_ln_g, d_ln_b, d_r_k) = rowwise_bwd(
            t + 'rw_post', f_rw_post, [sv['ysum'], sv['r'], sv['v'], sv['kdir'], sv['g']],
            [p['ln_g'], p['ln_b'], p['r_k']], [ones], [[d_yc]], TM)
        G['rw_ln_g'][l], G['rw_ln_b'][l] = d_ln_g.reshape(-1), d_ln_b.reshape(-1)
        G['rw_r_k'][l] = d_r_k.reshape(RWKV_HEADS, RWKV_HEAD)
        sc = sv['sc']
        res = scan_bwd(t + 'scan', sc['w'], sc['k'], sc['b'], sc['kk'], sc['r'], sc['v'], sv['sp'], sv['sa'],
                       sv['s_last'], to_i(t + 'lay_dy', d_y))
        s_dw, s_dk, s_db, s_dkk, s_dr, s_dv = [
            from_scan(t + 'lay_' + nm, res[2 * i], res[2 * i + 1], B, S, nd, nm == 'dv')
            for i, (nm, nd) in enumerate((('dw', 2), ('dk', 2), ('db', 2), ('dkk', 1), ('dr', 1), ('dv', 1)))]
        rw_par = [p['mu'], p['w0'], p['w2'], p['a0'], p['a2'], p['g2'], p['k_k'], p['k_a']]
        d_outs = [[d_r1, s_dr], [d_v1, s_dv], [s_dw], [d_kdir1, s_dk], [s_dkk], [s_db], [d_g]]
        (d_z, d_zp, d_zn), d_rw = rowwise_bwd(
            t + 'rw_pre', f_rw_pre, [sv['z'], sv['zp'], sv['zn']], rw_par, [ones], d_outs, TMH)
        (d_prw,) = rowwise(t + 'shift_sum', f_add3, [d_z, shift_next(d_zp), shift_prev(d_zn)], [], [], [RWKV_IN], TM,
                           [BF16])
        G['rw_mu'][l] = d_rw[0].reshape(-1)
        G['rw_w0'][l] = d_rw[1].reshape(2, RWKV_DIM)
        G['rw_w2'][l] = jnp.stack([d_rw[2][:64, :RWKV_DIM], d_rw[2][64:, RWKV_DIM:]])
        G['rw_a0'][l] = d_rw[3].reshape(2, RWKV_DIM)
        G['rw_a2'][l] = jnp.stack([d_rw[4][:64, :RWKV_DIM], d_rw[4][64:, RWKV_DIM:]])
        G['rw_g2'][l] = d_rw[5]
        G['rw_k_k'][l], G['rw_k_a'][l] = d_rw[6].reshape(-1), d_rw[7].reshape(-1)
        (d_psg,), (d_sg_g, d_sg_b, d_sg_w, d_sg_bias) = rowwise_bwd(
            t + 'sg', f_sg, [sv['p_sg']], [p['sg_g'], p['sg_b'], p['sg_w'], p['sg_bias']], [], [[d_yb]], TMH,
            drow_dtypes=[BF16])
        G['sg_ln_g'][l], G['sg_ln_b'][l], G['sg_w'][l] = d_sg_g.reshape(-1), d_sg_b.reshape(-1), d_sg_w
        G['sg_b'][l] = d_sg_bias.reshape(SG_CHUNK, SG_GROUPS, SG_DIM // SG_GROUPS).sum(-1).T
        d_qh, d_kh, d_kvv = attention_bwd(t + 'attn', sv['qh'], sv['kh'], sv['kv'], d_ya, B, S, TQ)
        (d_qq, d_kv, d_pkr), _ = rowwise_bwd(
            t + 'rope', f_rope, [sv['qq'], sv['kv'], sv['p_kr'], ct, st], [], [], [[d_qh], [d_kh]], TM,
            n_row_diff=3, extra=[(1, d_kvv)], drow_dtypes=[BF16] * 3)
        d_cq = matmul(t + 'uq_dx', d_qq, p['w_uq'], 'nt')
        d_wuq = matmul(t + 'uq_dw', sv['cq'], d_qq, 'tn')
        g1, g2_ = _unpermute_full(d_wuq, wuq_src, wuq_sgn, MLA_HEADS * (QK_NOPE + QK_ROPE))
        G['w_uq'][l] = g1 + g2_
        d_ckv = matmul(t + 'ukv_dx', d_kv, p['w_ukv'], 'nt')
        G['w_ukv'][l] = matmul(t + 'ukv_dw', sv['ckv'], d_kv, 'tn')
        (d_pq,), (dg,) = rowwise_bwd(t + 'q_norm', f_rms, [sv['p_q']], [p['q_g']], [], [[d_cq]], TM,
                                     drow_dtypes=[BF16])
        G['q_norm_g'][l] = dg.reshape(-1)
        (d_pckv,), (dg,) = rowwise_bwd(t + 'kv_norm', f_rms, [sv['p_ckv']], [p['kv_g']], [], [[d_ckv]], TM,
                                       drow_dtypes=[BF16])
        G['kv_norm_g'][l] = dg.reshape(-1)
        d_h, d_cols = None, []
        for (sn, a, b), d_seg in zip(IN_SEGMENTS, [d_pq, d_pckv, d_pkr, d_psg, d_prw, d_pgate]):
            d_h = matmul(t + 'in_proj_dx_' + sn, d_seg, p['w_in'][:, a:b], 'nt', add=d_h)
            d_cols.append(matmul(t + 'in_proj_dw_' + sn, sv['h'], d_seg, 'tn'))
        d_win = jnp.concatenate(d_cols, axis=1)
        g1, g2_ = _unpermute_full(d_win, win_src, win_sgn, N_IN)
        kr0 = Q_LORA + KV_LORA
        G['w_in'][l] = g1.at[:, kr0:kr0 + QK_ROPE].add(g2_[:, kr0:kr0 + QK_ROPE])
        (dx,), (dg,) = rowwise_bwd(t + 'attn_norm', f_rms, [sv['x']], [p['attn_g']], [], [[d_h]], TM, extra=[(0, dx2)])
        G['attn_norm_g'][l] = dg.reshape(-1)

    grads = {n: jnp.stack(G[n]) for n in G}
    grads['final_norm_g'] = d_final_g.reshape(-1)
    grad_x = dx.reshape(B, S, D)

    per_shard = []
    for q in range(4):
        sl = []
        for n in SHARDED:
            ax = SHARD_AXIS[n]
            w = W[n].shape[ax]
            sl.append(lax.slice_in_dim(grads[n], q * w, (q + 1) * w, axis=ax).astype(BF16))
        per_shard.append(_pack(sl, 2 * JOIN_ROWS))
    R = per_shard[0].shape[0]
    gpack = jnp.stack(per_shard).astype(BF16).reshape(4, 2, R // 2, PACK_C)
    both = sum_join(scatter_partials(gpack))
    g_shard = dict(zip(SHARDED, _unpack(both.reshape(R, PACK_C), shard_shapes)))
    rep_shapes = [W[n].shape for n in REPLICATED]
    rpack = _pack([grads[n] for n in REPLICATED], 8)
    g_rep = sum_slots("sum_replicated", gather_all("gather_replicated", rpack))

    outs = {}
    for n in MATMUL_SHARDED:
        shp = W[n].shape
        two = lambda a: a.reshape(-1, shp[-1])
        res = adamw("adamw_" + n, two(W[n]), two(g_shard[n]), two(M1[n]), two(M2[n]))
        outs['grad', n] = g_shard[n]
        for key, a in zip(('delta', 'new_m', 'new_v'), res):
            outs[key, n] = a.reshape(shp)
    small = SMALL_SHARDED + REPLICATED
    small_shapes = [W[n].shape for n in small]
    g_small = [g_shard[n] for n in SMALL_SHARDED] + _unpack(g_rep, rep_shapes)
    res = adamw("adamw_small", _pack([W[n] for n in small], 8), _pack(g_small, 8),
                _pack([M1[n] for n in small], 8), _pack([M2[n] for n in small], 8))
    for n, a in zip(small, g_small):
        outs['grad', n] = a
    for key, buf in zip(('delta', 'new_m', 'new_v'), res):
        for n, a in zip(small, _unpack(buf, small_shapes)):
            outs[key, n] = a
    return (loss, grad_x, *[outs['grad', n] for n in WEIGHTS], *[outs['delta', n] for n in WEIGHTS],
            *[outs['new_m', n] for n in WEIGHTS], *[outs['new_v', n] for n in WEIGHTS])
```

```python
import functools
import math

import numpy as np
import jax
import jax.numpy as jnp
from jax import lax
from jax.experimental import pallas as pl
from jax.experimental.pallas import tpu as pltpu

F32 = jnp.float32
BF16 = jnp.bfloat16

DEPTH = 2
MLA_HEADS = 8
Q_LORA = 384
KV_LORA = 256
QK_NOPE = 64
QK_ROPE = 32
V_HEAD = 64
ROPE_THETA = 10000.0
SG_GROUPS = 8
SG_DIM = 512
SG_CHUNK = 128
RWKV_HEADS = 8
RWKV_HEAD = 64
RWKV_DIM = 512
GN_EPS = 64e-5
NORM_EPS = 1e-6
D_FF = 2816
RWKV_IN = 1920
N_IN = 6688
ADAM_LR, ADAM_B1, ADAM_B2, ADAM_EPS, ADAM_WD, ADAM_STEP = 0.001, 0.9, 0.999, 1e-08, 0.01, 10

LANES = 128
HEAD_PAD = 128
VMEM_LIMIT = 56 * 1024 * 1024
MESH = pl.DeviceIdType.MESH

WEIGHTS = ['attn_norm_g', 'w_in', 'gate_b', 'q_norm_g', 'w_uq', 'kv_norm_g', 'w_ukv', 'sg_ln_g', 'sg_ln_b', 'sg_w',
           'sg_b', 'rw_mu', 'rw_w0', 'rw_w2', 'rw_a0', 'rw_a2', 'rw_g2', 'rw_k_k', 'rw_k_a', 'rw_r_k', 'rw_ln_g',
           'rw_ln_b', 'w_branch', 'w_out', 'ffn_norm_g', 'w_ffn_gate', 'w_ffn_up', 'w_ffn_down', 'final_norm_g']
SHARD_AXIS = {'w_in': 2, 'gate_b': 2, 'w_uq': 2, 'w_ukv': 2, 'rw_w0': 2, 'rw_w2': 3, 'rw_a0': 2, 'rw_a2': 3,
              'rw_g2': 2, 'w_branch': 3, 'w_out': 1, 'w_ffn_gate': 2, 'w_ffn_up': 2, 'w_ffn_down': 1}
SHARDED = [n for n in WEIGHTS if n in SHARD_AXIS]
REPLICATED = [n for n in WEIGHTS if n not in SHARD_AXIS]
SMALL_SHARDED = ['gate_b', 'rw_w0', 'rw_a0']
MATMUL_SHARDED = [n for n in SHARDED if n not in SMALL_SHARDED]


def _params(sem=None):
    return pltpu.CompilerParams(dimension_semantics=sem, vmem_limit_bytes=VMEM_LIMIT)


def _pick(n, cands):
    for c in cands:
        if n % c == 0:
            return c
    return n


def _dot(a, b, dims):
    return lax.dot_general(a.astype(BF16), b.astype(BF16), (dims, ((), ())), preferred_element_type=F32)


def _nn(a, b):
    return _dot(a, b, ((1,), (0,)))


def _nt(a, b):
    return _dot(a, b, ((1,), (1,)))


def _tn(a, b):
    return _dot(a, b, ((0,), (0,)))


@jax.custom_vjp
def mm(a, b):
    return _nn(a, b)


mm.defvjp(lambda a, b: (_nn(a, b), (a, b)), lambda res, g: (_nt(g, res[1]), _tn(res[0], g)))


@jax.custom_vjp
def mm_nt(a, b):
    return _nt(a, b)


mm_nt.defvjp(lambda a, b: (_nt(a, b), (a, b)), lambda res, g: (_nn(g, res[1]), _tn(g, res[0])))


def _seg_raw(x, ones):
    hi = x.astype(BF16)
    lo = (x - hi.astype(F32)).astype(BF16)
    d = (((1,), (0,)), ((), ()))
    return (lax.dot_general(hi, ones, d, preferred_element_type=F32)
            + lax.dot_general(lo, ones, d, preferred_element_type=F32))


@jax.custom_vjp
def segsum(x, ones):
    return _seg_raw(x, ones)


segsum.defvjp(lambda x, ones: (_seg_raw(x, ones), ones),
              lambda ones, g: (_seg_raw(g, ones), jnp.zeros_like(ones)))


def _sigmoid(x):
    return 0.5 * (jnp.tanh(0.5 * x) + 1.0)


def _rms(x, g):
    return x * lax.rsqrt(jnp.mean(x * x, axis=-1, keepdims=True) + NORM_EPS) * g


def matmul(name, a, b, mode, add=None, out_dtype=F32):
    if mode == 'nn':
        (M, K), (_, N) = a.shape, b.shape
    elif mode == 'nt':
        (M, K), (N, _) = a.shape, b.shape
    else:
        (K, M), (_, N) = a.shape, b.shape
    tm = _pick(M, (1408, 1024, 512, 384, 256, 128))
    tn = _pick(N, (1408, 1024, 768, 512, 384, 256, 128))
    tk = _pick(K, (1024, 512, 384, 256, 128))
    nk = K // tk
    dims = {'nn': ((1,), (0,)), 'nt': ((1,), (1,)), 'tn': ((0,), (0,))}[mode]
    a_spec = pl.BlockSpec((tk, tm), lambda i, j, k: (k, i)) if mode == 'tn' else pl.BlockSpec((tm, tk), lambda i, j, k: (i, k))
    b_spec = pl.BlockSpec((tn, tk), lambda i, j, k: (j, k)) if mode == 'nt' else pl.BlockSpec((tk, tn), lambda i, j, k: (k, j))
    o_spec = pl.BlockSpec((tm, tn), lambda i, j, k: (i, j))
    has_add = add is not None

    def body(*refs):
        if has_add:
            a_ref, b_ref, add_ref, o_ref, acc = refs
        else:
            a_ref, b_ref, o_ref, acc = refs
        k = pl.program_id(2)

        @pl.when(k == 0)
        def _():
            acc[...] = jnp.zeros_like(acc)

        acc[...] += _dot(a_ref[...], b_ref[...], dims)

        @pl.when(k == nk - 1)
        def _():
            o_ref[...] = (acc[...] + add_ref[...] if has_add else acc[...]).astype(o_ref.dtype)

    ins = [a, b] + ([add] if has_add else [])
    specs = [a_spec, b_spec] + ([o_spec] if has_add else [])
    return pl.pallas_call(
        body, name=name, grid=(M // tm, N // tn, nk), in_specs=specs, out_specs=o_spec,
        out_shape=jax.ShapeDtypeStruct((M, N), out_dtype), scratch_shapes=[pltpu.VMEM((tm, tn), F32)],
        compiler_params=_params(("parallel", "parallel", "arbitrary")))(*ins)


def _full_spec(p):
    nd = p.ndim
    return pl.BlockSpec(p.shape, lambda i, _nd=nd: (0,) * _nd)


def rowwise(name, fn, rows, params, consts, out_widths, tm, out_dtypes=None):
    N = rows[0].shape[0]
    nr, npar, nc = len(rows), len(params), len(consts)

    def body(*refs):
        vals = [r[...] for r in refs[:nr + npar + nc]]
        res = fn(*vals)
        for o, v in zip(refs[nr + npar + nc:], res):
            o[...] = v.astype(o.dtype)

    in_specs = ([pl.BlockSpec((tm, r.shape[1]), lambda i: (i, 0)) for r in rows]
                + [_full_spec(p) for p in list(params) + list(consts)])
    out_specs = [pl.BlockSpec((tm, w), lambda i: (i, 0)) for w in out_widths]
    return pl.pallas_call(
        body, name=name, grid=(N // tm,), in_specs=in_specs, out_specs=out_specs,
        out_shape=[jax.ShapeDtypeStruct((N, w), d) for w, d in zip(out_widths, out_dtypes or [F32] * len(out_widths))],
        compiler_params=_params(("parallel",)))(*rows, *params, *consts)


def rowwise_bwd(name, fn, rows, params, consts, d_outs, tm, n_row_diff=None, extra=(), drow_dtypes=None):
    N = rows[0].shape[0]
    nr, npar, nc = len(rows), len(params), len(consts)
    nd = nr if n_row_diff is None else n_row_diff
    counts = [len(p) for p in d_outs]
    flat_d = [a for parts in d_outs for a in parts]
    nflat, nex = len(flat_d), len(extra)

    def body(*refs):
        pos = 0
        row_v = [r[...] for r in refs[pos:pos + nr]]; pos += nr
        par_v = [r[...] for r in refs[pos:pos + npar]]; pos += npar
        con_v = [r[...] for r in refs[pos:pos + nc]]; pos += nc
        d_refs = refs[pos:pos + nflat]; pos += nflat
        ex_refs = refs[pos:pos + nex]; pos += nex
        drow_refs = refs[pos:pos + nd]; pos += nd
        dpar_refs = refs[pos:pos + npar]

        def f(*diff):
            return fn(*diff[:nd], *row_v[nd:], *diff[nd:], *con_v)

        _, vjp = jax.vjp(f, *row_v[:nd], *par_v)
        cts, q = [], 0
        for c in counts:
            g = d_refs[q][...].astype(F32)
            for t in range(1, c):
                g = g + d_refs[q + t][...].astype(F32)
            cts.append(g)
            q += c
        grads = vjp(tuple(cts))
        drow = list(grads[:nd])
        for (idx, _), r in zip(extra, ex_refs):
            drow[idx] = drow[idx] + r[...].astype(F32)
        for o, v in zip(drow_refs, drow):
            o[...] = v.astype(o.dtype)

        @pl.when(pl.program_id(0) == 0)
        def _():
            for o in dpar_refs:
                o[...] = jnp.zeros_like(o)

        for o, v in zip(dpar_refs, grads[nd:]):
            o[...] += v

    ex_arrs = [a for _, a in extra]
    in_specs = ([pl.BlockSpec((tm, r.shape[1]), lambda i: (i, 0)) for r in rows]
                + [_full_spec(p) for p in list(params) + list(consts)]
                + [pl.BlockSpec((tm, a.shape[1]), lambda i: (i, 0)) for a in flat_d + ex_arrs])
    out_specs = ([pl.BlockSpec((tm, r.shape[1]), lambda i: (i, 0)) for r in rows[:nd]]
                 + [_full_spec(p) for p in params])
    out_shape = ([jax.ShapeDtypeStruct(r.shape, d) for r, d in zip(rows[:nd], drow_dtypes or [F32] * nd)]
                 + [jax.ShapeDtypeStruct(p.shape, F32) for p in params])
    res = pl.pallas_call(
        body, name=name, grid=(N // tm,), in_specs=in_specs, out_specs=out_specs, out_shape=out_shape,
        compiler_params=_params(("arbitrary",)))(*rows, *params, *consts, *flat_d, *ex_arrs)
    return list(res[:nd]), list(res[nd:])


def f_rms(x, g):
    return (_rms(x, g),)


def f_rope(qq, kv, krr, ct, st):
    hw = MLA_HEADS * HEAD_PAD
    c8 = jnp.tile(ct, (1, MLA_HEADS))
    s8 = jnp.tile(st, (1, MLA_HEADS))
    q = qq[:, :hw] * c8 + qq[:, hw:] * s8
    kr = krr[:, :HEAD_PAD] * ct + krr[:, HEAD_PAD:] * st
    lane = lax.broadcasted_iota(jnp.int32, kv.shape, 1) % HEAD_PAD
    k = jnp.where(lane < QK_NOPE, kv, jnp.tile(kr, (1, MLA_HEADS)))
    return q, k


def f_sg(p, ln_g, ln_b, w, bias):
    z = 0.5 * p * (1.0 + jnp.tanh(0.7978845608028654 * (p + 0.044715 * p * p * p)))
    u, v = z[:, :SG_DIM], z[:, SG_DIM:]
    mu = jnp.mean(v, axis=-1, keepdims=True)
    var = jnp.mean(jnp.square(v - mu), axis=-1, keepdims=True)
    v = (v - mu) * lax.rsqrt(var + 1e-5) * ln_g + ln_b
    lane = lax.broadcasted_iota(jnp.int32, (SG_CHUNK, LANES), 1)
    outs = []
    for c in range(p.shape[0] // SG_CHUNK):
        vc = v[c * SG_CHUNK:(c + 1) * SG_CHUNK]
        cols = []
        for m in range(SG_DIM // LANES):
            blk = vc[:, m * LANES:(m + 1) * LANES]
            cols.append(jnp.where(lane < 64, mm(w[2 * m], blk), mm(w[2 * m + 1], blk)))
        outs.append(jnp.concatenate(cols, axis=1) + bias)
    mixed = outs[0] if len(outs) == 1 else jnp.concatenate(outs, axis=0)
    return (u * mixed,)


def f_rw_pre(z, zp, zn, mu, w0, w2, a0, a2, g2, k_k, k_a, ones):
    z = z + mu * (0.5 * (zp + zn) - z)
    C = RWKV_DIM
    r, k, v = z[:, :C], z[:, C:2 * C], z[:, 2 * C:3 * C]
    wl, al, gl = z[:, 3 * C:3 * C + 128], z[:, 3 * C + 128:3 * C + 256], z[:, 3 * C + 256:]
    w = w0 + mm(jnp.tanh(wl), w2)
    decay = jnp.exp(-0.6065306597126334 * _sigmoid(w))
    a = _sigmoid(a0 + mm(al, a2))
    g = mm(_sigmoid(gl), g2)
    kk = k * k_k
    kk = kk / jnp.maximum(jnp.sqrt(segsum(kk * kk, ones)), 1e-12)
    k2 = jnp.concatenate([k, k], axis=1)
    kdir = k2 * (1.0 + (a - 1.0) * jnp.concatenate([k_a, k_a], axis=1))
    bdir = jnp.concatenate([kk, kk], axis=1) * a
    return r, v, decay, kdir, kk, bdir, g


def f_rw_post(y, r, v, kdir, g, ln_g, ln_b, r_k, ones):
    mean = segsum(y, ones) * (1.0 / RWKV_HEAD)
    yc = y - mean
    var = segsum(yc * yc, ones) * (1.0 / RWKV_HEAD)
    y = yc * lax.rsqrt(var + GN_EPS) * ln_g + ln_b
    C = RWKV_DIM
    bonus = segsum(r * kdir[:, :C] * r_k, ones) + segsum(r * kdir[:, C:] * r_k, ones)
    return ((y + bonus * v) * g,)


def f_merge(pg, b0, b1, b2, gate_b):
    D = b0.shape[1]
    gt = _sigmoid(pg + gate_b)
    return (gt[:, :D] * b0 + gt[:, D:2 * D] * b1 + gt[:, 2 * D:] * b2,)


def f_swiglu(au):
    a, u = au[:, :D_FF], au[:, D_FF:]
    return (a * _sigmoid(a) * u,)


def f_add3(a, b, c):
    return (a + b + c,)


def loss_head(x, tgt, g, tm):
    N, D = x.shape

    def body(x_ref, t_ref, g_ref, loss_ref, dx_ref, dg_ref):
        t = t_ref[...]

        def f(xv, gv):
            err = _rms(xv, gv) - t
            return 0.5 * jnp.sum(jnp.mean(err * err, axis=-1, keepdims=True))

        val, (dx, dg) = jax.value_and_grad(f, argnums=(0, 1))(x_ref[...], g_ref[...])
        dx_ref[...] = dx

        @pl.when(pl.program_id(0) == 0)
        def _():
            loss_ref[...] = jnp.zeros_like(loss_ref)
            dg_ref[...] = jnp.zeros_like(dg_ref)

        loss_ref[...] += jnp.full(loss_ref.shape, val, F32)
        dg_ref[...] += dg

    row = pl.BlockSpec((tm, D), lambda i: (i, 0))
    return pl.pallas_call(
        body, name="loss_head", grid=(N // tm,), in_specs=[row, row, _full_spec(g)],
        out_specs=[pl.BlockSpec((1, LANES), lambda i: (0, 0)), row, _full_spec(g)],
        out_shape=[jax.ShapeDtypeStruct((1, LANES), F32), jax.ShapeDtypeStruct((N, D), F32),
                   jax.ShapeDtypeStruct(g.shape, F32)],
        compiler_params=_params(("arbitrary",)))(x, tgt, g)


ATT_SCALE = float((QK_NOPE + QK_ROPE) ** -0.5)


def _attn_block(q, k, kv):
    s = mm_nt(q, k) * ATT_SCALE
    m = lax.stop_gradient(jnp.max(s, axis=-1, keepdims=True))
    e = jnp.exp(s - m)
    return mm(e, kv) * (1.0 / jnp.sum(e, axis=-1, keepdims=True))


def attention_fwd(name, q, k, kv, B, S, tq, out_dtype):
    nq = S // tq
    qspec = pl.BlockSpec((tq, HEAD_PAD), lambda b, h, i: (b * nq + i, h))
    kspec = pl.BlockSpec((S, HEAD_PAD), lambda b, h, i: (b, h))

    def body(q_ref, k_ref, kv_ref, o_ref):
        o_ref[...] = _attn_block(q_ref[...], k_ref[...], kv_ref[...]).astype(o_ref.dtype)

    return pl.pallas_call(
        body, name=name, grid=(B, MLA_HEADS, nq), in_specs=[qspec, kspec, kspec], out_specs=qspec,
        out_shape=jax.ShapeDtypeStruct(q.shape, out_dtype),
        compiler_params=_params(("parallel", "parallel", "arbitrary")))(q, k, kv)


def attention_bwd(name, q, k, kv, do, B, S, tq):
    nq = S // tq
    qspec = pl.BlockSpec((tq, HEAD_PAD), lambda b, h, i: (b * nq + i, h))
    kspec = pl.BlockSpec((S, HEAD_PAD), lambda b, h, i: (b, h))

    def body(q_ref, k_ref, kv_ref, do_ref, dq_ref, dk_ref, dkv_ref):
        _, vjp = jax.vjp(_attn_block, q_ref[...], k_ref[...], kv_ref[...])
        dq, dk, dkv = vjp(do_ref[...])
        dq_ref[...] = dq

        @pl.when(pl.program_id(2) == 0)
        def _():
            dk_ref[...] = jnp.zeros_like(dk_ref)
            dkv_ref[...] = jnp.zeros_like(dkv_ref)

        dk_ref[...] += dk
        dkv_ref[...] += dkv

    sh = jax.ShapeDtypeStruct(q.shape, F32)
    return pl.pallas_call(
        body, name=name, grid=(B, MLA_HEADS, nq), in_specs=[qspec, kspec, kspec, qspec],
        out_specs=[qspec, kspec, kspec], out_shape=[sh, sh, sh],
        compiler_params=_params(("parallel", "parallel", "arbitrary")))(q, k, kv, do)


SCAN_TC = 8
SCAN_UNROLL = 16


def _jloop(n, body, init):
    def outer(o, c):
        for u in range(SCAN_UNROLL):
            c = body(o * SCAN_UNROLL + u, c)
        return c

    return lax.fori_loop(0, n // SCAN_UNROLL, outer, init)


def _dir_mask(L, Ip):
    per_dir = L // (2 * (RWKV_HEAD // Ip))
    lane = lax.broadcasted_iota(jnp.int32, (1, L), 1)
    return (lane // per_dir) % 2 == 1


def _merge_dirs(mask, fwd_ref, rev_ref, out_ref):
    for tt in range(SCAN_TC):
        out_ref[tt] = jnp.where(mask, rev_ref[SCAN_TC - 1 - tt], fwd_ref[tt])


def scan_fwd(name, w, k, b, kk, r, v):
    T, J, L = w.shape
    Ip = v.shape[1]
    nT = T // SCAN_TC
    fwd3, rev3 = (lambda g: (g, 0, 0)), (lambda g: (nT - 1 - g, 0, 0))
    jf, jr = pl.BlockSpec((SCAN_TC, J, L), fwd3), pl.BlockSpec((SCAN_TC, J, L), rev3)
    i_f, i_r = pl.BlockSpec((SCAN_TC, Ip, L), fwd3), pl.BlockSpec((SCAN_TC, Ip, L), rev3)
    sspec = pl.BlockSpec((SCAN_TC, J, Ip, L), lambda g: (g, 0, 0, 0))
    last_spec = pl.BlockSpec((J, Ip, L), lambda g: (0, 0, 0))

    def body(wf, wr, kf, kr, bf, br, kkf, kkr, rf, rr, vf, vr, yf_ref, yr_ref, sp_ref, sa_ref, last_ref,
             s_ref, w_ref, k_ref, b_ref, kk_ref, r_ref, v_ref):
        @pl.when(pl.program_id(0) == 0)
        def _():
            s_ref[...] = jnp.zeros_like(s_ref)

        mask = _dir_mask(L, Ip)
        for f_, r_, m_ in ((wf, wr, w_ref), (kf, kr, k_ref), (bf, br, b_ref), (kkf, kkr, kk_ref), (rf, rr, r_ref),
                           (vf, vr, v_ref)):
            _merge_dirs(mask, f_, r_, m_)

        def row(ref, tt, j):
            return jnp.broadcast_to(ref[tt, pl.ds(j, 1), :], (Ip, L))

        def step(tt, carry):
            def p1(j, sa):
                s = s_ref[j]
                sp_ref[tt, j] = s
                return sa + s * row(kk_ref, tt, j)

            sa = _jloop(J, p1, jnp.zeros((Ip, L), F32))
            sa_ref[tt] = sa
            vt = v_ref[tt]

            def p2(j, y):
                s = s_ref[j] * row(w_ref, tt, j) - sa * row(b_ref, tt, j) + vt * row(k_ref, tt, j)
                s_ref[j] = s
                return y + s * row(r_ref, tt, j)

            y = _jloop(J, p2, jnp.zeros((Ip, L), F32))
            yf_ref[tt] = y
            yr_ref[SCAN_TC - 1 - tt] = y
            return carry

        lax.fori_loop(0, SCAN_TC, step, 0)

        @pl.when(pl.program_id(0) == nT - 1)
        def _():
            last_ref[...] = s_ref[...]

    ish = jax.ShapeDtypeStruct((T, Ip, L), F32)
    jscr = pltpu.VMEM((SCAN_TC, J, L), F32)
    return pl.pallas_call(
        body, name=name, grid=(nT,), in_specs=[jf, jr] * 5 + [i_f, i_r],
        out_specs=[i_f, i_r, sspec, i_f, last_spec],
        out_shape=[ish, ish, jax.ShapeDtypeStruct((T, J, Ip, L), F32), ish, jax.ShapeDtypeStruct((J, Ip, L), F32)],
        scratch_shapes=[pltpu.VMEM((J, Ip, L), F32)] + [jscr] * 5 + [pltpu.VMEM((SCAN_TC, Ip, L), F32)],
        compiler_params=_params(("arbitrary",)))(w, w, k, k, b, b, kk, kk, r, r, v, v)


def scan_bwd(name, w, k, b, kk, r, v, sp, sa_all, s_last, dy):
    T, J, L = w.shape
    Ip = v.shape[1]
    nT = T // SCAN_TC
    stp3, mir3 = (lambda g: (nT - 1 - g, 0, 0)), (lambda g: (g, 0, 0))
    jf, jr = pl.BlockSpec((SCAN_TC, J, L), stp3), pl.BlockSpec((SCAN_TC, J, L), mir3)
    i_f, i_r = pl.BlockSpec((SCAN_TC, Ip, L), stp3), pl.BlockSpec((SCAN_TC, Ip, L), mir3)
    sspec = pl.BlockSpec((SCAN_TC, J, Ip, L), lambda g: (nT - 1 - g, 0, 0, 0))
    last_spec = pl.BlockSpec((J, Ip, L), lambda g: (0, 0, 0))

    def body(wf, wr, kf, kr, bf, br, kkf, kkr, rf, rr, vf, vr, dyf, dyr, sp_ref, sa_ref, last_ref,
             dwf, dwr, dkf, dkr, dbf, dbr, dkkf, dkkr, drf, drr, dvf, dvr,
             ds_ref, nxt_ref, w_ref, k_ref, b_ref, kk_ref, r_ref, v_ref, dy_ref):
        @pl.when(pl.program_id(0) == 0)
        def _():
            ds_ref[...] = jnp.zeros_like(ds_ref)
            nxt_ref[...] = last_ref[...]

        mask = _dir_mask(L, Ip)
        for f_, r_, m_ in ((wf, wr, w_ref), (kf, kr, k_ref), (bf, br, b_ref), (kkf, kkr, kk_ref), (rf, rr, r_ref),
                           (vf, vr, v_ref), (dyf, dyr, dy_ref)):
            _merge_dirs(mask, f_, r_, m_)

        def row(ref, tt, j):
            return jnp.broadcast_to(ref[tt, pl.ds(j, 1), :], (Ip, L))

        def rsum(x):
            return jnp.sum(x, axis=0, keepdims=True)

        def make_step(first):
            def step(n, carry):
                tt = SCAN_TC - 1 - n
                dyt, vt, sa = dy_ref[tt], v_ref[tt], sa_ref[tt]

                def p1(j, c):
                    dsa, dv = c
                    ds = ds_ref[j] + dyt * row(r_ref, tt, j)
                    ds_ref[j] = ds
                    return dsa - ds * row(b_ref, tt, j), dv + ds * row(k_ref, tt, j)

                z = jnp.zeros((Ip, L), F32)
                dsa, dv = _jloop(J, p1, (z, z))
                dvf[tt] = dv
                dvr[SCAN_TC - 1 - tt] = dv

                def put(f_ref, r_ref_, j, val):
                    f_ref[tt, pl.ds(j, 1), :] = val
                    r_ref_[SCAN_TC - 1 - tt, pl.ds(j, 1), :] = val

                def p2(j, c):
                    ds = ds_ref[j]
                    s0 = sp_ref[tt, j]
                    s1 = nxt_ref[j] if first else sp_ref[tt + 1, j]
                    put(drf, drr, j, rsum(s1 * dyt))
                    put(dkf, dkr, j, rsum(ds * vt))
                    put(dbf, dbr, j, -rsum(ds * sa))
                    put(dwf, dwr, j, rsum(ds * s0))
                    put(dkkf, dkkr, j, rsum(s0 * dsa))
                    ds_ref[j] = ds * row(w_ref, tt, j) + dsa * row(kk_ref, tt, j)
                    return c

                _jloop(J, p2, 0)
                return carry

            return step

        make_step(True)(0, 0)
        lax.fori_loop(1, SCAN_TC, make_step(False), 0)
        nxt_ref[...] = sp_ref[0]

    jsh = jax.ShapeDtypeStruct((T, J, L), F32)
    ish = jax.ShapeDtypeStruct((T, Ip, L), F32)
    jscr = pltpu.VMEM((SCAN_TC, J, L), F32)
    iscr = pltpu.VMEM((SCAN_TC, Ip, L), F32)
    return pl.pallas_call(
        body, name=name, grid=(nT,), in_specs=[jf, jr] * 5 + [i_f, i_r] * 2 + [sspec, i_f, last_spec],
        out_specs=[jf, jr] * 5 + [i_f, i_r], out_shape=[jsh] * 10 + [ish] * 2,
        scratch_shapes=[pltpu.VMEM((J, Ip, L), F32)] * 2 + [jscr] * 5 + [iscr] * 2,
        compiler_params=_params(("arbitrary",)))(w, w, k, k, b, b, kk, kk, r, r, v, v, dy, dy, sp, sa_all, s_last)


LAYOUT_TT = 128
LAYOUT_U = 8


def _lane_group(L, n):
    return lax.broadcasted_iota(jnp.int32, (1, L), 1) // (L // n)


def to_scan(name, x, B, S, nd, Ip):
    isplit = LANES // (2 * B * RWKV_HEADS)
    tt_n = _pick(S, (LAYOUT_TT, 64, 16, 8))
    x5 = x.reshape(B, S, nd, RWKV_HEADS, RWKV_HEAD)
    rows_out = RWKV_HEAD if Ip is None else Ip

    def body(x_ref, o_ref):
        group = _lane_group(LANES, isplit)

        def chunk(c, carry):
            t0 = c * LAYOUT_U
            z = jnp.concatenate([x_ref[b, t0 + u, min(d, nd - 1)] for u in range(LAYOUT_U) for _ in range(isplit)
                                 for d in range(2) for b in range(B)], axis=0)
            mt = z.T
            for u in range(LAYOUT_U):
                m = mt[:, u * LANES:(u + 1) * LANES]
                if Ip is not None:
                    m = sum(jnp.where(group == i2, m[i2 * Ip:(i2 + 1) * Ip], 0.0) for i2 in range(isplit))
                o_ref[t0 + u] = m
            return carry

        lax.fori_loop(0, tt_n // LAYOUT_U, chunk, 0)

    return pl.pallas_call(
        body, name=name, grid=(S // tt_n,),
        in_specs=[pl.BlockSpec((B, tt_n, nd, RWKV_HEADS, RWKV_HEAD), lambda g: (0, g, 0, 0, 0))],
        out_specs=pl.BlockSpec((tt_n, rows_out, LANES), lambda g: (g, 0, 0)),
        out_shape=jax.ShapeDtypeStruct((S, rows_out, LANES), F32), compiler_params=_params(("parallel",)))(x5)


def from_scan(name, f, r, B, S, nd, i_indexed):
    rows_in = f.shape[1]
    isplit = LANES // (2 * B * RWKV_HEADS)
    Ip = rows_in if i_indexed else RWKV_HEAD // isplit
    tt_n = _pick(S, (LAYOUT_TT, 64, 16, 8))
    per_i2 = LANES // isplit

    def body(f_ref, r_ref, o_ref):
        mask = _dir_mask(LANES, Ip)
        group = _lane_group(LANES, isplit)

        def chunk(c, carry):
            t0 = c * LAYOUT_U
            ms = []
            for u in range(LAYOUT_U):
                m = jnp.where(mask, r_ref[t0 + u], f_ref[t0 + u])
                if i_indexed:
                    m = jnp.concatenate([jnp.where(group == i2, m, 0.0) for i2 in range(isplit)], axis=0)
                ms.append(m)
            zt = jnp.concatenate(ms, axis=1).T
            for u in range(LAYOUT_U):
                z = zt[u * LANES:(u + 1) * LANES]
                zf = sum(z[i2 * per_i2:(i2 + 1) * per_i2] for i2 in range(isplit))
                for b in range(B):
                    d0 = zf[b * RWKV_HEADS:(b + 1) * RWKV_HEADS]
                    d1 = zf[(B + b) * RWKV_HEADS:(B + b + 1) * RWKV_HEADS]
                    if nd == 1:
                        o_ref[b, t0 + u, 0] = d0 + d1
                    else:
                        o_ref[b, t0 + u, 0] = d0
                        o_ref[b, t0 + u, 1] = d1
            return carry

        lax.fori_loop(0, tt_n // LAYOUT_U, chunk, 0)

    spec = pl.BlockSpec((tt_n, rows_in, LANES), lambda g: (g, 0, 0))
    out = pl.pallas_call(
        body, name=name, grid=(S // tt_n,), in_specs=[spec, spec],
        out_specs=pl.BlockSpec((B, tt_n, nd, RWKV_HEADS, RWKV_HEAD), lambda g: (0, g, 0, 0, 0)),
        out_shape=jax.ShapeDtypeStruct((B, S, nd, RWKV_HEADS, RWKV_HEAD), F32),
        compiler_params=_params(("parallel",)))(f, r)
    return out.reshape(B * S, nd * RWKV_DIM)


def adamw(name, w, g, m, v):
    R, C = w.shape
    tr = _pick(R, (256, 128, 64, 32, 16, 8))
    c1 = 1.0 - ADAM_B1 ** ADAM_STEP
    c2 = 1.0 - ADAM_B2 ** ADAM_STEP

    def body(w_ref, g_ref, m_ref, v_ref, d_ref, nm_ref, nv_ref):
        gv = g_ref[...]
        nm = ADAM_B1 * m_ref[...] + (1.0 - ADAM_B1) * gv
        nv = ADAM_B2 * v_ref[...] + (1.0 - ADAM_B2) * jnp.square(gv)
        d_ref[...] = -ADAM_LR * ((nm / c1) / (jnp.sqrt(nv / c2) + ADAM_EPS) + ADAM_WD * w_ref[...])
        nm_ref[...] = nm
        nv_ref[...] = nv

    spec = pl.BlockSpec((tr, C), lambda i: (i, 0))
    sh = jax.ShapeDtypeStruct((R, C), F32)
    return pl.pallas_call(body, name=name, grid=(R // tr,), in_specs=[spec] * 4, out_specs=[spec] * 3,
                          out_shape=[sh] * 3, compiler_params=_params(("parallel",)))(w, g, m, v)


def sum_slots(name, x):
    n, R, C = x.shape
    tr = _pick(R, (256, 128, 64, 32, 16, 8))

    def body(x_ref, o_ref):
        acc = x_ref[0].astype(F32)
        for s in range(1, n):
            acc = acc + x_ref[s].astype(F32)
        o_ref[...] = acc

    return pl.pallas_call(
        body, name=name, grid=(R // tr,), in_specs=[pl.BlockSpec((n, tr, C), lambda i: (0, i, 0))],
        out_specs=pl.BlockSpec((tr, C), lambda i: (i, 0)), out_shape=jax.ShapeDtypeStruct((R, C), F32),
        compiler_params=_params(("parallel",)))(x)


ANY = pl.BlockSpec(memory_space=pl.ANY)


def _xyc():
    return lax.axis_index("x"), lax.axis_index("y"), lax.axis_index("c")


def gather_shards(shard):
    _, R, C = shard.shape

    def body(x_ref, out_ref, send_sems, recv_sems, local_sem):
        x, y, c = _xyc()
        me, sibling = (x, y, c), (x, y, 1 - c)
        chips = [(1 - x, y), (x, 1 - y), (1 - x, 1 - y)]

        def cp(k, cx, cy, half, to, src=None):
            dst = out_ref.at[2 * cx + cy, half]
            return pltpu.make_async_remote_copy(
                src_ref=dst if src is None else src, dst_ref=dst, send_sem=send_sems.at[k],
                recv_sem=recv_sems.at[k], device_id=to, device_id_type=MESH)

        mine = pltpu.make_async_copy(x_ref, out_ref.at[2 * x + y], local_sem)
        mine.start()
        first = [cp(j, x, y, c, (*chip, c), src=x_ref.at[c]) for j, chip in enumerate(chips)]
        for f in first:
            f.start()
        passed = [cp(3 + j, *chip, c, sibling) for j, chip in enumerate(chips)]
        for j, chip in enumerate(chips):
            cp(j, *chip, c, me).wait_recv()
            passed[j].start()
        for j, chip in enumerate(chips):
            cp(3 + j, *chip, 1 - c, me).wait_recv()
        for f in first + passed:
            f.wait_send()
        mine.wait()

    return pl.pallas_call(
        body, name="gather_shards", in_specs=[ANY], out_specs=ANY,
        out_shape=jax.ShapeDtypeStruct((4, 2, R, C), shard.dtype),
        scratch_shapes=[pltpu.SemaphoreType.DMA((6,)), pltpu.SemaphoreType.DMA((6,)), pltpu.SemaphoreType.DMA])(shard)


FLIPS = [(0, 0, 1), (0, 1, 0), (0, 1, 1), (1, 0, 0), (1, 0, 1), (1, 1, 0), (1, 1, 1)]


def scatter_partials(g):
    _, _, R, C = g.shape

    def body(g_ref, out_ref, send_sems, recv_sems, local_sem):
        x, y, c = _xyc()
        me_idx = 4 * x + 2 * y + c
        mine = pltpu.make_async_copy(g_ref.at[2 * x + y, c], out_ref.at[me_idx], local_sem)
        mine.start()
        sends = []
        for k, (fx, fy, fc) in enumerate(FLIPS):
            px, py, pc = (x + fx) % 2, (y + fy) % 2, (c + fc) % 2
            s = pltpu.make_async_remote_copy(
                src_ref=g_ref.at[2 * px + py, pc], dst_ref=out_ref.at[me_idx], send_sem=send_sems.at[k],
                recv_sem=recv_sems.at[k], device_id=(px, py, pc), device_id_type=MESH)
            s.start()
            sends.append(s)
        for k, (fx, fy, fc) in enumerate(FLIPS):
            px, py, pc = (x + fx) % 2, (y + fy) % 2, (c + fc) % 2
            slot = out_ref.at[4 * px + 2 * py + pc]
            pltpu.make_async_remote_copy(
                src_ref=slot, dst_ref=slot, send_sem=send_sems.at[k], recv_sem=recv_sems.at[k],
                device_id=(px, py, pc), device_id_type=MESH).wait_recv()
        for s in sends:
            s.wait_send()
        mine.wait()

    return pl.pallas_call(
        body, name="scatter_partials", in_specs=[ANY], out_specs=ANY,
        out_shape=jax.ShapeDtypeStruct((8, R, C), g.dtype),
        scratch_shapes=[pltpu.SemaphoreType.DMA((7,)), pltpu.SemaphoreType.DMA((7,)), pltpu.SemaphoreType.DMA])(g)


JOIN_ROWS = 256


def sum_join(x):
    n, R, C = x.shape
    steps = R // JOIN_ROWS

    def body(x_ref, out_ref, acc, local_sems, send_sems, recv_sem):
        i = pl.program_id(0)
        slot = lax.rem(i, 2)
        xx, yy, c = _xyc()
        sibling = (xx, yy, 1 - c)

        def copies(k, s):
            dst = out_ref.at[c, pl.ds(k * JOIN_ROWS, JOIN_ROWS)]
            return (pltpu.make_async_copy(acc.at[s], dst, local_sems.at[s]),
                    pltpu.make_async_remote_copy(src_ref=acc.at[s], dst_ref=dst, send_sem=send_sems.at[s],
                                                 recv_sem=recv_sem, device_id=sibling, device_id_type=MESH))

        def drain(k, s):
            loc, rem = copies(k, s)
            loc.wait()
            rem.wait_send()

        @pl.when(i >= 2)
        def _():
            drain(i - 2, slot)

        a = x_ref[0].astype(F32)
        for s in range(1, n):
            a = a + x_ref[s].astype(F32)
        acc[slot] = a
        loc, rem = copies(i, slot)
        loc.start()
        rem.start()

        @pl.when(i == steps - 1)
        def _():
            if steps >= 2:
                drain(i - 1, 1 - slot)
            drain(i, slot)
            theirs = out_ref.at[1 - c]
            pltpu.make_async_remote_copy(src_ref=theirs, dst_ref=theirs, send_sem=send_sems.at[0], recv_sem=recv_sem,
                                         device_id=sibling, device_id_type=MESH).wait_recv()

    return pl.pallas_call(
        body, name="sum_join", grid=(steps,),
        in_specs=[pl.BlockSpec((n, JOIN_ROWS, C), lambda i: (0, i, 0))], out_specs=ANY,
        out_shape=jax.ShapeDtypeStruct((2, R, C), F32),
        scratch_shapes=[pltpu.VMEM((2, JOIN_ROWS, C), F32), pltpu.SemaphoreType.DMA((2,)),
                        pltpu.SemaphoreType.DMA((2,)), pltpu.SemaphoreType.DMA],
        compiler_params=_params(("arbitrary",)))(x)


def gather_all(name, block):
    R, C = block.shape

    def body(x_ref, out_ref, send_sems, recv_sems, local_sem):
        x, y, c = _xyc()
        mine = pltpu.make_async_copy(x_ref, out_ref.at[4 * x + 2 * y + c], local_sem)
        mine.start()
        sends = []
        for k, (fx, fy, fc) in enumerate(FLIPS):
            px, py, pc = (x + fx) % 2, (y + fy) % 2, (c + fc) % 2
            s = pltpu.make_async_remote_copy(
                src_ref=x_ref, dst_ref=out_ref.at[4 * x + 2 * y + c], send_sem=send_sems.at[k],
                recv_sem=recv_sems.at[k], device_id=(px, py, pc), device_id_type=MESH)
            s.start()
            sends.append(s)
        for k, (fx, fy, fc) in enumerate(FLIPS):
            px, py, pc = (x + fx) % 2, (y + fy) % 2, (c + fc) % 2
            slot = out_ref.at[4 * px + 2 * py + pc]
            pltpu.make_async_remote_copy(
                src_ref=slot, dst_ref=slot, send_sem=send_sems.at[k], recv_sem=recv_sems.at[k],
                device_id=(px, py, pc), device_id_type=MESH).wait_recv()
        for s in sends:
            s.wait_send()
        mine.wait()

    return pl.pallas_call(
        body, name=name, in_specs=[ANY], out_specs=ANY,
        out_shape=jax.ShapeDtypeStruct((8, R, C), block.dtype),
        scratch_shapes=[pltpu.SemaphoreType.DMA((7,)), pltpu.SemaphoreType.DMA((7,)), pltpu.SemaphoreType.DMA])(block)


PACK_C = 1024


def _pack(arrs, row_mult):
    flat = jnp.concatenate([a.reshape(-1) for a in arrs])
    n = flat.shape[0]
    rows = -(-n // PACK_C)
    rows = -(-rows // row_mult) * row_mult
    return jnp.pad(flat, (0, rows * PACK_C - n)).reshape(rows, PACK_C)


def _unpack(buf, shapes):
    flat = buf.reshape(-1)
    out, off = [], 0
    for s in shapes:
        n = int(np.prod(s))
        out.append(flat[off:off + n].reshape(s))
        off += n
    return out


OFF_Q, OFF_CKV, OFF_KR, OFF_SG, OFF_RW, OFF_GATE, N_IN_PAD = 0, 384, 640, 896, 1920, 3840, 6912
IN_SEGMENTS = [('q', OFF_Q, OFF_CKV), ('ckv', OFF_CKV, OFF_KR), ('kr', OFF_KR, OFF_SG), ('sg', OFF_SG, OFF_RW),
               ('rw', OFF_RW, OFF_GATE), ('gate', OFF_GATE, N_IN_PAD)]
ROPE_LANE = QK_NOPE
HALF = QK_ROPE // 2


def _win_layout():
    src = np.full((N_IN_PAD,), -1, np.int64)
    sgn = np.ones((N_IN_PAD,), np.float32)
    src[0:640] = np.arange(0, 640)
    kr0 = Q_LORA + KV_LORA
    src[OFF_KR + ROPE_LANE:OFF_KR + ROPE_LANE + QK_ROPE] = kr0 + np.arange(QK_ROPE)
    sw = OFF_KR + HEAD_PAD + ROPE_LANE
    src[sw:sw + HALF] = kr0 + HALF + np.arange(HALF)
    sgn[sw:sw + HALF] = -1.0
    src[sw + HALF:sw + QK_ROPE] = kr0 + np.arange(HALF)
    src[OFF_SG:N_IN_PAD] = 672 + np.arange(N_IN_PAD - OFF_SG)
    return src, sgn


def _wuq_layout():
    hw = MLA_HEADS * HEAD_PAD
    src = np.full((2 * hw,), -1, np.int64)
    sgn = np.ones((2 * hw,), np.float32)
    per = QK_NOPE + QK_ROPE
    for h in range(MLA_HEADS):
        src[h * HEAD_PAD:h * HEAD_PAD + per] = h * per + np.arange(per)
        sw = hw + h * HEAD_PAD + ROPE_LANE
        src[sw:sw + HALF] = h * per + QK_NOPE + HALF + np.arange(HALF)
        sgn[sw:sw + HALF] = -1.0
        src[sw + HALF:sw + QK_ROPE] = h * per + QK_NOPE + np.arange(HALF)
    return src, sgn


def _runs(idx, sgn):
    out, lo = [], 0
    for pos in range(1, len(idx) + 1):
        if pos == len(idx) or not (
                (idx[pos] == -1 and idx[pos - 1] == -1)
                or (idx[pos - 1] >= 0 and idx[pos] == idx[pos - 1] + 1 and sgn[pos] == sgn[pos - 1])):
            out.append((lo, pos, int(idx[lo]), float(sgn[lo])))
            lo = pos
    return out


def _select_cols(w, idx, sgn):
    pieces = []
    for lo, hi, s0, sg in _runs(idx, sgn):
        if s0 < 0:
            pieces.append(jnp.zeros((w.shape[0], hi - lo), w.dtype))
        else:
            piece = w[:, s0:s0 + hi - lo]
            pieces.append(piece if sg > 0 else -piece)
    return jnp.concatenate(pieces, axis=1)


def _permute_cols(w, src, sgn):
    return _select_cols(w, src, sgn)


def _unpermute_full(dw, src, sgn, n_cols):
    first = np.full((n_cols,), -1, np.int64)
    second = np.full((n_cols,), -1, np.int64)
    for pos, s in enumerate(src):
        if s < 0:
            continue
        if first[s] < 0:
            first[s] = pos
        else:
            second[s] = pos
    sg2 = np.where(second >= 0, sgn[np.maximum(second, 0)], 1.0)
    return _select_cols(dw, first, sgn[first]), _select_cols(dw, second, sg2)


def _blockdiag(w):
    z = jnp.zeros_like(w[0])
    return jnp.concatenate([jnp.concatenate([w[0], z], axis=1), jnp.concatenate([z, w[1]], axis=1)], axis=0)


def _rope_tables(pos):
    inv = 1.0 / (ROPE_THETA ** (jnp.arange(0, QK_ROPE, 2, dtype=F32) / QK_ROPE))
    ang = pos.astype(F32)[:, None] * inv[None, :]
    cos, sin = jnp.cos(ang), jnp.sin(ang)
    pad = lambda t, fill: jnp.concatenate(
        [jnp.full((t.shape[0], ROPE_LANE), fill, F32), t, t, jnp.full((t.shape[0], HEAD_PAD - ROPE_LANE - QK_ROPE), fill, F32)], axis=1)
    return pad(cos, 1.0), pad(sin, 0.0)


def kernel(x, positions, attn_norm_g, w_in, gate_b, q_norm_g, w_uq, kv_norm_g, w_ukv, sg_ln_g, sg_ln_b, sg_w, sg_b, rw_mu, rw_w0, rw_w2, rw_a0, rw_a2, rw_g2, rw_k_k, rw_k_a, rw_r_k, rw_ln_g, rw_ln_b, w_branch, w_out, ffn_norm_g, w_ffn_gate, w_ffn_up, w_ffn_down, final_norm_g, loss_target, m_attn_norm_g, m_w_in, m_gate_b, m_q_norm_g, m_w_uq, m_kv_norm_g, m_w_ukv, m_sg_ln_g, m_sg_ln_b, m_sg_w, m_sg_b, m_rw_mu, m_rw_w0, m_rw_w2, m_rw_a0, m_rw_a2, m_rw_g2, m_rw_k_k, m_rw_k_a, m_rw_r_k, m_rw_ln_g, m_rw_ln_b, m_w_branch, m_w_out, m_ffn_norm_g, m_w_ffn_gate, m_w_ffn_up, m_w_ffn_down, m_final_norm_g, v_attn_norm_g, v_w_in, v_gate_b, v_q_norm_g, v_w_uq, v_kv_norm_g, v_w_ukv, v_sg_ln_g, v_sg_ln_b, v_sg_w, v_sg_b, v_rw_mu, v_rw_w0, v_rw_w2, v_rw_a0, v_rw_a2, v_rw_g2, v_rw_k_k, v_rw_k_a, v_rw_r_k, v_rw_ln_g, v_rw_ln_b, v_w_branch, v_w_out, v_ffn_norm_g, v_w_ffn_gate, v_w_ffn_up, v_w_ffn_down, v_final_norm_g):
    args = locals()
    W = {n: args[n] for n in WEIGHTS}
    M1 = {n: args['m_' + n] for n in WEIGHTS}
    M2 = {n: args['v_' + n] for n in WEIGHTS}
    B, S, D = x.shape
    N = B * S
    TM = _pick(N, (256, 128))
    TMH = 128
    TQ = _pick(S, (512, 256, 128))

    shard_shapes = [W[n].shape for n in SHARDED]
    full = {}
    mm_pack = _pack([W[n].astype(BF16) for n in MATMUL_SHARDED], 32)
    Rm = mm_pack.shape[0]
    gathered = gather_shards(mm_pack.reshape(2, Rm // 2, PACK_C)).reshape(4, Rm, PACK_C)
    pieces = [_unpack(gathered[q], [W[n].shape for n in MATMUL_SHARDED]) for q in range(4)]
    for i, n in enumerate(MATMUL_SHARDED):
        full[n] = jnp.concatenate([pieces[q][i] for q in range(4)], axis=SHARD_AXIS[n])
    small = gather_all("gather_small", _pack([W[n] for n in SMALL_SHARDED], 8))
    pieces = [_unpack(small[2 * q], [W[n].shape for n in SMALL_SHARDED]) for q in range(4)]
    for i, n in enumerate(SMALL_SHARDED):
        full[n] = jnp.concatenate([pieces[q][i] for q in range(4)], axis=SHARD_AXIS[n])
    for n in ('rw_w2', 'rw_a2', 'rw_g2'):
        full[n] = full[n].astype(F32)
    for n in REPLICATED:
        full[n] = W[n]

    win_src, win_sgn = _win_layout()
    wuq_src, wuq_sgn = _wuq_layout()
    ones = jnp.asarray(np.kron(np.eye(RWKV_HEADS), np.ones((RWKV_HEAD, RWKV_HEAD))), BF16)
    ct, st = _rope_tables(positions.reshape(N))
    row = lambda v: v.reshape(1, -1)

    H, HD = RWKV_HEADS, RWKV_HEAD
    inst = 2 * B * H
    isplit = LANES // inst
    Ip = HD // isplit
    to_j = lambda nm, c: to_scan(nm, c, B, S, c.shape[1] // RWKV_DIM, None)
    to_i = lambda nm, c: to_scan(nm, c, B, S, 1, Ip)

    def shift_prev(z):
        z = z.reshape(B, S, -1)
        return jnp.pad(z[:, :-1], ((0, 0), (1, 0), (0, 0))).reshape(N, -1)

    def shift_next(z):
        z = z.reshape(B, S, -1)
        return jnp.pad(z[:, 1:], ((0, 0), (0, 1), (0, 0))).reshape(N, -1)

    LW = []
    for l in range(DEPTH):
        wb = full['w_branch'][l]
        wb0 = jnp.zeros((MLA_HEADS, HEAD_PAD, D), F32).at[:, QK_NOPE:].set(wb[0].reshape(MLA_HEADS, V_HEAD, D))
        LW.append(dict(
            attn_g=row(full['attn_norm_g'][l]),
            w_in=_permute_cols(full['w_in'][l], win_src, win_sgn),
            gate_b=row(full['gate_b'][l]),
            q_g=row(full['q_norm_g'][l]),
            w_uq=_permute_cols(full['w_uq'][l], wuq_src, wuq_sgn),
            kv_g=row(full['kv_norm_g'][l]),
            w_ukv=full['w_ukv'][l],
            sg_g=row(full['sg_ln_g'][l]), sg_b=row(full['sg_ln_b'][l]), sg_w=full['sg_w'][l],
            sg_bias=jnp.repeat(full['sg_b'][l].T, SG_DIM // SG_GROUPS, axis=1),
            mu=row(full['rw_mu'][l]), w0=row(full['rw_w0'][l]), w2=_blockdiag(full['rw_w2'][l]),
            a0=row(full['rw_a0'][l]), a2=_blockdiag(full['rw_a2'][l]), g2=full['rw_g2'][l],
            k_k=row(full['rw_k_k'][l]), k_a=row(full['rw_k_a'][l]), r_k=row(full['rw_r_k'][l]),
            ln_g=row(full['rw_ln_g'][l]), ln_b=row(full['rw_ln_b'][l]),
            wb0=wb0.reshape(MLA_HEADS * HEAD_PAD, D), wb1=wb[1], wb2=wb[2],
            w_out=full['w_out'][l], ffn_g=row(full['ffn_norm_g'][l]),
            w_gu=jnp.concatenate([full['w_ffn_gate'][l], full['w_ffn_up'][l]], axis=1),
            w_down=full['w_ffn_down'][l]))

    saved = []
    xc = x.reshape(N, D)
    for l in range(DEPTH):
        p = LW[l]
        t = 'l%d_' % l
        sv = dict(x=xc)
        (h,) = rowwise(t + 'attn_norm', f_rms, [xc], [p['attn_g']], [], [D], TM, [BF16])
        p_q, p_ckv, p_kr, p_sg, z, p_gate = [
            matmul(t + 'in_proj_' + sn, h, p['w_in'][:, a:b], 'nn') for sn, a, b in IN_SEGMENTS]
        sv['h'] = h
        (cq,) = rowwise(t + 'q_norm', f_rms, [p_q], [p['q_g']], [], [Q_LORA], TM, [BF16])
        (ckv,) = rowwise(t + 'kv_norm', f_rms, [p_ckv], [p['kv_g']], [], [KV_LORA], TM, [BF16])
        qq = matmul(t + 'uq', cq, p['w_uq'], 'nn')
        kv = matmul(t + 'ukv', ckv, p['w_ukv'], 'nn')
        qh, kh = rowwise(t + 'rope', f_rope, [qq, kv, p_kr, ct, st], [], [], [MLA_HEADS * HEAD_PAD] * 2, TM)
        ya = attention_fwd(t + 'attn', qh, kh, kv, B, S, TQ, BF16)
        sv.update(p_q=p_q, p_ckv=p_ckv, p_kr=p_kr, cq=cq, ckv=ckv, qq=qq, kv=kv, qh=qh, kh=kh, ya=ya)
        (yb,) = rowwise(t + 'sg', f_sg, [p_sg], [p['sg_g'], p['sg_b'], p['sg_w'], p['sg_bias']], [], [SG_DIM], TM,
                        [BF16])
        sv.update(p_sg=p_sg, yb=yb)
        zp, zn = shift_prev(z), shift_next(z)
        rw_par = [p['mu'], p['w0'], p['w2'], p['a0'], p['a2'], p['g2'], p['k_k'], p['k_a']]
        r_, v_, decay, kdir, kk, bdir, g_ = rowwise(
            t + 'rw_pre', f_rw_pre, [z, zp, zn], rw_par, [ones],
            [RWKV_DIM, RWKV_DIM, 2 * RWKV_DIM, 2 * RWKV_DIM, RWKV_DIM, 2 * RWKV_DIM, RWKV_DIM], TM)
        sc = dict(w=to_j(t + 'lay_w', decay), k=to_j(t + 'lay_k', kdir), b=to_j(t + 'lay_b', bdir),
                  kk=to_j(t + 'lay_kk', kk), r=to_j(t + 'lay_r', r_), v=to_i(t + 'lay_v', v_))
        y_f, y_r, sp, sa_all, s_last = scan_fwd(t + 'scan', sc['w'], sc['k'], sc['b'], sc['kk'], sc['r'], sc['v'])
        ysum = from_scan(t + 'lay_y', y_f, y_r, B, S, 1, True)
        (yc,) = rowwise(t + 'rw_post', f_rw_post, [ysum, r_, v_, kdir, g_], [p['ln_g'], p['ln_b'], p['r_k']],
                        [ones], [RWKV_DIM], TM, [BF16])
        sv.update(z=z, zp=zp, zn=zn, r=r_, v=v_, kdir=kdir, g=g_, sc=sc, sp=sp, sa=sa_all, s_last=s_last,
                  ysum=ysum, yc=yc)
        b0 = matmul(t + 'br0', ya, p['wb0'], 'nn')
        b1 = matmul(t + 'br1', yb, p['wb1'], 'nn')
        b2 = matmul(t + 'br2', yc, p['wb2'], 'nn')
        (merged,) = rowwise(t + 'merge', f_merge, [p_gate, b0, b1, b2], [p['gate_b']], [], [D], TM, [BF16])
        x2 = matmul(t + 'out_proj', merged, p['w_out'], 'nn', add=xc)
        sv.update(p_gate=p_gate, b0=b0, b1=b1, b2=b2, merged=merged, x2=x2)
        (h2,) = rowwise(t + 'ffn_norm', f_rms, [x2], [p['ffn_g']], [], [D], TM, [BF16])
        au = matmul(t + 'ffn_in', h2, p['w_gu'], 'nn')
        (act,) = rowwise(t + 'swiglu', f_swiglu, [au], [], [], [D_FF], TM, [BF16])
        xc = matmul(t + 'ffn_out', act, p['w_down'], 'nn', add=x2)
        sv.update(h2=h2, au=au, act=act)
        saved.append(sv)

    loss_part, dx, d_final_g = loss_head(xc, loss_target.reshape(N, D), row(full['final_norm_g']), TM)
    loss = lax.psum(loss_part[0, 0], ("x", "y", "c"))

    G = {n: [None] * DEPTH for n in WEIGHTS if n != 'final_norm_g'}
    for l in reversed(range(DEPTH)):
        p, sv = LW[l], saved[l]
        t = 'l%d_bwd_' % l
        d_act = matmul(t + 'ffn_out_dx', dx, p['w_down'], 'nt')
        G['w_ffn_down'][l] = matmul(t + 'ffn_out_dw', sv['act'], dx, 'tn')
        (d_au,), _ = rowwise_bwd(t + 'swiglu', f_swiglu, [sv['au']], [], [], [[d_act]], TM, drow_dtypes=[BF16])
        d_h2 = matmul(t + 'ffn_in_dx', d_au, p['w_gu'], 'nt')
        d_wgu = matmul(t + 'ffn_in_dw', sv['h2'], d_au, 'tn')
        G['w_ffn_gate'][l], G['w_ffn_up'][l] = d_wgu[:, :D_FF], d_wgu[:, D_FF:]
        (dx2,), (dg,) = rowwise_bwd(t + 'ffn_norm', f_rms, [sv['x2']], [p['ffn_g']], [], [[d_h2]], TM, extra=[(0, dx)])
        G['ffn_norm_g'][l] = dg.reshape(-1)
        d_merged = matmul(t + 'out_proj_dx', dx2, p['w_out'], 'nt')
        G['w_out'][l] = matmul(t + 'out_proj_dw', sv['merged'], dx2, 'tn')
        (d_pgate, d_b0, d_b1, d_b2), (d_gate_b,) = rowwise_bwd(
            t + 'merge', f_merge, [sv['p_gate'], sv['b0'], sv['b1'], sv['b2']], [p['gate_b']], [], [[d_merged]], TM,
            drow_dtypes=[BF16] * 4)
        G['gate_b'][l] = d_gate_b.reshape(3, D)
        d_ya = matmul(t + 'br0_dx', d_b0, p['wb0'], 'nt')
        d_yb = matmul(t + 'br1_dx', d_b1, p['wb1'], 'nt')
        d_yc = matmul(t + 'br2_dx', d_b2, p['wb2'], 'nt')
        d_wb0 = matmul(t + 'br0_dw', sv['ya'], d_b0, 'tn').reshape(MLA_HEADS, HEAD_PAD, D)[:, QK_NOPE:].reshape(-1, D)
        G['w_branch'][l] = jnp.stack([d_wb0, matmul(t + 'br1_dw', sv['yb'], d_b1, 'tn'),
                                      matmul(t + 'br2_dw', sv['yc'], d_b2, 'tn')])
        (d_y, d_r1, d_v1, d_kdir1, d_g), (d_ln_g, d_ln_b, d_r_k) = rowwise_bwd(
            t + 'rw_post', f_rw_post, [sv['ysum'], sv['r'], sv['v'], sv['kdir'], sv['g']],
            [p['ln_g'], p['ln_b'], p['r_k']], [ones], [[d_yc]], TM)
        G['rw_ln_g'][l], G['rw_ln_b'][l] = d_ln_g.reshape(-1), d_ln_b.reshape(-1)
        G['rw_r_k'][l] = d_r_k.reshape(RWKV_HEADS, RWKV_HEAD)
        sc = sv['sc']
        res = scan_bwd(t + 'scan', sc['w'], sc['k'], sc['b'], sc['kk'], sc['r'], sc['v'], sv['sp'], sv['sa'],
                       sv['s_last'], to_i(t + 'lay_dy', d_y))
        s_dw, s_dk, s_db, s_dkk, s_dr, s_dv = [
            from_scan(t + 'lay_' + nm, res[2 * i], res[2 * i + 1], B, S, nd, nm == 'dv')
            for i, (nm, nd) in enumerate((('dw', 2), ('dk', 2), ('db', 2), ('dkk', 1), ('dr', 1), ('dv', 1)))]
        rw_par = [p['mu'], p['w0'], p['w2'], p['a0'], p['a2'], p['g2'], p['k_k'], p['k_a']]
        d_outs = [[d_r1, s_dr], [d_v1, s_dv], [s_dw], [d_kdir1, s_dk], [s_dkk], [s_db], [d_g]]
        (d_z, d_zp, d_zn), d_rw = rowwise_bwd(
            t + 'rw_pre', f_rw_pre, [sv['z'], sv['zp'], sv['zn']], rw_par, [ones], d_outs, TMH)
        (d_prw,) = rowwise(t + 'shift_sum', f_add3, [d_z, shift_next(d_zp), shift_prev(d_zn)], [], [], [RWKV_IN], TM,
                           [BF16])
        G['rw_mu'][l] = d_rw[0].reshape(-1)
        G['rw_w0'][l] = d_rw[1].reshape(2, RWKV_DIM)
        G['rw_w2'][l] = jnp.stack([d_rw[2][:64, :RWKV_DIM], d_rw[2][64:, RWKV_DIM:]])
        G['rw_a0'][l] = d_rw[3].reshape(2, RWKV_DIM)
        G['rw_a2'][l] = jnp.stack([d_rw[4][:64, :RWKV_DIM], d_rw[4][64:, RWKV_DIM:]])
        G['rw_g2'][l] = d_rw[5]
        G['rw_k_k'][l], G['rw_k_a'][l] = d_rw[6].reshape(-1), d_rw[7].reshape(-1)
        (d_psg,), (d_sg_g, d_sg_b, d_sg_w, d_sg_bias) = rowwise_bwd(
            t + 'sg', f_sg, [sv['p_sg']], [p['sg_g'], p['sg_b'], p['sg_w'], p['sg_bias']], [], [[d_yb]], TMH,
            drow_dtypes=[BF16])
        G['sg_ln_g'][l], G['sg_ln_b'][l], G['sg_w'][l] = d_sg_g.reshape(-1), d_sg_b.reshape(-1), d_sg_w
        G['sg_b'][l] = d_sg_bias.reshape(SG_CHUNK, SG_GROUPS, SG_DIM // SG_GROUPS).sum(-1).T
        d_qh, d_kh, d_kvv = attention_bwd(t + 'attn', sv['qh'], sv['kh'], sv['kv'], d_ya, B, S, TQ)
        (d_qq, d_kv, d_pkr), _ = rowwise_bwd(
            t + 'rope', f_rope, [sv['qq'], sv['kv'], sv['p_kr'], ct, st], [], [], [[d_qh], [d_kh]], TM,
            n_row_diff=3, extra=[(1, d_kvv)], drow_dtypes=[BF16] * 3)
        d_cq = matmul(t + 'uq_dx', d_qq, p['w_uq'], 'nt')
        d_wuq = matmul(t + 'uq_dw', sv['cq'], d_qq, 'tn')
        g1, g2_ = _unpermute_full(d_wuq, wuq_src, wuq_sgn, MLA_HEADS * (QK_NOPE + QK_ROPE))
        G['w_uq'][l] = g1 + g2_
        d_ckv = matmul(t + 'ukv_dx', d_kv, p['w_ukv'], 'nt')
        G['w_ukv'][l] = matmul(t + 'ukv_dw', sv['ckv'], d_kv, 'tn')
        (d_pq,), (dg,) = rowwise_bwd(t + 'q_norm', f_rms, [sv['p_q']], [p['q_g']], [], [[d_cq]], TM,
                                     drow_dtypes=[BF16])
        G['q_norm_g'][l] = dg.reshape(-1)
        (d_pckv,), (dg,) = rowwise_bwd(t + 'kv_norm', f_rms, [sv['p_ckv']], [p['kv_g']], [], [[d_ckv]], TM,
                                       drow_dtypes=[BF16])
        G['kv_norm_g'][l] = dg.reshape(-1)
        d_h, d_cols = None, []
        for (sn, a, b), d_seg in zip(IN_SEGMENTS, [d_pq, d_pckv, d_pkr, d_psg, d_prw, d_pgate]):
            d_h = matmul(t + 'in_proj_dx_' + sn, d_seg, p['w_in'][:, a:b], 'nt', add=d_h)
            d_cols.append(matmul(t + 'in_proj_dw_' + sn, sv['h'], d_seg, 'tn'))
        d_win = jnp.concatenate(d_cols, axis=1)
        g1, g2_ = _unpermute_full(d_win, win_src, win_sgn, N_IN)
        kr0 = Q_LORA + KV_LORA
        G['w_in'][l] = g1.at[:, kr0:kr0 + QK_ROPE].add(g2_[:, kr0:kr0 + QK_ROPE])
        (dx,), (dg,) = rowwise_bwd(t + 'attn_norm', f_rms, [sv['x']], [p['attn_g']], [], [[d_h]], TM, extra=[(0, dx2)])
        G['attn_norm_g'][l] = dg.reshape(-1)

    grads = {n: jnp.stack(G[n]) for n in G}
    grads['final_norm_g'] = d_final_g.reshape(-1)
    grad_x = dx.reshape(B, S, D)

    per_shard = []
    for q in range(4):
        sl = []
        for n in SHARDED:
            ax = SHARD_AXIS[n]
            w = W[n].shape[ax]
            sl.append(lax.slice_in_dim(grads[n], q * w, (q + 1) * w, axis=ax).astype(BF16))
        per_shard.append(_pack(sl, 2 * JOIN_ROWS))
    R = per_shard[0].shape[0]
    gpack = jnp.stack(per_shard).astype(BF16).reshape(4, 2, R // 2, PACK_C)
    both = sum_join(scatter_partials(gpack))
    g_shard = dict(zip(SHARDED, _unpack(both.reshape(R, PACK_C), shard_shapes)))
    rep_shapes = [W[n].shape for n in REPLICATED]
    rpack = _pack([grads[n] for n in REPLICATED], 8)
    g_rep = sum_slots("sum_replicated", gather_all("gather_replicated", rpack))

    outs = {}
    for n in MATMUL_SHARDED:
        shp = W[n].shape
        two = lambda a: a.reshape(-1, shp[-1])
        res = adamw("adamw_" + n, two(W[n]), two(g_shard[n]), two(M1[n]), two(M2[n]))
        outs['grad', n] = g_shard[n]
        for key, a in zip(('delta', 'new_m', 'new_v'), res):
            outs[key, n] = a.reshape(shp)
    small = SMALL_SHARDED + REPLICATED
    small_shapes = [W[n].shape for n in small]
    g_small = [g_shard[n] for n in SMALL_SHARDED] + _unpack(g_rep, rep_shapes)
    res = adamw("adamw_small", _pack([W[n] for n in small], 8), _pack(g_small, 8),
                _pack([M1[n] for n in small], 8), _pack([M2[n] for n in small], 8))
    for n, a in zip(small, g_small):
        outs['grad', n] = a
    for key, buf in zip(('delta', 'new_m', 'new_v'), res):
        for n, a in zip(small, _unpack(buf, small_shapes)):
            outs[key, n] = a
    return (loss, grad_x, *[outs['grad', n] for n in WEIGHTS], *[outs['delta', n] for n in WEIGHTS],
            *[outs['new_m', n] for n in WEIGHTS], *[outs['new_v', n] for n in WEIGHTS])
```

```python
import functools
import math

import numpy as np
import jax
import jax.numpy as jnp
from jax import lax
from jax.experimental import pallas as pl
from jax.experimental.pallas import tpu as pltpu

F32 = jnp.float32
BF16 = jnp.bfloat16

DEPTH = 2
MLA_HEADS = 8
Q_LORA = 384
KV_LORA = 256
QK_NOPE = 64
QK_ROPE = 32
V_HEAD = 64
ROPE_THETA = 10000.0
SG_GROUPS = 8
SG_DIM = 512
SG_CHUNK = 128
RWKV_HEADS = 8
RWKV_HEAD = 64
RWKV_DIM = 512
GN_EPS = 64e-5
NORM_EPS = 1e-6
D_FF = 2816
RWKV_IN = 1920
N_IN = 6688
ADAM_LR, ADAM_B1, ADAM_B2, ADAM_EPS, ADAM_WD, ADAM_STEP = 0.001, 0.9, 0.999, 1e-08, 0.01, 10

LANES = 128
HEAD_PAD = 128
VMEM_LIMIT = 56 * 1024 * 1024
MESH = pl.DeviceIdType.MESH

WEIGHTS = ['attn_norm_g', 'w_in', 'gate_b', 'q_norm_g', 'w_uq', 'kv_norm_g', 'w_ukv', 'sg_ln_g', 'sg_ln_b', 'sg_w',
           'sg_b', 'rw_mu', 'rw_w0', 'rw_w2', 'rw_a0', 'rw_a2', 'rw_g2', 'rw_k_k', 'rw_k_a', 'rw_r_k', 'rw_ln_g',
           'rw_ln_b', 'w_branch', 'w_out', 'ffn_norm_g', 'w_ffn_gate', 'w_ffn_up', 'w_ffn_down', 'final_norm_g']
SHARD_AXIS = {'w_in': 2, 'gate_b': 2, 'w_uq': 2, 'w_ukv': 2, 'rw_w0': 2, 'rw_w2': 3, 'rw_a0': 2, 'rw_a2': 3,
              'rw_g2': 2, 'w_branch': 3, 'w_out': 1, 'w_ffn_gate': 2, 'w_ffn_up': 2, 'w_ffn_down': 1}
SHARDED = [n for n in WEIGHTS if n in SHARD_AXIS]
REPLICATED = [n for n in WEIGHTS if n not in SHARD_AXIS]
SMALL_SHARDED = ['gate_b', 'rw_w0', 'rw_a0']
MATMUL_SHARDED = [n for n in SHARDED if n not in SMALL_SHARDED]


def _params(sem=None):
    return pltpu.CompilerParams(dimension_semantics=sem, vmem_limit_bytes=VMEM_LIMIT)


def _pick(n, cands):
    for c in cands:
        if n % c == 0:
            return c
    return n


def _dot(a, b, dims):
    return lax.dot_general(a.astype(BF16), b.astype(BF16), (dims, ((), ())), preferred_element_type=F32)


def _nn(a, b):
    return _dot(a, b, ((1,), (0,)))


def _nt(a, b):
    return _dot(a, b, ((1,), (1,)))


def _tn(a, b):
    return _dot(a, b, ((0,), (0,)))


@jax.custom_vjp
def mm(a, b):
    return _nn(a, b)


mm.defvjp(lambda a, b: (_nn(a, b), (a, b)), lambda res, g: (_nt(g, res[1]), _tn(res[0], g)))


@jax.custom_vjp
def mm_nt(a, b):
    return _nt(a, b)


mm_nt.defvjp(lambda a, b: (_nt(a, b), (a, b)), lambda res, g: (_nn(g, res[1]), _tn(g, res[0])))


def _seg_raw(x, ones):
    hi = x.astype(BF16)
    lo = (x - hi.astype(F32)).astype(BF16)
    d = (((1,), (0,)), ((), ()))
    return (lax.dot_general(hi, ones, d, preferred_element_type=F32)
            + lax.dot_general(lo, ones, d, preferred_element_type=F32))


@jax.custom_vjp
def segsum(x, ones):
    return _seg_raw(x, ones)


segsum.defvjp(lambda x, ones: (_seg_raw(x, ones), ones),
              lambda ones, g: (_seg_raw(g, ones), jnp.zeros_like(ones)))


def _sigmoid(x):
    return 0.5 * (jnp.tanh(0.5 * x) + 1.0)


def _rms(x, g):
    return x * lax.rsqrt(jnp.mean(x * x, axis=-1, keepdims=True) + NORM_EPS) * g


def matmul(name, a, b, mode, add=None, out_dtype=F32):
    if mode == 'nn':
        (M, K), (_, N) = a.shape, b.shape
    elif mode == 'nt':
        (M, K), (N, _) = a.shape, b.shape
    else:
        (K, M), (_, N) = a.shape, b.shape
    tm = _pick(M, (1408, 1024, 512, 384, 256, 128))
    tn = _pick(N, (1408, 1024, 768, 512, 384, 256, 128))
    tk = _pick(K, (1024, 512, 384, 256, 128))
    nk = K // tk
    dims = {'nn': ((1,), (0,)), 'nt': ((1,), (1,)), 'tn': ((0,), (0,))}[mode]
    a_spec = pl.BlockSpec((tk, tm), lambda i, j, k: (k, i)) if mode == 'tn' else pl.BlockSpec((tm, tk), lambda i, j, k: (i, k))
    b_spec = pl.BlockSpec((tn, tk), lambda i, j, k: (j, k)) if mode == 'nt' else pl.BlockSpec((tk, tn), lambda i, j, k: (k, j))
    o_spec = pl.BlockSpec((tm, tn), lambda i, j, k: (i, j))
    has_add = add is not None

    def body(*refs):
        if has_add:
            a_ref, b_ref, add_ref, o_ref, acc = refs
        else:
            a_ref, b_ref, o_ref, acc = refs
        k = pl.program_id(2)

        @pl.when(k == 0)
        def _():
            acc[...] = jnp.zeros_like(acc)

        acc[...] += _dot(a_ref[...], b_ref[...], dims)

        @pl.when(k == nk - 1)
        def _():
            o_ref[...] = (acc[...] + add_ref[...] if has_add else acc[...]).astype(o_ref.dtype)

    ins = [a, b] + ([add] if has_add else [])
    specs = [a_spec, b_spec] + ([o_spec] if has_add else [])
    return pl.pallas_call(
        body, name=name, grid=(M // tm, N // tn, nk), in_specs=specs, out_specs=o_spec,
        out_shape=jax.ShapeDtypeStruct((M, N), out_dtype), scratch_shapes=[pltpu.VMEM((tm, tn), F32)],
        compiler_params=_params(("parallel", "parallel", "arbitrary")))(*ins)


def _full_spec(p):
    nd = p.ndim
    return pl.BlockSpec(p.shape, lambda i, _nd=nd: (0,) * _nd)


def rowwise(name, fn, rows, params, consts, out_widths, tm, out_dtypes=None):
    N = rows[0].shape[0]
    nr, npar, nc = len(rows), len(params), len(consts)

    def body(*refs):
        vals = [r[...] for r in refs[:nr + npar + nc]]
        res = fn(*vals)
        for o, v in zip(refs[nr + npar + nc:], res):
            o[...] = v.astype(o.dtype)

    in_specs = ([pl.BlockSpec((tm, r.shape[1]), lambda i: (i, 0)) for r in rows]
                + [_full_spec(p) for p in list(params) + list(consts)])
    out_specs = [pl.BlockSpec((tm, w), lambda i: (i, 0)) for w in out_widths]
    return pl.pallas_call(
        body, name=name, grid=(N // tm,), in_specs=in_specs, out_specs=out_specs,
        out_shape=[jax.ShapeDtypeStruct((N, w), d) for w, d in zip(out_widths, out_dtypes or [F32] * len(out_widths))],
        compiler_params=_params(("parallel",)))(*rows, *params, *consts)


def rowwise_bwd(name, fn, rows, params, consts, d_outs, tm, n_row_diff=None, extra=(), drow_dtypes=None):
    N = rows[0].shape[0]
    nr, npar, nc = len(rows), len(params), len(consts)
    nd = nr if n_row_diff is None else n_row_diff
    counts = [len(p) for p in d_outs]
    flat_d = [a for parts in d_outs for a in parts]
    nflat, nex = len(flat_d), len(extra)

    def body(*refs):
        pos = 0
        row_v = [r[...] for r in refs[pos:pos + nr]]; pos += nr
        par_v = [r[...] for r in refs[pos:pos + npar]]; pos += npar
        con_v = [r[...] for r in refs[pos:pos + nc]]; pos += nc
        d_refs = refs[pos:pos + nflat]; pos += nflat
        ex_refs = refs[pos:pos + nex]; pos += nex
        drow_refs = refs[pos:pos + nd]; pos += nd
        dpar_refs = refs[pos:pos + npar]

        def f(*diff):
            return fn(*diff[:nd], *row_v[nd:], *diff[nd:], *con_v)

        _, vjp = jax.vjp(f, *row_v[:nd], *par_v)
        cts, q = [], 0
        for c in counts:
            g = d_refs[q][...].astype(F32)
            for t in range(1, c):
                g = g + d_refs[q + t][...].astype(F32)
            cts.append(g)
            q += c
        grads = vjp(tuple(cts))
        drow = list(grads[:nd])
        for (idx, _), r in zip(extra, ex_refs):
            drow[idx] = drow[idx] + r[...].astype(F32)
        for o, v in zip(drow_refs, drow):
            o[...] = v.astype(o.dtype)

        @pl.when(pl.program_id(0) == 0)
        def _():
            for o in dpar_refs:
                o[...] = jnp.zeros_like(o)

        for o, v in zip(dpar_refs, grads[nd:]):
            o[...] += v

    ex_arrs = [a for _, a in extra]
    in_specs = ([pl.BlockSpec((tm, r.shape[1]), lambda i: (i, 0)) for r in rows]
                + [_full_spec(p) for p in list(params) + list(consts)]
                + [pl.BlockSpec((tm, a.shape[1]), lambda i: (i, 0)) for a in flat_d + ex_arrs])
    out_specs = ([pl.BlockSpec((tm, r.shape[1]), lambda i: (i, 0)) for r in rows[:nd]]
                 + [_full_spec(p) for p in params])
    out_shape = ([jax.ShapeDtypeStruct(r.shape, d) for r, d in zip(rows[:nd], drow_dtypes or [F32] * nd)]
                 + [jax.ShapeDtypeStruct(p.shape, F32) for p in params])
    res = pl.pallas_call(
        body, name=name, grid=(N // tm,), in_specs=in_specs, out_specs=out_specs, out_shape=out_shape,
        compiler_params=_params(("arbitrary",)))(*rows, *params, *consts, *flat_d, *ex_arrs)
    return list(res[:nd]), list(res[nd:])


def f_rms(x, g):
    return (_rms(x, g),)


def f_rope(qq, kv, krr, ct, st):
    hw = MLA_HEADS * HEAD_PAD
    c8 = jnp.tile(ct, (1, MLA_HEADS))
    s8 = jnp.tile(st, (1, MLA_HEADS))
    q = qq[:, :hw] * c8 + qq[:, hw:] * s8
    kr = krr[:, :HEAD_PAD] * ct + krr[:, HEAD_PAD:] * st
    lane = lax.broadcasted_iota(jnp.int32, kv.shape, 1) % HEAD_PAD
    k = jnp.where(lane < QK_NOPE, kv, jnp.tile(kr, (1, MLA_HEADS)))
    return q, k


def f_sg(p, ln_g, ln_b, w, bias):
    z = 0.5 * p * (1.0 + jnp.tanh(0.7978845608028654 * (p + 0.044715 * p * p * p)))
    u, v = z[:, :SG_DIM], z[:, SG_DIM:]
    mu = jnp.mean(v, axis=-1, keepdims=True)
    var = jnp.mean(jnp.square(v - mu), axis=-1, keepdims=True)
    v = (v - mu) * lax.rsqrt(var + 1e-5) * ln_g + ln_b
    lane = lax.broadcasted_iota(jnp.int32, (SG_CHUNK, LANES), 1)
    outs = []
    for c in range(p.shape[0] // SG_CHUNK):
        vc = v[c * SG_CHUNK:(c + 1) * SG_CHUNK]
        cols = []
        for m in range(SG_DIM // LANES):
            blk = vc[:, m * LANES:(m + 1) * LANES]
            cols.append(jnp.where(lane < 64, mm(w[2 * m], blk), mm(w[2 * m + 1], blk)))
        outs.append(jnp.concatenate(cols, axis=1) + bias)
    mixed = outs[0] if len(outs) == 1 else jnp.concatenate(outs, axis=0)
    return (u * mixed,)


def f_rw_pre(z, zp, zn, mu, w0, w2, a0, a2, g2, k_k, k_a, ones):
    z = z + mu * (0.5 * (zp + zn) - z)
    C = RWKV_DIM
    r, k, v = z[:, :C], z[:, C:2 * C], z[:, 2 * C:3 * C]
    wl, al, gl = z[:, 3 * C:3 * C + 128], z[:, 3 * C + 128:3 * C + 256], z[:, 3 * C + 256:]
    w = w0 + mm(jnp.tanh(wl), w2)
    decay = jnp.exp(-0.6065306597126334 * _sigmoid(w))
    a = _sigmoid(a0 + mm(al, a2))
    g = mm(_sigmoid(gl), g2)
    kk = k * k_k
    kk = kk / jnp.maximum(jnp.sqrt(segsum(kk * kk, ones)), 1e-12)
    k2 = jnp.concatenate([k, k], axis=1)
    kdir = k2 * (1.0 + (a - 1.0) * jnp.concatenate([k_a, k_a], axis=1))
    bdir = jnp.concatenate([kk, kk], axis=1) * a
    return r, v, decay, kdir, kk, bdir, g


def f_rw_post(y, r, v, kdir, g, ln_g, ln_b, r_k, ones):
    mean = segsum(y, ones) * (1.0 / RWKV_HEAD)
    yc = y - mean
    var = segsum(yc * yc, ones) * (1.0 / RWKV_HEAD)
    y = yc * lax.rsqrt(var + GN_EPS) * ln_g + ln_b
    C = RWKV_DIM
    bonus = segsum(r * kdir[:, :C] * r_k, ones) + segsum(r * kdir[:, C:] * r_k, ones)
    return ((y + bonus * v) * g,)


def f_merge(pg, b0, b1, b2, gate_b):
    D = b0.shape[1]
    gt = _sigmoid(pg + gate_b)
    return (gt[:, :D] * b0 + gt[:, D:2 * D] * b1 + gt[:, 2 * D:] * b2,)


def f_swiglu(au):
    a, u = au[:, :D_FF], au[:, D_FF:]
    return (a * _sigmoid(a) * u,)


def f_add3(a, b, c):
    return (a + b + c,)


def loss_head(x, tgt, g, tm):
    N, D = x.shape

    def body(x_ref, t_ref, g_ref, loss_ref, dx_ref, dg_ref):
        t = t_ref[...]

        def f(xv, gv):
            err = _rms(xv, gv) - t
            return 0.5 * jnp.sum(jnp.mean(err * err, axis=-1, keepdims=True))

        val, (dx, dg) = jax.value_and_grad(f, argnums=(0, 1))(x_ref[...], g_ref[...])
        dx_ref[...] = dx

        @pl.when(pl.program_id(0) == 0)
        def _():
            loss_ref[...] = jnp.zeros_like(loss_ref)
            dg_ref[...] = jnp.zeros_like(dg_ref)

        loss_ref[...] += jnp.full(loss_ref.shape, val, F32)
        dg_ref[...] += dg

    row = pl.BlockSpec((tm, D), lambda i: (i, 0))
    return pl.pallas_call(
        body, name="loss_head", grid=(N // tm,), in_specs=[row, row, _full_spec(g)],
        out_specs=[pl.BlockSpec((1, LANES), lambda i: (0, 0)), row, _full_spec(g)],
        out_shape=[jax.ShapeDtypeStruct((1, LANES), F32), jax.ShapeDtypeStruct((N, D), F32),
                   jax.ShapeDtypeStruct(g.shape, F32)],
        compiler_params=_params(("arbitrary",)))(x, tgt, g)


ATT_SCALE = float((QK_NOPE + QK_ROPE) ** -0.5)


def _attn_block(q, k, kv):
    s = mm_nt(q, k) * ATT_SCALE
    m = lax.stop_gradient(jnp.max(s, axis=-1, keepdims=True))
    e = jnp.exp(s - m)
    return mm(e, kv) * (1.0 / jnp.sum(e, axis=-1, keepdims=True))


def attention_fwd(name, q, k, kv, B, S, tq, out_dtype):
    nq = S // tq
    qspec = pl.BlockSpec((tq, HEAD_PAD), lambda b, h, i: (b * nq + i, h))
    kspec = pl.BlockSpec((S, HEAD_PAD), lambda b, h, i: (b, h))

    def body(q_ref, k_ref, kv_ref, o_ref):
        o_ref[...] = _attn_block(q_ref[...], k_ref[...], kv_ref[...]).astype(o_ref.dtype)

    return pl.pallas_call(
        body, name=name, grid=(B, MLA_HEADS, nq), in_specs=[qspec, kspec, kspec], out_specs=qspec,
        out_shape=jax.ShapeDtypeStruct(q.shape, out_dtype),
        compiler_params=_params(("parallel", "parallel", "arbitrary")))(q, k, kv)


def attention_bwd(name, q, k, kv, do, B, S, tq):
    nq = S // tq
    qspec = pl.BlockSpec((tq, HEAD_PAD), lambda b, h, i: (b * nq + i, h))
    kspec = pl.BlockSpec((S, HEAD_PAD), lambda b, h, i: (b, h))

    def body(q_ref, k_ref, kv_ref, do_ref, dq_ref, dk_ref, dkv_ref):
        _, vjp = jax.vjp(_attn_block, q_ref[...].astype(F32), k_ref[...].astype(F32), kv_ref[...])
        dq, dk, dkv = vjp(do_ref[...])
        dq_ref[...] = dq

        @pl.when(pl.program_id(2) == 0)
        def _():
            dk_ref[...] = jnp.zeros_like(dk_ref)
            dkv_ref[...] = jnp.zeros_like(dkv_ref)

        dk_ref[...] += dk
        dkv_ref[...] += dkv

    sh = jax.ShapeDtypeStruct(q.shape, F32)
    return pl.pallas_call(
        body, name=name, grid=(B, MLA_HEADS, nq), in_specs=[qspec, kspec, kspec, qspec],
        out_specs=[qspec, kspec, kspec], out_shape=[sh, sh, sh],
        compiler_params=_params(("parallel", "parallel", "arbitrary")))(q, k, kv, do)


SCAN_TC = 8
SCAN_UNROLL = 16


def _jloop(n, body, init):
    def outer(o, c):
        for u in range(SCAN_UNROLL):
            c = body(o * SCAN_UNROLL + u, c)
        return c

    return lax.fori_loop(0, n // SCAN_UNROLL, outer, init)


def _dir_mask(L, Ip):
    per_dir = L // (2 * (RWKV_HEAD // Ip))
    lane = lax.broadcasted_iota(jnp.int32, (1, L), 1)
    return (lane // per_dir) % 2 == 1


def _merge_dirs(mask, fwd_ref, rev_ref, out_ref):
    for tt in range(SCAN_TC):
        out_ref[tt] = jnp.where(mask, rev_ref[SCAN_TC - 1 - tt], fwd_ref[tt])


def scan_fwd(name, w, k, b, kk, r, v):
    T, J, L = w.shape
    Ip = v.shape[1]
    nT = T // SCAN_TC
    fwd3, rev3 = (lambda g: (g, 0, 0)), (lambda g: (nT - 1 - g, 0, 0))
    jf, jr = pl.BlockSpec((SCAN_TC, J, L), fwd3), pl.BlockSpec((SCAN_TC, J, L), rev3)
    i_f, i_r = pl.BlockSpec((SCAN_TC, Ip, L), fwd3), pl.BlockSpec((SCAN_TC, Ip, L), rev3)
    sspec = pl.BlockSpec((SCAN_TC, J, Ip, L), lambda g: (g, 0, 0, 0))
    last_spec = pl.BlockSpec((J, Ip, L), lambda g: (0, 0, 0))

    def body(wf, wr, kf, kr, bf, br, kkf, kkr, rf, rr, vf, vr, yf_ref, yr_ref, sp_ref, sa_ref, last_ref,
             s_ref, w_ref, k_ref, b_ref, kk_ref, r_ref, v_ref):
        @pl.when(pl.program_id(0) == 0)
        def _():
            s_ref[...] = jnp.zeros_like(s_ref)

        mask = _dir_mask(L, Ip)
        for f_, r_, m_ in ((wf, wr, w_ref), (kf, kr, k_ref), (bf, br, b_ref), (kkf, kkr, kk_ref), (rf, rr, r_ref),
                           (vf, vr, v_ref)):
            _merge_dirs(mask, f_, r_, m_)

        def row(ref, tt, j):
            return jnp.broadcast_to(ref[tt, pl.ds(j, 1), :], (Ip, L))

        def step(tt, carry):
            def p1(j, sa):
                s = s_ref[j]
                sp_ref[tt, j] = s
                return sa + s * row(kk_ref, tt, j)

            sa = _jloop(J, p1, jnp.zeros((Ip, L), F32))
            sa_ref[tt] = sa
            vt = v_ref[tt]

            def p2(j, y):
                s = s_ref[j] * row(w_ref, tt, j) - sa * row(b_ref, tt, j) + vt * row(k_ref, tt, j)
                s_ref[j] = s
                return y + s * row(r_ref, tt, j)

            y = _jloop(J, p2, jnp.zeros((Ip, L), F32))
            yf_ref[tt] = y
            yr_ref[SCAN_TC - 1 - tt] = y
            return carry

        lax.fori_loop(0, SCAN_TC, step, 0)

        @pl.when(pl.program_id(0) == nT - 1)
        def _():
            last_ref[...] = s_ref[...]

    ish = jax.ShapeDtypeStruct((T, Ip, L), F32)
    jscr = pltpu.VMEM((SCAN_TC, J, L), F32)
    return pl.pallas_call(
        body, name=name, grid=(nT,), in_specs=[jf, jr] * 5 + [i_f, i_r],
        out_specs=[i_f, i_r, sspec, i_f, last_spec],
        out_shape=[ish, ish, jax.ShapeDtypeStruct((T, J, Ip, L), F32), ish, jax.ShapeDtypeStruct((J, Ip, L), F32)],
        scratch_shapes=[pltpu.VMEM((J, Ip, L), F32)] + [jscr] * 5 + [pltpu.VMEM((SCAN_TC, Ip, L), F32)],
        compiler_params=_params(("arbitrary",)))(w, w, k, k, b, b, kk, kk, r, r, v, v)


def scan_bwd(name, w, k, b, kk, r, v, sp, sa_all, s_last, dy):
    T, J, L = w.shape
    Ip = v.shape[1]
    nT = T // SCAN_TC
    stp3, mir3 = (lambda g: (nT - 1 - g, 0, 0)), (lambda g: (g, 0, 0))
    jf, jr = pl.BlockSpec((SCAN_TC, J, L), stp3), pl.BlockSpec((SCAN_TC, J, L), mir3)
    i_f, i_r = pl.BlockSpec((SCAN_TC, Ip, L), stp3), pl.BlockSpec((SCAN_TC, Ip, L), mir3)
    sspec = pl.BlockSpec((SCAN_TC, J, Ip, L), lambda g: (nT - 1 - g, 0, 0, 0))
    last_spec = pl.BlockSpec((J, Ip, L), lambda g: (0, 0, 0))

    def body(wf, wr, kf, kr, bf, br, kkf, kkr, rf, rr, vf, vr, dyf, dyr, sp_ref, sa_ref, last_ref,
             dwf, dwr, dkf, dkr, dbf, dbr, dkkf, dkkr, drf, drr, dvf, dvr,
             ds_ref, nxt_ref, w_ref, k_ref, b_ref, kk_ref, r_ref, v_ref, dy_ref):
        @pl.when(pl.program_id(0) == 0)
        def _():
            ds_ref[...] = jnp.zeros_like(ds_ref)
            nxt_ref[...] = last_ref[...]

        mask = _dir_mask(L, Ip)
        for f_, r_, m_ in ((wf, wr, w_ref), (kf, kr, k_ref), (bf, br, b_ref), (kkf, kkr, kk_ref), (rf, rr, r_ref),
                           (vf, vr, v_ref), (dyf, dyr, dy_ref)):
            _merge_dirs(mask, f_, r_, m_)

        def row(ref, tt, j):
            return jnp.broadcast_to(ref[tt, pl.ds(j, 1), :], (Ip, L))

        def rsum(x):
            return jnp.sum(x, axis=0, keepdims=True)

        def make_step(first):
            def step(n, carry):
                tt = SCAN_TC - 1 - n
                dyt, vt, sa = dy_ref[tt], v_ref[tt], sa_ref[tt]

                def p1(j, c):
                    dsa, dv = c
                    ds = ds_ref[j] + dyt * row(r_ref, tt, j)
                    ds_ref[j] = ds
                    return dsa - ds * row(b_ref, tt, j), dv + ds * row(k_ref, tt, j)

                z = jnp.zeros((Ip, L), F32)
                dsa, dv = _jloop(J, p1, (z, z))
                dvf[tt] = dv
                dvr[SCAN_TC - 1 - tt] = dv

                def put(f_ref, r_ref_, j, val):
                    f_ref[tt, pl.ds(j, 1), :] = val
                    r_ref_[SCAN_TC - 1 - tt, pl.ds(j, 1), :] = val

                def p2(j, c):
                    ds = ds_ref[j]
                    s0 = sp_ref[tt, j]
                    s1 = nxt_ref[j] if first else sp_ref[tt + 1, j]
                    put(drf, drr, j, rsum(s1 * dyt))
                    put(dkf, dkr, j, rsum(ds * vt))
                    put(dbf, dbr, j, -rsum(ds * sa))
                    put(dwf, dwr, j, rsum(ds * s0))
                    put(dkkf, dkkr, j, rsum(s0 * dsa))
                    ds_ref[j] = ds * row(w_ref, tt, j) + dsa * row(kk_ref, tt, j)
                    return c

                _jloop(J, p2, 0)
                return carry

            return step

        make_step(True)(0, 0)
        lax.fori_loop(1, SCAN_TC, make_step(False), 0)
        nxt_ref[...] = sp_ref[0]

    jsh = jax.ShapeDtypeStruct((T, J, L), F32)
    ish = jax.ShapeDtypeStruct((T, Ip, L), F32)
    jscr = pltpu.VMEM((SCAN_TC, J, L), F32)
    iscr = pltpu.VMEM((SCAN_TC, Ip, L), F32)
    return pl.pallas_call(
        body, name=name, grid=(nT,), in_specs=[jf, jr] * 5 + [i_f, i_r] * 2 + [sspec, i_f, last_spec],
        out_specs=[jf, jr] * 5 + [i_f, i_r], out_shape=[jsh] * 10 + [ish] * 2,
        scratch_shapes=[pltpu.VMEM((J, Ip, L), F32)] * 2 + [jscr] * 5 + [iscr] * 2,
        compiler_params=_params(("arbitrary",)))(w, w, k, k, b, b, kk, kk, r, r, v, v, dy, dy, sp, sa_all, s_last)


LAYOUT_TT = 128
LAYOUT_U = 8


def _lane_group(L, n):
    return lax.broadcasted_iota(jnp.int32, (1, L), 1) // (L // n)


def to_scan(name, x, B, S, nd, Ip):
    isplit = LANES // (2 * B * RWKV_HEADS)
    tt_n = _pick(S, (LAYOUT_TT, 64, 16, 8))
    x5 = x.reshape(B, S, nd, RWKV_HEADS, RWKV_HEAD)
    rows_out = RWKV_HEAD if Ip is None else Ip

    def body(x_ref, o_ref):
        group = _lane_group(LANES, isplit)

        def chunk(c, carry):
            t0 = c * LAYOUT_U
            z = jnp.concatenate([x_ref[b, t0 + u, min(d, nd - 1)] for u in range(LAYOUT_U) for _ in range(isplit)
                                 for d in range(2) for b in range(B)], axis=0)
            mt = z.T
            for u in range(LAYOUT_U):
                m = mt[:, u * LANES:(u + 1) * LANES]
                if Ip is not None:
                    m = sum(jnp.where(group == i2, m[i2 * Ip:(i2 + 1) * Ip], 0.0) for i2 in range(isplit))
                o_ref[t0 + u] = m
            return carry

        lax.fori_loop(0, tt_n // LAYOUT_U, chunk, 0)

    return pl.pallas_call(
        body, name=name, grid=(S // tt_n,),
        in_specs=[pl.BlockSpec((B, tt_n, nd, RWKV_HEADS, RWKV_HEAD), lambda g: (0, g, 0, 0, 0))],
        out_specs=pl.BlockSpec((tt_n, rows_out, LANES), lambda g: (g, 0, 0)),
        out_shape=jax.ShapeDtypeStruct((S, rows_out, LANES), F32), compiler_params=_params(("parallel",)))(x5)


def from_scan(name, f, r, B, S, nd, i_indexed):
    rows_in = f.shape[1]
    isplit = LANES // (2 * B * RWKV_HEADS)
    Ip = rows_in if i_indexed else RWKV_HEAD // isplit
    tt_n = _pick(S, (LAYOUT_TT, 64, 16, 8))
    per_i2 = LANES // isplit

    def body(f_ref, r_ref, o_ref):
        mask = _dir_mask(LANES, Ip)
        group = _lane_group(LANES, isplit)

        def chunk(c, carry):
            t0 = c * LAYOUT_U
            ms = []
            for u in range(LAYOUT_U):
                m = jnp.where(mask, r_ref[t0 + u], f_ref[t0 + u])
                if i_indexed:
                    m = jnp.concatenate([jnp.where(group == i2, m, 0.0) for i2 in range(isplit)], axis=0)
                ms.append(m)
            zt = jnp.concatenate(ms, axis=1).T
            for u in range(LAYOUT_U):
                z = zt[u * LANES:(u + 1) * LANES]
                zf = sum(z[i2 * per_i2:(i2 + 1) * per_i2] for i2 in range(isplit))
                for b in range(B):
                    d0 = zf[b * RWKV_HEADS:(b + 1) * RWKV_HEADS]
                    d1 = zf[(B + b) * RWKV_HEADS:(B + b + 1) * RWKV_HEADS]
                    if nd == 1:
                        o_ref[b, t0 + u, 0] = d0 + d1
                    else:
                        o_ref[b, t0 + u, 0] = d0
                        o_ref[b, t0 + u, 1] = d1
            return carry

        lax.fori_loop(0, tt_n // LAYOUT_U, chunk, 0)

    spec = pl.BlockSpec((tt_n, rows_in, LANES), lambda g: (g, 0, 0))
    out = pl.pallas_call(
        body, name=name, grid=(S // tt_n,), in_specs=[spec, spec],
        out_specs=pl.BlockSpec((B, tt_n, nd, RWKV_HEADS, RWKV_HEAD), lambda g: (0, g, 0, 0, 0)),
        out_shape=jax.ShapeDtypeStruct((B, S, nd, RWKV_HEADS, RWKV_HEAD), F32),
        compiler_params=_params(("parallel",)))(f, r)
    return out.reshape(B * S, nd * RWKV_DIM)


def adamw(name, w, g, m, v):
    R, C = w.shape
    tr = _pick(R, (256, 128, 64, 32, 16, 8))
    c1 = 1.0 - ADAM_B1 ** ADAM_STEP
    c2 = 1.0 - ADAM_B2 ** ADAM_STEP

    def body(w_ref, g_ref, m_ref, v_ref, d_ref, nm_ref, nv_ref):
        gv = g_ref[...]
        nm = ADAM_B1 * m_ref[...] + (1.0 - ADAM_B1) * gv
        nv = ADAM_B2 * v_ref[...] + (1.0 - ADAM_B2) * jnp.square(gv)
        d_ref[...] = -ADAM_LR * ((nm / c1) / (jnp.sqrt(nv / c2) + ADAM_EPS) + ADAM_WD * w_ref[...])
        nm_ref[...] = nm
        nv_ref[...] = nv

    spec = pl.BlockSpec((tr, C), lambda i: (i, 0))
    sh = jax.ShapeDtypeStruct((R, C), F32)
    return pl.pallas_call(body, name=name, grid=(R // tr,), in_specs=[spec] * 4, out_specs=[spec] * 3,
                          out_shape=[sh] * 3, compiler_params=_params(("parallel",)))(w, g, m, v)


def sum_slots(name, x):
    n, R, C = x.shape
    tr = _pick(R, (256, 128, 64, 32, 16, 8))

    def body(x_ref, o_ref):
        acc = x_ref[0].astype(F32)
        for s in range(1, n):
            acc = acc + x_ref[s].astype(F32)
        o_ref[...] = acc

    return pl.pallas_call(
        body, name=name, grid=(R // tr,), in_specs=[pl.BlockSpec((n, tr, C), lambda i: (0, i, 0))],
        out_specs=pl.BlockSpec((tr, C), lambda i: (i, 0)), out_shape=jax.ShapeDtypeStruct((R, C), F32),
        compiler_params=_params(("parallel",)))(x)


ANY = pl.BlockSpec(memory_space=pl.ANY)


def _xyc():
    return lax.axis_index("x"), lax.axis_index("y"), lax.axis_index("c")


def gather_shards(shard):
    _, R, C = shard.shape

    def body(x_ref, out_ref, send_sems, recv_sems, local_sem):
        x, y, c = _xyc()
        me, sibling = (x, y, c), (x, y, 1 - c)
        chips = [(1 - x, y), (x, 1 - y), (1 - x, 1 - y)]

        def cp(k, cx, cy, half, to, src=None):
            dst = out_ref.at[2 * cx + cy, half]
            return pltpu.make_async_remote_copy(
                src_ref=dst if src is None else src, dst_ref=dst, send_sem=send_sems.at[k],
                recv_sem=recv_sems.at[k], device_id=to, device_id_type=MESH)

        mine = pltpu.make_async_copy(x_ref, out_ref.at[2 * x + y], local_sem)
        mine.start()
        first = [cp(j, x, y, c, (*chip, c), src=x_ref.at[c]) for j, chip in enumerate(chips)]
        for f in first:
            f.start()
        passed = [cp(3 + j, *chip, c, sibling) for j, chip in enumerate(chips)]
        for j, chip in enumerate(chips):
            cp(j, *chip, c, me).wait_recv()
            passed[j].start()
        for j, chip in enumerate(chips):
            cp(3 + j, *chip, 1 - c, me).wait_recv()
        for f in first + passed:
            f.wait_send()
        mine.wait()

    return pl.pallas_call(
        body, name="gather_shards", in_specs=[ANY], out_specs=ANY,
        out_shape=jax.ShapeDtypeStruct((4, 2, R, C), shard.dtype),
        scratch_shapes=[pltpu.SemaphoreType.DMA((6,)), pltpu.SemaphoreType.DMA((6,)), pltpu.SemaphoreType.DMA])(shard)


FLIPS = [(0, 0, 1), (0, 1, 0), (0, 1, 1), (1, 0, 0), (1, 0, 1), (1, 1, 0), (1, 1, 1)]


def scatter_partials(g):
    _, _, R, C = g.shape

    def body(g_ref, out_ref, send_sems, recv_sems, local_sem):
        x, y, c = _xyc()
        me_idx = 4 * x + 2 * y + c
        mine = pltpu.make_async_copy(g_ref.at[2 * x + y, c], out_ref.at[me_idx], local_sem)
        mine.start()
        sends = []
        for k, (fx, fy, fc) in enumerate(FLIPS):
            px, py, pc = (x + fx) % 2, (y + fy) % 2, (c + fc) % 2
            s = pltpu.make_async_remote_copy(
                src_ref=g_ref.at[2 * px + py, pc], dst_ref=out_ref.at[me_idx], send_sem=send_sems.at[k],
                recv_sem=recv_sems.at[k], device_id=(px, py, pc), device_id_type=MESH)
            s.start()
            sends.append(s)
        for k, (fx, fy, fc) in enumerate(FLIPS):
            px, py, pc = (x + fx) % 2, (y + fy) % 2, (c + fc) % 2
            slot = out_ref.at[4 * px + 2 * py + pc]
            pltpu.make_async_remote_copy(
                src_ref=slot, dst_ref=slot, send_sem=send_sems.at[k], recv_sem=recv_sems.at[k],
                device_id=(px, py, pc), device_id_type=MESH).wait_recv()
        for s in sends:
            s.wait_send()
        mine.wait()

    return pl.pallas_call(
        body, name="scatter_partials", in_specs=[ANY], out_specs=ANY,
        out_shape=jax.ShapeDtypeStruct((8, R, C), g.dtype),
        scratch_shapes=[pltpu.SemaphoreType.DMA((7,)), pltpu.SemaphoreType.DMA((7,)), pltpu.SemaphoreType.DMA])(g)


JOIN_ROWS = 256


def sum_join(x):
    n, R, C = x.shape
    steps = R // JOIN_ROWS

    def body(x_ref, out_ref, acc, local_sems, send_sems, recv_sem):
        i = pl.program_id(0)
        slot = lax.rem(i, 2)
        xx, yy, c = _xyc()
        sibling = (xx, yy, 1 - c)

        def copies(k, s):
            dst = out_ref.at[c, pl.ds(k * JOIN_ROWS, JOIN_ROWS)]
            return (pltpu.make_async_copy(acc.at[s], dst, local_sems.at[s]),
                    pltpu.make_async_remote_copy(src_ref=acc.at[s], dst_ref=dst, send_sem=send_sems.at[s],
                                                 recv_sem=recv_sem, device_id=sibling, device_id_type=MESH))

        def drain(k, s):
            loc, rem = copies(k, s)
            loc.wait()
            rem.wait_send()

        @pl.when(i >= 2)
        def _():
            drain(i - 2, slot)

        a = x_ref[0].astype(F32)
        for s in range(1, n):
            a = a + x_ref[s].astype(F32)
        acc[slot] = a
        loc, rem = copies(i, slot)
        loc.start()
        rem.start()

        @pl.when(i == steps - 1)
        def _():
            if steps >= 2:
                drain(i - 1, 1 - slot)
            drain(i, slot)
            theirs = out_ref.at[1 - c]
            pltpu.make_async_remote_copy(src_ref=theirs, dst_ref=theirs, send_sem=send_sems.at[0], recv_sem=recv_sem,
                                         device_id=sibling, device_id_type=MESH).wait_recv()

    return pl.pallas_call(
        body, name="sum_join", grid=(steps,),
        in_specs=[pl.BlockSpec((n, JOIN_ROWS, C), lambda i: (0, i, 0))], out_specs=ANY,
        out_shape=jax.ShapeDtypeStruct((2, R, C), F32),
        scratch_shapes=[pltpu.VMEM((2, JOIN_ROWS, C), F32), pltpu.SemaphoreType.DMA((2,)),
                        pltpu.SemaphoreType.DMA((2,)), pltpu.SemaphoreType.DMA],
        compiler_params=_params(("arbitrary",)))(x)


def gather_all(name, block):
    R, C = block.shape

    def body(x_ref, out_ref, send_sems, recv_sems, local_sem):
        x, y, c = _xyc()
        mine = pltpu.make_async_copy(x_ref, out_ref.at[4 * x + 2 * y + c], local_sem)
        mine.start()
        sends = []
        for k, (fx, fy, fc) in enumerate(FLIPS):
            px, py, pc = (x + fx) % 2, (y + fy) % 2, (c + fc) % 2
            s = pltpu.make_async_remote_copy(
                src_ref=x_ref, dst_ref=out_ref.at[4 * x + 2 * y + c], send_sem=send_sems.at[k],
                recv_sem=recv_sems.at[k], device_id=(px, py, pc), device_id_type=MESH)
            s.start()
            sends.append(s)
        for k, (fx, fy, fc) in enumerate(FLIPS):
            px, py, pc = (x + fx) % 2, (y + fy) % 2, (c + fc) % 2
            slot = out_ref.at[4 * px + 2 * py + pc]
            pltpu.make_async_remote_copy(
                src_ref=slot, dst_ref=slot, send_sem=send_sems.at[k], recv_sem=recv_sems.at[k],
                device_id=(px, py, pc), device_id_type=MESH).wait_recv()
        for s in sends:
            s.wait_send()
        mine.wait()

    return pl.pallas_call(
        body, name=name, in_specs=[ANY], out_specs=ANY,
        out_shape=jax.ShapeDtypeStruct((8, R, C), block.dtype),
        scratch_shapes=[pltpu.SemaphoreType.DMA((7,)), pltpu.SemaphoreType.DMA((7,)), pltpu.SemaphoreType.DMA])(block)


PACK_C = 1024


def _pack(arrs, row_mult):
    flat = jnp.concatenate([a.reshape(-1) for a in arrs])
    n = flat.shape[0]
    rows = -(-n // PACK_C)
    rows = -(-rows // row_mult) * row_mult
    return jnp.pad(flat, (0, rows * PACK_C - n)).reshape(rows, PACK_C)


def _unpack(buf, shapes):
    flat = buf.reshape(-1)
    out, off = [], 0
    for s in shapes:
        n = int(np.prod(s))
        out.append(flat[off:off + n].reshape(s))
        off += n
    return out


OFF_Q, OFF_CKV, OFF_KR, OFF_SG, OFF_RW, OFF_GATE, N_IN_PAD = 0, 384, 640, 896, 1920, 3840, 6912
IN_SEGMENTS = [('q', OFF_Q, OFF_CKV), ('ckv', OFF_CKV, OFF_KR), ('kr', OFF_KR, OFF_SG), ('sg', OFF_SG, OFF_RW),
               ('rw', OFF_RW, OFF_GATE), ('gate', OFF_GATE, N_IN_PAD)]
ROPE_LANE = QK_NOPE
HALF = QK_ROPE // 2


def _win_layout():
    src = np.full((N_IN_PAD,), -1, np.int64)
    sgn = np.ones((N_IN_PAD,), np.float32)
    src[0:640] = np.arange(0, 640)
    kr0 = Q_LORA + KV_LORA
    src[OFF_KR + ROPE_LANE:OFF_KR + ROPE_LANE + QK_ROPE] = kr0 + np.arange(QK_ROPE)
    sw = OFF_KR + HEAD_PAD + ROPE_LANE
    src[sw:sw + HALF] = kr0 + HALF + np.arange(HALF)
    sgn[sw:sw + HALF] = -1.0
    src[sw + HALF:sw + QK_ROPE] = kr0 + np.arange(HALF)
    src[OFF_SG:N_IN_PAD] = 672 + np.arange(N_IN_PAD - OFF_SG)
    return src, sgn


def _wuq_layout():
    hw = MLA_HEADS * HEAD_PAD
    src = np.full((2 * hw,), -1, np.int64)
    sgn = np.ones((2 * hw,), np.float32)
    per = QK_NOPE + QK_ROPE
    for h in range(MLA_HEADS):
        src[h * HEAD_PAD:h * HEAD_PAD + per] = h * per + np.arange(per)
        sw = hw + h * HEAD_PAD + ROPE_LANE
        src[sw:sw + HALF] = h * per + QK_NOPE + HALF + np.arange(HALF)
        sgn[sw:sw + HALF] = -1.0
        src[sw + HALF:sw + QK_ROPE] = h * per + QK_NOPE + np.arange(HALF)
    return src, sgn


def _runs(idx, sgn):
    out, lo = [], 0
    for pos in range(1, len(idx) + 1):
        if pos == len(idx) or not (
                (idx[pos] == -1 and idx[pos - 1] == -1)
                or (idx[pos - 1] >= 0 and idx[pos] == idx[pos - 1] + 1 and sgn[pos] == sgn[pos - 1])):
            out.append((lo, pos, int(idx[lo]), float(sgn[lo])))
            lo = pos
    return out


def _select_cols(w, idx, sgn):
    pieces = []
    for lo, hi, s0, sg in _runs(idx, sgn):
        if s0 < 0:
            pieces.append(jnp.zeros((w.shape[0], hi - lo), w.dtype))
        else:
            piece = w[:, s0:s0 + hi - lo]
            pieces.append(piece if sg > 0 else -piece)
    return jnp.concatenate(pieces, axis=1)


def _permute_cols(w, src, sgn):
    return _select_cols(w, src, sgn)


def _unpermute_full(dw, src, sgn, n_cols):
    first = np.full((n_cols,), -1, np.int64)
    second = np.full((n_cols,), -1, np.int64)
    for pos, s in enumerate(src):
        if s < 0:
            continue
        if first[s] < 0:
            first[s] = pos
        else:
            second[s] = pos
    sg2 = np.where(second >= 0, sgn[np.maximum(second, 0)], 1.0)
    return _select_cols(dw, first, sgn[first]), _select_cols(dw, second, sg2)


def _blockdiag(w):
    z = jnp.zeros_like(w[0])
    return jnp.concatenate([jnp.concatenate([w[0], z], axis=1), jnp.concatenate([z, w[1]], axis=1)], axis=0)


def _rope_tables(pos):
    inv = 1.0 / (ROPE_THETA ** (jnp.arange(0, QK_ROPE, 2, dtype=F32) / QK_ROPE))
    ang = pos.astype(F32)[:, None] * inv[None, :]
    cos, sin = jnp.cos(ang), jnp.sin(ang)
    pad = lambda t, fill: jnp.concatenate(
        [jnp.full((t.shape[0], ROPE_LANE), fill, F32), t, t, jnp.full((t.shape[0], HEAD_PAD - ROPE_LANE - QK_ROPE), fill, F32)], axis=1)
    return pad(cos, 1.0), pad(sin, 0.0)


def kernel(x, positions, attn_norm_g, w_in, gate_b, q_norm_g, w_uq, kv_norm_g, w_ukv, sg_ln_g, sg_ln_b, sg_w, sg_b, rw_mu, rw_w0, rw_w2, rw_a0, rw_a2, rw_g2, rw_k_k, rw_k_a, rw_r_k, rw_ln_g, rw_ln_b, w_branch, w_out, ffn_norm_g, w_ffn_gate, w_ffn_up, w_ffn_down, final_norm_g, loss_target, m_attn_norm_g, m_w_in, m_gate_b, m_q_norm_g, m_w_uq, m_kv_norm_g, m_w_ukv, m_sg_ln_g, m_sg_ln_b, m_sg_w, m_sg_b, m_rw_mu, m_rw_w0, m_rw_w2, m_rw_a0, m_rw_a2, m_rw_g2, m_rw_k_k, m_rw_k_a, m_rw_r_k, m_rw_ln_g, m_rw_ln_b, m_w_branch, m_w_out, m_ffn_norm_g, m_w_ffn_gate, m_w_ffn_up, m_w_ffn_down, m_final_norm_g, v_attn_norm_g, v_w_in, v_gate_b, v_q_norm_g, v_w_uq, v_kv_norm_g, v_w_ukv, v_sg_ln_g, v_sg_ln_b, v_sg_w, v_sg_b, v_rw_mu, v_rw_w0, v_rw_w2, v_rw_a0, v_rw_a2, v_rw_g2, v_rw_k_k, v_rw_k_a, v_rw_r_k, v_rw_ln_g, v_rw_ln_b, v_w_branch, v_w_out, v_ffn_norm_g, v_w_ffn_gate, v_w_ffn_up, v_w_ffn_down, v_final_norm_g):
    args = locals()
    W = {n: args[n] for n in WEIGHTS}
    M1 = {n: args['m_' + n] for n in WEIGHTS}
    M2 = {n: args['v_' + n] for n in WEIGHTS}
    B, S, D = x.shape
    N = B * S
    TM = _pick(N, (256, 128))
    TMH = 128
    TQ = _pick(S, (512, 256, 128))

    shard_shapes = [W[n].shape for n in SHARDED]
    full = {}
    mm_pack = _pack([W[n].astype(BF16) for n in MATMUL_SHARDED], 32)
    Rm = mm_pack.shape[0]
    gathered = gather_shards(mm_pack.reshape(2, Rm // 2, PACK_C)).reshape(4, Rm, PACK_C)
    pieces = [_unpack(gathered[q], [W[n].shape for n in MATMUL_SHARDED]) for q in range(4)]
    for i, n in enumerate(MATMUL_SHARDED):
        full[n] = jnp.concatenate([pieces[q][i] for q in range(4)], axis=SHARD_AXIS[n])
    small = gather_all("gather_small", _pack([W[n] for n in SMALL_SHARDED], 8))
    pieces = [_unpack(small[2 * q], [W[n].shape for n in SMALL_SHARDED]) for q in range(4)]
    for i, n in enumerate(SMALL_SHARDED):
        full[n] = jnp.concatenate([pieces[q][i] for q in range(4)], axis=SHARD_AXIS[n])
    for n in ('rw_w2', 'rw_a2', 'rw_g2'):
        full[n] = full[n].astype(F32)
    for n in REPLICATED:
        full[n] = W[n]

    win_src, win_sgn = _win_layout()
    wuq_src, wuq_sgn = _wuq_layout()
    ones = jnp.asarray(np.kron(np.eye(RWKV_HEADS), np.ones((RWKV_HEAD, RWKV_HEAD))), BF16)
    ct, st = _rope_tables(positions.reshape(N))
    row = lambda v: v.reshape(1, -1)

    H, HD = RWKV_HEADS, RWKV_HEAD
    inst = 2 * B * H
    isplit = LANES // inst
    Ip = HD // isplit
    to_j = lambda nm, c: to_scan(nm, c, B, S, c.shape[1] // RWKV_DIM, None)
    to_i = lambda nm, c: to_scan(nm, c, B, S, 1, Ip)

    def shift_prev(z):
        z = z.reshape(B, S, -1)
        return jnp.pad(z[:, :-1], ((0, 0), (1, 0), (0, 0))).reshape(N, -1)

    def shift_next(z):
        z = z.reshape(B, S, -1)
        return jnp.pad(z[:, 1:], ((0, 0), (0, 1), (0, 0))).reshape(N, -1)

    LW = []
    for l in range(DEPTH):
        wb = full['w_branch'][l]
        wb0 = jnp.zeros((MLA_HEADS, HEAD_PAD, D), F32).at[:, QK_NOPE:].set(wb[0].reshape(MLA_HEADS, V_HEAD, D))
        LW.append(dict(
            attn_g=row(full['attn_norm_g'][l]),
            w_in=_permute_cols(full['w_in'][l], win_src, win_sgn),
            gate_b=row(full['gate_b'][l]),
            q_g=row(full['q_norm_g'][l]),
            w_uq=_permute_cols(full['w_uq'][l], wuq_src, wuq_sgn),
            kv_g=row(full['kv_norm_g'][l]),
            w_ukv=full['w_ukv'][l],
            sg_g=row(full['sg_ln_g'][l]), sg_b=row(full['sg_ln_b'][l]), sg_w=full['sg_w'][l],
            sg_bias=jnp.repeat(full['sg_b'][l].T, SG_DIM // SG_GROUPS, axis=1),
            mu=row(full['rw_mu'][l]), w0=row(full['rw_w0'][l]), w2=_blockdiag(full['rw_w2'][l]),
            a0=row(full['rw_a0'][l]), a2=_blockdiag(full['rw_a2'][l]), g2=full['rw_g2'][l],
            k_k=row(full['rw_k_k'][l]), k_a=row(full['rw_k_a'][l]), r_k=row(full['rw_r_k'][l]),
            ln_g=row(full['rw_ln_g'][l]), ln_b=row(full['rw_ln_b'][l]),
            wb0=wb0.reshape(MLA_HEADS * HEAD_PAD, D), wb1=wb[1], wb2=wb[2],
            w_out=full['w_out'][l], ffn_g=row(full['ffn_norm_g'][l]),
            w_gu=jnp.concatenate([full['w_ffn_gate'][l], full['w_ffn_up'][l]], axis=1),
            w_down=full['w_ffn_down'][l]))

    saved = []
    xc = x.reshape(N, D)
    for l in range(DEPTH):
        p = LW[l]
        t = 'l%d_' % l
        sv = dict(x=xc)
        (h,) = rowwise(t + 'attn_norm', f_rms, [xc], [p['attn_g']], [], [D], TM, [BF16])
        p_q, p_ckv, p_kr, p_sg, z, p_gate = [
            matmul(t + 'in_proj_' + sn, h, p['w_in'][:, a:b], 'nn') for sn, a, b in IN_SEGMENTS]
        sv['h'] = h
        (cq,) = rowwise(t + 'q_norm', f_rms, [p_q], [p['q_g']], [], [Q_LORA], TM, [BF16])
        (ckv,) = rowwise(t + 'kv_norm', f_rms, [p_ckv], [p['kv_g']], [], [KV_LORA], TM, [BF16])
        qq = matmul(t + 'uq', cq, p['w_uq'], 'nn')
        kv = matmul(t + 'ukv', ckv, p['w_ukv'], 'nn')
        qh, kh = rowwise(t + 'rope', f_rope, [qq, kv, p_kr, ct, st], [], [], [MLA_HEADS * HEAD_PAD] * 2, TM,
                         [BF16, BF16])
        ya = attention_fwd(t + 'attn', qh, kh, kv, B, S, TQ, BF16)
        sv.update(p_q=p_q, p_ckv=p_ckv, p_kr=p_kr, cq=cq, ckv=ckv, qq=qq, kv=kv, qh=qh, kh=kh, ya=ya)
        (yb,) = rowwise(t + 'sg', f_sg, [p_sg], [p['sg_g'], p['sg_b'], p['sg_w'], p['sg_bias']], [], [SG_DIM], TM,
                        [BF16])
        sv.update(p_sg=p_sg, yb=yb)
        zp, zn = shift_prev(z), shift_next(z)
        rw_par = [p['mu'], p['w0'], p['w2'], p['a0'], p['a2'], p['g2'], p['k_k'], p['k_a']]
        r_, v_, decay, kdir, kk, bdir, g_ = rowwise(
            t + 'rw_pre', f_rw_pre, [z, zp, zn], rw_par, [ones],
            [RWKV_DIM, RWKV_DIM, 2 * RWKV_DIM, 2 * RWKV_DIM, RWKV_DIM, 2 * RWKV_DIM, RWKV_DIM], TM)
        sc = dict(w=to_j(t + 'lay_w', decay), k=to_j(t + 'lay_k', kdir), b=to_j(t + 'lay_b', bdir),
                  kk=to_j(t + 'lay_kk', kk), r=to_j(t + 'lay_r', r_), v=to_i(t + 'lay_v', v_))
        y_f, y_r, sp, sa_all, s_last = scan_fwd(t + 'scan', sc['w'], sc['k'], sc['b'], sc['kk'], sc['r'], sc['v'])
        ysum = from_scan(t + 'lay_y', y_f, y_r, B, S, 1, True)
        (yc,) = rowwise(t + 'rw_post', f_rw_post, [ysum, r_, v_, kdir, g_], [p['ln_g'], p['ln_b'], p['r_k']],
                        [ones], [RWKV_DIM], TM, [BF16])
        sv.update(z=z, zp=zp, zn=zn, r=r_, v=v_, kdir=kdir, g=g_, sc=sc, sp=sp, sa=sa_all, s_last=s_last,
                  ysum=ysum, yc=yc)
        b0 = matmul(t + 'br0', ya, p['wb0'], 'nn')
        b1 = matmul(t + 'br1', yb, p['wb1'], 'nn')
        b2 = matmul(t + 'br2', yc, p['wb2'], 'nn')
        (merged,) = rowwise(t + 'merge', f_merge, [p_gate, b0, b1, b2], [p['gate_b']], [], [D], TM, [BF16])
        x2 = matmul(t + 'out_proj', merged, p['w_out'], 'nn', add=xc)
        sv.update(p_gate=p_gate, b0=b0, b1=b1, b2=b2, merged=merged, x2=x2)
        (h2,) = rowwise(t + 'ffn_norm', f_rms, [x2], [p['ffn_g']], [], [D], TM, [BF16])
        au = matmul(t + 'ffn_in', h2, p['w_gu'], 'nn')
        (act,) = rowwise(t + 'swiglu', f_swiglu, [au], [], [], [D_FF], TM, [BF16])
        xc = matmul(t + 'ffn_out', act, p['w_down'], 'nn', add=x2)
        sv.update(h2=h2, au=au, act=act)
        saved.append(sv)

    loss_part, dx, d_final_g = loss_head(xc, loss_target.reshape(N, D), row(full['final_norm_g']), TM)
    loss = lax.psum(loss_part[0, 0], ("x", "y", "c"))

    G = {n: [None] * DEPTH for n in WEIGHTS if n != 'final_norm_g'}
    for l in reversed(range(DEPTH)):
        p, sv = LW[l], saved[l]
        t = 'l%d_bwd_' % l
        d_act = matmul(t + 'ffn_out_dx', dx, p['w_down'], 'nt')
        G['w_ffn_down'][l] = matmul(t + 'ffn_out_dw', sv['act'], dx, 'tn')
        (d_au,), _ = rowwise_bwd(t + 'swiglu', f_swiglu, [sv['au']], [], [], [[d_act]], TM, drow_dtypes=[BF16])
        d_h2 = matmul(t + 'ffn_in_dx', d_au, p['w_gu'], 'nt')
        d_wgu = matmul(t + 'ffn_in_dw', sv['h2'], d_au, 'tn')
        G['w_ffn_gate'][l], G['w_ffn_up'][l] = d_wgu[:, :D_FF], d_wgu[:, D_FF:]
        (dx2,), (dg,) = rowwise_bwd(t + 'ffn_norm', f_rms, [sv['x2']], [p['ffn_g']], [], [[d_h2]], TM, extra=[(0, dx)])
        G['ffn_norm_g'][l] = dg.reshape(-1)
        d_merged = matmul(t + 'out_proj_dx', dx2, p['w_out'], 'nt')
        G['w_out'][l] = matmul(t + 'out_proj_dw', sv['merged'], dx2, 'tn')
        (d_pgate, d_b0, d_b1, d_b2), (d_gate_b,) = rowwise_bwd(
            t + 'merge', f_merge, [sv['p_gate'], sv['b0'], sv['b1'], sv['b2']], [p['gate_b']], [], [[d_merged]], TM,
            drow_dtypes=[BF16] * 4)
        G['gate_b'][l] = d_gate_b.reshape(3, D)
        d_ya = matmul(t + 'br0_dx', d_b0, p['wb0'], 'nt')
        d_yb = matmul(t + 'br1_dx', d_b1, p['wb1'], 'nt')
        d_yc = matmul(t + 'br2_dx', d_b2, p['wb2'], 'nt')
        d_wb0 = matmul(t + 'br0_dw', sv['ya'], d_b0, 'tn').reshape(MLA_HEADS, HEAD_PAD, D)[:, QK_NOPE:].reshape(-1, D)
        G['w_branch'][l] = jnp.stack([d_wb0, matmul(t + 'br1_dw', sv['yb'], d_b1, 'tn'),
                                      matmul(t + 'br2_dw', sv['yc'], d_b2, 'tn')])
        (d_y, d_r1, d_v1, d_kdir1, d_g), (d_ln_g, d_ln_b, d_r_k) = rowwise_bwd(
            t + 'rw_post', f_rw_post, [sv['ysum'], sv['r'], sv['v'], sv['kdir'], sv['g']],
            [p['ln_g'], p['ln_b'], p['r_k']], [ones], [[d_yc]], TM)
        G['rw_ln_g'][l], G['rw_ln_b'][l] = d_ln_g.reshape(-1), d_ln_b.reshape(-1)
        G['rw_r_k'][l] = d_r_k.reshape(RWKV_HEADS, RWKV_HEAD)
        sc = sv['sc']
        res = scan_bwd(t + 'scan', sc['w'], sc['k'], sc['b'], sc['kk'], sc['r'], sc['v'], sv['sp'], sv['sa'],
                       sv['s_last'], to_i(t + 'lay_dy', d_y))
        s_dw, s_dk, s_db, s_dkk, s_dr, s_dv = [
            from_scan(t + 'lay_' + nm, res[2 * i], res[2 * i + 1], B, S, nd, nm == 'dv')
            for i, (nm, nd) in enumerate((('dw', 2), ('dk', 2), ('db', 2), ('dkk', 1), ('dr', 1), ('dv', 1)))]
        rw_par = [p['mu'], p['w0'], p['w2'], p['a0'], p['a2'], p['g2'], p['k_k'], p['k_a']]
        d_outs = [[d_r1, s_dr], [d_v1, s_dv], [s_dw], [d_kdir1, s_dk], [s_dkk], [s_db], [d_g]]
        (d_z, d_zp, d_zn), d_rw = rowwise_bwd(
            t + 'rw_pre', f_rw_pre, [sv['z'], sv['zp'], sv['zn']], rw_par, [ones], d_outs, TMH)
        (d_prw,) = rowwise(t + 'shift_sum', f_add3, [d_z, shift_next(d_zp), shift_prev(d_zn)], [], [], [RWKV_IN], TM,
                           [BF16])
        G['rw_mu'][l] = d_rw[0].reshape(-1)
        G['rw_w0'][l] = d_rw[1].reshape(2, RWKV_DIM)
        G['rw_w2'][l] = jnp.stack([d_rw[2][:64, :RWKV_DIM], d_rw[2][64:, RWKV_DIM:]])
        G['rw_a0'][l] = d_rw[3].reshape(2, RWKV_DIM)
        G['rw_a2'][l] = jnp.stack([d_rw[4][:64, :RWKV_DIM], d_rw[4][64:, RWKV_DIM:]])
        G['rw_g2'][l] = d_rw[5]
        G['rw_k_k'][l], G['rw_k_a'][l] = d_rw[6].reshape(-1), d_rw[7].reshape(-1)
        (d_psg,), (d_sg_g, d_sg_b, d_sg_w, d_sg_bias) = rowwise_bwd(
            t + 'sg', f_sg, [sv['p_sg']], [p['sg_g'], p['sg_b'], p['sg_w'], p['sg_bias']], [], [[d_yb]], TMH,
            drow_dtypes=[BF16])
        G['sg_ln_g'][l], G['sg_ln_b'][l], G['sg_w'][l] = d_sg_g.reshape(-1), d_sg_b.reshape(-1), d_sg_w
        G['sg_b'][l] = d_sg_bias.reshape(SG_CHUNK, SG_GROUPS, SG_DIM // SG_GROUPS).sum(-1).T
        d_qh, d_kh, d_kvv = attention_bwd(t + 'attn', sv['qh'], sv['kh'], sv['kv'], d_ya, B, S, TQ)
        (d_qq, d_kv, d_pkr), _ = rowwise_bwd(
            t + 'rope', f_rope, [sv['qq'], sv['kv'], sv['p_kr'], ct, st], [], [], [[d_qh], [d_kh]], TM,
            n_row_diff=3, extra=[(1, d_kvv)], drow_dtypes=[BF16] * 3)
        d_cq = matmul(t + 'uq_dx', d_qq, p['w_uq'], 'nt')
        d_wuq = matmul(t + 'uq_dw', sv['cq'], d_qq, 'tn')
        g1, g2_ = _unpermute_full(d_wuq, wuq_src, wuq_sgn, MLA_HEADS * (QK_NOPE + QK_ROPE))
        G['w_uq'][l] = g1 + g2_
        d_ckv = matmul(t + 'ukv_dx', d_kv, p['w_ukv'], 'nt')
        G['w_ukv'][l] = matmul(t + 'ukv_dw', sv['ckv'], d_kv, 'tn')
        (d_pq,), (dg,) = rowwise_bwd(t + 'q_norm', f_rms, [sv['p_q']], [p['q_g']], [], [[d_cq]], TM,
                                     drow_dtypes=[BF16])
        G['q_norm_g'][l] = dg.reshape(-1)
        (d_pckv,), (dg,) = rowwise_bwd(t + 'kv_norm', f_rms, [sv['p_ckv']], [p['kv_g']], [], [[d_ckv]], TM,
                                       drow_dtypes=[BF16])
        G['kv_norm_g'][l] = dg.reshape(-1)
        d_h, d_cols = None, []
        for (sn, a, b), d_seg in zip(IN_SEGMENTS, [d_pq, d_pckv, d_pkr, d_psg, d_prw, d_pgate]):
            d_h = matmul(t + 'in_proj_dx_' + sn, d_seg, p['w_in'][:, a:b], 'nt', add=d_h)
            d_cols.append(matmul(t + 'in_proj_dw_' + sn, sv['h'], d_seg, 'tn'))
        d_win = jnp.concatenate(d_cols, axis=1)
        g1, g2_ = _unpermute_full(d_win, win_src, win_sgn, N_IN)
        kr0 = Q_LORA + KV_LORA
        G['w_in'][l] = g1.at[:, kr0:kr0 + QK_ROPE].add(g2_[:, kr0:kr0 + QK_ROPE])
        (dx,), (dg,) = rowwise_bwd(t + 'attn_norm', f_rms, [sv['x']], [p['attn_g']], [], [[d_h]], TM, extra=[(0, dx2)])
        G['attn_norm_g'][l] = dg.reshape(-1)

    grads = {n: jnp.stack(G[n]) for n in G}
    grads['final_norm_g'] = d_final_g.reshape(-1)
    grad_x = dx.reshape(B, S, D)

    per_shard = []
    for q in range(4):
        sl = []
        for n in SHARDED:
            ax = SHARD_AXIS[n]
            w = W[n].shape[ax]
            sl.append(lax.slice_in_dim(grads[n], q * w, (q + 1) * w, axis=ax).astype(BF16))
        per_shard.append(_pack(sl, 2 * JOIN_ROWS))
    R = per_shard[0].shape[0]
    gpack = jnp.stack(per_shard).astype(BF16).reshape(4, 2, R // 2, PACK_C)
    both = sum_join(scatter_partials(gpack))
    g_shard = dict(zip(SHARDED, _unpack(both.reshape(R, PACK_C), shard_shapes)))
    rep_shapes = [W[n].shape for n in REPLICATED]
    rpack = _pack([grads[n] for n in REPLICATED], 8)
    g_rep = sum_slots("sum_replicated", gather_all("gather_replicated", rpack))

    outs = {}
    for n in MATMUL_SHARDED:
        shp = W[n].shape
        two = lambda a: a.reshape(-1, shp[-1])
        res = adamw("adamw_" + n, two(W[n]), two(g_shard[n]), two(M1[n]), two(M2[n]))
        outs['grad', n] = g_shard[n]
        for key, a in zip(('delta', 'new_m', 'new_v'), res):
            outs[key, n] = a.reshape(shp)
    small = SMALL_SHARDED + REPLICATED
    small_shapes = [W[n].shape for n in small]
    g_small = [g_shard[n] for n in SMALL_SHARDED] + _unpack(g_rep, rep_shapes)
    res = adamw("adamw_small", _pack([W[n] for n in small], 8), _pack(g_small, 8),
                _pack([M1[n] for n in small], 8), _pack([M2[n] for n in small], 8))
    for n, a in zip(small, g_small):
        outs['grad', n] = a
    for key, buf in zip(('delta', 'new_m', 'new_v'), res):
        for n, a in zip(small, _unpack(buf, small_shapes)):
            outs[key, n] = a
    return (loss, grad_x, *[outs['grad', n] for n in WEIGHTS], *[outs['delta', n] for n in WEIGHTS],
            *[outs['new_m', n] for n in WEIGHTS], *[outs['new_v', n] for n in WEIGHTS])
```
